```python
import math
import jax, jax.numpy as jnp
from jax import lax
import numpy as np

D_MODEL = 1024
BATCH = 8
SEQ = 2048
DEPTH = 1
DEC_BATCH = 8
DEC_SEQ = 16
PAST_LEN = 4096

CHUNK = 64
A_HEADS = 4
A_DK = 128
A_DV = 128
A_WIDTH = A_HEADS * A_DV
CONV_W = 4
GDN_CHUNK = CHUNK
B_HEADS = 4
B_HD = 64
B_WIDTH = B_HEADS * B_HD
IDX_HEADS = 8
IDX_HD = 32
IDX_SCALE = (IDX_HEADS ** -0.5) * (IDX_HD ** -0.5)
TOPK_MAX = 256
Q_BLOCK = 128
N_MEM = 256
M_HEADS = 4
M_HD = 64
M_WIDTH = M_HEADS * M_HD
D_MIX = A_WIDTH + B_WIDTH + M_WIDTH
EPS = 1e-6
IN_SPLITS = (3 * A_WIDTH, A_WIDTH, A_HEADS, A_HEADS,
             B_WIDTH, B_WIDTH, B_WIDTH, B_WIDTH, IDX_HEADS * IDX_HD, IDX_HD, IDX_HEADS,
             M_WIDTH, M_WIDTH)
IN_WIDTH = sum(IN_SPLITS)

kernel_name = 'hybrid_gdn_dsa_mem_stream_step'


def rms_norm(x, g):
    xf = x.astype(jnp.float32)
    y = xf * lax.rsqrt(jnp.mean(xf * xf, axis=-1, keepdims=True) + EPS)
    return (y * g.astype(jnp.float32)).astype(x.dtype)


def l2_norm(x):
    return x * lax.rsqrt(jnp.sum(x * x, axis=-1, keepdims=True) + EPS)


def split_cols(p):
    offs = [int(o) for o in np.cumsum(IN_SPLITS)[:-1]]
    return jnp.split(p, offs, axis=-1)


def causal_conv(buf, x, w):
    xp = jnp.concatenate([buf.astype(x.dtype), x], axis=1)
    t_len = x.shape[1]
    y = xp[:, 0:t_len] * w[0]
    for j in range(1, CONV_W):
        y = y + xp[:, j:j + t_len] * w[j]
    return jax.nn.silu(y), xp[:, -(CONV_W - 1):]


def gated_delta(q, k, v, g, beta, s0, chunk):
    bsz, t_len, nh, dk = q.shape
    dv = v.shape[-1]
    n = t_len // chunk

    def blk(t):
        return jnp.moveaxis(t.reshape(bsz, n, chunk, nh, *t.shape[3:]), 3, 2)

    q, k, v, g, beta = blk(q), blk(k), blk(v), blk(g), blk(beta)
    gc = jnp.cumsum(g, axis=-1)
    tri = jnp.tril(jnp.ones((chunk, chunk), bool))
    strict = tri & ~jnp.eye(chunk, dtype=bool)
    diff = gc[..., :, None] - gc[..., None, :]
    decay = jnp.where(tri, jnp.exp(jnp.where(tri, diff, 0.0)), 0.0)
    kb = k * beta[..., None]
    lmat = jnp.where(strict, jnp.einsum('bnhid,bnhjd->bnhij', kb, k) * decay, 0.0)
    amat = lmat + jnp.eye(chunk, dtype=jnp.float32)
    rhs = jnp.concatenate([v * beta[..., None], kb * jnp.exp(gc)[..., None]], axis=-1)
    sol = lax.linalg.triangular_solve(amat, rhs, left_side=True, lower=True)
    u0, wk = sol[..., :dv], sol[..., dv:]
    qk = jnp.einsum('bnhid,bnhjd->bnhij', q, k) * decay
    q_dec = q * jnp.exp(gc)[..., None]
    k_dec = k * jnp.exp(gc[..., -1:] - gc)[..., None]
    last = jnp.exp(gc[..., -1])

    def step(s, xs):
        qk_c, qd_c, kd_c, u0_c, w_c, last_c = xs
        u = u0_c - jnp.einsum('bhcd,bhde->bhce', w_c, s)
        o = jnp.einsum('bhcd,bhde->bhce', qd_c, s) + jnp.einsum('bhij,bhje->bhie', qk_c, u)
        s = s * last_c[..., None, None] + jnp.einsum('bhcd,bhce->bhde', kd_c, u)
        return s, o

    xs = tuple(jnp.moveaxis(t, 1, 0) for t in (qk, q_dec, k_dec, u0, wk, last))
    s_fin, o = lax.scan(step, s0, xs)
    o = jnp.transpose(o, (1, 0, 3, 2, 4)).reshape(bsz, t_len, nh, dv)
    return o, s_fin


def sparse_block(q, qi, wi, q_pos, k, v, ki, k_pos, n_sel):
    f32 = jnp.float32
    adm = (k_pos[None, :] // CHUNK) <= (q_pos[:, None] // CHUNK)
    logits = jnp.einsum('bqhd,bsd->bqhs', qi.astype(f32), ki.astype(f32))
    score = jnp.einsum('bqh,bqhs->bqs', wi.astype(f32), jax.nn.relu(logits))
    score = jnp.where(adm[None], score, -jnp.inf)
    _, idx = lax.top_k(score, n_sel)
    valid = (k_pos[idx] // CHUNK) <= (q_pos[None, :, None] // CHUNK)
    ks = jax.vmap(lambda a, i: a[i])(k, idx)
    vs = jax.vmap(lambda a, i: a[i])(v, idx)
    s = jnp.einsum('bqhd,bqkhd->bqhk', q.astype(f32), ks.astype(f32)) * (B_HD ** -0.5)
    s = jnp.where(valid[:, :, None, :], s, -jnp.inf)
    p = jax.nn.softmax(s, axis=-1)
    return jnp.einsum('bqhk,bqkhd->bqhd', p, vs.astype(f32))


def sparse_attention(q, qi, wi, q_pos, k, v, ki, k_pos, n_sel):
    bsz, t_len = q.shape[0], q.shape[1]
    if t_len <= Q_BLOCK:
        return sparse_block(q, qi, wi, q_pos, k, v, ki, k_pos, n_sel)
    nb = t_len // Q_BLOCK

    def blocks(t):
        return jnp.moveaxis(t.reshape(bsz, nb, Q_BLOCK, *t.shape[2:]), 1, 0)

    xs = (blocks(q), blocks(qi), blocks(wi), q_pos.reshape(nb, Q_BLOCK))
    o = lax.map(lambda a: sparse_block(a[0], a[1], a[2], a[3], k, v, ki, k_pos, n_sel), xs)
    return jnp.moveaxis(o, 0, 1).reshape(bsz, t_len, *o.shape[3:])


def memory_kv(mem, g_mem, w_mem_kv, g_km):
    bsz = mem.shape[0]
    m = rms_norm(mem, g_mem) @ w_mem_kv
    mk, mv = jnp.split(m, 2, axis=-1)
    mk = rms_norm(mk.reshape(bsz, N_MEM, M_HEADS, M_HD), g_km)
    mv = mv.reshape(bsz, N_MEM, M_HEADS, M_HD)
    return mk, mv


def layer(x, mem_k, mem_v, conv_buf, s0, past_k, past_v, past_ki, gdn_chunk,
          g_in, w_in, conv_w, a_log, dt_bias, g_o, g_qb, g_kb, g_ki, g_qm, w_out):
    f32 = jnp.float32
    bsz, t_len, _ = x.shape
    p_len = past_k.shape[1]
    h = rms_norm(x, g_in)
    (qkv_a, z_a, b_a, a_a, q_b, k_b, v_b, z_b, q_i, k_i, w_i, q_m, z_m) = split_cols(h @ w_in)

    conv_out, conv_new = causal_conv(conv_buf, qkv_a, conv_w)
    q_a, k_a, v_a = jnp.split(conv_out.astype(f32), 3, axis=-1)
    q_a = l2_norm(q_a.reshape(bsz, t_len, A_HEADS, A_DK)) * (A_DK ** -0.5)
    k_a = l2_norm(k_a.reshape(bsz, t_len, A_HEADS, A_DK))
    v_a = v_a.reshape(bsz, t_len, A_HEADS, A_DV)
    beta = jax.nn.sigmoid(b_a.astype(f32))
    g_log = -jnp.exp(a_log.astype(f32)) * jax.nn.softplus(a_a.astype(f32) + dt_bias.astype(f32))
    o_a, s_new = gated_delta(q_a, k_a, v_a, g_log, beta, s0.astype(f32), gdn_chunk)
    o_a = rms_norm(o_a, g_o) * jax.nn.silu(z_a.astype(f32).reshape(bsz, t_len, A_HEADS, A_DV))
    o_a = o_a.reshape(bsz, t_len, A_WIDTH)

    q_b = rms_norm(q_b.reshape(bsz, t_len, B_HEADS, B_HD), g_qb)
    k_b = rms_norm(k_b.reshape(bsz, t_len, B_HEADS, B_HD), g_kb)
    v_b = v_b.reshape(bsz, t_len, B_HEADS, B_HD)
    k_i = rms_norm(k_i, g_ki)
    q_i = q_i.reshape(bsz, t_len, IDX_HEADS, IDX_HD)
    w_i = w_i * IDX_SCALE
    k_all = jnp.concatenate([past_k.astype(k_b.dtype), k_b], axis=1)
    v_all = jnp.concatenate([past_v.astype(v_b.dtype), v_b], axis=1)
    ki_all = jnp.concatenate([past_ki.astype(k_i.dtype), k_i], axis=1)
    q_pos = p_len + jnp.arange(t_len, dtype=jnp.int32)
    k_pos = jnp.arange(p_len + t_len, dtype=jnp.int32)
    n_sel = min(TOPK_MAX, (p_len + t_len) // 4)
    o_b = sparse_attention(q_b, q_i, w_i, q_pos, k_all, v_all, ki_all, k_pos, n_sel)
    o_b = o_b.reshape(bsz, t_len, B_WIDTH) * jax.nn.silu(z_b.astype(f32))

    q_m = rms_norm(q_m.reshape(bsz, t_len, M_HEADS, M_HD), g_qm).astype(f32)
    s_m = jnp.einsum('bqhd,bmhd->bhqm', q_m, mem_k.astype(f32)) * (M_HD ** -0.5)
    o_m = jnp.einsum('bhqm,bmhd->bqhd', jax.nn.softmax(s_m, axis=-1), mem_v.astype(f32))
    o_m = o_m.reshape(bsz, t_len, M_WIDTH) * jax.nn.silu(z_m.astype(f32))

    mix = jnp.concatenate([o_a, o_b, o_m], axis=-1).astype(x.dtype)
    y = x + mix @ w_out
    return y, conv_new, s_new.astype(x.dtype), k_b, v_b, k_i


def setup_inputs(seed: int = 0) -> dict:
    key = jax.random.key(seed)
    ks = jax.random.split(key, 32)
    f32 = jnp.float32

    def nrm(k, shape, scale=1.0):
        return jax.random.normal(k, shape, f32) * scale

    def gain(k, shape):
        return 1.0 + 0.02 * jax.random.normal(k, shape, f32)

    dt = jnp.exp(jax.random.uniform(ks[14], (DEPTH, A_HEADS), f32, math.log(1e-3), math.log(1e-1)))
    return {
        'x_prompt': nrm(ks[0], (BATCH, SEQ, D_MODEL)),
        'x_sample': nrm(ks[1], (DEC_BATCH, DEC_SEQ, D_MODEL)),
        'state_conv_A': nrm(ks[2], (DEPTH, DEC_BATCH, CONV_W - 1, 3 * A_WIDTH)),
        'state_ssm_A': nrm(ks[3], (DEPTH, DEC_BATCH, A_HEADS, A_DK, A_DV), 0.1),
        'cache_k_B': nrm(ks[4], (DEPTH, DEC_BATCH, PAST_LEN, B_HEADS, B_HD)),
        'cache_v_B': nrm(ks[5], (DEPTH, DEC_BATCH, PAST_LEN, B_HEADS, B_HD)),
        'cache_kidx_B': nrm(ks[6], (DEPTH, DEC_BATCH, PAST_LEN, IDX_HD)),
        'cache_mem_k': nrm(ks[7], (DEPTH, DEC_BATCH, N_MEM, M_HEADS, M_HD)),
        'cache_mem_v': nrm(ks[8], (DEPTH, DEC_BATCH, N_MEM, M_HEADS, M_HD)),
        'mem_prompt': nrm(ks[9], (BATCH, N_MEM, D_MODEL)),
        'g_in': gain(ks[10], (DEPTH, D_MODEL)),
        'w_in': nrm(ks[11], (DEPTH, D_MODEL, IN_WIDTH), D_MODEL ** -0.5),
        'conv_w_A': nrm(ks[12], (DEPTH, CONV_W, 3 * A_WIDTH), CONV_W ** -0.5),
        'a_log_A': jnp.log(jax.random.uniform(ks[13], (DEPTH, A_HEADS), f32, 1.0, 16.0)),
        'dt_bias_A': dt + jnp.log(-jnp.expm1(-dt)),
        'g_o_A': gain(ks[15], (DEPTH, A_DV)),
        'g_q_B': gain(ks[16], (DEPTH, B_HD)),
        'g_k_B': gain(ks[17], (DEPTH, B_HD)),
        'g_kidx_B': gain(ks[18], (DEPTH, IDX_HD)),
        'g_mem': gain(ks[19], (DEPTH, D_MODEL)),
        'w_mem_kv': nrm(ks[20], (DEPTH, D_MODEL, 2 * M_WIDTH), D_MODEL ** -0.5),
        'g_q_M': gain(ks[21], (DEPTH, M_HD)),
        'g_k_M': gain(ks[22], (DEPTH, M_HD)),
        'w_out': nrm(ks[23], (DEPTH, D_MIX, D_MODEL), D_MIX ** -0.5),
    }


def reference(x_prompt, x_sample, state_conv_A, state_ssm_A, cache_k_B, cache_v_B, cache_kidx_B,
              cache_mem_k, cache_mem_v, mem_prompt, g_in, w_in, conv_w_A, a_log_A, dt_bias_A, g_o_A,
              g_q_B, g_k_B, g_kidx_B, g_mem, w_mem_kv, g_q_M, g_k_M, w_out):
    bp = x_prompt.shape[0]
    dt_ = x_prompt.dtype
    zero_conv = jnp.zeros((bp, CONV_W - 1, 3 * A_WIDTH), dt_)
    zero_ssm = jnp.zeros((bp, A_HEADS, A_DK, A_DV), jnp.float32)
    empty_k = jnp.zeros((bp, 0, B_HEADS, B_HD), dt_)
    empty_ki = jnp.zeros((bp, 0, IDX_HD), dt_)
    yp, ys = x_prompt, x_sample
    pc, pS, pk, pv, pki, pmk, pmv = [], [], [], [], [], [], []
    sc, sS, sk, sv, ski = [], [], [], [], []
    for l in range(DEPTH):
        wts = (g_in[l], w_in[l], conv_w_A[l], a_log_A[l], dt_bias_A[l], g_o_A[l],
               g_q_B[l], g_k_B[l], g_kidx_B[l], g_q_M[l], w_out[l])
        mk, mv = memory_kv(mem_prompt, g_mem[l], w_mem_kv[l], g_k_M[l])
        yp, c1, s1, k1, v1, ki1 = layer(yp, mk, mv, zero_conv, zero_ssm, empty_k, empty_k, empty_ki,
                                        GDN_CHUNK, *wts)
        ys, c2, s2, k2, v2, ki2 = layer(ys, cache_mem_k[l], cache_mem_v[l], state_conv_A[l], state_ssm_A[l],
                                        cache_k_B[l], cache_v_B[l], cache_kidx_B[l], ys.shape[1], *wts)
        pc.append(c1); pS.append(s1); pk.append(k1); pv.append(v1); pki.append(ki1)
        pmk.append(mk); pmv.append(mv)
        sc.append(c2); sS.append(s2); sk.append(k2); sv.append(v2); ski.append(ki2)
    p_conv_A = jnp.stack(pc)
    p_ssm_A = jnp.stack(pS)
    p_k_B = jnp.stack(pk)
    p_v_B = jnp.stack(pv)
    p_kidx_B = jnp.stack(pki)
    p_mem_k = jnp.stack(pmk)
    p_mem_v = jnp.stack(pmv)
    s_conv_A = jnp.stack(sc)
    s_ssm_A = jnp.stack(sS)
    s_k_B = jnp.stack(sk)
    s_v_B = jnp.stack(sv)
    s_kidx_B = jnp.stack(ski)
    return (yp, ys, p_conv_A, p_ssm_A, p_k_B, p_v_B, p_kidx_B, p_mem_k, p_mem_v,
            s_conv_A, s_ssm_A, s_k_B, s_v_B, s_kidx_B)
```

```python
import functools
import math

import numpy as np
import jax
import jax.numpy as jnp
from jax import lax
from jax.experimental import pallas as pl
from jax.experimental.pallas import tpu as pltpu

F32 = jnp.float32
BF16 = jnp.bfloat16
HIGHEST = lax.Precision.HIGHEST

D_MODEL = 1024
CHUNK = 64
A_HEADS = 4
A_DK = 128
A_DV = 128
A_WIDTH = A_HEADS * A_DV
CONV_W = 4
B_HEADS = 4
B_HD = 64
B_WIDTH = B_HEADS * B_HD
IDX_HEADS = 8
IDX_HD = 32
IDX_SCALE = (IDX_HEADS ** -0.5) * (IDX_HD ** -0.5)
TOPK_MAX = 256
Q_BLOCK = 128
N_MEM = 256
M_HEADS = 4
M_HD = 64
M_WIDTH = M_HEADS * M_HD
EPS = 1e-6
IN_SPLITS = (3 * A_WIDTH, A_WIDTH, A_HEADS, A_HEADS,
             B_WIDTH, B_WIDTH, B_WIDTH, B_WIDTH, IDX_HEADS * IDX_HD, IDX_HD, IDX_HEADS,
             M_WIDTH, M_WIDTH)

LANES = 128
KEY_BLOCK = 256
MISC_W = LANES
MISC_WI = IDX_HD
MISC_B = IDX_HD + IDX_HEADS
MISC_A = MISC_B + A_HEADS
VMEM_LIMIT = 48 * 1024 * 1024
NEG_BIG = -1e30
BISECT_MAX_ITERS = 32


def _cparams(sem):
    return pltpu.CompilerParams(dimension_semantics=sem, vmem_limit_bytes=VMEM_LIMIT)


def _dot(a, b):
    return jnp.dot(a, b, preferred_element_type=F32)


def _dot_nt(a, b, precision=None):
    return lax.dot_general(a, b, (((1,), (1,)), ((), ())), precision=precision,
                           preferred_element_type=F32)


def _silu(x):
    return x * jax.nn.sigmoid(x)


def _lane_mask(width, lo, hi):
    lane = lax.broadcasted_iota(jnp.int32, (1, width), 1)
    return (lane >= lo) & (lane < hi)


def _proj_body(*refs, groups):
    x_ref, g_ref, w_ref = refs[0], refs[1], refs[2]
    n_aux = 3 * sum(1 for _, normed in groups if normed)
    aux = refs[3:3 + n_aux]
    outs = refs[3 + n_aux:]
    x = x_ref[...]
    h = x * lax.rsqrt(jnp.mean(x * x, axis=-1, keepdims=True) + EPS) * g_ref[...]
    hb = h.astype(BF16)
    off = 0
    ai = 0
    for (width, normed), o_ref in zip(groups, outs):
        y = _dot(hb, w_ref[:, off:off + width])
        if normed:
            bd_ref, gain_ref, nm_ref = aux[ai], aux[ai + 1], aux[ai + 2]
            ai += 3
            sq = y * y
            hi = sq.astype(BF16)
            lo = (sq - hi.astype(F32)).astype(BF16)
            ms = _dot(hi, bd_ref[...]) + _dot(lo, bd_ref[...])
            scale = jnp.where(nm_ref[...] > 0.0, lax.rsqrt(ms + EPS), 1.0)
            y = y * scale * gain_ref[...]
        o_ref[...] = y
        off += width


def _group_mean_matrix(width, group, n_lanes):
    i = np.arange(width)
    m = ((i[:, None] // group) == (i[None, :] // group)) & (i[:, None] < n_lanes) & (i[None, :] < n_lanes)
    return jnp.asarray(m.astype(np.float32) / group, dtype=BF16)


def _proj(x, g, w, groups, aux, tm):
    m, d = x.shape
    nw = w.shape[1]
    in_specs = [pl.BlockSpec((tm, d), lambda i: (i, 0)),
                pl.BlockSpec((1, d), lambda i: (0, 0)),
                pl.BlockSpec((d, nw), lambda i: (0, 0))]
    for a in aux:
        in_specs.append(pl.BlockSpec(a.shape, lambda i: (0, 0)))
    out_shape = [jax.ShapeDtypeStruct((m, wd), F32) for wd, _ in groups]
    out_specs = [pl.BlockSpec((tm, wd), lambda i: (i, 0)) for wd, _ in groups]
    return pl.pallas_call(
        functools.partial(_proj_body, groups=tuple(groups)),
        grid=(m // tm,),
        in_specs=in_specs, out_specs=out_specs, out_shape=out_shape,
        compiler_params=_cparams(("parallel",)),
        name="proj",
    )(x, g, w, *aux)


def _gdn_body(qkv_ref, za_ref, misc_ref, cbuf_ref, s0_ref, cw_ref, avec_ref, dtvec_ref, go_ref,
              o_ref, convnew_ref, sfin_ref,
              xp_ref, q_sc, k_sc, v_sc, g_sc, b_sc, s_sc, *, chunk, rows):
    t = pl.program_id(1)
    c = chunk

    @pl.when(t == 0)
    def _():
        xp_ref[0:8, :] = cbuf_ref[0]
        s_sc[...] = s0_ref[0]

    xp_ref[8:8 + rows, :] = qkv_ref[...]
    cw = cw_ref[...]
    y = xp_ref[5:5 + rows, :] * cw[0:1, :]
    for j in range(1, CONV_W):
        y = y + xp_ref[5 + j:5 + j + rows, :] * cw[j:j + 1, :]
    y = _silu(y)
    convnew_ref[0] = xp_ref[rows + 5:rows + 8, :]
    xp_ref[0:8, :] = xp_ref[rows:rows + 8, :]

    for h in range(A_HEADS):
        qh = y[:, h * A_DK:(h + 1) * A_DK]
        kh = y[:, A_WIDTH + h * A_DK:A_WIDTH + (h + 1) * A_DK]
        q_sc[:, h * A_DK:(h + 1) * A_DK] = qh * (
            lax.rsqrt(jnp.sum(qh * qh, axis=-1, keepdims=True) + EPS) * (A_DK ** -0.5))
        k_sc[:, h * A_DK:(h + 1) * A_DK] = kh * lax.rsqrt(jnp.sum(kh * kh, axis=-1, keepdims=True) + EPS)
    v_sc[...] = y[:, 2 * A_WIDTH:3 * A_WIDTH]
    misc = misc_ref[...]
    b_sc[...] = jax.nn.sigmoid(misc)
    g_sc[...] = avec_ref[...] * jax.nn.softplus(misc + dtvec_ref[...])

    ri = lax.broadcasted_iota(jnp.int32, (c, c), 0)
    ci = lax.broadcasted_iota(jnp.int32, (c, c), 1)
    tri = ri >= ci
    strict = ri > ci
    tri_f = tri.astype(F32)
    eye_f = (ri == ci).astype(F32)
    lane_c = lax.broadcasted_iota(jnp.int32, (c, MISC_W), 1)
    go = go_ref[...]

    def chunk_step(ic, carry):
        r0 = pl.multiple_of(ic * c, c)
        rs = pl.ds(r0, c)
        gc_all = jnp.dot(tri_f, g_sc[rs, :], precision=HIGHEST, preferred_element_type=F32)
        b_all = b_sc[rs, :]
        for h in range(A_HEADS):
            hs = slice(h * A_DK, (h + 1) * A_DK)
            q = q_sc[rs, hs]
            k = k_sc[rs, hs]
            v = v_sc[rs, hs]
            beta = b_all[:, MISC_B + h:MISC_B + h + 1]
            gc = gc_all[:, MISC_A + h:MISC_A + h + 1]
            sel = (lane_c == MISC_A + h).astype(F32)
            gc_row = _dot_nt(sel, gc_all, precision=HIGHEST)
            decay = jnp.where(tri, jnp.exp(jnp.where(tri, gc - gc_row, 0.0)), 0.0)
            egc = jnp.exp(gc)
            kb = k * beta
            kbb = kb.astype(BF16)
            kbf = k.astype(BF16)
            lmat = jnp.where(strict, _dot_nt(kbb, kbf) * decay, 0.0)
            x_inv = eye_f - lmat
            p = lmat
            for _ in range(int(math.log2(c)) - 1):
                p = jnp.dot(p, p, precision=HIGHEST, preferred_element_type=F32)
                x_inv = x_inv + jnp.dot(x_inv, p, precision=HIGHEST, preferred_element_type=F32)
            rhs = jnp.concatenate([v * beta, kb * egc], axis=-1).astype(BF16)
            sol = _dot(x_inv.astype(BF16), rhs)
            u0 = sol[:, :A_DV]
            wk = sol[:, A_DV:]
            qk = _dot_nt(q.astype(BF16), kbf) * decay
            q_dec = q * egc
            g_last = gc_all[c - 1:c, MISC_A + h:MISC_A + h + 1]
            k_dec = k * jnp.exp(g_last - gc)
            s = s_sc[h]
            sb = s.astype(BF16)
            u = u0 - _dot(wk.astype(BF16), sb)
            ub = u.astype(BF16)
            o = _dot(q_dec.astype(BF16), sb) + _dot(qk.astype(BF16), ub)
            s_sc[h] = s * jnp.exp(g_last) + lax.dot_general(
                k_dec.astype(BF16), ub, (((0,), (0,)), ((), ())), preferred_element_type=F32)
            on = o * lax.rsqrt(jnp.mean(o * o, axis=-1, keepdims=True) + EPS) * go
            o_ref[rs, hs] = on * _silu(za_ref[rs, hs])
        return carry

    lax.fori_loop(0, rows // c, chunk_step, 0)
    sfin_ref[0] = s_sc[...]


def _gdn(qkv, za, misc, cbuf8, s0, conv_w, avec, dtvec, g_o, *, bsz, t_len, chunk, rows):
    nt = t_len // rows
    w3 = 3 * A_WIDTH
    row_map = lambda b, t: (b * nt + t, 0)
    const2 = lambda b, t: (0, 0)
    return pl.pallas_call(
        functools.partial(_gdn_body, chunk=chunk, rows=rows),
        grid=(bsz, nt),
        in_specs=[pl.BlockSpec((rows, w3), row_map),
                  pl.BlockSpec((rows, A_WIDTH), row_map),
                  pl.BlockSpec((rows, MISC_W), row_map),
                  pl.BlockSpec((1, 8, w3), lambda b, t: (b, 0, 0)),
                  pl.BlockSpec((1, A_HEADS, A_DK, A_DV), lambda b, t: (b, 0, 0, 0)),
                  pl.BlockSpec((CONV_W, w3), const2),
                  pl.BlockSpec((1, MISC_W), const2),
                  pl.BlockSpec((1, MISC_W), const2),
                  pl.BlockSpec((1, A_DV), const2)],
        out_specs=[pl.BlockSpec((rows, A_WIDTH), row_map),
                   pl.BlockSpec((1, CONV_W - 1, w3), lambda b, t: (b, 0, 0)),
                   pl.BlockSpec((1, A_HEADS, A_DK, A_DV), lambda b, t: (b, 0, 0, 0))],
        out_shape=[jax.ShapeDtypeStruct((bsz * t_len, A_WIDTH), F32),
                   jax.ShapeDtypeStruct((bsz, CONV_W - 1, w3), F32),
                   jax.ShapeDtypeStruct((bsz, A_HEADS, A_DK, A_DV), F32)],
        scratch_shapes=[pltpu.VMEM((rows + 8, w3), F32),
                        pltpu.VMEM((rows, A_WIDTH), F32),
                        pltpu.VMEM((rows, A_WIDTH), F32),
                        pltpu.VMEM((rows, A_WIDTH), F32),
                        pltpu.VMEM((rows, MISC_W), F32),
                        pltpu.VMEM((rows, MISC_W), F32),
                        pltpu.VMEM((A_HEADS, A_DK, A_DV), F32)],
        compiler_params=_cparams(("parallel", "arbitrary")),
        name="gdn",
    )(qkv, za, misc, cbuf8, s0, conv_w, avec, dtvec, g_o)


def _topk_threshold(sc_ref, nkb, n_rows, n_sel, n_adm):
    kf = float(n_sel)
    shape = (n_rows, KEY_BLOCK)

    def reduce_blocks(fn, init, combine):
        def body(kb, acc):
            return combine(acc, fn(kb, sc_ref[kb]))
        return lax.fori_loop(0, nkb, body, init)

    def count_gt(thr):
        acc = reduce_blocks(lambda kb, s: jnp.where(s > thr, 1.0, 0.0), jnp.zeros(shape, F32), jnp.add)
        return jnp.sum(acc, axis=-1, keepdims=True)

    vmax = jnp.max(reduce_blocks(lambda kb, s: s, jnp.full(shape, -jnp.inf, F32), jnp.maximum),
                   axis=-1, keepdims=True)
    vmin = jnp.min(reduce_blocks(lambda kb, s: jnp.where(s == -jnp.inf, jnp.inf, s),
                                 jnp.full(shape, jnp.inf, F32), jnp.minimum),
                   axis=-1, keepdims=True)
    lo0 = vmin - jnp.maximum(1.0, jnp.abs(vmin))
    hi0 = vmax
    cnt_lo0 = n_adm
    cnt_hi0 = jnp.zeros_like(n_adm)

    def cond(st):
        it, go = st[0], st[1]
        return (it < BISECT_MAX_ITERS) & (go > 0)

    def body(st):
        it, _, lo, hi, cnt_lo, cnt_hi = st
        mid = 0.5 * lo + 0.5 * hi
        cm = count_gt(mid)
        up = cm >= kf
        lo = jnp.where(up, mid, lo)
        cnt_lo = jnp.where(up, cm, cnt_lo)
        hi = jnp.where(up, hi, mid)
        cnt_hi = jnp.where(up, cnt_hi, cm)
        go = (jnp.max(cnt_lo) > kf).astype(jnp.int32)
        return it + 1, go, lo, hi, cnt_lo, cnt_hi

    go0 = (jnp.max(cnt_lo0) > kf).astype(jnp.int32)
    _, go, lo, hi, cnt_lo, cnt_hi = lax.while_loop(
        cond, body, (jnp.int32(0), go0, lo0, hi0, cnt_lo0, cnt_hi0))

    fix = cnt_lo > kf
    need0 = jnp.where(fix, kf - cnt_hi, 0.0)
    lane = lax.broadcasted_iota(jnp.int32, shape, 1)
    big_idx = jnp.int32(2 ** 30)

    def fix_cond(need):
        return jnp.max(need) > 0.0

    def fix_body(need):
        def in_cluster(s):
            return (s > lo) & (s <= hi)
        mval = jnp.max(reduce_blocks(lambda kb, s: jnp.where(in_cluster(s), s, -jnp.inf),
                                     jnp.full(shape, -jnp.inf, F32), jnp.maximum),
                       axis=-1, keepdims=True)
        idx = jnp.min(reduce_blocks(
            lambda kb, s: jnp.where(in_cluster(s) & (s == mval), lane + kb * KEY_BLOCK, big_idx),
            jnp.full(shape, big_idx, jnp.int32), jnp.minimum), axis=-1, keepdims=True)
        active = need > 0.0

        def promote(kb, carry):
            s = sc_ref[kb]
            sc_ref[kb] = jnp.where(active & ((lane + kb * KEY_BLOCK) == idx), jnp.inf, s)
            return carry
        lax.fori_loop(0, nkb, promote, 0)
        return jnp.where(active, need - 1.0, need)

    lax.while_loop(fix_cond, fix_body, need0)
    return jnp.where(fix, hi, lo)


def _dsa_body(qb_ref, qi_ref, misc_ref, zb_ref, k_ref, v_ref, ki8_ref, o_ref, sc_ref, *, n_sel):
    j = pl.program_id(1)
    tq = Q_BLOCK
    nkb = (j + 2) // 2
    qi = qi_ref[...]
    misc = misc_ref[...]
    row = lax.broadcasted_iota(jnp.int32, (tq, 1), 0)
    qchunk = 2 * j + (row >= CHUNK).astype(jnp.int32)
    lane = lax.broadcasted_iota(jnp.int32, (1, KEY_BLOCK), 1)

    qi_h = [jnp.where(_lane_mask(IDX_HEADS * IDX_HD, h * IDX_HD, (h + 1) * IDX_HD), qi, 0.0).astype(BF16)
            for h in range(IDX_HEADS)]
    w_h = [misc[:, MISC_WI + h:MISC_WI + h + 1] for h in range(IDX_HEADS)]

    def score_blk(kb, carry):
        k0 = pl.multiple_of(kb * KEY_BLOCK, KEY_BLOCK)
        kblk = ki8_ref[pl.ds(k0, KEY_BLOCK), :].astype(BF16)
        acc = jnp.zeros((tq, KEY_BLOCK), F32)
        for h in range(IDX_HEADS):
            acc = acc + w_h[h] * jnp.maximum(_dot_nt(qi_h[h], kblk), 0.0)
        adm = ((k0 + lane) >> 6) <= qchunk
        sc_ref[kb] = jnp.where(adm, acc, -jnp.inf)
        return carry

    lax.fori_loop(0, nkb, score_blk, 0)

    n_adm = ((qchunk + 1) * CHUNK).astype(F32)
    thr = _topk_threshold(sc_ref, nkb, tq, n_sel, n_adm)

    qb = qb_ref[...] * (B_HD ** -0.5)
    masks = [_lane_mask(B_WIDTH, h * B_HD, (h + 1) * B_HD) for h in range(B_HEADS)]
    qb_h = [jnp.where(masks[h], qb, 0.0).astype(BF16) for h in range(B_HEADS)]

    def att_blk(kb, carry):
        k0 = pl.multiple_of(kb * KEY_BLOCK, KEY_BLOCK)
        kblk = k_ref[pl.ds(k0, KEY_BLOCK), :].astype(BF16)
        vblk = v_ref[pl.ds(k0, KEY_BLOCK), :].astype(BF16)
        sel = sc_ref[kb] > thr
        new = []
        for h in range(B_HEADS):
            m, l, acc = carry[h]
            s = jnp.where(sel, _dot_nt(qb_h[h], kblk), NEG_BIG)
            m_new = jnp.maximum(m, jnp.max(s, axis=-1, keepdims=True))
            alpha = jnp.exp(m - m_new)
            p = jnp.exp(s - m_new)
            l = alpha * l + jnp.sum(p, axis=-1, keepdims=True)
            acc = alpha * acc + _dot(p.astype(BF16), vblk)
            new.append((m_new, l, acc))
        return tuple(new)

    init = tuple((jnp.full((tq, 1), NEG_BIG, F32), jnp.zeros((tq, 1), F32), jnp.zeros((tq, B_WIDTH), F32))
                 for _ in range(B_HEADS))
    res = lax.fori_loop(0, nkb, att_blk, init)
    out = jnp.zeros((tq, B_WIDTH), F32)
    for h in range(B_HEADS):
        _, l, acc = res[h]
        out = out + jnp.where(masks[h], acc / l, 0.0)
    o_ref[...] = out * _silu(zb_ref[...])


def _dsa_prompt(qb, qi, misc, zb, kb, vb, ki8, *, bsz, t_len, n_sel):
    nq = t_len // Q_BLOCK
    qmap = lambda b, j: (b * nq + j, 0)
    kmap = lambda b, j: (b, 0)
    return pl.pallas_call(
        functools.partial(_dsa_body, n_sel=n_sel),
        grid=(bsz, nq),
        in_specs=[pl.BlockSpec((Q_BLOCK, B_WIDTH), qmap),
                  pl.BlockSpec((Q_BLOCK, IDX_HEADS * IDX_HD), qmap),
                  pl.BlockSpec((Q_BLOCK, MISC_W), qmap),
                  pl.BlockSpec((Q_BLOCK, B_WIDTH), qmap),
                  pl.BlockSpec((t_len, B_WIDTH), kmap),
                  pl.BlockSpec((t_len, B_WIDTH), kmap),
                  pl.BlockSpec((t_len, IDX_HEADS * IDX_HD), kmap)],
        out_specs=pl.BlockSpec((Q_BLOCK, B_WIDTH), qmap),
        out_shape=jax.ShapeDtypeStruct((bsz * t_len, B_WIDTH), F32),
        scratch_shapes=[pltpu.VMEM((t_len // KEY_BLOCK, Q_BLOCK, KEY_BLOCK), F32)],
        compiler_params=_cparams(("parallel", "arbitrary")),
        name="dsa_prompt",
    )(qb, qi, misc, zb, kb, vb, ki8)


def _dsa_dec_body(qb_ref, qi_ref, misc_ref, zb_ref, kn_ref, vn_ref, ck_ref, cv_ref, cki_ref, o_ref, sc_ref,
                  *, p_len, tq, n_sel):
    npast = p_len // KEY_BLOCK
    nkb = npast + 1
    qi = qi_ref[...]
    misc = misc_ref[...]
    q2 = jnp.concatenate([qi[:, h * IDX_HD:(h + 1) * IDX_HD] for h in range(IDX_HEADS)], axis=0).astype(BF16)
    w_h = [misc[:, MISC_WI + h:MISC_WI + h + 1] for h in range(IDX_HEADS)]
    lane = lax.broadcasted_iota(jnp.int32, (1, KEY_BLOCK), 1)
    pad_rows = KEY_BLOCK - tq

    def scores(kblk):
        lg = _dot_nt(q2, kblk.astype(BF16))
        acc = jnp.zeros((tq, KEY_BLOCK), F32)
        for h in range(IDX_HEADS):
            acc = acc + w_h[h] * jnp.maximum(lg[h * tq:(h + 1) * tq, :], 0.0)
        return acc

    def score_blk(kb, carry):
        k0 = pl.multiple_of(kb * KEY_BLOCK, KEY_BLOCK)
        sc_ref[kb] = scores(cki_ref[pl.ds(k0, KEY_BLOCK), :])
        return carry

    lax.fori_loop(0, npast, score_blk, 0)
    ki_new = jnp.concatenate([misc[:, 0:IDX_HD], jnp.zeros((pad_rows, IDX_HD), F32)], axis=0)
    sc_ref[npast] = jnp.where(lane < tq, scores(ki_new), -jnp.inf)

    n_adm = jnp.full((tq, 1), float(p_len + tq), F32)
    thr = _topk_threshold(sc_ref, nkb, tq, n_sel, n_adm)

    qb = qb_ref[...] * (B_HD ** -0.5)
    masks = [_lane_mask(B_WIDTH, h * B_HD, (h + 1) * B_HD) for h in range(B_HEADS)]
    qs = jnp.concatenate([jnp.where(masks[h], qb, 0.0) for h in range(B_HEADS)], axis=0).astype(BF16)
    rows = B_HEADS * tq

    def att(carry, kblk, vblk, sel):
        m, l, acc = carry
        sel4 = jnp.concatenate([sel] * B_HEADS, axis=0)
        s = jnp.where(sel4, _dot_nt(qs, kblk.astype(BF16)), NEG_BIG)
        m_new = jnp.maximum(m, jnp.max(s, axis=-1, keepdims=True))
        alpha = jnp.exp(m - m_new)
        p = jnp.exp(s - m_new)
        l = alpha * l + jnp.sum(p, axis=-1, keepdims=True)
        acc = alpha * acc + _dot(p.astype(BF16), vblk.astype(BF16))
        return m_new, l, acc

    def att_blk(kb, carry):
        k0 = pl.multiple_of(kb * KEY_BLOCK, KEY_BLOCK)
        return att(carry, ck_ref[pl.ds(k0, KEY_BLOCK), :], cv_ref[pl.ds(k0, KEY_BLOCK), :], sc_ref[kb] > thr)

    init = (jnp.full((rows, 1), NEG_BIG, F32), jnp.zeros((rows, 1), F32), jnp.zeros((rows, B_WIDTH), F32))
    carry = lax.fori_loop(0, npast, att_blk, init)
    zpad = jnp.zeros((pad_rows, B_WIDTH), F32)
    _, l, acc = att(carry, jnp.concatenate([kn_ref[...], zpad], axis=0),
                    jnp.concatenate([vn_ref[...], zpad], axis=0), sc_ref[npast] > thr)
    res = acc / l
    out = jnp.zeros((tq, B_WIDTH), F32)
    for h in range(B_HEADS):
        out = out + jnp.where(masks[h], res[h * tq:(h + 1) * tq, :], 0.0)
    o_ref[...] = out * _silu(zb_ref[...])


def _dsa_decode(qb, qi, misc, zb, kb, vb, ck, cv, cki, *, bsz, t_len, p_len, n_sel):
    qmap = lambda b: (b, 0)
    cmap = lambda b: (b, 0, 0)
    return pl.pallas_call(
        functools.partial(_dsa_dec_body, p_len=p_len, tq=t_len, n_sel=n_sel),
        grid=(bsz,),
        in_specs=[pl.BlockSpec((t_len, B_WIDTH), qmap),
                  pl.BlockSpec((t_len, IDX_HEADS * IDX_HD), qmap),
                  pl.BlockSpec((t_len, MISC_W), qmap),
                  pl.BlockSpec((t_len, B_WIDTH), qmap),
                  pl.BlockSpec((t_len, B_WIDTH), qmap),
                  pl.BlockSpec((t_len, B_WIDTH), qmap),
                  pl.BlockSpec((None, p_len, B_WIDTH), cmap),
                  pl.BlockSpec((None, p_len, B_WIDTH), cmap),
                  pl.BlockSpec((None, p_len, IDX_HD), cmap)],
        out_specs=pl.BlockSpec((t_len, B_WIDTH), qmap),
        out_shape=jax.ShapeDtypeStruct((bsz * t_len, B_WIDTH), F32),
        scratch_shapes=[pltpu.VMEM((p_len // KEY_BLOCK + 1, t_len, KEY_BLOCK), F32)],
        compiler_params=_cparams(("parallel",)),
        name="dsa_decode",
    )(qb, qi, misc, zb, kb, vb, ck, cv, cki)


def _mem_body(qm_ref, zm_ref, mk_ref, mv_ref, o_ref):
    qm = qm_ref[...] * (M_HD ** -0.5)
    mk = mk_ref[...].astype(BF16)
    mv = mv_ref[...].astype(BF16)
    out = jnp.zeros(qm.shape, F32)
    for h in range(M_HEADS):
        mask = _lane_mask(M_WIDTH, h * M_HD, (h + 1) * M_HD)
        s = _dot_nt(jnp.where(mask, qm, 0.0).astype(BF16), mk)
        p = jnp.exp(s - jnp.max(s, axis=-1, keepdims=True))
        l = jnp.sum(p, axis=-1, keepdims=True)
        out = out + jnp.where(mask, _dot(p.astype(BF16), mv) / l, 0.0)
    o_ref[...] = out * _silu(zm_ref[...])


def _mem_attn(qm, zm, mk, mv, *, bsz, t_len, tq):
    nq = t_len // tq
    qmap = lambda b, j: (b * nq + j, 0)
    kmap = lambda b, j: (b, 0, 0)
    return pl.pallas_call(
        _mem_body,
        grid=(bsz, nq),
        in_specs=[pl.BlockSpec((tq, M_WIDTH), qmap),
                  pl.BlockSpec((tq, M_WIDTH), qmap),
                  pl.BlockSpec((None, N_MEM, M_WIDTH), kmap),
                  pl.BlockSpec((None, N_MEM, M_WIDTH), kmap)],
        out_specs=pl.BlockSpec((tq, M_WIDTH), qmap),
        out_shape=jax.ShapeDtypeStruct((bsz * t_len, M_WIDTH), F32),
        compiler_params=_cparams(("parallel", "parallel")),
        name="mem_attn",
    )(qm, zm, mk, mv)


def _oproj_body(x_ref, oa_ref, ob_ref, om_ref, w_ref, y_ref):
    acc = _dot(oa_ref[...].astype(BF16), w_ref[0:A_WIDTH, :])
    acc = acc + _dot(ob_ref[...].astype(BF16), w_ref[A_WIDTH:A_WIDTH + B_WIDTH, :])
    acc = acc + _dot(om_ref[...].astype(BF16), w_ref[A_WIDTH + B_WIDTH:, :])
    y_ref[...] = x_ref[...] + acc


def _oproj(x, oa, ob, om, w, tm):
    m, d = x.shape
    rmap = lambda i: (i, 0)
    return pl.pallas_call(
        _oproj_body,
        grid=(m // tm,),
        in_specs=[pl.BlockSpec((tm, d), rmap),
                  pl.BlockSpec((tm, A_WIDTH), rmap),
                  pl.BlockSpec((tm, B_WIDTH), rmap),
                  pl.BlockSpec((tm, M_WIDTH), rmap),
                  pl.BlockSpec(w.shape, lambda i: (0, 0))],
        out_specs=pl.BlockSpec((tm, d), rmap),
        out_shape=jax.ShapeDtypeStruct((m, d), F32),
        compiler_params=_cparams(("parallel",)),
        name="oproj",
    )(x, oa, ob, om, w)


def _prep_weights(g_in, w_in, conv_w, a_log, dt_bias, g_o, g_qb, g_kb, g_ki, g_qm, w_out):
    offs = np.concatenate([[0], np.cumsum(IN_SPLITS)])
    col = lambda i: w_in[:, int(offs[i]):int(offs[i + 1])]
    (w_qkv, w_za, w_ba, w_aa, w_qb, w_kb, w_vb, w_zb, w_qi, w_ki, w_wi, w_qm, w_zm) = [col(i) for i in range(13)]
    d = w_in.shape[0]
    w_misc = jnp.concatenate(
        [w_ki, w_wi, w_ba, w_aa, jnp.zeros((d, MISC_W - MISC_A - A_HEADS), w_in.dtype)], axis=1)
    w_ki8 = jnp.tile(w_ki, (1, IDX_HEADS))
    w_all = jnp.concatenate([w_qkv, w_za, w_qb, w_kb, w_vb, w_zb, w_qi, w_qm, w_zm, w_misc, w_ki8],
                            axis=1).astype(BF16)
    groups = [(3 * A_WIDTH, False), (A_WIDTH, False), (B_WIDTH, True), (B_WIDTH, True), (B_WIDTH, False),
              (B_WIDTH, False), (IDX_HEADS * IDX_HD, False), (M_WIDTH, True), (M_WIDTH, False),
              (MISC_W, True), (IDX_HEADS * IDX_HD, True)]
    bd64 = _group_mean_matrix(B_WIDTH, B_HD, B_WIDTH)
    ones256 = jnp.ones((1, B_WIDTH), F32)
    misc_gain = jnp.concatenate([g_ki, jnp.full((IDX_HEADS,), IDX_SCALE, F32),
                                 jnp.ones((MISC_W - MISC_B,), F32)])[None, :]
    misc_nm = (jnp.arange(MISC_W) < IDX_HD).astype(F32)[None, :]
    aux = [bd64, jnp.tile(g_qb, B_HEADS)[None, :], ones256,
           bd64, jnp.tile(g_kb, B_HEADS)[None, :], ones256,
           bd64, jnp.tile(g_qm, M_HEADS)[None, :], ones256,
           _group_mean_matrix(MISC_W, IDX_HD, IDX_HD), misc_gain, misc_nm,
           _group_mean_matrix(IDX_HEADS * IDX_HD, IDX_HD, IDX_HEADS * IDX_HD),
           jnp.tile(g_ki, IDX_HEADS)[None, :], ones256]
    pad_a = lambda v: jnp.zeros((1, MISC_W), F32).at[0, MISC_A:MISC_A + A_HEADS].set(v)
    return dict(g_in=g_in[None, :], w_all=w_all, groups=groups, aux=aux, conv_w=conv_w,
                avec=pad_a(-jnp.exp(a_log)), dtvec=pad_a(dt_bias), g_o=g_o[None, :],
                w_out=w_out.astype(BF16))


def _layer(x, mem_k, mem_v, conv_buf, s0, past, gdn_chunk, gdn_rows, wp):
    bsz, t_len, d = x.shape
    m = bsz * t_len
    x2 = x.reshape(m, d)
    tm = min(256, m)
    (qkv, za, qb, kb, vb, zb, qi, qm, zm, misc, ki8) = _proj(x2, wp["g_in"], wp["w_all"], wp["groups"],
                                                           wp["aux"], tm)
    cbuf8 = jnp.concatenate([jnp.zeros((bsz, 8 - (CONV_W - 1), 3 * A_WIDTH), F32), conv_buf], axis=1)
    oa, conv_new, s_new = _gdn(qkv, za, misc, cbuf8, s0, wp["conv_w"], wp["avec"], wp["dtvec"], wp["g_o"],
                               bsz=bsz, t_len=t_len, chunk=gdn_chunk, rows=gdn_rows)
    if past is None:
        n_sel = min(TOPK_MAX, t_len // 4)
        ob = _dsa_prompt(qb, qi, misc, zb, kb, vb, ki8, bsz=bsz, t_len=t_len, n_sel=n_sel)
    else:
        ck, cv, cki = past
        p_len = ck.shape[1]
        assert (p_len + t_len - 1) // CHUNK <= p_len // CHUNK and p_len % KEY_BLOCK == 0
        n_sel = min(TOPK_MAX, (p_len + t_len) // 4)
        ob = _dsa_decode(qb, qi, misc, zb, kb, vb, ck.reshape(bsz, p_len, B_WIDTH),
                         cv.reshape(bsz, p_len, B_WIDTH), cki, bsz=bsz, t_len=t_len, p_len=p_len, n_sel=n_sel)
    om = _mem_attn(qm, zm, mem_k, mem_v, bsz=bsz, t_len=t_len, tq=min(256, t_len))
    y = _oproj(x2, oa, ob, om, wp["w_out"], min(512, m))
    return (y.reshape(bsz, t_len, d), conv_new, s_new,
            kb.reshape(bsz, t_len, B_HEADS, B_HD), vb.reshape(bsz, t_len, B_HEADS, B_HD),
            misc[:, :IDX_HD].reshape(bsz, t_len, IDX_HD))


def _memory_kv(mem, g_mem, w_mem_kv, g_km):
    bsz, n_mem, d = mem.shape
    groups = [(M_WIDTH, True), (M_WIDTH, False)]
    aux = [_group_mean_matrix(M_WIDTH, M_HD, M_WIDTH), jnp.tile(g_km, M_HEADS)[None, :],
           jnp.ones((1, M_WIDTH), F32)]
    mk, mv = _proj(mem.reshape(bsz * n_mem, d), g_mem[None, :], w_mem_kv.astype(BF16), groups, aux, 256)
    return mk.reshape(bsz, n_mem, M_WIDTH), mv.reshape(bsz, n_mem, M_WIDTH)


def kernel(x_prompt, x_sample, state_conv_A, state_ssm_A, cache_k_B, cache_v_B, cache_kidx_B, cache_mem_k,
           cache_mem_v, mem_prompt, g_in, w_in, conv_w_A, a_log_A, dt_bias_A, g_o_A, g_q_B, g_k_B, g_kidx_B,
           g_mem, w_mem_kv, g_q_M, g_k_M, w_out):
    depth = w_in.shape[0]
    assert depth == 1
    l = 0
    bp, t_p, _ = x_prompt.shape
    bs, t_s, _ = x_sample.shape
    wp = _prep_weights(g_in[l], w_in[l], conv_w_A[l], a_log_A[l], dt_bias_A[l], g_o_A[l], g_q_B[l], g_k_B[l],
                       g_kidx_B[l], g_q_M[l], w_out[l])
    mk, mv = _memory_kv(mem_prompt, g_mem[l], w_mem_kv[l], g_k_M[l])
    zero_conv = jnp.zeros((bp, CONV_W - 1, 3 * A_WIDTH), F32)
    zero_ssm = jnp.zeros((bp, A_HEADS, A_DK, A_DV), F32)
    yp, c1, s1, k1, v1, ki1 = _layer(x_prompt, mk, mv, zero_conv, zero_ssm, None, CHUNK, 4 * CHUNK, wp)
    ys, c2, s2, k2, v2, ki2 = _layer(
        x_sample, cache_mem_k[l].reshape(bs, N_MEM, M_WIDTH), cache_mem_v[l].reshape(bs, N_MEM, M_WIDTH),
        state_conv_A[l], state_ssm_A[l], (cache_k_B[l], cache_v_B[l], cache_kidx_B[l]), t_s, t_s, wp)
    st = lambda a: a[None]
    return (yp, ys, st(c1), st(s1), st(k1), st(v1), st(ki1),
            st(mk.reshape(bp, N_MEM, M_HEADS, M_HD)), st(mv.reshape(bp, N_MEM, M_HEADS, M_HD)),
            st(c2), st(s2), st(k2), st(v2), st(ki2))
```

```python
import functools
import math

import numpy as np
import jax
import jax.numpy as jnp
from jax import lax
from jax.experimental import pallas as pl
from jax.experimental.pallas import tpu as pltpu

F32 = jnp.float32
BF16 = jnp.bfloat16
HIGHEST = lax.Precision.HIGHEST

D_MODEL = 1024
CHUNK = 64
A_HEADS = 4
A_DK = 128
A_DV = 128
A_WIDTH = A_HEADS * A_DV
CONV_W = 4
B_HEADS = 4
B_HD = 64
B_WIDTH = B_HEADS * B_HD
IDX_HEADS = 8
IDX_HD = 32
IDX_SCALE = (IDX_HEADS ** -0.5) * (IDX_HD ** -0.5)
TOPK_MAX = 256
Q_BLOCK = 128
N_MEM = 256
M_HEADS = 4
M_HD = 64
M_WIDTH = M_HEADS * M_HD
EPS = 1e-6
IN_SPLITS = (3 * A_WIDTH, A_WIDTH, A_HEADS, A_HEADS,
             B_WIDTH, B_WIDTH, B_WIDTH, B_WIDTH, IDX_HEADS * IDX_HD, IDX_HD, IDX_HEADS,
             M_WIDTH, M_WIDTH)

LANES = 128
KEY_BLOCK = 256
MISC_W = LANES
MISC_WI = IDX_HD
MISC_B = IDX_HD + IDX_HEADS
MISC_A = MISC_B + A_HEADS
VMEM_LIMIT = 48 * 1024 * 1024
NEG_BIG = -1e30
BISECT_MAX_ITERS = 32


def _cparams(sem):
    return pltpu.CompilerParams(dimension_semantics=sem, vmem_limit_bytes=VMEM_LIMIT)


def _dot(a, b):
    return jnp.dot(a, b, preferred_element_type=F32)


def _dot_nt(a, b, precision=None):
    return lax.dot_general(a, b, (((1,), (1,)), ((), ())), precision=precision,
                           preferred_element_type=F32)


def _split2(x):
    hi = x.astype(BF16)
    return hi, (x - hi.astype(F32)).astype(BF16)


def _mm_split(a, b):
    ah, al = a
    bh, bl = b
    n = ah.shape[0]
    t = _dot(jnp.concatenate([ah, al], axis=0), bh)
    return t[:n] + t[n:] + _dot(ah, bl)


def _silu(x):
    return x * jax.nn.sigmoid(x)


def _lane_mask(width, lo, hi):
    lane = lax.broadcasted_iota(jnp.int32, (1, width), 1)
    return (lane >= lo) & (lane < hi)


def _proj_body(*refs, groups):
    x_ref, g_ref, w_ref = refs[0], refs[1], refs[2]
    n_aux = 3 * sum(1 for _, normed in groups if normed)
    aux = refs[3:3 + n_aux]
    outs = refs[3 + n_aux:]
    x = x_ref[...]
    h = x * lax.rsqrt(jnp.mean(x * x, axis=-1, keepdims=True) + EPS) * g_ref[...]
    hb = h.astype(BF16)
    off = 0
    ai = 0
    for (width, normed), o_ref in zip(groups, outs):
        y = _dot(hb, w_ref[:, off:off + width])
        if normed:
            bd_ref, gain_ref, nm_ref = aux[ai], aux[ai + 1], aux[ai + 2]
            ai += 3
            sq = y * y
            hi = sq.astype(BF16)
            lo = (sq - hi.astype(F32)).astype(BF16)
            ms = _dot(hi, bd_ref[...]) + _dot(lo, bd_ref[...])
            scale = jnp.where(nm_ref[...] > 0.0, lax.rsqrt(ms + EPS), 1.0)
            y = y * scale * gain_ref[...]
        o_ref[...] = y
        off += width


def _group_mean_matrix(width, group, n_lanes):
    i = np.arange(width)
    m = ((i[:, None] // group) == (i[None, :] // group)) & (i[:, None] < n_lanes) & (i[None, :] < n_lanes)
    return jnp.asarray(m.astype(np.float32) / group, dtype=BF16)


def _proj(x, g, w, groups, aux, tm):
    m, d = x.shape
    nw = w.shape[1]
    in_specs = [pl.BlockSpec((tm, d), lambda i: (i, 0)),
                pl.BlockSpec((1, d), lambda i: (0, 0)),
                pl.BlockSpec((d, nw), lambda i: (0, 0))]
    for a in aux:
        in_specs.append(pl.BlockSpec(a.shape, lambda i: (0, 0)))
    out_shape = [jax.ShapeDtypeStruct((m, wd), F32) for wd, _ in groups]
    out_specs = [pl.BlockSpec((tm, wd), lambda i: (i, 0)) for wd, _ in groups]
    return pl.pallas_call(
        functools.partial(_proj_body, groups=tuple(groups)),
        grid=(m // tm,),
        in_specs=in_specs, out_specs=out_specs, out_shape=out_shape,
        compiler_params=_cparams(("parallel",)),
        name="proj",
    )(x, g, w, *aux)


def _gdn_body(qkv_ref, za_ref, misc_ref, cbuf_ref, s0_ref, cw_ref, avec_ref, dtvec_ref, go_ref,
              o_ref, convnew_ref, sfin_ref,
              xp_ref, q_sc, k_sc, v_sc, g_sc, b_sc, s_sc, *, chunk, rows):
    t = pl.program_id(1)
    c = chunk

    @pl.when(t == 0)
    def _():
        xp_ref[0:8, :] = cbuf_ref[0]
        s_sc[...] = s0_ref[0]

    xp_ref[8:8 + rows, :] = qkv_ref[...]
    cw = cw_ref[...]
    y = xp_ref[5:5 + rows, :] * cw[0:1, :]
    for j in range(1, CONV_W):
        y = y + xp_ref[5 + j:5 + j + rows, :] * cw[j:j + 1, :]
    y = _silu(y)
    convnew_ref[0] = xp_ref[rows + 5:rows + 8, :]
    xp_ref[0:8, :] = xp_ref[rows:rows + 8, :]

    for h in range(A_HEADS):
        qh = y[:, h * A_DK:(h + 1) * A_DK]
        kh = y[:, A_WIDTH + h * A_DK:A_WIDTH + (h + 1) * A_DK]
        q_sc[:, h * A_DK:(h + 1) * A_DK] = qh * (
            lax.rsqrt(jnp.sum(qh * qh, axis=-1, keepdims=True) + EPS) * (A_DK ** -0.5))
        k_sc[:, h * A_DK:(h + 1) * A_DK] = kh * lax.rsqrt(jnp.sum(kh * kh, axis=-1, keepdims=True) + EPS)
    v_sc[...] = y[:, 2 * A_WIDTH:3 * A_WIDTH]
    misc = misc_ref[...]
    b_sc[...] = jax.nn.sigmoid(misc)
    g_sc[...] = avec_ref[...] * jax.nn.softplus(misc + dtvec_ref[...])

    ri = lax.broadcasted_iota(jnp.int32, (c, c), 0)
    ci = lax.broadcasted_iota(jnp.int32, (c, c), 1)
    tri = ri >= ci
    strict = ri > ci
    eye_f = (ri == ci).astype(F32)
    tri3 = (lax.broadcasted_iota(jnp.int32, (c, 3 * c), 0)
            >= lax.broadcasted_iota(jnp.int32, (c, 3 * c), 1) % c).astype(BF16)
    go = go_ref[...]
    n_chunks = rows // c

    problems = [(ic, h) for ic in range(n_chunks) for h in range(A_HEADS)]
    gc_all, gc_t = [], []
    for ic in range(n_chunks):
        g = g_sc[ic * c:(ic + 1) * c, :]
        g1 = g.astype(BF16)
        r1 = g - g1.astype(F32)
        g2 = r1.astype(BF16)
        g3 = (r1 - g2.astype(F32)).astype(BF16)
        gc_all.append(_dot(tri3, jnp.concatenate([g1, g2, g3], axis=0)))
    for ic in range(n_chunks):
        gc_t.append(gc_all[ic].T)

    def load(ref, ic, h):
        return ref[ic * c:(ic + 1) * c, h * A_DK:(h + 1) * A_DK]

    col = lambda a, lane: a[:, lane:lane + 1]
    k_bf = [load(k_sc, ic, h).astype(BF16) for ic, h in problems]
    kb = [load(k_sc, ic, h) * col(b_sc[ic * c:(ic + 1) * c, :], MISC_B + h) for ic, h in problems]
    kk = [_dot_nt(kb[i].astype(BF16), k_bf[i]) for i in range(len(problems))]
    qk_raw = [_dot_nt(load(q_sc, ic, h).astype(BF16), k_bf[i]) for i, (ic, h) in enumerate(problems)]
    decay = []
    for ic, h in problems:
        diff = col(gc_all[ic], MISC_A + h) - gc_t[ic][MISC_A + h:MISC_A + h + 1, :]
        decay.append(jnp.where(tri, jnp.exp(jnp.where(tri, diff, 0.0)), 0.0))
    lmat = [jnp.where(strict, kk[i] * decay[i], 0.0) for i in range(len(problems))]
    qk = [(qk_raw[i] * decay[i]).astype(BF16) for i in range(len(problems))]
    x_inv = [eye_f - m for m in lmat]
    p_split = [_split2(m) for m in lmat]
    for _ in range(int(math.log2(c)) - 1):
        p_split = [_split2(_mm_split(ps, ps)) for ps in p_split]
        x_inv = [x + _mm_split(_split2(x), ps) for x, ps in zip(x_inv, p_split)]
    sol = []
    for i, (ic, h) in enumerate(problems):
        egc = jnp.exp(col(gc_all[ic], MISC_A + h))
        beta = col(b_sc[ic * c:(ic + 1) * c, :], MISC_B + h)
        rhs = jnp.concatenate([load(v_sc, ic, h) * beta, kb[i] * egc], axis=-1).astype(BF16)
        th, tl = _split2(x_inv[i])
        sol.append(_dot(th, rhs) + _dot(tl, rhs))
    pre = []
    for i, (ic, h) in enumerate(problems):
        gc = col(gc_all[ic], MISC_A + h)
        g_last = gc[c - 1:c, :]
        k_dec_t = (load(k_sc, ic, h) * jnp.exp(g_last - gc)).T.astype(BF16)
        wq = jnp.concatenate([sol[i][:, A_DV:], load(q_sc, ic, h) * jnp.exp(gc)], axis=0).astype(BF16)
        pre.append((sol[i][:, :A_DV], wq, k_dec_t, jnp.exp(g_last)))

    heads = range(A_HEADS)
    s_cur = [s_sc[h] for h in heads]
    for ic in range(n_chunks):
        pr = [pre[ic * A_HEADS + h] for h in heads]
        ws = [_dot(pr[h][1], s_cur[h].astype(BF16)) for h in heads]
        ub = [(pr[h][0] - ws[h][:c]).astype(BF16) for h in heads]
        o = [ws[h][c:] + _dot(qk[ic * A_HEADS + h], ub[h]) for h in heads]
        s_cur = [s_cur[h] * pr[h][3] + _dot(pr[h][2], ub[h]) for h in heads]
        for h in heads:
            on = o[h] * lax.rsqrt(jnp.mean(o[h] * o[h], axis=-1, keepdims=True) + EPS) * go
            o_ref[ic * c:(ic + 1) * c, h * A_DV:(h + 1) * A_DV] = on * _silu(load(za_ref, ic, h))
    for h in heads:
        s_sc[h] = s_cur[h]
    sfin_ref[0] = s_sc[...]


def _gdn(qkv, za, misc, cbuf8, s0, conv_w, avec, dtvec, g_o, *, bsz, t_len, chunk, rows):
    nt = t_len // rows
    w3 = 3 * A_WIDTH
    row_map = lambda b, t: (b * nt + t, 0)
    const2 = lambda b, t: (0, 0)
    return pl.pallas_call(
        functools.partial(_gdn_body, chunk=chunk, rows=rows),
        grid=(bsz, nt),
        in_specs=[pl.BlockSpec((rows, w3), row_map),
                  pl.BlockSpec((rows, A_WIDTH), row_map),
                  pl.BlockSpec((rows, MISC_W), row_map),
                  pl.BlockSpec((1, 8, w3), lambda b, t: (b, 0, 0)),
                  pl.BlockSpec((1, A_HEADS, A_DK, A_DV), lambda b, t: (b, 0, 0, 0)),
                  pl.BlockSpec((CONV_W, w3), const2),
                  pl.BlockSpec((1, MISC_W), const2),
                  pl.BlockSpec((1, MISC_W), const2),
                  pl.BlockSpec((1, A_DV), const2)],
        out_specs=[pl.BlockSpec((rows, A_WIDTH), row_map),
                   pl.BlockSpec((1, CONV_W - 1, w3), lambda b, t: (b, 0, 0)),
                   pl.BlockSpec((1, A_HEADS, A_DK, A_DV), lambda b, t: (b, 0, 0, 0))],
        out_shape=[jax.ShapeDtypeStruct((bsz * t_len, A_WIDTH), F32),
                   jax.ShapeDtypeStruct((bsz, CONV_W - 1, w3), F32),
                   jax.ShapeDtypeStruct((bsz, A_HEADS, A_DK, A_DV), F32)],
        scratch_shapes=[pltpu.VMEM((rows + 8, w3), F32),
                        pltpu.VMEM((rows, A_WIDTH), F32),
                        pltpu.VMEM((rows, A_WIDTH), F32),
                        pltpu.VMEM((rows, A_WIDTH), F32),
                        pltpu.VMEM((rows, MISC_W), F32),
                        pltpu.VMEM((rows, MISC_W), F32),
                        pltpu.VMEM((A_HEADS, A_DK, A_DV), F32)],
        compiler_params=_cparams(("parallel", "arbitrary")),
        name="gdn",
    )(qkv, za, misc, cbuf8, s0, conv_w, avec, dtvec, g_o)


def _topk_threshold(sc_ref, nkb, n_rows, n_sel, n_adm):
    kf = float(n_sel)
    shape = (n_rows, KEY_BLOCK)

    def reduce_blocks(fn, init, combine):
        def body(kb, acc):
            return combine(acc, fn(kb, sc_ref[kb]))
        return lax.fori_loop(0, nkb, body, init)

    def count_gt(thr):
        acc = reduce_blocks(lambda kb, s: jnp.where(s > thr, 1.0, 0.0), jnp.zeros(shape, F32), jnp.add)
        return jnp.sum(acc, axis=-1, keepdims=True)

    vmax = jnp.max(reduce_blocks(lambda kb, s: s, jnp.full(shape, -jnp.inf, F32), jnp.maximum),
                   axis=-1, keepdims=True)
    vmin = jnp.min(reduce_blocks(lambda kb, s: jnp.where(s == -jnp.inf, jnp.inf, s),
                                 jnp.full(shape, jnp.inf, F32), jnp.minimum),
                   axis=-1, keepdims=True)
    lo0 = vmin - jnp.maximum(1.0, jnp.abs(vmin))
    hi0 = vmax
    cnt_lo0 = n_adm
    cnt_hi0 = jnp.zeros_like(n_adm)

    def cond(st):
        it, go = st[0], st[1]
        return (it < BISECT_MAX_ITERS) & (go > 0)

    def body(st):
        it, _, lo, hi, cnt_lo, cnt_hi = st
        mid = 0.5 * lo + 0.5 * hi
        cm = count_gt(mid)
        up = cm >= kf
        lo = jnp.where(up, mid, lo)
        cnt_lo = jnp.where(up, cm, cnt_lo)
        hi = jnp.where(up, hi, mid)
        cnt_hi = jnp.where(up, cnt_hi, cm)
        go = (jnp.max(cnt_lo) > kf).astype(jnp.int32)
        return it + 1, go, lo, hi, cnt_lo, cnt_hi

    go0 = (jnp.max(cnt_lo0) > kf).astype(jnp.int32)
    _, go, lo, hi, cnt_lo, cnt_hi = lax.while_loop(
        cond, body, (jnp.int32(0), go0, lo0, hi0, cnt_lo0, cnt_hi0))

    fix = cnt_lo > kf
    need0 = jnp.where(fix, kf - cnt_hi, 0.0)
    lane = lax.broadcasted_iota(jnp.int32, shape, 1)
    big_idx = jnp.int32(2 ** 30)

    def fix_cond(need):
        return jnp.max(need) > 0.0

    def fix_body(need):
        def in_cluster(s):
            return (s > lo) & (s <= hi)
        mval = jnp.max(reduce_blocks(lambda kb, s: jnp.where(in_cluster(s), s, -jnp.inf),
                                     jnp.full(shape, -jnp.inf, F32), jnp.maximum),
                       axis=-1, keepdims=True)
        idx = jnp.min(reduce_blocks(
            lambda kb, s: jnp.where(in_cluster(s) & (s == mval), lane + kb * KEY_BLOCK, big_idx),
            jnp.full(shape, big_idx, jnp.int32), jnp.minimum), axis=-1, keepdims=True)
        active = need > 0.0

        def promote(kb, carry):
            s = sc_ref[kb]
            sc_ref[kb] = jnp.where(active & ((lane + kb * KEY_BLOCK) == idx), jnp.inf, s)
            return carry
        lax.fori_loop(0, nkb, promote, 0)
        return jnp.where(active, need - 1.0, need)

    lax.while_loop(fix_cond, fix_body, need0)
    return jnp.where(fix, hi, lo)


def _dsa_body(qb_ref, qi_ref, misc_ref, zb_ref, k_ref, v_ref, ki8_ref, o_ref, sc_ref, *, n_sel):
    j = pl.program_id(1)
    tq = Q_BLOCK
    nkb = (j + 2) // 2
    qi = qi_ref[...]
    misc = misc_ref[...]
    row = lax.broadcasted_iota(jnp.int32, (tq, 1), 0)
    qchunk = 2 * j + (row >= CHUNK).astype(jnp.int32)
    lane = lax.broadcasted_iota(jnp.int32, (1, KEY_BLOCK), 1)

    qi_h = [jnp.where(_lane_mask(IDX_HEADS * IDX_HD, h * IDX_HD, (h + 1) * IDX_HD), qi, 0.0).astype(BF16)
            for h in range(IDX_HEADS)]
    w_h = [misc[:, MISC_WI + h:MISC_WI + h + 1] for h in range(IDX_HEADS)]

    def score_blk(kb, carry):
        k0 = pl.multiple_of(kb * KEY_BLOCK, KEY_BLOCK)
        kblk = ki8_ref[pl.ds(k0, KEY_BLOCK), :].astype(BF16)
        acc = jnp.zeros((tq, KEY_BLOCK), F32)
        for h in range(IDX_HEADS):
            acc = acc + w_h[h] * jnp.maximum(_dot_nt(qi_h[h], kblk), 0.0)
        adm = ((k0 + lane) >> 6) <= qchunk
        sc_ref[kb] = jnp.where(adm, acc, -jnp.inf)
        return carry

    lax.fori_loop(0, nkb, score_blk, 0)

    n_adm = ((qchunk + 1) * CHUNK).astype(F32)
    thr = _topk_threshold(sc_ref, nkb, tq, n_sel, n_adm)

    qb = qb_ref[...] * (B_HD ** -0.5)
    masks = [_lane_mask(B_WIDTH, h * B_HD, (h + 1) * B_HD) for h in range(B_HEADS)]
    qb_h = [jnp.where(masks[h], qb, 0.0).astype(BF16) for h in range(B_HEADS)]

    def att_blk(kb, carry):
        k0 = pl.multiple_of(kb * KEY_BLOCK, KEY_BLOCK)
        kblk = k_ref[pl.ds(k0, KEY_BLOCK), :].astype(BF16)
        vblk = v_ref[pl.ds(k0, KEY_BLOCK), :].astype(BF16)
        sel = sc_ref[kb] > thr
        new = []
        for h in range(B_HEADS):
            m, l, acc = carry[h]
            s = jnp.where(sel, _dot_nt(qb_h[h], kblk), NEG_BIG)
            m_new = jnp.maximum(m, jnp.max(s, axis=-1, keepdims=True))
            alpha = jnp.exp(m - m_new)
            p = jnp.exp(s - m_new)
            l = alpha * l + jnp.sum(p, axis=-1, keepdims=True)
            acc = alpha * acc + _dot(p.astype(BF16), vblk)
            new.append((m_new, l, acc))
        return tuple(new)

    init = tuple((jnp.full((tq, 1), NEG_BIG, F32), jnp.zeros((tq, 1), F32), jnp.zeros((tq, B_WIDTH), F32))
                 for _ in range(B_HEADS))
    res = lax.fori_loop(0, nkb, att_blk, init)
    out = jnp.zeros((tq, B_WIDTH), F32)
    for h in range(B_HEADS):
        _, l, acc = res[h]
        out = out + jnp.where(masks[h], acc / l, 0.0)
    o_ref[...] = out * _silu(zb_ref[...])


def _dsa_prompt(qb, qi, misc, zb, kb, vb, ki8, *, bsz, t_len, n_sel):
    nq = t_len // Q_BLOCK
    qmap = lambda b, j: (b * nq + j, 0)
    kmap = lambda b, j: (b, 0)
    return pl.pallas_call(
        functools.partial(_dsa_body, n_sel=n_sel),
        grid=(bsz, nq),
        in_specs=[pl.BlockSpec((Q_BLOCK, B_WIDTH), qmap),
                  pl.BlockSpec((Q_BLOCK, IDX_HEADS * IDX_HD), qmap),
                  pl.BlockSpec((Q_BLOCK, MISC_W), qmap),
                  pl.BlockSpec((Q_BLOCK, B_WIDTH), qmap),
                  pl.BlockSpec((t_len, B_WIDTH), kmap),
                  pl.BlockSpec((t_len, B_WIDTH), kmap),
                  pl.BlockSpec((t_len, IDX_HEADS * IDX_HD), kmap)],
        out_specs=pl.BlockSpec((Q_BLOCK, B_WIDTH), qmap),
        out_shape=jax.ShapeDtypeStruct((bsz * t_len, B_WIDTH), F32),
        scratch_shapes=[pltpu.VMEM((t_len // KEY_BLOCK, Q_BLOCK, KEY_BLOCK), F32)],
        compiler_params=_cparams(("parallel", "arbitrary")),
        name="dsa_prompt",
    )(qb, qi, misc, zb, kb, vb, ki8)


def _dsa_dec_body(qb_ref, qi_ref, misc_ref, zb_ref, kn_ref, vn_ref, ck_ref, cv_ref, cki_ref, o_ref, sc_ref,
                  *, p_len, tq, n_sel):
    npast = p_len // KEY_BLOCK
    nkb = npast + 1
    qi = qi_ref[...]
    misc = misc_ref[...]
    q2 = jnp.concatenate([qi[:, h * IDX_HD:(h + 1) * IDX_HD] for h in range(IDX_HEADS)], axis=0).astype(BF16)
    w_h = [misc[:, MISC_WI + h:MISC_WI + h + 1] for h in range(IDX_HEADS)]
    lane = lax.broadcasted_iota(jnp.int32, (1, KEY_BLOCK), 1)
    pad_rows = KEY_BLOCK - tq

    def scores(kblk):
        lg = _dot_nt(q2, kblk.astype(BF16))
        acc = jnp.zeros((tq, KEY_BLOCK), F32)
        for h in range(IDX_HEADS):
            acc = acc + w_h[h] * jnp.maximum(lg[h * tq:(h + 1) * tq, :], 0.0)
        return acc

    def score_blk(kb, carry):
        k0 = pl.multiple_of(kb * KEY_BLOCK, KEY_BLOCK)
        sc_ref[kb] = scores(cki_ref[pl.ds(k0, KEY_BLOCK), :])
        return carry

    lax.fori_loop(0, npast, score_blk, 0)
    ki_new = jnp.concatenate([misc[:, 0:IDX_HD], jnp.zeros((pad_rows, IDX_HD), F32)], axis=0)
    sc_ref[npast] = jnp.where(lane < tq, scores(ki_new), -jnp.inf)

    n_adm = jnp.full((tq, 1), float(p_len + tq), F32)
    thr = _topk_threshold(sc_ref, nkb, tq, n_sel, n_adm)

    qb = qb_ref[...] * (B_HD ** -0.5)
    masks = [_lane_mask(B_WIDTH, h * B_HD, (h + 1) * B_HD) for h in range(B_HEADS)]
    qs = jnp.concatenate([jnp.where(masks[h], qb, 0.0) for h in range(B_HEADS)], axis=0).astype(BF16)
    rows = B_HEADS * tq

    def att(carry, kblk, vblk, sel):
        m, l, acc = carry
        sel4 = jnp.concatenate([sel] * B_HEADS, axis=0)
        s = jnp.where(sel4, _dot_nt(qs, kblk.astype(BF16)), NEG_BIG)
        m_new = jnp.maximum(m, jnp.max(s, axis=-1, keepdims=True))
        alpha = jnp.exp(m - m_new)
        p = jnp.exp(s - m_new)
        l = alpha * l + jnp.sum(p, axis=-1, keepdims=True)
        acc = alpha * acc + _dot(p.astype(BF16), vblk.astype(BF16))
        return m_new, l, acc

    def att_blk(kb, carry):
        k0 = pl.multiple_of(kb * KEY_BLOCK, KEY_BLOCK)
        return att(carry, ck_ref[pl.ds(k0, KEY_BLOCK), :], cv_ref[pl.ds(k0, KEY_BLOCK), :], sc_ref[kb] > thr)

    init = (jnp.full((rows, 1), NEG_BIG, F32), jnp.zeros((rows, 1), F32), jnp.zeros((rows, B_WIDTH), F32))
    carry = lax.fori_loop(0, npast, att_blk, init)
    zpad = jnp.zeros((pad_rows, B_WIDTH), F32)
    _, l, acc = att(carry, jnp.concatenate([kn_ref[...], zpad], axis=0),
                    jnp.concatenate([vn_ref[...], zpad], axis=0), sc_ref[npast] > thr)
    res = acc / l
    out = jnp.zeros((tq, B_WIDTH), F32)
    for h in range(B_HEADS):
        out = out + jnp.where(masks[h], res[h * tq:(h + 1) * tq, :], 0.0)
    o_ref[...] = out * _silu(zb_ref[...])


def _dsa_decode(qb, qi, misc, zb, kb, vb, ck, cv, cki, *, bsz, t_len, p_len, n_sel):
    qmap = lambda b: (b, 0)
    cmap = lambda b: (b, 0, 0)
    return pl.pallas_call(
        functools.partial(_dsa_dec_body, p_len=p_len, tq=t_len, n_sel=n_sel),
        grid=(bsz,),
        in_specs=[pl.BlockSpec((t_len, B_WIDTH), qmap),
                  pl.BlockSpec((t_len, IDX_HEADS * IDX_HD), qmap),
                  pl.BlockSpec((t_len, MISC_W), qmap),
                  pl.BlockSpec((t_len, B_WIDTH), qmap),
                  pl.BlockSpec((t_len, B_WIDTH), qmap),
                  pl.BlockSpec((t_len, B_WIDTH), qmap),
                  pl.BlockSpec((None, p_len, B_WIDTH), cmap),
                  pl.BlockSpec((None, p_len, B_WIDTH), cmap),
                  pl.BlockSpec((None, p_len, IDX_HD), cmap)],
        out_specs=pl.BlockSpec((t_len, B_WIDTH), qmap),
        out_shape=jax.ShapeDtypeStruct((bsz * t_len, B_WIDTH), F32),
        scratch_shapes=[pltpu.VMEM((p_len // KEY_BLOCK + 1, t_len, KEY_BLOCK), F32)],
        compiler_params=_cparams(("parallel",)),
        name="dsa_decode",
    )(qb, qi, misc, zb, kb, vb, ck, cv, cki)


def _mem_body(qm_ref, zm_ref, mk_ref, mv_ref, o_ref):
    qm = qm_ref[...] * (M_HD ** -0.5)
    mk = mk_ref[...].astype(BF16)
    mv = mv_ref[...].astype(BF16)
    out = jnp.zeros(qm.shape, F32)
    for h in range(M_HEADS):
        mask = _lane_mask(M_WIDTH, h * M_HD, (h + 1) * M_HD)
        s = _dot_nt(jnp.where(mask, qm, 0.0).astype(BF16), mk)
        p = jnp.exp(s - jnp.max(s, axis=-1, keepdims=True))
        l = jnp.sum(p, axis=-1, keepdims=True)
        out = out + jnp.where(mask, _dot(p.astype(BF16), mv) / l, 0.0)
    o_ref[...] = out * _silu(zm_ref[...])


def _mem_attn(qm, zm, mk, mv, *, bsz, t_len, tq):
    nq = t_len // tq
    qmap = lambda b, j: (b * nq + j, 0)
    kmap = lambda b, j: (b, 0, 0)
    return pl.pallas_call(
        _mem_body,
        grid=(bsz, nq),
        in_specs=[pl.BlockSpec((tq, M_WIDTH), qmap),
                  pl.BlockSpec((tq, M_WIDTH), qmap),
                  pl.BlockSpec((None, N_MEM, M_WIDTH), kmap),
                  pl.BlockSpec((None, N_MEM, M_WIDTH), kmap)],
        out_specs=pl.BlockSpec((tq, M_WIDTH), qmap),
        out_shape=jax.ShapeDtypeStruct((bsz * t_len, M_WIDTH), F32),
        compiler_params=_cparams(("parallel", "parallel")),
        name="mem_attn",
    )(qm, zm, mk, mv)


def _oproj_body(x_ref, oa_ref, ob_ref, om_ref, w_ref, y_ref):
    acc = _dot(oa_ref[...].astype(BF16), w_ref[0:A_WIDTH, :])
    acc = acc + _dot(ob_ref[...].astype(BF16), w_ref[A_WIDTH:A_WIDTH + B_WIDTH, :])
    acc = acc + _dot(om_ref[...].astype(BF16), w_ref[A_WIDTH + B_WIDTH:, :])
    y_ref[...] = x_ref[...] + acc


def _oproj(x, oa, ob, om, w, tm):
    m, d = x.shape
    rmap = lambda i: (i, 0)
    return pl.pallas_call(
        _oproj_body,
        grid=(m // tm,),
        in_specs=[pl.BlockSpec((tm, d), rmap),
                  pl.BlockSpec((tm, A_WIDTH), rmap),
                  pl.BlockSpec((tm, B_WIDTH), rmap),
                  pl.BlockSpec((tm, M_WIDTH), rmap),
                  pl.BlockSpec(w.shape, lambda i: (0, 0))],
        out_specs=pl.BlockSpec((tm, d), rmap),
        out_shape=jax.ShapeDtypeStruct((m, d), F32),
        compiler_params=_cparams(("parallel",)),
        name="oproj",
    )(x, oa, ob, om, w)


def _prep_weights(g_in, w_in, conv_w, a_log, dt_bias, g_o, g_qb, g_kb, g_ki, g_qm, w_out):
    offs = np.concatenate([[0], np.cumsum(IN_SPLITS)])
    col = lambda i: w_in[:, int(offs[i]):int(offs[i + 1])]
    (w_qkv, w_za, w_ba, w_aa, w_qb, w_kb, w_vb, w_zb, w_qi, w_ki, w_wi, w_qm, w_zm) = [col(i) for i in range(13)]
    d = w_in.shape[0]
    w_misc = jnp.concatenate(
        [w_ki, w_wi, w_ba, w_aa, jnp.zeros((d, MISC_W - MISC_A - A_HEADS), w_in.dtype)], axis=1)
    w_ki8 = jnp.tile(w_ki, (1, IDX_HEADS))
    w_all = jnp.concatenate([w_qkv, w_za, w_qb, w_kb, w_vb, w_zb, w_qi, w_qm, w_zm, w_misc, w_ki8],
                            axis=1).astype(BF16)
    groups = [(3 * A_WIDTH, False), (A_WIDTH, False), (B_WIDTH, True), (B_WIDTH, True), (B_WIDTH, False),
              (B_WIDTH, False), (IDX_HEADS * IDX_HD, False), (M_WIDTH, True), (M_WIDTH, False),
              (MISC_W, True), (IDX_HEADS * IDX_HD, True)]
    bd64 = _group_mean_matrix(B_WIDTH, B_HD, B_WIDTH)
    ones256 = jnp.ones((1, B_WIDTH), F32)
    misc_gain = jnp.concatenate([g_ki, jnp.full((IDX_HEADS,), IDX_SCALE, F32),
                                 jnp.ones((MISC_W - MISC_B,), F32)])[None, :]
    misc_nm = (jnp.arange(MISC_W) < IDX_HD).astype(F32)[None, :]
    aux = [bd64, jnp.tile(g_qb, B_HEADS)[None, :], ones256,
           bd64, jnp.tile(g_kb, B_HEADS)[None, :], ones256,
           bd64, jnp.tile(g_qm, M_HEADS)[None, :], ones256,
           _group_mean_matrix(MISC_W, IDX_HD, IDX_HD), misc_gain, misc_nm,
           _group_mean_matrix(IDX_HEADS * IDX_HD, IDX_HD, IDX_HEADS * IDX_HD),
           jnp.tile(g_ki, IDX_HEADS)[None, :], ones256]
    pad_a = lambda v: jnp.zeros((1, MISC_W), F32).at[0, MISC_A:MISC_A + A_HEADS].set(v)
    return dict(g_in=g_in[None, :], w_all=w_all, groups=groups, aux=aux, conv_w=conv_w,
                avec=pad_a(-jnp.exp(a_log)), dtvec=pad_a(dt_bias), g_o=g_o[None, :],
                w_out=w_out.astype(BF16))


def _layer(x, mem_k, mem_v, conv_buf, s0, past, gdn_chunk, gdn_rows, wp):
    bsz, t_len, d = x.shape
    m = bsz * t_len
    x2 = x.reshape(m, d)
    tm = min(256, m)
    (qkv, za, qb, kb, vb, zb, qi, qm, zm, misc, ki8) = _proj(x2, wp["g_in"], wp["w_all"], wp["groups"],
                                                           wp["aux"], tm)
    cbuf8 = jnp.concatenate([jnp.zeros((bsz, 8 - (CONV_W - 1), 3 * A_WIDTH), F32), conv_buf], axis=1)
    oa, conv_new, s_new = _gdn(qkv, za, misc, cbuf8, s0, wp["conv_w"], wp["avec"], wp["dtvec"], wp["g_o"],
                               bsz=bsz, t_len=t_len, chunk=gdn_chunk, rows=gdn_rows)
    if past is None:
        n_sel = min(TOPK_MAX, t_len // 4)
        ob = _dsa_prompt(qb, qi, misc, zb, kb, vb, ki8, bsz=bsz, t_len=t_len, n_sel=n_sel)
    else:
        ck, cv, cki = past
        p_len = ck.shape[1]
        assert (p_len + t_len - 1) // CHUNK <= p_len // CHUNK and p_len % KEY_BLOCK == 0
        n_sel = min(TOPK_MAX, (p_len + t_len) // 4)
        ob = _dsa_decode(qb, qi, misc, zb, kb, vb, ck.reshape(bsz, p_len, B_WIDTH),
                         cv.reshape(bsz, p_len, B_WIDTH), cki, bsz=bsz, t_len=t_len, p_len=p_len, n_sel=n_sel)
    om = _mem_attn(qm, zm, mem_k, mem_v, bsz=bsz, t_len=t_len, tq=min(256, t_len))
    y = _oproj(x2, oa, ob, om, wp["w_out"], min(512, m))
    return (y.reshape(bsz, t_len, d), conv_new, s_new,
            kb.reshape(bsz, t_len, B_HEADS, B_HD), vb.reshape(bsz, t_len, B_HEADS, B_HD),
            misc[:, :IDX_HD].reshape(bsz, t_len, IDX_HD))


def _memory_kv(mem, g_mem, w_mem_kv, g_km):
    bsz, n_mem, d = mem.shape
    groups = [(M_WIDTH, True), (M_WIDTH, False)]
    aux = [_group_mean_matrix(M_WIDTH, M_HD, M_WIDTH), jnp.tile(g_km, M_HEADS)[None, :],
           jnp.ones((1, M_WIDTH), F32)]
    mk, mv = _proj(mem.reshape(bsz * n_mem, d), g_mem[None, :], w_mem_kv.astype(BF16), groups, aux, 256)
    return mk.reshape(bsz, n_mem, M_WIDTH), mv.reshape(bsz, n_mem, M_WIDTH)


def kernel(x_prompt, x_sample, state_conv_A, state_ssm_A, cache_k_B, cache_v_B, cache_kidx_B, cache_mem_k,
           cache_mem_v, mem_prompt, g_in, w_in, conv_w_A, a_log_A, dt_bias_A, g_o_A, g_q_B, g_k_B, g_kidx_B,
           g_mem, w_mem_kv, g_q_M, g_k_M, w_out):
    depth = w_in.shape[0]
    assert depth == 1
    l = 0
    bp, t_p, _ = x_prompt.shape
    bs, t_s, _ = x_sample.shape
    wp = _prep_weights(g_in[l], w_in[l], conv_w_A[l], a_log_A[l], dt_bias_A[l], g_o_A[l], g_q_B[l], g_k_B[l],
                       g_kidx_B[l], g_q_M[l], w_out[l])
    mk, mv = _memory_kv(mem_prompt, g_mem[l], w_mem_kv[l], g_k_M[l])
    zero_conv = jnp.zeros((bp, CONV_W - 1, 3 * A_WIDTH), F32)
    zero_ssm = jnp.zeros((bp, A_HEADS, A_DK, A_DV), F32)
    yp, c1, s1, k1, v1, ki1 = _layer(x_prompt, mk, mv, zero_conv, zero_ssm, None, CHUNK, 4 * CHUNK, wp)
    ys, c2, s2, k2, v2, ki2 = _layer(
        x_sample, cache_mem_k[l].reshape(bs, N_MEM, M_WIDTH), cache_mem_v[l].reshape(bs, N_MEM, M_WIDTH),
        state_conv_A[l], state_ssm_A[l], (cache_k_B[l], cache_v_B[l], cache_kidx_B[l]), t_s, t_s, wp)
    st = lambda a: a[None]
    return (yp, ys, st(c1), st(s1), st(k1), st(v1), st(ki1),
            st(mk.reshape(bp, N_MEM, M_HEADS, M_HD)), st(mv.reshape(bp, N_MEM, M_HEADS, M_HD)),
            st(c2), st(s2), st(k2), st(v2), st(ki2))
```

```python
import functools
import math

import numpy as np
import jax
import jax.numpy as jnp
from jax import lax
from jax.experimental import pallas as pl
from jax.experimental.pallas import tpu as pltpu

F32 = jnp.float32
BF16 = jnp.bfloat16
HIGHEST = lax.Precision.HIGHEST

D_MODEL = 1024
CHUNK = 64
A_HEADS = 4
A_DK = 128
A_DV = 128
A_WIDTH = A_HEADS * A_DV
CONV_W = 4
B_HEADS = 4
B_HD = 64
B_WIDTH = B_HEADS * B_HD
IDX_HEADS = 8
IDX_HD = 32
IDX_SCALE = (IDX_HEADS ** -0.5) * (IDX_HD ** -0.5)
TOPK_MAX = 256
Q_BLOCK = 128
N_MEM = 256
M_HEADS = 4
M_HD = 64
M_WIDTH = M_HEADS * M_HD
EPS = 1e-6
IN_SPLITS = (3 * A_WIDTH, A_WIDTH, A_HEADS, A_HEADS,
             B_WIDTH, B_WIDTH, B_WIDTH, B_WIDTH, IDX_HEADS * IDX_HD, IDX_HD, IDX_HEADS,
             M_WIDTH, M_WIDTH)

LANES = 128
KEY_BLOCK = 256
DSA_QBLK = 256
MISC_W = LANES
MISC_WI = IDX_HD
MISC_B = IDX_HD + IDX_HEADS
MISC_A = MISC_B + A_HEADS
VMEM_LIMIT = 48 * 1024 * 1024
NEG_BIG = -1e30
BISECT_PROBES = 3
BISECT_MAX_ITERS = 20


def _cparams(sem):
    return pltpu.CompilerParams(dimension_semantics=sem, vmem_limit_bytes=VMEM_LIMIT)


def _dot(a, b):
    return jnp.dot(a, b, preferred_element_type=F32)


def _dot_nt(a, b, precision=None):
    return lax.dot_general(a, b, (((1,), (1,)), ((), ())), precision=precision,
                           preferred_element_type=F32)


def _split2(x):
    hi = x.astype(BF16)
    return hi, (x - hi.astype(F32)).astype(BF16)


def _mm_split(a, b):
    ah, al = a
    bh, bl = b
    n = ah.shape[0]
    t = _dot(jnp.concatenate([ah, al], axis=0), bh)
    return t[:n] + t[n:] + _dot(ah, bl)


def _silu(x):
    return x * jax.nn.sigmoid(x)


def _lane_mask(width, lo, hi):
    lane = lax.broadcasted_iota(jnp.int32, (1, width), 1)
    return (lane >= lo) & (lane < hi)


def _proj_body(*refs, groups):
    x_ref, g_ref, w_ref = refs[0], refs[1], refs[2]
    n_aux = 3 * sum(1 for _, normed, _ in groups if normed)
    aux = refs[3:3 + n_aux]
    outs = refs[3 + n_aux:3 + n_aux + len(groups)]
    outs_t = list(refs[3 + n_aux + len(groups):])
    x = x_ref[...]
    h = x * lax.rsqrt(jnp.mean(x * x, axis=-1, keepdims=True) + EPS) * g_ref[...]
    hb = h.astype(BF16)
    off = 0
    ai = 0
    for (width, normed, transposed), o_ref in zip(groups, outs):
        y = _dot(hb, w_ref[:, off:off + width])
        if normed:
            bd_ref, gain_ref, nm_ref = aux[ai], aux[ai + 1], aux[ai + 2]
            ai += 3
            sq = y * y
            hi = sq.astype(BF16)
            lo = (sq - hi.astype(F32)).astype(BF16)
            ms = _dot(hi, bd_ref[...]) + _dot(lo, bd_ref[...])
            scale = jnp.where(nm_ref[...] > 0.0, lax.rsqrt(ms + EPS), 1.0)
            y = y * scale * gain_ref[...]
        o_ref[...] = y
        if transposed:
            outs_t.pop(0)[0] = y.T
        off += width


def _group_mean_matrix(width, group, n_lanes):
    i = np.arange(width)
    m = ((i[:, None] // group) == (i[None, :] // group)) & (i[:, None] < n_lanes) & (i[None, :] < n_lanes)
    return jnp.asarray(m.astype(np.float32) / group, dtype=BF16)


def _proj(x, g, w, groups, aux, tm):
    m, d = x.shape
    nw = w.shape[1]
    in_specs = [pl.BlockSpec((tm, d), lambda i: (i, 0)),
                pl.BlockSpec((1, d), lambda i: (0, 0)),
                pl.BlockSpec((d, nw), lambda i: (0, 0))]
    for a in aux:
        in_specs.append(pl.BlockSpec(a.shape, lambda i: (0, 0)))
    out_shape = [jax.ShapeDtypeStruct((m, wd), F32) for wd, _, _ in groups]
    out_specs = [pl.BlockSpec((tm, wd), lambda i: (i, 0)) for wd, _, _ in groups]
    for wd, _, transposed in groups:
        if transposed:
            out_shape.append(jax.ShapeDtypeStruct((m // tm, wd, tm), F32))
            out_specs.append(pl.BlockSpec((1, wd, tm), lambda i: (i, 0, 0)))
    return pl.pallas_call(
        functools.partial(_proj_body, groups=tuple(groups)),
        grid=(m // tm,),
        in_specs=in_specs, out_specs=out_specs, out_shape=out_shape,
        compiler_params=_cparams(("parallel",)),
        name="proj",
    )(x, g, w, *aux)


def _gdn_body(qkv_ref, za_ref, misc_ref, cbuf_ref, s0_ref, cw_ref, avec_ref, dtvec_ref, go_ref,
              o_ref, convnew_ref, sfin_ref,
              xp_ref, q_sc, k_sc, v_sc, g_sc, b_sc, s_sc, *, chunk, rows):
    t = pl.program_id(1)
    c = chunk

    @pl.when(t == 0)
    def _():
        xp_ref[0:8, :] = cbuf_ref[0]
        s_sc[...] = s0_ref[0]

    xp_ref[8:8 + rows, :] = qkv_ref[...]
    cw = cw_ref[...]
    y = xp_ref[5:5 + rows, :] * cw[0:1, :]
    for j in range(1, CONV_W):
        y = y + xp_ref[5 + j:5 + j + rows, :] * cw[j:j + 1, :]
    y = _silu(y)
    convnew_ref[0] = xp_ref[rows + 5:rows + 8, :]
    xp_ref[0:8, :] = xp_ref[rows:rows + 8, :]

    for h in range(A_HEADS):
        qh = y[:, h * A_DK:(h + 1) * A_DK]
        kh = y[:, A_WIDTH + h * A_DK:A_WIDTH + (h + 1) * A_DK]
        q_sc[:, h * A_DK:(h + 1) * A_DK] = qh * (
            lax.rsqrt(jnp.sum(qh * qh, axis=-1, keepdims=True) + EPS) * (A_DK ** -0.5))
        k_sc[:, h * A_DK:(h + 1) * A_DK] = kh * lax.rsqrt(jnp.sum(kh * kh, axis=-1, keepdims=True) + EPS)
    v_sc[...] = y[:, 2 * A_WIDTH:3 * A_WIDTH]
    misc = misc_ref[...]
    b_sc[...] = jax.nn.sigmoid(misc)
    g_sc[...] = avec_ref[...] * jax.nn.softplus(misc + dtvec_ref[...])

    ri = lax.broadcasted_iota(jnp.int32, (c, c), 0)
    ci = lax.broadcasted_iota(jnp.int32, (c, c), 1)
    tri = ri >= ci
    strict = ri > ci
    eye_f = (ri == ci).astype(F32)
    tri3 = (lax.broadcasted_iota(jnp.int32, (c, 3 * c), 0)
            >= lax.broadcasted_iota(jnp.int32, (c, 3 * c), 1) % c).astype(BF16)
    go = go_ref[...]
    n_chunks = rows // c

    problems = [(ic, h) for ic in range(n_chunks) for h in range(A_HEADS)]
    gc_all, gc_t = [], []
    for ic in range(n_chunks):
        g = g_sc[ic * c:(ic + 1) * c, :]
        g1 = g.astype(BF16)
        r1 = g - g1.astype(F32)
        g2 = r1.astype(BF16)
        g3 = (r1 - g2.astype(F32)).astype(BF16)
        gc_all.append(_dot(tri3, jnp.concatenate([g1, g2, g3], axis=0)))
    for ic in range(n_chunks):
        gc_t.append(gc_all[ic].T)

    def load(ref, ic, h):
        return ref[ic * c:(ic + 1) * c, h * A_DK:(h + 1) * A_DK]

    col = lambda a, lane: a[:, lane:lane + 1]
    k_bf = [load(k_sc, ic, h).astype(BF16) for ic, h in problems]
    kb = [load(k_sc, ic, h) * col(b_sc[ic * c:(ic + 1) * c, :], MISC_B + h) for ic, h in problems]
    kk = [_dot_nt(kb[i].astype(BF16), k_bf[i]) for i in range(len(problems))]
    qk_raw = [_dot_nt(load(q_sc, ic, h).astype(BF16), k_bf[i]) for i, (ic, h) in enumerate(problems)]
    decay = []
    for ic, h in problems:
        diff = col(gc_all[ic], MISC_A + h) - gc_t[ic][MISC_A + h:MISC_A + h + 1, :]
        decay.append(jnp.where(tri, jnp.exp(jnp.where(tri, diff, 0.0)), 0.0))
    lmat = [jnp.where(strict, kk[i] * decay[i], 0.0) for i in range(len(problems))]
    qk = [(qk_raw[i] * decay[i]).astype(BF16) for i in range(len(problems))]
    x_inv = [eye_f - m for m in lmat]
    p_split = [_split2(m) for m in lmat]
    for _ in range(int(math.log2(c)) - 1):
        p_split = [_split2(_mm_split(ps, ps)) for ps in p_split]
        x_inv = [x + _mm_split(_split2(x), ps) for x, ps in zip(x_inv, p_split)]
    sol = []
    for i, (ic, h) in enumerate(problems):
        egc = jnp.exp(col(gc_all[ic], MISC_A + h))
        beta = col(b_sc[ic * c:(ic + 1) * c, :], MISC_B + h)
        rhs = jnp.concatenate([load(v_sc, ic, h) * beta, kb[i] * egc], axis=-1).astype(BF16)
        th, tl = _split2(x_inv[i])
        sol.append(_dot(th, rhs) + _dot(tl, rhs))
    pre = []
    for i, (ic, h) in enumerate(problems):
        gc = col(gc_all[ic], MISC_A + h)
        g_last = gc[c - 1:c, :]
        k_dec_t = (load(k_sc, ic, h) * jnp.exp(g_last - gc)).T.astype(BF16)
        wq = jnp.concatenate([sol[i][:, A_DV:], load(q_sc, ic, h) * jnp.exp(gc)], axis=0).astype(BF16)
        pre.append((sol[i][:, :A_DV], wq, k_dec_t, jnp.exp(g_last)))

    heads = range(A_HEADS)
    s_cur = [s_sc[h] for h in heads]
    for ic in range(n_chunks):
        pr = [pre[ic * A_HEADS + h] for h in heads]
        ws = [_dot(pr[h][1], s_cur[h].astype(BF16)) for h in heads]
        ub = [(pr[h][0] - ws[h][:c]).astype(BF16) for h in heads]
        o = [ws[h][c:] + _dot(qk[ic * A_HEADS + h], ub[h]) for h in heads]
        s_cur = [s_cur[h] * pr[h][3] + _dot(pr[h][2], ub[h]) for h in heads]
        for h in heads:
            on = o[h] * lax.rsqrt(jnp.mean(o[h] * o[h], axis=-1, keepdims=True) + EPS) * go
            o_ref[ic * c:(ic + 1) * c, h * A_DV:(h + 1) * A_DV] = on * _silu(load(za_ref, ic, h))
    for h in heads:
        s_sc[h] = s_cur[h]
    sfin_ref[0] = s_sc[...]


def _gdn(qkv, za, misc, cbuf8, s0, conv_w, avec, dtvec, g_o, *, bsz, t_len, chunk, rows):
    nt = t_len // rows
    w3 = 3 * A_WIDTH
    row_map = lambda b, t: (b * nt + t, 0)
    const2 = lambda b, t: (0, 0)
    return pl.pallas_call(
        functools.partial(_gdn_body, chunk=chunk, rows=rows),
        grid=(bsz, nt),
        in_specs=[pl.BlockSpec((rows, w3), row_map),
                  pl.BlockSpec((rows, A_WIDTH), row_map),
                  pl.BlockSpec((rows, MISC_W), row_map),
                  pl.BlockSpec((1, 8, w3), lambda b, t: (b, 0, 0)),
                  pl.BlockSpec((1, A_HEADS, A_DK, A_DV), lambda b, t: (b, 0, 0, 0)),
                  pl.BlockSpec((CONV_W, w3), const2),
                  pl.BlockSpec((1, MISC_W), const2),
                  pl.BlockSpec((1, MISC_W), const2),
                  pl.BlockSpec((1, A_DV), const2)],
        out_specs=[pl.BlockSpec((rows, A_WIDTH), row_map),
                   pl.BlockSpec((1, CONV_W - 1, w3), lambda b, t: (b, 0, 0)),
                   pl.BlockSpec((1, A_HEADS, A_DK, A_DV), lambda b, t: (b, 0, 0, 0))],
        out_shape=[jax.ShapeDtypeStruct((bsz * t_len, A_WIDTH), F32),
                   jax.ShapeDtypeStruct((bsz, CONV_W - 1, w3), F32),
                   jax.ShapeDtypeStruct((bsz, A_HEADS, A_DK, A_DV), F32)],
        scratch_shapes=[pltpu.VMEM((rows + 8, w3), F32),
                        pltpu.VMEM((rows, A_WIDTH), F32),
                        pltpu.VMEM((rows, A_WIDTH), F32),
                        pltpu.VMEM((rows, A_WIDTH), F32),
                        pltpu.VMEM((rows, MISC_W), F32),
                        pltpu.VMEM((rows, MISC_W), F32),
                        pltpu.VMEM((A_HEADS, A_DK, A_DV), F32)],
        compiler_params=_cparams(("parallel", "arbitrary")),
        name="gdn",
    )(qkv, za, misc, cbuf8, s0, conv_w, avec, dtvec, g_o)


def _topk_threshold(sc_ref, nkb, shape, key_axis, n_sel, n_adm):
    kf = float(n_sel)
    kblk = shape[key_axis]

    if key_axis == 0:
        part_shape = (8, shape[1])

        def fold(x, combine):
            while x.shape[0] > 8:
                half = x.shape[0] // 2
                x = combine(x[:half], x[half:])
            return x
    else:
        part_shape = shape
        fold = lambda x, combine: x

    def reduce_blocks(fns, init, dtype, op, combine):
        def body(kb, accs):
            s = sc_ref[kb]
            return tuple(combine(acc, fold(fn(kb, s), combine)) for acc, fn in zip(accs, fns))
        accs = lax.fori_loop(0, nkb, body, tuple(jnp.full(part_shape, init, dtype) for _ in fns))
        return [op(acc, axis=key_axis, keepdims=True) for acc in accs]

    (vmax,) = reduce_blocks([lambda kb, s: s], -jnp.inf, F32, jnp.max, jnp.maximum)
    (vmin,) = reduce_blocks([lambda kb, s: jnp.where(s == -jnp.inf, jnp.inf, s)], jnp.inf, F32, jnp.min,
                            jnp.minimum)
    lo0 = vmin - jnp.maximum(1.0, jnp.abs(vmin))
    hi0 = vmax
    cnt_lo0 = n_adm
    cnt_hi0 = jnp.zeros_like(n_adm)

    def cond(st):
        it, go = st[0], st[1]
        return (it < BISECT_MAX_ITERS) & (go > 0)

    def body(st):
        it, _, lo, hi, cnt_lo, cnt_hi = st
        fracs = [(t + 1.0) / (BISECT_PROBES + 1.0) for t in range(BISECT_PROBES)]
        mids = [lo * (1.0 - f) + hi * f for f in fracs]
        cnts = reduce_blocks([(lambda kb, s, mid=mid: jnp.where(s > mid, 1.0, 0.0)) for mid in mids],
                             0.0, F32, jnp.sum, jnp.add)
        lo_n, cnt_lo_n, hi_n, cnt_hi_n = lo, cnt_lo, hi, cnt_hi
        for mid, cm in zip(mids, cnts):
            up = cm >= kf
            lo_n = jnp.where(up, mid, lo_n)
            cnt_lo_n = jnp.where(up, cm, cnt_lo_n)
        for mid, cm in zip(mids[::-1], cnts[::-1]):
            dn = cm < kf
            hi_n = jnp.where(dn, mid, hi_n)
            cnt_hi_n = jnp.where(dn, cm, cnt_hi_n)
        go = (jnp.max(cnt_lo_n) > kf).astype(jnp.int32)
        return it + 1, go, lo_n, hi_n, cnt_lo_n, cnt_hi_n

    go0 = (jnp.max(cnt_lo0) > kf).astype(jnp.int32)
    _, go, lo, hi, cnt_lo, cnt_hi = lax.while_loop(
        cond, body, (jnp.int32(0), go0, lo0, hi0, cnt_lo0, cnt_hi0))

    fix = cnt_lo > kf
    need0 = jnp.where(fix, kf - cnt_hi, 0.0)
    kpos = lax.broadcasted_iota(jnp.int32, shape, key_axis)
    big_idx = 2 ** 30

    def fix_cond(need):
        return jnp.max(need) > 0.0

    def fix_body(need):
        def in_cluster(s):
            return (s > lo) & (s <= hi)
        (mval,) = reduce_blocks([lambda kb, s: jnp.where(in_cluster(s), s, -jnp.inf)], -jnp.inf, F32,
                                jnp.max, jnp.maximum)
        (idx,) = reduce_blocks(
            [lambda kb, s: jnp.where(in_cluster(s) & (s == mval), kpos + kb * kblk, big_idx)],
            big_idx, jnp.int32, jnp.min, jnp.minimum)
        active = need > 0.0

        def promote(kb, carry):
            s = sc_ref[kb]
            sc_ref[kb] = jnp.where(active & ((kpos + kb * kblk) == idx), jnp.inf, s)
            return carry
        lax.fori_loop(0, nkb, promote, 0)
        return jnp.where(active, need - 1.0, need)

    lax.while_loop(fix_cond, fix_body, need0)
    return jnp.where(fix, hi, lo)


def _dsa_body(qb_ref, qi_ref, miscq_ref, zb_ref, k_ref, vt_ref, misck_ref, o_ref, sc_ref, s_ref, *, n_sel, qblk):
    j = pl.program_id(1)
    nkb = ((j + 1) * qblk + KEY_BLOCK - 1) // KEY_BLOCK
    qi = qi_ref[...]
    q2 = jnp.concatenate([qi[:, h * IDX_HD:(h + 1) * IDX_HD] for h in range(IDX_HEADS)],
                         axis=0).astype(BF16)
    w_t = miscq_ref[...].T[MISC_WI:MISC_WI + IDX_HEADS, :]
    qchunk = (j * qblk + lax.broadcasted_iota(jnp.int32, (1, qblk), 1)) // CHUNK
    krow = lax.broadcasted_iota(jnp.int32, (KEY_BLOCK, 1), 0)

    def score_blk(kb, carry):
        k0 = pl.multiple_of(kb * KEY_BLOCK, KEY_BLOCK)
        ki = misck_ref[pl.ds(k0, KEY_BLOCK), 0:IDX_HD].astype(BF16)
        lg = _dot_nt(ki, q2)
        acc = jnp.zeros((KEY_BLOCK, qblk), F32)
        for h in range(IDX_HEADS):
            acc = acc + w_t[h:h + 1, :] * jnp.maximum(lg[:, h * qblk:(h + 1) * qblk], 0.0)
        adm = ((k0 + krow) // CHUNK) <= qchunk
        sc_ref[kb] = jnp.where(adm, acc, -jnp.inf)
        return carry

    lax.fori_loop(0, nkb, score_blk, 0)

    n_adm = ((qchunk + 1) * CHUNK).astype(F32)
    thr = _topk_threshold(sc_ref, nkb, (KEY_BLOCK, qblk), 0, n_sel, n_adm)

    qb = qb_ref[...] * (B_HD ** -0.5)
    qs = jnp.concatenate([jnp.where(_lane_mask(B_WIDTH, h * B_HD, (h + 1) * B_HD), qb, 0.0)
                          for h in range(B_HEADS)], axis=0).astype(BF16)
    wide = B_HEADS * qblk

    def fold(x, combine):
        while x.shape[0] > 8:
            half = x.shape[0] // 2
            x = combine(x[:half], x[half:])
        return x

    def logits_blk(kb, m8):
        k0 = pl.multiple_of(kb * KEY_BLOCK, KEY_BLOCK)
        s = _dot_nt(k_ref[pl.ds(k0, KEY_BLOCK), :].astype(BF16), qs)
        sel = sc_ref[kb] > thr
        s = jnp.where(jnp.concatenate([sel] * B_HEADS, axis=1), s, NEG_BIG)
        s_ref[kb] = s
        return jnp.maximum(m8, fold(s, jnp.maximum))

    m8 = lax.fori_loop(0, nkb, logits_blk, jnp.full((8, wide), NEG_BIG, F32))
    m = jnp.max(m8, axis=0, keepdims=True)

    def pv_blk(kb, carry):
        l8, accs = carry
        p = jnp.exp(s_ref[kb] - m)
        pb = p.astype(BF16)
        vt = vt_ref[kb].astype(BF16)
        accs = tuple(accs[h] + _dot(vt[h * B_HD:(h + 1) * B_HD, :], pb[:, h * qblk:(h + 1) * qblk])
                     for h in range(B_HEADS))
        return l8 + fold(p, jnp.add), accs

    l8, accs = lax.fori_loop(
        0, nkb, pv_blk,
        (jnp.zeros((8, wide), F32), tuple(jnp.zeros((B_HD, qblk), F32) for _ in range(B_HEADS))))
    l = jnp.sum(l8, axis=0, keepdims=True)
    o_t = jnp.concatenate([accs[h] / l[:, h * qblk:(h + 1) * qblk] for h in range(B_HEADS)], axis=0)
    o_ref[...] = o_t.T * _silu(zb_ref[...])


def _dsa_prompt(qb, qi, misc, zb, kb, vt, *, bsz, t_len, n_sel, qblk):
    nq = t_len // qblk
    nkb = t_len // KEY_BLOCK
    qmap = lambda b, j: (b * nq + j, 0)
    kmap = lambda b, j: (b, 0)
    return pl.pallas_call(
        functools.partial(_dsa_body, n_sel=n_sel, qblk=qblk),
        grid=(bsz, nq),
        in_specs=[pl.BlockSpec((qblk, B_WIDTH), qmap),
                  pl.BlockSpec((qblk, IDX_HEADS * IDX_HD), qmap),
                  pl.BlockSpec((qblk, MISC_W), qmap),
                  pl.BlockSpec((qblk, B_WIDTH), qmap),
                  pl.BlockSpec((t_len, B_WIDTH), kmap),
                  pl.BlockSpec((nkb, B_WIDTH, KEY_BLOCK), lambda b, j: (b, 0, 0)),
                  pl.BlockSpec((t_len, MISC_W), kmap)],
        out_specs=pl.BlockSpec((qblk, B_WIDTH), qmap),
        out_shape=jax.ShapeDtypeStruct((bsz * t_len, B_WIDTH), F32),
        scratch_shapes=[pltpu.VMEM((nkb, KEY_BLOCK, qblk), F32),
                        pltpu.VMEM((nkb, KEY_BLOCK, B_HEADS * qblk), F32)],
        compiler_params=_cparams(("parallel", "arbitrary")),
        name="dsa_prompt",
    )(qb, qi, misc, zb, kb, vt, misc)


def _dsa_dec_body(qb_ref, qi_ref, misc_ref, zb_ref, kn_ref, vn_ref, ck_ref, cv_ref, cki_ref, o_ref, sc_ref,
                  *, p_len, tq, n_sel):
    npast = p_len // KEY_BLOCK
    nkb = npast + 1
    qi = qi_ref[...]
    misc = misc_ref[...]
    q2 = jnp.concatenate([qi[:, h * IDX_HD:(h + 1) * IDX_HD] for h in range(IDX_HEADS)], axis=0).astype(BF16)
    w_h = [misc[:, MISC_WI + h:MISC_WI + h + 1] for h in range(IDX_HEADS)]
    lane = lax.broadcasted_iota(jnp.int32, (1, KEY_BLOCK), 1)
    pad_rows = KEY_BLOCK - tq

    def scores(kblk):
        lg = _dot_nt(q2, kblk.astype(BF16))
        acc = jnp.zeros((tq, KEY_BLOCK), F32)
        for h in range(IDX_HEADS):
            acc = acc + w_h[h] * jnp.maximum(lg[h * tq:(h + 1) * tq, :], 0.0)
        return acc

    def score_blk(kb, carry):
        k0 = pl.multiple_of(kb * KEY_BLOCK, KEY_BLOCK)
        sc_ref[kb] = scores(cki_ref[pl.ds(k0, KEY_BLOCK), :])
        return carry

    lax.fori_loop(0, npast, score_blk, 0)
    ki_new = jnp.concatenate([misc[:, 0:IDX_HD], jnp.zeros((pad_rows, IDX_HD), F32)], axis=0)
    sc_ref[npast] = jnp.where(lane < tq, scores(ki_new), -jnp.inf)

    n_adm = jnp.full((tq, 1), float(p_len + tq), F32)
    thr = _topk_threshold(sc_ref, nkb, (tq, KEY_BLOCK), 1, n_sel, n_adm)

    qb = qb_ref[...] * (B_HD ** -0.5)
    masks = [_lane_mask(B_WIDTH, h * B_HD, (h + 1) * B_HD) for h in range(B_HEADS)]
    qs = jnp.concatenate([jnp.where(masks[h], qb, 0.0) for h in range(B_HEADS)], axis=0).astype(BF16)
    rows = B_HEADS * tq

    def att(carry, kblk, vblk, sel):
        m, l, acc = carry
        sel4 = jnp.concatenate([sel] * B_HEADS, axis=0)
        s = jnp.where(sel4, _dot_nt(qs, kblk.astype(BF16)), NEG_BIG)
        m_new = jnp.maximum(m, jnp.max(s, axis=-1, keepdims=True))
        alpha = jnp.exp(m - m_new)
        p = jnp.exp(s - m_new)
        l = alpha * l + jnp.sum(p, axis=-1, keepdims=True)
        acc = alpha * acc + _dot(p.astype(BF16), vblk.astype(BF16))
        return m_new, l, acc

    def att_blk(kb, carry):
        k0 = pl.multiple_of(kb * KEY_BLOCK, KEY_BLOCK)
        return att(carry, ck_ref[pl.ds(k0, KEY_BLOCK), :], cv_ref[pl.ds(k0, KEY_BLOCK), :], sc_ref[kb] > thr)

    init = (jnp.full((rows, 1), NEG_BIG, F32), jnp.zeros((rows, 1), F32), jnp.zeros((rows, B_WIDTH), F32))
    carry = lax.fori_loop(0, npast, att_blk, init)
    zpad = jnp.zeros((pad_rows, B_WIDTH), F32)
    _, l, acc = att(carry, jnp.concatenate([kn_ref[...], zpad], axis=0),
                    jnp.concatenate([vn_ref[...], zpad], axis=0), sc_ref[npast] > thr)
    res = acc / l
    out = jnp.zeros((tq, B_WIDTH), F32)
    for h in range(B_HEADS):
        out = out + jnp.where(masks[h], res[h * tq:(h + 1) * tq, :], 0.0)
    o_ref[...] = out * _silu(zb_ref[...])


def _dsa_decode(qb, qi, misc, zb, kb, vb, ck, cv, cki, *, bsz, t_len, p_len, n_sel):
    qmap = lambda b: (b, 0)
    cmap = lambda b: (b, 0, 0)
    return pl.pallas_call(
        functools.partial(_dsa_dec_body, p_len=p_len, tq=t_len, n_sel=n_sel),
        grid=(bsz,),
        in_specs=[pl.BlockSpec((t_len, B_WIDTH), qmap),
                  pl.BlockSpec((t_len, IDX_HEADS * IDX_HD), qmap),
                  pl.BlockSpec((t_len, MISC_W), qmap),
                  pl.BlockSpec((t_len, B_WIDTH), qmap),
                  pl.BlockSpec((t_len, B_WIDTH), qmap),
                  pl.BlockSpec((t_len, B_WIDTH), qmap),
                  pl.BlockSpec((None, p_len, B_WIDTH), cmap),
                  pl.BlockSpec((None, p_len, B_WIDTH), cmap),
                  pl.BlockSpec((None, p_len, IDX_HD), cmap)],
        out_specs=pl.BlockSpec((t_len, B_WIDTH), qmap),
        out_shape=jax.ShapeDtypeStruct((bsz * t_len, B_WIDTH), F32),
        scratch_shapes=[pltpu.VMEM((p_len // KEY_BLOCK + 1, t_len, KEY_BLOCK), F32)],
        compiler_params=_cparams(("parallel",)),
        name="dsa_decode",
    )(qb, qi, misc, zb, kb, vb, ck, cv, cki)


def _mem_body(qm_ref, zm_ref, mk_ref, mv_ref, o_ref):
    qm = qm_ref[...] * (M_HD ** -0.5)
    mk = mk_ref[...].astype(BF16)
    mv = mv_ref[...].astype(BF16)
    out = jnp.zeros(qm.shape, F32)
    for h in range(M_HEADS):
        mask = _lane_mask(M_WIDTH, h * M_HD, (h + 1) * M_HD)
        s = _dot_nt(jnp.where(mask, qm, 0.0).astype(BF16), mk)
        p = jnp.exp(s - jnp.max(s, axis=-1, keepdims=True))
        l = jnp.sum(p, axis=-1, keepdims=True)
        out = out + jnp.where(mask, _dot(p.astype(BF16), mv) / l, 0.0)
    o_ref[...] = out * _silu(zm_ref[...])


def _mem_attn(qm, zm, mk, mv, *, bsz, t_len, tq):
    nq = t_len // tq
    qmap = lambda b, j: (b * nq + j, 0)
    kmap = lambda b, j: (b, 0, 0)
    return pl.pallas_call(
        _mem_body,
        grid=(bsz, nq),
        in_specs=[pl.BlockSpec((tq, M_WIDTH), qmap),
                  pl.BlockSpec((tq, M_WIDTH), qmap),
                  pl.BlockSpec((None, N_MEM, M_WIDTH), kmap),
                  pl.BlockSpec((None, N_MEM, M_WIDTH), kmap)],
        out_specs=pl.BlockSpec((tq, M_WIDTH), qmap),
        out_shape=jax.ShapeDtypeStruct((bsz * t_len, M_WIDTH), F32),
        compiler_params=_cparams(("parallel", "parallel")),
        name="mem_attn",
    )(qm, zm, mk, mv)


def _oproj_body(x_ref, oa_ref, ob_ref, om_ref, w_ref, y_ref):
    acc = _dot(oa_ref[...].astype(BF16), w_ref[0:A_WIDTH, :])
    acc = acc + _dot(ob_ref[...].astype(BF16), w_ref[A_WIDTH:A_WIDTH + B_WIDTH, :])
    acc = acc + _dot(om_ref[...].astype(BF16), w_ref[A_WIDTH + B_WIDTH:, :])
    y_ref[...] = x_ref[...] + acc


def _oproj(x, oa, ob, om, w, tm):
    m, d = x.shape
    rmap = lambda i: (i, 0)
    return pl.pallas_call(
        _oproj_body,
        grid=(m // tm,),
        in_specs=[pl.BlockSpec((tm, d), rmap),
                  pl.BlockSpec((tm, A_WIDTH), rmap),
                  pl.BlockSpec((tm, B_WIDTH), rmap),
                  pl.BlockSpec((tm, M_WIDTH), rmap),
                  pl.BlockSpec(w.shape, lambda i: (0, 0))],
        out_specs=pl.BlockSpec((tm, d), rmap),
        out_shape=jax.ShapeDtypeStruct((m, d), F32),
        compiler_params=_cparams(("parallel",)),
        name="oproj",
    )(x, oa, ob, om, w)


def _prep_weights(g_in, w_in, conv_w, a_log, dt_bias, g_o, g_qb, g_kb, g_ki, g_qm, w_out):
    offs = np.concatenate([[0], np.cumsum(IN_SPLITS)])
    col = lambda i: w_in[:, int(offs[i]):int(offs[i + 1])]
    (w_qkv, w_za, w_ba, w_aa, w_qb, w_kb, w_vb, w_zb, w_qi, w_ki, w_wi, w_qm, w_zm) = [col(i) for i in range(13)]
    d = w_in.shape[0]
    w_misc = jnp.concatenate(
        [w_ki, w_wi, w_ba, w_aa, jnp.zeros((d, MISC_W - MISC_A - A_HEADS), w_in.dtype)], axis=1)
    w_all = jnp.concatenate([w_qkv, w_za, w_qb, w_kb, w_vb, w_zb, w_qi, w_qm, w_zm, w_misc],
                            axis=1).astype(BF16)
    groups = [(3 * A_WIDTH, False, False), (A_WIDTH, False, False), (B_WIDTH, True, False),
              (B_WIDTH, True, False), (B_WIDTH, False, True), (B_WIDTH, False, False),
              (IDX_HEADS * IDX_HD, False, False), (M_WIDTH, True, False), (M_WIDTH, False, False),
              (MISC_W, True, False)]
    bd64 = _group_mean_matrix(B_WIDTH, B_HD, B_WIDTH)
    ones256 = jnp.ones((1, B_WIDTH), F32)
    misc_gain = jnp.concatenate([g_ki, jnp.full((IDX_HEADS,), IDX_SCALE, F32),
                                 jnp.ones((MISC_W - MISC_B,), F32)])[None, :]
    misc_nm = (jnp.arange(MISC_W) < IDX_HD).astype(F32)[None, :]
    aux = [bd64, jnp.tile(g_qb, B_HEADS)[None, :], ones256,
           bd64, jnp.tile(g_kb, B_HEADS)[None, :], ones256,
           bd64, jnp.tile(g_qm, M_HEADS)[None, :], ones256,
           _group_mean_matrix(MISC_W, IDX_HD, IDX_HD), misc_gain, misc_nm]
    pad_a = lambda v: jnp.zeros((1, MISC_W), F32).at[0, MISC_A:MISC_A + A_HEADS].set(v)
    return dict(g_in=g_in[None, :], w_all=w_all, groups=groups, aux=aux, conv_w=conv_w,
                avec=pad_a(-jnp.exp(a_log)), dtvec=pad_a(dt_bias), g_o=g_o[None, :],
                w_out=w_out.astype(BF16))


def _layer(x, mem_k, mem_v, conv_buf, s0, past, gdn_chunk, gdn_rows, wp):
    bsz, t_len, d = x.shape
    m = bsz * t_len
    x2 = x.reshape(m, d)
    tm = min(KEY_BLOCK, m)
    (qkv, za, qb, kb, vb, zb, qi, qm, zm, misc, vt) = _proj(x2, wp["g_in"], wp["w_all"], wp["groups"],
                                                          wp["aux"], tm)
    cbuf8 = jnp.concatenate([jnp.zeros((bsz, 8 - (CONV_W - 1), 3 * A_WIDTH), F32), conv_buf], axis=1)
    oa, conv_new, s_new = _gdn(qkv, za, misc, cbuf8, s0, wp["conv_w"], wp["avec"], wp["dtvec"], wp["g_o"],
                               bsz=bsz, t_len=t_len, chunk=gdn_chunk, rows=gdn_rows)
    if past is None:
        n_sel = min(TOPK_MAX, t_len // 4)
        ob = _dsa_prompt(qb, qi, misc, zb, kb, vt, bsz=bsz, t_len=t_len, n_sel=n_sel, qblk=DSA_QBLK)
    else:
        ck, cv, cki = past
        p_len = ck.shape[1]
        assert (p_len + t_len - 1) // CHUNK <= p_len // CHUNK and p_len % KEY_BLOCK == 0
        n_sel = min(TOPK_MAX, (p_len + t_len) // 4)
        ob = _dsa_decode(qb, qi, misc, zb, kb, vb, ck.reshape(bsz, p_len, B_WIDTH),
                         cv.reshape(bsz, p_len, B_WIDTH), cki, bsz=bsz, t_len=t_len, p_len=p_len, n_sel=n_sel)
    om = _mem_attn(qm, zm, mem_k, mem_v, bsz=bsz, t_len=t_len, tq=min(256, t_len))
    y = _oproj(x2, oa, ob, om, wp["w_out"], min(512, m))
    return (y.reshape(bsz, t_len, d), conv_new, s_new,
            kb.reshape(bsz, t_len, B_HEADS, B_HD), vb.reshape(bsz, t_len, B_HEADS, B_HD),
            misc[:, :IDX_HD].reshape(bsz, t_len, IDX_HD))


def _memory_kv(mem, g_mem, w_mem_kv, g_km):
    bsz, n_mem, d = mem.shape
    groups = [(M_WIDTH, True, False), (M_WIDTH, False, False)]
    aux = [_group_mean_matrix(M_WIDTH, M_HD, M_WIDTH), jnp.tile(g_km, M_HEADS)[None, :],
           jnp.ones((1, M_WIDTH), F32)]
    mk, mv = _proj(mem.reshape(bsz * n_mem, d), g_mem[None, :], w_mem_kv.astype(BF16), groups, aux, 256)
    return mk.reshape(bsz, n_mem, M_WIDTH), mv.reshape(bsz, n_mem, M_WIDTH)


def kernel(x_prompt, x_sample, state_conv_A, state_ssm_A, cache_k_B, cache_v_B, cache_kidx_B, cache_mem_k,
           cache_mem_v, mem_prompt, g_in, w_in, conv_w_A, a_log_A, dt_bias_A, g_o_A, g_q_B, g_k_B, g_kidx_B,
           g_mem, w_mem_kv, g_q_M, g_k_M, w_out):
    depth = w_in.shape[0]
    assert depth == 1
    l = 0
    bp, t_p, _ = x_prompt.shape
    bs, t_s, _ = x_sample.shape
    wp = _prep_weights(g_in[l], w_in[l], conv_w_A[l], a_log_A[l], dt_bias_A[l], g_o_A[l], g_q_B[l], g_k_B[l],
                       g_kidx_B[l], g_q_M[l], w_out[l])
    mk, mv = _memory_kv(mem_prompt, g_mem[l], w_mem_kv[l], g_k_M[l])
    zero_conv = jnp.zeros((bp, CONV_W - 1, 3 * A_WIDTH), F32)
    zero_ssm = jnp.zeros((bp, A_HEADS, A_DK, A_DV), F32)
    yp, c1, s1, k1, v1, ki1 = _layer(x_prompt, mk, mv, zero_conv, zero_ssm, None, CHUNK, 4 * CHUNK, wp)
    ys, c2, s2, k2, v2, ki2 = _layer(
        x_sample, cache_mem_k[l].reshape(bs, N_MEM, M_WIDTH), cache_mem_v[l].reshape(bs, N_MEM, M_WIDTH),
        state_conv_A[l], state_ssm_A[l], (cache_k_B[l], cache_v_B[l], cache_kidx_B[l]), t_s, t_s, wp)
    st = lambda a: a[None]
    return (yp, ys, st(c1), st(s1), st(k1), st(v1), st(ki1),
            st(mk.reshape(bp, N_MEM, M_HEADS, M_HD)), st(mv.reshape(bp, N_MEM, M_HEADS, M_HD)),
            st(c2), st(s2), st(k2), st(v2), st(ki2))
```

```python
import functools
import math

import numpy as np
import jax
import jax.numpy as jnp
from jax import lax
from jax.experimental import pallas as pl
from jax.experimental.pallas import tpu as pltpu

F32 = jnp.float32
BF16 = jnp.bfloat16
HIGHEST = lax.Precision.HIGHEST

D_MODEL = 1024
CHUNK = 64
A_HEADS = 4
A_DK = 128
A_DV = 128
A_WIDTH = A_HEADS * A_DV
CONV_W = 4
B_HEADS = 4
B_HD = 64
B_WIDTH = B_HEADS * B_HD
IDX_HEADS = 8
IDX_HD = 32
IDX_SCALE = (IDX_HEADS ** -0.5) * (IDX_HD ** -0.5)
TOPK_MAX = 256
Q_BLOCK = 128
N_MEM = 256
M_HEADS = 4
M_HD = 64
M_WIDTH = M_HEADS * M_HD
EPS = 1e-6
IN_SPLITS = (3 * A_WIDTH, A_WIDTH, A_HEADS, A_HEADS,
             B_WIDTH, B_WIDTH, B_WIDTH, B_WIDTH, IDX_HEADS * IDX_HD, IDX_HD, IDX_HEADS,
             M_WIDTH, M_WIDTH)

LANES = 128
KEY_BLOCK = 256
DSA_QBLK = 256
MISC_W = LANES
MISC_WI = IDX_HD
MISC_B = IDX_HD + IDX_HEADS
MISC_A = MISC_B + A_HEADS
VMEM_LIMIT = 48 * 1024 * 1024
NEG_BIG = -1e30
BISECT_PROBES = 3
COUNT_RADIX = 1024
BISECT_MAX_ITERS = 20


def _cparams(sem):
    return pltpu.CompilerParams(dimension_semantics=sem, vmem_limit_bytes=VMEM_LIMIT)


def _dot(a, b):
    return jnp.dot(a, b, preferred_element_type=F32)


def _dot_nt(a, b, precision=None):
    return lax.dot_general(a, b, (((1,), (1,)), ((), ())), precision=precision,
                           preferred_element_type=F32)


def _split2(x):
    hi = x.astype(BF16)
    return hi, (x - hi.astype(F32)).astype(BF16)


def _mm_split(a, b):
    ah, al = a
    bh, bl = b
    n = ah.shape[0]
    t = _dot(jnp.concatenate([ah, al], axis=0), bh)
    return t[:n] + t[n:] + _dot(ah, bl)


def _silu(x):
    return x * jax.nn.sigmoid(x)


def _lane_mask(width, lo, hi):
    lane = lax.broadcasted_iota(jnp.int32, (1, width), 1)
    return (lane >= lo) & (lane < hi)


def _proj_body(*refs, groups):
    x_ref, g_ref, w_ref = refs[0], refs[1], refs[2]
    n_aux = 3 * sum(1 for _, normed, _ in groups if normed)
    aux = refs[3:3 + n_aux]
    outs = refs[3 + n_aux:3 + n_aux + len(groups)]
    outs_t = list(refs[3 + n_aux + len(groups):])
    x = x_ref[...]
    h = x * lax.rsqrt(jnp.mean(x * x, axis=-1, keepdims=True) + EPS) * g_ref[...]
    hb = h.astype(BF16)
    off = 0
    ai = 0
    for (width, normed, transposed), o_ref in zip(groups, outs):
        y = _dot(hb, w_ref[:, off:off + width])
        if normed:
            bd_ref, gain_ref, nm_ref = aux[ai], aux[ai + 1], aux[ai + 2]
            ai += 3
            sq = y * y
            hi = sq.astype(BF16)
            lo = (sq - hi.astype(F32)).astype(BF16)
            ms = _dot(hi, bd_ref[...]) + _dot(lo, bd_ref[...])
            scale = jnp.where(nm_ref[...] > 0.0, lax.rsqrt(ms + EPS), 1.0)
            y = y * scale * gain_ref[...]
        o_ref[...] = y
        if transposed:
            outs_t.pop(0)[0] = y.T
        off += width


def _group_mean_matrix(width, group, n_lanes):
    i = np.arange(width)
    m = ((i[:, None] // group) == (i[None, :] // group)) & (i[:, None] < n_lanes) & (i[None, :] < n_lanes)
    return jnp.asarray(m.astype(np.float32) / group, dtype=BF16)


def _proj(x, g, w, groups, aux, tm):
    m, d = x.shape
    nw = w.shape[1]
    in_specs = [pl.BlockSpec((tm, d), lambda i: (i, 0)),
                pl.BlockSpec((1, d), lambda i: (0, 0)),
                pl.BlockSpec((d, nw), lambda i: (0, 0))]
    for a in aux:
        in_specs.append(pl.BlockSpec(a.shape, lambda i: (0, 0)))
    out_shape = [jax.ShapeDtypeStruct((m, wd), F32) for wd, _, _ in groups]
    out_specs = [pl.BlockSpec((tm, wd), lambda i: (i, 0)) for wd, _, _ in groups]
    for wd, _, transposed in groups:
        if transposed:
            out_shape.append(jax.ShapeDtypeStruct((m // tm, wd, tm), F32))
            out_specs.append(pl.BlockSpec((1, wd, tm), lambda i: (i, 0, 0)))
    return pl.pallas_call(
        functools.partial(_proj_body, groups=tuple(groups)),
        grid=(m // tm,),
        in_specs=in_specs, out_specs=out_specs, out_shape=out_shape,
        compiler_params=_cparams(("parallel",)),
        name="proj",
    )(x, g, w, *aux)


def _gdn_body(qkv_ref, za_ref, misc_ref, cbuf_ref, s0_ref, cw_ref, avec_ref, dtvec_ref, go_ref,
              o_ref, convnew_ref, sfin_ref,
              xp_ref, q_sc, k_sc, v_sc, g_sc, b_sc, s_sc, *, chunk, rows):
    t = pl.program_id(1)
    c = chunk

    @pl.when(t == 0)
    def _():
        xp_ref[0:8, :] = cbuf_ref[0]
        s_sc[...] = s0_ref[0]

    xp_ref[8:8 + rows, :] = qkv_ref[...]
    cw = cw_ref[...]
    y = xp_ref[5:5 + rows, :] * cw[0:1, :]
    for j in range(1, CONV_W):
        y = y + xp_ref[5 + j:5 + j + rows, :] * cw[j:j + 1, :]
    y = _silu(y)
    convnew_ref[0] = xp_ref[rows + 5:rows + 8, :]
    xp_ref[0:8, :] = xp_ref[rows:rows + 8, :]

    for h in range(A_HEADS):
        qh = y[:, h * A_DK:(h + 1) * A_DK]
        kh = y[:, A_WIDTH + h * A_DK:A_WIDTH + (h + 1) * A_DK]
        q_sc[:, h * A_DK:(h + 1) * A_DK] = qh * (
            lax.rsqrt(jnp.sum(qh * qh, axis=-1, keepdims=True) + EPS) * (A_DK ** -0.5))
        k_sc[:, h * A_DK:(h + 1) * A_DK] = kh * lax.rsqrt(jnp.sum(kh * kh, axis=-1, keepdims=True) + EPS)
    v_sc[...] = y[:, 2 * A_WIDTH:3 * A_WIDTH]
    misc = misc_ref[...]
    b_sc[...] = jax.nn.sigmoid(misc)
    g_sc[...] = avec_ref[...] * jax.nn.softplus(misc + dtvec_ref[...])

    ri = lax.broadcasted_iota(jnp.int32, (c, c), 0)
    ci = lax.broadcasted_iota(jnp.int32, (c, c), 1)
    tri = ri >= ci
    strict = ri > ci
    eye_f = (ri == ci).astype(F32)
    tri3 = (lax.broadcasted_iota(jnp.int32, (c, 3 * c), 0)
            >= lax.broadcasted_iota(jnp.int32, (c, 3 * c), 1) % c).astype(BF16)
    go = go_ref[...]
    n_chunks = rows // c

    problems = [(ic, h) for ic in range(n_chunks) for h in range(A_HEADS)]
    gc_all, gc_t = [], []
    for ic in range(n_chunks):
        g = g_sc[ic * c:(ic + 1) * c, :]
        g1 = g.astype(BF16)
        r1 = g - g1.astype(F32)
        g2 = r1.astype(BF16)
        g3 = (r1 - g2.astype(F32)).astype(BF16)
        gc_all.append(_dot(tri3, jnp.concatenate([g1, g2, g3], axis=0)))
    for ic in range(n_chunks):
        gc_t.append(gc_all[ic].T)

    def load(ref, ic, h):
        return ref[ic * c:(ic + 1) * c, h * A_DK:(h + 1) * A_DK]

    col = lambda a, lane: a[:, lane:lane + 1]
    k_bf = [load(k_sc, ic, h).astype(BF16) for ic, h in problems]
    kb = [load(k_sc, ic, h) * col(b_sc[ic * c:(ic + 1) * c, :], MISC_B + h) for ic, h in problems]
    kk = [_dot_nt(kb[i].astype(BF16), k_bf[i]) for i in range(len(problems))]
    qk_raw = [_dot_nt(load(q_sc, ic, h).astype(BF16), k_bf[i]) for i, (ic, h) in enumerate(problems)]
    decay = []
    for ic, h in problems:
        diff = col(gc_all[ic], MISC_A + h) - gc_t[ic][MISC_A + h:MISC_A + h + 1, :]
        decay.append(jnp.where(tri, jnp.exp(jnp.where(tri, diff, 0.0)), 0.0))
    lmat = [jnp.where(strict, kk[i] * decay[i], 0.0) for i in range(len(problems))]
    qk = [(qk_raw[i] * decay[i]).astype(BF16) for i in range(len(problems))]
    x_inv = [eye_f - m for m in lmat]
    p_split = [_split2(m) for m in lmat]
    for _ in range(int(math.log2(c)) - 1):
        p_split = [_split2(_mm_split(ps, ps)) for ps in p_split]
        x_inv = [x + _mm_split(_split2(x), ps) for x, ps in zip(x_inv, p_split)]
    sol = []
    for i, (ic, h) in enumerate(problems):
        egc = jnp.exp(col(gc_all[ic], MISC_A + h))
        beta = col(b_sc[ic * c:(ic + 1) * c, :], MISC_B + h)
        rhs = jnp.concatenate([load(v_sc, ic, h) * beta, kb[i] * egc], axis=-1).astype(BF16)
        th, tl = _split2(x_inv[i])
        sol.append(_dot(th, rhs) + _dot(tl, rhs))
    pre = []
    for i, (ic, h) in enumerate(problems):
        gc = col(gc_all[ic], MISC_A + h)
        g_last = gc[c - 1:c, :]
        k_dec_t = (load(k_sc, ic, h) * jnp.exp(g_last - gc)).T.astype(BF16)
        wq = jnp.concatenate([sol[i][:, A_DV:], load(q_sc, ic, h) * jnp.exp(gc)], axis=0).astype(BF16)
        pre.append((sol[i][:, :A_DV], wq, k_dec_t, jnp.exp(g_last)))

    heads = range(A_HEADS)
    s_cur = [s_sc[h] for h in heads]
    for ic in range(n_chunks):
        pr = [pre[ic * A_HEADS + h] for h in heads]
        ws = [_dot(pr[h][1], s_cur[h].astype(BF16)) for h in heads]
        ub = [(pr[h][0] - ws[h][:c]).astype(BF16) for h in heads]
        o = [ws[h][c:] + _dot(qk[ic * A_HEADS + h], ub[h]) for h in heads]
        s_cur = [s_cur[h] * pr[h][3] + _dot(pr[h][2], ub[h]) for h in heads]
        for h in heads:
            on = o[h] * lax.rsqrt(jnp.mean(o[h] * o[h], axis=-1, keepdims=True) + EPS) * go
            o_ref[ic * c:(ic + 1) * c, h * A_DV:(h + 1) * A_DV] = on * _silu(load(za_ref, ic, h))
    for h in heads:
        s_sc[h] = s_cur[h]
    sfin_ref[0] = s_sc[...]


def _gdn(qkv, za, misc, cbuf8, s0, conv_w, avec, dtvec, g_o, *, bsz, t_len, chunk, rows):
    nt = t_len // rows
    w3 = 3 * A_WIDTH
    row_map = lambda b, t: (b * nt + t, 0)
    const2 = lambda b, t: (0, 0)
    return pl.pallas_call(
        functools.partial(_gdn_body, chunk=chunk, rows=rows),
        grid=(bsz, nt),
        in_specs=[pl.BlockSpec((rows, w3), row_map),
                  pl.BlockSpec((rows, A_WIDTH), row_map),
                  pl.BlockSpec((rows, MISC_W), row_map),
                  pl.BlockSpec((1, 8, w3), lambda b, t: (b, 0, 0)),
                  pl.BlockSpec((1, A_HEADS, A_DK, A_DV), lambda b, t: (b, 0, 0, 0)),
                  pl.BlockSpec((CONV_W, w3), const2),
                  pl.BlockSpec((1, MISC_W), const2),
                  pl.BlockSpec((1, MISC_W), const2),
                  pl.BlockSpec((1, A_DV), const2)],
        out_specs=[pl.BlockSpec((rows, A_WIDTH), row_map),
                   pl.BlockSpec((1, CONV_W - 1, w3), lambda b, t: (b, 0, 0)),
                   pl.BlockSpec((1, A_HEADS, A_DK, A_DV), lambda b, t: (b, 0, 0, 0))],
        out_shape=[jax.ShapeDtypeStruct((bsz * t_len, A_WIDTH), F32),
                   jax.ShapeDtypeStruct((bsz, CONV_W - 1, w3), F32),
                   jax.ShapeDtypeStruct((bsz, A_HEADS, A_DK, A_DV), F32)],
        scratch_shapes=[pltpu.VMEM((rows + 8, w3), F32),
                        pltpu.VMEM((rows, A_WIDTH), F32),
                        pltpu.VMEM((rows, A_WIDTH), F32),
                        pltpu.VMEM((rows, A_WIDTH), F32),
                        pltpu.VMEM((rows, MISC_W), F32),
                        pltpu.VMEM((rows, MISC_W), F32),
                        pltpu.VMEM((A_HEADS, A_DK, A_DV), F32)],
        compiler_params=_cparams(("parallel", "arbitrary")),
        name="gdn",
    )(qkv, za, misc, cbuf8, s0, conv_w, avec, dtvec, g_o)


def _topk_threshold(sc_ref, nkb, shape, key_axis, n_sel, n_adm):
    kf = float(n_sel)
    kblk = shape[key_axis]

    if key_axis == 0:
        strip = min(kblk, 32)
        part_shape = (strip, shape[1])
        strip_of = lambda kb, i: sc_ref[kb, i * strip:(i + 1) * strip, :]

        def put_strip(kb, i, v):
            sc_ref[kb, i * strip:(i + 1) * strip, :] = v
    else:
        strip = kblk
        part_shape = shape
        strip_of = lambda kb, i: sc_ref[kb]

        def put_strip(kb, i, v):
            sc_ref[kb] = v
    kpos = lax.broadcasted_iota(jnp.int32, part_shape, key_axis)

    def reduce_blocks(specs):
        def body(kb, accs):
            accs = list(accs)
            for i in range(kblk // strip):
                s = strip_of(kb, i)
                k0 = kb * kblk + i * strip
                accs = [cmb(acc, fn(s, k0)) for acc, (fn, _, _, _, cmb) in zip(accs, specs)]
            return tuple(accs)
        init = tuple(jnp.full(part_shape, i, dt) for _, i, dt, _, _ in specs)
        accs = lax.fori_loop(0, nkb, body, init)
        return [acc if op is None else op(acc, axis=key_axis, keepdims=True)
                for acc, (_, _, _, op, _) in zip(accs, specs)]

    one_if = lambda c: jnp.where(c, 1.0, 0.0)
    vmax, vmin, cpos, czero = reduce_blocks([
        (lambda s, k0: s, -jnp.inf, F32, jnp.max, jnp.maximum),
        (lambda s, k0: jnp.where(s == -jnp.inf, jnp.inf, s), jnp.inf, F32, jnp.min, jnp.minimum),
        (lambda s, k0: one_if(s > 0.0), 0.0, F32, jnp.sum, jnp.add),
        (lambda s, k0: one_if(s == 0.0), 0.0, F32, jnp.sum, jnp.add)])

    long_row = n_adm > kf
    pos_row = long_row & (cpos >= kf)
    zero_row = long_row & (cpos < kf) & (cpos + czero >= kf)
    neg_row = long_row & (cpos + czero < kf)
    lo0 = jnp.where(pos_row | zero_row, 0.0, vmin - jnp.maximum(1.0, jnp.abs(vmin)))
    cnt_lo0 = jnp.where(pos_row, cpos, jnp.where(zero_row, kf, n_adm))
    hi0 = jnp.where(neg_row, 0.0, vmax)
    cnt_hi0 = jnp.where(neg_row, cpos, 0.0)
    need_zero = jnp.where(zero_row, kf - cpos, 0.0)

    def cond(st):
        it, go = st[0], st[1]
        return (it < BISECT_MAX_ITERS) & (go > 0)

    def body(st):
        it, _, lo, hi, cnt_lo, cnt_hi = st
        fracs = [(t + 1.0) / (BISECT_PROBES + 1.0) for t in range(BISECT_PROBES)]
        mids = [lo * (1.0 - f) + hi * f for f in fracs]
        for t in range(1, BISECT_PROBES):
            mids[t] = jnp.maximum(mids[t], mids[t - 1])
        codes = [sum(COUNT_RADIX ** u for u in range(t + 1)) for t in range(BISECT_PROBES)]

        def encode(s, k0):
            e = jnp.zeros(s.shape, jnp.int32)
            for mid, code in zip(mids, codes):
                e = jnp.where(s > mid, code, e)
            return e
        (packed,) = reduce_blocks([(encode, 0, jnp.int32, None, jnp.add)])
        cnts = []
        for t in range(BISECT_PROBES):
            digit = lax.rem(packed, COUNT_RADIX) if t < BISECT_PROBES - 1 else packed
            packed = lax.div(packed, COUNT_RADIX)
            cnts.append(jnp.sum(digit.astype(F32), axis=key_axis, keepdims=True))
        lo_n, cnt_lo_n, hi_n, cnt_hi_n = lo, cnt_lo, hi, cnt_hi
        for mid, cm in zip(mids, cnts):
            up = cm >= kf
            lo_n = jnp.where(up, mid, lo_n)
            cnt_lo_n = jnp.where(up, cm, cnt_lo_n)
        for mid, cm in zip(mids[::-1], cnts[::-1]):
            dn = cm < kf
            hi_n = jnp.where(dn, mid, hi_n)
            cnt_hi_n = jnp.where(dn, cm, cnt_hi_n)
        go = (jnp.max(cnt_lo_n) > kf).astype(jnp.int32)
        return it + 1, go, lo_n, hi_n, cnt_lo_n, cnt_hi_n

    go0 = (jnp.max(cnt_lo0) > kf).astype(jnp.int32)
    _, go, lo, hi, cnt_lo, cnt_hi = lax.while_loop(
        cond, body, (jnp.int32(0), go0, lo0, hi0, cnt_lo0, cnt_hi0))

    fix = cnt_lo > kf
    need0 = jnp.where(fix, kf - cnt_hi, 0.0)
    big_idx = 2 ** 30

    def fix_cond(need):
        return jnp.max(need) > 0.0

    def fix_body(need):
        def in_cluster(s):
            return (s > lo) & (s <= hi)
        (mval,) = reduce_blocks([(lambda s, k0: jnp.where(in_cluster(s), s, -jnp.inf), -jnp.inf, F32,
                                  jnp.max, jnp.maximum)])
        (idx,) = reduce_blocks(
            [(lambda s, k0: jnp.where(in_cluster(s) & (s == mval), kpos + k0, big_idx),
              big_idx, jnp.int32, jnp.min, jnp.minimum)])
        active = need > 0.0

        def promote(kb, carry):
            for i in range(kblk // strip):
                s = strip_of(kb, i)
                put_strip(kb, i, jnp.where(active & ((kpos + kb * kblk + i * strip) == idx), jnp.inf, s))
            return carry
        lax.fori_loop(0, nkb, promote, 0)
        return jnp.where(active, need - 1.0, need)

    lax.while_loop(fix_cond, fix_body, need0)

    @pl.when(jnp.max(need_zero) > 0.0)
    def _():
        r = lax.broadcasted_iota(jnp.int32, (kblk, kblk), 0)
        c = lax.broadcasted_iota(jnp.int32, (kblk, kblk), 1)
        tri = ((r >= c) if key_axis == 0 else (r <= c)).astype(BF16)

        def tie_blk(kb, seen):
            s = sc_ref[kb]
            z = (s == 0.0) & zero_row
            zb = one_if(z).astype(BF16)
            if key_axis == 0:
                rank = _dot(tri, zb) + seen
                last = rank[kblk - 1:kblk, :]
            else:
                rank = _dot(zb, tri) + seen
                last = rank[:, kblk - 1:kblk]
            sc_ref[kb] = jnp.where(z & (rank <= need_zero), jnp.inf, s)
            return last
        lax.fori_loop(0, nkb, tie_blk, jnp.zeros_like(need_zero))

    return jnp.where(fix, hi, lo)


def _dsa_body(qb_ref, qi_ref, miscq_ref, zb_ref, k_ref, vt_ref, misck_ref, o_ref, sc_ref, s_ref, *, n_sel, qblk):
    j = pl.program_id(1)
    nkb = ((j + 1) * qblk + KEY_BLOCK - 1) // KEY_BLOCK
    qi = qi_ref[...]
    q2 = jnp.concatenate([qi[:, h * IDX_HD:(h + 1) * IDX_HD] for h in range(IDX_HEADS)],
                         axis=0).astype(BF16)
    w_t = miscq_ref[...].T[MISC_WI:MISC_WI + IDX_HEADS, :]
    qchunk = (j * qblk + lax.broadcasted_iota(jnp.int32, (1, qblk), 1)) // CHUNK
    krow = lax.broadcasted_iota(jnp.int32, (KEY_BLOCK, 1), 0)

    def score_blk(kb, carry):
        k0 = pl.multiple_of(kb * KEY_BLOCK, KEY_BLOCK)
        ki = misck_ref[pl.ds(k0, KEY_BLOCK), 0:IDX_HD].astype(BF16)
        lg = _dot_nt(ki, q2)
        acc = jnp.zeros((KEY_BLOCK, qblk), F32)
        for h in range(IDX_HEADS):
            acc = acc + w_t[h:h + 1, :] * jnp.maximum(lg[:, h * qblk:(h + 1) * qblk], 0.0)
        adm = ((k0 + krow) // CHUNK) <= qchunk
        sc_ref[kb] = jnp.where(adm, acc, -jnp.inf)
        return carry

    lax.fori_loop(0, nkb, score_blk, 0)

    n_adm = ((qchunk + 1) * CHUNK).astype(F32)
    thr = _topk_threshold(sc_ref, nkb, (KEY_BLOCK, qblk), 0, n_sel, n_adm)

    qb = qb_ref[...] * (B_HD ** -0.5)
    qs = jnp.concatenate([jnp.where(_lane_mask(B_WIDTH, h * B_HD, (h + 1) * B_HD), qb, 0.0)
                          for h in range(B_HEADS)], axis=0).astype(BF16)
    wide = B_HEADS * qblk

    def fold(acc, x, combine):
        for i in range(x.shape[0] // 8):
            acc = combine(acc, x[8 * i:8 * (i + 1)])
        return acc

    def logits_blk(kb, m8):
        k0 = pl.multiple_of(kb * KEY_BLOCK, KEY_BLOCK)
        s = _dot_nt(k_ref[pl.ds(k0, KEY_BLOCK), :].astype(BF16), qs)
        sel = sc_ref[kb] > thr
        s = jnp.where(jnp.concatenate([sel] * B_HEADS, axis=1), s, NEG_BIG)
        s_ref[kb] = s
        return fold(m8, s, jnp.maximum)

    m8 = lax.fori_loop(0, nkb, logits_blk, jnp.full((8, wide), NEG_BIG, F32))
    m = jnp.max(m8, axis=0, keepdims=True)

    def pv_blk(kb, carry):
        l8, accs = carry
        p = jnp.exp(s_ref[kb] - m)
        pb = p.astype(BF16)
        vt = vt_ref[kb].astype(BF16)
        accs = tuple(accs[h] + _dot(vt[h * B_HD:(h + 1) * B_HD, :], pb[:, h * qblk:(h + 1) * qblk])
                     for h in range(B_HEADS))
        return fold(l8, p, jnp.add), accs

    l8, accs = lax.fori_loop(
        0, nkb, pv_blk,
        (jnp.zeros((8, wide), F32), tuple(jnp.zeros((B_HD, qblk), F32) for _ in range(B_HEADS))))
    l = jnp.sum(l8, axis=0, keepdims=True)
    o_t = jnp.concatenate([accs[h] / l[:, h * qblk:(h + 1) * qblk] for h in range(B_HEADS)], axis=0)
    o_ref[...] = o_t.T * _silu(zb_ref[...])


def _dsa_prompt(qb, qi, misc, zb, kb, vt, *, bsz, t_len, n_sel, qblk):
    nq = t_len // qblk
    nkb = t_len // KEY_BLOCK
    assert nkb * (KEY_BLOCK // 8) < COUNT_RADIX
    qmap = lambda b, j: (b * nq + j, 0)
    kmap = lambda b, j: (b, 0)
    return pl.pallas_call(
        functools.partial(_dsa_body, n_sel=n_sel, qblk=qblk),
        grid=(bsz, nq),
        in_specs=[pl.BlockSpec((qblk, B_WIDTH), qmap),
                  pl.BlockSpec((qblk, IDX_HEADS * IDX_HD), qmap),
                  pl.BlockSpec((qblk, MISC_W), qmap),
                  pl.BlockSpec((qblk, B_WIDTH), qmap),
                  pl.BlockSpec((t_len, B_WIDTH), kmap),
                  pl.BlockSpec((nkb, B_WIDTH, KEY_BLOCK), lambda b, j: (b, 0, 0)),
                  pl.BlockSpec((t_len, MISC_W), kmap)],
        out_specs=pl.BlockSpec((qblk, B_WIDTH), qmap),
        out_shape=jax.ShapeDtypeStruct((bsz * t_len, B_WIDTH), F32),
        scratch_shapes=[pltpu.VMEM((nkb, KEY_BLOCK, qblk), F32),
                        pltpu.VMEM((nkb, KEY_BLOCK, B_HEADS * qblk), F32)],
        compiler_params=_cparams(("parallel", "arbitrary")),
        name="dsa_prompt",
    )(qb, qi, misc, zb, kb, vt, misc)


def _dsa_dec_body(qb_ref, qi_ref, misc_ref, zb_ref, kn_ref, vn_ref, ck_ref, cv_ref, cki_ref, o_ref, sc_ref,
                  *, p_len, tq, n_sel):
    npast = p_len // KEY_BLOCK
    nkb = npast + 1
    qi = qi_ref[...]
    misc = misc_ref[...]
    q2 = jnp.concatenate([qi[:, h * IDX_HD:(h + 1) * IDX_HD] for h in range(IDX_HEADS)], axis=0).astype(BF16)
    w_h = [misc[:, MISC_WI + h:MISC_WI + h + 1] for h in range(IDX_HEADS)]
    lane = lax.broadcasted_iota(jnp.int32, (1, KEY_BLOCK), 1)
    pad_rows = KEY_BLOCK - tq

    def scores(kblk):
        lg = _dot_nt(q2, kblk.astype(BF16))
        acc = jnp.zeros((tq, KEY_BLOCK), F32)
        for h in range(IDX_HEADS):
            acc = acc + w_h[h] * jnp.maximum(lg[h * tq:(h + 1) * tq, :], 0.0)
        return acc

    def score_blk(kb, carry):
        k0 = pl.multiple_of(kb * KEY_BLOCK, KEY_BLOCK)
        sc_ref[kb] = scores(cki_ref[pl.ds(k0, KEY_BLOCK), :])
        return carry

    lax.fori_loop(0, npast, score_blk, 0)
    ki_new = jnp.concatenate([misc[:, 0:IDX_HD], jnp.zeros((pad_rows, IDX_HD), F32)], axis=0)
    sc_ref[npast] = jnp.where(lane < tq, scores(ki_new), -jnp.inf)

    n_adm = jnp.full((tq, 1), float(p_len + tq), F32)
    thr = _topk_threshold(sc_ref, nkb, (tq, KEY_BLOCK), 1, n_sel, n_adm)

    qb = qb_ref[...] * (B_HD ** -0.5)
    masks = [_lane_mask(B_WIDTH, h * B_HD, (h + 1) * B_HD) for h in range(B_HEADS)]
    qs = jnp.concatenate([jnp.where(masks[h], qb, 0.0) for h in range(B_HEADS)], axis=0).astype(BF16)
    rows = B_HEADS * tq

    def att(carry, kblk, vblk, sel):
        m, l, acc = carry
        sel4 = jnp.concatenate([sel] * B_HEADS, axis=0)
        s = jnp.where(sel4, _dot_nt(qs, kblk.astype(BF16)), NEG_BIG)
        m_new = jnp.maximum(m, jnp.max(s, axis=-1, keepdims=True))
        alpha = jnp.exp(m - m_new)
        p = jnp.exp(s - m_new)
        l = alpha * l + jnp.sum(p, axis=-1, keepdims=True)
        acc = alpha * acc + _dot(p.astype(BF16), vblk.astype(BF16))
        return m_new, l, acc

    def att_blk(kb, carry):
        k0 = pl.multiple_of(kb * KEY_BLOCK, KEY_BLOCK)
        return att(carry, ck_ref[pl.ds(k0, KEY_BLOCK), :], cv_ref[pl.ds(k0, KEY_BLOCK), :], sc_ref[kb] > thr)

    init = (jnp.full((rows, 1), NEG_BIG, F32), jnp.zeros((rows, 1), F32), jnp.zeros((rows, B_WIDTH), F32))
    carry = lax.fori_loop(0, npast, att_blk, init)
    zpad = jnp.zeros((pad_rows, B_WIDTH), F32)
    _, l, acc = att(carry, jnp.concatenate([kn_ref[...], zpad], axis=0),
                    jnp.concatenate([vn_ref[...], zpad], axis=0), sc_ref[npast] > thr)
    res = acc / l
    out = jnp.zeros((tq, B_WIDTH), F32)
    for h in range(B_HEADS):
        out = out + jnp.where(masks[h], res[h * tq:(h + 1) * tq, :], 0.0)
    o_ref[...] = out * _silu(zb_ref[...])


def _dsa_decode(qb, qi, misc, zb, kb, vb, ck, cv, cki, *, bsz, t_len, p_len, n_sel):
    assert p_len // KEY_BLOCK + 1 < COUNT_RADIX
    qmap = lambda b: (b, 0)
    cmap = lambda b: (b, 0, 0)
    return pl.pallas_call(
        functools.partial(_dsa_dec_body, p_len=p_len, tq=t_len, n_sel=n_sel),
        grid=(bsz,),
        in_specs=[pl.BlockSpec((t_len, B_WIDTH), qmap),
                  pl.BlockSpec((t_len, IDX_HEADS * IDX_HD), qmap),
                  pl.BlockSpec((t_len, MISC_W), qmap),
                  pl.BlockSpec((t_len, B_WIDTH), qmap),
                  pl.BlockSpec((t_len, B_WIDTH), qmap),
                  pl.BlockSpec((t_len, B_WIDTH), qmap),
                  pl.BlockSpec((None, p_len, B_WIDTH), cmap),
                  pl.BlockSpec((None, p_len, B_WIDTH), cmap),
                  pl.BlockSpec((None, p_len, IDX_HD), cmap)],
        out_specs=pl.BlockSpec((t_len, B_WIDTH), qmap),
        out_shape=jax.ShapeDtypeStruct((bsz * t_len, B_WIDTH), F32),
        scratch_shapes=[pltpu.VMEM((p_len // KEY_BLOCK + 1, t_len, KEY_BLOCK), F32)],
        compiler_params=_cparams(("parallel",)),
        name="dsa_decode",
    )(qb, qi, misc, zb, kb, vb, ck, cv, cki)


def _mem_body(qm_ref, zm_ref, mk_ref, mv_ref, o_ref):
    qm = qm_ref[...] * (M_HD ** -0.5)
    mk = mk_ref[...].astype(BF16)
    mv = mv_ref[...].astype(BF16)
    out = jnp.zeros(qm.shape, F32)
    for h in range(M_HEADS):
        mask = _lane_mask(M_WIDTH, h * M_HD, (h + 1) * M_HD)
        s = _dot_nt(jnp.where(mask, qm, 0.0).astype(BF16), mk)
        p = jnp.exp(s - jnp.max(s, axis=-1, keepdims=True))
        l = jnp.sum(p, axis=-1, keepdims=True)
        out = out + jnp.where(mask, _dot(p.astype(BF16), mv) / l, 0.0)
    o_ref[...] = out * _silu(zm_ref[...])


def _mem_attn(qm, zm, mk, mv, *, bsz, t_len, tq):
    nq = t_len // tq
    qmap = lambda b, j: (b * nq + j, 0)
    kmap = lambda b, j: (b, 0, 0)
    return pl.pallas_call(
        _mem_body,
        grid=(bsz, nq),
        in_specs=[pl.BlockSpec((tq, M_WIDTH), qmap),
                  pl.BlockSpec((tq, M_WIDTH), qmap),
                  pl.BlockSpec((None, N_MEM, M_WIDTH), kmap),
                  pl.BlockSpec((None, N_MEM, M_WIDTH), kmap)],
        out_specs=pl.BlockSpec((tq, M_WIDTH), qmap),
        out_shape=jax.ShapeDtypeStruct((bsz * t_len, M_WIDTH), F32),
        compiler_params=_cparams(("parallel", "parallel")),
        name="mem_attn",
    )(qm, zm, mk, mv)


def _oproj_body(x_ref, oa_ref, ob_ref, om_ref, w_ref, y_ref):
    acc = _dot(oa_ref[...].astype(BF16), w_ref[0:A_WIDTH, :])
    acc = acc + _dot(ob_ref[...].astype(BF16), w_ref[A_WIDTH:A_WIDTH + B_WIDTH, :])
    acc = acc + _dot(om_ref[...].astype(BF16), w_ref[A_WIDTH + B_WIDTH:, :])
    y_ref[...] = x_ref[...] + acc


def _oproj(x, oa, ob, om, w, tm):
    m, d = x.shape
    rmap = lambda i: (i, 0)
    return pl.pallas_call(
        _oproj_body,
        grid=(m // tm,),
        in_specs=[pl.BlockSpec((tm, d), rmap),
                  pl.BlockSpec((tm, A_WIDTH), rmap),
                  pl.BlockSpec((tm, B_WIDTH), rmap),
                  pl.BlockSpec((tm, M_WIDTH), rmap),
                  pl.BlockSpec(w.shape, lambda i: (0, 0))],
        out_specs=pl.BlockSpec((tm, d), rmap),
        out_shape=jax.ShapeDtypeStruct((m, d), F32),
        compiler_params=_cparams(("parallel",)),
        name="oproj",
    )(x, oa, ob, om, w)


def _prep_weights(g_in, w_in, conv_w, a_log, dt_bias, g_o, g_qb, g_kb, g_ki, g_qm, w_out):
    offs = np.concatenate([[0], np.cumsum(IN_SPLITS)])
    col = lambda i: w_in[:, int(offs[i]):int(offs[i + 1])]
    (w_qkv, w_za, w_ba, w_aa, w_qb, w_kb, w_vb, w_zb, w_qi, w_ki, w_wi, w_qm, w_zm) = [col(i) for i in range(13)]
    d = w_in.shape[0]
    w_misc = jnp.concatenate(
        [w_ki, w_wi, w_ba, w_aa, jnp.zeros((d, MISC_W - MISC_A - A_HEADS), w_in.dtype)], axis=1)
    w_all = jnp.concatenate([w_qkv, w_za, w_qb, w_kb, w_vb, w_zb, w_qi, w_qm, w_zm, w_misc],
                            axis=1).astype(BF16)
    groups = [(3 * A_WIDTH, False, False), (A_WIDTH, False, False), (B_WIDTH, True, False),
              (B_WIDTH, True, False), (B_WIDTH, False, True), (B_WIDTH, False, False),
              (IDX_HEADS * IDX_HD, False, False), (M_WIDTH, True, False), (M_WIDTH, False, False),
              (MISC_W, True, False)]
    bd64 = _group_mean_matrix(B_WIDTH, B_HD, B_WIDTH)
    ones256 = jnp.ones((1, B_WIDTH), F32)
    misc_gain = jnp.concatenate([g_ki, jnp.full((IDX_HEADS,), IDX_SCALE, F32),
                                 jnp.ones((MISC_W - MISC_B,), F32)])[None, :]
    misc_nm = (jnp.arange(MISC_W) < IDX_HD).astype(F32)[None, :]
    aux = [bd64, jnp.tile(g_qb, B_HEADS)[None, :], ones256,
           bd64, jnp.tile(g_kb, B_HEADS)[None, :], ones256,
           bd64, jnp.tile(g_qm, M_HEADS)[None, :], ones256,
           _group_mean_matrix(MISC_W, IDX_HD, IDX_HD), misc_gain, misc_nm]
    pad_a = lambda v: jnp.zeros((1, MISC_W), F32).at[0, MISC_A:MISC_A + A_HEADS].set(v)
    return dict(g_in=g_in[None, :], w_all=w_all, groups=groups, aux=aux, conv_w=conv_w,
                avec=pad_a(-jnp.exp(a_log)), dtvec=pad_a(dt_bias), g_o=g_o[None, :],
                w_out=w_out.astype(BF16))


def _layer(x, mem_k, mem_v, conv_buf, s0, past, gdn_chunk, gdn_rows, wp):
    bsz, t_len, d = x.shape
    m = bsz * t_len
    x2 = x.reshape(m, d)
    tm = min(KEY_BLOCK, m)
    (qkv, za, qb, kb, vb, zb, qi, qm, zm, misc, vt) = _proj(x2, wp["g_in"], wp["w_all"], wp["groups"],
                                                          wp["aux"], tm)
    cbuf8 = jnp.concatenate([jnp.zeros((bsz, 8 - (CONV_W - 1), 3 * A_WIDTH), F32), conv_buf], axis=1)
    oa, conv_new, s_new = _gdn(qkv, za, misc, cbuf8, s0, wp["conv_w"], wp["avec"], wp["dtvec"], wp["g_o"],
                               bsz=bsz, t_len=t_len, chunk=gdn_chunk, rows=gdn_rows)
    if past is None:
        n_sel = min(TOPK_MAX, t_len // 4)
        ob = _dsa_prompt(qb, qi, misc, zb, kb, vt, bsz=bsz, t_len=t_len, n_sel=n_sel, qblk=DSA_QBLK)
    else:
        ck, cv, cki = past
        p_len = ck.shape[1]
        assert (p_len + t_len - 1) // CHUNK <= p_len // CHUNK and p_len % KEY_BLOCK == 0
        n_sel = min(TOPK_MAX, (p_len + t_len) // 4)
        ob = _dsa_decode(qb, qi, misc, zb, kb, vb, ck.reshape(bsz, p_len, B_WIDTH),
                         cv.reshape(bsz, p_len, B_WIDTH), cki, bsz=bsz, t_len=t_len, p_len=p_len, n_sel=n_sel)
    om = _mem_attn(qm, zm, mem_k, mem_v, bsz=bsz, t_len=t_len, tq=min(256, t_len))
    y = _oproj(x2, oa, ob, om, wp["w_out"], min(512, m))
    return (y.reshape(bsz, t_len, d), conv_new, s_new,
            kb.reshape(bsz, t_len, B_HEADS, B_HD), vb.reshape(bsz, t_len, B_HEADS, B_HD),
            misc[:, :IDX_HD].reshape(bsz, t_len, IDX_HD))


def _memory_kv(mem, g_mem, w_mem_kv, g_km):
    bsz, n_mem, d = mem.shape
    groups = [(M_WIDTH, True, False), (M_WIDTH, False, False)]
    aux = [_group_mean_matrix(M_WIDTH, M_HD, M_WIDTH), jnp.tile(g_km, M_HEADS)[None, :],
           jnp.ones((1, M_WIDTH), F32)]
    mk, mv = _proj(mem.reshape(bsz * n_mem, d), g_mem[None, :], w_mem_kv.astype(BF16), groups, aux, 256)
    return mk.reshape(bsz, n_mem, M_WIDTH), mv.reshape(bsz, n_mem, M_WIDTH)


def kernel(x_prompt, x_sample, state_conv_A, state_ssm_A, cache_k_B, cache_v_B, cache_kidx_B, cache_mem_k,
           cache_mem_v, mem_prompt, g_in, w_in, conv_w_A, a_log_A, dt_bias_A, g_o_A, g_q_B, g_k_B, g_kidx_B,
           g_mem, w_mem_kv, g_q_M, g_k_M, w_out):
    depth = w_in.shape[0]
    assert depth == 1
    l = 0
    bp, t_p, _ = x_prompt.shape
    bs, t_s, _ = x_sample.shape
    wp = _prep_weights(g_in[l], w_in[l], conv_w_A[l], a_log_A[l], dt_bias_A[l], g_o_A[l], g_q_B[l], g_k_B[l],
                       g_kidx_B[l], g_q_M[l], w_out[l])
    mk, mv = _memory_kv(mem_prompt, g_mem[l], w_mem_kv[l], g_k_M[l])
    zero_conv = jnp.zeros((bp, CONV_W - 1, 3 * A_WIDTH), F32)
    zero_ssm = jnp.zeros((bp, A_HEADS, A_DK, A_DV), F32)
    yp, c1, s1, k1, v1, ki1 = _layer(x_prompt, mk, mv, zero_conv, zero_ssm, None, CHUNK, 4 * CHUNK, wp)
    ys, c2, s2, k2, v2, ki2 = _layer(
        x_sample, cache_mem_k[l].reshape(bs, N_MEM, M_WIDTH), cache_mem_v[l].reshape(bs, N_MEM, M_WIDTH),
        state_conv_A[l], state_ssm_A[l], (cache_k_B[l], cache_v_B[l], cache_kidx_B[l]), t_s, t_s, wp)
    st = lambda a: a[None]
    return (yp, ys, st(c1), st(s1), st(k1), st(v1), st(ki1),
            st(mk.reshape(bp, N_MEM, M_HEADS, M_HD)), st(mv.reshape(bp, N_MEM, M_HEADS, M_HD)),
            st(c2), st(s2), st(k2), st(v2), st(ki2))
```

```python
import functools
import math

import numpy as np
import jax
import jax.numpy as jnp
from jax import lax
from jax.experimental import pallas as pl
from jax.experimental.pallas import tpu as pltpu

F32 = jnp.float32
BF16 = jnp.bfloat16
HIGHEST = lax.Precision.HIGHEST

D_MODEL = 1024
CHUNK = 64
A_HEADS = 4
A_DK = 128
A_DV = 128
A_WIDTH = A_HEADS * A_DV
CONV_W = 4
B_HEADS = 4
B_HD = 64
B_WIDTH = B_HEADS * B_HD
IDX_HEADS = 8
IDX_HD = 32
IDX_SCALE = (IDX_HEADS ** -0.5) * (IDX_HD ** -0.5)
TOPK_MAX = 256
Q_BLOCK = 128
N_MEM = 256
M_HEADS = 4
M_HD = 64
M_WIDTH = M_HEADS * M_HD
EPS = 1e-6
IN_SPLITS = (3 * A_WIDTH, A_WIDTH, A_HEADS, A_HEADS,
             B_WIDTH, B_WIDTH, B_WIDTH, B_WIDTH, IDX_HEADS * IDX_HD, IDX_HD, IDX_HEADS,
             M_WIDTH, M_WIDTH)

LANES = 128
KEY_BLOCK = 256
DSA_QBLK = 256
SCORE_STRIP = 128
MISC_W = LANES
MISC_WI = IDX_HD
MISC_B = IDX_HD + IDX_HEADS
MISC_A = MISC_B + A_HEADS
VMEM_LIMIT = 48 * 1024 * 1024
NEG_BIG = -1e30
BISECT_PROBES = 3
COUNT_RADIX = 1024
BISECT_MAX_ITERS = 20


def _cparams(sem):
    return pltpu.CompilerParams(dimension_semantics=sem, vmem_limit_bytes=VMEM_LIMIT)


def _dot(a, b):
    return jnp.dot(a, b, preferred_element_type=F32)


def _dot_nt(a, b, precision=None):
    return lax.dot_general(a, b, (((1,), (1,)), ((), ())), precision=precision,
                           preferred_element_type=F32)


def _split2(x):
    hi = x.astype(BF16)
    return hi, (x - hi.astype(F32)).astype(BF16)


def _mm_split(a, b):
    ah, al = a
    bh, bl = b
    n = ah.shape[0]
    t = _dot(jnp.concatenate([ah, al], axis=0), bh)
    return t[:n] + t[n:] + _dot(ah, bl)


def _silu(x):
    return x * jax.nn.sigmoid(x)


def _lane_mask(width, lo, hi):
    lane = lax.broadcasted_iota(jnp.int32, (1, width), 1)
    return (lane >= lo) & (lane < hi)


def _proj_body(*refs, groups):
    x_ref, g_ref, w_ref = refs[0], refs[1], refs[2]
    n_aux = 3 * sum(1 for _, normed, _ in groups if normed)
    aux = refs[3:3 + n_aux]
    outs = refs[3 + n_aux:3 + n_aux + len(groups)]
    outs_t = list(refs[3 + n_aux + len(groups):])
    x = x_ref[...]
    h = x * lax.rsqrt(jnp.mean(x * x, axis=-1, keepdims=True) + EPS) * g_ref[...]
    hb = h.astype(BF16)
    off = 0
    ai = 0
    for (width, normed, transposed), o_ref in zip(groups, outs):
        y = _dot(hb, w_ref[:, off:off + width])
        if normed:
            bd_ref, gain_ref, nm_ref = aux[ai], aux[ai + 1], aux[ai + 2]
            ai += 3
            sq = y * y
            hi = sq.astype(BF16)
            lo = (sq - hi.astype(F32)).astype(BF16)
            ms = _dot(hi, bd_ref[...]) + _dot(lo, bd_ref[...])
            scale = jnp.where(nm_ref[...] > 0.0, lax.rsqrt(ms + EPS), 1.0)
            y = y * scale * gain_ref[...]
        o_ref[...] = y
        if transposed:
            outs_t.pop(0)[0] = y.T
        off += width


def _group_mean_matrix(width, group, n_lanes):
    i = np.arange(width)
    m = ((i[:, None] // group) == (i[None, :] // group)) & (i[:, None] < n_lanes) & (i[None, :] < n_lanes)
    return jnp.asarray(m.astype(np.float32) / group, dtype=BF16)


def _proj(x, g, w, groups, aux, tm):
    m, d = x.shape
    nw = w.shape[1]
    in_specs = [pl.BlockSpec((tm, d), lambda i: (i, 0)),
                pl.BlockSpec((1, d), lambda i: (0, 0)),
                pl.BlockSpec((d, nw), lambda i: (0, 0))]
    for a in aux:
        in_specs.append(pl.BlockSpec(a.shape, lambda i: (0, 0)))
    out_shape = [jax.ShapeDtypeStruct((m, wd), F32) for wd, _, _ in groups]
    out_specs = [pl.BlockSpec((tm, wd), lambda i: (i, 0)) for wd, _, _ in groups]
    for wd, _, transposed in groups:
        if transposed:
            out_shape.append(jax.ShapeDtypeStruct((m // tm, wd, tm), F32))
            out_specs.append(pl.BlockSpec((1, wd, tm), lambda i: (i, 0, 0)))
    return pl.pallas_call(
        functools.partial(_proj_body, groups=tuple(groups)),
        grid=(m // tm,),
        in_specs=in_specs, out_specs=out_specs, out_shape=out_shape,
        compiler_params=_cparams(("parallel",)),
        name="proj",
    )(x, g, w, *aux)


def _gdn_body(qkv_ref, za_ref, misc_ref, cbuf_ref, s0_ref, cw_ref, avec_ref, dtvec_ref, go_ref,
              o_ref, convnew_ref, sfin_ref,
              xp_ref, q_sc, k_sc, v_sc, g_sc, b_sc, s_sc, *, chunk, rows):
    t = pl.program_id(1)
    c = chunk

    @pl.when(t == 0)
    def _():
        xp_ref[0:8, :] = cbuf_ref[0]
        s_sc[...] = s0_ref[0]

    xp_ref[8:8 + rows, :] = qkv_ref[...]
    cw = cw_ref[...]
    y = xp_ref[5:5 + rows, :] * cw[0:1, :]
    for j in range(1, CONV_W):
        y = y + xp_ref[5 + j:5 + j + rows, :] * cw[j:j + 1, :]
    y = _silu(y)
    convnew_ref[0] = xp_ref[rows + 5:rows + 8, :]
    xp_ref[0:8, :] = xp_ref[rows:rows + 8, :]

    for h in range(A_HEADS):
        qh = y[:, h * A_DK:(h + 1) * A_DK]
        kh = y[:, A_WIDTH + h * A_DK:A_WIDTH + (h + 1) * A_DK]
        q_sc[:, h * A_DK:(h + 1) * A_DK] = qh * (
            lax.rsqrt(jnp.sum(qh * qh, axis=-1, keepdims=True) + EPS) * (A_DK ** -0.5))
        k_sc[:, h * A_DK:(h + 1) * A_DK] = kh * lax.rsqrt(jnp.sum(kh * kh, axis=-1, keepdims=True) + EPS)
    v_sc[...] = y[:, 2 * A_WIDTH:3 * A_WIDTH]
    misc = misc_ref[...]
    b_sc[...] = jax.nn.sigmoid(misc)
    g_sc[...] = avec_ref[...] * jax.nn.softplus(misc + dtvec_ref[...])

    ri = lax.broadcasted_iota(jnp.int32, (c, c), 0)
    ci = lax.broadcasted_iota(jnp.int32, (c, c), 1)
    tri = ri >= ci
    strict = ri > ci
    eye_f = (ri == ci).astype(F32)
    tri3 = (lax.broadcasted_iota(jnp.int32, (c, 3 * c), 0)
            >= lax.broadcasted_iota(jnp.int32, (c, 3 * c), 1) % c).astype(BF16)
    go = go_ref[...]
    n_chunks = rows // c

    problems = [(ic, h) for ic in range(n_chunks) for h in range(A_HEADS)]
    gc_all, gc_t = [], []
    for ic in range(n_chunks):
        g = g_sc[ic * c:(ic + 1) * c, :]
        g1 = g.astype(BF16)
        r1 = g - g1.astype(F32)
        g2 = r1.astype(BF16)
        g3 = (r1 - g2.astype(F32)).astype(BF16)
        gc_all.append(_dot(tri3, jnp.concatenate([g1, g2, g3], axis=0)))
    for ic in range(n_chunks):
        gc_t.append(gc_all[ic].T)

    def load(ref, ic, h):
        return ref[ic * c:(ic + 1) * c, h * A_DK:(h + 1) * A_DK]

    col = lambda a, lane: a[:, lane:lane + 1]
    k_bf = [load(k_sc, ic, h).astype(BF16) for ic, h in problems]
    kb = [load(k_sc, ic, h) * col(b_sc[ic * c:(ic + 1) * c, :], MISC_B + h) for ic, h in problems]
    kk = [_dot_nt(kb[i].astype(BF16), k_bf[i]) for i in range(len(problems))]
    qk_raw = [_dot_nt(load(q_sc, ic, h).astype(BF16), k_bf[i]) for i, (ic, h) in enumerate(problems)]
    decay = []
    for ic, h in problems:
        diff = col(gc_all[ic], MISC_A + h) - gc_t[ic][MISC_A + h:MISC_A + h + 1, :]
        decay.append(jnp.where(tri, jnp.exp(jnp.where(tri, diff, 0.0)), 0.0))
    lmat = [jnp.where(strict, kk[i] * decay[i], 0.0) for i in range(len(problems))]
    qk = [(qk_raw[i] * decay[i]).astype(BF16) for i in range(len(problems))]
    x_inv = [eye_f - m for m in lmat]
    p_split = [_split2(m) for m in lmat]
    for _ in range(int(math.log2(c)) - 1):
        p_split = [_split2(_mm_split(ps, ps)) for ps in p_split]
        x_inv = [x + _mm_split(_split2(x), ps) for x, ps in zip(x_inv, p_split)]
    sol = []
    for i, (ic, h) in enumerate(problems):
        egc = jnp.exp(col(gc_all[ic], MISC_A + h))
        beta = col(b_sc[ic * c:(ic + 1) * c, :], MISC_B + h)
        rhs = jnp.concatenate([load(v_sc, ic, h) * beta, kb[i] * egc], axis=-1).astype(BF16)
        th, tl = _split2(x_inv[i])
        sol.append(_dot(th, rhs) + _dot(tl, rhs))
    pre = []
    for i, (ic, h) in enumerate(problems):
        gc = col(gc_all[ic], MISC_A + h)
        g_last = gc[c - 1:c, :]
        k_dec_t = (load(k_sc, ic, h) * jnp.exp(g_last - gc)).T.astype(BF16)
        wq = jnp.concatenate([sol[i][:, A_DV:], load(q_sc, ic, h) * jnp.exp(gc)], axis=0).astype(BF16)
        pre.append((sol[i][:, :A_DV], wq, k_dec_t, jnp.exp(g_last)))

    heads = range(A_HEADS)
    s_cur = [s_sc[h] for h in heads]
    for ic in range(n_chunks):
        pr = [pre[ic * A_HEADS + h] for h in heads]
        ws = [_dot(pr[h][1], s_cur[h].astype(BF16)) for h in heads]
        ub = [(pr[h][0] - ws[h][:c]).astype(BF16) for h in heads]
        o = [ws[h][c:] + _dot(qk[ic * A_HEADS + h], ub[h]) for h in heads]
        s_cur = [s_cur[h] * pr[h][3] + _dot(pr[h][2], ub[h]) for h in heads]
        for h in heads:
            on = o[h] * lax.rsqrt(jnp.mean(o[h] * o[h], axis=-1, keepdims=True) + EPS) * go
            o_ref[ic * c:(ic + 1) * c, h * A_DV:(h + 1) * A_DV] = on * _silu(load(za_ref, ic, h))
    for h in heads:
        s_sc[h] = s_cur[h]
    sfin_ref[0] = s_sc[...]


def _gdn(qkv, za, misc, cbuf8, s0, conv_w, avec, dtvec, g_o, *, bsz, t_len, chunk, rows):
    nt = t_len // rows
    w3 = 3 * A_WIDTH
    row_map = lambda b, t: (b * nt + t, 0)
    const2 = lambda b, t: (0, 0)
    return pl.pallas_call(
        functools.partial(_gdn_body, chunk=chunk, rows=rows),
        grid=(bsz, nt),
        in_specs=[pl.BlockSpec((rows, w3), row_map),
                  pl.BlockSpec((rows, A_WIDTH), row_map),
                  pl.BlockSpec((rows, MISC_W), row_map),
                  pl.BlockSpec((1, 8, w3), lambda b, t: (b, 0, 0)),
                  pl.BlockSpec((1, A_HEADS, A_DK, A_DV), lambda b, t: (b, 0, 0, 0)),
                  pl.BlockSpec((CONV_W, w3), const2),
                  pl.BlockSpec((1, MISC_W), const2),
                  pl.BlockSpec((1, MISC_W), const2),
                  pl.BlockSpec((1, A_DV), const2)],
        out_specs=[pl.BlockSpec((rows, A_WIDTH), row_map),
                   pl.BlockSpec((1, CONV_W - 1, w3), lambda b, t: (b, 0, 0)),
                   pl.BlockSpec((1, A_HEADS, A_DK, A_DV), lambda b, t: (b, 0, 0, 0))],
        out_shape=[jax.ShapeDtypeStruct((bsz * t_len, A_WIDTH), F32),
                   jax.ShapeDtypeStruct((bsz, CONV_W - 1, w3), F32),
                   jax.ShapeDtypeStruct((bsz, A_HEADS, A_DK, A_DV), F32)],
        scratch_shapes=[pltpu.VMEM((rows + 8, w3), F32),
                        pltpu.VMEM((rows, A_WIDTH), F32),
                        pltpu.VMEM((rows, A_WIDTH), F32),
                        pltpu.VMEM((rows, A_WIDTH), F32),
                        pltpu.VMEM((rows, MISC_W), F32),
                        pltpu.VMEM((rows, MISC_W), F32),
                        pltpu.VMEM((A_HEADS, A_DK, A_DV), F32)],
        compiler_params=_cparams(("parallel", "arbitrary")),
        name="gdn",
    )(qkv, za, misc, cbuf8, s0, conv_w, avec, dtvec, g_o)


def _topk_threshold(sc_ref, nkb, shape, key_axis, n_sel, n_adm):
    kf = float(n_sel)
    kblk = shape[key_axis]

    if key_axis == 0:
        strip = min(kblk, 32)
        part_shape = (strip, shape[1])
        strip_of = lambda kb, i: sc_ref[kb, i * strip:(i + 1) * strip, :]

        def put_strip(kb, i, v):
            sc_ref[kb, i * strip:(i + 1) * strip, :] = v
    else:
        strip = kblk
        part_shape = shape
        strip_of = lambda kb, i: sc_ref[kb]

        def put_strip(kb, i, v):
            sc_ref[kb] = v
    kpos = lax.broadcasted_iota(jnp.int32, part_shape, key_axis)

    def reduce_blocks(specs):
        def body(kb, accs):
            accs = list(accs)
            for i in range(kblk // strip):
                s = strip_of(kb, i)
                k0 = kb * kblk + i * strip
                accs = [cmb(acc, fn(s, k0)) for acc, (fn, _, _, _, cmb) in zip(accs, specs)]
            return tuple(accs)
        init = tuple(jnp.full(part_shape, i, dt) for _, i, dt, _, _ in specs)
        accs = lax.fori_loop(0, nkb, body, init)
        return [acc if op is None else op(acc, axis=key_axis, keepdims=True)
                for acc, (_, _, _, op, _) in zip(accs, specs)]

    one_if = lambda c: jnp.where(c, 1.0, 0.0)
    vmax, vmin, cpos, czero = reduce_blocks([
        (lambda s, k0: s, -jnp.inf, F32, jnp.max, jnp.maximum),
        (lambda s, k0: jnp.where(s == -jnp.inf, jnp.inf, s), jnp.inf, F32, jnp.min, jnp.minimum),
        (lambda s, k0: one_if(s > 0.0), 0.0, F32, jnp.sum, jnp.add),
        (lambda s, k0: one_if(s == 0.0), 0.0, F32, jnp.sum, jnp.add)])

    long_row = n_adm > kf
    pos_row = long_row & (cpos >= kf)
    zero_row = long_row & (cpos < kf) & (cpos + czero >= kf)
    neg_row = long_row & (cpos + czero < kf)
    lo0 = jnp.where(pos_row | zero_row, 0.0, vmin - jnp.maximum(1.0, jnp.abs(vmin)))
    cnt_lo0 = jnp.where(pos_row, cpos, jnp.where(zero_row, kf, n_adm))
    hi0 = jnp.where(neg_row, 0.0, vmax)
    cnt_hi0 = jnp.where(neg_row, cpos, 0.0)
    need_zero = jnp.where(zero_row, kf - cpos, 0.0)

    def cond(st):
        it, go = st[0], st[1]
        return (it < BISECT_MAX_ITERS) & (go > 0)

    def body(st):
        it, _, lo, hi, cnt_lo, cnt_hi = st
        fracs = [(t + 1.0) / (BISECT_PROBES + 1.0) for t in range(BISECT_PROBES)]
        mids = [lo * (1.0 - f) + hi * f for f in fracs]
        for t in range(1, BISECT_PROBES):
            mids[t] = jnp.maximum(mids[t], mids[t - 1])
        codes = [sum(COUNT_RADIX ** u for u in range(t + 1)) for t in range(BISECT_PROBES)]

        def encode(s, k0):
            e = jnp.zeros(s.shape, jnp.int32)
            for mid, code in zip(mids, codes):
                e = jnp.where(s > mid, code, e)
            return e
        (packed,) = reduce_blocks([(encode, 0, jnp.int32, None, jnp.add)])
        cnts = []
        for t in range(BISECT_PROBES):
            digit = packed & (COUNT_RADIX - 1) if t < BISECT_PROBES - 1 else packed
            packed = packed >> COUNT_RADIX.bit_length() - 1
            cnts.append(jnp.sum(digit.astype(F32), axis=key_axis, keepdims=True))
        lo_n, cnt_lo_n, hi_n, cnt_hi_n = lo, cnt_lo, hi, cnt_hi
        for mid, cm in zip(mids, cnts):
            up = cm >= kf
            lo_n = jnp.where(up, mid, lo_n)
            cnt_lo_n = jnp.where(up, cm, cnt_lo_n)
        for mid, cm in zip(mids[::-1], cnts[::-1]):
            dn = cm < kf
            hi_n = jnp.where(dn, mid, hi_n)
            cnt_hi_n = jnp.where(dn, cm, cnt_hi_n)
        go = (jnp.max(cnt_lo_n) > kf).astype(jnp.int32)
        return it + 1, go, lo_n, hi_n, cnt_lo_n, cnt_hi_n

    go0 = (jnp.max(cnt_lo0) > kf).astype(jnp.int32)
    _, go, lo, hi, cnt_lo, cnt_hi = lax.while_loop(
        cond, body, (jnp.int32(0), go0, lo0, hi0, cnt_lo0, cnt_hi0))

    fix = cnt_lo > kf
    need0 = jnp.where(fix, kf - cnt_hi, 0.0)
    big_idx = 2 ** 30

    def fix_cond(need):
        return jnp.max(need) > 0.0

    def fix_body(need):
        def in_cluster(s):
            return (s > lo) & (s <= hi)
        (mval,) = reduce_blocks([(lambda s, k0: jnp.where(in_cluster(s), s, -jnp.inf), -jnp.inf, F32,
                                  jnp.max, jnp.maximum)])
        (idx,) = reduce_blocks(
            [(lambda s, k0: jnp.where(in_cluster(s) & (s == mval), kpos + k0, big_idx),
              big_idx, jnp.int32, jnp.min, jnp.minimum)])
        active = need > 0.0

        def promote(kb, carry):
            for i in range(kblk // strip):
                s = strip_of(kb, i)
                put_strip(kb, i, jnp.where(active & ((kpos + kb * kblk + i * strip) == idx), jnp.inf, s))
            return carry
        lax.fori_loop(0, nkb, promote, 0)
        return jnp.where(active, need - 1.0, need)

    lax.while_loop(fix_cond, fix_body, need0)

    @pl.when(jnp.max(need_zero) > 0.0)
    def _():
        r = lax.broadcasted_iota(jnp.int32, (kblk, kblk), 0)
        c = lax.broadcasted_iota(jnp.int32, (kblk, kblk), 1)
        tri = ((r >= c) if key_axis == 0 else (r <= c)).astype(BF16)

        def tie_blk(kb, seen):
            s = sc_ref[kb]
            z = (s == 0.0) & zero_row
            zb = one_if(z).astype(BF16)
            if key_axis == 0:
                rank = _dot(tri, zb) + seen
                last = rank[kblk - 1:kblk, :]
            else:
                rank = _dot(zb, tri) + seen
                last = rank[:, kblk - 1:kblk]
            sc_ref[kb] = jnp.where(z & (rank <= need_zero), jnp.inf, s)
            return last
        lax.fori_loop(0, nkb, tie_blk, jnp.zeros_like(need_zero))

    return jnp.where(fix, hi, lo)


def _dsa_body(qb_ref, qi_ref, miscq_ref, zb_ref, k_ref, vt_ref, misck_ref, o_ref, sc_ref, s_ref, *, n_sel, qblk):
    j = pl.program_id(1)
    nkb = ((j + 1) * qblk + KEY_BLOCK - 1) // KEY_BLOCK
    qi = qi_ref[...]
    q_h = [qi[:, h * IDX_HD:(h + 1) * IDX_HD].astype(BF16) for h in range(IDX_HEADS)]
    w_t = miscq_ref[...].T[MISC_WI:MISC_WI + IDX_HEADS, :]
    qchunk = (j * qblk + lax.broadcasted_iota(jnp.int32, (1, qblk), 1)) // CHUNK
    krow = lax.broadcasted_iota(jnp.int32, (KEY_BLOCK, 1), 0)

    def score_blk(kb, carry):
        k0 = pl.multiple_of(kb * KEY_BLOCK, KEY_BLOCK)
        for half in range(KEY_BLOCK // SCORE_STRIP):
            r0 = half * SCORE_STRIP
            ki = misck_ref[pl.ds(k0 + r0, SCORE_STRIP), 0:IDX_HD].astype(BF16)
            acc = jnp.zeros((SCORE_STRIP, qblk), F32)
            for h in range(IDX_HEADS):
                acc = acc + w_t[h:h + 1, :] * jnp.maximum(_dot_nt(ki, q_h[h]), 0.0)
            adm = ((k0 + r0 + krow[:SCORE_STRIP]) // CHUNK) <= qchunk
            sc_ref[kb, r0:r0 + SCORE_STRIP, :] = jnp.where(adm, acc, -jnp.inf)
        return carry

    npairs = (nkb + 1) // 2

    def pair_loop(body, init):
        return lax.fori_loop(0, npairs, lambda i, c: body(2 * i + 1, body(2 * i, c)), init)

    pair_loop(score_blk, 0)

    n_adm = ((qchunk + 1) * CHUNK).astype(F32)
    thr = _topk_threshold(sc_ref, nkb, (KEY_BLOCK, qblk), 0, n_sel, n_adm)

    qb = qb_ref[...] * (B_HD ** -0.5 * math.log2(math.e))
    qs = jnp.concatenate([jnp.where(_lane_mask(B_WIDTH, h * B_HD, (h + 1) * B_HD), qb, 0.0)
                          for h in range(B_HEADS)], axis=0).astype(BF16)
    wide = B_HEADS * qblk

    def fold(acc, x, combine):
        for i in range(x.shape[0] // 8):
            acc = combine(acc, x[8 * i:8 * (i + 1)])
        return acc

    def logits_blk(kb, m8):
        k0 = pl.multiple_of(kb * KEY_BLOCK, KEY_BLOCK)
        s = _dot_nt(k_ref[pl.ds(k0, KEY_BLOCK), :].astype(BF16), qs)
        sel = sc_ref[kb] > thr
        s = jnp.where(jnp.concatenate([sel] * B_HEADS, axis=1), s, NEG_BIG)
        s_ref[kb] = s
        return fold(m8, s, jnp.maximum)

    m8 = pair_loop(logits_blk, jnp.full((8, wide), NEG_BIG, F32))
    m = jnp.max(m8, axis=0, keepdims=True)

    def pv_blk(kb, carry):
        l8, accs = carry
        p = jnp.exp2(s_ref[kb] - m)
        pb = p.astype(BF16)
        vt = vt_ref[kb].astype(BF16)
        accs = tuple(accs[h] + _dot(vt[h * B_HD:(h + 1) * B_HD, :], pb[:, h * qblk:(h + 1) * qblk])
                     for h in range(B_HEADS))
        return fold(l8, p, jnp.add), accs

    l8, accs = pair_loop(
        pv_blk, (jnp.zeros((8, wide), F32), tuple(jnp.zeros((B_HD, qblk), F32) for _ in range(B_HEADS))))
    l = jnp.sum(l8, axis=0, keepdims=True)
    o_t = jnp.concatenate([accs[h] / l[:, h * qblk:(h + 1) * qblk] for h in range(B_HEADS)], axis=0)
    o_ref[...] = o_t.T * _silu(zb_ref[...])


def _dsa_prompt(qb, qi, misc, zb, kb, vt, *, bsz, t_len, n_sel, qblk):
    nq = t_len // qblk
    nkb = t_len // KEY_BLOCK
    assert nkb * (KEY_BLOCK // 8) < COUNT_RADIX and nkb % 2 == 0
    qmap = lambda b, j: (b * nq + j, 0)
    kmap = lambda b, j: (b, 0)
    return pl.pallas_call(
        functools.partial(_dsa_body, n_sel=n_sel, qblk=qblk),
        grid=(bsz, nq),
        in_specs=[pl.BlockSpec((qblk, B_WIDTH), qmap),
                  pl.BlockSpec((qblk, IDX_HEADS * IDX_HD), qmap),
                  pl.BlockSpec((qblk, MISC_W), qmap),
                  pl.BlockSpec((qblk, B_WIDTH), qmap),
                  pl.BlockSpec((t_len, B_WIDTH), kmap),
                  pl.BlockSpec((nkb, B_WIDTH, KEY_BLOCK), lambda b, j: (b, 0, 0)),
                  pl.BlockSpec((t_len, MISC_W), kmap)],
        out_specs=pl.BlockSpec((qblk, B_WIDTH), qmap),
        out_shape=jax.ShapeDtypeStruct((bsz * t_len, B_WIDTH), F32),
        scratch_shapes=[pltpu.VMEM((nkb, KEY_BLOCK, qblk), F32),
                        pltpu.VMEM((nkb, KEY_BLOCK, B_HEADS * qblk), F32)],
        compiler_params=_cparams(("parallel", "arbitrary")),
        name="dsa_prompt",
    )(qb, qi, misc, zb, kb, vt, misc)


def _dsa_dec_body(qb_ref, qi_ref, misc_ref, zb_ref, kn_ref, vn_ref, ck_ref, cv_ref, cki_ref, o_ref, sc_ref,
                  *, p_len, tq, n_sel):
    npast = p_len // KEY_BLOCK
    nkb = npast + 1
    qi = qi_ref[...]
    misc = misc_ref[...]
    q2 = jnp.concatenate([qi[:, h * IDX_HD:(h + 1) * IDX_HD] for h in range(IDX_HEADS)], axis=0).astype(BF16)
    w_h = [misc[:, MISC_WI + h:MISC_WI + h + 1] for h in range(IDX_HEADS)]
    lane = lax.broadcasted_iota(jnp.int32, (1, KEY_BLOCK), 1)
    pad_rows = KEY_BLOCK - tq

    def scores(kblk):
        lg = _dot_nt(q2, kblk.astype(BF16))
        acc = jnp.zeros((tq, KEY_BLOCK), F32)
        for h in range(IDX_HEADS):
            acc = acc + w_h[h] * jnp.maximum(lg[h * tq:(h + 1) * tq, :], 0.0)
        return acc

    def score_blk(kb, carry):
        k0 = pl.multiple_of(kb * KEY_BLOCK, KEY_BLOCK)
        sc_ref[kb] = scores(cki_ref[pl.ds(k0, KEY_BLOCK), :])
        return carry

    lax.fori_loop(0, npast, score_blk, 0)
    ki_new = jnp.concatenate([misc[:, 0:IDX_HD], jnp.zeros((pad_rows, IDX_HD), F32)], axis=0)
    sc_ref[npast] = jnp.where(lane < tq, scores(ki_new), -jnp.inf)

    n_adm = jnp.full((tq, 1), float(p_len + tq), F32)
    thr = _topk_threshold(sc_ref, nkb, (tq, KEY_BLOCK), 1, n_sel, n_adm)

    qb = qb_ref[...] * (B_HD ** -0.5)
    masks = [_lane_mask(B_WIDTH, h * B_HD, (h + 1) * B_HD) for h in range(B_HEADS)]
    qs = jnp.concatenate([jnp.where(masks[h], qb, 0.0) for h in range(B_HEADS)], axis=0).astype(BF16)
    rows = B_HEADS * tq

    def att(carry, kblk, vblk, sel):
        m, l, acc = carry
        sel4 = jnp.concatenate([sel] * B_HEADS, axis=0)
        s = jnp.where(sel4, _dot_nt(qs, kblk.astype(BF16)), NEG_BIG)
        m_new = jnp.maximum(m, jnp.max(s, axis=-1, keepdims=True))
        alpha = jnp.exp(m - m_new)
        p = jnp.exp(s - m_new)
        l = alpha * l + jnp.sum(p, axis=-1, keepdims=True)
        acc = alpha * acc + _dot(p.astype(BF16), vblk.astype(BF16))
        return m_new, l, acc

    def att_blk(kb, carry):
        k0 = pl.multiple_of(kb * KEY_BLOCK, KEY_BLOCK)
        return att(carry, ck_ref[pl.ds(k0, KEY_BLOCK), :], cv_ref[pl.ds(k0, KEY_BLOCK), :], sc_ref[kb] > thr)

    init = (jnp.full((rows, 1), NEG_BIG, F32), jnp.zeros((rows, 1), F32), jnp.zeros((rows, B_WIDTH), F32))
    carry = lax.fori_loop(0, npast, att_blk, init)
    zpad = jnp.zeros((pad_rows, B_WIDTH), F32)
    _, l, acc = att(carry, jnp.concatenate([kn_ref[...], zpad], axis=0),
                    jnp.concatenate([vn_ref[...], zpad], axis=0), sc_ref[npast] > thr)
    res = acc / l
    out = jnp.zeros((tq, B_WIDTH), F32)
    for h in range(B_HEADS):
        out = out + jnp.where(masks[h], res[h * tq:(h + 1) * tq, :], 0.0)
    o_ref[...] = out * _silu(zb_ref[...])


def _dsa_decode(qb, qi, misc, zb, kb, vb, ck, cv, cki, *, bsz, t_len, p_len, n_sel):
    assert p_len // KEY_BLOCK + 1 < COUNT_RADIX
    qmap = lambda b: (b, 0)
    cmap = lambda b: (b, 0, 0)
    return pl.pallas_call(
        functools.partial(_dsa_dec_body, p_len=p_len, tq=t_len, n_sel=n_sel),
        grid=(bsz,),
        in_specs=[pl.BlockSpec((t_len, B_WIDTH), qmap),
                  pl.BlockSpec((t_len, IDX_HEADS * IDX_HD), qmap),
                  pl.BlockSpec((t_len, MISC_W), qmap),
                  pl.BlockSpec((t_len, B_WIDTH), qmap),
                  pl.BlockSpec((t_len, B_WIDTH), qmap),
                  pl.BlockSpec((t_len, B_WIDTH), qmap),
                  pl.BlockSpec((None, p_len, B_WIDTH), cmap),
                  pl.BlockSpec((None, p_len, B_WIDTH), cmap),
                  pl.BlockSpec((None, p_len, IDX_HD), cmap)],
        out_specs=pl.BlockSpec((t_len, B_WIDTH), qmap),
        out_shape=jax.ShapeDtypeStruct((bsz * t_len, B_WIDTH), F32),
        scratch_shapes=[pltpu.VMEM((p_len // KEY_BLOCK + 1, t_len, KEY_BLOCK), F32)],
        compiler_params=_cparams(("parallel",)),
        name="dsa_decode",
    )(qb, qi, misc, zb, kb, vb, ck, cv, cki)


def _mem_body(qm_ref, zm_ref, mk_ref, mv_ref, o_ref):
    qm = qm_ref[...] * (M_HD ** -0.5)
    mk = mk_ref[...].astype(BF16)
    mv = mv_ref[...].astype(BF16)
    out = jnp.zeros(qm.shape, F32)
    for h in range(M_HEADS):
        mask = _lane_mask(M_WIDTH, h * M_HD, (h + 1) * M_HD)
        s = _dot_nt(jnp.where(mask, qm, 0.0).astype(BF16), mk)
        p = jnp.exp(s - jnp.max(s, axis=-1, keepdims=True))
        l = jnp.sum(p, axis=-1, keepdims=True)
        out = out + jnp.where(mask, _dot(p.astype(BF16), mv) / l, 0.0)
    o_ref[...] = out * _silu(zm_ref[...])


def _mem_attn(qm, zm, mk, mv, *, bsz, t_len, tq):
    nq = t_len // tq
    qmap = lambda b, j: (b * nq + j, 0)
    kmap = lambda b, j: (b, 0, 0)
    return pl.pallas_call(
        _mem_body,
        grid=(bsz, nq),
        in_specs=[pl.BlockSpec((tq, M_WIDTH), qmap),
                  pl.BlockSpec((tq, M_WIDTH), qmap),
                  pl.BlockSpec((None, N_MEM, M_WIDTH), kmap),
                  pl.BlockSpec((None, N_MEM, M_WIDTH), kmap)],
        out_specs=pl.BlockSpec((tq, M_WIDTH), qmap),
        out_shape=jax.ShapeDtypeStruct((bsz * t_len, M_WIDTH), F32),
        compiler_params=_cparams(("parallel", "parallel")),
        name="mem_attn",
    )(qm, zm, mk, mv)


def _oproj_body(x_ref, oa_ref, ob_ref, om_ref, w_ref, y_ref):
    acc = _dot(oa_ref[...].astype(BF16), w_ref[0:A_WIDTH, :])
    acc = acc + _dot(ob_ref[...].astype(BF16), w_ref[A_WIDTH:A_WIDTH + B_WIDTH, :])
    acc = acc + _dot(om_ref[...].astype(BF16), w_ref[A_WIDTH + B_WIDTH:, :])
    y_ref[...] = x_ref[...] + acc


def _oproj(x, oa, ob, om, w, tm):
    m, d = x.shape
    rmap = lambda i: (i, 0)
    return pl.pallas_call(
        _oproj_body,
        grid=(m // tm,),
        in_specs=[pl.BlockSpec((tm, d), rmap),
                  pl.BlockSpec((tm, A_WIDTH), rmap),
                  pl.BlockSpec((tm, B_WIDTH), rmap),
                  pl.BlockSpec((tm, M_WIDTH), rmap),
                  pl.BlockSpec(w.shape, lambda i: (0, 0))],
        out_specs=pl.BlockSpec((tm, d), rmap),
        out_shape=jax.ShapeDtypeStruct((m, d), F32),
        compiler_params=_cparams(("parallel",)),
        name="oproj",
    )(x, oa, ob, om, w)


def _prep_weights(g_in, w_in, conv_w, a_log, dt_bias, g_o, g_qb, g_kb, g_ki, g_qm, w_out):
    offs = np.concatenate([[0], np.cumsum(IN_SPLITS)])
    col = lambda i: w_in[:, int(offs[i]):int(offs[i + 1])]
    (w_qkv, w_za, w_ba, w_aa, w_qb, w_kb, w_vb, w_zb, w_qi, w_ki, w_wi, w_qm, w_zm) = [col(i) for i in range(13)]
    d = w_in.shape[0]
    w_misc = jnp.concatenate(
        [w_ki, w_wi, w_ba, w_aa, jnp.zeros((d, MISC_W - MISC_A - A_HEADS), w_in.dtype)], axis=1)
    w_all = jnp.concatenate([w_qkv, w_za, w_qb, w_kb, w_vb, w_zb, w_qi, w_qm, w_zm, w_misc],
                            axis=1).astype(BF16)
    groups = [(3 * A_WIDTH, False, False), (A_WIDTH, False, False), (B_WIDTH, True, False),
              (B_WIDTH, True, False), (B_WIDTH, False, True), (B_WIDTH, False, False),
              (IDX_HEADS * IDX_HD, False, False), (M_WIDTH, True, False), (M_WIDTH, False, False),
              (MISC_W, True, False)]
    bd64 = _group_mean_matrix(B_WIDTH, B_HD, B_WIDTH)
    ones256 = jnp.ones((1, B_WIDTH), F32)
    misc_gain = jnp.concatenate([g_ki, jnp.full((IDX_HEADS,), IDX_SCALE, F32),
                                 jnp.ones((MISC_W - MISC_B,), F32)])[None, :]
    misc_nm = (jnp.arange(MISC_W) < IDX_HD).astype(F32)[None, :]
    aux = [bd64, jnp.tile(g_qb, B_HEADS)[None, :], ones256,
           bd64, jnp.tile(g_kb, B_HEADS)[None, :], ones256,
           bd64, jnp.tile(g_qm, M_HEADS)[None, :], ones256,
           _group_mean_matrix(MISC_W, IDX_HD, IDX_HD), misc_gain, misc_nm]
    pad_a = lambda v: jnp.zeros((1, MISC_W), F32).at[0, MISC_A:MISC_A + A_HEADS].set(v)
    return dict(g_in=g_in[None, :], w_all=w_all, groups=groups, aux=aux, conv_w=conv_w,
                avec=pad_a(-jnp.exp(a_log)), dtvec=pad_a(dt_bias), g_o=g_o[None, :],
                w_out=w_out.astype(BF16))


def _layer(x, mem_k, mem_v, conv_buf, s0, past, gdn_chunk, gdn_rows, wp):
    bsz, t_len, d = x.shape
    m = bsz * t_len
    x2 = x.reshape(m, d)
    tm = min(KEY_BLOCK, m)
    (qkv, za, qb, kb, vb, zb, qi, qm, zm, misc, vt) = _proj(x2, wp["g_in"], wp["w_all"], wp["groups"],
                                                          wp["aux"], tm)
    cbuf8 = jnp.concatenate([jnp.zeros((bsz, 8 - (CONV_W - 1), 3 * A_WIDTH), F32), conv_buf], axis=1)
    oa, conv_new, s_new = _gdn(qkv, za, misc, cbuf8, s0, wp["conv_w"], wp["avec"], wp["dtvec"], wp["g_o"],
                               bsz=bsz, t_len=t_len, chunk=gdn_chunk, rows=gdn_rows)
    if past is None:
        n_sel = min(TOPK_MAX, t_len // 4)
        ob = _dsa_prompt(qb, qi, misc, zb, kb, vt, bsz=bsz, t_len=t_len, n_sel=n_sel, qblk=DSA_QBLK)
    else:
        ck, cv, cki = past
        p_len = ck.shape[1]
        assert (p_len + t_len - 1) // CHUNK <= p_len // CHUNK and p_len % KEY_BLOCK == 0
        n_sel = min(TOPK_MAX, (p_len + t_len) // 4)
        ob = _dsa_decode(qb, qi, misc, zb, kb, vb, ck.reshape(bsz, p_len, B_WIDTH),
                         cv.reshape(bsz, p_len, B_WIDTH), cki, bsz=bsz, t_len=t_len, p_len=p_len, n_sel=n_sel)
    om = _mem_attn(qm, zm, mem_k, mem_v, bsz=bsz, t_len=t_len, tq=min(256, t_len))
    y = _oproj(x2, oa, ob, om, wp["w_out"], min(512, m))
    return (y.reshape(bsz, t_len, d), conv_new, s_new,
            kb.reshape(bsz, t_len, B_HEADS, B_HD), vb.reshape(bsz, t_len, B_HEADS, B_HD),
            misc[:, :IDX_HD].reshape(bsz, t_len, IDX_HD))


def _memory_kv(mem, g_mem, w_mem_kv, g_km):
    bsz, n_mem, d = mem.shape
    groups = [(M_WIDTH, True, False), (M_WIDTH, False, False)]
    aux = [_group_mean_matrix(M_WIDTH, M_HD, M_WIDTH), jnp.tile(g_km, M_HEADS)[None, :],
           jnp.ones((1, M_WIDTH), F32)]
    mk, mv = _proj(mem.reshape(bsz * n_mem, d), g_mem[None, :], w_mem_kv.astype(BF16), groups, aux, 256)
    return mk.reshape(bsz, n_mem, M_WIDTH), mv.reshape(bsz, n_mem, M_WIDTH)


def kernel(x_prompt, x_sample, state_conv_A, state_ssm_A, cache_k_B, cache_v_B, cache_kidx_B, cache_mem_k,
           cache_mem_v, mem_prompt, g_in, w_in, conv_w_A, a_log_A, dt_bias_A, g_o_A, g_q_B, g_k_B, g_kidx_B,
           g_mem, w_mem_kv, g_q_M, g_k_M, w_out):
    depth = w_in.shape[0]
    assert depth == 1
    l = 0
    bp, t_p, _ = x_prompt.shape
    bs, t_s, _ = x_sample.shape
    wp = _prep_weights(g_in[l], w_in[l], conv_w_A[l], a_log_A[l], dt_bias_A[l], g_o_A[l], g_q_B[l], g_k_B[l],
                       g_kidx_B[l], g_q_M[l], w_out[l])
    mk, mv = _memory_kv(mem_prompt, g_mem[l], w_mem_kv[l], g_k_M[l])
    zero_conv = jnp.zeros((bp, CONV_W - 1, 3 * A_WIDTH), F32)
    zero_ssm = jnp.zeros((bp, A_HEADS, A_DK, A_DV), F32)
    yp, c1, s1, k1, v1, ki1 = _layer(x_prompt, mk, mv, zero_conv, zero_ssm, None, CHUNK, 4 * CHUNK, wp)
    ys, c2, s2, k2, v2, ki2 = _layer(
        x_sample, cache_mem_k[l].reshape(bs, N_MEM, M_WIDTH), cache_mem_v[l].reshape(bs, N_MEM, M_WIDTH),
        state_conv_A[l], state_ssm_A[l], (cache_k_B[l], cache_v_B[l], cache_kidx_B[l]), t_s, t_s, wp)
    st = lambda a: a[None]
    return (yp, ys, st(c1), st(s1), st(k1), st(v1), st(ki1),
            st(mk.reshape(bp, N_MEM, M_HEADS, M_HD)), st(mv.reshape(bp, N_MEM, M_HEADS, M_HD)),
            st(c2), st(s2), st(k2), st(v2), st(ki2))
```

```python
import functools
import math

import numpy as np
import jax
import jax.numpy as jnp
from jax import lax
from jax.experimental import pallas as pl
from jax.experimental.pallas import tpu as pltpu

F32 = jnp.float32
BF16 = jnp.bfloat16
HIGHEST = lax.Precision.HIGHEST

D_MODEL = 1024
CHUNK = 64
A_HEADS = 4
A_DK = 128
A_DV = 128
A_WIDTH = A_HEADS * A_DV
CONV_W = 4
B_HEADS = 4
B_HD = 64
B_WIDTH = B_HEADS * B_HD
IDX_HEADS = 8
IDX_HD = 32
IDX_SCALE = (IDX_HEADS ** -0.5) * (IDX_HD ** -0.5)
TOPK_MAX = 256
Q_BLOCK = 128
N_MEM = 256
M_HEADS = 4
M_HD = 64
M_WIDTH = M_HEADS * M_HD
EPS = 1e-6
IN_SPLITS = (3 * A_WIDTH, A_WIDTH, A_HEADS, A_HEADS,
             B_WIDTH, B_WIDTH, B_WIDTH, B_WIDTH, IDX_HEADS * IDX_HD, IDX_HD, IDX_HEADS,
             M_WIDTH, M_WIDTH)

LANES = 128
KEY_BLOCK = 256
DSA_QBLK = 256
DEC_KEYS_PER_STEP = 1024
SCORE_STRIP = 128
MISC_W = LANES
MISC_WI = IDX_HD
MISC_B = IDX_HD + IDX_HEADS
MISC_A = MISC_B + A_HEADS
VMEM_LIMIT = 48 * 1024 * 1024
NEG_BIG = -1e30
BISECT_PROBES = 3
COUNT_RADIX = 1024
BISECT_MAX_ITERS = 20


def _cparams(sem):
    return pltpu.CompilerParams(dimension_semantics=sem, vmem_limit_bytes=VMEM_LIMIT)


def _dot(a, b):
    return jnp.dot(a, b, preferred_element_type=F32)


def _dot_nt(a, b, precision=None):
    return lax.dot_general(a, b, (((1,), (1,)), ((), ())), precision=precision,
                           preferred_element_type=F32)


def _split2(x):
    hi = x.astype(BF16)
    return hi, (x - hi.astype(F32)).astype(BF16)


def _mm_split(a, b):
    ah, al = a
    bh, bl = b
    n = ah.shape[0]
    t = _dot(jnp.concatenate([ah, al], axis=0), bh)
    return t[:n] + t[n:] + _dot(ah, bl)


def _silu(x):
    return x * jax.nn.sigmoid(x)


def _lane_mask(width, lo, hi):
    lane = lax.broadcasted_iota(jnp.int32, (1, width), 1)
    return (lane >= lo) & (lane < hi)


def _proj_body(*refs, groups):
    x_ref, g_ref, w_ref = refs[0], refs[1], refs[2]
    n_aux = 3 * sum(1 for _, normed, _ in groups if normed)
    aux = refs[3:3 + n_aux]
    outs = refs[3 + n_aux:3 + n_aux + len(groups)]
    outs_t = list(refs[3 + n_aux + len(groups):])
    x = x_ref[...]
    h = x * lax.rsqrt(jnp.mean(x * x, axis=-1, keepdims=True) + EPS) * g_ref[...]
    hb = h.astype(BF16)
    off = 0
    ai = 0
    for (width, normed, transposed), o_ref in zip(groups, outs):
        y = _dot(hb, w_ref[:, off:off + width])
        if normed:
            bd_ref, gain_ref, nm_ref = aux[ai], aux[ai + 1], aux[ai + 2]
            ai += 3
            sq = y * y
            hi = sq.astype(BF16)
            lo = (sq - hi.astype(F32)).astype(BF16)
            ms = _dot(hi, bd_ref[...]) + _dot(lo, bd_ref[...])
            scale = jnp.where(nm_ref[...] > 0.0, lax.rsqrt(ms + EPS), 1.0)
            y = y * scale * gain_ref[...]
        o_ref[...] = y
        if "t" in transposed:
            outs_t.pop(0)[0] = y.T
        if "heads" in transposed:
            oh_ref = outs_t.pop(0)
            n_heads = oh_ref.shape[1]
            hd = width // n_heads
            oh_ref[...] = y.reshape(y.shape[0], n_heads, hd)
        off += width


def _group_mean_matrix(width, group, n_lanes):
    i = np.arange(width)
    m = ((i[:, None] // group) == (i[None, :] // group)) & (i[:, None] < n_lanes) & (i[None, :] < n_lanes)
    return jnp.asarray(m.astype(np.float32) / group, dtype=BF16)


def _proj(x, g, w, groups, aux, tm):
    m, d = x.shape
    nw = w.shape[1]
    in_specs = [pl.BlockSpec((tm, d), lambda i: (i, 0)),
                pl.BlockSpec((1, d), lambda i: (0, 0)),
                pl.BlockSpec((d, nw), lambda i: (0, 0))]
    for a in aux:
        in_specs.append(pl.BlockSpec(a.shape, lambda i: (0, 0)))
    out_shape = [jax.ShapeDtypeStruct((m, wd), F32) for wd, _, _ in groups]
    out_specs = [pl.BlockSpec((tm, wd), lambda i: (i, 0)) for wd, _, _ in groups]
    for wd, _, transposed in groups:
        if "t" in transposed:
            out_shape.append(jax.ShapeDtypeStruct((m // tm, wd, tm), F32))
            out_specs.append(pl.BlockSpec((1, wd, tm), lambda i: (i, 0, 0)))
        if "heads" in transposed:
            out_shape.append(jax.ShapeDtypeStruct((m, B_HEADS, wd // B_HEADS), F32))
            out_specs.append(pl.BlockSpec((tm, B_HEADS, wd // B_HEADS), lambda i: (i, 0, 0)))
    return pl.pallas_call(
        functools.partial(_proj_body, groups=tuple(groups)),
        grid=(m // tm,),
        in_specs=in_specs, out_specs=out_specs, out_shape=out_shape,
        compiler_params=_cparams(("parallel",)),
        name="proj",
    )(x, g, w, *aux)


def _gdn_body(qkv_ref, za_ref, misc_ref, cbuf_ref, s0_ref, cw_ref, avec_ref, dtvec_ref, go_ref,
              o_ref, convnew_ref, sfin_ref,
              xp_ref, q_sc, k_sc, v_sc, g_sc, b_sc, s_sc, *, chunk, rows):
    t = pl.program_id(1)
    c = chunk

    @pl.when(t == 0)
    def _():
        xp_ref[0:8, :] = cbuf_ref[0]
        s_sc[...] = s0_ref[0]

    xp_ref[8:8 + rows, :] = qkv_ref[...]
    cw = cw_ref[...]
    y = xp_ref[5:5 + rows, :] * cw[0:1, :]
    for j in range(1, CONV_W):
        y = y + xp_ref[5 + j:5 + j + rows, :] * cw[j:j + 1, :]
    y = _silu(y)
    convnew_ref[0] = xp_ref[rows + 5:rows + 8, :]
    xp_ref[0:8, :] = xp_ref[rows:rows + 8, :]

    for h in range(A_HEADS):
        qh = y[:, h * A_DK:(h + 1) * A_DK]
        kh = y[:, A_WIDTH + h * A_DK:A_WIDTH + (h + 1) * A_DK]
        q_sc[:, h * A_DK:(h + 1) * A_DK] = qh * (
            lax.rsqrt(jnp.sum(qh * qh, axis=-1, keepdims=True) + EPS) * (A_DK ** -0.5))
        k_sc[:, h * A_DK:(h + 1) * A_DK] = kh * lax.rsqrt(jnp.sum(kh * kh, axis=-1, keepdims=True) + EPS)
    v_sc[...] = y[:, 2 * A_WIDTH:3 * A_WIDTH]
    misc = misc_ref[...]
    b_sc[...] = jax.nn.sigmoid(misc)
    g_sc[...] = avec_ref[...] * jax.nn.softplus(misc + dtvec_ref[...])

    ri = lax.broadcasted_iota(jnp.int32, (c, c), 0)
    ci = lax.broadcasted_iota(jnp.int32, (c, c), 1)
    tri = ri >= ci
    strict = ri > ci
    eye_f = (ri == ci).astype(F32)
    tri3 = (lax.broadcasted_iota(jnp.int32, (c, 3 * c), 0)
            >= lax.broadcasted_iota(jnp.int32, (c, 3 * c), 1) % c).astype(BF16)
    go = go_ref[...]
    n_chunks = rows // c

    problems = [(ic, h) for ic in range(n_chunks) for h in range(A_HEADS)]
    gc_all, gc_t = [], []
    for ic in range(n_chunks):
        g = g_sc[ic * c:(ic + 1) * c, :]
        g1 = g.astype(BF16)
        r1 = g - g1.astype(F32)
        g2 = r1.astype(BF16)
        g3 = (r1 - g2.astype(F32)).astype(BF16)
        gc_all.append(_dot(tri3, jnp.concatenate([g1, g2, g3], axis=0)))
    for ic in range(n_chunks):
        gc_t.append(gc_all[ic].T)

    def load(ref, ic, h):
        return ref[ic * c:(ic + 1) * c, h * A_DK:(h + 1) * A_DK]

    col = lambda a, lane: a[:, lane:lane + 1]
    k_bf = [load(k_sc, ic, h).astype(BF16) for ic, h in problems]
    kb = [load(k_sc, ic, h) * col(b_sc[ic * c:(ic + 1) * c, :], MISC_B + h) for ic, h in problems]
    kk = [_dot_nt(kb[i].astype(BF16), k_bf[i]) for i in range(len(problems))]
    qk_raw = [_dot_nt(load(q_sc, ic, h).astype(BF16), k_bf[i]) for i, (ic, h) in enumerate(problems)]
    decay = []
    for ic, h in problems:
        diff = col(gc_all[ic], MISC_A + h) - gc_t[ic][MISC_A + h:MISC_A + h + 1, :]
        decay.append(jnp.where(tri, jnp.exp(jnp.where(tri, diff, 0.0)), 0.0))
    lmat = [jnp.where(strict, kk[i] * decay[i], 0.0) for i in range(len(problems))]
    qk = [(qk_raw[i] * decay[i]).astype(BF16) for i in range(len(problems))]
    x_inv = [eye_f - m for m in lmat]
    p_split = [_split2(m) for m in lmat]
    for _ in range(int(math.log2(c)) - 1):
        p_split = [_split2(_mm_split(ps, ps)) for ps in p_split]
        x_inv = [x + _mm_split(_split2(x), ps) for x, ps in zip(x_inv, p_split)]
    sol = []
    for i, (ic, h) in enumerate(problems):
        egc = jnp.exp(col(gc_all[ic], MISC_A + h))
        beta = col(b_sc[ic * c:(ic + 1) * c, :], MISC_B + h)
        rhs = jnp.concatenate([load(v_sc, ic, h) * beta, kb[i] * egc], axis=-1).astype(BF16)
        th, tl = _split2(x_inv[i])
        sol.append(_dot(th, rhs) + _dot(tl, rhs))
    pre = []
    for i, (ic, h) in enumerate(problems):
        gc = col(gc_all[ic], MISC_A + h)
        g_last = gc[c - 1:c, :]
        k_dec_t = (load(k_sc, ic, h) * jnp.exp(g_last - gc)).T.astype(BF16)
        wq = jnp.concatenate([sol[i][:, A_DV:], load(q_sc, ic, h) * jnp.exp(gc)], axis=0).astype(BF16)
        pre.append((sol[i][:, :A_DV], wq, k_dec_t, jnp.exp(g_last)))

    heads = range(A_HEADS)
    s_cur = [s_sc[h] for h in heads]
    for ic in range(n_chunks):
        pr = [pre[ic * A_HEADS + h] for h in heads]
        ws = [_dot(pr[h][1], s_cur[h].astype(BF16)) for h in heads]
        ub = [(pr[h][0] - ws[h][:c]).astype(BF16) for h in heads]
        o = [ws[h][c:] + _dot(qk[ic * A_HEADS + h], ub[h]) for h in heads]
        s_cur = [s_cur[h] * pr[h][3] + _dot(pr[h][2], ub[h]) for h in heads]
        for h in heads:
            on = o[h] * lax.rsqrt(jnp.mean(o[h] * o[h], axis=-1, keepdims=True) + EPS) * go
            o_ref[ic * c:(ic + 1) * c, h * A_DV:(h + 1) * A_DV] = on * _silu(load(za_ref, ic, h))
    for h in heads:
        s_sc[h] = s_cur[h]
    sfin_ref[0] = s_sc[...]


def _gdn(qkv, za, misc, cbuf8, s0, conv_w, avec, dtvec, g_o, *, bsz, t_len, chunk, rows):
    nt = t_len // rows
    w3 = 3 * A_WIDTH
    row_map = lambda b, t: (b * nt + t, 0)
    const2 = lambda b, t: (0, 0)
    return pl.pallas_call(
        functools.partial(_gdn_body, chunk=chunk, rows=rows),
        grid=(bsz, nt),
        in_specs=[pl.BlockSpec((rows, w3), row_map),
                  pl.BlockSpec((rows, A_WIDTH), row_map),
                  pl.BlockSpec((rows, MISC_W), row_map),
                  pl.BlockSpec((1, 8, w3), lambda b, t: (b, 0, 0)),
                  pl.BlockSpec((1, A_HEADS, A_DK, A_DV), lambda b, t: (b, 0, 0, 0)),
                  pl.BlockSpec((CONV_W, w3), const2),
                  pl.BlockSpec((1, MISC_W), const2),
                  pl.BlockSpec((1, MISC_W), const2),
                  pl.BlockSpec((1, A_DV), const2)],
        out_specs=[pl.BlockSpec((rows, A_WIDTH), row_map),
                   pl.BlockSpec((1, CONV_W - 1, w3), lambda b, t: (b, 0, 0)),
                   pl.BlockSpec((1, A_HEADS, A_DK, A_DV), lambda b, t: (b, 0, 0, 0))],
        out_shape=[jax.ShapeDtypeStruct((bsz * t_len, A_WIDTH), F32),
                   jax.ShapeDtypeStruct((bsz, CONV_W - 1, w3), F32),
                   jax.ShapeDtypeStruct((bsz, A_HEADS, A_DK, A_DV), F32)],
        scratch_shapes=[pltpu.VMEM((rows + 8, w3), F32),
                        pltpu.VMEM((rows, A_WIDTH), F32),
                        pltpu.VMEM((rows, A_WIDTH), F32),
                        pltpu.VMEM((rows, A_WIDTH), F32),
                        pltpu.VMEM((rows, MISC_W), F32),
                        pltpu.VMEM((rows, MISC_W), F32),
                        pltpu.VMEM((A_HEADS, A_DK, A_DV), F32)],
        compiler_params=_cparams(("parallel", "arbitrary")),
        name="gdn",
    )(qkv, za, misc, cbuf8, s0, conv_w, avec, dtvec, g_o)


def _topk_threshold(sc_ref, nkb, shape, key_axis, n_sel, n_adm):
    kf = float(n_sel)
    kblk = shape[key_axis]

    if key_axis == 0:
        strip = min(kblk, 32)
        part_shape = (strip, shape[1])
        strip_of = lambda kb, i: sc_ref[kb, i * strip:(i + 1) * strip, :]

        def put_strip(kb, i, v):
            sc_ref[kb, i * strip:(i + 1) * strip, :] = v
    else:
        strip = kblk
        part_shape = shape
        strip_of = lambda kb, i: sc_ref[kb]

        def put_strip(kb, i, v):
            sc_ref[kb] = v
    kpos = lax.broadcasted_iota(jnp.int32, part_shape, key_axis)

    def reduce_blocks(specs):
        def body(kb, accs):
            accs = list(accs)
            for i in range(kblk // strip):
                s = strip_of(kb, i)
                k0 = kb * kblk + i * strip
                accs = [cmb(acc, fn(s, k0)) for acc, (fn, _, _, _, cmb) in zip(accs, specs)]
            return tuple(accs)
        init = tuple(jnp.full(part_shape, i, dt) for _, i, dt, _, _ in specs)
        accs = lax.fori_loop(0, nkb, body, init)
        return [acc if op is None else op(acc, axis=key_axis, keepdims=True)
                for acc, (_, _, _, op, _) in zip(accs, specs)]

    one_if = lambda c: jnp.where(c, 1.0, 0.0)
    vmax, vmin, cpos, czero = reduce_blocks([
        (lambda s, k0: s, -jnp.inf, F32, jnp.max, jnp.maximum),
        (lambda s, k0: jnp.where(s == -jnp.inf, jnp.inf, s), jnp.inf, F32, jnp.min, jnp.minimum),
        (lambda s, k0: one_if(s > 0.0), 0.0, F32, jnp.sum, jnp.add),
        (lambda s, k0: one_if(s == 0.0), 0.0, F32, jnp.sum, jnp.add)])

    long_row = n_adm > kf
    pos_row = long_row & (cpos >= kf)
    zero_row = long_row & (cpos < kf) & (cpos + czero >= kf)
    neg_row = long_row & (cpos + czero < kf)
    lo0 = jnp.where(pos_row | zero_row, 0.0, vmin - jnp.maximum(1.0, jnp.abs(vmin)))
    cnt_lo0 = jnp.where(pos_row, cpos, jnp.where(zero_row, kf, n_adm))
    hi0 = jnp.where(neg_row, 0.0, vmax)
    cnt_hi0 = jnp.where(neg_row, cpos, 0.0)
    need_zero = jnp.where(zero_row, kf - cpos, 0.0)

    def cond(st):
        it, go = st[0], st[1]
        return (it < BISECT_MAX_ITERS) & (go > 0)

    def body(st):
        it, _, lo, hi, cnt_lo, cnt_hi = st
        fracs = [(t + 1.0) / (BISECT_PROBES + 1.0) for t in range(BISECT_PROBES)]
        mids = [lo * (1.0 - f) + hi * f for f in fracs]
        for t in range(1, BISECT_PROBES):
            mids[t] = jnp.maximum(mids[t], mids[t - 1])
        codes = [sum(COUNT_RADIX ** u for u in range(t + 1)) for t in range(BISECT_PROBES)]

        def encode(s, k0):
            e = jnp.zeros(s.shape, jnp.int32)
            for mid, code in zip(mids, codes):
                e = jnp.where(s > mid, code, e)
            return e
        (packed,) = reduce_blocks([(encode, 0, jnp.int32, None, jnp.add)])
        cnts = []
        for t in range(BISECT_PROBES):
            digit = packed & (COUNT_RADIX - 1) if t < BISECT_PROBES - 1 else packed
            packed = packed >> COUNT_RADIX.bit_length() - 1
            cnts.append(jnp.sum(digit.astype(F32), axis=key_axis, keepdims=True))
        lo_n, cnt_lo_n, hi_n, cnt_hi_n = lo, cnt_lo, hi, cnt_hi
        for mid, cm in zip(mids, cnts):
            up = cm >= kf
            lo_n = jnp.where(up, mid, lo_n)
            cnt_lo_n = jnp.where(up, cm, cnt_lo_n)
        for mid, cm in zip(mids[::-1], cnts[::-1]):
            dn = cm < kf
            hi_n = jnp.where(dn, mid, hi_n)
            cnt_hi_n = jnp.where(dn, cm, cnt_hi_n)
        go = (jnp.max(cnt_lo_n) > kf).astype(jnp.int32)
        return it + 1, go, lo_n, hi_n, cnt_lo_n, cnt_hi_n

    go0 = (jnp.max(cnt_lo0) > kf).astype(jnp.int32)
    _, go, lo, hi, cnt_lo, cnt_hi = lax.while_loop(
        cond, body, (jnp.int32(0), go0, lo0, hi0, cnt_lo0, cnt_hi0))

    fix = cnt_lo > kf
    need0 = jnp.where(fix, kf - cnt_hi, 0.0)
    big_idx = 2 ** 30

    def fix_cond(need):
        return jnp.max(need) > 0.0

    def fix_body(need):
        def in_cluster(s):
            return (s > lo) & (s <= hi)
        (mval,) = reduce_blocks([(lambda s, k0: jnp.where(in_cluster(s), s, -jnp.inf), -jnp.inf, F32,
                                  jnp.max, jnp.maximum)])
        (idx,) = reduce_blocks(
            [(lambda s, k0: jnp.where(in_cluster(s) & (s == mval), kpos + k0, big_idx),
              big_idx, jnp.int32, jnp.min, jnp.minimum)])
        active = need > 0.0

        def promote(kb, carry):
            for i in range(kblk // strip):
                s = strip_of(kb, i)
                put_strip(kb, i, jnp.where(active & ((kpos + kb * kblk + i * strip) == idx), jnp.inf, s))
            return carry
        lax.fori_loop(0, nkb, promote, 0)
        return jnp.where(active, need - 1.0, need)

    lax.while_loop(fix_cond, fix_body, need0)

    @pl.when(jnp.max(need_zero) > 0.0)
    def _():
        r = lax.broadcasted_iota(jnp.int32, (kblk, kblk), 0)
        c = lax.broadcasted_iota(jnp.int32, (kblk, kblk), 1)
        tri = ((r >= c) if key_axis == 0 else (r <= c)).astype(BF16)

        def tie_blk(kb, seen):
            s = sc_ref[kb]
            z = (s == 0.0) & zero_row
            zb = one_if(z).astype(BF16)
            if key_axis == 0:
                rank = _dot(tri, zb) + seen
                last = rank[kblk - 1:kblk, :]
            else:
                rank = _dot(zb, tri) + seen
                last = rank[:, kblk - 1:kblk]
            sc_ref[kb] = jnp.where(z & (rank <= need_zero), jnp.inf, s)
            return last
        lax.fori_loop(0, nkb, tie_blk, jnp.zeros_like(need_zero))

    return jnp.where(fix, hi, lo)


def _dsa_body(qb_ref, qi_ref, miscq_ref, zb_ref, k_ref, vt_ref, misck_ref, o_ref, sc_ref, s_ref, *, n_sel, qblk):
    j = pl.program_id(1)
    nkb = ((j + 1) * qblk + KEY_BLOCK - 1) // KEY_BLOCK
    qi = qi_ref[...]
    q_h = [qi[:, h * IDX_HD:(h + 1) * IDX_HD].astype(BF16) for h in range(IDX_HEADS)]
    w_t = miscq_ref[...].T[MISC_WI:MISC_WI + IDX_HEADS, :]
    qchunk = (j * qblk + lax.broadcasted_iota(jnp.int32, (1, qblk), 1)) // CHUNK
    krow = lax.broadcasted_iota(jnp.int32, (KEY_BLOCK, 1), 0)

    def score_blk(kb, carry):
        k0 = pl.multiple_of(kb * KEY_BLOCK, KEY_BLOCK)
        for half in range(KEY_BLOCK // SCORE_STRIP):
            r0 = half * SCORE_STRIP
            ki = misck_ref[pl.ds(k0 + r0, SCORE_STRIP), 0:IDX_HD].astype(BF16)
            acc = jnp.zeros((SCORE_STRIP, qblk), F32)
            for h in range(IDX_HEADS):
                acc = acc + w_t[h:h + 1, :] * jnp.maximum(_dot_nt(ki, q_h[h]), 0.0)
            adm = ((k0 + r0 + krow[:SCORE_STRIP]) // CHUNK) <= qchunk
            sc_ref[kb, r0:r0 + SCORE_STRIP, :] = jnp.where(adm, acc, -jnp.inf)
        return carry

    npairs = (nkb + 1) // 2

    def pair_loop(body, init):
        return lax.fori_loop(0, npairs, lambda i, c: body(2 * i + 1, body(2 * i, c)), init)

    pair_loop(score_blk, 0)

    n_adm = ((qchunk + 1) * CHUNK).astype(F32)
    thr = _topk_threshold(sc_ref, nkb, (KEY_BLOCK, qblk), 0, n_sel, n_adm)

    qb = qb_ref[...] * (B_HD ** -0.5 * math.log2(math.e))
    qs = jnp.concatenate([jnp.where(_lane_mask(B_WIDTH, h * B_HD, (h + 1) * B_HD), qb, 0.0)
                          for h in range(B_HEADS)], axis=0).astype(BF16)
    wide = B_HEADS * qblk

    def fold(acc, x, combine):
        for i in range(x.shape[0] // 8):
            acc = combine(acc, x[8 * i:8 * (i + 1)])
        return acc

    def logits_blk(kb, m8):
        k0 = pl.multiple_of(kb * KEY_BLOCK, KEY_BLOCK)
        s = _dot_nt(k_ref[pl.ds(k0, KEY_BLOCK), :].astype(BF16), qs)
        sel = sc_ref[kb] > thr
        s = jnp.where(jnp.concatenate([sel] * B_HEADS, axis=1), s, NEG_BIG)
        s_ref[kb] = s
        return fold(m8, s, jnp.maximum)

    m8 = pair_loop(logits_blk, jnp.full((8, wide), NEG_BIG, F32))
    m = jnp.max(m8, axis=0, keepdims=True)

    def pv_blk(kb, carry):
        l8, accs = carry
        p = jnp.exp2(s_ref[kb] - m)
        pb = p.astype(BF16)
        vt = vt_ref[kb].astype(BF16)
        accs = tuple(accs[h] + _dot(vt[h * B_HD:(h + 1) * B_HD, :], pb[:, h * qblk:(h + 1) * qblk])
                     for h in range(B_HEADS))
        return fold(l8, p, jnp.add), accs

    l8, accs = pair_loop(
        pv_blk, (jnp.zeros((8, wide), F32), tuple(jnp.zeros((B_HD, qblk), F32) for _ in range(B_HEADS))))
    l = jnp.sum(l8, axis=0, keepdims=True)
    o_t = jnp.concatenate([accs[h] / l[:, h * qblk:(h + 1) * qblk] for h in range(B_HEADS)], axis=0)
    o_ref[...] = o_t.T * _silu(zb_ref[...])


def _dsa_prompt(qb, qi, misc, zb, kb, vt, *, bsz, t_len, n_sel, qblk):
    nq = t_len // qblk
    nkb = t_len // KEY_BLOCK
    assert nkb * (KEY_BLOCK // 8) < COUNT_RADIX and nkb % 2 == 0
    qmap = lambda b, j: (b * nq + j, 0)
    kmap = lambda b, j: (b, 0)
    return pl.pallas_call(
        functools.partial(_dsa_body, n_sel=n_sel, qblk=qblk),
        grid=(bsz, nq),
        in_specs=[pl.BlockSpec((qblk, B_WIDTH), qmap),
                  pl.BlockSpec((qblk, IDX_HEADS * IDX_HD), qmap),
                  pl.BlockSpec((qblk, MISC_W), qmap),
                  pl.BlockSpec((qblk, B_WIDTH), qmap),
                  pl.BlockSpec((t_len, B_WIDTH), kmap),
                  pl.BlockSpec((nkb, B_WIDTH, KEY_BLOCK), lambda b, j: (b, 0, 0)),
                  pl.BlockSpec((t_len, MISC_W), kmap)],
        out_specs=pl.BlockSpec((qblk, B_WIDTH), qmap),
        out_shape=jax.ShapeDtypeStruct((bsz * t_len, B_WIDTH), F32),
        scratch_shapes=[pltpu.VMEM((nkb, KEY_BLOCK, qblk), F32),
                        pltpu.VMEM((nkb, KEY_BLOCK, B_HEADS * qblk), F32)],
        compiler_params=_cparams(("parallel", "arbitrary")),
        name="dsa_prompt",
    )(qb, qi, misc, zb, kb, vt, misc)


def _dsa_dec_body(qb_ref, qi_ref, misc_ref, zb_ref, kn_ref, vn_ref, ck_ref, cv_ref, cki_ref, o_ref,
                  sc_ref, thr_ref, m_ref, l_ref, acc_ref, *, p_len, tq, n_sel, kstep):
    step = pl.program_id(1)
    nsteps = p_len // kstep
    sub = kstep // KEY_BLOCK
    npast = p_len // KEY_BLOCK
    nkb = npast + 1
    pad_rows = KEY_BLOCK - tq
    rows = B_HEADS * tq

    @pl.when(step == 0)
    def _():
        qi = qi_ref[...]
        misc = misc_ref[...]
        q2 = jnp.concatenate([qi[:, h * IDX_HD:(h + 1) * IDX_HD] for h in range(IDX_HEADS)],
                             axis=0).astype(BF16)
        w_h = [misc[:, MISC_WI + h:MISC_WI + h + 1] for h in range(IDX_HEADS)]

        def scores(kblk):
            lg = _dot_nt(q2, kblk.astype(BF16))
            acc = jnp.zeros((tq, kblk.shape[0]), F32)
            for h in range(IDX_HEADS):
                acc = acc + w_h[h] * jnp.maximum(lg[h * tq:(h + 1) * tq, :], 0.0)
            return acc

        def score_blk(i, carry):
            k0 = pl.multiple_of(i * kstep, kstep)
            sc = scores(cki_ref[pl.ds(k0, kstep), :])
            for u in range(sub):
                sc_ref[i * sub + u] = sc[:, u * KEY_BLOCK:(u + 1) * KEY_BLOCK]
            return carry

        lax.fori_loop(0, nsteps, score_blk, 0)
        lane = lax.broadcasted_iota(jnp.int32, (1, KEY_BLOCK), 1)
        ki_new = jnp.concatenate([misc[:, 0:IDX_HD], jnp.zeros((pad_rows, IDX_HD), F32)], axis=0)
        sc_ref[npast] = jnp.where(lane < tq, scores(ki_new), -jnp.inf)
        n_adm = jnp.full((tq, 1), float(p_len + tq), F32)
        thr_ref[...] = _topk_threshold(sc_ref, nkb, (tq, KEY_BLOCK), 1, n_sel, n_adm)
        m_ref[...] = jnp.full((rows, 1), NEG_BIG, F32)
        l_ref[...] = jnp.zeros((rows, 1), F32)
        acc_ref[...] = jnp.zeros((rows, B_WIDTH), F32)

    masks = [_lane_mask(B_WIDTH, h * B_HD, (h + 1) * B_HD) for h in range(B_HEADS)]
    qb = qb_ref[...] * (B_HD ** -0.5 * math.log2(math.e))
    qs = jnp.concatenate([jnp.where(masks[h], qb, 0.0) for h in range(B_HEADS)], axis=0).astype(BF16)
    thr = thr_ref[...]

    def fold_in(kblk, vblk, sel):
        sel4 = jnp.concatenate([sel] * B_HEADS, axis=0)
        s = jnp.where(sel4, _dot_nt(qs, kblk.astype(BF16)), NEG_BIG)
        m_old = m_ref[...]
        m_new = jnp.maximum(m_old, jnp.max(s, axis=-1, keepdims=True))
        alpha = jnp.exp2(m_old - m_new)
        p = jnp.exp2(s - m_new)
        l_ref[...] = alpha * l_ref[...] + jnp.sum(p, axis=-1, keepdims=True)
        acc_ref[...] = alpha * acc_ref[...] + _dot(p.astype(BF16), vblk.astype(BF16))
        m_ref[...] = m_new

    sel = jnp.concatenate([sc_ref[step * sub + u] > thr for u in range(sub)], axis=1)
    fold_in(ck_ref[...].reshape(kstep, B_WIDTH), cv_ref[...].reshape(kstep, B_WIDTH), sel)

    @pl.when(step == nsteps - 1)
    def _():
        zpad = jnp.zeros((pad_rows, B_WIDTH), F32)
        fold_in(jnp.concatenate([kn_ref[...], zpad], axis=0), jnp.concatenate([vn_ref[...], zpad], axis=0),
                sc_ref[npast] > thr)
        res = acc_ref[...] / l_ref[...]
        out = jnp.zeros((tq, B_WIDTH), F32)
        for h in range(B_HEADS):
            out = out + jnp.where(masks[h], res[h * tq:(h + 1) * tq, :], 0.0)
        o_ref[...] = out * _silu(zb_ref[...])


def _dsa_decode(qb, qi, misc, zb, kb, vb, ck, cv, cki, *, bsz, t_len, p_len, n_sel):
    kstep = DEC_KEYS_PER_STEP
    assert p_len // KEY_BLOCK + 1 < COUNT_RADIX and p_len % kstep == 0
    rows = B_HEADS * t_len
    qmap = lambda b, s: (b, 0)
    cmap = lambda b, s: (b, s, 0, 0)
    return pl.pallas_call(
        functools.partial(_dsa_dec_body, p_len=p_len, tq=t_len, n_sel=n_sel, kstep=kstep),
        grid=(bsz, p_len // kstep),
        in_specs=[pl.BlockSpec((t_len, B_WIDTH), qmap),
                  pl.BlockSpec((t_len, IDX_HEADS * IDX_HD), qmap),
                  pl.BlockSpec((t_len, MISC_W), qmap),
                  pl.BlockSpec((t_len, B_WIDTH), qmap),
                  pl.BlockSpec((t_len, B_WIDTH), qmap),
                  pl.BlockSpec((t_len, B_WIDTH), qmap),
                  pl.BlockSpec((None, kstep, B_HEADS, B_HD), cmap),
                  pl.BlockSpec((None, kstep, B_HEADS, B_HD), cmap),
                  pl.BlockSpec((None, p_len, IDX_HD), lambda b, s: (b, 0, 0))],
        out_specs=pl.BlockSpec((t_len, B_WIDTH), qmap),
        out_shape=jax.ShapeDtypeStruct((bsz * t_len, B_WIDTH), F32),
        scratch_shapes=[pltpu.VMEM((p_len // KEY_BLOCK + 1, t_len, KEY_BLOCK), F32),
                        pltpu.VMEM((t_len, 1), F32),
                        pltpu.VMEM((rows, 1), F32),
                        pltpu.VMEM((rows, 1), F32),
                        pltpu.VMEM((rows, B_WIDTH), F32)],
        compiler_params=_cparams(("parallel", "arbitrary")),
        name="dsa_decode",
    )(qb, qi, misc, zb, kb, vb, ck, cv, cki)


def _mem_body(qm_ref, zm_ref, mk_ref, mv_ref, o_ref):
    qm = qm_ref[...] * (M_HD ** -0.5)
    mk = mk_ref[...].reshape(N_MEM, M_WIDTH).astype(BF16)
    mv = mv_ref[...].reshape(N_MEM, M_WIDTH).astype(BF16)
    out = jnp.zeros(qm.shape, F32)
    for h in range(M_HEADS):
        mask = _lane_mask(M_WIDTH, h * M_HD, (h + 1) * M_HD)
        s = _dot_nt(jnp.where(mask, qm, 0.0).astype(BF16), mk)
        p = jnp.exp(s - jnp.max(s, axis=-1, keepdims=True))
        l = jnp.sum(p, axis=-1, keepdims=True)
        out = out + jnp.where(mask, _dot(p.astype(BF16), mv) / l, 0.0)
    o_ref[...] = out * _silu(zm_ref[...])


def _mem_attn(qm, zm, mk, mv, *, bsz, t_len, tq):
    nq = t_len // tq
    qmap = lambda b, j: (b * nq + j, 0)
    mem_spec = pl.BlockSpec((None,) + mk.shape[1:], lambda b, j: (b,) + (0,) * (mk.ndim - 1))
    return pl.pallas_call(
        _mem_body,
        grid=(bsz, nq),
        in_specs=[pl.BlockSpec((tq, M_WIDTH), qmap),
                  pl.BlockSpec((tq, M_WIDTH), qmap),
                  mem_spec, mem_spec],
        out_specs=pl.BlockSpec((tq, M_WIDTH), qmap),
        out_shape=jax.ShapeDtypeStruct((bsz * t_len, M_WIDTH), F32),
        compiler_params=_cparams(("parallel", "parallel")),
        name="mem_attn",
    )(qm, zm, mk, mv)


def _oproj_body(x_ref, oa_ref, ob_ref, om_ref, w_ref, y_ref):
    acc = _dot(oa_ref[...].astype(BF16), w_ref[0:A_WIDTH, :])
    acc = acc + _dot(ob_ref[...].astype(BF16), w_ref[A_WIDTH:A_WIDTH + B_WIDTH, :])
    acc = acc + _dot(om_ref[...].astype(BF16), w_ref[A_WIDTH + B_WIDTH:, :])
    y_ref[...] = x_ref[...] + acc


def _oproj(x, oa, ob, om, w, tm):
    m, d = x.shape
    rmap = lambda i: (i, 0)
    return pl.pallas_call(
        _oproj_body,
        grid=(m // tm,),
        in_specs=[pl.BlockSpec((tm, d), rmap),
                  pl.BlockSpec((tm, A_WIDTH), rmap),
                  pl.BlockSpec((tm, B_WIDTH), rmap),
                  pl.BlockSpec((tm, M_WIDTH), rmap),
                  pl.BlockSpec(w.shape, lambda i: (0, 0))],
        out_specs=pl.BlockSpec((tm, d), rmap),
        out_shape=jax.ShapeDtypeStruct((m, d), F32),
        compiler_params=_cparams(("parallel",)),
        name="oproj",
    )(x, oa, ob, om, w)


def _prep_weights(g_in, w_in, conv_w, a_log, dt_bias, g_o, g_qb, g_kb, g_ki, g_qm, w_out):
    offs = np.concatenate([[0], np.cumsum(IN_SPLITS)])
    col = lambda i: w_in[:, int(offs[i]):int(offs[i + 1])]
    (w_qkv, w_za, w_ba, w_aa, w_qb, w_kb, w_vb, w_zb, w_qi, w_ki, w_wi, w_qm, w_zm) = [col(i) for i in range(13)]
    d = w_in.shape[0]
    w_misc = jnp.concatenate(
        [w_ki, w_wi, w_ba, w_aa, jnp.zeros((d, MISC_W - MISC_A - A_HEADS), w_in.dtype)], axis=1)
    w_all = jnp.concatenate([w_qkv, w_za, w_qb, w_kb, w_vb, w_zb, w_qi, w_qm, w_zm, w_misc],
                            axis=1).astype(BF16)
    groups = [(3 * A_WIDTH, False, ""), (A_WIDTH, False, ""), (B_WIDTH, True, ""),
              (B_WIDTH, True, "heads"), (B_WIDTH, False, "t,heads"), (B_WIDTH, False, ""),
              (IDX_HEADS * IDX_HD, False, ""), (M_WIDTH, True, ""), (M_WIDTH, False, ""),
              (MISC_W, True, "")]
    bd64 = _group_mean_matrix(B_WIDTH, B_HD, B_WIDTH)
    ones256 = jnp.ones((1, B_WIDTH), F32)
    misc_gain = jnp.concatenate([g_ki, jnp.full((IDX_HEADS,), IDX_SCALE, F32),
                                 jnp.ones((MISC_W - MISC_B,), F32)])[None, :]
    misc_nm = (jnp.arange(MISC_W) < IDX_HD).astype(F32)[None, :]
    aux = [bd64, jnp.tile(g_qb, B_HEADS)[None, :], ones256,
           bd64, jnp.tile(g_kb, B_HEADS)[None, :], ones256,
           bd64, jnp.tile(g_qm, M_HEADS)[None, :], ones256,
           _group_mean_matrix(MISC_W, IDX_HD, IDX_HD), misc_gain, misc_nm]
    pad_a = lambda v: jnp.zeros((1, MISC_W), F32).at[0, MISC_A:MISC_A + A_HEADS].set(v)
    return dict(g_in=g_in[None, :], w_all=w_all, groups=groups, aux=aux, conv_w=conv_w,
                avec=pad_a(-jnp.exp(a_log)), dtvec=pad_a(dt_bias), g_o=g_o[None, :],
                w_out=w_out.astype(BF16))


def _layer(x, mem_k, mem_v, conv_buf, s0, past, gdn_chunk, gdn_rows, wp):
    bsz, t_len, d = x.shape
    m = bsz * t_len
    x2 = x.reshape(m, d)
    tm = min(KEY_BLOCK, m)
    (qkv, za, qb, kb, vb, zb, qi, qm, zm, misc, kb_heads, vt, vb_heads) = _proj(
        x2, wp["g_in"], wp["w_all"], wp["groups"], wp["aux"], tm)
    cbuf8 = jnp.concatenate([jnp.zeros((bsz, 8 - (CONV_W - 1), 3 * A_WIDTH), F32), conv_buf], axis=1)
    oa, conv_new, s_new = _gdn(qkv, za, misc, cbuf8, s0, wp["conv_w"], wp["avec"], wp["dtvec"], wp["g_o"],
                               bsz=bsz, t_len=t_len, chunk=gdn_chunk, rows=gdn_rows)
    if past is None:
        n_sel = min(TOPK_MAX, t_len // 4)
        ob = _dsa_prompt(qb, qi, misc, zb, kb, vt, bsz=bsz, t_len=t_len, n_sel=n_sel, qblk=DSA_QBLK)
    else:
        ck, cv, cki = past
        p_len = ck.shape[1]
        assert (p_len + t_len - 1) // CHUNK <= p_len // CHUNK and p_len % KEY_BLOCK == 0
        n_sel = min(TOPK_MAX, (p_len + t_len) // 4)
        ob = _dsa_decode(qb, qi, misc, zb, kb, vb, ck, cv, cki, bsz=bsz, t_len=t_len, p_len=p_len,
                         n_sel=n_sel)
    om = _mem_attn(qm, zm, mem_k, mem_v, bsz=bsz, t_len=t_len, tq=min(256, t_len))
    y = _oproj(x2, oa, ob, om, wp["w_out"], min(512, m))
    return (y.reshape(bsz, t_len, d), conv_new, s_new,
            kb_heads.reshape(bsz, t_len, B_HEADS, B_HD), vb_heads.reshape(bsz, t_len, B_HEADS, B_HD),
            misc[:, :IDX_HD].reshape(bsz, t_len, IDX_HD))


def _memory_kv(mem, g_mem, w_mem_kv, g_km):
    bsz, n_mem, d = mem.shape
    assert (M_HEADS, M_HD) == (B_HEADS, B_HD)
    groups = [(M_WIDTH, True, "heads"), (M_WIDTH, False, "heads")]
    aux = [_group_mean_matrix(M_WIDTH, M_HD, M_WIDTH), jnp.tile(g_km, M_HEADS)[None, :],
           jnp.ones((1, M_WIDTH), F32)]
    mk, mv, mk_heads, mv_heads = _proj(mem.reshape(bsz * n_mem, d), g_mem[None, :], w_mem_kv.astype(BF16),
                                       groups, aux, 256)
    return (mk.reshape(bsz, n_mem, M_WIDTH), mv.reshape(bsz, n_mem, M_WIDTH),
            mk_heads.reshape(bsz, n_mem, M_HEADS, M_HD), mv_heads.reshape(bsz, n_mem, M_HEADS, M_HD))


def kernel(x_prompt, x_sample, state_conv_A, state_ssm_A, cache_k_B, cache_v_B, cache_kidx_B, cache_mem_k,
           cache_mem_v, mem_prompt, g_in, w_in, conv_w_A, a_log_A, dt_bias_A, g_o_A, g_q_B, g_k_B, g_kidx_B,
           g_mem, w_mem_kv, g_q_M, g_k_M, w_out):
    depth = w_in.shape[0]
    assert depth == 1
    l = 0
    bp, t_p, _ = x_prompt.shape
    bs, t_s, _ = x_sample.shape
    wp = _prep_weights(g_in[l], w_in[l], conv_w_A[l], a_log_A[l], dt_bias_A[l], g_o_A[l], g_q_B[l], g_k_B[l],
                       g_kidx_B[l], g_q_M[l], w_out[l])
    mk, mv, mk_heads, mv_heads = _memory_kv(mem_prompt, g_mem[l], w_mem_kv[l], g_k_M[l])
    zero_conv = jnp.zeros((bp, CONV_W - 1, 3 * A_WIDTH), F32)
    zero_ssm = jnp.zeros((bp, A_HEADS, A_DK, A_DV), F32)
    yp, c1, s1, k1, v1, ki1 = _layer(x_prompt, mk, mv, zero_conv, zero_ssm, None, CHUNK, 4 * CHUNK, wp)
    ys, c2, s2, k2, v2, ki2 = _layer(
        x_sample, cache_mem_k[l], cache_mem_v[l],
        state_conv_A[l], state_ssm_A[l], (cache_k_B[l], cache_v_B[l], cache_kidx_B[l]), t_s, t_s, wp)
    st = lambda a: a[None]
    return (yp, ys, st(c1), st(s1), st(k1), st(v1), st(ki1),
            st(mk_heads), st(mv_heads),
            st(c2), st(s2), st(k2), st(v2), st(ki2))
```

```python
import functools
import math

import numpy as np
import jax
import jax.numpy as jnp
from jax import lax
from jax.experimental import pallas as pl
from jax.experimental.pallas import tpu as pltpu

F32 = jnp.float32
BF16 = jnp.bfloat16
HIGHEST = lax.Precision.HIGHEST

D_MODEL = 1024
CHUNK = 64
A_HEADS = 4
A_DK = 128
A_DV = 128
A_WIDTH = A_HEADS * A_DV
CONV_W = 4
B_HEADS = 4
B_HD = 64
B_WIDTH = B_HEADS * B_HD
IDX_HEADS = 8
IDX_HD = 32
IDX_SCALE = (IDX_HEADS ** -0.5) * (IDX_HD ** -0.5)
TOPK_MAX = 256
Q_BLOCK = 128
N_MEM = 256
M_HEADS = 4
M_HD = 64
M_WIDTH = M_HEADS * M_HD
EPS = 1e-6
IN_SPLITS = (3 * A_WIDTH, A_WIDTH, A_HEADS, A_HEADS,
             B_WIDTH, B_WIDTH, B_WIDTH, B_WIDTH, IDX_HEADS * IDX_HD, IDX_HD, IDX_HEADS,
             M_WIDTH, M_WIDTH)

LANES = 128
KEY_BLOCK = 256
DSA_QBLK = 256
DEC_KEYS_PER_STEP = 1024
SCORE_STRIP = 128
MISC_W = LANES
MISC_WI = IDX_HD
MISC_B = IDX_HD + IDX_HEADS
MISC_A = MISC_B + A_HEADS
VMEM_LIMIT = 48 * 1024 * 1024
NEG_BIG = -1e30
BISECT_PROBES = 3
COUNT_RADIX = 1024
BISECT_MAX_ITERS = 20


def _cparams(sem):
    return pltpu.CompilerParams(dimension_semantics=sem, vmem_limit_bytes=VMEM_LIMIT)


def _dot(a, b):
    return jnp.dot(a, b, preferred_element_type=F32)


def _dot_nt(a, b, precision=None):
    return lax.dot_general(a, b, (((1,), (1,)), ((), ())), precision=precision,
                           preferred_element_type=F32)


def _split2(x):
    hi = x.astype(BF16)
    return hi, (x - hi.astype(F32)).astype(BF16)


def _mm_split(a, b):
    ah, al = a
    bh, bl = b
    n = ah.shape[0]
    t = _dot(jnp.concatenate([ah, al], axis=0), bh)
    return t[:n] + t[n:] + _dot(ah, bl)


def _silu(x):
    return x * jax.nn.sigmoid(x)


def _lane_mask(width, lo, hi):
    lane = lax.broadcasted_iota(jnp.int32, (1, width), 1)
    return (lane >= lo) & (lane < hi)


def _proj_body(*refs, groups):
    x_ref, g_ref, w_ref = refs[0], refs[1], refs[2]
    n_aux = 3 * sum(1 for _, normed, _ in groups if normed)
    aux = refs[3:3 + n_aux]
    outs = refs[3 + n_aux:3 + n_aux + len(groups)]
    outs_t = list(refs[3 + n_aux + len(groups):])
    x = x_ref[...]
    h = x * lax.rsqrt(jnp.mean(x * x, axis=-1, keepdims=True) + EPS) * g_ref[...]
    hb = h.astype(BF16)
    off = 0
    ai = 0
    for (width, normed, transposed), o_ref in zip(groups, outs):
        y = _dot(hb, w_ref[:, off:off + width])
        if normed:
            bd_ref, gain_ref, nm_ref = aux[ai], aux[ai + 1], aux[ai + 2]
            ai += 3
            sq = y * y
            hi = sq.astype(BF16)
            lo = (sq - hi.astype(F32)).astype(BF16)
            ms = _dot(hi, bd_ref[...]) + _dot(lo, bd_ref[...])
            scale = jnp.where(nm_ref[...] > 0.0, lax.rsqrt(ms + EPS), 1.0)
            y = y * scale * gain_ref[...]
        o_ref[...] = y
        if transposed:
            outs_t.pop(0)[...] = y.T
        off += width


def _group_mean_matrix(width, group, n_lanes):
    i = np.arange(width)
    m = ((i[:, None] // group) == (i[None, :] // group)) & (i[:, None] < n_lanes) & (i[None, :] < n_lanes)
    return jnp.asarray(m.astype(np.float32) / group, dtype=BF16)


def _proj(x, g, w, groups, aux, tm, t_len):
    m, d = x.shape
    nw = w.shape[1]
    tiles = t_len // tm
    in_specs = [pl.BlockSpec((tm, d), lambda i: (i, 0)),
                pl.BlockSpec((1, d), lambda i: (0, 0)),
                pl.BlockSpec((d, nw), lambda i: (0, 0))]
    for a in aux:
        in_specs.append(pl.BlockSpec(a.shape, lambda i: (0, 0)))
    out_shape = [jax.ShapeDtypeStruct((m, wd), F32) for wd, _, _ in groups]
    out_specs = [pl.BlockSpec((tm, wd), lambda i: (i, 0)) for wd, _, _ in groups]
    for wd, _, transposed in groups:
        if transposed:
            out_shape.append(jax.ShapeDtypeStruct((m // t_len, wd, t_len), F32))
            out_specs.append(pl.BlockSpec((None, wd, tm), lambda i: (i // tiles, 0, i % tiles)))
    return pl.pallas_call(
        functools.partial(_proj_body, groups=tuple(groups)),
        grid=(m // tm,),
        in_specs=in_specs, out_specs=out_specs, out_shape=out_shape,
        compiler_params=_cparams(("parallel",)),
        name="proj",
    )(x, g, w, *aux)


def _gdn_body(qkv_ref, za_ref, misc_ref, cbuf_ref, s0_ref, cw_ref, avec_ref, dtvec_ref, go_ref,
              o_ref, convnew_ref, sfin_ref,
              xp_ref, q_sc, k_sc, v_sc, g_sc, b_sc, s_sc, *, chunk, rows):
    t = pl.program_id(1)
    c = chunk

    @pl.when(t == 0)
    def _():
        xp_ref[0:8, :] = cbuf_ref[0]
        s_sc[...] = s0_ref[0]

    xp_ref[8:8 + rows, :] = qkv_ref[...]
    cw = cw_ref[...]
    y = xp_ref[5:5 + rows, :] * cw[0:1, :]
    for j in range(1, CONV_W):
        y = y + xp_ref[5 + j:5 + j + rows, :] * cw[j:j + 1, :]
    y = _silu(y)
    convnew_ref[0] = xp_ref[rows + 5:rows + 8, :]
    xp_ref[0:8, :] = xp_ref[rows:rows + 8, :]

    for h in range(A_HEADS):
        qh = y[:, h * A_DK:(h + 1) * A_DK]
        kh = y[:, A_WIDTH + h * A_DK:A_WIDTH + (h + 1) * A_DK]
        q_sc[:, h * A_DK:(h + 1) * A_DK] = qh * (
            lax.rsqrt(jnp.sum(qh * qh, axis=-1, keepdims=True) + EPS) * (A_DK ** -0.5))
        k_sc[:, h * A_DK:(h + 1) * A_DK] = kh * lax.rsqrt(jnp.sum(kh * kh, axis=-1, keepdims=True) + EPS)
    v_sc[...] = y[:, 2 * A_WIDTH:3 * A_WIDTH]
    misc = misc_ref[...]
    b_sc[...] = jax.nn.sigmoid(misc)
    g_sc[...] = avec_ref[...] * jax.nn.softplus(misc + dtvec_ref[...])

    ri = lax.broadcasted_iota(jnp.int32, (c, c), 0)
    ci = lax.broadcasted_iota(jnp.int32, (c, c), 1)
    tri = ri >= ci
    strict = ri > ci
    eye_f = (ri == ci).astype(F32)
    tri3 = (lax.broadcasted_iota(jnp.int32, (c, 3 * c), 0)
            >= lax.broadcasted_iota(jnp.int32, (c, 3 * c), 1) % c).astype(BF16)
    go = go_ref[...]
    n_chunks = rows // c

    problems = [(ic, h) for ic in range(n_chunks) for h in range(A_HEADS)]
    gc_all, gc_t = [], []
    for ic in range(n_chunks):
        g = g_sc[ic * c:(ic + 1) * c, :]
        g1 = g.astype(BF16)
        r1 = g - g1.astype(F32)
        g2 = r1.astype(BF16)
        g3 = (r1 - g2.astype(F32)).astype(BF16)
        gc_all.append(_dot(tri3, jnp.concatenate([g1, g2, g3], axis=0)))
    for ic in range(n_chunks):
        gc_t.append(gc_all[ic].T)

    def load(ref, ic, h):
        return ref[ic * c:(ic + 1) * c, h * A_DK:(h + 1) * A_DK]

    col = lambda a, lane: a[:, lane:lane + 1]
    k_bf = [load(k_sc, ic, h).astype(BF16) for ic, h in problems]
    kb = [load(k_sc, ic, h) * col(b_sc[ic * c:(ic + 1) * c, :], MISC_B + h) for ic, h in problems]
    kk = [_dot_nt(kb[i].astype(BF16), k_bf[i]) for i in range(len(problems))]
    qk_raw = [_dot_nt(load(q_sc, ic, h).astype(BF16), k_bf[i]) for i, (ic, h) in enumerate(problems)]
    decay = []
    for ic, h in problems:
        diff = col(gc_all[ic], MISC_A + h) - gc_t[ic][MISC_A + h:MISC_A + h + 1, :]
        decay.append(jnp.where(tri, jnp.exp(jnp.where(tri, diff, 0.0)), 0.0))
    lmat = [jnp.where(strict, kk[i] * decay[i], 0.0) for i in range(len(problems))]
    qk = [(qk_raw[i] * decay[i]).astype(BF16) for i in range(len(problems))]
    x_inv = [eye_f - m for m in lmat]
    p_split = [_split2(m) for m in lmat]
    for _ in range(int(math.log2(c)) - 1):
        p_split = [_split2(_mm_split(ps, ps)) for ps in p_split]
        x_inv = [x + _mm_split(_split2(x), ps) for x, ps in zip(x_inv, p_split)]
    sol = []
    for i, (ic, h) in enumerate(problems):
        egc = jnp.exp(col(gc_all[ic], MISC_A + h))
        beta = col(b_sc[ic * c:(ic + 1) * c, :], MISC_B + h)
        rhs = jnp.concatenate([load(v_sc, ic, h) * beta, kb[i] * egc], axis=-1).astype(BF16)
        th, tl = _split2(x_inv[i])
        sol.append(_dot(th, rhs) + _dot(tl, rhs))
    pre = []
    for i, (ic, h) in enumerate(problems):
        gc = col(gc_all[ic], MISC_A + h)
        g_last = gc[c - 1:c, :]
        k_dec_t = (load(k_sc, ic, h) * jnp.exp(g_last - gc)).T.astype(BF16)
        wq = jnp.concatenate([sol[i][:, A_DV:], load(q_sc, ic, h) * jnp.exp(gc)], axis=0).astype(BF16)
        pre.append((sol[i][:, :A_DV], wq, k_dec_t, jnp.exp(g_last)))

    heads = range(A_HEADS)
    s_cur = [s_sc[h] for h in heads]
    for ic in range(n_chunks):
        pr = [pre[ic * A_HEADS + h] for h in heads]
        ws = [_dot(pr[h][1], s_cur[h].astype(BF16)) for h in heads]
        ub = [(pr[h][0] - ws[h][:c]).astype(BF16) for h in heads]
        o = [ws[h][c:] + _dot(qk[ic * A_HEADS + h], ub[h]) for h in heads]
        s_cur = [s_cur[h] * pr[h][3] + _dot(pr[h][2], ub[h]) for h in heads]
        for h in heads:
            on = o[h] * lax.rsqrt(jnp.mean(o[h] * o[h], axis=-1, keepdims=True) + EPS) * go
            o_ref[ic * c:(ic + 1) * c, h * A_DV:(h + 1) * A_DV] = on * _silu(load(za_ref, ic, h))
    for h in heads:
        s_sc[h] = s_cur[h]
    sfin_ref[0] = s_sc[...]


def _gdn(qkv, za, misc, cbuf8, s0, conv_w, avec, dtvec, g_o, *, bsz, t_len, chunk, rows):
    nt = t_len // rows
    w3 = 3 * A_WIDTH
    row_map = lambda b, t: (b * nt + t, 0)
    const2 = lambda b, t: (0, 0)
    return pl.pallas_call(
        functools.partial(_gdn_body, chunk=chunk, rows=rows),
        grid=(bsz, nt),
        in_specs=[pl.BlockSpec((rows, w3), row_map),
                  pl.BlockSpec((rows, A_WIDTH), row_map),
                  pl.BlockSpec((rows, MISC_W), row_map),
                  pl.BlockSpec((1, 8, w3), lambda b, t: (b, 0, 0)),
                  pl.BlockSpec((1, A_HEADS, A_DK, A_DV), lambda b, t: (b, 0, 0, 0)),
                  pl.BlockSpec((CONV_W, w3), const2),
                  pl.BlockSpec((1, MISC_W), const2),
                  pl.BlockSpec((1, MISC_W), const2),
                  pl.BlockSpec((1, A_DV), const2)],
        out_specs=[pl.BlockSpec((rows, A_WIDTH), row_map),
                   pl.BlockSpec((1, CONV_W - 1, w3), lambda b, t: (b, 0, 0)),
                   pl.BlockSpec((1, A_HEADS, A_DK, A_DV), lambda b, t: (b, 0, 0, 0))],
        out_shape=[jax.ShapeDtypeStruct((bsz * t_len, A_WIDTH), F32),
                   jax.ShapeDtypeStruct((bsz, CONV_W - 1, w3), F32),
                   jax.ShapeDtypeStruct((bsz, A_HEADS, A_DK, A_DV), F32)],
        scratch_shapes=[pltpu.VMEM((rows + 8, w3), F32),
                        pltpu.VMEM((rows, A_WIDTH), F32),
                        pltpu.VMEM((rows, A_WIDTH), F32),
                        pltpu.VMEM((rows, A_WIDTH), F32),
                        pltpu.VMEM((rows, MISC_W), F32),
                        pltpu.VMEM((rows, MISC_W), F32),
                        pltpu.VMEM((A_HEADS, A_DK, A_DV), F32)],
        compiler_params=_cparams(("parallel", "arbitrary")),
        name="gdn",
    )(qkv, za, misc, cbuf8, s0, conv_w, avec, dtvec, g_o)


def _topk_threshold(sc_ref, nkb, shape, key_axis, n_sel, n_adm):
    kf = float(n_sel)
    kblk = shape[key_axis]

    if key_axis == 0:
        strip = min(kblk, 32)
        part_shape = (strip, shape[1])
        strip_of = lambda kb, i: sc_ref[kb, i * strip:(i + 1) * strip, :]

        def put_strip(kb, i, v):
            sc_ref[kb, i * strip:(i + 1) * strip, :] = v
    else:
        strip = kblk
        part_shape = shape
        strip_of = lambda kb, i: sc_ref[kb]

        def put_strip(kb, i, v):
            sc_ref[kb] = v
    kpos = lax.broadcasted_iota(jnp.int32, part_shape, key_axis)

    def reduce_blocks(specs):
        def body(kb, accs):
            accs = list(accs)
            for i in range(kblk // strip):
                s = strip_of(kb, i)
                k0 = kb * kblk + i * strip
                accs = [cmb(acc, fn(s, k0)) for acc, (fn, _, _, _, cmb) in zip(accs, specs)]
            return tuple(accs)
        init = tuple(jnp.full(part_shape, i, dt) for _, i, dt, _, _ in specs)
        accs = lax.fori_loop(0, nkb, body, init)
        return [acc if op is None else op(acc, axis=key_axis, keepdims=True)
                for acc, (_, _, _, op, _) in zip(accs, specs)]

    one_if = lambda c: jnp.where(c, 1.0, 0.0)
    vmax, vmin, cpos, czero = reduce_blocks([
        (lambda s, k0: s, -jnp.inf, F32, jnp.max, jnp.maximum),
        (lambda s, k0: jnp.where(s == -jnp.inf, jnp.inf, s), jnp.inf, F32, jnp.min, jnp.minimum),
        (lambda s, k0: one_if(s > 0.0), 0.0, F32, jnp.sum, jnp.add),
        (lambda s, k0: one_if(s == 0.0), 0.0, F32, jnp.sum, jnp.add)])

    long_row = n_adm > kf
    pos_row = long_row & (cpos >= kf)
    zero_row = long_row & (cpos < kf) & (cpos + czero >= kf)
    neg_row = long_row & (cpos + czero < kf)
    lo0 = jnp.where(pos_row | zero_row, 0.0, vmin - jnp.maximum(1.0, jnp.abs(vmin)))
    cnt_lo0 = jnp.where(pos_row, cpos, jnp.where(zero_row, kf, n_adm))
    hi0 = jnp.where(neg_row, 0.0, vmax)
    cnt_hi0 = jnp.where(neg_row, cpos, 0.0)
    need_zero = jnp.where(zero_row, kf - cpos, 0.0)

    def cond(st):
        it, go = st[0], st[1]
        return (it < BISECT_MAX_ITERS) & (go > 0)

    def body(st):
        it, _, lo, hi, cnt_lo, cnt_hi = st
        fracs = [(t + 1.0) / (BISECT_PROBES + 1.0) for t in range(BISECT_PROBES)]
        mids = [lo * (1.0 - f) + hi * f for f in fracs]
        for t in range(1, BISECT_PROBES):
            mids[t] = jnp.maximum(mids[t], mids[t - 1])
        codes = [sum(COUNT_RADIX ** u for u in range(t + 1)) for t in range(BISECT_PROBES)]

        def encode(s, k0):
            e = jnp.zeros(s.shape, jnp.int32)
            for mid, code in zip(mids, codes):
                e = jnp.where(s > mid, code, e)
            return e
        (packed,) = reduce_blocks([(encode, 0, jnp.int32, None, jnp.add)])
        cnts = []
        for t in range(BISECT_PROBES):
            digit = packed & (COUNT_RADIX - 1) if t < BISECT_PROBES - 1 else packed
            packed = packed >> COUNT_RADIX.bit_length() - 1
            cnts.append(jnp.sum(digit.astype(F32), axis=key_axis, keepdims=True))
        lo_n, cnt_lo_n, hi_n, cnt_hi_n = lo, cnt_lo, hi, cnt_hi
        for mid, cm in zip(mids, cnts):
            up = cm >= kf
            lo_n = jnp.where(up, mid, lo_n)
            cnt_lo_n = jnp.where(up, cm, cnt_lo_n)
        for mid, cm in zip(mids[::-1], cnts[::-1]):
            dn = cm < kf
            hi_n = jnp.where(dn, mid, hi_n)
            cnt_hi_n = jnp.where(dn, cm, cnt_hi_n)
        go = (jnp.max(cnt_lo_n) > kf).astype(jnp.int32)
        return it + 1, go, lo_n, hi_n, cnt_lo_n, cnt_hi_n

    go0 = (jnp.max(cnt_lo0) > kf).astype(jnp.int32)
    _, go, lo, hi, cnt_lo, cnt_hi = lax.while_loop(
        cond, body, (jnp.int32(0), go0, lo0, hi0, cnt_lo0, cnt_hi0))

    fix = cnt_lo > kf
    need0 = jnp.where(fix, kf - cnt_hi, 0.0)
    big_idx = 2 ** 30

    def fix_cond(need):
        return jnp.max(need) > 0.0

    def fix_body(need):
        def in_cluster(s):
            return (s > lo) & (s <= hi)
        (mval,) = reduce_blocks([(lambda s, k0: jnp.where(in_cluster(s), s, -jnp.inf), -jnp.inf, F32,
                                  jnp.max, jnp.maximum)])
        (idx,) = reduce_blocks(
            [(lambda s, k0: jnp.where(in_cluster(s) & (s == mval), kpos + k0, big_idx),
              big_idx, jnp.int32, jnp.min, jnp.minimum)])
        active = need > 0.0

        def promote(kb, carry):
            for i in range(kblk // strip):
                s = strip_of(kb, i)
                put_strip(kb, i, jnp.where(active & ((kpos + kb * kblk + i * strip) == idx), jnp.inf, s))
            return carry
        lax.fori_loop(0, nkb, promote, 0)
        return jnp.where(active, need - 1.0, need)

    lax.while_loop(fix_cond, fix_body, need0)

    @pl.when(jnp.max(need_zero) > 0.0)
    def _():
        r = lax.broadcasted_iota(jnp.int32, (kblk, kblk), 0)
        c = lax.broadcasted_iota(jnp.int32, (kblk, kblk), 1)
        tri = ((r >= c) if key_axis == 0 else (r <= c)).astype(BF16)

        def tie_blk(kb, seen):
            s = sc_ref[kb]
            z = (s == 0.0) & zero_row
            zb = one_if(z).astype(BF16)
            if key_axis == 0:
                rank = _dot(tri, zb) + seen
                last = rank[kblk - 1:kblk, :]
            else:
                rank = _dot(zb, tri) + seen
                last = rank[:, kblk - 1:kblk]
            sc_ref[kb] = jnp.where(z & (rank <= need_zero), jnp.inf, s)
            return last
        lax.fori_loop(0, nkb, tie_blk, jnp.zeros_like(need_zero))

    return jnp.where(fix, hi, lo)


def _dsa_body(qb_ref, qi_ref, miscq_ref, zb_ref, k_ref, vt_ref, misck_ref, o_ref, sc_ref, s_ref, *, n_sel, qblk):
    j = pl.program_id(1)
    nkb = ((j + 1) * qblk + KEY_BLOCK - 1) // KEY_BLOCK
    qi = qi_ref[...]
    q_h = [qi[:, h * IDX_HD:(h + 1) * IDX_HD].astype(BF16) for h in range(IDX_HEADS)]
    w_t = miscq_ref[...].T[MISC_WI:MISC_WI + IDX_HEADS, :]
    qchunk = (j * qblk + lax.broadcasted_iota(jnp.int32, (1, qblk), 1)) // CHUNK
    krow = lax.broadcasted_iota(jnp.int32, (KEY_BLOCK, 1), 0)

    def score_blk(kb, carry):
        k0 = pl.multiple_of(kb * KEY_BLOCK, KEY_BLOCK)
        for half in range(KEY_BLOCK // SCORE_STRIP):
            r0 = half * SCORE_STRIP
            ki = misck_ref[pl.ds(k0 + r0, SCORE_STRIP), 0:IDX_HD].astype(BF16)
            acc = jnp.zeros((SCORE_STRIP, qblk), F32)
            for h in range(IDX_HEADS):
                acc = acc + w_t[h:h + 1, :] * jnp.maximum(_dot_nt(ki, q_h[h]), 0.0)
            adm = ((k0 + r0 + krow[:SCORE_STRIP]) // CHUNK) <= qchunk
            sc_ref[kb, r0:r0 + SCORE_STRIP, :] = jnp.where(adm, acc, -jnp.inf)
        return carry

    npairs = (nkb + 1) // 2

    def pair_loop(body, init):
        return lax.fori_loop(0, npairs, lambda i, c: body(2 * i + 1, body(2 * i, c)), init)

    pair_loop(score_blk, 0)

    n_adm = ((qchunk + 1) * CHUNK).astype(F32)
    thr = _topk_threshold(sc_ref, nkb, (KEY_BLOCK, qblk), 0, n_sel, n_adm)

    qb = qb_ref[...] * (B_HD ** -0.5 * math.log2(math.e))
    qs = jnp.concatenate([jnp.where(_lane_mask(B_WIDTH, h * B_HD, (h + 1) * B_HD), qb, 0.0)
                          for h in range(B_HEADS)], axis=0).astype(BF16)
    wide = B_HEADS * qblk

    def fold(acc, x, combine):
        for i in range(x.shape[0] // 8):
            acc = combine(acc, x[8 * i:8 * (i + 1)])
        return acc

    def logits_blk(kb, m8):
        k0 = pl.multiple_of(kb * KEY_BLOCK, KEY_BLOCK)
        s = _dot_nt(k_ref[pl.ds(k0, KEY_BLOCK), :].astype(BF16), qs)
        sel = sc_ref[kb] > thr
        s = jnp.where(jnp.concatenate([sel] * B_HEADS, axis=1), s, NEG_BIG)
        s_ref[kb] = s
        return fold(m8, s, jnp.maximum)

    m8 = pair_loop(logits_blk, jnp.full((8, wide), NEG_BIG, F32))
    m = jnp.max(m8, axis=0, keepdims=True)

    def pv_blk(kb, carry):
        l8, accs = carry
        p = jnp.exp2(s_ref[kb] - m)
        pb = p.astype(BF16)
        k0 = pl.multiple_of(kb * KEY_BLOCK, KEY_BLOCK)
        vt = vt_ref[:, pl.ds(k0, KEY_BLOCK)].astype(BF16)
        accs = tuple(accs[h] + _dot(vt[h * B_HD:(h + 1) * B_HD, :], pb[:, h * qblk:(h + 1) * qblk])
                     for h in range(B_HEADS))
        return fold(l8, p, jnp.add), accs

    l8, accs = pair_loop(
        pv_blk, (jnp.zeros((8, wide), F32), tuple(jnp.zeros((B_HD, qblk), F32) for _ in range(B_HEADS))))
    l = jnp.sum(l8, axis=0, keepdims=True)
    o_t = jnp.concatenate([accs[h] / l[:, h * qblk:(h + 1) * qblk] for h in range(B_HEADS)], axis=0)
    o_ref[...] = o_t.T * _silu(zb_ref[...])


def _dsa_prompt(qb, qi, misc, zb, kb, vt, *, bsz, t_len, n_sel, qblk):
    nq = t_len // qblk
    nkb = t_len // KEY_BLOCK
    assert nkb * (KEY_BLOCK // 8) < COUNT_RADIX and nkb % 2 == 0
    qmap = lambda b, j: (b * nq + j, 0)
    kmap = lambda b, j: (b, 0)
    return pl.pallas_call(
        functools.partial(_dsa_body, n_sel=n_sel, qblk=qblk),
        grid=(bsz, nq),
        in_specs=[pl.BlockSpec((qblk, B_WIDTH), qmap),
                  pl.BlockSpec((qblk, IDX_HEADS * IDX_HD), qmap),
                  pl.BlockSpec((qblk, MISC_W), qmap),
                  pl.BlockSpec((qblk, B_WIDTH), qmap),
                  pl.BlockSpec((t_len, B_WIDTH), kmap),
                  pl.BlockSpec((None, B_WIDTH, t_len), lambda b, j: (b, 0, 0)),
                  pl.BlockSpec((t_len, MISC_W), kmap)],
        out_specs=pl.BlockSpec((qblk, B_WIDTH), qmap),
        out_shape=jax.ShapeDtypeStruct((bsz * t_len, B_WIDTH), F32),
        scratch_shapes=[pltpu.VMEM((nkb, KEY_BLOCK, qblk), F32),
                        pltpu.VMEM((nkb, KEY_BLOCK, B_HEADS * qblk), F32)],
        compiler_params=_cparams(("parallel", "arbitrary")),
        name="dsa_prompt",
    )(qb, qi, misc, zb, kb, vt, misc)


def _dsa_dec_body(qb_ref, qi_ref, misc_ref, zb_ref, kn_ref, vn_ref, ck_ref, cv_ref, cki_ref, o_ref,
                  sc_ref, thr_ref, m_ref, l_ref, acc_ref, *, p_len, tq, n_sel, kstep):
    step = pl.program_id(1)
    nsteps = p_len // kstep
    sub = kstep // KEY_BLOCK
    npast = p_len // KEY_BLOCK
    nkb = npast + 1
    pad_rows = KEY_BLOCK - tq
    rows = B_HEADS * tq

    @pl.when(step == 0)
    def _():
        qi = qi_ref[...]
        misc = misc_ref[...]
        q2 = jnp.concatenate([qi[:, h * IDX_HD:(h + 1) * IDX_HD] for h in range(IDX_HEADS)],
                             axis=0).astype(BF16)
        w_h = [misc[:, MISC_WI + h:MISC_WI + h + 1] for h in range(IDX_HEADS)]

        def head_sum(lg):
            acc = jnp.zeros((tq, lg.shape[1]), F32)
            for h in range(IDX_HEADS):
                acc = acc + w_h[h] * jnp.maximum(lg[h * tq:(h + 1) * tq, :], 0.0)
            return acc

        for i in range(nsteps):
            sc = head_sum(_dot(q2, cki_ref[:, i * kstep:(i + 1) * kstep].astype(BF16)))
            for u in range(sub):
                sc_ref[i * sub + u] = sc[:, u * KEY_BLOCK:(u + 1) * KEY_BLOCK]
        lane = lax.broadcasted_iota(jnp.int32, (1, KEY_BLOCK), 1)
        ki_new = jnp.concatenate([misc[:, 0:IDX_HD], jnp.zeros((pad_rows, IDX_HD), F32)], axis=0)
        sc_ref[npast] = jnp.where(lane < tq, head_sum(_dot_nt(q2, ki_new.astype(BF16))), -jnp.inf)
        n_adm = jnp.full((tq, 1), float(p_len + tq), F32)
        thr_ref[...] = _topk_threshold(sc_ref, nkb, (tq, KEY_BLOCK), 1, n_sel, n_adm)
        m_ref[...] = jnp.full((rows, 1), NEG_BIG, F32)
        l_ref[...] = jnp.zeros((rows, 1), F32)
        acc_ref[...] = jnp.zeros((rows, B_HD), F32)

    qb = qb_ref[...] * (B_HD ** -0.5 * math.log2(math.e))
    q_h = [qb[:, h * B_HD:(h + 1) * B_HD].astype(BF16) for h in range(B_HEADS)]
    thr = thr_ref[...]

    def fold_in(logits_of, pv_of, sel):
        s = jnp.concatenate([jnp.where(sel, logits_of(h), NEG_BIG) for h in range(B_HEADS)], axis=0)
        m_old = m_ref[...]
        m_new = jnp.maximum(m_old, jnp.max(s, axis=-1, keepdims=True))
        alpha = jnp.exp2(m_old - m_new)
        p = jnp.exp2(s - m_new)
        pb = p.astype(BF16)
        pv = jnp.concatenate([pv_of(h, pb[h * tq:(h + 1) * tq, :]) for h in range(B_HEADS)], axis=0)
        l_ref[...] = alpha * l_ref[...] + jnp.sum(p, axis=-1, keepdims=True)
        acc_ref[...] = alpha * acc_ref[...] + pv
        m_ref[...] = m_new

    sel = jnp.concatenate([sc_ref[step * sub + u] > thr for u in range(sub)], axis=1)
    fold_in(lambda h: _dot(q_h[h], ck_ref[h].astype(BF16)),
            lambda h, p: _dot_nt(p, cv_ref[h].astype(BF16)), sel)

    @pl.when(step == nsteps - 1)
    def _():
        zpad = jnp.zeros((pad_rows, B_WIDTH), F32)
        kn = jnp.concatenate([kn_ref[...], zpad], axis=0).astype(BF16)
        vn = jnp.concatenate([vn_ref[...], zpad], axis=0).astype(BF16)
        fold_in(lambda h: _dot_nt(q_h[h], kn[:, h * B_HD:(h + 1) * B_HD]),
                lambda h, p: _dot(p, vn[:, h * B_HD:(h + 1) * B_HD]), sc_ref[npast] > thr)
        res = acc_ref[...] / l_ref[...]
        out = jnp.concatenate([res[h * tq:(h + 1) * tq, :] for h in range(B_HEADS)], axis=1)
        o_ref[...] = out * _silu(zb_ref[...])


def _dsa_decode(qb, qi, misc, zb, kb, vb, ck, cv, cki, *, bsz, t_len, p_len, n_sel):
    kstep = DEC_KEYS_PER_STEP
    assert p_len // KEY_BLOCK + 1 < COUNT_RADIX and p_len % kstep == 0
    rows = B_HEADS * t_len
    qmap = lambda b, s: (b, 0)
    cmap = lambda b, s: (b, 0, 0, s)
    ck, cv = (a.transpose(0, 2, 3, 1) for a in (ck, cv))
    cki = cki.transpose(0, 2, 1)
    return pl.pallas_call(
        functools.partial(_dsa_dec_body, p_len=p_len, tq=t_len, n_sel=n_sel, kstep=kstep),
        grid=(bsz, p_len // kstep),
        in_specs=[pl.BlockSpec((t_len, B_WIDTH), qmap),
                  pl.BlockSpec((t_len, IDX_HEADS * IDX_HD), qmap),
                  pl.BlockSpec((t_len, MISC_W), qmap),
                  pl.BlockSpec((t_len, B_WIDTH), qmap),
                  pl.BlockSpec((t_len, B_WIDTH), qmap),
                  pl.BlockSpec((t_len, B_WIDTH), qmap),
                  pl.BlockSpec((None, B_HEADS, B_HD, kstep), cmap),
                  pl.BlockSpec((None, B_HEADS, B_HD, kstep), cmap),
                  pl.BlockSpec((None, IDX_HD, p_len), lambda b, s: (b, 0, 0))],
        out_specs=pl.BlockSpec((t_len, B_WIDTH), qmap),
        out_shape=jax.ShapeDtypeStruct((bsz * t_len, B_WIDTH), F32),
        scratch_shapes=[pltpu.VMEM((p_len // KEY_BLOCK + 1, t_len, KEY_BLOCK), F32),
                        pltpu.VMEM((t_len, 1), F32),
                        pltpu.VMEM((rows, 1), F32),
                        pltpu.VMEM((rows, 1), F32),
                        pltpu.VMEM((rows, B_HD), F32)],
        compiler_params=_cparams(("parallel", "arbitrary")),
        name="dsa_decode",
    )(qb, qi, misc, zb, kb, vb, ck, cv, cki)


def _mem_body(qm_ref, zm_ref, mk_ref, mv_ref, o_ref):
    qm = qm_ref[...] * (M_HD ** -0.5)
    mk = mk_ref[...].reshape(N_MEM, M_WIDTH).astype(BF16)
    mv = mv_ref[...].reshape(N_MEM, M_WIDTH).astype(BF16)
    out = jnp.zeros(qm.shape, F32)
    for h in range(M_HEADS):
        mask = _lane_mask(M_WIDTH, h * M_HD, (h + 1) * M_HD)
        s = _dot_nt(jnp.where(mask, qm, 0.0).astype(BF16), mk)
        p = jnp.exp(s - jnp.max(s, axis=-1, keepdims=True))
        l = jnp.sum(p, axis=-1, keepdims=True)
        out = out + jnp.where(mask, _dot(p.astype(BF16), mv) / l, 0.0)
    o_ref[...] = out * _silu(zm_ref[...])


def _mem_attn(qm, zm, mk, mv, *, bsz, t_len, tq):
    nq = t_len // tq
    qmap = lambda b, j: (b * nq + j, 0)
    mem_spec = pl.BlockSpec((None,) + mk.shape[1:], lambda b, j: (b,) + (0,) * (mk.ndim - 1))
    return pl.pallas_call(
        _mem_body,
        grid=(bsz, nq),
        in_specs=[pl.BlockSpec((tq, M_WIDTH), qmap),
                  pl.BlockSpec((tq, M_WIDTH), qmap),
                  mem_spec, mem_spec],
        out_specs=pl.BlockSpec((tq, M_WIDTH), qmap),
        out_shape=jax.ShapeDtypeStruct((bsz * t_len, M_WIDTH), F32),
        compiler_params=_cparams(("parallel", "parallel")),
        name="mem_attn",
    )(qm, zm, mk, mv)


def _oproj_body(x_ref, oa_ref, ob_ref, om_ref, w_ref, y_ref):
    acc = _dot(oa_ref[...].astype(BF16), w_ref[0:A_WIDTH, :])
    acc = acc + _dot(ob_ref[...].astype(BF16), w_ref[A_WIDTH:A_WIDTH + B_WIDTH, :])
    acc = acc + _dot(om_ref[...].astype(BF16), w_ref[A_WIDTH + B_WIDTH:, :])
    y_ref[...] = x_ref[...] + acc


def _oproj(x, oa, ob, om, w, tm):
    m, d = x.shape
    rmap = lambda i: (i, 0)
    return pl.pallas_call(
        _oproj_body,
        grid=(m // tm,),
        in_specs=[pl.BlockSpec((tm, d), rmap),
                  pl.BlockSpec((tm, A_WIDTH), rmap),
                  pl.BlockSpec((tm, B_WIDTH), rmap),
                  pl.BlockSpec((tm, M_WIDTH), rmap),
                  pl.BlockSpec(w.shape, lambda i: (0, 0))],
        out_specs=pl.BlockSpec((tm, d), rmap),
        out_shape=jax.ShapeDtypeStruct((m, d), F32),
        compiler_params=_cparams(("parallel",)),
        name="oproj",
    )(x, oa, ob, om, w)


def _prep_weights(g_in, w_in, conv_w, a_log, dt_bias, g_o, g_qb, g_kb, g_ki, g_qm, w_out):
    offs = np.concatenate([[0], np.cumsum(IN_SPLITS)])
    col = lambda i: w_in[:, int(offs[i]):int(offs[i + 1])]
    (w_qkv, w_za, w_ba, w_aa, w_qb, w_kb, w_vb, w_zb, w_qi, w_ki, w_wi, w_qm, w_zm) = [col(i) for i in range(13)]
    d = w_in.shape[0]
    w_misc = jnp.concatenate(
        [w_ki, w_wi, w_ba, w_aa, jnp.zeros((d, MISC_W - MISC_A - A_HEADS), w_in.dtype)], axis=1)
    w_all = jnp.concatenate([w_qkv, w_za, w_qb, w_kb, w_vb, w_zb, w_qi, w_qm, w_zm, w_misc],
                            axis=1).astype(BF16)
    groups = [(3 * A_WIDTH, False, False), (A_WIDTH, False, False), (B_WIDTH, True, False),
              (B_WIDTH, True, True), (B_WIDTH, False, True), (B_WIDTH, False, False),
              (IDX_HEADS * IDX_HD, False, False), (M_WIDTH, True, False), (M_WIDTH, False, False),
              (MISC_W, True, False)]
    bd64 = _group_mean_matrix(B_WIDTH, B_HD, B_WIDTH)
    ones256 = jnp.ones((1, B_WIDTH), F32)
    misc_gain = jnp.concatenate([g_ki, jnp.full((IDX_HEADS,), IDX_SCALE, F32),
                                 jnp.ones((MISC_W - MISC_B,), F32)])[None, :]
    misc_nm = (jnp.arange(MISC_W) < IDX_HD).astype(F32)[None, :]
    aux = [bd64, jnp.tile(g_qb, B_HEADS)[None, :], ones256,
           bd64, jnp.tile(g_kb, B_HEADS)[None, :], ones256,
           bd64, jnp.tile(g_qm, M_HEADS)[None, :], ones256,
           _group_mean_matrix(MISC_W, IDX_HD, IDX_HD), misc_gain, misc_nm]
    pad_a = lambda v: jnp.zeros((1, MISC_W), F32).at[0, MISC_A:MISC_A + A_HEADS].set(v)
    return dict(g_in=g_in[None, :], w_all=w_all, groups=groups, aux=aux, conv_w=conv_w,
                avec=pad_a(-jnp.exp(a_log)), dtvec=pad_a(dt_bias), g_o=g_o[None, :],
                w_out=w_out.astype(BF16))


def _layer(x, mem_k, mem_v, conv_buf, s0, past, gdn_chunk, gdn_rows, wp):
    bsz, t_len, d = x.shape
    m = bsz * t_len
    x2 = x.reshape(m, d)
    tm = min(KEY_BLOCK, m)
    heads_last = lambda a_t: a_t.reshape(bsz, B_HEADS, B_HD, t_len).transpose(0, 3, 1, 2)
    if past is None:
        (qkv, za, qb, kb, vb, zb, qi, qm, zm, misc, kt, vt) = _proj(
            x2, wp["g_in"], wp["w_all"], wp["groups"], wp["aux"], tm, t_len)
        k_new, v_new = heads_last(kt), heads_last(vt)
    else:
        groups = [(wd, normed, False) for wd, normed, _ in wp["groups"]]
        (qkv, za, qb, kb, vb, zb, qi, qm, zm, misc) = _proj(x2, wp["g_in"], wp["w_all"], groups, wp["aux"],
                                                           tm, m)
        k_new, v_new = (a.reshape(bsz, t_len, B_HEADS, B_HD) for a in (kb, vb))
    cbuf8 = jnp.concatenate([jnp.zeros((bsz, 8 - (CONV_W - 1), 3 * A_WIDTH), F32), conv_buf], axis=1)
    oa, conv_new, s_new = _gdn(qkv, za, misc, cbuf8, s0, wp["conv_w"], wp["avec"], wp["dtvec"], wp["g_o"],
                               bsz=bsz, t_len=t_len, chunk=gdn_chunk, rows=gdn_rows)
    if past is None:
        n_sel = min(TOPK_MAX, t_len // 4)
        ob = _dsa_prompt(qb, qi, misc, zb, kb, vt, bsz=bsz, t_len=t_len, n_sel=n_sel, qblk=DSA_QBLK)
    else:
        ck, cv, cki = past
        p_len = ck.shape[1]
        assert (p_len + t_len - 1) // CHUNK <= p_len // CHUNK and p_len % KEY_BLOCK == 0
        n_sel = min(TOPK_MAX, (p_len + t_len) // 4)
        ob = _dsa_decode(qb, qi, misc, zb, kb, vb, ck, cv, cki, bsz=bsz, t_len=t_len, p_len=p_len,
                         n_sel=n_sel)
    om = _mem_attn(qm, zm, mem_k, mem_v, bsz=bsz, t_len=t_len, tq=min(256, t_len))
    y = _oproj(x2, oa, ob, om, wp["w_out"], min(512, m))
    return (y.reshape(bsz, t_len, d), conv_new, s_new, k_new, v_new,
            misc[:, :IDX_HD].reshape(bsz, t_len, IDX_HD))


def _memory_kv(mem, g_mem, w_mem_kv, g_km):
    bsz, n_mem, d = mem.shape
    groups = [(M_WIDTH, True, True), (M_WIDTH, False, True)]
    aux = [_group_mean_matrix(M_WIDTH, M_HD, M_WIDTH), jnp.tile(g_km, M_HEADS)[None, :],
           jnp.ones((1, M_WIDTH), F32)]
    mk, mv, mk_t, mv_t = _proj(mem.reshape(bsz * n_mem, d), g_mem[None, :], w_mem_kv.astype(BF16),
                               groups, aux, n_mem, n_mem)
    heads_last = lambda a_t: a_t.reshape(bsz, M_HEADS, M_HD, n_mem).transpose(0, 3, 1, 2)
    return (mk.reshape(bsz, n_mem, M_WIDTH), mv.reshape(bsz, n_mem, M_WIDTH), heads_last(mk_t), heads_last(mv_t))


def kernel(x_prompt, x_sample, state_conv_A, state_ssm_A, cache_k_B, cache_v_B, cache_kidx_B, cache_mem_k,
           cache_mem_v, mem_prompt, g_in, w_in, conv_w_A, a_log_A, dt_bias_A, g_o_A, g_q_B, g_k_B, g_kidx_B,
           g_mem, w_mem_kv, g_q_M, g_k_M, w_out):
    depth = w_in.shape[0]
    assert depth == 1
    l = 0
    bp, t_p, _ = x_prompt.shape
    bs, t_s, _ = x_sample.shape
    wp = _prep_weights(g_in[l], w_in[l], conv_w_A[l], a_log_A[l], dt_bias_A[l], g_o_A[l], g_q_B[l], g_k_B[l],
                       g_kidx_B[l], g_q_M[l], w_out[l])
    mk, mv, mk_heads, mv_heads = _memory_kv(mem_prompt, g_mem[l], w_mem_kv[l], g_k_M[l])
    zero_conv = jnp.zeros((bp, CONV_W - 1, 3 * A_WIDTH), F32)
    zero_ssm = jnp.zeros((bp, A_HEADS, A_DK, A_DV), F32)
    yp, c1, s1, k1, v1, ki1 = _layer(x_prompt, mk, mv, zero_conv, zero_ssm, None, CHUNK, 4 * CHUNK, wp)
    ys, c2, s2, k2, v2, ki2 = _layer(
        x_sample, cache_mem_k[l], cache_mem_v[l],
        state_conv_A[l], state_ssm_A[l], (cache_k_B[l], cache_v_B[l], cache_kidx_B[l]), t_s, t_s, wp)
    st = lambda a: a[None]
    return (yp, ys, st(c1), st(s1), st(k1), st(v1), st(ki1),
            st(mk_heads), st(mv_heads),
            st(c2), st(s2), st(k2), st(v2), st(ki2))
```

```python
import functools
import math

import numpy as np
import jax
import jax.numpy as jnp
from jax import lax
from jax.experimental import pallas as pl
from jax.experimental.pallas import tpu as pltpu

F32 = jnp.float32
BF16 = jnp.bfloat16
HIGHEST = lax.Precision.HIGHEST

D_MODEL = 1024
CHUNK = 64
A_HEADS = 4
A_DK = 128
A_DV = 128
A_WIDTH = A_HEADS * A_DV
CONV_W = 4
B_HEADS = 4
B_HD = 64
B_WIDTH = B_HEADS * B_HD
IDX_HEADS = 8
IDX_HD = 32
IDX_SCALE = (IDX_HEADS ** -0.5) * (IDX_HD ** -0.5)
TOPK_MAX = 256
Q_BLOCK = 128
N_MEM = 256
M_HEADS = 4
M_HD = 64
M_WIDTH = M_HEADS * M_HD
EPS = 1e-6
IN_SPLITS = (3 * A_WIDTH, A_WIDTH, A_HEADS, A_HEADS,
             B_WIDTH, B_WIDTH, B_WIDTH, B_WIDTH, IDX_HEADS * IDX_HD, IDX_HD, IDX_HEADS,
             M_WIDTH, M_WIDTH)

LANES = 128
KEY_BLOCK = 256
DSA_QBLK = 256
PROJ_ROWS = 512
DEC_KEYS_PER_STEP = 1024
SCORE_STRIP = 128
MISC_W = LANES
MISC_WI = IDX_HD
MISC_B = IDX_HD + IDX_HEADS
MISC_A = MISC_B + A_HEADS
VMEM_LIMIT = 48 * 1024 * 1024
NEG_BIG = -1e30
BISECT_PROBES = 3
COUNT_RADIX = 1024
BISECT_MAX_ITERS = 20


def _cparams(sem):
    return pltpu.CompilerParams(dimension_semantics=sem, vmem_limit_bytes=VMEM_LIMIT)


def _dot(a, b):
    return jnp.dot(a, b, preferred_element_type=F32)


def _dot_nt(a, b, precision=None):
    return lax.dot_general(a, b, (((1,), (1,)), ((), ())), precision=precision,
                           preferred_element_type=F32)


def _split2(x):
    hi = x.astype(BF16)
    return hi, (x - hi.astype(F32)).astype(BF16)


def _mm_split(a, b):
    ah, al = a
    bh, _ = b
    n = ah.shape[0]
    t = _dot(jnp.concatenate([ah, al], axis=0), bh)
    return t[:n] + t[n:]


def _silu(x):
    return x * jax.nn.sigmoid(x)


def _lane_mask(width, lo, hi):
    lane = lax.broadcasted_iota(jnp.int32, (1, width), 1)
    return (lane >= lo) & (lane < hi)


CARRY_ROWS = 8


def _conv_silu_norm(xp_ref, raw, cw, rows):
    first = CARRY_ROWS - (CONV_W - 1)
    xp_ref[CARRY_ROWS:CARRY_ROWS + rows, :] = raw
    y = xp_ref[first:first + rows, :] * cw[0:1, :]
    for j in range(1, CONV_W):
        y = y + xp_ref[first + j:first + j + rows, :] * cw[j:j + 1, :]
    y = _silu(y)
    conv_new = xp_ref[rows + first:rows + CARRY_ROWS, :]
    xp_ref[0:CARRY_ROWS, :] = xp_ref[rows:rows + CARRY_ROWS, :]
    parts = []
    for h in range(A_HEADS):
        qh = y[:, h * A_DK:(h + 1) * A_DK]
        parts.append(qh * (lax.rsqrt(jnp.sum(qh * qh, axis=-1, keepdims=True) + EPS) * (A_DK ** -0.5)))
    for h in range(A_HEADS):
        kh = y[:, A_WIDTH + h * A_DK:A_WIDTH + (h + 1) * A_DK]
        parts.append(kh * lax.rsqrt(jnp.sum(kh * kh, axis=-1, keepdims=True) + EPS))
    parts.append(y[:, 2 * A_WIDTH:3 * A_WIDTH])
    return jnp.concatenate(parts, axis=1), conv_new


def _proj_body(*refs, groups, tiles):
    x_ref, g_ref, w_ref = refs[0], refs[1], refs[2]
    n_aux = 3 * sum(1 for _, normed, _ in groups if normed)
    aux = refs[3:3 + n_aux]
    pos = 3 + n_aux
    has_conv = any(extra == "conv" for _, _, extra in groups)
    if has_conv:
        cbuf_ref, cw_ref = refs[pos], refs[pos + 1]
        pos += 2
    outs = refs[pos:pos + len(groups)]
    pos += len(groups)
    n_t = sum(1 for _, _, extra in groups if extra == "t")
    outs_t = list(refs[pos:pos + n_t])
    pos += n_t
    if has_conv:
        convnew_ref, xp_ref = refs[pos], refs[pos + 1]

        @pl.when(pl.program_id(0) % tiles == 0)
        def _():
            xp_ref[0:CARRY_ROWS, :] = cbuf_ref[...]
    x = x_ref[...]
    r = lax.rsqrt(jnp.mean(x * x, axis=-1, keepdims=True) + EPS)
    hb = (x * g_ref[...]).astype(BF16)
    off = 0
    ai = 0
    for (width, normed, extra), o_ref in zip(groups, outs):
        y = _dot(hb, w_ref[:, off:off + width]) * r
        if extra == "conv":
            y, conv_new = _conv_silu_norm(xp_ref, y, cw_ref[...], y.shape[0])
            convnew_ref[...] = conv_new
        if normed:
            bd_ref, gain_ref, nm_ref = aux[ai], aux[ai + 1], aux[ai + 2]
            ai += 3
            sq = y * y
            hi = sq.astype(BF16)
            lo = (sq - hi.astype(F32)).astype(BF16)
            ms = _dot(hi, bd_ref[...]) + _dot(lo, bd_ref[...])
            scale = jnp.where(nm_ref[...] > 0.0, lax.rsqrt(ms + EPS), 1.0)
            y = y * scale * gain_ref[...]
        o_ref[...] = y
        if extra == "t":
            outs_t.pop(0)[...] = y.T
        off += width


def _group_mean_matrix(width, group, n_lanes):
    i = np.arange(width)
    m = ((i[:, None] // group) == (i[None, :] // group)) & (i[:, None] < n_lanes) & (i[None, :] < n_lanes)
    return jnp.asarray(m.astype(np.float32) / group, dtype=BF16)


def _proj(x, g, w, groups, aux, tm, t_len, conv=None):
    m, d = x.shape
    nw = w.shape[1]
    tiles = t_len // tm
    bsz = m // t_len
    in_specs = [pl.BlockSpec((tm, d), lambda i: (i, 0)),
                pl.BlockSpec((1, d), lambda i: (0, 0)),
                pl.BlockSpec((d, nw), lambda i: (0, 0))]
    for a in aux:
        in_specs.append(pl.BlockSpec(a.shape, lambda i: (0, 0)))
    out_shape = [jax.ShapeDtypeStruct((m, wd), F32) for wd, _, _ in groups]
    out_specs = [pl.BlockSpec((tm, wd), lambda i: (i, 0)) for wd, _, _ in groups]
    for wd, _, extra in groups:
        if extra == "t":
            out_shape.append(jax.ShapeDtypeStruct((bsz, wd, t_len), F32))
            out_specs.append(pl.BlockSpec((None, wd, tm), lambda i: (i // tiles, 0, i % tiles)))
    args = [x, g, w, *aux]
    scratch = []
    if conv is not None:
        cbuf, cw = conv
        wd = cw.shape[1]
        in_specs += [pl.BlockSpec((None, CARRY_ROWS, wd), lambda i: (i // tiles, 0, 0)),
                     pl.BlockSpec(cw.shape, lambda i: (0, 0))]
        args += [cbuf, cw]
        out_shape.append(jax.ShapeDtypeStruct((bsz, CONV_W - 1, wd), F32))
        out_specs.append(pl.BlockSpec((None, CONV_W - 1, wd), lambda i: (i // tiles, 0, 0)))
        scratch.append(pltpu.VMEM((tm + CARRY_ROWS, wd), F32))
    return pl.pallas_call(
        functools.partial(_proj_body, groups=tuple(groups), tiles=tiles),
        grid=(m // tm,),
        in_specs=in_specs, out_specs=out_specs, out_shape=out_shape, scratch_shapes=scratch,
        compiler_params=_cparams(("arbitrary",) if conv is not None else ("parallel",)),
        name="proj",
    )(*args)


def _gdn_body(*refs, chunk, rows, fused_conv):
    if fused_conv:
        (qkv_ref, za_ref, misc_ref, s0_ref, avec_ref, dtvec_ref, go_ref, cbuf_ref, cw_ref,
         o_ref, sfin_ref, convnew_ref, g_sc, b_sc, s_sc, xp_ref, act_sc) = refs
    else:
        (qkv_ref, za_ref, misc_ref, s0_ref, avec_ref, dtvec_ref, go_ref,
         o_ref, sfin_ref, g_sc, b_sc, s_sc) = refs
    t = pl.program_id(1)
    c = chunk

    @pl.when(t == 0)
    def _():
        s_sc[...] = s0_ref[0]

    if fused_conv:
        @pl.when(t == 0)
        def _():
            xp_ref[0:CARRY_ROWS, :] = cbuf_ref[...]
        act, conv_new = _conv_silu_norm(xp_ref, qkv_ref[...], cw_ref[...], rows)
        convnew_ref[...] = conv_new
        act_sc[...] = act
        act_ref = act_sc
    else:
        act_ref = qkv_ref
    q_sc = act_ref.at[:, 0:A_WIDTH]
    k_sc = act_ref.at[:, A_WIDTH:2 * A_WIDTH]
    v_sc = act_ref.at[:, 2 * A_WIDTH:3 * A_WIDTH]
    misc = misc_ref[...]
    b_sc[...] = jax.nn.sigmoid(misc)
    g_sc[...] = avec_ref[...] * jax.nn.softplus(misc + dtvec_ref[...])

    ri = lax.broadcasted_iota(jnp.int32, (c, c), 0)
    ci = lax.broadcasted_iota(jnp.int32, (c, c), 1)
    tri = ri >= ci
    strict = ri > ci
    eye_f = (ri == ci).astype(F32)
    tri3 = (lax.broadcasted_iota(jnp.int32, (c, 3 * c), 0)
            >= lax.broadcasted_iota(jnp.int32, (c, 3 * c), 1) % c).astype(BF16)
    go = go_ref[...]
    n_chunks = rows // c

    problems = [(ic, h) for ic in range(n_chunks) for h in range(A_HEADS)]
    gc_all, gc_t = [], []
    for ic in range(n_chunks):
        g = g_sc[ic * c:(ic + 1) * c, :]
        g1 = g.astype(BF16)
        r1 = g - g1.astype(F32)
        g2 = r1.astype(BF16)
        g3 = (r1 - g2.astype(F32)).astype(BF16)
        gc_all.append(_dot(tri3, jnp.concatenate([g1, g2, g3], axis=0)))
    for ic in range(n_chunks):
        gc_t.append(gc_all[ic].T)

    def load(ref, ic, h):
        return ref[ic * c:(ic + 1) * c, h * A_DK:(h + 1) * A_DK]

    col = lambda a, lane: a[:, lane:lane + 1]
    k_bf = [load(k_sc, ic, h).astype(BF16) for ic, h in problems]
    kb = [load(k_sc, ic, h) * col(b_sc[ic * c:(ic + 1) * c, :], MISC_B + h) for ic, h in problems]
    kk = [_dot_nt(kb[i].astype(BF16), k_bf[i]) for i in range(len(problems))]
    qk_raw = [_dot_nt(load(q_sc, ic, h).astype(BF16), k_bf[i]) for i, (ic, h) in enumerate(problems)]
    decay = []
    for ic, h in problems:
        diff = col(gc_all[ic], MISC_A + h) - gc_t[ic][MISC_A + h:MISC_A + h + 1, :]
        decay.append(jnp.where(tri, jnp.exp(jnp.where(tri, diff, 0.0)), 0.0))
    lmat = [jnp.where(strict, kk[i] * decay[i], 0.0) for i in range(len(problems))]
    qk = [(qk_raw[i] * decay[i]).astype(BF16) for i in range(len(problems))]
    x_inv = [eye_f - m for m in lmat]
    p_split = [_split2(m) for m in lmat]
    for _ in range(int(math.log2(c)) - 1):
        p_split = [_split2(_mm_split(ps, ps)) for ps in p_split]
        x_inv = [x + _mm_split(_split2(x), ps) for x, ps in zip(x_inv, p_split)]
    sol = []
    for i, (ic, h) in enumerate(problems):
        egc = jnp.exp(col(gc_all[ic], MISC_A + h))
        beta = col(b_sc[ic * c:(ic + 1) * c, :], MISC_B + h)
        rhs = jnp.concatenate([load(v_sc, ic, h) * beta, kb[i] * egc], axis=-1).astype(BF16)
        sol.append(_mm_split(_split2(x_inv[i]), (rhs, None)))
    pre = []
    for i, (ic, h) in enumerate(problems):
        gc = col(gc_all[ic], MISC_A + h)
        g_last = gc[c - 1:c, :]
        k_dec_t = (load(k_sc, ic, h) * jnp.exp(g_last - gc)).T.astype(BF16)
        wq = jnp.concatenate([sol[i][:, A_DV:], load(q_sc, ic, h) * jnp.exp(gc)], axis=0).astype(BF16)
        pre.append((sol[i][:, :A_DV], wq, k_dec_t, jnp.exp(g_last)))

    heads = range(A_HEADS)
    s_cur = [s_sc[h] for h in heads]
    for ic in range(n_chunks):
        pr = [pre[ic * A_HEADS + h] for h in heads]
        ws = [_dot(pr[h][1], s_cur[h].astype(BF16)) for h in heads]
        ub = [(pr[h][0] - ws[h][:c]).astype(BF16) for h in heads]
        o = [ws[h][c:] + _dot(qk[ic * A_HEADS + h], ub[h]) for h in heads]
        s_cur = [s_cur[h] * pr[h][3] + _dot(pr[h][2], ub[h]) for h in heads]
        for h in heads:
            on = o[h] * lax.rsqrt(jnp.mean(o[h] * o[h], axis=-1, keepdims=True) + EPS) * go
            o_ref[ic * c:(ic + 1) * c, h * A_DV:(h + 1) * A_DV] = on * _silu(load(za_ref, ic, h))
    for h in heads:
        s_sc[h] = s_cur[h]
    sfin_ref[0] = s_sc[...]


def _gdn(qkv, za, misc, s0, avec, dtvec, g_o, *, bsz, t_len, chunk, rows, conv=None):
    nt = t_len // rows
    w3 = 3 * A_WIDTH
    row_map = lambda b, t: (b * nt + t, 0)
    const2 = lambda b, t: (0, 0)
    state_spec = pl.BlockSpec((1, A_HEADS, A_DK, A_DV), lambda b, t: (b, 0, 0, 0))
    in_specs = [pl.BlockSpec((rows, w3), row_map),
                pl.BlockSpec((rows, A_WIDTH), row_map),
                pl.BlockSpec((rows, MISC_W), row_map),
                state_spec,
                pl.BlockSpec((1, MISC_W), const2),
                pl.BlockSpec((1, MISC_W), const2),
                pl.BlockSpec((1, A_DV), const2)]
    args = [qkv, za, misc, s0, avec, dtvec, g_o]
    out_specs = [pl.BlockSpec((rows, A_WIDTH), row_map), state_spec]
    out_shape = [jax.ShapeDtypeStruct((bsz * t_len, A_WIDTH), F32),
                 jax.ShapeDtypeStruct((bsz, A_HEADS, A_DK, A_DV), F32)]
    scratch = [pltpu.VMEM((rows, MISC_W), F32),
               pltpu.VMEM((rows, MISC_W), F32),
               pltpu.VMEM((A_HEADS, A_DK, A_DV), F32)]
    if conv is not None:
        cbuf, cw = conv
        in_specs += [pl.BlockSpec((None, CARRY_ROWS, w3), lambda b, t: (b, 0, 0)),
                     pl.BlockSpec((CONV_W, w3), const2)]
        args += [cbuf, cw]
        out_specs.append(pl.BlockSpec((None, CONV_W - 1, w3), lambda b, t: (b, 0, 0)))
        out_shape.append(jax.ShapeDtypeStruct((bsz, CONV_W - 1, w3), F32))
        scratch += [pltpu.VMEM((rows + CARRY_ROWS, w3), F32), pltpu.VMEM((rows, w3), F32)]
    return pl.pallas_call(
        functools.partial(_gdn_body, chunk=chunk, rows=rows, fused_conv=conv is not None),
        grid=(bsz, nt),
        in_specs=in_specs, out_specs=out_specs, out_shape=out_shape, scratch_shapes=scratch,
        compiler_params=_cparams(("parallel", "arbitrary")),
        name="gdn",
    )(*args)


def _topk_threshold(sc_ref, nkb, shape, key_axis, n_sel, n_adm):
    kf = float(n_sel)
    kblk = shape[key_axis]

    if key_axis == 0:
        strip = min(kblk, 32)
        part_shape = (strip, shape[1])
        strip_of = lambda kb, i: sc_ref[kb, i * strip:(i + 1) * strip, :]

        def put_strip(kb, i, v):
            sc_ref[kb, i * strip:(i + 1) * strip, :] = v
    else:
        strip = kblk
        part_shape = shape
        strip_of = lambda kb, i: sc_ref[kb]

        def put_strip(kb, i, v):
            sc_ref[kb] = v
    kpos = lax.broadcasted_iota(jnp.int32, part_shape, key_axis)

    def reduce_blocks(specs):
        def body(kb, accs):
            accs = list(accs)
            for i in range(kblk // strip):
                s = strip_of(kb, i)
                k0 = kb * kblk + i * strip
                accs = [cmb(acc, fn(s, k0)) for acc, (fn, _, _, _, cmb) in zip(accs, specs)]
            return tuple(accs)
        init = tuple(jnp.full(part_shape, i, dt) for _, i, dt, _, _ in specs)
        accs = lax.fori_loop(0, nkb, body, init)
        return [acc if op is None else op(acc, axis=key_axis, keepdims=True)
                for acc, (_, _, _, op, _) in zip(accs, specs)]

    one_if = lambda c: jnp.where(c, 1.0, 0.0)
    vmax, vmin, cpos, czero = reduce_blocks([
        (lambda s, k0: s, -jnp.inf, F32, jnp.max, jnp.maximum),
        (lambda s, k0: jnp.where(s == -jnp.inf, jnp.inf, s), jnp.inf, F32, jnp.min, jnp.minimum),
        (lambda s, k0: one_if(s > 0.0), 0.0, F32, jnp.sum, jnp.add),
        (lambda s, k0: one_if(s == 0.0), 0.0, F32, jnp.sum, jnp.add)])

    long_row = n_adm > kf
    pos_row = long_row & (cpos >= kf)
    zero_row = long_row & (cpos < kf) & (cpos + czero >= kf)
    neg_row = long_row & (cpos + czero < kf)
    lo0 = jnp.where(pos_row | zero_row, 0.0, vmin - jnp.maximum(1.0, jnp.abs(vmin)))
    cnt_lo0 = jnp.where(pos_row, cpos, jnp.where(zero_row, kf, n_adm))
    hi0 = jnp.where(neg_row, 0.0, vmax)
    cnt_hi0 = jnp.where(neg_row, cpos, 0.0)
    need_zero = jnp.where(zero_row, kf - cpos, 0.0)

    def cond(st):
        it, go = st[0], st[1]
        return (it < BISECT_MAX_ITERS) & (go > 0)

    def body(st):
        it, _, lo, hi, cnt_lo, cnt_hi = st
        fracs = [(t + 1.0) / (BISECT_PROBES + 1.0) for t in range(BISECT_PROBES)]
        mids = [lo * (1.0 - f) + hi * f for f in fracs]
        for t in range(1, BISECT_PROBES):
            mids[t] = jnp.maximum(mids[t], mids[t - 1])
        codes = [sum(COUNT_RADIX ** u for u in range(t + 1)) for t in range(BISECT_PROBES)]

        def encode(s, k0):
            e = jnp.zeros(s.shape, jnp.int32)
            for mid, code in zip(mids, codes):
                e = jnp.where(s > mid, code, e)
            return e
        (packed,) = reduce_blocks([(encode, 0, jnp.int32, None, jnp.add)])
        cnts = []
        for t in range(BISECT_PROBES):
            digit = packed & (COUNT_RADIX - 1) if t < BISECT_PROBES - 1 else packed
            packed = packed >> COUNT_RADIX.bit_length() - 1
            cnts.append(jnp.sum(digit.astype(F32), axis=key_axis, keepdims=True))
        lo_n, cnt_lo_n, hi_n, cnt_hi_n = lo, cnt_lo, hi, cnt_hi
        for mid, cm in zip(mids, cnts):
            up = cm >= kf
            lo_n = jnp.where(up, mid, lo_n)
            cnt_lo_n = jnp.where(up, cm, cnt_lo_n)
        for mid, cm in zip(mids[::-1], cnts[::-1]):
            dn = cm < kf
            hi_n = jnp.where(dn, mid, hi_n)
            cnt_hi_n = jnp.where(dn, cm, cnt_hi_n)
        go = (jnp.max(cnt_lo_n) > kf).astype(jnp.int32)
        return it + 1, go, lo_n, hi_n, cnt_lo_n, cnt_hi_n

    go0 = (jnp.max(cnt_lo0) > kf).astype(jnp.int32)
    _, go, lo, hi, cnt_lo, cnt_hi = lax.while_loop(
        cond, body, (jnp.int32(0), go0, lo0, hi0, cnt_lo0, cnt_hi0))

    fix = cnt_lo > kf
    need0 = jnp.where(fix, kf - cnt_hi, 0.0)
    big_idx = 2 ** 30

    def fix_cond(need):
        return jnp.max(need) > 0.0

    def fix_body(need):
        def in_cluster(s):
            return (s > lo) & (s <= hi)
        (mval,) = reduce_blocks([(lambda s, k0: jnp.where(in_cluster(s), s, -jnp.inf), -jnp.inf, F32,
                                  jnp.max, jnp.maximum)])
        (idx,) = reduce_blocks(
            [(lambda s, k0: jnp.where(in_cluster(s) & (s == mval), kpos + k0, big_idx),
              big_idx, jnp.int32, jnp.min, jnp.minimum)])
        active = need > 0.0

        def promote(kb, carry):
            for i in range(kblk // strip):
                s = strip_of(kb, i)
                put_strip(kb, i, jnp.where(active & ((kpos + kb * kblk + i * strip) == idx), jnp.inf, s))
            return carry
        lax.fori_loop(0, nkb, promote, 0)
        return jnp.where(active, need - 1.0, need)

    lax.while_loop(fix_cond, fix_body, need0)

    @pl.when(jnp.max(need_zero) > 0.0)
    def _():
        r = lax.broadcasted_iota(jnp.int32, (kblk, kblk), 0)
        c = lax.broadcasted_iota(jnp.int32, (kblk, kblk), 1)
        tri = ((r >= c) if key_axis == 0 else (r <= c)).astype(BF16)

        def tie_blk(kb, seen):
            s = sc_ref[kb]
            z = (s == 0.0) & zero_row
            zb = one_if(z).astype(BF16)
            if key_axis == 0:
                rank = _dot(tri, zb) + seen
                last = rank[kblk - 1:kblk, :]
            else:
                rank = _dot(zb, tri) + seen
                last = rank[:, kblk - 1:kblk]
            sc_ref[kb] = jnp.where(z & (rank <= need_zero), jnp.inf, s)
            return last
        lax.fori_loop(0, nkb, tie_blk, jnp.zeros_like(need_zero))

    return jnp.where(fix, hi, lo)


def _dsa_body(qb_ref, qi_ref, miscq_ref, zb_ref, k_ref, vt_ref, misck_ref, o_ref, sc_ref, s_ref, *, n_sel, qblk):
    j = pl.program_id(1)
    nkb = ((j + 1) * qblk + KEY_BLOCK - 1) // KEY_BLOCK
    qi = qi_ref[...]
    q_h = [qi[:, h * IDX_HD:(h + 1) * IDX_HD].astype(BF16) for h in range(IDX_HEADS)]
    w_t = miscq_ref[...].T[MISC_WI:MISC_WI + IDX_HEADS, :]
    qchunk = (j * qblk + lax.broadcasted_iota(jnp.int32, (1, qblk), 1)) // CHUNK
    krow = lax.broadcasted_iota(jnp.int32, (KEY_BLOCK, 1), 0)

    def score_blk(kb, carry):
        k0 = pl.multiple_of(kb * KEY_BLOCK, KEY_BLOCK)
        for half in range(KEY_BLOCK // SCORE_STRIP):
            r0 = half * SCORE_STRIP
            ki = misck_ref[pl.ds(k0 + r0, SCORE_STRIP), 0:IDX_HD].astype(BF16)
            acc = jnp.zeros((SCORE_STRIP, qblk), F32)
            for h in range(IDX_HEADS):
                acc = acc + w_t[h:h + 1, :] * jnp.maximum(_dot_nt(ki, q_h[h]), 0.0)
            adm = ((k0 + r0 + krow[:SCORE_STRIP]) // CHUNK) <= qchunk
            sc_ref[kb, r0:r0 + SCORE_STRIP, :] = jnp.where(adm, acc, -jnp.inf)
        return carry

    npairs = (nkb + 1) // 2

    def pair_loop(body, init):
        return lax.fori_loop(0, npairs, lambda i, c: body(2 * i + 1, body(2 * i, c)), init)

    pair_loop(score_blk, 0)

    n_adm = ((qchunk + 1) * CHUNK).astype(F32)
    thr = _topk_threshold(sc_ref, nkb, (KEY_BLOCK, qblk), 0, n_sel, n_adm)

    qb = qb_ref[...] * (B_HD ** -0.5 * math.log2(math.e))
    qs = jnp.concatenate([jnp.where(_lane_mask(B_WIDTH, h * B_HD, (h + 1) * B_HD), qb, 0.0)
                          for h in range(B_HEADS)], axis=0).astype(BF16)
    wide = B_HEADS * qblk

    def fold(acc, x, combine):
        for i in range(x.shape[0] // 8):
            acc = combine(acc, x[8 * i:8 * (i + 1)])
        return acc

    def logits_blk(kb, m8):
        k0 = pl.multiple_of(kb * KEY_BLOCK, KEY_BLOCK)
        s = _dot_nt(k_ref[pl.ds(k0, KEY_BLOCK), :].astype(BF16), qs)
        sel = sc_ref[kb] > thr
        s = jnp.where(jnp.concatenate([sel] * B_HEADS, axis=1), s, NEG_BIG)
        s_ref[kb] = s
        return fold(m8, s, jnp.maximum)

    m8 = pair_loop(logits_blk, jnp.full((8, wide), NEG_BIG, F32))
    m = jnp.max(m8, axis=0, keepdims=True)

    def pv_blk(kb, carry):
        l8, accs = carry
        p = jnp.exp2(s_ref[kb] - m)
        pb = p.astype(BF16)
        k0 = pl.multiple_of(kb * KEY_BLOCK, KEY_BLOCK)
        vt = vt_ref[:, pl.ds(k0, KEY_BLOCK)].astype(BF16)
        accs = tuple(accs[h] + _dot(vt[h * B_HD:(h + 1) * B_HD, :], pb[:, h * qblk:(h + 1) * qblk])
                     for h in range(B_HEADS))
        return fold(l8, p, jnp.add), accs

    l8, accs = pair_loop(
        pv_blk, (jnp.zeros((8, wide), F32), tuple(jnp.zeros((B_HD, qblk), F32) for _ in range(B_HEADS))))
    l = jnp.sum(l8, axis=0, keepdims=True)
    o_t = jnp.concatenate([accs[h] / l[:, h * qblk:(h + 1) * qblk] for h in range(B_HEADS)], axis=0)
    o_ref[...] = o_t.T * _silu(zb_ref[...])


def _dsa_prompt(qb, qi, misc, zb, kb, vt, *, bsz, t_len, n_sel, qblk):
    nq = t_len // qblk
    nkb = t_len // KEY_BLOCK
    assert nkb * (KEY_BLOCK // 8) < COUNT_RADIX and nkb % 2 == 0
    qmap = lambda b, j: (b * nq + j, 0)
    kmap = lambda b, j: (b, 0)
    return pl.pallas_call(
        functools.partial(_dsa_body, n_sel=n_sel, qblk=qblk),
        grid=(bsz, nq),
        in_specs=[pl.BlockSpec((qblk, B_WIDTH), qmap),
                  pl.BlockSpec((qblk, IDX_HEADS * IDX_HD), qmap),
                  pl.BlockSpec((qblk, MISC_W), qmap),
                  pl.BlockSpec((qblk, B_WIDTH), qmap),
                  pl.BlockSpec((t_len, B_WIDTH), kmap),
                  pl.BlockSpec((None, B_WIDTH, t_len), lambda b, j: (b, 0, 0)),
                  pl.BlockSpec((t_len, MISC_W), kmap)],
        out_specs=pl.BlockSpec((qblk, B_WIDTH), qmap),
        out_shape=jax.ShapeDtypeStruct((bsz * t_len, B_WIDTH), F32),
        scratch_shapes=[pltpu.VMEM((nkb, KEY_BLOCK, qblk), F32),
                        pltpu.VMEM((nkb, KEY_BLOCK, B_HEADS * qblk), F32)],
        compiler_params=_cparams(("parallel", "arbitrary")),
        name="dsa_prompt",
    )(qb, qi, misc, zb, kb, vt, misc)


def _dsa_dec_body(qb_ref, qi_ref, misc_ref, zb_ref, kn_ref, vn_ref, ck_ref, cv_ref, cki_ref, o_ref,
                  sc_ref, thr_ref, m_ref, l_ref, acc_ref, *, p_len, tq, n_sel, kstep):
    step = pl.program_id(1)
    nsteps = p_len // kstep
    sub = kstep // KEY_BLOCK
    npast = p_len // KEY_BLOCK
    nkb = npast + 1
    pad_rows = KEY_BLOCK - tq
    rows = B_HEADS * tq

    @pl.when(step == 0)
    def _():
        qi = qi_ref[...]
        misc = misc_ref[...]
        q2 = jnp.concatenate([qi[:, h * IDX_HD:(h + 1) * IDX_HD] for h in range(IDX_HEADS)],
                             axis=0).astype(BF16)
        w_h = [misc[:, MISC_WI + h:MISC_WI + h + 1] for h in range(IDX_HEADS)]

        def head_sum(lg):
            acc = jnp.zeros((tq, lg.shape[1]), F32)
            for h in range(IDX_HEADS):
                acc = acc + w_h[h] * jnp.maximum(lg[h * tq:(h + 1) * tq, :], 0.0)
            return acc

        for i in range(nsteps):
            sc = head_sum(_dot(q2, cki_ref[:, i * kstep:(i + 1) * kstep].astype(BF16)))
            for u in range(sub):
                sc_ref[i * sub + u] = sc[:, u * KEY_BLOCK:(u + 1) * KEY_BLOCK]
        lane = lax.broadcasted_iota(jnp.int32, (1, KEY_BLOCK), 1)
        ki_new = jnp.concatenate([misc[:, 0:IDX_HD], jnp.zeros((pad_rows, IDX_HD), F32)], axis=0)
        sc_ref[npast] = jnp.where(lane < tq, head_sum(_dot_nt(q2, ki_new.astype(BF16))), -jnp.inf)
        n_adm = jnp.full((tq, 1), float(p_len + tq), F32)
        thr_ref[...] = _topk_threshold(sc_ref, nkb, (tq, KEY_BLOCK), 1, n_sel, n_adm)
        m_ref[...] = jnp.full((rows, 1), NEG_BIG, F32)
        l_ref[...] = jnp.zeros((rows, 1), F32)
        acc_ref[...] = jnp.zeros((rows, B_HD), F32)

    qb = qb_ref[...] * (B_HD ** -0.5 * math.log2(math.e))
    q_h = [qb[:, h * B_HD:(h + 1) * B_HD].astype(BF16) for h in range(B_HEADS)]
    thr = thr_ref[...]

    def fold_in(logits_of, pv_of, sel):
        s = jnp.concatenate([jnp.where(sel, logits_of(h), NEG_BIG) for h in range(B_HEADS)], axis=0)
        m_old = m_ref[...]
        m_new = jnp.maximum(m_old, jnp.max(s, axis=-1, keepdims=True))
        alpha = jnp.exp2(m_old - m_new)
        p = jnp.exp2(s - m_new)
        pb = p.astype(BF16)
        pv = jnp.concatenate([pv_of(h, pb[h * tq:(h + 1) * tq, :]) for h in range(B_HEADS)], axis=0)
        l_ref[...] = alpha * l_ref[...] + jnp.sum(p, axis=-1, keepdims=True)
        acc_ref[...] = alpha * acc_ref[...] + pv
        m_ref[...] = m_new

    sel = jnp.concatenate([sc_ref[step * sub + u] > thr for u in range(sub)], axis=1)
    fold_in(lambda h: _dot(q_h[h], ck_ref[h].astype(BF16)),
            lambda h, p: _dot_nt(p, cv_ref[h].astype(BF16)), sel)

    @pl.when(step == nsteps - 1)
    def _():
        zpad = jnp.zeros((pad_rows, B_WIDTH), F32)
        kn = jnp.concatenate([kn_ref[...], zpad], axis=0).astype(BF16)
        vn = jnp.concatenate([vn_ref[...], zpad], axis=0).astype(BF16)
        fold_in(lambda h: _dot_nt(q_h[h], kn[:, h * B_HD:(h + 1) * B_HD]),
                lambda h, p: _dot(p, vn[:, h * B_HD:(h + 1) * B_HD]), sc_ref[npast] > thr)
        res = acc_ref[...] / l_ref[...]
        out = jnp.concatenate([res[h * tq:(h + 1) * tq, :] for h in range(B_HEADS)], axis=1)
        o_ref[...] = out * _silu(zb_ref[...])


def _dsa_decode(qb, qi, misc, zb, kb, vb, ck, cv, cki, *, bsz, t_len, p_len, n_sel):
    kstep = DEC_KEYS_PER_STEP
    assert p_len // KEY_BLOCK + 1 < COUNT_RADIX and p_len % kstep == 0
    rows = B_HEADS * t_len
    qmap = lambda b, s: (b, 0)
    cmap = lambda b, s: (b, 0, 0, s)
    ck, cv = (a.transpose(0, 2, 3, 1) for a in (ck, cv))
    cki = cki.transpose(0, 2, 1)
    return pl.pallas_call(
        functools.partial(_dsa_dec_body, p_len=p_len, tq=t_len, n_sel=n_sel, kstep=kstep),
        grid=(bsz, p_len // kstep),
        in_specs=[pl.BlockSpec((t_len, B_WIDTH), qmap),
                  pl.BlockSpec((t_len, IDX_HEADS * IDX_HD), qmap),
                  pl.BlockSpec((t_len, MISC_W), qmap),
                  pl.BlockSpec((t_len, B_WIDTH), qmap),
                  pl.BlockSpec((t_len, B_WIDTH), qmap),
                  pl.BlockSpec((t_len, B_WIDTH), qmap),
                  pl.BlockSpec((None, B_HEADS, B_HD, kstep), cmap),
                  pl.BlockSpec((None, B_HEADS, B_HD, kstep), cmap),
                  pl.BlockSpec((None, IDX_HD, p_len), lambda b, s: (b, 0, 0))],
        out_specs=pl.BlockSpec((t_len, B_WIDTH), qmap),
        out_shape=jax.ShapeDtypeStruct((bsz * t_len, B_WIDTH), F32),
        scratch_shapes=[pltpu.VMEM((p_len // KEY_BLOCK + 1, t_len, KEY_BLOCK), F32),
                        pltpu.VMEM((t_len, 1), F32),
                        pltpu.VMEM((rows, 1), F32),
                        pltpu.VMEM((rows, 1), F32),
                        pltpu.VMEM((rows, B_HD), F32)],
        compiler_params=_cparams(("parallel", "arbitrary")),
        name="dsa_decode",
    )(qb, qi, misc, zb, kb, vb, ck, cv, cki)


def _mem_body(qm_ref, zm_ref, mk_ref, mv_ref, o_ref):
    qm = qm_ref[...] * (M_HD ** -0.5)
    mk = mk_ref[...].reshape(N_MEM, M_WIDTH).astype(BF16)
    mv = mv_ref[...].reshape(N_MEM, M_WIDTH).astype(BF16)
    out = jnp.zeros(qm.shape, F32)
    for h in range(M_HEADS):
        mask = _lane_mask(M_WIDTH, h * M_HD, (h + 1) * M_HD)
        s = _dot_nt(jnp.where(mask, qm, 0.0).astype(BF16), mk)
        p = jnp.exp(s - jnp.max(s, axis=-1, keepdims=True))
        l = jnp.sum(p, axis=-1, keepdims=True)
        out = out + jnp.where(mask, _dot(p.astype(BF16), mv) / l, 0.0)
    o_ref[...] = out * _silu(zm_ref[...])


def _mem_attn(qm, zm, mk, mv, *, bsz, t_len, tq):
    nq = t_len // tq
    qmap = lambda b, j: (b * nq + j, 0)
    mem_spec = pl.BlockSpec((None,) + mk.shape[1:], lambda b, j: (b,) + (0,) * (mk.ndim - 1))
    return pl.pallas_call(
        _mem_body,
        grid=(bsz, nq),
        in_specs=[pl.BlockSpec((tq, M_WIDTH), qmap),
                  pl.BlockSpec((tq, M_WIDTH), qmap),
                  mem_spec, mem_spec],
        out_specs=pl.BlockSpec((tq, M_WIDTH), qmap),
        out_shape=jax.ShapeDtypeStruct((bsz * t_len, M_WIDTH), F32),
        compiler_params=_cparams(("parallel", "parallel")),
        name="mem_attn",
    )(qm, zm, mk, mv)


def _oproj_body(x_ref, oa_ref, ob_ref, om_ref, w_ref, y_ref):
    acc = _dot(oa_ref[...].astype(BF16), w_ref[0:A_WIDTH, :])
    acc = acc + _dot(ob_ref[...].astype(BF16), w_ref[A_WIDTH:A_WIDTH + B_WIDTH, :])
    acc = acc + _dot(om_ref[...].astype(BF16), w_ref[A_WIDTH + B_WIDTH:, :])
    y_ref[...] = x_ref[...] + acc


def _oproj(x, oa, ob, om, w, tm):
    m, d = x.shape
    rmap = lambda i: (i, 0)
    return pl.pallas_call(
        _oproj_body,
        grid=(m // tm,),
        in_specs=[pl.BlockSpec((tm, d), rmap),
                  pl.BlockSpec((tm, A_WIDTH), rmap),
                  pl.BlockSpec((tm, B_WIDTH), rmap),
                  pl.BlockSpec((tm, M_WIDTH), rmap),
                  pl.BlockSpec(w.shape, lambda i: (0, 0))],
        out_specs=pl.BlockSpec((tm, d), rmap),
        out_shape=jax.ShapeDtypeStruct((m, d), F32),
        compiler_params=_cparams(("parallel",)),
        name="oproj",
    )(x, oa, ob, om, w)


def _prep_weights(g_in, w_in, conv_w, a_log, dt_bias, g_o, g_qb, g_kb, g_ki, g_qm, w_out):
    offs = np.concatenate([[0], np.cumsum(IN_SPLITS)])
    col = lambda i: w_in[:, int(offs[i]):int(offs[i + 1])]
    (w_qkv, w_za, w_ba, w_aa, w_qb, w_kb, w_vb, w_zb, w_qi, w_ki, w_wi, w_qm, w_zm) = [col(i) for i in range(13)]
    d = w_in.shape[0]
    w_misc = jnp.concatenate(
        [w_ki, w_wi, w_ba, w_aa, jnp.zeros((d, MISC_W - MISC_A - A_HEADS), w_in.dtype)], axis=1)
    w_all = jnp.concatenate([w_za, w_qb, w_kb, w_vb, w_zb, w_qi, w_qm, w_zm, w_misc, w_qkv],
                            axis=1).astype(BF16)
    groups = [(A_WIDTH, False, ""), (B_WIDTH, True, ""),
              (B_WIDTH, True, "t"), (B_WIDTH, False, "t"), (B_WIDTH, False, ""),
              (IDX_HEADS * IDX_HD, False, ""), (M_WIDTH, True, ""), (M_WIDTH, False, ""),
              (MISC_W, True, ""), (3 * A_WIDTH, False, "conv")]
    bd64 = _group_mean_matrix(B_WIDTH, B_HD, B_WIDTH)
    ones256 = jnp.ones((1, B_WIDTH), F32)
    misc_gain = jnp.concatenate([g_ki, jnp.full((IDX_HEADS,), IDX_SCALE, F32),
                                 jnp.ones((MISC_W - MISC_B,), F32)])[None, :]
    misc_nm = (jnp.arange(MISC_W) < IDX_HD).astype(F32)[None, :]
    aux = [bd64, jnp.tile(g_qb, B_HEADS)[None, :], ones256,
           bd64, jnp.tile(g_kb, B_HEADS)[None, :], ones256,
           bd64, jnp.tile(g_qm, M_HEADS)[None, :], ones256,
           _group_mean_matrix(MISC_W, IDX_HD, IDX_HD), misc_gain, misc_nm]
    pad_a = lambda v: jnp.zeros((1, MISC_W), F32).at[0, MISC_A:MISC_A + A_HEADS].set(v)
    return dict(g_in=g_in[None, :], w_all=w_all, groups=groups, aux=aux, conv_w=conv_w,
                avec=pad_a(-jnp.exp(a_log)), dtvec=pad_a(dt_bias), g_o=g_o[None, :],
                w_out=w_out.astype(BF16))


def _layer(x, mem_k, mem_v, conv_buf, s0, past, gdn_chunk, gdn_rows, wp):
    bsz, t_len, d = x.shape
    m = bsz * t_len
    x2 = x.reshape(m, d)
    tm = min(PROJ_ROWS, m)
    heads_last = lambda a_t: a_t.reshape(bsz, B_HEADS, B_HD, t_len).transpose(0, 3, 1, 2)
    cbuf = jnp.concatenate([jnp.zeros((bsz, CARRY_ROWS - (CONV_W - 1), 3 * A_WIDTH), F32), conv_buf], axis=1)
    conv = (cbuf, wp["conv_w"])
    gdn_args = dict(bsz=bsz, t_len=t_len, chunk=gdn_chunk, rows=gdn_rows)
    if past is None:
        (za, qb, kb, vb, zb, qi, qm, zm, misc, qkv, kt, vt, conv_new) = _proj(
            x2, wp["g_in"], wp["w_all"], wp["groups"], wp["aux"], tm, t_len, conv=conv)
        k_new, v_new = heads_last(kt), heads_last(vt)
        oa, s_new = _gdn(qkv, za, misc, s0, wp["avec"], wp["dtvec"], wp["g_o"], **gdn_args)
    else:
        groups = [(wd, normed, "") for wd, normed, _ in wp["groups"]]
        (za, qb, kb, vb, zb, qi, qm, zm, misc, qkv) = _proj(x2, wp["g_in"], wp["w_all"], groups, wp["aux"],
                                                           tm, m)
        k_new, v_new = (a.reshape(bsz, t_len, B_HEADS, B_HD) for a in (kb, vb))
        oa, s_new, conv_new = _gdn(qkv, za, misc, s0, wp["avec"], wp["dtvec"], wp["g_o"], conv=conv,
                                   **gdn_args)
    if past is None:
        n_sel = min(TOPK_MAX, t_len // 4)
        ob = _dsa_prompt(qb, qi, misc, zb, kb, vt, bsz=bsz, t_len=t_len, n_sel=n_sel, qblk=DSA_QBLK)
    else:
        ck, cv, cki = past
        p_len = ck.shape[1]
        assert (p_len + t_len - 1) // CHUNK <= p_len // CHUNK and p_len % KEY_BLOCK == 0
        n_sel = min(TOPK_MAX, (p_len + t_len) // 4)
        ob = _dsa_decode(qb, qi, misc, zb, kb, vb, ck, cv, cki, bsz=bsz, t_len=t_len, p_len=p_len,
                         n_sel=n_sel)
    om = _mem_attn(qm, zm, mem_k, mem_v, bsz=bsz, t_len=t_len, tq=min(256, t_len))
    y = _oproj(x2, oa, ob, om, wp["w_out"], min(512, m))
    return (y.reshape(bsz, t_len, d), conv_new, s_new, k_new, v_new,
            misc[:, :IDX_HD].reshape(bsz, t_len, IDX_HD))


def _memory_kv(mem, g_mem, w_mem_kv, g_km):
    bsz, n_mem, d = mem.shape
    groups = [(M_WIDTH, True, "t"), (M_WIDTH, False, "t")]
    aux = [_group_mean_matrix(M_WIDTH, M_HD, M_WIDTH), jnp.tile(g_km, M_HEADS)[None, :],
           jnp.ones((1, M_WIDTH), F32)]
    mk, mv, mk_t, mv_t = _proj(mem.reshape(bsz * n_mem, d), g_mem[None, :], w_mem_kv.astype(BF16),
                               groups, aux, n_mem, n_mem)
    heads_last = lambda a_t: a_t.reshape(bsz, M_HEADS, M_HD, n_mem).transpose(0, 3, 1, 2)
    return (mk.reshape(bsz, n_mem, M_WIDTH), mv.reshape(bsz, n_mem, M_WIDTH), heads_last(mk_t), heads_last(mv_t))


def kernel(x_prompt, x_sample, state_conv_A, state_ssm_A, cache_k_B, cache_v_B, cache_kidx_B, cache_mem_k,
           cache_mem_v, mem_prompt, g_in, w_in, conv_w_A, a_log_A, dt_bias_A, g_o_A, g_q_B, g_k_B, g_kidx_B,
           g_mem, w_mem_kv, g_q_M, g_k_M, w_out):
    depth = w_in.shape[0]
    assert depth == 1
    l = 0
    bp, t_p, _ = x_prompt.shape
    bs, t_s, _ = x_sample.shape
    wp = _prep_weights(g_in[l], w_in[l], conv_w_A[l], a_log_A[l], dt_bias_A[l], g_o_A[l], g_q_B[l], g_k_B[l],
                       g_kidx_B[l], g_q_M[l], w_out[l])
    mk, mv, mk_heads, mv_heads = _memory_kv(mem_prompt, g_mem[l], w_mem_kv[l], g_k_M[l])
    zero_conv = jnp.zeros((bp, CONV_W - 1, 3 * A_WIDTH), F32)
    zero_ssm = jnp.zeros((bp, A_HEADS, A_DK, A_DV), F32)
    yp, c1, s1, k1, v1, ki1 = _layer(x_prompt, mk, mv, zero_conv, zero_ssm, None, CHUNK, 4 * CHUNK, wp)
    ys, c2, s2, k2, v2, ki2 = _layer(
        x_sample, cache_mem_k[l], cache_mem_v[l],
        state_conv_A[l], state_ssm_A[l], (cache_k_B[l], cache_v_B[l], cache_kidx_B[l]), t_s, t_s, wp)
    st = lambda a: a[None]
    return (yp, ys, st(c1), st(s1), st(k1), st(v1), st(ki1),
            st(mk_heads), st(mv_heads),
            st(c2), st(s2), st(k2), st(v2), st(ki2))
```

```python
import functools
import math

import numpy as np
import jax
import jax.numpy as jnp
from jax import lax
from jax.experimental import pallas as pl
from jax.experimental.pallas import tpu as pltpu

F32 = jnp.float32
BF16 = jnp.bfloat16
HIGHEST = lax.Precision.HIGHEST

D_MODEL = 1024
CHUNK = 64
A_HEADS = 4
A_DK = 128
A_DV = 128
A_WIDTH = A_HEADS * A_DV
CONV_W = 4
B_HEADS = 4
B_HD = 64
B_WIDTH = B_HEADS * B_HD
IDX_HEADS = 8
IDX_HD = 32
IDX_SCALE = (IDX_HEADS ** -0.5) * (IDX_HD ** -0.5)
TOPK_MAX = 256
Q_BLOCK = 128
N_MEM = 256
M_HEADS = 4
M_HD = 64
M_WIDTH = M_HEADS * M_HD
EPS = 1e-6
IN_SPLITS = (3 * A_WIDTH, A_WIDTH, A_HEADS, A_HEADS,
             B_WIDTH, B_WIDTH, B_WIDTH, B_WIDTH, IDX_HEADS * IDX_HD, IDX_HD, IDX_HEADS,
             M_WIDTH, M_WIDTH)

LANES = 128
KEY_BLOCK = 256
DSA_QBLK = 256
PROJ_ROWS = 512
DEC_KEYS_PER_STEP = 1024
SCORE_STRIP = 128
MISC_W = LANES
MISC_WI = IDX_HD
MISC_B = IDX_HD + IDX_HEADS
MISC_A = MISC_B + A_HEADS
VMEM_LIMIT = 48 * 1024 * 1024
NEG_BIG = -1e30
BISECT_PROBES = 3
COUNT_RADIX = 1024
BISECT_BLIND_ITERS = 6
BISECT_MAX_ITERS = 20


def _cparams(sem):
    return pltpu.CompilerParams(dimension_semantics=sem, vmem_limit_bytes=VMEM_LIMIT)


def _dot(a, b):
    return jnp.dot(a, b, preferred_element_type=F32)


def _dot_nt(a, b, precision=None):
    return lax.dot_general(a, b, (((1,), (1,)), ((), ())), precision=precision,
                           preferred_element_type=F32)


def _split2(x):
    hi = x.astype(BF16)
    return hi, (x - hi.astype(F32)).astype(BF16)


def _mm_split(a, b):
    ah, al = a
    bh, _ = b
    n = ah.shape[0]
    t = _dot(jnp.concatenate([ah, al], axis=0), bh)
    return t[:n] + t[n:]


def _silu(x):
    return x * jax.nn.sigmoid(x)


def _lane_mask(width, lo, hi):
    lane = lax.broadcasted_iota(jnp.int32, (1, width), 1)
    return (lane >= lo) & (lane < hi)


CARRY_ROWS = 8


def _conv_silu_norm(xp_ref, raw, cw, rows, part):
    first = CARRY_ROWS - (CONV_W - 1)
    xp_ref[CARRY_ROWS:CARRY_ROWS + rows, :] = raw
    y = xp_ref[first:first + rows, :] * cw[0:1, :]
    for j in range(1, CONV_W):
        y = y + xp_ref[first + j:first + j + rows, :] * cw[j:j + 1, :]
    y = _silu(y)
    conv_new = xp_ref[rows + first:rows + CARRY_ROWS, :]
    xp_ref[0:CARRY_ROWS, :] = xp_ref[rows:rows + CARRY_ROWS, :]
    if part == 2:
        return y, conv_new
    scale = A_DK ** -0.5 if part == 0 else 1.0
    heads = []
    for h in range(A_HEADS):
        yh = y[:, h * A_DK:(h + 1) * A_DK]
        heads.append(yh * (lax.rsqrt(jnp.sum(yh * yh, axis=-1, keepdims=True) + EPS) * scale))
    return jnp.concatenate(heads, axis=1), conv_new


def _conv_part(extra):
    return int(extra[len("conv"):]) if extra.startswith("conv") else None


def _proj_body(*refs, groups, tiles):
    x_ref, g_ref, w_ref = refs[0], refs[1], refs[2]
    n_aux = 3 * sum(1 for _, normed, _ in groups if normed)
    aux = refs[3:3 + n_aux]
    pos = 3 + n_aux
    n_conv = sum(1 for _, _, extra in groups if _conv_part(extra) is not None)
    if n_conv:
        cbuf_ref, cw_ref = refs[pos], refs[pos + 1]
        pos += 2
    outs = refs[pos:pos + len(groups)]
    pos += len(groups)
    n_t = sum(1 for _, _, extra in groups if extra == "t")
    outs_t = list(refs[pos:pos + n_t])
    pos += n_t
    if n_conv:
        convnew_ref = refs[pos]
        xp_refs = refs[pos + 1:pos + 1 + n_conv]

        @pl.when(pl.program_id(0) % tiles == 0)
        def _():
            for part, xp_ref in enumerate(xp_refs):
                xp_ref[0:CARRY_ROWS, :] = cbuf_ref[:, part * A_WIDTH:(part + 1) * A_WIDTH]
    x = x_ref[...]
    r = lax.rsqrt(jnp.mean(x * x, axis=-1, keepdims=True) + EPS)
    hb = (x * g_ref[...]).astype(BF16)
    off = 0
    ai = 0
    for (width, normed, extra), o_ref in zip(groups, outs):
        y = _dot(hb, w_ref[:, off:off + width]) * r
        part = _conv_part(extra)
        if part is not None:
            cols = slice(part * A_WIDTH, (part + 1) * A_WIDTH)
            y, conv_new = _conv_silu_norm(xp_refs[part], y, cw_ref[:, cols], y.shape[0], part)
            convnew_ref[:, cols] = conv_new
        if normed:
            bd_ref, gain_ref, nm_ref = aux[ai], aux[ai + 1], aux[ai + 2]
            ai += 3
            sq = y * y
            hi = sq.astype(BF16)
            lo = (sq - hi.astype(F32)).astype(BF16)
            ms = _dot(hi, bd_ref[...]) + _dot(lo, bd_ref[...])
            scale = jnp.where(nm_ref[...] > 0.0, lax.rsqrt(ms + EPS), 1.0)
            y = y * scale * gain_ref[...]
        o_ref[...] = y
        if extra == "t":
            outs_t.pop(0)[...] = y.T
        off += width


def _group_mean_matrix(width, group, n_lanes):
    i = np.arange(width)
    m = ((i[:, None] // group) == (i[None, :] // group)) & (i[:, None] < n_lanes) & (i[None, :] < n_lanes)
    return jnp.asarray(m.astype(np.float32) / group, dtype=BF16)


def _proj(x, g, w, groups, aux, tm, t_len, conv=None):
    m, d = x.shape
    nw = w.shape[1]
    tiles = t_len // tm
    bsz = m // t_len
    in_specs = [pl.BlockSpec((tm, d), lambda i: (i, 0)),
                pl.BlockSpec((1, d), lambda i: (0, 0)),
                pl.BlockSpec((d, nw), lambda i: (0, 0))]
    for a in aux:
        in_specs.append(pl.BlockSpec(a.shape, lambda i: (0, 0)))
    out_shape = [jax.ShapeDtypeStruct((m, wd), F32) for wd, _, _ in groups]
    out_specs = [pl.BlockSpec((tm, wd), lambda i: (i, 0)) for wd, _, _ in groups]
    for wd, _, extra in groups:
        if extra == "t":
            out_shape.append(jax.ShapeDtypeStruct((bsz, wd, t_len), F32))
            out_specs.append(pl.BlockSpec((None, wd, tm), lambda i: (i // tiles, 0, i % tiles)))
    args = [x, g, w, *aux]
    scratch = []
    if conv is not None:
        cbuf, cw = conv
        wd = cw.shape[1]
        in_specs += [pl.BlockSpec((None, CARRY_ROWS, wd), lambda i: (i // tiles, 0, 0)),
                     pl.BlockSpec(cw.shape, lambda i: (0, 0))]
        args += [cbuf, cw]
        out_shape.append(jax.ShapeDtypeStruct((bsz, CONV_W - 1, wd), F32))
        out_specs.append(pl.BlockSpec((None, CONV_W - 1, wd), lambda i: (i // tiles, 0, 0)))
        scratch += [pltpu.VMEM((tm + CARRY_ROWS, A_WIDTH), F32) for _ in range(wd // A_WIDTH)]
    return pl.pallas_call(
        functools.partial(_proj_body, groups=tuple(groups), tiles=tiles),
        grid=(m // tm,),
        in_specs=in_specs, out_specs=out_specs, out_shape=out_shape, scratch_shapes=scratch,
        compiler_params=_cparams(("arbitrary",) if conv is not None else ("parallel",)),
        name="proj",
    )(*args)


def _gdn_body(*refs, chunk, rows, fused_conv):
    if fused_conv:
        (q_ref, k_ref, v_ref, za_ref, misc_ref, s0_ref, avec_ref, dtvec_ref, go_ref, cbuf_ref, cw_ref,
         o_ref, sfin_ref, convnew_ref, g_sc, b_sc, s_sc, *conv_sc) = refs
    else:
        (q_ref, k_ref, v_ref, za_ref, misc_ref, s0_ref, avec_ref, dtvec_ref, go_ref,
         o_ref, sfin_ref, g_sc, b_sc, s_sc) = refs
    t = pl.program_id(1)
    c = chunk

    @pl.when(t == 0)
    def _():
        s_sc[...] = s0_ref[0]

    if fused_conv:
        xp_refs, act_refs = conv_sc[:3], conv_sc[3:]

        @pl.when(t == 0)
        def _():
            for part, xp_ref in enumerate(xp_refs):
                xp_ref[0:CARRY_ROWS, :] = cbuf_ref[:, part * A_WIDTH:(part + 1) * A_WIDTH]
        for part, raw_ref in enumerate((q_ref, k_ref, v_ref)):
            cols = slice(part * A_WIDTH, (part + 1) * A_WIDTH)
            act, conv_new = _conv_silu_norm(xp_refs[part], raw_ref[...], cw_ref[:, cols], rows, part)
            convnew_ref[:, cols] = conv_new
            act_refs[part][...] = act
        q_sc, k_sc, v_sc = act_refs
    else:
        q_sc, k_sc, v_sc = q_ref, k_ref, v_ref
    misc = misc_ref[...]
    b_sc[...] = jax.nn.sigmoid(misc)
    g_sc[...] = avec_ref[...] * jax.nn.softplus(misc + dtvec_ref[...])

    ri = lax.broadcasted_iota(jnp.int32, (c, c), 0)
    ci = lax.broadcasted_iota(jnp.int32, (c, c), 1)
    tri = ri >= ci
    strict = ri > ci
    eye_f = (ri == ci).astype(F32)
    tri3 = (lax.broadcasted_iota(jnp.int32, (c, 3 * c), 0)
            >= lax.broadcasted_iota(jnp.int32, (c, 3 * c), 1) % c).astype(BF16)
    go = go_ref[...]
    n_chunks = rows // c

    problems = [(ic, h) for ic in range(n_chunks) for h in range(A_HEADS)]
    gc_all, gc_t = [], []
    for ic in range(n_chunks):
        g = g_sc[ic * c:(ic + 1) * c, :]
        g1 = g.astype(BF16)
        r1 = g - g1.astype(F32)
        g2 = r1.astype(BF16)
        g3 = (r1 - g2.astype(F32)).astype(BF16)
        gc_all.append(_dot(tri3, jnp.concatenate([g1, g2, g3], axis=0)))
    for ic in range(n_chunks):
        gc_t.append(gc_all[ic].T)

    def load(ref, ic, h):
        return ref[ic * c:(ic + 1) * c, h * A_DK:(h + 1) * A_DK]

    col = lambda a, lane: a[:, lane:lane + 1]
    k_bf = [load(k_sc, ic, h).astype(BF16) for ic, h in problems]
    kb = [load(k_sc, ic, h) * col(b_sc[ic * c:(ic + 1) * c, :], MISC_B + h) for ic, h in problems]
    kk = [_dot_nt(kb[i].astype(BF16), k_bf[i]) for i in range(len(problems))]
    qk_raw = [_dot_nt(load(q_sc, ic, h).astype(BF16), k_bf[i]) for i, (ic, h) in enumerate(problems)]
    decay = []
    for ic, h in problems:
        diff = col(gc_all[ic], MISC_A + h) - gc_t[ic][MISC_A + h:MISC_A + h + 1, :]
        decay.append(jnp.where(tri, jnp.exp(jnp.where(tri, diff, 0.0)), 0.0))
    lmat = [jnp.where(strict, kk[i] * decay[i], 0.0) for i in range(len(problems))]
    qk = [(qk_raw[i] * decay[i]).astype(BF16) for i in range(len(problems))]
    x_inv = [eye_f - m for m in lmat]
    p_split = [_split2(m) for m in lmat]
    for _ in range(int(math.log2(c)) - 1):
        p_split = [_split2(_mm_split(ps, ps)) for ps in p_split]
        x_inv = [x + _mm_split(_split2(x), ps) for x, ps in zip(x_inv, p_split)]
    sol = []
    for i, (ic, h) in enumerate(problems):
        egc = jnp.exp(col(gc_all[ic], MISC_A + h))
        beta = col(b_sc[ic * c:(ic + 1) * c, :], MISC_B + h)
        rhs = jnp.concatenate([load(v_sc, ic, h) * beta, kb[i] * egc], axis=-1).astype(BF16)
        sol.append(_mm_split(_split2(x_inv[i]), (rhs, None)))
    pre = []
    for i, (ic, h) in enumerate(problems):
        gc = col(gc_all[ic], MISC_A + h)
        g_last = gc[c - 1:c, :]
        k_dec_t = (load(k_sc, ic, h) * jnp.exp(g_last - gc)).T.astype(BF16)
        wq = jnp.concatenate([sol[i][:, A_DV:], load(q_sc, ic, h) * jnp.exp(gc)], axis=0).astype(BF16)
        pre.append((sol[i][:, :A_DV], wq, k_dec_t, jnp.exp(g_last)))

    heads = range(A_HEADS)
    s_cur = [s_sc[h] for h in heads]
    for ic in range(n_chunks):
        pr = [pre[ic * A_HEADS + h] for h in heads]
        ws = [_dot(pr[h][1], s_cur[h].astype(BF16)) for h in heads]
        ub = [(pr[h][0] - ws[h][:c]).astype(BF16) for h in heads]
        o = [ws[h][c:] + _dot(qk[ic * A_HEADS + h], ub[h]) for h in heads]
        s_cur = [s_cur[h] * pr[h][3] + _dot(pr[h][2], ub[h]) for h in heads]
        for h in heads:
            on = o[h] * lax.rsqrt(jnp.mean(o[h] * o[h], axis=-1, keepdims=True) + EPS) * go
            o_ref[ic * c:(ic + 1) * c, h * A_DV:(h + 1) * A_DV] = on * _silu(load(za_ref, ic, h))
    for h in heads:
        s_sc[h] = s_cur[h]
    sfin_ref[0] = s_sc[...]


def _gdn(q, k, v, za, misc, s0, avec, dtvec, g_o, *, bsz, t_len, chunk, rows, conv=None):
    nt = t_len // rows
    w3 = 3 * A_WIDTH
    row_map = lambda b, t: (b * nt + t, 0)
    const2 = lambda b, t: (0, 0)
    state_spec = pl.BlockSpec((1, A_HEADS, A_DK, A_DV), lambda b, t: (b, 0, 0, 0))
    in_specs = [pl.BlockSpec((rows, A_WIDTH), row_map)] * 4 + [
                pl.BlockSpec((rows, MISC_W), row_map),
                state_spec,
                pl.BlockSpec((1, MISC_W), const2),
                pl.BlockSpec((1, MISC_W), const2),
                pl.BlockSpec((1, A_DV), const2)]
    args = [q, k, v, za, misc, s0, avec, dtvec, g_o]
    out_specs = [pl.BlockSpec((rows, A_WIDTH), row_map), state_spec]
    out_shape = [jax.ShapeDtypeStruct((bsz * t_len, A_WIDTH), F32),
                 jax.ShapeDtypeStruct((bsz, A_HEADS, A_DK, A_DV), F32)]
    scratch = [pltpu.VMEM((rows, MISC_W), F32),
               pltpu.VMEM((rows, MISC_W), F32),
               pltpu.VMEM((A_HEADS, A_DK, A_DV), F32)]
    if conv is not None:
        cbuf, cw = conv
        in_specs += [pl.BlockSpec((None, CARRY_ROWS, w3), lambda b, t: (b, 0, 0)),
                     pl.BlockSpec((CONV_W, w3), const2)]
        args += [cbuf, cw]
        out_specs.append(pl.BlockSpec((None, CONV_W - 1, w3), lambda b, t: (b, 0, 0)))
        out_shape.append(jax.ShapeDtypeStruct((bsz, CONV_W - 1, w3), F32))
        scratch += [pltpu.VMEM((rows + CARRY_ROWS, A_WIDTH), F32)] * 3 + [pltpu.VMEM((rows, A_WIDTH), F32)] * 3
    return pl.pallas_call(
        functools.partial(_gdn_body, chunk=chunk, rows=rows, fused_conv=conv is not None),
        grid=(bsz, nt),
        in_specs=in_specs, out_specs=out_specs, out_shape=out_shape, scratch_shapes=scratch,
        compiler_params=_cparams(("parallel", "arbitrary")),
        name="gdn",
    )(*args)


def _topk_threshold(sc_ref, nkb, shape, key_axis, n_sel, n_adm):
    kf = float(n_sel)
    kblk = shape[key_axis]

    if key_axis == 0:
        strip = min(kblk, 32)
        part_shape = (strip, shape[1])
        strip_of = lambda kb, i: sc_ref[kb, i * strip:(i + 1) * strip, :]

        def put_strip(kb, i, v):
            sc_ref[kb, i * strip:(i + 1) * strip, :] = v
    else:
        strip = kblk
        part_shape = shape
        strip_of = lambda kb, i: sc_ref[kb]

        def put_strip(kb, i, v):
            sc_ref[kb] = v
    kpos = lax.broadcasted_iota(jnp.int32, part_shape, key_axis)

    def reduce_blocks(specs):
        def body(kb, accs):
            accs = list(accs)
            for i in range(kblk // strip):
                s = strip_of(kb, i)
                k0 = kb * kblk + i * strip
                accs = [cmb(acc, fn(s, k0)) for acc, (fn, _, _, _, cmb) in zip(accs, specs)]
            return tuple(accs)
        init = tuple(jnp.full(part_shape, i, dt) for _, i, dt, _, _ in specs)
        accs = lax.fori_loop(0, nkb, body, init)
        return [acc if op is None else op(acc, axis=key_axis, keepdims=True)
                for acc, (_, _, _, op, _) in zip(accs, specs)]

    one_if = lambda c: jnp.where(c, 1.0, 0.0)
    vmax, vmin, cpos, czero = reduce_blocks([
        (lambda s, k0: s, -jnp.inf, F32, jnp.max, jnp.maximum),
        (lambda s, k0: jnp.where(s == -jnp.inf, jnp.inf, s), jnp.inf, F32, jnp.min, jnp.minimum),
        (lambda s, k0: one_if(s > 0.0), 0.0, F32, jnp.sum, jnp.add),
        (lambda s, k0: one_if(s == 0.0), 0.0, F32, jnp.sum, jnp.add)])

    long_row = n_adm > kf
    pos_row = long_row & (cpos >= kf)
    zero_row = long_row & (cpos < kf) & (cpos + czero >= kf)
    neg_row = long_row & (cpos + czero < kf)
    lo0 = jnp.where(pos_row | zero_row, 0.0, vmin - jnp.maximum(1.0, jnp.abs(vmin)))
    cnt_lo0 = jnp.where(pos_row, cpos, jnp.where(zero_row, kf, n_adm))
    hi0 = jnp.where(neg_row, 0.0, vmax)
    cnt_hi0 = jnp.where(neg_row, cpos, 0.0)
    need_zero = jnp.where(zero_row, kf - cpos, 0.0)

    def narrow(lo, hi, cnt_lo, cnt_hi):
        fracs = [(t + 1.0) / (BISECT_PROBES + 1.0) for t in range(BISECT_PROBES)]
        mids = [lo * (1.0 - f) + hi * f for f in fracs]
        for t in range(1, BISECT_PROBES):
            mids[t] = jnp.maximum(mids[t], mids[t - 1])
        codes = [sum(COUNT_RADIX ** u for u in range(t + 1)) for t in range(BISECT_PROBES)]

        def encode(s, k0):
            e = jnp.zeros(s.shape, jnp.int32)
            for mid, code in zip(mids, codes):
                e = jnp.where(s > mid, code, e)
            return e
        (packed,) = reduce_blocks([(encode, 0, jnp.int32, None, jnp.add)])
        cnts = []
        for t in range(BISECT_PROBES):
            digit = packed & (COUNT_RADIX - 1) if t < BISECT_PROBES - 1 else packed
            packed = packed >> COUNT_RADIX.bit_length() - 1
            cnts.append(jnp.sum(digit.astype(F32), axis=key_axis, keepdims=True))
        lo_n, cnt_lo_n, hi_n, cnt_hi_n = lo, cnt_lo, hi, cnt_hi
        for mid, cm in zip(mids, cnts):
            up = cm >= kf
            lo_n = jnp.where(up, mid, lo_n)
            cnt_lo_n = jnp.where(up, cm, cnt_lo_n)
        for mid, cm in zip(mids[::-1], cnts[::-1]):
            dn = cm < kf
            hi_n = jnp.where(dn, mid, hi_n)
            cnt_hi_n = jnp.where(dn, cm, cnt_hi_n)
        return lo_n, hi_n, cnt_lo_n, cnt_hi_n

    def unresolved(cnt_lo):
        return (jnp.max(cnt_lo) > kf).astype(jnp.int32)

    blind = jnp.where(unresolved(cnt_lo0) > 0, BISECT_BLIND_ITERS, 0)
    state = lax.fori_loop(0, blind, lambda i, st: narrow(*st), (lo0, hi0, cnt_lo0, cnt_hi0))

    def cond(st):
        return (st[0] < BISECT_MAX_ITERS) & (st[1] > 0)

    def body(st):
        new = narrow(*st[2:])
        return (st[0] + 1, unresolved(new[2])) + new

    _, _, lo, hi, cnt_lo, cnt_hi = lax.while_loop(
        cond, body, (blind, unresolved(state[2])) + state)

    fix = cnt_lo > kf
    need0 = jnp.where(fix, kf - cnt_hi, 0.0)
    big_idx = 2 ** 30

    def fix_cond(need):
        return jnp.max(need) > 0.0

    def fix_body(need):
        def in_cluster(s):
            return (s > lo) & (s <= hi)
        (mval,) = reduce_blocks([(lambda s, k0: jnp.where(in_cluster(s), s, -jnp.inf), -jnp.inf, F32,
                                  jnp.max, jnp.maximum)])
        (idx,) = reduce_blocks(
            [(lambda s, k0: jnp.where(in_cluster(s) & (s == mval), kpos + k0, big_idx),
              big_idx, jnp.int32, jnp.min, jnp.minimum)])
        active = need > 0.0

        def promote(kb, carry):
            for i in range(kblk // strip):
                s = strip_of(kb, i)
                put_strip(kb, i, jnp.where(active & ((kpos + kb * kblk + i * strip) == idx), jnp.inf, s))
            return carry
        lax.fori_loop(0, nkb, promote, 0)
        return jnp.where(active, need - 1.0, need)

    lax.while_loop(fix_cond, fix_body, need0)

    @pl.when(jnp.max(need_zero) > 0.0)
    def _():
        r = lax.broadcasted_iota(jnp.int32, (kblk, kblk), 0)
        c = lax.broadcasted_iota(jnp.int32, (kblk, kblk), 1)
        tri = ((r >= c) if key_axis == 0 else (r <= c)).astype(BF16)

        def tie_blk(kb, seen):
            s = sc_ref[kb]
            z = (s == 0.0) & zero_row
            zb = one_if(z).astype(BF16)
            if key_axis == 0:
                rank = _dot(tri, zb) + seen
                last = rank[kblk - 1:kblk, :]
            else:
                rank = _dot(zb, tri) + seen
                last = rank[:, kblk - 1:kblk]
            sc_ref[kb] = jnp.where(z & (rank <= need_zero), jnp.inf, s)
            return last
        lax.fori_loop(0, nkb, tie_blk, jnp.zeros_like(need_zero))

    return jnp.where(fix, hi, lo)


def _dsa_body(qb_ref, qi_ref, miscq_ref, zb_ref, k_ref, vt_ref, misck_ref, o_ref, sc_ref, s_ref, *, n_sel, qblk):
    j = pl.program_id(1)
    nkb = ((j + 1) * qblk + KEY_BLOCK - 1) // KEY_BLOCK
    qi = qi_ref[...]
    q_h = [qi[:, h * IDX_HD:(h + 1) * IDX_HD].astype(BF16) for h in range(IDX_HEADS)]
    w_t = miscq_ref[...].T[MISC_WI:MISC_WI + IDX_HEADS, :]
    qchunk = (j * qblk + lax.broadcasted_iota(jnp.int32, (1, qblk), 1)) // CHUNK
    krow = lax.broadcasted_iota(jnp.int32, (KEY_BLOCK, 1), 0)

    def score_blk(kb, carry):
        k0 = pl.multiple_of(kb * KEY_BLOCK, KEY_BLOCK)
        for half in range(KEY_BLOCK // SCORE_STRIP):
            r0 = half * SCORE_STRIP
            ki = misck_ref[pl.ds(k0 + r0, SCORE_STRIP), 0:IDX_HD].astype(BF16)
            acc = jnp.zeros((SCORE_STRIP, qblk), F32)
            for h in range(IDX_HEADS):
                acc = acc + w_t[h:h + 1, :] * jnp.maximum(_dot_nt(ki, q_h[h]), 0.0)
            adm = ((k0 + r0 + krow[:SCORE_STRIP]) // CHUNK) <= qchunk
            sc_ref[kb, r0:r0 + SCORE_STRIP, :] = jnp.where(adm, acc, -jnp.inf)
        return carry

    npairs = (nkb + 1) // 2

    def pair_loop(body, init):
        return lax.fori_loop(0, npairs, lambda i, c: body(2 * i + 1, body(2 * i, c)), init)

    pair_loop(score_blk, 0)

    n_adm = ((qchunk + 1) * CHUNK).astype(F32)
    thr = _topk_threshold(sc_ref, nkb, (KEY_BLOCK, qblk), 0, n_sel, n_adm)

    qb = qb_ref[...] * (B_HD ** -0.5 * math.log2(math.e))
    qs = jnp.concatenate([jnp.where(_lane_mask(B_WIDTH, h * B_HD, (h + 1) * B_HD), qb, 0.0)
                          for h in range(B_HEADS)], axis=0).astype(BF16)
    wide = B_HEADS * qblk

    def fold(acc, x, combine):
        for i in range(x.shape[0] // 8):
            acc = combine(acc, x[8 * i:8 * (i + 1)])
        return acc

    def logits_blk(kb, m8):
        k0 = pl.multiple_of(kb * KEY_BLOCK, KEY_BLOCK)
        s = _dot_nt(k_ref[pl.ds(k0, KEY_BLOCK), :].astype(BF16), qs)
        sel = sc_ref[kb] > thr
        s = jnp.where(jnp.concatenate([sel] * B_HEADS, axis=1), s, NEG_BIG)
        s_ref[kb] = s
        return fold(m8, s, jnp.maximum)

    m8 = pair_loop(logits_blk, jnp.full((8, wide), NEG_BIG, F32))
    m = jnp.max(m8, axis=0, keepdims=True)

    def pv_blk(kb, carry):
        l8, accs = carry
        p = jnp.exp2(s_ref[kb] - m)
        pb = p.astype(BF16)
        k0 = pl.multiple_of(kb * KEY_BLOCK, KEY_BLOCK)
        vt = vt_ref[:, pl.ds(k0, KEY_BLOCK)].astype(BF16)
        accs = tuple(accs[h] + _dot(vt[h * B_HD:(h + 1) * B_HD, :], pb[:, h * qblk:(h + 1) * qblk])
                     for h in range(B_HEADS))
        return fold(l8, p, jnp.add), accs

    l8, accs = pair_loop(
        pv_blk, (jnp.zeros((8, wide), F32), tuple(jnp.zeros((B_HD, qblk), F32) for _ in range(B_HEADS))))
    l = jnp.sum(l8, axis=0, keepdims=True)
    o_t = jnp.concatenate([accs[h] / l[:, h * qblk:(h + 1) * qblk] for h in range(B_HEADS)], axis=0)
    o_ref[...] = o_t.T * _silu(zb_ref[...])


def _dsa_prompt(qb, qi, misc, zb, kb, vt, *, bsz, t_len, n_sel, qblk):
    nq = t_len // qblk
    nkb = t_len // KEY_BLOCK
    assert nkb * (KEY_BLOCK // 8) < COUNT_RADIX and nkb % 2 == 0
    qmap = lambda b, j: (b * nq + j, 0)
    kmap = lambda b, j: (b, 0)
    return pl.pallas_call(
        functools.partial(_dsa_body, n_sel=n_sel, qblk=qblk),
        grid=(bsz, nq),
        in_specs=[pl.BlockSpec((qblk, B_WIDTH), qmap),
                  pl.BlockSpec((qblk, IDX_HEADS * IDX_HD), qmap),
                  pl.BlockSpec((qblk, MISC_W), qmap),
                  pl.BlockSpec((qblk, B_WIDTH), qmap),
                  pl.BlockSpec((t_len, B_WIDTH), kmap),
                  pl.BlockSpec((None, B_WIDTH, t_len), lambda b, j: (b, 0, 0)),
                  pl.BlockSpec((t_len, MISC_W), kmap)],
        out_specs=pl.BlockSpec((qblk, B_WIDTH), qmap),
        out_shape=jax.ShapeDtypeStruct((bsz * t_len, B_WIDTH), F32),
        scratch_shapes=[pltpu.VMEM((nkb, KEY_BLOCK, qblk), F32),
                        pltpu.VMEM((nkb, KEY_BLOCK, B_HEADS * qblk), F32)],
        compiler_params=_cparams(("parallel", "arbitrary")),
        name="dsa_prompt",
    )(qb, qi, misc, zb, kb, vt, misc)


def _dsa_dec_body(qb_ref, qi_ref, misc_ref, zb_ref, kn_ref, vn_ref, ck_ref, cv_ref, cki_ref, o_ref,
                  sc_ref, thr_ref, m_ref, l_ref, acc_ref, *, p_len, tq, n_sel, kstep):
    step = pl.program_id(1)
    nsteps = p_len // kstep
    sub = kstep // KEY_BLOCK
    npast = p_len // KEY_BLOCK
    nkb = npast + 1
    pad_rows = KEY_BLOCK - tq
    rows = B_HEADS * tq

    @pl.when(step == 0)
    def _():
        qi = qi_ref[...]
        misc = misc_ref[...]
        q2 = jnp.concatenate([qi[:, h * IDX_HD:(h + 1) * IDX_HD] for h in range(IDX_HEADS)],
                             axis=0).astype(BF16)
        w_h = [misc[:, MISC_WI + h:MISC_WI + h + 1] for h in range(IDX_HEADS)]

        def head_sum(lg):
            acc = jnp.zeros((tq, lg.shape[1]), F32)
            for h in range(IDX_HEADS):
                acc = acc + w_h[h] * jnp.maximum(lg[h * tq:(h + 1) * tq, :], 0.0)
            return acc

        for i in range(nsteps):
            sc = head_sum(_dot(q2, cki_ref[:, i * kstep:(i + 1) * kstep].astype(BF16)))
            for u in range(sub):
                sc_ref[i * sub + u] = sc[:, u * KEY_BLOCK:(u + 1) * KEY_BLOCK]
        lane = lax.broadcasted_iota(jnp.int32, (1, KEY_BLOCK), 1)
        ki_new = jnp.concatenate([misc[:, 0:IDX_HD], jnp.zeros((pad_rows, IDX_HD), F32)], axis=0)
        sc_ref[npast] = jnp.where(lane < tq, head_sum(_dot_nt(q2, ki_new.astype(BF16))), -jnp.inf)
        n_adm = jnp.full((tq, 1), float(p_len + tq), F32)
        thr_ref[...] = _topk_threshold(sc_ref, nkb, (tq, KEY_BLOCK), 1, n_sel, n_adm)
        m_ref[...] = jnp.full((rows, 1), NEG_BIG, F32)
        l_ref[...] = jnp.zeros((rows, 1), F32)
        acc_ref[...] = jnp.zeros((rows, B_HD), F32)

    qb = qb_ref[...] * (B_HD ** -0.5 * math.log2(math.e))
    q_h = [qb[:, h * B_HD:(h + 1) * B_HD].astype(BF16) for h in range(B_HEADS)]
    thr = thr_ref[...]

    def fold_in(logits_of, pv_of, sel):
        s = jnp.concatenate([jnp.where(sel, logits_of(h), NEG_BIG) for h in range(B_HEADS)], axis=0)
        m_old = m_ref[...]
        m_new = jnp.maximum(m_old, jnp.max(s, axis=-1, keepdims=True))
        alpha = jnp.exp2(m_old - m_new)
        p = jnp.exp2(s - m_new)
        pb = p.astype(BF16)
        pv = jnp.concatenate([pv_of(h, pb[h * tq:(h + 1) * tq, :]) for h in range(B_HEADS)], axis=0)
        l_ref[...] = alpha * l_ref[...] + jnp.sum(p, axis=-1, keepdims=True)
        acc_ref[...] = alpha * acc_ref[...] + pv
        m_ref[...] = m_new

    sel = jnp.concatenate([sc_ref[step * sub + u] > thr for u in range(sub)], axis=1)
    fold_in(lambda h: _dot(q_h[h], ck_ref[h].astype(BF16)),
            lambda h, p: _dot_nt(p, cv_ref[h].astype(BF16)), sel)

    @pl.when(step == nsteps - 1)
    def _():
        zpad = jnp.zeros((pad_rows, B_WIDTH), F32)
        kn = jnp.concatenate([kn_ref[...], zpad], axis=0).astype(BF16)
        vn = jnp.concatenate([vn_ref[...], zpad], axis=0).astype(BF16)
        fold_in(lambda h: _dot_nt(q_h[h], kn[:, h * B_HD:(h + 1) * B_HD]),
                lambda h, p: _dot(p, vn[:, h * B_HD:(h + 1) * B_HD]), sc_ref[npast] > thr)
        res = acc_ref[...] / l_ref[...]
        out = jnp.concatenate([res[h * tq:(h + 1) * tq, :] for h in range(B_HEADS)], axis=1)
        o_ref[...] = out * _silu(zb_ref[...])


def _dsa_decode(qb, qi, misc, zb, kb, vb, ck, cv, cki, *, bsz, t_len, p_len, n_sel):
    kstep = DEC_KEYS_PER_STEP
    assert p_len // KEY_BLOCK + 1 < COUNT_RADIX and p_len % kstep == 0
    rows = B_HEADS * t_len
    qmap = lambda b, s: (b, 0)
    cmap = lambda b, s: (b, 0, 0, s)
    ck, cv = (a.transpose(0, 2, 3, 1) for a in (ck, cv))
    cki = cki.transpose(0, 2, 1)
    return pl.pallas_call(
        functools.partial(_dsa_dec_body, p_len=p_len, tq=t_len, n_sel=n_sel, kstep=kstep),
        grid=(bsz, p_len // kstep),
        in_specs=[pl.BlockSpec((t_len, B_WIDTH), qmap),
                  pl.BlockSpec((t_len, IDX_HEADS * IDX_HD), qmap),
                  pl.BlockSpec((t_len, MISC_W), qmap),
                  pl.BlockSpec((t_len, B_WIDTH), qmap),
                  pl.BlockSpec((t_len, B_WIDTH), qmap),
                  pl.BlockSpec((t_len, B_WIDTH), qmap),
                  pl.BlockSpec((None, B_HEADS, B_HD, kstep), cmap),
                  pl.BlockSpec((None, B_HEADS, B_HD, kstep), cmap),
                  pl.BlockSpec((None, IDX_HD, p_len), lambda b, s: (b, 0, 0))],
        out_specs=pl.BlockSpec((t_len, B_WIDTH), qmap),
        out_shape=jax.ShapeDtypeStruct((bsz * t_len, B_WIDTH), F32),
        scratch_shapes=[pltpu.VMEM((p_len // KEY_BLOCK + 1, t_len, KEY_BLOCK), F32),
                        pltpu.VMEM((t_len, 1), F32),
                        pltpu.VMEM((rows, 1), F32),
                        pltpu.VMEM((rows, 1), F32),
                        pltpu.VMEM((rows, B_HD), F32)],
        compiler_params=_cparams(("parallel", "arbitrary")),
        name="dsa_decode",
    )(qb, qi, misc, zb, kb, vb, ck, cv, cki)


def _mem_body(qm_ref, zm_ref, mk_ref, mv_ref, o_ref):
    qm = qm_ref[...] * (M_HD ** -0.5)
    mk = mk_ref[...].reshape(N_MEM, M_WIDTH).astype(BF16)
    mv = mv_ref[...].reshape(N_MEM, M_WIDTH).astype(BF16)
    out = jnp.zeros(qm.shape, F32)
    for h in range(M_HEADS):
        mask = _lane_mask(M_WIDTH, h * M_HD, (h + 1) * M_HD)
        s = _dot_nt(jnp.where(mask, qm, 0.0).astype(BF16), mk)
        p = jnp.exp(s - jnp.max(s, axis=-1, keepdims=True))
        l = jnp.sum(p, axis=-1, keepdims=True)
        out = out + jnp.where(mask, _dot(p.astype(BF16), mv) / l, 0.0)
    o_ref[...] = out * _silu(zm_ref[...])


def _mem_attn(qm, zm, mk, mv, *, bsz, t_len, tq):
    nq = t_len // tq
    qmap = lambda b, j: (b * nq + j, 0)
    mem_spec = pl.BlockSpec((None,) + mk.shape[1:], lambda b, j: (b,) + (0,) * (mk.ndim - 1))
    return pl.pallas_call(
        _mem_body,
        grid=(bsz, nq),
        in_specs=[pl.BlockSpec((tq, M_WIDTH), qmap),
                  pl.BlockSpec((tq, M_WIDTH), qmap),
                  mem_spec, mem_spec],
        out_specs=pl.BlockSpec((tq, M_WIDTH), qmap),
        out_shape=jax.ShapeDtypeStruct((bsz * t_len, M_WIDTH), F32),
        compiler_params=_cparams(("parallel", "parallel")),
        name="mem_attn",
    )(qm, zm, mk, mv)


def _oproj_body(x_ref, oa_ref, ob_ref, om_ref, w_ref, y_ref):
    acc = _dot(oa_ref[...].astype(BF16), w_ref[0:A_WIDTH, :])
    acc = acc + _dot(ob_ref[...].astype(BF16), w_ref[A_WIDTH:A_WIDTH + B_WIDTH, :])
    acc = acc + _dot(om_ref[...].astype(BF16), w_ref[A_WIDTH + B_WIDTH:, :])
    y_ref[...] = x_ref[...] + acc


def _oproj(x, oa, ob, om, w, tm):
    m, d = x.shape
    rmap = lambda i: (i, 0)
    return pl.pallas_call(
        _oproj_body,
        grid=(m // tm,),
        in_specs=[pl.BlockSpec((tm, d), rmap),
                  pl.BlockSpec((tm, A_WIDTH), rmap),
                  pl.BlockSpec((tm, B_WIDTH), rmap),
                  pl.BlockSpec((tm, M_WIDTH), rmap),
                  pl.BlockSpec(w.shape, lambda i: (0, 0))],
        out_specs=pl.BlockSpec((tm, d), rmap),
        out_shape=jax.ShapeDtypeStruct((m, d), F32),
        compiler_params=_cparams(("parallel",)),
        name="oproj",
    )(x, oa, ob, om, w)


def _prep_weights(g_in, w_in, conv_w, a_log, dt_bias, g_o, g_qb, g_kb, g_ki, g_qm, w_out):
    offs = np.concatenate([[0], np.cumsum(IN_SPLITS)])
    col = lambda i: w_in[:, int(offs[i]):int(offs[i + 1])]
    (w_qkv, w_za, w_ba, w_aa, w_qb, w_kb, w_vb, w_zb, w_qi, w_ki, w_wi, w_qm, w_zm) = [col(i) for i in range(13)]
    d = w_in.shape[0]
    w_misc = jnp.concatenate(
        [w_ki, w_wi, w_ba, w_aa, jnp.zeros((d, MISC_W - MISC_A - A_HEADS), w_in.dtype)], axis=1)
    w_qa, w_ka, w_va = (w_qkv[:, i * A_WIDTH:(i + 1) * A_WIDTH] for i in range(3))
    w_all = jnp.concatenate([w_qa, w_za, w_qb, w_ka, w_kb, w_vb, w_va, w_zb, w_qi, w_qm, w_zm, w_misc],
                            axis=1).astype(BF16)
    groups = [(A_WIDTH, False, "conv0"), (A_WIDTH, False, ""), (B_WIDTH, True, ""),
              (A_WIDTH, False, "conv1"), (B_WIDTH, True, "t"), (B_WIDTH, False, "t"),
              (A_WIDTH, False, "conv2"), (B_WIDTH, False, ""),
              (IDX_HEADS * IDX_HD, False, ""), (M_WIDTH, True, ""), (M_WIDTH, False, ""),
              (MISC_W, True, "")]
    bd64 = _group_mean_matrix(B_WIDTH, B_HD, B_WIDTH)
    ones256 = jnp.ones((1, B_WIDTH), F32)
    misc_gain = jnp.concatenate([g_ki, jnp.full((IDX_HEADS,), IDX_SCALE, F32),
                                 jnp.ones((MISC_W - MISC_B,), F32)])[None, :]
    misc_nm = (jnp.arange(MISC_W) < IDX_HD).astype(F32)[None, :]
    aux = [bd64, jnp.tile(g_qb, B_HEADS)[None, :], ones256,
           bd64, jnp.tile(g_kb, B_HEADS)[None, :], ones256,
           bd64, jnp.tile(g_qm, M_HEADS)[None, :], ones256,
           _group_mean_matrix(MISC_W, IDX_HD, IDX_HD), misc_gain, misc_nm]
    pad_a = lambda v: jnp.zeros((1, MISC_W), F32).at[0, MISC_A:MISC_A + A_HEADS].set(v)
    return dict(g_in=g_in[None, :], w_all=w_all, groups=groups, aux=aux, conv_w=conv_w,
                avec=pad_a(-jnp.exp(a_log)), dtvec=pad_a(dt_bias), g_o=g_o[None, :],
                w_out=w_out.astype(BF16))


def _layer(x, mem_k, mem_v, conv_buf, s0, past, gdn_chunk, gdn_rows, wp):
    bsz, t_len, d = x.shape
    m = bsz * t_len
    x2 = x.reshape(m, d)
    tm = min(PROJ_ROWS, m)
    heads_last = lambda a_t: a_t.reshape(bsz, B_HEADS, B_HD, t_len).transpose(0, 3, 1, 2)
    cbuf = jnp.concatenate([jnp.zeros((bsz, CARRY_ROWS - (CONV_W - 1), 3 * A_WIDTH), F32), conv_buf], axis=1)
    conv = (cbuf, wp["conv_w"])
    gdn_args = dict(bsz=bsz, t_len=t_len, chunk=gdn_chunk, rows=gdn_rows)
    if past is None:
        (qa, za, qb, ka, kb, vb, va, zb, qi, qm, zm, misc, kt, vt, conv_new) = _proj(
            x2, wp["g_in"], wp["w_all"], wp["groups"], wp["aux"], tm, t_len, conv=conv)
        k_new, v_new = heads_last(kt), heads_last(vt)
        oa, s_new = _gdn(qa, ka, va, za, misc, s0, wp["avec"], wp["dtvec"], wp["g_o"], **gdn_args)
    else:
        groups = [(wd, normed, "") for wd, normed, _ in wp["groups"]]
        (qa, za, qb, ka, kb, vb, va, zb, qi, qm, zm, misc) = _proj(x2, wp["g_in"], wp["w_all"], groups,
                                                                   wp["aux"], tm, m)
        k_new, v_new = (a.reshape(bsz, t_len, B_HEADS, B_HD) for a in (kb, vb))
        oa, s_new, conv_new = _gdn(qa, ka, va, za, misc, s0, wp["avec"], wp["dtvec"], wp["g_o"], conv=conv,
                                   **gdn_args)
    if past is None:
        n_sel = min(TOPK_MAX, t_len // 4)
        ob = _dsa_prompt(qb, qi, misc, zb, kb, vt, bsz=bsz, t_len=t_len, n_sel=n_sel, qblk=DSA_QBLK)
    else:
        ck, cv, cki = past
        p_len = ck.shape[1]
        assert (p_len + t_len - 1) // CHUNK <= p_len // CHUNK and p_len % KEY_BLOCK == 0
        n_sel = min(TOPK_MAX, (p_len + t_len) // 4)
        ob = _dsa_decode(qb, qi, misc, zb, kb, vb, ck, cv, cki, bsz=bsz, t_len=t_len, p_len=p_len,
                         n_sel=n_sel)
    om = _mem_attn(qm, zm, mem_k, mem_v, bsz=bsz, t_len=t_len, tq=min(256, t_len))
    y = _oproj(x2, oa, ob, om, wp["w_out"], min(512, m))
    return (y.reshape(bsz, t_len, d), conv_new, s_new, k_new, v_new,
            misc[:, :IDX_HD].reshape(bsz, t_len, IDX_HD))


def _memory_kv(mem, g_mem, w_mem_kv, g_km):
    bsz, n_mem, d = mem.shape
    groups = [(M_WIDTH, True, "t"), (M_WIDTH, False, "t")]
    aux = [_group_mean_matrix(M_WIDTH, M_HD, M_WIDTH), jnp.tile(g_km, M_HEADS)[None, :],
           jnp.ones((1, M_WIDTH), F32)]
    mk, mv, mk_t, mv_t = _proj(mem.reshape(bsz * n_mem, d), g_mem[None, :], w_mem_kv.astype(BF16),
                               groups, aux, n_mem, n_mem)
    heads_last = lambda a_t: a_t.reshape(bsz, M_HEADS, M_HD, n_mem).transpose(0, 3, 1, 2)
    return (mk.reshape(bsz, n_mem, M_WIDTH), mv.reshape(bsz, n_mem, M_WIDTH), heads_last(mk_t), heads_last(mv_t))


def kernel(x_prompt, x_sample, state_conv_A, state_ssm_A, cache_k_B, cache_v_B, cache_kidx_B, cache_mem_k,
           cache_mem_v, mem_prompt, g_in, w_in, conv_w_A, a_log_A, dt_bias_A, g_o_A, g_q_B, g_k_B, g_kidx_B,
           g_mem, w_mem_kv, g_q_M, g_k_M, w_out):
    depth = w_in.shape[0]
    assert depth == 1
    l = 0
    bp, t_p, _ = x_prompt.shape
    bs, t_s, _ = x_sample.shape
    wp = _prep_weights(g_in[l], w_in[l], conv_w_A[l], a_log_A[l], dt_bias_A[l], g_o_A[l], g_q_B[l], g_k_B[l],
                       g_kidx_B[l], g_q_M[l], w_out[l])
    mk, mv, mk_heads, mv_heads = _memory_kv(mem_prompt, g_mem[l], w_mem_kv[l], g_k_M[l])
    zero_conv = jnp.zeros((bp, CONV_W - 1, 3 * A_WIDTH), F32)
    zero_ssm = jnp.zeros((bp, A_HEADS, A_DK, A_DV), F32)
    yp, c1, s1, k1, v1, ki1 = _layer(x_prompt, mk, mv, zero_conv, zero_ssm, None, CHUNK, 4 * CHUNK, wp)
    ys, c2, s2, k2, v2, ki2 = _layer(
        x_sample, cache_mem_k[l], cache_mem_v[l],
        state_conv_A[l], state_ssm_A[l], (cache_k_B[l], cache_v_B[l], cache_kidx_B[l]), t_s, t_s, wp)
    st = lambda a: a[None]
    return (yp, ys, st(c1), st(s1), st(k1), st(v1), st(ki1),
            st(mk_heads), st(mv_heads),
            st(c2), st(s2), st(k2), st(v2), st(ki2))
```

```python
import functools
import math

import numpy as np
import jax
import jax.numpy as jnp
from jax import lax
from jax.experimental import pallas as pl
from jax.experimental.pallas import tpu as pltpu

F32 = jnp.float32
BF16 = jnp.bfloat16
HIGHEST = lax.Precision.HIGHEST

D_MODEL = 1024
CHUNK = 64
A_HEADS = 4
A_DK = 128
A_DV = 128
A_WIDTH = A_HEADS * A_DV
CONV_W = 4
B_HEADS = 4
B_HD = 64
B_WIDTH = B_HEADS * B_HD
IDX_HEADS = 8
IDX_HD = 32
IDX_SCALE = (IDX_HEADS ** -0.5) * (IDX_HD ** -0.5)
TOPK_MAX = 256
Q_BLOCK = 128
N_MEM = 256
M_HEADS = 4
M_HD = 64
M_WIDTH = M_HEADS * M_HD
EPS = 1e-6
IN_SPLITS = (3 * A_WIDTH, A_WIDTH, A_HEADS, A_HEADS,
             B_WIDTH, B_WIDTH, B_WIDTH, B_WIDTH, IDX_HEADS * IDX_HD, IDX_HD, IDX_HEADS,
             M_WIDTH, M_WIDTH)

LANES = 128
KEY_BLOCK = 256
DSA_QBLK = 256
TAIL_ROWS = 512
PROJ_ROWS = 512
DEC_KEYS_PER_STEP = 1024
SCORE_STRIP = 128
MISC_W = LANES
MISC_WI = IDX_HD
MISC_B = IDX_HD + IDX_HEADS
MISC_A = MISC_B + A_HEADS
VMEM_LIMIT = 48 * 1024 * 1024
NEG_BIG = -1e30
BISECT_PROBES = 3
COUNT_RADIX = 1024
BISECT_BLIND_ITERS = 6
BISECT_MAX_ITERS = 20


def _cparams(sem):
    return pltpu.CompilerParams(dimension_semantics=sem, vmem_limit_bytes=VMEM_LIMIT)


def _dot(a, b):
    return jnp.dot(a, b, preferred_element_type=F32)


def _dot_nt(a, b, precision=None):
    return lax.dot_general(a, b, (((1,), (1,)), ((), ())), precision=precision,
                           preferred_element_type=F32)


def _split2(x):
    hi = x.astype(BF16)
    return hi, (x - hi.astype(F32)).astype(BF16)


def _mm_split(a, b):
    ah, al = a
    bh, _ = b
    n = ah.shape[0]
    t = _dot(jnp.concatenate([ah, al], axis=0), bh)
    return t[:n] + t[n:]


def _silu(x):
    return x * jax.nn.sigmoid(x)


def _lane_mask(width, lo, hi):
    lane = lax.broadcasted_iota(jnp.int32, (1, width), 1)
    return (lane >= lo) & (lane < hi)


CARRY_ROWS = 8


def _conv_silu_norm(xp_ref, raw, cw, rows, part):
    first = CARRY_ROWS - (CONV_W - 1)
    xp_ref[CARRY_ROWS:CARRY_ROWS + rows, :] = raw
    y = xp_ref[first:first + rows, :] * cw[0:1, :]
    for j in range(1, CONV_W):
        y = y + xp_ref[first + j:first + j + rows, :] * cw[j:j + 1, :]
    y = _silu(y)
    conv_new = xp_ref[rows + first:rows + CARRY_ROWS, :]
    xp_ref[0:CARRY_ROWS, :] = xp_ref[rows:rows + CARRY_ROWS, :]
    if part == 2:
        return y, conv_new
    scale = A_DK ** -0.5 if part == 0 else 1.0
    heads = []
    for h in range(A_HEADS):
        yh = y[:, h * A_DK:(h + 1) * A_DK]
        heads.append(yh * (lax.rsqrt(jnp.sum(yh * yh, axis=-1, keepdims=True) + EPS) * scale))
    return jnp.concatenate(heads, axis=1), conv_new


def _conv_part(extra):
    return int(extra[len("conv"):]) if extra.startswith("conv") else None


def _proj_body(*refs, groups, tiles):
    x_ref, g_ref, w_ref = refs[0], refs[1], refs[2]
    n_aux = 3 * sum(1 for _, normed, _ in groups if normed)
    aux = refs[3:3 + n_aux]
    pos = 3 + n_aux
    n_conv = sum(1 for _, _, extra in groups if _conv_part(extra) is not None)
    if n_conv:
        cbuf_ref, cw_ref = refs[pos], refs[pos + 1]
        pos += 2
    outs = refs[pos:pos + len(groups)]
    pos += len(groups)
    n_t = sum(1 for _, _, extra in groups if extra == "t")
    outs_t = list(refs[pos:pos + n_t])
    pos += n_t
    if n_conv:
        convnew_ref = refs[pos]
        xp_refs = refs[pos + 1:pos + 1 + n_conv]

        @pl.when(pl.program_id(0) % tiles == 0)
        def _():
            for part, xp_ref in enumerate(xp_refs):
                xp_ref[0:CARRY_ROWS, :] = cbuf_ref[:, part * A_WIDTH:(part + 1) * A_WIDTH]
    x = x_ref[...]
    r = lax.rsqrt(jnp.mean(x * x, axis=-1, keepdims=True) + EPS)
    hb = (x * g_ref[...]).astype(BF16)
    off = 0
    ai = 0
    for (width, normed, extra), o_ref in zip(groups, outs):
        y = _dot(hb, w_ref[:, off:off + width]) * r
        part = _conv_part(extra)
        if part is not None:
            cols = slice(part * A_WIDTH, (part + 1) * A_WIDTH)
            y, conv_new = _conv_silu_norm(xp_refs[part], y, cw_ref[:, cols], y.shape[0], part)
            convnew_ref[:, cols] = conv_new
        if normed:
            bd_ref, gain_ref, nm_ref = aux[ai], aux[ai + 1], aux[ai + 2]
            ai += 3
            sq = y * y
            hi = sq.astype(BF16)
            lo = (sq - hi.astype(F32)).astype(BF16)
            ms = _dot(hi, bd_ref[...]) + _dot(lo, bd_ref[...])
            scale = jnp.where(nm_ref[...] > 0.0, lax.rsqrt(ms + EPS), 1.0)
            y = y * scale * gain_ref[...]
        o_ref[...] = y
        if extra == "t":
            outs_t.pop(0)[...] = y.T
        off += width


def _group_mean_matrix(width, group, n_lanes):
    i = np.arange(width)
    m = ((i[:, None] // group) == (i[None, :] // group)) & (i[:, None] < n_lanes) & (i[None, :] < n_lanes)
    return jnp.asarray(m.astype(np.float32) / group, dtype=BF16)


def _proj(x, g, w, groups, aux, tm, t_len, conv=None):
    m, d = x.shape
    nw = w.shape[1]
    tiles = t_len // tm
    bsz = m // t_len
    in_specs = [pl.BlockSpec((tm, d), lambda i: (i, 0)),
                pl.BlockSpec((1, d), lambda i: (0, 0)),
                pl.BlockSpec((d, nw), lambda i: (0, 0))]
    for a in aux:
        in_specs.append(pl.BlockSpec(a.shape, lambda i: (0, 0)))
    out_shape = [jax.ShapeDtypeStruct((m, wd), F32) for wd, _, _ in groups]
    out_specs = [pl.BlockSpec((tm, wd), lambda i: (i, 0)) for wd, _, _ in groups]
    for wd, _, extra in groups:
        if extra == "t":
            out_shape.append(jax.ShapeDtypeStruct((bsz, wd, t_len), F32))
            out_specs.append(pl.BlockSpec((None, wd, tm), lambda i: (i // tiles, 0, i % tiles)))
    args = [x, g, w, *aux]
    scratch = []
    if conv is not None:
        cbuf, cw = conv
        wd = cw.shape[1]
        in_specs += [pl.BlockSpec((None, CARRY_ROWS, wd), lambda i: (i // tiles, 0, 0)),
                     pl.BlockSpec(cw.shape, lambda i: (0, 0))]
        args += [cbuf, cw]
        out_shape.append(jax.ShapeDtypeStruct((bsz, CONV_W - 1, wd), F32))
        out_specs.append(pl.BlockSpec((None, CONV_W - 1, wd), lambda i: (i // tiles, 0, 0)))
        scratch += [pltpu.VMEM((tm + CARRY_ROWS, A_WIDTH), F32) for _ in range(wd // A_WIDTH)]
    return pl.pallas_call(
        functools.partial(_proj_body, groups=tuple(groups), tiles=tiles),
        grid=(m // tm,),
        in_specs=in_specs, out_specs=out_specs, out_shape=out_shape, scratch_shapes=scratch,
        compiler_params=_cparams(("arbitrary",) if conv is not None else ("parallel",)),
        name="proj",
    )(*args)


def _gdn_body(*refs, chunk, rows, fused_conv):
    if fused_conv:
        (q_ref, k_ref, v_ref, za_ref, misc_ref, s0_ref, avec_ref, dtvec_ref, go_ref, cbuf_ref, cw_ref,
         o_ref, sfin_ref, convnew_ref, g_sc, b_sc, s_sc, *conv_sc) = refs
    else:
        (q_ref, k_ref, v_ref, za_ref, misc_ref, s0_ref, avec_ref, dtvec_ref, go_ref,
         o_ref, sfin_ref, g_sc, b_sc, s_sc) = refs
    t = pl.program_id(1)
    c = chunk

    @pl.when(t == 0)
    def _():
        s_sc[...] = s0_ref[0]

    if fused_conv:
        xp_refs, act_refs = conv_sc[:3], conv_sc[3:]

        @pl.when(t == 0)
        def _():
            for part, xp_ref in enumerate(xp_refs):
                xp_ref[0:CARRY_ROWS, :] = cbuf_ref[:, part * A_WIDTH:(part + 1) * A_WIDTH]
        for part, raw_ref in enumerate((q_ref, k_ref, v_ref)):
            cols = slice(part * A_WIDTH, (part + 1) * A_WIDTH)
            act, conv_new = _conv_silu_norm(xp_refs[part], raw_ref[...], cw_ref[:, cols], rows, part)
            convnew_ref[:, cols] = conv_new
            act_refs[part][...] = act
        q_sc, k_sc, v_sc = act_refs
    else:
        q_sc, k_sc, v_sc = q_ref, k_ref, v_ref
    misc = misc_ref[...]
    b_sc[...] = jax.nn.sigmoid(misc)
    g_sc[...] = avec_ref[...] * jax.nn.softplus(misc + dtvec_ref[...])

    ri = lax.broadcasted_iota(jnp.int32, (c, c), 0)
    ci = lax.broadcasted_iota(jnp.int32, (c, c), 1)
    tri = ri >= ci
    strict = ri > ci
    eye_f = (ri == ci).astype(F32)
    tri3 = (lax.broadcasted_iota(jnp.int32, (c, 3 * c), 0)
            >= lax.broadcasted_iota(jnp.int32, (c, 3 * c), 1) % c).astype(BF16)
    go = go_ref[...]
    n_chunks = rows // c

    problems = [(ic, h) for ic in range(n_chunks) for h in range(A_HEADS)]
    gc_all, gc_t = [], []
    for ic in range(n_chunks):
        g = g_sc[ic * c:(ic + 1) * c, :]
        g1 = g.astype(BF16)
        r1 = g - g1.astype(F32)
        g2 = r1.astype(BF16)
        g3 = (r1 - g2.astype(F32)).astype(BF16)
        gc_all.append(_dot(tri3, jnp.concatenate([g1, g2, g3], axis=0)))
    for ic in range(n_chunks):
        gc_t.append(gc_all[ic].T)

    def load(ref, ic, h):
        return ref[ic * c:(ic + 1) * c, h * A_DK:(h + 1) * A_DK]

    col = lambda a, lane: a[:, lane:lane + 1]
    k_bf = [load(k_sc, ic, h).astype(BF16) for ic, h in problems]
    kb = [load(k_sc, ic, h) * col(b_sc[ic * c:(ic + 1) * c, :], MISC_B + h) for ic, h in problems]
    kk = [_dot_nt(kb[i].astype(BF16), k_bf[i]) for i in range(len(problems))]
    qk_raw = [_dot_nt(load(q_sc, ic, h).astype(BF16), k_bf[i]) for i, (ic, h) in enumerate(problems)]
    decay = []
    for ic, h in problems:
        diff = col(gc_all[ic], MISC_A + h) - gc_t[ic][MISC_A + h:MISC_A + h + 1, :]
        decay.append(jnp.where(tri, jnp.exp(jnp.where(tri, diff, 0.0)), 0.0))
    lmat = [jnp.where(strict, kk[i] * decay[i], 0.0) for i in range(len(problems))]
    qk = [(qk_raw[i] * decay[i]).astype(BF16) for i in range(len(problems))]
    x_inv = [eye_f - m for m in lmat]
    p_split = [_split2(m) for m in lmat]
    for _ in range(int(math.log2(c)) - 1):
        p_split = [_split2(_mm_split(ps, ps)) for ps in p_split]
        x_inv = [x + _mm_split(_split2(x), ps) for x, ps in zip(x_inv, p_split)]
    sol = []
    for i, (ic, h) in enumerate(problems):
        egc = jnp.exp(col(gc_all[ic], MISC_A + h))
        beta = col(b_sc[ic * c:(ic + 1) * c, :], MISC_B + h)
        rhs = jnp.concatenate([load(v_sc, ic, h) * beta, kb[i] * egc], axis=-1).astype(BF16)
        sol.append(_mm_split(_split2(x_inv[i]), (rhs, None)))
    pre = []
    for i, (ic, h) in enumerate(problems):
        gc = col(gc_all[ic], MISC_A + h)
        g_last = gc[c - 1:c, :]
        k_dec_t = (load(k_sc, ic, h) * jnp.exp(g_last - gc)).T.astype(BF16)
        wq = jnp.concatenate([sol[i][:, A_DV:], load(q_sc, ic, h) * jnp.exp(gc)], axis=0).astype(BF16)
        pre.append((sol[i][:, :A_DV], wq, k_dec_t, jnp.exp(g_last)))

    heads = range(A_HEADS)
    s_cur = [s_sc[h] for h in heads]
    for ic in range(n_chunks):
        pr = [pre[ic * A_HEADS + h] for h in heads]
        ws = [_dot(pr[h][1], s_cur[h].astype(BF16)) for h in heads]
        ub = [(pr[h][0] - ws[h][:c]).astype(BF16) for h in heads]
        o = [ws[h][c:] + _dot(qk[ic * A_HEADS + h], ub[h]) for h in heads]
        s_cur = [s_cur[h] * pr[h][3] + _dot(pr[h][2], ub[h]) for h in heads]
        for h in heads:
            on = o[h] * lax.rsqrt(jnp.mean(o[h] * o[h], axis=-1, keepdims=True) + EPS) * go
            o_ref[ic * c:(ic + 1) * c, h * A_DV:(h + 1) * A_DV] = on * _silu(load(za_ref, ic, h))
    for h in heads:
        s_sc[h] = s_cur[h]
    sfin_ref[0] = s_sc[...]


def _gdn(q, k, v, za, misc, s0, avec, dtvec, g_o, *, bsz, t_len, chunk, rows, conv=None):
    nt = t_len // rows
    w3 = 3 * A_WIDTH
    row_map = lambda b, t: (b * nt + t, 0)
    const2 = lambda b, t: (0, 0)
    state_spec = pl.BlockSpec((1, A_HEADS, A_DK, A_DV), lambda b, t: (b, 0, 0, 0))
    in_specs = [pl.BlockSpec((rows, A_WIDTH), row_map)] * 4 + [
                pl.BlockSpec((rows, MISC_W), row_map),
                state_spec,
                pl.BlockSpec((1, MISC_W), const2),
                pl.BlockSpec((1, MISC_W), const2),
                pl.BlockSpec((1, A_DV), const2)]
    args = [q, k, v, za, misc, s0, avec, dtvec, g_o]
    out_specs = [pl.BlockSpec((rows, A_WIDTH), row_map), state_spec]
    out_shape = [jax.ShapeDtypeStruct((bsz * t_len, A_WIDTH), F32),
                 jax.ShapeDtypeStruct((bsz, A_HEADS, A_DK, A_DV), F32)]
    scratch = [pltpu.VMEM((rows, MISC_W), F32),
               pltpu.VMEM((rows, MISC_W), F32),
               pltpu.VMEM((A_HEADS, A_DK, A_DV), F32)]
    if conv is not None:
        cbuf, cw = conv
        in_specs += [pl.BlockSpec((None, CARRY_ROWS, w3), lambda b, t: (b, 0, 0)),
                     pl.BlockSpec((CONV_W, w3), const2)]
        args += [cbuf, cw]
        out_specs.append(pl.BlockSpec((None, CONV_W - 1, w3), lambda b, t: (b, 0, 0)))
        out_shape.append(jax.ShapeDtypeStruct((bsz, CONV_W - 1, w3), F32))
        scratch += [pltpu.VMEM((rows + CARRY_ROWS, A_WIDTH), F32)] * 3 + [pltpu.VMEM((rows, A_WIDTH), F32)] * 3
    return pl.pallas_call(
        functools.partial(_gdn_body, chunk=chunk, rows=rows, fused_conv=conv is not None),
        grid=(bsz, nt),
        in_specs=in_specs, out_specs=out_specs, out_shape=out_shape, scratch_shapes=scratch,
        compiler_params=_cparams(("parallel", "arbitrary")),
        name="gdn",
    )(*args)


def _topk_threshold(sc_ref, nkb, shape, key_axis, n_sel, n_adm):
    kf = float(n_sel)
    kblk = shape[key_axis]

    if key_axis == 0:
        strip = min(kblk, 32)
        part_shape = (strip, shape[1])
        strip_of = lambda kb, i: sc_ref[kb, i * strip:(i + 1) * strip, :]

        def put_strip(kb, i, v):
            sc_ref[kb, i * strip:(i + 1) * strip, :] = v
    else:
        strip = kblk
        part_shape = shape
        strip_of = lambda kb, i: sc_ref[kb]

        def put_strip(kb, i, v):
            sc_ref[kb] = v
    kpos = lax.broadcasted_iota(jnp.int32, part_shape, key_axis)

    def reduce_blocks(specs):
        def body(kb, accs):
            accs = list(accs)
            for i in range(kblk // strip):
                s = strip_of(kb, i)
                k0 = kb * kblk + i * strip
                accs = [cmb(acc, fn(s, k0)) for acc, (fn, _, _, _, cmb) in zip(accs, specs)]
            return tuple(accs)
        init = tuple(jnp.full(part_shape, i, dt) for _, i, dt, _, _ in specs)
        accs = lax.fori_loop(0, nkb, body, init)
        return [acc if op is None else op(acc, axis=key_axis, keepdims=True)
                for acc, (_, _, _, op, _) in zip(accs, specs)]

    one_if = lambda c: jnp.where(c, 1.0, 0.0)
    vmax, vmin, cpos, czero = reduce_blocks([
        (lambda s, k0: s, -jnp.inf, F32, jnp.max, jnp.maximum),
        (lambda s, k0: jnp.where(s == -jnp.inf, jnp.inf, s), jnp.inf, F32, jnp.min, jnp.minimum),
        (lambda s, k0: one_if(s > 0.0), 0.0, F32, jnp.sum, jnp.add),
        (lambda s, k0: one_if(s == 0.0), 0.0, F32, jnp.sum, jnp.add)])

    long_row = n_adm > kf
    pos_row = long_row & (cpos >= kf)
    zero_row = long_row & (cpos < kf) & (cpos + czero >= kf)
    neg_row = long_row & (cpos + czero < kf)
    lo0 = jnp.where(pos_row | zero_row, 0.0, vmin - jnp.maximum(1.0, jnp.abs(vmin)))
    cnt_lo0 = jnp.where(pos_row, cpos, jnp.where(zero_row, kf, n_adm))
    hi0 = jnp.where(neg_row, 0.0, vmax)
    cnt_hi0 = jnp.where(neg_row, cpos, 0.0)
    need_zero = jnp.where(zero_row, kf - cpos, 0.0)

    def narrow(lo, hi, cnt_lo, cnt_hi):
        fracs = [(t + 1.0) / (BISECT_PROBES + 1.0) for t in range(BISECT_PROBES)]
        mids = [lo * (1.0 - f) + hi * f for f in fracs]
        for t in range(1, BISECT_PROBES):
            mids[t] = jnp.maximum(mids[t], mids[t - 1])
        codes = [sum(COUNT_RADIX ** u for u in range(t + 1)) for t in range(BISECT_PROBES)]

        def encode(s, k0):
            e = jnp.zeros(s.shape, jnp.int32)
            for mid, code in zip(mids, codes):
                e = jnp.where(s > mid, code, e)
            return e
        (packed,) = reduce_blocks([(encode, 0, jnp.int32, None, jnp.add)])
        cnts = []
        for t in range(BISECT_PROBES):
            digit = packed & (COUNT_RADIX - 1) if t < BISECT_PROBES - 1 else packed
            packed = packed >> COUNT_RADIX.bit_length() - 1
            cnts.append(jnp.sum(digit.astype(F32), axis=key_axis, keepdims=True))
        lo_n, cnt_lo_n, hi_n, cnt_hi_n = lo, cnt_lo, hi, cnt_hi
        for mid, cm in zip(mids, cnts):
            up = cm >= kf
            lo_n = jnp.where(up, mid, lo_n)
            cnt_lo_n = jnp.where(up, cm, cnt_lo_n)
        for mid, cm in zip(mids[::-1], cnts[::-1]):
            dn = cm < kf
            hi_n = jnp.where(dn, mid, hi_n)
            cnt_hi_n = jnp.where(dn, cm, cnt_hi_n)
        return lo_n, hi_n, cnt_lo_n, cnt_hi_n

    def unresolved(cnt_lo):
        return (jnp.max(cnt_lo) > kf).astype(jnp.int32)

    blind = jnp.where(unresolved(cnt_lo0) > 0, BISECT_BLIND_ITERS, 0)
    state = lax.fori_loop(0, blind, lambda i, st: narrow(*st), (lo0, hi0, cnt_lo0, cnt_hi0))

    def cond(st):
        return (st[0] < BISECT_MAX_ITERS) & (st[1] > 0)

    def body(st):
        new = narrow(*st[2:])
        return (st[0] + 1, unresolved(new[2])) + new

    _, _, lo, hi, cnt_lo, cnt_hi = lax.while_loop(
        cond, body, (blind, unresolved(state[2])) + state)

    fix = cnt_lo > kf
    need0 = jnp.where(fix, kf - cnt_hi, 0.0)
    big_idx = 2 ** 30

    def fix_cond(need):
        return jnp.max(need) > 0.0

    def fix_body(need):
        def in_cluster(s):
            return (s > lo) & (s <= hi)
        (mval,) = reduce_blocks([(lambda s, k0: jnp.where(in_cluster(s), s, -jnp.inf), -jnp.inf, F32,
                                  jnp.max, jnp.maximum)])
        (idx,) = reduce_blocks(
            [(lambda s, k0: jnp.where(in_cluster(s) & (s == mval), kpos + k0, big_idx),
              big_idx, jnp.int32, jnp.min, jnp.minimum)])
        active = need > 0.0

        def promote(kb, carry):
            for i in range(kblk // strip):
                s = strip_of(kb, i)
                put_strip(kb, i, jnp.where(active & ((kpos + kb * kblk + i * strip) == idx), jnp.inf, s))
            return carry
        lax.fori_loop(0, nkb, promote, 0)
        return jnp.where(active, need - 1.0, need)

    lax.while_loop(fix_cond, fix_body, need0)

    @pl.when(jnp.max(need_zero) > 0.0)
    def _():
        r = lax.broadcasted_iota(jnp.int32, (kblk, kblk), 0)
        c = lax.broadcasted_iota(jnp.int32, (kblk, kblk), 1)
        tri = ((r >= c) if key_axis == 0 else (r <= c)).astype(BF16)

        def tie_blk(kb, seen):
            s = sc_ref[kb]
            z = (s == 0.0) & zero_row
            zb = one_if(z).astype(BF16)
            if key_axis == 0:
                rank = _dot(tri, zb) + seen
                last = rank[kblk - 1:kblk, :]
            else:
                rank = _dot(zb, tri) + seen
                last = rank[:, kblk - 1:kblk]
            sc_ref[kb] = jnp.where(z & (rank <= need_zero), jnp.inf, s)
            return last
        lax.fori_loop(0, nkb, tie_blk, jnp.zeros_like(need_zero))

    return jnp.where(fix, hi, lo)


def _dsa_body(qb_ref, qi_ref, miscq_ref, zb_ref, k_ref, vt_ref, misck_ref, o_ref, sc_ref, s_ref, *, n_sel, qblk):
    j = pl.program_id(1)
    nkb = ((j + 1) * qblk + KEY_BLOCK - 1) // KEY_BLOCK
    qi = qi_ref[...]
    q_h = [qi[:, h * IDX_HD:(h + 1) * IDX_HD].astype(BF16) for h in range(IDX_HEADS)]
    w_t = miscq_ref[...].T[MISC_WI:MISC_WI + IDX_HEADS, :]
    qchunk = (j * qblk + lax.broadcasted_iota(jnp.int32, (1, qblk), 1)) // CHUNK
    krow = lax.broadcasted_iota(jnp.int32, (KEY_BLOCK, 1), 0)

    def score_blk(kb, carry):
        k0 = pl.multiple_of(kb * KEY_BLOCK, KEY_BLOCK)
        for half in range(KEY_BLOCK // SCORE_STRIP):
            r0 = half * SCORE_STRIP
            ki = misck_ref[pl.ds(k0 + r0, SCORE_STRIP), 0:IDX_HD].astype(BF16)
            acc = jnp.zeros((SCORE_STRIP, qblk), F32)
            for h in range(IDX_HEADS):
                acc = acc + w_t[h:h + 1, :] * jnp.maximum(_dot_nt(ki, q_h[h]), 0.0)
            adm = ((k0 + r0 + krow[:SCORE_STRIP]) // CHUNK) <= qchunk
            sc_ref[kb, r0:r0 + SCORE_STRIP, :] = jnp.where(adm, acc, -jnp.inf)
        return carry

    npairs = (nkb + 1) // 2

    def pair_loop(body, init):
        return lax.fori_loop(0, npairs, lambda i, c: body(2 * i + 1, body(2 * i, c)), init)

    pair_loop(score_blk, 0)

    n_adm = ((qchunk + 1) * CHUNK).astype(F32)
    thr = _topk_threshold(sc_ref, nkb, (KEY_BLOCK, qblk), 0, n_sel, n_adm)

    qb = qb_ref[...] * (B_HD ** -0.5 * math.log2(math.e))
    qs = jnp.concatenate([jnp.where(_lane_mask(B_WIDTH, h * B_HD, (h + 1) * B_HD), qb, 0.0)
                          for h in range(B_HEADS)], axis=0).astype(BF16)
    wide = B_HEADS * qblk

    def fold(acc, x, combine):
        for i in range(x.shape[0] // 8):
            acc = combine(acc, x[8 * i:8 * (i + 1)])
        return acc

    def logits_blk(kb, m8):
        k0 = pl.multiple_of(kb * KEY_BLOCK, KEY_BLOCK)
        s = _dot_nt(k_ref[pl.ds(k0, KEY_BLOCK), :].astype(BF16), qs)
        sel = sc_ref[kb] > thr
        s = jnp.where(jnp.concatenate([sel] * B_HEADS, axis=1), s, NEG_BIG)
        s_ref[kb] = s
        return fold(m8, s, jnp.maximum)

    m8 = pair_loop(logits_blk, jnp.full((8, wide), NEG_BIG, F32))
    m = jnp.max(m8, axis=0, keepdims=True)

    def pv_blk(kb, carry):
        l8, accs = carry
        p = jnp.exp2(s_ref[kb] - m)
        pb = p.astype(BF16)
        k0 = pl.multiple_of(kb * KEY_BLOCK, KEY_BLOCK)
        vt = vt_ref[:, pl.ds(k0, KEY_BLOCK)].astype(BF16)
        accs = tuple(accs[h] + _dot(vt[h * B_HD:(h + 1) * B_HD, :], pb[:, h * qblk:(h + 1) * qblk])
                     for h in range(B_HEADS))
        return fold(l8, p, jnp.add), accs

    l8, accs = pair_loop(
        pv_blk, (jnp.zeros((8, wide), F32), tuple(jnp.zeros((B_HD, qblk), F32) for _ in range(B_HEADS))))
    l = jnp.sum(l8, axis=0, keepdims=True)
    o_t = jnp.concatenate([accs[h] / l[:, h * qblk:(h + 1) * qblk] for h in range(B_HEADS)], axis=0)
    o_ref[...] = o_t.T * _silu(zb_ref[...])


def _dsa_prompt(qb, qi, misc, zb, kb, vt, *, bsz, t_len, n_sel, qblk):
    nq = t_len // qblk
    nkb = t_len // KEY_BLOCK
    assert nkb * (KEY_BLOCK // 8) < COUNT_RADIX and nkb % 2 == 0
    qmap = lambda b, j: (b * nq + j, 0)
    kmap = lambda b, j: (b, 0)
    return pl.pallas_call(
        functools.partial(_dsa_body, n_sel=n_sel, qblk=qblk),
        grid=(bsz, nq),
        in_specs=[pl.BlockSpec((qblk, B_WIDTH), qmap),
                  pl.BlockSpec((qblk, IDX_HEADS * IDX_HD), qmap),
                  pl.BlockSpec((qblk, MISC_W), qmap),
                  pl.BlockSpec((qblk, B_WIDTH), qmap),
                  pl.BlockSpec((t_len, B_WIDTH), kmap),
                  pl.BlockSpec((None, B_WIDTH, t_len), lambda b, j: (b, 0, 0)),
                  pl.BlockSpec((t_len, MISC_W), kmap)],
        out_specs=pl.BlockSpec((qblk, B_WIDTH), qmap),
        out_shape=jax.ShapeDtypeStruct((bsz * t_len, B_WIDTH), F32),
        scratch_shapes=[pltpu.VMEM((nkb, KEY_BLOCK, qblk), F32),
                        pltpu.VMEM((nkb, KEY_BLOCK, B_HEADS * qblk), F32)],
        compiler_params=_cparams(("parallel", "arbitrary")),
        name="dsa_prompt",
    )(qb, qi, misc, zb, kb, vt, misc)


def _dsa_dec_body(qb_ref, qi_ref, misc_ref, zb_ref, kn_ref, vn_ref, ck_ref, cv_ref, cki_ref, o_ref,
                  sc_ref, thr_ref, m_ref, l_ref, acc_ref, *, p_len, tq, n_sel, kstep):
    step = pl.program_id(1)
    nsteps = p_len // kstep
    sub = kstep // KEY_BLOCK
    npast = p_len // KEY_BLOCK
    nkb = npast + 1
    pad_rows = KEY_BLOCK - tq
    rows = B_HEADS * tq

    @pl.when(step == 0)
    def _():
        qi = qi_ref[...]
        misc = misc_ref[...]
        q2 = jnp.concatenate([qi[:, h * IDX_HD:(h + 1) * IDX_HD] for h in range(IDX_HEADS)],
                             axis=0).astype(BF16)
        w_h = [misc[:, MISC_WI + h:MISC_WI + h + 1] for h in range(IDX_HEADS)]

        def head_sum(lg):
            acc = jnp.zeros((tq, lg.shape[1]), F32)
            for h in range(IDX_HEADS):
                acc = acc + w_h[h] * jnp.maximum(lg[h * tq:(h + 1) * tq, :], 0.0)
            return acc

        for i in range(nsteps):
            sc = head_sum(_dot(q2, cki_ref[:, i * kstep:(i + 1) * kstep].astype(BF16)))
            for u in range(sub):
                sc_ref[i * sub + u] = sc[:, u * KEY_BLOCK:(u + 1) * KEY_BLOCK]
        lane = lax.broadcasted_iota(jnp.int32, (1, KEY_BLOCK), 1)
        ki_new = jnp.concatenate([misc[:, 0:IDX_HD], jnp.zeros((pad_rows, IDX_HD), F32)], axis=0)
        sc_ref[npast] = jnp.where(lane < tq, head_sum(_dot_nt(q2, ki_new.astype(BF16))), -jnp.inf)
        n_adm = jnp.full((tq, 1), float(p_len + tq), F32)
        thr_ref[...] = _topk_threshold(sc_ref, nkb, (tq, KEY_BLOCK), 1, n_sel, n_adm)
        m_ref[...] = jnp.full((rows, 1), NEG_BIG, F32)
        l_ref[...] = jnp.zeros((rows, 1), F32)
        acc_ref[...] = jnp.zeros((rows, B_HD), F32)

    qb = qb_ref[...] * (B_HD ** -0.5 * math.log2(math.e))
    q_h = [qb[:, h * B_HD:(h + 1) * B_HD].astype(BF16) for h in range(B_HEADS)]
    thr = thr_ref[...]

    def fold_in(logits_of, pv_of, sel):
        s = jnp.concatenate([jnp.where(sel, logits_of(h), NEG_BIG) for h in range(B_HEADS)], axis=0)
        m_old = m_ref[...]
        m_new = jnp.maximum(m_old, jnp.max(s, axis=-1, keepdims=True))
        alpha = jnp.exp2(m_old - m_new)
        p = jnp.exp2(s - m_new)
        pb = p.astype(BF16)
        pv = jnp.concatenate([pv_of(h, pb[h * tq:(h + 1) * tq, :]) for h in range(B_HEADS)], axis=0)
        l_ref[...] = alpha * l_ref[...] + jnp.sum(p, axis=-1, keepdims=True)
        acc_ref[...] = alpha * acc_ref[...] + pv
        m_ref[...] = m_new

    sel = jnp.concatenate([sc_ref[step * sub + u] > thr for u in range(sub)], axis=1)
    fold_in(lambda h: _dot(q_h[h], ck_ref[h].astype(BF16)),
            lambda h, p: _dot_nt(p, cv_ref[h].astype(BF16)), sel)

    @pl.when(step == nsteps - 1)
    def _():
        zpad = jnp.zeros((pad_rows, B_WIDTH), F32)
        kn = jnp.concatenate([kn_ref[...], zpad], axis=0).astype(BF16)
        vn = jnp.concatenate([vn_ref[...], zpad], axis=0).astype(BF16)
        fold_in(lambda h: _dot_nt(q_h[h], kn[:, h * B_HD:(h + 1) * B_HD]),
                lambda h, p: _dot(p, vn[:, h * B_HD:(h + 1) * B_HD]), sc_ref[npast] > thr)
        res = acc_ref[...] / l_ref[...]
        out = jnp.concatenate([res[h * tq:(h + 1) * tq, :] for h in range(B_HEADS)], axis=1)
        o_ref[...] = out * _silu(zb_ref[...])


def _dsa_decode(qb, qi, misc, zb, kb, vb, ck, cv, cki, *, bsz, t_len, p_len, n_sel):
    kstep = DEC_KEYS_PER_STEP
    assert p_len // KEY_BLOCK + 1 < COUNT_RADIX and p_len % kstep == 0
    rows = B_HEADS * t_len
    qmap = lambda b, s: (b, 0)
    cmap = lambda b, s: (b, 0, 0, s)
    ck, cv = (a.transpose(0, 2, 3, 1) for a in (ck, cv))
    cki = cki.transpose(0, 2, 1)
    return pl.pallas_call(
        functools.partial(_dsa_dec_body, p_len=p_len, tq=t_len, n_sel=n_sel, kstep=kstep),
        grid=(bsz, p_len // kstep),
        in_specs=[pl.BlockSpec((t_len, B_WIDTH), qmap),
                  pl.BlockSpec((t_len, IDX_HEADS * IDX_HD), qmap),
                  pl.BlockSpec((t_len, MISC_W), qmap),
                  pl.BlockSpec((t_len, B_WIDTH), qmap),
                  pl.BlockSpec((t_len, B_WIDTH), qmap),
                  pl.BlockSpec((t_len, B_WIDTH), qmap),
                  pl.BlockSpec((None, B_HEADS, B_HD, kstep), cmap),
                  pl.BlockSpec((None, B_HEADS, B_HD, kstep), cmap),
                  pl.BlockSpec((None, IDX_HD, p_len), lambda b, s: (b, 0, 0))],
        out_specs=pl.BlockSpec((t_len, B_WIDTH), qmap),
        out_shape=jax.ShapeDtypeStruct((bsz * t_len, B_WIDTH), F32),
        scratch_shapes=[pltpu.VMEM((p_len // KEY_BLOCK + 1, t_len, KEY_BLOCK), F32),
                        pltpu.VMEM((t_len, 1), F32),
                        pltpu.VMEM((rows, 1), F32),
                        pltpu.VMEM((rows, 1), F32),
                        pltpu.VMEM((rows, B_HD), F32)],
        compiler_params=_cparams(("parallel", "arbitrary")),
        name="dsa_decode",
    )(qb, qi, misc, zb, kb, vb, ck, cv, cki)


def _tail_body(x_ref, oa_ref, ob_ref, qm_ref, zm_ref, mk_ref, mv_ref, w_ref, y_ref):
    qm = qm_ref[...] * (M_HD ** -0.5)
    mk = mk_ref[...].reshape(N_MEM, M_WIDTH).astype(BF16)
    mv = mv_ref[...].reshape(N_MEM, M_WIDTH).astype(BF16)
    acc = _dot(oa_ref[...].astype(BF16), w_ref[0:A_WIDTH, :])
    acc = acc + _dot(ob_ref[...].astype(BF16), w_ref[A_WIDTH:A_WIDTH + B_WIDTH, :])
    om = jnp.zeros(qm.shape, F32)
    for h in range(M_HEADS):
        mask = _lane_mask(M_WIDTH, h * M_HD, (h + 1) * M_HD)
        s = _dot_nt(jnp.where(mask, qm, 0.0).astype(BF16), mk)
        p = jnp.exp(s - jnp.max(s, axis=-1, keepdims=True))
        l = jnp.sum(p, axis=-1, keepdims=True)
        om = om + jnp.where(mask, _dot(p.astype(BF16), mv) / l, 0.0)
    om = om * _silu(zm_ref[...])
    acc = acc + _dot(om.astype(BF16), w_ref[A_WIDTH + B_WIDTH:, :])
    y_ref[...] = x_ref[...] + acc


def _tail(x, oa, ob, qm, zm, mk, mv, w, *, bsz, t_len, tq):
    m, d = x.shape
    nq = t_len // tq
    qmap = lambda b, j: (b * nq + j, 0)
    mem_spec = pl.BlockSpec((None,) + mk.shape[1:], lambda b, j: (b,) + (0,) * (mk.ndim - 1))
    return pl.pallas_call(
        _tail_body,
        grid=(bsz, nq),
        in_specs=[pl.BlockSpec((tq, d), qmap),
                  pl.BlockSpec((tq, A_WIDTH), qmap),
                  pl.BlockSpec((tq, B_WIDTH), qmap),
                  pl.BlockSpec((tq, M_WIDTH), qmap),
                  pl.BlockSpec((tq, M_WIDTH), qmap),
                  mem_spec, mem_spec,
                  pl.BlockSpec(w.shape, lambda b, j: (0, 0))],
        out_specs=pl.BlockSpec((tq, d), qmap),
        out_shape=jax.ShapeDtypeStruct((m, d), F32),
        compiler_params=_cparams(("parallel", "parallel")),
        name="tail",
    )(x, oa, ob, qm, zm, mk, mv, w)


def _prep_weights(g_in, w_in, conv_w, a_log, dt_bias, g_o, g_qb, g_kb, g_ki, g_qm, w_out):
    offs = np.concatenate([[0], np.cumsum(IN_SPLITS)])
    col = lambda i: w_in[:, int(offs[i]):int(offs[i + 1])]
    (w_qkv, w_za, w_ba, w_aa, w_qb, w_kb, w_vb, w_zb, w_qi, w_ki, w_wi, w_qm, w_zm) = [col(i) for i in range(13)]
    d = w_in.shape[0]
    w_misc = jnp.concatenate(
        [w_ki, w_wi, w_ba, w_aa, jnp.zeros((d, MISC_W - MISC_A - A_HEADS), w_in.dtype)], axis=1)
    w_all = jnp.concatenate([w_za, w_qb, w_kb, w_vb, w_zb, w_qi, w_qm, w_zm, w_misc, w_qkv],
                            axis=1).astype(BF16)
    groups = [(A_WIDTH, False, ""), (B_WIDTH, True, ""),
              (B_WIDTH, True, "t"), (B_WIDTH, False, "t"), (B_WIDTH, False, ""),
              (IDX_HEADS * IDX_HD, False, ""), (M_WIDTH, True, ""), (M_WIDTH, False, ""),
              (MISC_W, True, ""),
              (A_WIDTH, False, "conv0"), (A_WIDTH, False, "conv1"), (A_WIDTH, False, "conv2")]
    bd64 = _group_mean_matrix(B_WIDTH, B_HD, B_WIDTH)
    ones256 = jnp.ones((1, B_WIDTH), F32)
    misc_gain = jnp.concatenate([g_ki, jnp.full((IDX_HEADS,), IDX_SCALE, F32),
                                 jnp.ones((MISC_W - MISC_B,), F32)])[None, :]
    misc_nm = (jnp.arange(MISC_W) < IDX_HD).astype(F32)[None, :]
    aux = [bd64, jnp.tile(g_qb, B_HEADS)[None, :], ones256,
           bd64, jnp.tile(g_kb, B_HEADS)[None, :], ones256,
           bd64, jnp.tile(g_qm, M_HEADS)[None, :], ones256,
           _group_mean_matrix(MISC_W, IDX_HD, IDX_HD), misc_gain, misc_nm]
    pad_a = lambda v: jnp.zeros((1, MISC_W), F32).at[0, MISC_A:MISC_A + A_HEADS].set(v)
    return dict(g_in=g_in[None, :], w_all=w_all, groups=groups, aux=aux, conv_w=conv_w,
                avec=pad_a(-jnp.exp(a_log)), dtvec=pad_a(dt_bias), g_o=g_o[None, :],
                w_out=w_out.astype(BF16))


def _layer(x, mem_k, mem_v, conv_buf, s0, past, gdn_chunk, gdn_rows, wp):
    bsz, t_len, d = x.shape
    m = bsz * t_len
    x2 = x.reshape(m, d)
    tm = min(PROJ_ROWS, m)
    heads_last = lambda a_t: a_t.reshape(bsz, B_HEADS, B_HD, t_len).transpose(0, 3, 1, 2)
    cbuf = jnp.concatenate([jnp.zeros((bsz, CARRY_ROWS - (CONV_W - 1), 3 * A_WIDTH), F32), conv_buf], axis=1)
    conv = (cbuf, wp["conv_w"])
    gdn_args = dict(bsz=bsz, t_len=t_len, chunk=gdn_chunk, rows=gdn_rows)
    if past is None:
        (za, qb, kb, vb, zb, qi, qm, zm, misc, qa, ka, va, kt, vt, conv_new) = _proj(
            x2, wp["g_in"], wp["w_all"], wp["groups"], wp["aux"], tm, t_len, conv=conv)
        k_new, v_new = heads_last(kt), heads_last(vt)
        oa, s_new = _gdn(qa, ka, va, za, misc, s0, wp["avec"], wp["dtvec"], wp["g_o"], **gdn_args)
    else:
        groups = [(wd, normed, "") for wd, normed, _ in wp["groups"]]
        (za, qb, kb, vb, zb, qi, qm, zm, misc, qa, ka, va) = _proj(x2, wp["g_in"], wp["w_all"], groups,
                                                                   wp["aux"], tm, m)
        k_new, v_new = (a.reshape(bsz, t_len, B_HEADS, B_HD) for a in (kb, vb))
        oa, s_new, conv_new = _gdn(qa, ka, va, za, misc, s0, wp["avec"], wp["dtvec"], wp["g_o"], conv=conv,
                                   **gdn_args)
    if past is None:
        n_sel = min(TOPK_MAX, t_len // 4)
        ob = _dsa_prompt(qb, qi, misc, zb, kb, vt, bsz=bsz, t_len=t_len, n_sel=n_sel, qblk=DSA_QBLK)
    else:
        ck, cv, cki = past
        p_len = ck.shape[1]
        assert (p_len + t_len - 1) // CHUNK <= p_len // CHUNK and p_len % KEY_BLOCK == 0
        n_sel = min(TOPK_MAX, (p_len + t_len) // 4)
        ob = _dsa_decode(qb, qi, misc, zb, kb, vb, ck, cv, cki, bsz=bsz, t_len=t_len, p_len=p_len,
                         n_sel=n_sel)
    y = _tail(x2, oa, ob, qm, zm, mem_k, mem_v, wp["w_out"], bsz=bsz, t_len=t_len, tq=min(TAIL_ROWS, t_len))
    return (y.reshape(bsz, t_len, d), conv_new, s_new, k_new, v_new,
            misc[:, :IDX_HD].reshape(bsz, t_len, IDX_HD))


def _memory_kv(mem, g_mem, w_mem_kv, g_km):
    bsz, n_mem, d = mem.shape
    groups = [(M_WIDTH, True, "t"), (M_WIDTH, False, "t")]
    aux = [_group_mean_matrix(M_WIDTH, M_HD, M_WIDTH), jnp.tile(g_km, M_HEADS)[None, :],
           jnp.ones((1, M_WIDTH), F32)]
    mk, mv, mk_t, mv_t = _proj(mem.reshape(bsz * n_mem, d), g_mem[None, :], w_mem_kv.astype(BF16),
                               groups, aux, n_mem, n_mem)
    heads_last = lambda a_t: a_t.reshape(bsz, M_HEADS, M_HD, n_mem).transpose(0, 3, 1, 2)
    return (mk.reshape(bsz, n_mem, M_WIDTH), mv.reshape(bsz, n_mem, M_WIDTH), heads_last(mk_t), heads_last(mv_t))


def kernel(x_prompt, x_sample, state_conv_A, state_ssm_A, cache_k_B, cache_v_B, cache_kidx_B, cache_mem_k,
           cache_mem_v, mem_prompt, g_in, w_in, conv_w_A, a_log_A, dt_bias_A, g_o_A, g_q_B, g_k_B, g_kidx_B,
           g_mem, w_mem_kv, g_q_M, g_k_M, w_out):
    depth = w_in.shape[0]
    assert depth == 1
    l = 0
    bp, t_p, _ = x_prompt.shape
    bs, t_s, _ = x_sample.shape
    wp = _prep_weights(g_in[l], w_in[l], conv_w_A[l], a_log_A[l], dt_bias_A[l], g_o_A[l], g_q_B[l], g_k_B[l],
                       g_kidx_B[l], g_q_M[l], w_out[l])
    mk, mv, mk_heads, mv_heads = _memory_kv(mem_prompt, g_mem[l], w_mem_kv[l], g_k_M[l])
    zero_conv = jnp.zeros((bp, CONV_W - 1, 3 * A_WIDTH), F32)
    zero_ssm = jnp.zeros((bp, A_HEADS, A_DK, A_DV), F32)
    yp, c1, s1, k1, v1, ki1 = _layer(x_prompt, mk, mv, zero_conv, zero_ssm, None, CHUNK, 4 * CHUNK, wp)
    ys, c2, s2, k2, v2, ki2 = _layer(
        x_sample, cache_mem_k[l], cache_mem_v[l],
        state_conv_A[l], state_ssm_A[l], (cache_k_B[l], cache_v_B[l], cache_kidx_B[l]), t_s, t_s, wp)
    st = lambda a: a[None]
    return (yp, ys, st(c1), st(s1), st(k1), st(v1), st(ki1),
            st(mk_heads), st(mv_heads),
            st(c2), st(s2), st(k2), st(v2), st(ki2))
```

```python
import functools
import math

import numpy as np
import jax
import jax.numpy as jnp
from jax import lax
from jax.experimental import pallas as pl
from jax.experimental.pallas import tpu as pltpu

F32 = jnp.float32
BF16 = jnp.bfloat16
HIGHEST = lax.Precision.HIGHEST

D_MODEL = 1024
CHUNK = 64
A_HEADS = 4
A_DK = 128
A_DV = 128
A_WIDTH = A_HEADS * A_DV
CONV_W = 4
B_HEADS = 4
B_HD = 64
B_WIDTH = B_HEADS * B_HD
IDX_HEADS = 8
IDX_HD = 32
IDX_SCALE = (IDX_HEADS ** -0.5) * (IDX_HD ** -0.5)
TOPK_MAX = 256
Q_BLOCK = 128
N_MEM = 256
M_HEADS = 4
M_HD = 64
M_WIDTH = M_HEADS * M_HD
EPS = 1e-6
IN_SPLITS = (3 * A_WIDTH, A_WIDTH, A_HEADS, A_HEADS,
             B_WIDTH, B_WIDTH, B_WIDTH, B_WIDTH, IDX_HEADS * IDX_HD, IDX_HD, IDX_HEADS,
             M_WIDTH, M_WIDTH)

LANES = 128
KEY_BLOCK = 256
DSA_QBLK = 512
TAIL_ROWS = 512
PROJ_ROWS = 512
DEC_KEYS_PER_STEP = 1024
SCORE_STRIP = 128
MISC_W = LANES
MISC_WI = IDX_HD
MISC_B = IDX_HD + IDX_HEADS
MISC_A = MISC_B + A_HEADS
VMEM_LIMIT = 48 * 1024 * 1024
NEG_BIG = -1e30
BISECT_PROBES = 3
COUNT_RADIX = 1024
BISECT_BLIND_ITERS = 6
BISECT_MAX_ITERS = 20


def _cparams(sem):
    return pltpu.CompilerParams(dimension_semantics=sem, vmem_limit_bytes=VMEM_LIMIT)


def _dot(a, b):
    return jnp.dot(a, b, preferred_element_type=F32)


def _dot_nt(a, b, precision=None):
    return lax.dot_general(a, b, (((1,), (1,)), ((), ())), precision=precision,
                           preferred_element_type=F32)


def _split2(x):
    hi = x.astype(BF16)
    return hi, (x - hi.astype(F32)).astype(BF16)


def _mm_split(a, b):
    ah, al = a
    bh, _ = b
    n = ah.shape[0]
    t = _dot(jnp.concatenate([ah, al], axis=0), bh)
    return t[:n] + t[n:]


def _silu(x):
    return x * jax.nn.sigmoid(x)


def _lane_mask(width, lo, hi):
    lane = lax.broadcasted_iota(jnp.int32, (1, width), 1)
    return (lane >= lo) & (lane < hi)


CARRY_ROWS = 8


def _conv_silu_norm(xp_ref, raw, cw, rows, part):
    first = CARRY_ROWS - (CONV_W - 1)
    xp_ref[CARRY_ROWS:CARRY_ROWS + rows, :] = raw
    y = xp_ref[first:first + rows, :] * cw[0:1, :]
    for j in range(1, CONV_W):
        y = y + xp_ref[first + j:first + j + rows, :] * cw[j:j + 1, :]
    y = _silu(y)
    conv_new = xp_ref[rows + first:rows + CARRY_ROWS, :]
    xp_ref[0:CARRY_ROWS, :] = xp_ref[rows:rows + CARRY_ROWS, :]
    if part == 2:
        return y, conv_new
    scale = A_DK ** -0.5 if part == 0 else 1.0
    heads = []
    for h in range(A_HEADS):
        yh = y[:, h * A_DK:(h + 1) * A_DK]
        heads.append(yh * (lax.rsqrt(jnp.sum(yh * yh, axis=-1, keepdims=True) + EPS) * scale))
    return jnp.concatenate(heads, axis=1), conv_new


def _conv_part(extra):
    return int(extra[len("conv"):]) if extra.startswith("conv") else None


def _proj_body(*refs, groups, tiles):
    x_ref, g_ref, w_ref = refs[0], refs[1], refs[2]
    n_aux = 3 * sum(1 for _, normed, _ in groups if normed)
    aux = refs[3:3 + n_aux]
    pos = 3 + n_aux
    n_conv = sum(1 for _, _, extra in groups if _conv_part(extra) is not None)
    if n_conv:
        cbuf_ref, cw_ref = refs[pos], refs[pos + 1]
        pos += 2
    outs = refs[pos:pos + len(groups)]
    pos += len(groups)
    n_t = sum(1 for _, _, extra in groups if extra == "t")
    outs_t = list(refs[pos:pos + n_t])
    pos += n_t
    if n_conv:
        convnew_ref = refs[pos]
        xp_refs = refs[pos + 1:pos + 1 + n_conv]

        @pl.when(pl.program_id(0) % tiles == 0)
        def _():
            for part, xp_ref in enumerate(xp_refs):
                xp_ref[0:CARRY_ROWS, :] = cbuf_ref[:, part * A_WIDTH:(part + 1) * A_WIDTH]
    x = x_ref[...]
    r = lax.rsqrt(jnp.mean(x * x, axis=-1, keepdims=True) + EPS)
    hb = (x * g_ref[...]).astype(BF16)
    off = 0
    ai = 0
    for (width, normed, extra), o_ref in zip(groups, outs):
        y = _dot(hb, w_ref[:, off:off + width]) * r
        part = _conv_part(extra)
        if part is not None:
            cols = slice(part * A_WIDTH, (part + 1) * A_WIDTH)
            y, conv_new = _conv_silu_norm(xp_refs[part], y, cw_ref[:, cols], y.shape[0], part)
            convnew_ref[:, cols] = conv_new
        if normed:
            bd_ref, gain_ref, nm_ref = aux[ai], aux[ai + 1], aux[ai + 2]
            ai += 3
            sq = y * y
            hi = sq.astype(BF16)
            lo = (sq - hi.astype(F32)).astype(BF16)
            ms = _dot(hi, bd_ref[...]) + _dot(lo, bd_ref[...])
            scale = jnp.where(nm_ref[...] > 0.0, lax.rsqrt(ms + EPS), 1.0)
            y = y * scale * gain_ref[...]
        o_ref[...] = y
        if extra == "t":
            outs_t.pop(0)[...] = y.T
        off += width


def _group_mean_matrix(width, group, n_lanes):
    i = np.arange(width)
    m = ((i[:, None] // group) == (i[None, :] // group)) & (i[:, None] < n_lanes) & (i[None, :] < n_lanes)
    return jnp.asarray(m.astype(np.float32) / group, dtype=BF16)


def _proj(x, g, w, groups, aux, tm, t_len, conv=None):
    m, d = x.shape
    nw = w.shape[1]
    tiles = t_len // tm
    bsz = m // t_len
    in_specs = [pl.BlockSpec((tm, d), lambda i: (i, 0)),
                pl.BlockSpec((1, d), lambda i: (0, 0)),
                pl.BlockSpec((d, nw), lambda i: (0, 0))]
    for a in aux:
        in_specs.append(pl.BlockSpec(a.shape, lambda i: (0, 0)))
    out_shape = [jax.ShapeDtypeStruct((m, wd), F32) for wd, _, _ in groups]
    out_specs = [pl.BlockSpec((tm, wd), lambda i: (i, 0)) for wd, _, _ in groups]
    for wd, _, extra in groups:
        if extra == "t":
            out_shape.append(jax.ShapeDtypeStruct((bsz, wd, t_len), F32))
            out_specs.append(pl.BlockSpec((None, wd, tm), lambda i: (i // tiles, 0, i % tiles)))
    args = [x, g, w, *aux]
    scratch = []
    if conv is not None:
        cbuf, cw = conv
        wd = cw.shape[1]
        in_specs += [pl.BlockSpec((None, CARRY_ROWS, wd), lambda i: (i // tiles, 0, 0)),
                     pl.BlockSpec(cw.shape, lambda i: (0, 0))]
        args += [cbuf, cw]
        out_shape.append(jax.ShapeDtypeStruct((bsz, CONV_W - 1, wd), F32))
        out_specs.append(pl.BlockSpec((None, CONV_W - 1, wd), lambda i: (i // tiles, 0, 0)))
        scratch += [pltpu.VMEM((tm + CARRY_ROWS, A_WIDTH), F32) for _ in range(wd // A_WIDTH)]
    return pl.pallas_call(
        functools.partial(_proj_body, groups=tuple(groups), tiles=tiles),
        grid=(m // tm,),
        in_specs=in_specs, out_specs=out_specs, out_shape=out_shape, scratch_shapes=scratch,
        compiler_params=_cparams(("arbitrary",) if conv is not None else ("parallel",)),
        name="proj",
    )(*args)


def _gdn_body(*refs, chunk, rows, fused_conv):
    if fused_conv:
        (q_ref, k_ref, v_ref, za_ref, misc_ref, s0_ref, avec_ref, dtvec_ref, go_ref, cbuf_ref, cw_ref,
         o_ref, sfin_ref, convnew_ref, g_sc, b_sc, s_sc, *conv_sc) = refs
    else:
        (q_ref, k_ref, v_ref, za_ref, misc_ref, s0_ref, avec_ref, dtvec_ref, go_ref,
         o_ref, sfin_ref, g_sc, b_sc, s_sc) = refs
    t = pl.program_id(1)
    c = chunk

    @pl.when(t == 0)
    def _():
        s_sc[...] = s0_ref[0]

    if fused_conv:
        xp_refs, act_refs = conv_sc[:3], conv_sc[3:]

        @pl.when(t == 0)
        def _():
            for part, xp_ref in enumerate(xp_refs):
                xp_ref[0:CARRY_ROWS, :] = cbuf_ref[:, part * A_WIDTH:(part + 1) * A_WIDTH]
        for part, raw_ref in enumerate((q_ref, k_ref, v_ref)):
            cols = slice(part * A_WIDTH, (part + 1) * A_WIDTH)
            act, conv_new = _conv_silu_norm(xp_refs[part], raw_ref[...], cw_ref[:, cols], rows, part)
            convnew_ref[:, cols] = conv_new
            act_refs[part][...] = act
        q_sc, k_sc, v_sc = act_refs
    else:
        q_sc, k_sc, v_sc = q_ref, k_ref, v_ref
    misc = misc_ref[...]
    b_sc[...] = jax.nn.sigmoid(misc)
    g_sc[...] = avec_ref[...] * jax.nn.softplus(misc + dtvec_ref[...])

    ri = lax.broadcasted_iota(jnp.int32, (c, c), 0)
    ci = lax.broadcasted_iota(jnp.int32, (c, c), 1)
    tri = ri >= ci
    strict = ri > ci
    eye_f = (ri == ci).astype(F32)
    tri3 = (lax.broadcasted_iota(jnp.int32, (c, 3 * c), 0)
            >= lax.broadcasted_iota(jnp.int32, (c, 3 * c), 1) % c).astype(BF16)
    go = go_ref[...]
    n_chunks = rows // c

    problems = [(ic, h) for ic in range(n_chunks) for h in range(A_HEADS)]
    gc_all, gc_t = [], []
    for ic in range(n_chunks):
        g = g_sc[ic * c:(ic + 1) * c, :]
        g1 = g.astype(BF16)
        r1 = g - g1.astype(F32)
        g2 = r1.astype(BF16)
        g3 = (r1 - g2.astype(F32)).astype(BF16)
        gc_all.append(_dot(tri3, jnp.concatenate([g1, g2, g3], axis=0)))
    for ic in range(n_chunks):
        gc_t.append(gc_all[ic].T)

    def load(ref, ic, h):
        return ref[ic * c:(ic + 1) * c, h * A_DK:(h + 1) * A_DK]

    col = lambda a, lane: a[:, lane:lane + 1]
    k_bf = [load(k_sc, ic, h).astype(BF16) for ic, h in problems]
    kb = [load(k_sc, ic, h) * col(b_sc[ic * c:(ic + 1) * c, :], MISC_B + h) for ic, h in problems]
    kk = [_dot_nt(kb[i].astype(BF16), k_bf[i]) for i in range(len(problems))]
    qk_raw = [_dot_nt(load(q_sc, ic, h).astype(BF16), k_bf[i]) for i, (ic, h) in enumerate(problems)]
    decay = []
    for ic, h in problems:
        diff = col(gc_all[ic], MISC_A + h) - gc_t[ic][MISC_A + h:MISC_A + h + 1, :]
        decay.append(jnp.where(tri, jnp.exp(jnp.where(tri, diff, 0.0)), 0.0))
    lmat = [jnp.where(strict, kk[i] * decay[i], 0.0) for i in range(len(problems))]
    qk = [(qk_raw[i] * decay[i]).astype(BF16) for i in range(len(problems))]
    x_inv = [eye_f - m for m in lmat]
    p_split = [_split2(m) for m in lmat]
    for _ in range(int(math.log2(c)) - 1):
        p_split = [_split2(_mm_split(ps, ps)) for ps in p_split]
        x_inv = [x + _mm_split(_split2(x), ps) for x, ps in zip(x_inv, p_split)]
    sol = []
    for i, (ic, h) in enumerate(problems):
        egc = jnp.exp(col(gc_all[ic], MISC_A + h))
        beta = col(b_sc[ic * c:(ic + 1) * c, :], MISC_B + h)
        rhs = jnp.concatenate([load(v_sc, ic, h) * beta, kb[i] * egc], axis=-1).astype(BF16)
        sol.append(_mm_split(_split2(x_inv[i]), (rhs, None)))
    pre = []
    for i, (ic, h) in enumerate(problems):
        gc = col(gc_all[ic], MISC_A + h)
        g_last = gc[c - 1:c, :]
        k_dec_t = (load(k_sc, ic, h) * jnp.exp(g_last - gc)).T.astype(BF16)
        wq = jnp.concatenate([sol[i][:, A_DV:], load(q_sc, ic, h) * jnp.exp(gc)], axis=0).astype(BF16)
        pre.append((sol[i][:, :A_DV], wq, k_dec_t, jnp.exp(g_last)))

    heads = range(A_HEADS)
    s_cur = [s_sc[h] for h in heads]
    for ic in range(n_chunks):
        pr = [pre[ic * A_HEADS + h] for h in heads]
        ws = [_dot(pr[h][1], s_cur[h].astype(BF16)) for h in heads]
        ub = [(pr[h][0] - ws[h][:c]).astype(BF16) for h in heads]
        o = [ws[h][c:] + _dot(qk[ic * A_HEADS + h], ub[h]) for h in heads]
        s_cur = [s_cur[h] * pr[h][3] + _dot(pr[h][2], ub[h]) for h in heads]
        for h in heads:
            on = o[h] * lax.rsqrt(jnp.mean(o[h] * o[h], axis=-1, keepdims=True) + EPS) * go
            o_ref[ic * c:(ic + 1) * c, h * A_DV:(h + 1) * A_DV] = on * _silu(load(za_ref, ic, h))
    for h in heads:
        s_sc[h] = s_cur[h]
    sfin_ref[0] = s_sc[...]


def _gdn(q, k, v, za, misc, s0, avec, dtvec, g_o, *, bsz, t_len, chunk, rows, conv=None):
    nt = t_len // rows
    w3 = 3 * A_WIDTH
    row_map = lambda b, t: (b * nt + t, 0)
    const2 = lambda b, t: (0, 0)
    state_spec = pl.BlockSpec((1, A_HEADS, A_DK, A_DV), lambda b, t: (b, 0, 0, 0))
    in_specs = [pl.BlockSpec((rows, A_WIDTH), row_map)] * 4 + [
                pl.BlockSpec((rows, MISC_W), row_map),
                state_spec,
                pl.BlockSpec((1, MISC_W), const2),
                pl.BlockSpec((1, MISC_W), const2),
                pl.BlockSpec((1, A_DV), const2)]
    args = [q, k, v, za, misc, s0, avec, dtvec, g_o]
    out_specs = [pl.BlockSpec((rows, A_WIDTH), row_map), state_spec]
    out_shape = [jax.ShapeDtypeStruct((bsz * t_len, A_WIDTH), F32),
                 jax.ShapeDtypeStruct((bsz, A_HEADS, A_DK, A_DV), F32)]
    scratch = [pltpu.VMEM((rows, MISC_W), F32),
               pltpu.VMEM((rows, MISC_W), F32),
               pltpu.VMEM((A_HEADS, A_DK, A_DV), F32)]
    if conv is not None:
        cbuf, cw = conv
        in_specs += [pl.BlockSpec((None, CARRY_ROWS, w3), lambda b, t: (b, 0, 0)),
                     pl.BlockSpec((CONV_W, w3), const2)]
        args += [cbuf, cw]
        out_specs.append(pl.BlockSpec((None, CONV_W - 1, w3), lambda b, t: (b, 0, 0)))
        out_shape.append(jax.ShapeDtypeStruct((bsz, CONV_W - 1, w3), F32))
        scratch += [pltpu.VMEM((rows + CARRY_ROWS, A_WIDTH), F32)] * 3 + [pltpu.VMEM((rows, A_WIDTH), F32)] * 3
    return pl.pallas_call(
        functools.partial(_gdn_body, chunk=chunk, rows=rows, fused_conv=conv is not None),
        grid=(bsz, nt),
        in_specs=in_specs, out_specs=out_specs, out_shape=out_shape, scratch_shapes=scratch,
        compiler_params=_cparams(("parallel", "arbitrary")),
        name="gdn",
    )(*args)


def _topk_threshold(sc_ref, nkb, shape, key_axis, n_sel, n_adm):
    kf = float(n_sel)
    kblk = shape[key_axis]

    if key_axis == 0:
        strip = min(kblk, 32)
        part_shape = (strip, shape[1])
        strip_of = lambda kb, i: sc_ref[kb, i * strip:(i + 1) * strip, :]

        def put_strip(kb, i, v):
            sc_ref[kb, i * strip:(i + 1) * strip, :] = v
    else:
        strip = kblk
        part_shape = shape
        strip_of = lambda kb, i: sc_ref[kb]

        def put_strip(kb, i, v):
            sc_ref[kb] = v
    kpos = lax.broadcasted_iota(jnp.int32, part_shape, key_axis)

    def reduce_blocks(specs):
        def body(kb, accs):
            accs = list(accs)
            for i in range(kblk // strip):
                s = strip_of(kb, i)
                k0 = kb * kblk + i * strip
                accs = [cmb(acc, fn(s, k0)) for acc, (fn, _, _, _, cmb) in zip(accs, specs)]
            return tuple(accs)
        init = tuple(jnp.full(part_shape, i, dt) for _, i, dt, _, _ in specs)
        accs = lax.fori_loop(0, nkb, body, init)
        return [acc if op is None else op(acc, axis=key_axis, keepdims=True)
                for acc, (_, _, _, op, _) in zip(accs, specs)]

    one_if = lambda c: jnp.where(c, 1.0, 0.0)
    vmax, vmin, cpos, czero = reduce_blocks([
        (lambda s, k0: s, -jnp.inf, F32, jnp.max, jnp.maximum),
        (lambda s, k0: jnp.where(s == -jnp.inf, jnp.inf, s), jnp.inf, F32, jnp.min, jnp.minimum),
        (lambda s, k0: one_if(s > 0.0), 0.0, F32, jnp.sum, jnp.add),
        (lambda s, k0: one_if(s == 0.0), 0.0, F32, jnp.sum, jnp.add)])

    long_row = n_adm > kf
    pos_row = long_row & (cpos >= kf)
    zero_row = long_row & (cpos < kf) & (cpos + czero >= kf)
    neg_row = long_row & (cpos + czero < kf)
    lo0 = jnp.where(pos_row | zero_row, 0.0, vmin - jnp.maximum(1.0, jnp.abs(vmin)))
    cnt_lo0 = jnp.where(pos_row, cpos, jnp.where(zero_row, kf, n_adm))
    hi0 = jnp.where(neg_row, 0.0, vmax)
    cnt_hi0 = jnp.where(neg_row, cpos, 0.0)
    need_zero = jnp.where(zero_row, kf - cpos, 0.0)

    def narrow(lo, hi, cnt_lo, cnt_hi):
        fracs = [(t + 1.0) / (BISECT_PROBES + 1.0) for t in range(BISECT_PROBES)]
        mids = [lo * (1.0 - f) + hi * f for f in fracs]
        for t in range(1, BISECT_PROBES):
            mids[t] = jnp.maximum(mids[t], mids[t - 1])
        codes = [sum(COUNT_RADIX ** u for u in range(t + 1)) for t in range(BISECT_PROBES)]

        def encode(s, k0):
            e = jnp.zeros(s.shape, jnp.int32)
            for mid, code in zip(mids, codes):
                e = jnp.where(s > mid, code, e)
            return e
        (packed,) = reduce_blocks([(encode, 0, jnp.int32, None, jnp.add)])
        cnts = []
        for t in range(BISECT_PROBES):
            digit = packed & (COUNT_RADIX - 1) if t < BISECT_PROBES - 1 else packed
            packed = packed >> COUNT_RADIX.bit_length() - 1
            cnts.append(jnp.sum(digit.astype(F32), axis=key_axis, keepdims=True))
        lo_n, cnt_lo_n, hi_n, cnt_hi_n = lo, cnt_lo, hi, cnt_hi
        for mid, cm in zip(mids, cnts):
            up = cm >= kf
            lo_n = jnp.where(up, mid, lo_n)
            cnt_lo_n = jnp.where(up, cm, cnt_lo_n)
        for mid, cm in zip(mids[::-1], cnts[::-1]):
            dn = cm < kf
            hi_n = jnp.where(dn, mid, hi_n)
            cnt_hi_n = jnp.where(dn, cm, cnt_hi_n)
        return lo_n, hi_n, cnt_lo_n, cnt_hi_n

    def unresolved(cnt_lo):
        return (jnp.max(cnt_lo) > kf).astype(jnp.int32)

    blind = jnp.where(unresolved(cnt_lo0) > 0, BISECT_BLIND_ITERS, 0)
    state = lax.fori_loop(0, blind, lambda i, st: narrow(*st), (lo0, hi0, cnt_lo0, cnt_hi0))

    def cond(st):
        return (st[0] < BISECT_MAX_ITERS) & (st[1] > 0)

    def body(st):
        new = narrow(*st[2:])
        return (st[0] + 1, unresolved(new[2])) + new

    _, _, lo, hi, cnt_lo, cnt_hi = lax.while_loop(
        cond, body, (blind, unresolved(state[2])) + state)

    fix = cnt_lo > kf
    need0 = jnp.where(fix, kf - cnt_hi, 0.0)
    big_idx = 2 ** 30

    def fix_cond(need):
        return jnp.max(need) > 0.0

    def fix_body(need):
        def in_cluster(s):
            return (s > lo) & (s <= hi)
        (mval,) = reduce_blocks([(lambda s, k0: jnp.where(in_cluster(s), s, -jnp.inf), -jnp.inf, F32,
                                  jnp.max, jnp.maximum)])
        (idx,) = reduce_blocks(
            [(lambda s, k0: jnp.where(in_cluster(s) & (s == mval), kpos + k0, big_idx),
              big_idx, jnp.int32, jnp.min, jnp.minimum)])
        active = need > 0.0

        def promote(kb, carry):
            for i in range(kblk // strip):
                s = strip_of(kb, i)
                put_strip(kb, i, jnp.where(active & ((kpos + kb * kblk + i * strip) == idx), jnp.inf, s))
            return carry
        lax.fori_loop(0, nkb, promote, 0)
        return jnp.where(active, need - 1.0, need)

    lax.while_loop(fix_cond, fix_body, need0)

    @pl.when(jnp.max(need_zero) > 0.0)
    def _():
        r = lax.broadcasted_iota(jnp.int32, (kblk, kblk), 0)
        c = lax.broadcasted_iota(jnp.int32, (kblk, kblk), 1)
        tri = ((r >= c) if key_axis == 0 else (r <= c)).astype(BF16)

        def tie_blk(kb, seen):
            s = sc_ref[kb]
            z = (s == 0.0) & zero_row
            zb = one_if(z).astype(BF16)
            if key_axis == 0:
                rank = _dot(tri, zb) + seen
                last = rank[kblk - 1:kblk, :]
            else:
                rank = _dot(zb, tri) + seen
                last = rank[:, kblk - 1:kblk]
            sc_ref[kb] = jnp.where(z & (rank <= need_zero), jnp.inf, s)
            return last
        lax.fori_loop(0, nkb, tie_blk, jnp.zeros_like(need_zero))

    return jnp.where(fix, hi, lo)


def _dsa_body(qb_ref, qi_ref, miscq_ref, zb_ref, k_ref, vt_ref, misck_ref, o_ref, sc_ref, s_ref, *, n_sel, qblk):
    j = pl.program_id(1)
    nkb = ((j + 1) * qblk + KEY_BLOCK - 1) // KEY_BLOCK
    qi = qi_ref[...]
    q_h = [qi[:, h * IDX_HD:(h + 1) * IDX_HD].astype(BF16) for h in range(IDX_HEADS)]
    w_t = miscq_ref[...].T[MISC_WI:MISC_WI + IDX_HEADS, :]
    qchunk = (j * qblk + lax.broadcasted_iota(jnp.int32, (1, qblk), 1)) // CHUNK
    krow = lax.broadcasted_iota(jnp.int32, (KEY_BLOCK, 1), 0)

    def score_blk(kb, carry):
        k0 = pl.multiple_of(kb * KEY_BLOCK, KEY_BLOCK)
        for half in range(KEY_BLOCK // SCORE_STRIP):
            r0 = half * SCORE_STRIP
            ki = misck_ref[pl.ds(k0 + r0, SCORE_STRIP), 0:IDX_HD].astype(BF16)
            acc = jnp.zeros((SCORE_STRIP, qblk), F32)
            for h in range(IDX_HEADS):
                acc = acc + w_t[h:h + 1, :] * jnp.maximum(_dot_nt(ki, q_h[h]), 0.0)
            adm = ((k0 + r0 + krow[:SCORE_STRIP]) // CHUNK) <= qchunk
            sc_ref[kb, r0:r0 + SCORE_STRIP, :] = jnp.where(adm, acc, -jnp.inf)
        return carry

    npairs = (nkb + 1) // 2

    def pair_loop(body, init):
        return lax.fori_loop(0, npairs, lambda i, c: body(2 * i + 1, body(2 * i, c)), init)

    pair_loop(score_blk, 0)

    n_adm = ((qchunk + 1) * CHUNK).astype(F32)
    thr = _topk_threshold(sc_ref, nkb, (KEY_BLOCK, qblk), 0, n_sel, n_adm)

    qb = qb_ref[...] * (B_HD ** -0.5 * math.log2(math.e))
    qs = jnp.concatenate([jnp.where(_lane_mask(B_WIDTH, h * B_HD, (h + 1) * B_HD), qb, 0.0)
                          for h in range(B_HEADS)], axis=0).astype(BF16)
    wide = B_HEADS * qblk

    def fold(acc, x, combine):
        for i in range(x.shape[0] // 8):
            acc = combine(acc, x[8 * i:8 * (i + 1)])
        return acc

    def logits_blk(kb, m8):
        k0 = pl.multiple_of(kb * KEY_BLOCK, KEY_BLOCK)
        s = _dot_nt(k_ref[pl.ds(k0, KEY_BLOCK), :].astype(BF16), qs)
        sel = sc_ref[kb] > thr
        s = jnp.where(jnp.concatenate([sel] * B_HEADS, axis=1), s, NEG_BIG)
        s_ref[kb] = s
        return fold(m8, s, jnp.maximum)

    m8 = pair_loop(logits_blk, jnp.full((8, wide), NEG_BIG, F32))
    m = jnp.max(m8, axis=0, keepdims=True)

    def pv_blk(kb, carry):
        l8, accs = carry
        p = jnp.exp2(s_ref[kb] - m)
        pb = p.astype(BF16)
        k0 = pl.multiple_of(kb * KEY_BLOCK, KEY_BLOCK)
        vt = vt_ref[:, pl.ds(k0, KEY_BLOCK)].astype(BF16)
        accs = tuple(accs[h] + _dot(vt[h * B_HD:(h + 1) * B_HD, :], pb[:, h * qblk:(h + 1) * qblk])
                     for h in range(B_HEADS))
        return fold(l8, p, jnp.add), accs

    l8, accs = pair_loop(
        pv_blk, (jnp.zeros((8, wide), F32), tuple(jnp.zeros((B_HD, qblk), F32) for _ in range(B_HEADS))))
    l = jnp.sum(l8, axis=0, keepdims=True)
    o_t = jnp.concatenate([accs[h] / l[:, h * qblk:(h + 1) * qblk] for h in range(B_HEADS)], axis=0)
    o_ref[...] = o_t.T * _silu(zb_ref[...])


def _dsa_prompt(qb, qi, misc, zb, kb, vt, *, bsz, t_len, n_sel, qblk):
    nq = t_len // qblk
    nkb = t_len // KEY_BLOCK
    assert nkb * (KEY_BLOCK // 8) < COUNT_RADIX and nkb % 2 == 0
    qmap = lambda b, j: (b * nq + j, 0)
    kmap = lambda b, j: (b, 0)
    return pl.pallas_call(
        functools.partial(_dsa_body, n_sel=n_sel, qblk=qblk),
        grid=(bsz, nq),
        in_specs=[pl.BlockSpec((qblk, B_WIDTH), qmap),
                  pl.BlockSpec((qblk, IDX_HEADS * IDX_HD), qmap),
                  pl.BlockSpec((qblk, MISC_W), qmap),
                  pl.BlockSpec((qblk, B_WIDTH), qmap),
                  pl.BlockSpec((t_len, B_WIDTH), kmap),
                  pl.BlockSpec((None, B_WIDTH, t_len), lambda b, j: (b, 0, 0)),
                  pl.BlockSpec((t_len, MISC_W), kmap)],
        out_specs=pl.BlockSpec((qblk, B_WIDTH), qmap),
        out_shape=jax.ShapeDtypeStruct((bsz * t_len, B_WIDTH), F32),
        scratch_shapes=[pltpu.VMEM((nkb, KEY_BLOCK, qblk), F32),
                        pltpu.VMEM((nkb, KEY_BLOCK, B_HEADS * qblk), F32)],
        compiler_params=_cparams(("parallel", "arbitrary")),
        name="dsa_prompt",
    )(qb, qi, misc, zb, kb, vt, misc)


def _dsa_dec_body(qb_ref, qi_ref, misc_ref, zb_ref, kn_ref, vn_ref, ck_ref, cv_ref, cki_ref, o_ref,
                  sc_ref, thr_ref, m_ref, l_ref, acc_ref, *, p_len, tq, n_sel, kstep):
    step = pl.program_id(1)
    nsteps = p_len // kstep
    sub = kstep // KEY_BLOCK
    npast = p_len // KEY_BLOCK
    nkb = npast + 1
    pad_rows = KEY_BLOCK - tq
    rows = B_HEADS * tq

    @pl.when(step == 0)
    def _():
        qi = qi_ref[...]
        misc = misc_ref[...]
        q2 = jnp.concatenate([qi[:, h * IDX_HD:(h + 1) * IDX_HD] for h in range(IDX_HEADS)],
                             axis=0).astype(BF16)
        w_h = [misc[:, MISC_WI + h:MISC_WI + h + 1] for h in range(IDX_HEADS)]

        def head_sum(lg):
            acc = jnp.zeros((tq, lg.shape[1]), F32)
            for h in range(IDX_HEADS):
                acc = acc + w_h[h] * jnp.maximum(lg[h * tq:(h + 1) * tq, :], 0.0)
            return acc

        for i in range(nsteps):
            sc = head_sum(_dot(q2, cki_ref[:, i * kstep:(i + 1) * kstep].astype(BF16)))
            for u in range(sub):
                sc_ref[i * sub + u] = sc[:, u * KEY_BLOCK:(u + 1) * KEY_BLOCK]
        lane = lax.broadcasted_iota(jnp.int32, (1, KEY_BLOCK), 1)
        ki_new = jnp.concatenate([misc[:, 0:IDX_HD], jnp.zeros((pad_rows, IDX_HD), F32)], axis=0)
        sc_ref[npast] = jnp.where(lane < tq, head_sum(_dot_nt(q2, ki_new.astype(BF16))), -jnp.inf)
        n_adm = jnp.full((tq, 1), float(p_len + tq), F32)
        thr_ref[...] = _topk_threshold(sc_ref, nkb, (tq, KEY_BLOCK), 1, n_sel, n_adm)
        m_ref[...] = jnp.full((rows, 1), NEG_BIG, F32)
        l_ref[...] = jnp.zeros((rows, 1), F32)
        acc_ref[...] = jnp.zeros((rows, B_HD), F32)

    qb = qb_ref[...] * (B_HD ** -0.5 * math.log2(math.e))
    q_h = [qb[:, h * B_HD:(h + 1) * B_HD].astype(BF16) for h in range(B_HEADS)]
    thr = thr_ref[...]

    def fold_in(logits_of, pv_of, sel):
        s = jnp.concatenate([jnp.where(sel, logits_of(h), NEG_BIG) for h in range(B_HEADS)], axis=0)
        m_old = m_ref[...]
        m_new = jnp.maximum(m_old, jnp.max(s, axis=-1, keepdims=True))
        alpha = jnp.exp2(m_old - m_new)
        p = jnp.exp2(s - m_new)
        pb = p.astype(BF16)
        pv = jnp.concatenate([pv_of(h, pb[h * tq:(h + 1) * tq, :]) for h in range(B_HEADS)], axis=0)
        l_ref[...] = alpha * l_ref[...] + jnp.sum(p, axis=-1, keepdims=True)
        acc_ref[...] = alpha * acc_ref[...] + pv
        m_ref[...] = m_new

    sel = jnp.concatenate([sc_ref[step * sub + u] > thr for u in range(sub)], axis=1)
    fold_in(lambda h: _dot(q_h[h], ck_ref[h].astype(BF16)),
            lambda h, p: _dot_nt(p, cv_ref[h].astype(BF16)), sel)

    @pl.when(step == nsteps - 1)
    def _():
        zpad = jnp.zeros((pad_rows, B_WIDTH), F32)
        kn = jnp.concatenate([kn_ref[...], zpad], axis=0).astype(BF16)
        vn = jnp.concatenate([vn_ref[...], zpad], axis=0).astype(BF16)
        fold_in(lambda h: _dot_nt(q_h[h], kn[:, h * B_HD:(h + 1) * B_HD]),
                lambda h, p: _dot(p, vn[:, h * B_HD:(h + 1) * B_HD]), sc_ref[npast] > thr)
        res = acc_ref[...] / l_ref[...]
        out = jnp.concatenate([res[h * tq:(h + 1) * tq, :] for h in range(B_HEADS)], axis=1)
        o_ref[...] = out * _silu(zb_ref[...])


def _dsa_decode(qb, qi, misc, zb, kb, vb, ck, cv, cki, *, bsz, t_len, p_len, n_sel):
    kstep = DEC_KEYS_PER_STEP
    assert p_len // KEY_BLOCK + 1 < COUNT_RADIX and p_len % kstep == 0
    rows = B_HEADS * t_len
    qmap = lambda b, s: (b, 0)
    cmap = lambda b, s: (b, 0, 0, s)
    ck, cv = (a.transpose(0, 2, 3, 1) for a in (ck, cv))
    cki = cki.transpose(0, 2, 1)
    return pl.pallas_call(
        functools.partial(_dsa_dec_body, p_len=p_len, tq=t_len, n_sel=n_sel, kstep=kstep),
        grid=(bsz, p_len // kstep),
        in_specs=[pl.BlockSpec((t_len, B_WIDTH), qmap),
                  pl.BlockSpec((t_len, IDX_HEADS * IDX_HD), qmap),
                  pl.BlockSpec((t_len, MISC_W), qmap),
                  pl.BlockSpec((t_len, B_WIDTH), qmap),
                  pl.BlockSpec((t_len, B_WIDTH), qmap),
                  pl.BlockSpec((t_len, B_WIDTH), qmap),
                  pl.BlockSpec((None, B_HEADS, B_HD, kstep), cmap),
                  pl.BlockSpec((None, B_HEADS, B_HD, kstep), cmap),
                  pl.BlockSpec((None, IDX_HD, p_len), lambda b, s: (b, 0, 0))],
        out_specs=pl.BlockSpec((t_len, B_WIDTH), qmap),
        out_shape=jax.ShapeDtypeStruct((bsz * t_len, B_WIDTH), F32),
        scratch_shapes=[pltpu.VMEM((p_len // KEY_BLOCK + 1, t_len, KEY_BLOCK), F32),
                        pltpu.VMEM((t_len, 1), F32),
                        pltpu.VMEM((rows, 1), F32),
                        pltpu.VMEM((rows, 1), F32),
                        pltpu.VMEM((rows, B_HD), F32)],
        compiler_params=_cparams(("parallel", "arbitrary")),
        name="dsa_decode",
    )(qb, qi, misc, zb, kb, vb, ck, cv, cki)


def _tail_body(x_ref, oa_ref, ob_ref, qm_ref, zm_ref, mk_ref, mv_ref, w_ref, y_ref):
    qm = qm_ref[...] * (M_HD ** -0.5)
    mk = mk_ref[...].reshape(N_MEM, M_WIDTH).astype(BF16)
    mv = mv_ref[...].reshape(N_MEM, M_WIDTH).astype(BF16)
    acc = _dot(oa_ref[...].astype(BF16), w_ref[0:A_WIDTH, :])
    acc = acc + _dot(ob_ref[...].astype(BF16), w_ref[A_WIDTH:A_WIDTH + B_WIDTH, :])
    om = jnp.zeros(qm.shape, F32)
    for h in range(M_HEADS):
        mask = _lane_mask(M_WIDTH, h * M_HD, (h + 1) * M_HD)
        s = _dot_nt(jnp.where(mask, qm, 0.0).astype(BF16), mk)
        p = jnp.exp(s - jnp.max(s, axis=-1, keepdims=True))
        l = jnp.sum(p, axis=-1, keepdims=True)
        om = om + jnp.where(mask, _dot(p.astype(BF16), mv) / l, 0.0)
    om = om * _silu(zm_ref[...])
    acc = acc + _dot(om.astype(BF16), w_ref[A_WIDTH + B_WIDTH:, :])
    y_ref[...] = x_ref[...] + acc


def _tail(x, oa, ob, qm, zm, mk, mv, w, *, bsz, t_len, tq):
    m, d = x.shape
    nq = t_len // tq
    qmap = lambda b, j: (b * nq + j, 0)
    mem_spec = pl.BlockSpec((None,) + mk.shape[1:], lambda b, j: (b,) + (0,) * (mk.ndim - 1))
    return pl.pallas_call(
        _tail_body,
        grid=(bsz, nq),
        in_specs=[pl.BlockSpec((tq, d), qmap),
                  pl.BlockSpec((tq, A_WIDTH), qmap),
                  pl.BlockSpec((tq, B_WIDTH), qmap),
                  pl.BlockSpec((tq, M_WIDTH), qmap),
                  pl.BlockSpec((tq, M_WIDTH), qmap),
                  mem_spec, mem_spec,
                  pl.BlockSpec(w.shape, lambda b, j: (0, 0))],
        out_specs=pl.BlockSpec((tq, d), qmap),
        out_shape=jax.ShapeDtypeStruct((m, d), F32),
        compiler_params=_cparams(("parallel", "parallel")),
        name="tail",
    )(x, oa, ob, qm, zm, mk, mv, w)


def _prep_weights(g_in, w_in, conv_w, a_log, dt_bias, g_o, g_qb, g_kb, g_ki, g_qm, w_out):
    offs = np.concatenate([[0], np.cumsum(IN_SPLITS)])
    col = lambda i: w_in[:, int(offs[i]):int(offs[i + 1])]
    (w_qkv, w_za, w_ba, w_aa, w_qb, w_kb, w_vb, w_zb, w_qi, w_ki, w_wi, w_qm, w_zm) = [col(i) for i in range(13)]
    d = w_in.shape[0]
    w_misc = jnp.concatenate(
        [w_ki, w_wi, w_ba, w_aa, jnp.zeros((d, MISC_W - MISC_A - A_HEADS), w_in.dtype)], axis=1)
    w_all = jnp.concatenate([w_za, w_qb, w_kb, w_vb, w_zb, w_qi, w_qm, w_zm, w_misc, w_qkv],
                            axis=1).astype(BF16)
    groups = [(A_WIDTH, False, ""), (B_WIDTH, True, ""),
              (B_WIDTH, True, "t"), (B_WIDTH, False, "t"), (B_WIDTH, False, ""),
              (IDX_HEADS * IDX_HD, False, ""), (M_WIDTH, True, ""), (M_WIDTH, False, ""),
              (MISC_W, True, ""),
              (A_WIDTH, False, "conv0"), (A_WIDTH, False, "conv1"), (A_WIDTH, False, "conv2")]
    bd64 = _group_mean_matrix(B_WIDTH, B_HD, B_WIDTH)
    ones256 = jnp.ones((1, B_WIDTH), F32)
    misc_gain = jnp.concatenate([g_ki, jnp.full((IDX_HEADS,), IDX_SCALE, F32),
                                 jnp.ones((MISC_W - MISC_B,), F32)])[None, :]
    misc_nm = (jnp.arange(MISC_W) < IDX_HD).astype(F32)[None, :]
    aux = [bd64, jnp.tile(g_qb, B_HEADS)[None, :], ones256,
           bd64, jnp.tile(g_kb, B_HEADS)[None, :], ones256,
           bd64, jnp.tile(g_qm, M_HEADS)[None, :], ones256,
           _group_mean_matrix(MISC_W, IDX_HD, IDX_HD), misc_gain, misc_nm]
    pad_a = lambda v: jnp.zeros((1, MISC_W), F32).at[0, MISC_A:MISC_A + A_HEADS].set(v)
    return dict(g_in=g_in[None, :], w_all=w_all, groups=groups, aux=aux, conv_w=conv_w,
                avec=pad_a(-jnp.exp(a_log)), dtvec=pad_a(dt_bias), g_o=g_o[None, :],
                w_out=w_out.astype(BF16))


def _layer(x, mem_k, mem_v, conv_buf, s0, past, gdn_chunk, gdn_rows, wp):
    bsz, t_len, d = x.shape
    m = bsz * t_len
    x2 = x.reshape(m, d)
    tm = min(PROJ_ROWS, m)
    heads_last = lambda a_t: a_t.reshape(bsz, B_HEADS, B_HD, t_len).transpose(0, 3, 1, 2)
    cbuf = jnp.concatenate([jnp.zeros((bsz, CARRY_ROWS - (CONV_W - 1), 3 * A_WIDTH), F32), conv_buf], axis=1)
    conv = (cbuf, wp["conv_w"])
    gdn_args = dict(bsz=bsz, t_len=t_len, chunk=gdn_chunk, rows=gdn_rows)
    if past is None:
        (za, qb, kb, vb, zb, qi, qm, zm, misc, qa, ka, va, kt, vt, conv_new) = _proj(
            x2, wp["g_in"], wp["w_all"], wp["groups"], wp["aux"], tm, t_len, conv=conv)
        k_new, v_new = heads_last(kt), heads_last(vt)
        oa, s_new = _gdn(qa, ka, va, za, misc, s0, wp["avec"], wp["dtvec"], wp["g_o"], **gdn_args)
    else:
        groups = [(wd, normed, "") for wd, normed, _ in wp["groups"]]
        (za, qb, kb, vb, zb, qi, qm, zm, misc, qa, ka, va) = _proj(x2, wp["g_in"], wp["w_all"], groups,
                                                                   wp["aux"], tm, m)
        k_new, v_new = (a.reshape(bsz, t_len, B_HEADS, B_HD) for a in (kb, vb))
        oa, s_new, conv_new = _gdn(qa, ka, va, za, misc, s0, wp["avec"], wp["dtvec"], wp["g_o"], conv=conv,
                                   **gdn_args)
    if past is None:
        n_sel = min(TOPK_MAX, t_len // 4)
        ob = _dsa_prompt(qb, qi, misc, zb, kb, vt, bsz=bsz, t_len=t_len, n_sel=n_sel, qblk=DSA_QBLK)
    else:
        ck, cv, cki = past
        p_len = ck.shape[1]
        assert (p_len + t_len - 1) // CHUNK <= p_len // CHUNK and p_len % KEY_BLOCK == 0
        n_sel = min(TOPK_MAX, (p_len + t_len) // 4)
        ob = _dsa_decode(qb, qi, misc, zb, kb, vb, ck, cv, cki, bsz=bsz, t_len=t_len, p_len=p_len,
                         n_sel=n_sel)
    y = _tail(x2, oa, ob, qm, zm, mem_k, mem_v, wp["w_out"], bsz=bsz, t_len=t_len, tq=min(TAIL_ROWS, t_len))
    return (y.reshape(bsz, t_len, d), conv_new, s_new, k_new, v_new,
            misc[:, :IDX_HD].reshape(bsz, t_len, IDX_HD))


def _memory_kv(mem, g_mem, w_mem_kv, g_km):
    bsz, n_mem, d = mem.shape
    groups = [(M_WIDTH, True, "t"), (M_WIDTH, False, "t")]
    aux = [_group_mean_matrix(M_WIDTH, M_HD, M_WIDTH), jnp.tile(g_km, M_HEADS)[None, :],
           jnp.ones((1, M_WIDTH), F32)]
    mk, mv, mk_t, mv_t = _proj(mem.reshape(bsz * n_mem, d), g_mem[None, :], w_mem_kv.astype(BF16),
                               groups, aux, n_mem, n_mem)
    heads_last = lambda a_t: a_t.reshape(bsz, M_HEADS, M_HD, n_mem).transpose(0, 3, 1, 2)
    return (mk.reshape(bsz, n_mem, M_WIDTH), mv.reshape(bsz, n_mem, M_WIDTH), heads_last(mk_t), heads_last(mv_t))


def kernel(x_prompt, x_sample, state_conv_A, state_ssm_A, cache_k_B, cache_v_B, cache_kidx_B, cache_mem_k,
           cache_mem_v, mem_prompt, g_in, w_in, conv_w_A, a_log_A, dt_bias_A, g_o_A, g_q_B, g_k_B, g_kidx_B,
           g_mem, w_mem_kv, g_q_M, g_k_M, w_out):
    depth = w_in.shape[0]
    assert depth == 1
    l = 0
    bp, t_p, _ = x_prompt.shape
    bs, t_s, _ = x_sample.shape
    wp = _prep_weights(g_in[l], w_in[l], conv_w_A[l], a_log_A[l], dt_bias_A[l], g_o_A[l], g_q_B[l], g_k_B[l],
                       g_kidx_B[l], g_q_M[l], w_out[l])
    mk, mv, mk_heads, mv_heads = _memory_kv(mem_prompt, g_mem[l], w_mem_kv[l], g_k_M[l])
    zero_conv = jnp.zeros((bp, CONV_W - 1, 3 * A_WIDTH), F32)
    zero_ssm = jnp.zeros((bp, A_HEADS, A_DK, A_DV), F32)
    yp, c1, s1, k1, v1, ki1 = _layer(x_prompt, mk, mv, zero_conv, zero_ssm, None, CHUNK, 4 * CHUNK, wp)
    ys, c2, s2, k2, v2, ki2 = _layer(
        x_sample, cache_mem_k[l], cache_mem_v[l],
        state_conv_A[l], state_ssm_A[l], (cache_k_B[l], cache_v_B[l], cache_kidx_B[l]), t_s, t_s, wp)
    st = lambda a: a[None]
    return (yp, ys, st(c1), st(s1), st(k1), st(v1), st(ki1),
            st(mk_heads), st(mv_heads),
            st(c2), st(s2), st(k2), st(v2), st(ki2))
```

```python
import functools
import math

import numpy as np
import jax
import jax.numpy as jnp
from jax import lax
from jax.experimental import pallas as pl
from jax.experimental.pallas import tpu as pltpu

F32 = jnp.float32
BF16 = jnp.bfloat16
HIGHEST = lax.Precision.HIGHEST

D_MODEL = 1024
CHUNK = 64
A_HEADS = 4
A_DK = 128
A_DV = 128
A_WIDTH = A_HEADS * A_DV
CONV_W = 4
B_HEADS = 4
B_HD = 64
B_WIDTH = B_HEADS * B_HD
IDX_HEADS = 8
IDX_HD = 32
IDX_SCALE = (IDX_HEADS ** -0.5) * (IDX_HD ** -0.5)
TOPK_MAX = 256
Q_BLOCK = 128
N_MEM = 256
M_HEADS = 4
M_HD = 64
M_WIDTH = M_HEADS * M_HD
EPS = 1e-6
IN_SPLITS = (3 * A_WIDTH, A_WIDTH, A_HEADS, A_HEADS,
             B_WIDTH, B_WIDTH, B_WIDTH, B_WIDTH, IDX_HEADS * IDX_HD, IDX_HD, IDX_HEADS,
             M_WIDTH, M_WIDTH)

LANES = 128
KEY_BLOCK = 256
DSA_QBLK = 512
TAIL_ROWS = 512
PROJ_ROWS = 512
DEC_KEYS_PER_STEP = 2048
SCORE_STRIP = 128
MISC_W = LANES
MISC_WI = IDX_HD
MISC_B = IDX_HD + IDX_HEADS
MISC_A = MISC_B + A_HEADS
VMEM_LIMIT = 48 * 1024 * 1024
NEG_BIG = -1e30
BISECT_PROBES = 3
COUNT_RADIX = 1024
BISECT_BLIND_ITERS = 7
BISECT_MAX_ITERS = 20


def _cparams(sem):
    return pltpu.CompilerParams(dimension_semantics=sem, vmem_limit_bytes=VMEM_LIMIT)


def _dot(a, b):
    return jnp.dot(a, b, preferred_element_type=F32)


def _dot_nt(a, b, precision=None):
    return lax.dot_general(a, b, (((1,), (1,)), ((), ())), precision=precision,
                           preferred_element_type=F32)


def _split2(x):
    hi = x.astype(BF16)
    return hi, (x - hi.astype(F32)).astype(BF16)


def _mm_split(a, b):
    ah, al = a
    bh, _ = b
    n = ah.shape[0]
    t = _dot(jnp.concatenate([ah, al], axis=0), bh)
    return t[:n] + t[n:]


def _silu(x):
    return x * jax.nn.sigmoid(x)


def _lane_mask(width, lo, hi):
    lane = lax.broadcasted_iota(jnp.int32, (1, width), 1)
    return (lane >= lo) & (lane < hi)


CARRY_ROWS = 8


def _conv_silu_norm(xp_ref, raw, cw, rows, part):
    first = CARRY_ROWS - (CONV_W - 1)
    xp_ref[CARRY_ROWS:CARRY_ROWS + rows, :] = raw
    y = xp_ref[first:first + rows, :] * cw[0:1, :]
    for j in range(1, CONV_W):
        y = y + xp_ref[first + j:first + j + rows, :] * cw[j:j + 1, :]
    y = _silu(y)
    conv_new = xp_ref[rows + first:rows + CARRY_ROWS, :]
    xp_ref[0:CARRY_ROWS, :] = xp_ref[rows:rows + CARRY_ROWS, :]
    if part == 2:
        return y, conv_new
    scale = A_DK ** -0.5 if part == 0 else 1.0
    heads = []
    for h in range(A_HEADS):
        yh = y[:, h * A_DK:(h + 1) * A_DK]
        heads.append(yh * (lax.rsqrt(jnp.sum(yh * yh, axis=-1, keepdims=True) + EPS) * scale))
    return jnp.concatenate(heads, axis=1), conv_new


def _conv_part(extra):
    return int(extra[len("conv"):]) if extra.startswith("conv") else None


def _proj_body(*refs, groups, tiles):
    x_ref, g_ref, w_ref = refs[0], refs[1], refs[2]
    n_aux = 3 * sum(1 for _, normed, _ in groups if normed)
    aux = refs[3:3 + n_aux]
    pos = 3 + n_aux
    n_conv = sum(1 for _, _, extra in groups if _conv_part(extra) is not None)
    if n_conv:
        cbuf_ref, cw_ref = refs[pos], refs[pos + 1]
        pos += 2
    outs = refs[pos:pos + len(groups)]
    pos += len(groups)
    n_t = sum(1 for _, _, extra in groups if extra == "t")
    outs_t = list(refs[pos:pos + n_t])
    pos += n_t
    if n_conv:
        convnew_ref = refs[pos]
        xp_refs = refs[pos + 1:pos + 1 + n_conv]

        @pl.when(pl.program_id(0) % tiles == 0)
        def _():
            for part, xp_ref in enumerate(xp_refs):
                xp_ref[0:CARRY_ROWS, :] = cbuf_ref[:, part * A_WIDTH:(part + 1) * A_WIDTH]
    x = x_ref[...]
    r = lax.rsqrt(jnp.mean(x * x, axis=-1, keepdims=True) + EPS)
    hb = (x * g_ref[...]).astype(BF16)
    off = 0
    ai = 0
    for (width, normed, extra), o_ref in zip(groups, outs):
        y = _dot(hb, w_ref[:, off:off + width]) * r
        part = _conv_part(extra)
        if part is not None:
            cols = slice(part * A_WIDTH, (part + 1) * A_WIDTH)
            y, conv_new = _conv_silu_norm(xp_refs[part], y, cw_ref[:, cols], y.shape[0], part)
            convnew_ref[:, cols] = conv_new
        if normed:
            bd_ref, gain_ref, nm_ref = aux[ai], aux[ai + 1], aux[ai + 2]
            ai += 3
            sq = y * y
            hi = sq.astype(BF16)
            lo = (sq - hi.astype(F32)).astype(BF16)
            ms = _dot(hi, bd_ref[...]) + _dot(lo, bd_ref[...])
            scale = jnp.where(nm_ref[...] > 0.0, lax.rsqrt(ms + EPS), 1.0)
            y = y * scale * gain_ref[...]
        o_ref[...] = y
        if extra == "t":
            outs_t.pop(0)[...] = y.T
        off += width


def _group_mean_matrix(width, group, n_lanes):
    i = np.arange(width)
    m = ((i[:, None] // group) == (i[None, :] // group)) & (i[:, None] < n_lanes) & (i[None, :] < n_lanes)
    return jnp.asarray(m.astype(np.float32) / group, dtype=BF16)


def _proj(x, g, w, groups, aux, tm, t_len, conv=None):
    m, d = x.shape
    nw = w.shape[1]
    tiles = t_len // tm
    bsz = m // t_len
    in_specs = [pl.BlockSpec((tm, d), lambda i: (i, 0)),
                pl.BlockSpec((1, d), lambda i: (0, 0)),
                pl.BlockSpec((d, nw), lambda i: (0, 0))]
    for a in aux:
        in_specs.append(pl.BlockSpec(a.shape, lambda i: (0, 0)))
    out_shape = [jax.ShapeDtypeStruct((m, wd), F32) for wd, _, _ in groups]
    out_specs = [pl.BlockSpec((tm, wd), lambda i: (i, 0)) for wd, _, _ in groups]
    for wd, _, extra in groups:
        if extra == "t":
            out_shape.append(jax.ShapeDtypeStruct((bsz, wd, t_len), F32))
            out_specs.append(pl.BlockSpec((None, wd, tm), lambda i: (i // tiles, 0, i % tiles)))
    args = [x, g, w, *aux]
    scratch = []
    if conv is not None:
        cbuf, cw = conv
        wd = cw.shape[1]
        in_specs += [pl.BlockSpec((None, CARRY_ROWS, wd), lambda i: (i // tiles, 0, 0)),
                     pl.BlockSpec(cw.shape, lambda i: (0, 0))]
        args += [cbuf, cw]
        out_shape.append(jax.ShapeDtypeStruct((bsz, CONV_W - 1, wd), F32))
        out_specs.append(pl.BlockSpec((None, CONV_W - 1, wd), lambda i: (i // tiles, 0, 0)))
        scratch += [pltpu.VMEM((tm + CARRY_ROWS, A_WIDTH), F32) for _ in range(wd // A_WIDTH)]
    return pl.pallas_call(
        functools.partial(_proj_body, groups=tuple(groups), tiles=tiles),
        grid=(m // tm,),
        in_specs=in_specs, out_specs=out_specs, out_shape=out_shape, scratch_shapes=scratch,
        compiler_params=_cparams(("arbitrary",) if conv is not None else ("parallel",)),
        name="proj",
    )(*args)


def _gdn_body(*refs, chunk, rows, fused_conv):
    if fused_conv:
        (q_ref, k_ref, v_ref, za_ref, misc_ref, s0_ref, avec_ref, dtvec_ref, go_ref, cbuf_ref, cw_ref,
         o_ref, sfin_ref, convnew_ref, g_sc, b_sc, s_sc, *conv_sc) = refs
    else:
        (q_ref, k_ref, v_ref, za_ref, misc_ref, s0_ref, avec_ref, dtvec_ref, go_ref,
         o_ref, sfin_ref, g_sc, b_sc, s_sc) = refs
    t = pl.program_id(1)
    c = chunk

    @pl.when(t == 0)
    def _():
        s_sc[...] = s0_ref[0]

    if fused_conv:
        xp_refs, act_refs = conv_sc[:3], conv_sc[3:]

        @pl.when(t == 0)
        def _():
            for part, xp_ref in enumerate(xp_refs):
                xp_ref[0:CARRY_ROWS, :] = cbuf_ref[:, part * A_WIDTH:(part + 1) * A_WIDTH]
        for part, raw_ref in enumerate((q_ref, k_ref, v_ref)):
            cols = slice(part * A_WIDTH, (part + 1) * A_WIDTH)
            act, conv_new = _conv_silu_norm(xp_refs[part], raw_ref[...], cw_ref[:, cols], rows, part)
            convnew_ref[:, cols] = conv_new
            act_refs[part][...] = act
        q_sc, k_sc, v_sc = act_refs
    else:
        q_sc, k_sc, v_sc = q_ref, k_ref, v_ref
    misc = misc_ref[...]
    b_sc[...] = jax.nn.sigmoid(misc)
    g_sc[...] = avec_ref[...] * jax.nn.softplus(misc + dtvec_ref[...])

    ri = lax.broadcasted_iota(jnp.int32, (c, c), 0)
    ci = lax.broadcasted_iota(jnp.int32, (c, c), 1)
    tri = ri >= ci
    strict = ri > ci
    eye_f = (ri == ci).astype(F32)
    tri3 = (lax.broadcasted_iota(jnp.int32, (c, 3 * c), 0)
            >= lax.broadcasted_iota(jnp.int32, (c, 3 * c), 1) % c).astype(BF16)
    go = go_ref[...]
    n_chunks = rows // c

    problems = [(ic, h) for ic in range(n_chunks) for h in range(A_HEADS)]
    gc_all, gc_t = [], []
    for ic in range(n_chunks):
        g = g_sc[ic * c:(ic + 1) * c, :]
        g1 = g.astype(BF16)
        r1 = g - g1.astype(F32)
        g2 = r1.astype(BF16)
        g3 = (r1 - g2.astype(F32)).astype(BF16)
        gc_all.append(_dot(tri3, jnp.concatenate([g1, g2, g3], axis=0)))
    for ic in range(n_chunks):
        gc_t.append(gc_all[ic].T)

    def load(ref, ic, h):
        return ref[ic * c:(ic + 1) * c, h * A_DK:(h + 1) * A_DK]

    col = lambda a, lane: a[:, lane:lane + 1]
    k_bf = [load(k_sc, ic, h).astype(BF16) for ic, h in problems]
    kb = [load(k_sc, ic, h) * col(b_sc[ic * c:(ic + 1) * c, :], MISC_B + h) for ic, h in problems]
    kk = [_dot_nt(kb[i].astype(BF16), k_bf[i]) for i in range(len(problems))]
    qk_raw = [_dot_nt(load(q_sc, ic, h).astype(BF16), k_bf[i]) for i, (ic, h) in enumerate(problems)]
    decay = []
    for ic, h in problems:
        diff = col(gc_all[ic], MISC_A + h) - gc_t[ic][MISC_A + h:MISC_A + h + 1, :]
        decay.append(jnp.where(tri, jnp.exp(jnp.where(tri, diff, 0.0)), 0.0))
    lmat = [jnp.where(strict, kk[i] * decay[i], 0.0) for i in range(len(problems))]
    qk = [(qk_raw[i] * decay[i]).astype(BF16) for i in range(len(problems))]
    x_inv = [eye_f - m for m in lmat]
    p_split = [_split2(m) for m in lmat]
    for _ in range(int(math.log2(c)) - 1):
        p_split = [_split2(_mm_split(ps, ps)) for ps in p_split]
        x_inv = [x + _mm_split(_split2(x), ps) for x, ps in zip(x_inv, p_split)]
    sol = []
    for i, (ic, h) in enumerate(problems):
        egc = jnp.exp(col(gc_all[ic], MISC_A + h))
        beta = col(b_sc[ic * c:(ic + 1) * c, :], MISC_B + h)
        rhs = jnp.concatenate([load(v_sc, ic, h) * beta, kb[i] * egc], axis=-1).astype(BF16)
        sol.append(_mm_split(_split2(x_inv[i]), (rhs, None)))
    pre = []
    for i, (ic, h) in enumerate(problems):
        gc = col(gc_all[ic], MISC_A + h)
        g_last = gc[c - 1:c, :]
        k_dec_t = (load(k_sc, ic, h) * jnp.exp(g_last - gc)).T.astype(BF16)
        wq = jnp.concatenate([sol[i][:, A_DV:], load(q_sc, ic, h) * jnp.exp(gc)], axis=0).astype(BF16)
        pre.append((sol[i][:, :A_DV], wq, k_dec_t, jnp.exp(g_last)))

    heads = range(A_HEADS)
    s_cur = [s_sc[h] for h in heads]
    for ic in range(n_chunks):
        pr = [pre[ic * A_HEADS + h] for h in heads]
        ws = [_dot(pr[h][1], s_cur[h].astype(BF16)) for h in heads]
        ub = [(pr[h][0] - ws[h][:c]).astype(BF16) for h in heads]
        o = [ws[h][c:] + _dot(qk[ic * A_HEADS + h], ub[h]) for h in heads]
        s_cur = [s_cur[h] * pr[h][3] + _dot(pr[h][2], ub[h]) for h in heads]
        for h in heads:
            on = o[h] * lax.rsqrt(jnp.mean(o[h] * o[h], axis=-1, keepdims=True) + EPS) * go
            o_ref[ic * c:(ic + 1) * c, h * A_DV:(h + 1) * A_DV] = on * _silu(load(za_ref, ic, h))
    for h in heads:
        s_sc[h] = s_cur[h]
    sfin_ref[0] = s_sc[...]


def _gdn(q, k, v, za, misc, s0, avec, dtvec, g_o, *, bsz, t_len, chunk, rows, conv=None):
    nt = t_len // rows
    w3 = 3 * A_WIDTH
    row_map = lambda b, t: (b * nt + t, 0)
    const2 = lambda b, t: (0, 0)
    state_spec = pl.BlockSpec((1, A_HEADS, A_DK, A_DV), lambda b, t: (b, 0, 0, 0))
    in_specs = [pl.BlockSpec((rows, A_WIDTH), row_map)] * 4 + [
                pl.BlockSpec((rows, MISC_W), row_map),
                state_spec,
                pl.BlockSpec((1, MISC_W), const2),
                pl.BlockSpec((1, MISC_W), const2),
                pl.BlockSpec((1, A_DV), const2)]
    args = [q, k, v, za, misc, s0, avec, dtvec, g_o]
    out_specs = [pl.BlockSpec((rows, A_WIDTH), row_map), state_spec]
    out_shape = [jax.ShapeDtypeStruct((bsz * t_len, A_WIDTH), F32),
                 jax.ShapeDtypeStruct((bsz, A_HEADS, A_DK, A_DV), F32)]
    scratch = [pltpu.VMEM((rows, MISC_W), F32),
               pltpu.VMEM((rows, MISC_W), F32),
               pltpu.VMEM((A_HEADS, A_DK, A_DV), F32)]
    if conv is not None:
        cbuf, cw = conv
        in_specs += [pl.BlockSpec((None, CARRY_ROWS, w3), lambda b, t: (b, 0, 0)),
                     pl.BlockSpec((CONV_W, w3), const2)]
        args += [cbuf, cw]
        out_specs.append(pl.BlockSpec((None, CONV_W - 1, w3), lambda b, t: (b, 0, 0)))
        out_shape.append(jax.ShapeDtypeStruct((bsz, CONV_W - 1, w3), F32))
        scratch += [pltpu.VMEM((rows + CARRY_ROWS, A_WIDTH), F32)] * 3 + [pltpu.VMEM((rows, A_WIDTH), F32)] * 3
    return pl.pallas_call(
        functools.partial(_gdn_body, chunk=chunk, rows=rows, fused_conv=conv is not None),
        grid=(bsz, nt),
        in_specs=in_specs, out_specs=out_specs, out_shape=out_shape, scratch_shapes=scratch,
        compiler_params=_cparams(("parallel", "arbitrary")),
        name="gdn",
    )(*args)


def _topk_threshold(sc_ref, nkb, shape, key_axis, n_sel, n_adm):
    kf = float(n_sel)
    kblk = shape[key_axis]

    if key_axis == 0:
        strip = min(kblk, 32)
        part_shape = (strip, shape[1])
        strip_of = lambda kb, i: sc_ref[kb, i * strip:(i + 1) * strip, :]

        def put_strip(kb, i, v):
            sc_ref[kb, i * strip:(i + 1) * strip, :] = v
    else:
        strip = kblk
        part_shape = shape
        strip_of = lambda kb, i: sc_ref[kb]

        def put_strip(kb, i, v):
            sc_ref[kb] = v
    kpos = lax.broadcasted_iota(jnp.int32, part_shape, key_axis)

    def reduce_blocks(specs):
        def body(kb, accs):
            accs = list(accs)
            for i in range(kblk // strip):
                s = strip_of(kb, i)
                k0 = kb * kblk + i * strip
                accs = [cmb(acc, fn(s, k0)) for acc, (fn, _, _, _, cmb) in zip(accs, specs)]
            return tuple(accs)
        init = tuple(jnp.full(part_shape, i, dt) for _, i, dt, _, _ in specs)
        accs = lax.fori_loop(0, nkb, body, init)
        return [acc if op is None else op(acc, axis=key_axis, keepdims=True)
                for acc, (_, _, _, op, _) in zip(accs, specs)]

    one_if = lambda c: jnp.where(c, 1.0, 0.0)
    vmax, vmin, cpos, czero = reduce_blocks([
        (lambda s, k0: s, -jnp.inf, F32, jnp.max, jnp.maximum),
        (lambda s, k0: jnp.where(s == -jnp.inf, jnp.inf, s), jnp.inf, F32, jnp.min, jnp.minimum),
        (lambda s, k0: one_if(s > 0.0), 0.0, F32, jnp.sum, jnp.add),
        (lambda s, k0: one_if(s == 0.0), 0.0, F32, jnp.sum, jnp.add)])

    long_row = n_adm > kf
    pos_row = long_row & (cpos >= kf)
    zero_row = long_row & (cpos < kf) & (cpos + czero >= kf)
    neg_row = long_row & (cpos + czero < kf)
    lo0 = jnp.where(pos_row | zero_row, 0.0, vmin - jnp.maximum(1.0, jnp.abs(vmin)))
    cnt_lo0 = jnp.where(pos_row, cpos, jnp.where(zero_row, kf, n_adm))
    hi0 = jnp.where(neg_row, 0.0, vmax)
    cnt_hi0 = jnp.where(neg_row, cpos, 0.0)
    need_zero = jnp.where(zero_row, kf - cpos, 0.0)

    def narrow(lo, hi, cnt_lo, cnt_hi):
        fracs = [(t + 1.0) / (BISECT_PROBES + 1.0) for t in range(BISECT_PROBES)]
        mids = [lo * (1.0 - f) + hi * f for f in fracs]
        for t in range(1, BISECT_PROBES):
            mids[t] = jnp.maximum(mids[t], mids[t - 1])
        codes = [sum(COUNT_RADIX ** u for u in range(t + 1)) for t in range(BISECT_PROBES)]

        def encode(s, k0):
            e = jnp.zeros(s.shape, jnp.int32)
            for mid, code in zip(mids, codes):
                e = jnp.where(s > mid, code, e)
            return e
        (packed,) = reduce_blocks([(encode, 0, jnp.int32, None, jnp.add)])
        cnts = []
        for t in range(BISECT_PROBES):
            digit = packed & (COUNT_RADIX - 1) if t < BISECT_PROBES - 1 else packed
            packed = packed >> COUNT_RADIX.bit_length() - 1
            cnts.append(jnp.sum(digit.astype(F32), axis=key_axis, keepdims=True))
        lo_n, cnt_lo_n, hi_n, cnt_hi_n = lo, cnt_lo, hi, cnt_hi
        for mid, cm in zip(mids, cnts):
            up = cm >= kf
            lo_n = jnp.where(up, mid, lo_n)
            cnt_lo_n = jnp.where(up, cm, cnt_lo_n)
        for mid, cm in zip(mids[::-1], cnts[::-1]):
            dn = cm < kf
            hi_n = jnp.where(dn, mid, hi_n)
            cnt_hi_n = jnp.where(dn, cm, cnt_hi_n)
        return lo_n, hi_n, cnt_lo_n, cnt_hi_n

    def unresolved(cnt_lo):
        return (jnp.max(cnt_lo) > kf).astype(jnp.int32)

    blind = jnp.where(unresolved(cnt_lo0) > 0, BISECT_BLIND_ITERS, 0)
    state = lax.fori_loop(0, blind, lambda i, st: narrow(*st), (lo0, hi0, cnt_lo0, cnt_hi0))

    def cond(st):
        return (st[0] < BISECT_MAX_ITERS) & (st[1] > 0)

    def body(st):
        new = narrow(*st[2:])
        return (st[0] + 1, unresolved(new[2])) + new

    _, _, lo, hi, cnt_lo, cnt_hi = lax.while_loop(
        cond, body, (blind, unresolved(state[2])) + state)

    fix = cnt_lo > kf
    need0 = jnp.where(fix, kf - cnt_hi, 0.0)
    big_idx = 2 ** 30

    def fix_cond(need):
        return jnp.max(need) > 0.0

    def fix_body(need):
        def in_cluster(s):
            return (s > lo) & (s <= hi)
        (mval,) = reduce_blocks([(lambda s, k0: jnp.where(in_cluster(s), s, -jnp.inf), -jnp.inf, F32,
                                  jnp.max, jnp.maximum)])
        (idx,) = reduce_blocks(
            [(lambda s, k0: jnp.where(in_cluster(s) & (s == mval), kpos + k0, big_idx),
              big_idx, jnp.int32, jnp.min, jnp.minimum)])
        active = need > 0.0

        def promote(kb, carry):
            for i in range(kblk // strip):
                s = strip_of(kb, i)
                put_strip(kb, i, jnp.where(active & ((kpos + kb * kblk + i * strip) == idx), jnp.inf, s))
            return carry
        lax.fori_loop(0, nkb, promote, 0)
        return jnp.where(active, need - 1.0, need)

    lax.while_loop(fix_cond, fix_body, need0)

    @pl.when(jnp.max(need_zero) > 0.0)
    def _():
        r = lax.broadcasted_iota(jnp.int32, (kblk, kblk), 0)
        c = lax.broadcasted_iota(jnp.int32, (kblk, kblk), 1)
        tri = ((r >= c) if key_axis == 0 else (r <= c)).astype(BF16)

        def tie_blk(kb, seen):
            s = sc_ref[kb]
            z = (s == 0.0) & zero_row
            zb = one_if(z).astype(BF16)
            if key_axis == 0:
                rank = _dot(tri, zb) + seen
                last = rank[kblk - 1:kblk, :]
            else:
                rank = _dot(zb, tri) + seen
                last = rank[:, kblk - 1:kblk]
            sc_ref[kb] = jnp.where(z & (rank <= need_zero), jnp.inf, s)
            return last
        lax.fori_loop(0, nkb, tie_blk, jnp.zeros_like(need_zero))

    return jnp.where(fix, hi, lo)


def _dsa_body(qb_ref, qi_ref, miscq_ref, zb_ref, k_ref, vt_ref, misck_ref, o_ref, sc_ref, s_ref, *, n_sel, qblk):
    j = pl.program_id(1)
    nkb = ((j + 1) * qblk + KEY_BLOCK - 1) // KEY_BLOCK
    qi = qi_ref[...]
    q_h = [qi[:, h * IDX_HD:(h + 1) * IDX_HD].astype(BF16) for h in range(IDX_HEADS)]
    w_t = miscq_ref[...].T[MISC_WI:MISC_WI + IDX_HEADS, :]
    qchunk = (j * qblk + lax.broadcasted_iota(jnp.int32, (1, qblk), 1)) // CHUNK
    krow = lax.broadcasted_iota(jnp.int32, (KEY_BLOCK, 1), 0)

    def score_blk(kb, carry):
        k0 = pl.multiple_of(kb * KEY_BLOCK, KEY_BLOCK)
        for half in range(KEY_BLOCK // SCORE_STRIP):
            r0 = half * SCORE_STRIP
            ki = misck_ref[pl.ds(k0 + r0, SCORE_STRIP), 0:IDX_HD].astype(BF16)
            acc = jnp.zeros((SCORE_STRIP, qblk), F32)
            for h in range(IDX_HEADS):
                acc = acc + w_t[h:h + 1, :] * jnp.maximum(_dot_nt(ki, q_h[h]), 0.0)
            adm = ((k0 + r0 + krow[:SCORE_STRIP]) // CHUNK) <= qchunk
            sc_ref[kb, r0:r0 + SCORE_STRIP, :] = jnp.where(adm, acc, -jnp.inf)
        return carry

    npairs = (nkb + 1) // 2

    def pair_loop(body, init):
        return lax.fori_loop(0, npairs, lambda i, c: body(2 * i + 1, body(2 * i, c)), init)

    pair_loop(score_blk, 0)

    n_adm = ((qchunk + 1) * CHUNK).astype(F32)
    thr = _topk_threshold(sc_ref, nkb, (KEY_BLOCK, qblk), 0, n_sel, n_adm)

    qb = qb_ref[...] * (B_HD ** -0.5 * math.log2(math.e))
    qs = jnp.concatenate([jnp.where(_lane_mask(B_WIDTH, h * B_HD, (h + 1) * B_HD), qb, 0.0)
                          for h in range(B_HEADS)], axis=0).astype(BF16)
    wide = B_HEADS * qblk

    def fold(acc, x, combine):
        for i in range(x.shape[0] // 8):
            acc = combine(acc, x[8 * i:8 * (i + 1)])
        return acc

    def logits_blk(kb, m8):
        k0 = pl.multiple_of(kb * KEY_BLOCK, KEY_BLOCK)
        s = _dot_nt(k_ref[pl.ds(k0, KEY_BLOCK), :].astype(BF16), qs)
        sel = sc_ref[kb] > thr
        s = jnp.where(jnp.concatenate([sel] * B_HEADS, axis=1), s, NEG_BIG)
        s_ref[kb] = s
        return fold(m8, s, jnp.maximum)

    m8 = pair_loop(logits_blk, jnp.full((8, wide), NEG_BIG, F32))
    m = jnp.max(m8, axis=0, keepdims=True)

    def pv_blk(kb, carry):
        l8, accs = carry
        p = jnp.exp2(s_ref[kb] - m)
        pb = p.astype(BF16)
        k0 = pl.multiple_of(kb * KEY_BLOCK, KEY_BLOCK)
        vt = vt_ref[:, pl.ds(k0, KEY_BLOCK)].astype(BF16)
        accs = tuple(accs[h] + _dot(vt[h * B_HD:(h + 1) * B_HD, :], pb[:, h * qblk:(h + 1) * qblk])
                     for h in range(B_HEADS))
        return fold(l8, p, jnp.add), accs

    l8, accs = pair_loop(
        pv_blk, (jnp.zeros((8, wide), F32), tuple(jnp.zeros((B_HD, qblk), F32) for _ in range(B_HEADS))))
    l = jnp.sum(l8, axis=0, keepdims=True)
    o_t = jnp.concatenate([accs[h] / l[:, h * qblk:(h + 1) * qblk] for h in range(B_HEADS)], axis=0)
    o_ref[...] = o_t.T * _silu(zb_ref[...])


def _dsa_prompt(qb, qi, misc, zb, kb, vt, *, bsz, t_len, n_sel, qblk):
    nq = t_len // qblk
    nkb = t_len // KEY_BLOCK
    assert nkb * (KEY_BLOCK // 8) < COUNT_RADIX and nkb % 2 == 0
    qmap = lambda b, j: (b * nq + j, 0)
    kmap = lambda b, j: (b, 0)
    return pl.pallas_call(
        functools.partial(_dsa_body, n_sel=n_sel, qblk=qblk),
        grid=(bsz, nq),
        in_specs=[pl.BlockSpec((qblk, B_WIDTH), qmap),
                  pl.BlockSpec((qblk, IDX_HEADS * IDX_HD), qmap),
                  pl.BlockSpec((qblk, MISC_W), qmap),
                  pl.BlockSpec((qblk, B_WIDTH), qmap),
                  pl.BlockSpec((t_len, B_WIDTH), kmap),
                  pl.BlockSpec((None, B_WIDTH, t_len), lambda b, j: (b, 0, 0)),
                  pl.BlockSpec((t_len, MISC_W), kmap)],
        out_specs=pl.BlockSpec((qblk, B_WIDTH), qmap),
        out_shape=jax.ShapeDtypeStruct((bsz * t_len, B_WIDTH), F32),
        scratch_shapes=[pltpu.VMEM((nkb, KEY_BLOCK, qblk), F32),
                        pltpu.VMEM((nkb, KEY_BLOCK, B_HEADS * qblk), F32)],
        compiler_params=_cparams(("parallel", "arbitrary")),
        name="dsa_prompt",
    )(qb, qi, misc, zb, kb, vt, misc)


def _dsa_dec_body(qb_ref, qi_ref, misc_ref, zb_ref, kn_ref, vn_ref, ck_ref, cv_ref, cki_ref, o_ref,
                  sc_ref, thr_ref, m_ref, l_ref, acc_ref, *, p_len, tq, n_sel, kstep):
    step = pl.program_id(1)
    nsteps = p_len // kstep
    sub = kstep // KEY_BLOCK
    npast = p_len // KEY_BLOCK
    nkb = npast + 1
    pad_rows = KEY_BLOCK - tq
    rows = B_HEADS * tq

    @pl.when(step == 0)
    def _():
        qi = qi_ref[...]
        misc = misc_ref[...]
        q2 = jnp.concatenate([qi[:, h * IDX_HD:(h + 1) * IDX_HD] for h in range(IDX_HEADS)],
                             axis=0).astype(BF16)
        w_h = [misc[:, MISC_WI + h:MISC_WI + h + 1] for h in range(IDX_HEADS)]

        def head_sum(lg):
            acc = jnp.zeros((tq, lg.shape[1]), F32)
            for h in range(IDX_HEADS):
                acc = acc + w_h[h] * jnp.maximum(lg[h * tq:(h + 1) * tq, :], 0.0)
            return acc

        for i in range(nsteps):
            sc = head_sum(_dot(q2, cki_ref[:, i * kstep:(i + 1) * kstep].astype(BF16)))
            for u in range(sub):
                sc_ref[i * sub + u] = sc[:, u * KEY_BLOCK:(u + 1) * KEY_BLOCK]
        lane = lax.broadcasted_iota(jnp.int32, (1, KEY_BLOCK), 1)
        ki_new = jnp.concatenate([misc[:, 0:IDX_HD], jnp.zeros((pad_rows, IDX_HD), F32)], axis=0)
        sc_ref[npast] = jnp.where(lane < tq, head_sum(_dot_nt(q2, ki_new.astype(BF16))), -jnp.inf)
        n_adm = jnp.full((tq, 1), float(p_len + tq), F32)
        thr_ref[...] = _topk_threshold(sc_ref, nkb, (tq, KEY_BLOCK), 1, n_sel, n_adm)
        m_ref[...] = jnp.full((rows, 1), NEG_BIG, F32)
        l_ref[...] = jnp.zeros((rows, 1), F32)
        acc_ref[...] = jnp.zeros((rows, B_HD), F32)

    qb = qb_ref[...] * (B_HD ** -0.5 * math.log2(math.e))
    q_h = [qb[:, h * B_HD:(h + 1) * B_HD].astype(BF16) for h in range(B_HEADS)]
    thr = thr_ref[...]

    def fold_in(logits_of, pv_of, sel):
        s = jnp.concatenate([jnp.where(sel, logits_of(h), NEG_BIG) for h in range(B_HEADS)], axis=0)
        m_old = m_ref[...]
        m_new = jnp.maximum(m_old, jnp.max(s, axis=-1, keepdims=True))
        alpha = jnp.exp2(m_old - m_new)
        p = jnp.exp2(s - m_new)
        pb = p.astype(BF16)
        pv = jnp.concatenate([pv_of(h, pb[h * tq:(h + 1) * tq, :]) for h in range(B_HEADS)], axis=0)
        l_ref[...] = alpha * l_ref[...] + jnp.sum(p, axis=-1, keepdims=True)
        acc_ref[...] = alpha * acc_ref[...] + pv
        m_ref[...] = m_new

    sel = jnp.concatenate([sc_ref[step * sub + u] > thr for u in range(sub)], axis=1)
    fold_in(lambda h: _dot(q_h[h], ck_ref[h].astype(BF16)),
            lambda h, p: _dot_nt(p, cv_ref[h].astype(BF16)), sel)

    @pl.when(step == nsteps - 1)
    def _():
        zpad = jnp.zeros((pad_rows, B_WIDTH), F32)
        kn = jnp.concatenate([kn_ref[...], zpad], axis=0).astype(BF16)
        vn = jnp.concatenate([vn_ref[...], zpad], axis=0).astype(BF16)
        fold_in(lambda h: _dot_nt(q_h[h], kn[:, h * B_HD:(h + 1) * B_HD]),
                lambda h, p: _dot(p, vn[:, h * B_HD:(h + 1) * B_HD]), sc_ref[npast] > thr)
        res = acc_ref[...] / l_ref[...]
        out = jnp.concatenate([res[h * tq:(h + 1) * tq, :] for h in range(B_HEADS)], axis=1)
        o_ref[...] = out * _silu(zb_ref[...])


def _dsa_decode(qb, qi, misc, zb, kb, vb, ck, cv, cki, *, bsz, t_len, p_len, n_sel):
    kstep = DEC_KEYS_PER_STEP
    assert p_len // KEY_BLOCK + 1 < COUNT_RADIX and p_len % kstep == 0
    rows = B_HEADS * t_len
    qmap = lambda b, s: (b, 0)
    cmap = lambda b, s: (b, 0, 0, s)
    ck, cv = (a.transpose(0, 2, 3, 1) for a in (ck, cv))
    cki = cki.transpose(0, 2, 1)
    return pl.pallas_call(
        functools.partial(_dsa_dec_body, p_len=p_len, tq=t_len, n_sel=n_sel, kstep=kstep),
        grid=(bsz, p_len // kstep),
        in_specs=[pl.BlockSpec((t_len, B_WIDTH), qmap),
                  pl.BlockSpec((t_len, IDX_HEADS * IDX_HD), qmap),
                  pl.BlockSpec((t_len, MISC_W), qmap),
                  pl.BlockSpec((t_len, B_WIDTH), qmap),
                  pl.BlockSpec((t_len, B_WIDTH), qmap),
                  pl.BlockSpec((t_len, B_WIDTH), qmap),
                  pl.BlockSpec((None, B_HEADS, B_HD, kstep), cmap),
                  pl.BlockSpec((None, B_HEADS, B_HD, kstep), cmap),
                  pl.BlockSpec((None, IDX_HD, p_len), lambda b, s: (b, 0, 0))],
        out_specs=pl.BlockSpec((t_len, B_WIDTH), qmap),
        out_shape=jax.ShapeDtypeStruct((bsz * t_len, B_WIDTH), F32),
        scratch_shapes=[pltpu.VMEM((p_len // KEY_BLOCK + 1, t_len, KEY_BLOCK), F32),
                        pltpu.VMEM((t_len, 1), F32),
                        pltpu.VMEM((rows, 1), F32),
                        pltpu.VMEM((rows, 1), F32),
                        pltpu.VMEM((rows, B_HD), F32)],
        compiler_params=_cparams(("parallel", "arbitrary")),
        name="dsa_decode",
    )(qb, qi, misc, zb, kb, vb, ck, cv, cki)


def _tail_body(x_ref, oa_ref, ob_ref, qm_ref, zm_ref, mk_ref, mv_ref, w_ref, y_ref):
    qm = qm_ref[...] * (M_HD ** -0.5)
    mk = mk_ref[...].reshape(N_MEM, M_WIDTH).astype(BF16)
    mv = mv_ref[...].reshape(N_MEM, M_WIDTH).astype(BF16)
    acc = _dot(oa_ref[...].astype(BF16), w_ref[0:A_WIDTH, :])
    acc = acc + _dot(ob_ref[...].astype(BF16), w_ref[A_WIDTH:A_WIDTH + B_WIDTH, :])
    om = jnp.zeros(qm.shape, F32)
    for h in range(M_HEADS):
        mask = _lane_mask(M_WIDTH, h * M_HD, (h + 1) * M_HD)
        s = _dot_nt(jnp.where(mask, qm, 0.0).astype(BF16), mk)
        p = jnp.exp(s - jnp.max(s, axis=-1, keepdims=True))
        l = jnp.sum(p, axis=-1, keepdims=True)
        om = om + jnp.where(mask, _dot(p.astype(BF16), mv) / l, 0.0)
    om = om * _silu(zm_ref[...])
    acc = acc + _dot(om.astype(BF16), w_ref[A_WIDTH + B_WIDTH:, :])
    y_ref[...] = x_ref[...] + acc


def _tail(x, oa, ob, qm, zm, mk, mv, w, *, bsz, t_len, tq):
    m, d = x.shape
    nq = t_len // tq
    qmap = lambda b, j: (b * nq + j, 0)
    mem_spec = pl.BlockSpec((None,) + mk.shape[1:], lambda b, j: (b,) + (0,) * (mk.ndim - 1))
    return pl.pallas_call(
        _tail_body,
        grid=(bsz, nq),
        in_specs=[pl.BlockSpec((tq, d), qmap),
                  pl.BlockSpec((tq, A_WIDTH), qmap),
                  pl.BlockSpec((tq, B_WIDTH), qmap),
                  pl.BlockSpec((tq, M_WIDTH), qmap),
                  pl.BlockSpec((tq, M_WIDTH), qmap),
                  mem_spec, mem_spec,
                  pl.BlockSpec(w.shape, lambda b, j: (0, 0))],
        out_specs=pl.BlockSpec((tq, d), qmap),
        out_shape=jax.ShapeDtypeStruct((m, d), F32),
        compiler_params=_cparams(("parallel", "parallel")),
        name="tail",
    )(x, oa, ob, qm, zm, mk, mv, w)


def _prep_weights(g_in, w_in, conv_w, a_log, dt_bias, g_o, g_qb, g_kb, g_ki, g_qm, w_out):
    offs = np.concatenate([[0], np.cumsum(IN_SPLITS)])
    col = lambda i: w_in[:, int(offs[i]):int(offs[i + 1])]
    (w_qkv, w_za, w_ba, w_aa, w_qb, w_kb, w_vb, w_zb, w_qi, w_ki, w_wi, w_qm, w_zm) = [col(i) for i in range(13)]
    d = w_in.shape[0]
    w_misc = jnp.concatenate(
        [w_ki, w_wi, w_ba, w_aa, jnp.zeros((d, MISC_W - MISC_A - A_HEADS), w_in.dtype)], axis=1)
    w_all = jnp.concatenate([w_za, w_qb, w_kb, w_vb, w_zb, w_qi, w_qm, w_zm, w_misc, w_qkv],
                            axis=1).astype(BF16)
    groups = [(A_WIDTH, False, ""), (B_WIDTH, True, ""),
              (B_WIDTH, True, "t"), (B_WIDTH, False, "t"), (B_WIDTH, False, ""),
              (IDX_HEADS * IDX_HD, False, ""), (M_WIDTH, True, ""), (M_WIDTH, False, ""),
              (MISC_W, True, ""),
              (A_WIDTH, False, "conv0"), (A_WIDTH, False, "conv1"), (A_WIDTH, False, "conv2")]
    bd64 = _group_mean_matrix(B_WIDTH, B_HD, B_WIDTH)
    ones256 = jnp.ones((1, B_WIDTH), F32)
    misc_gain = jnp.concatenate([g_ki, jnp.full((IDX_HEADS,), IDX_SCALE, F32),
                                 jnp.ones((MISC_W - MISC_B,), F32)])[None, :]
    misc_nm = (jnp.arange(MISC_W) < IDX_HD).astype(F32)[None, :]
    aux = [bd64, jnp.tile(g_qb, B_HEADS)[None, :], ones256,
           bd64, jnp.tile(g_kb, B_HEADS)[None, :], ones256,
           bd64, jnp.tile(g_qm, M_HEADS)[None, :], ones256,
           _group_mean_matrix(MISC_W, IDX_HD, IDX_HD), misc_gain, misc_nm]
    pad_a = lambda v: jnp.zeros((1, MISC_W), F32).at[0, MISC_A:MISC_A + A_HEADS].set(v)
    return dict(g_in=g_in[None, :], w_all=w_all, groups=groups, aux=aux, conv_w=conv_w,
                avec=pad_a(-jnp.exp(a_log)), dtvec=pad_a(dt_bias), g_o=g_o[None, :],
                w_out=w_out.astype(BF16))


def _layer(x, mem_k, mem_v, conv_buf, s0, past, gdn_chunk, gdn_rows, wp):
    bsz, t_len, d = x.shape
    m = bsz * t_len
    x2 = x.reshape(m, d)
    tm = min(PROJ_ROWS, m)
    heads_last = lambda a_t: a_t.reshape(bsz, B_HEADS, B_HD, t_len).transpose(0, 3, 1, 2)
    cbuf = jnp.concatenate([jnp.zeros((bsz, CARRY_ROWS - (CONV_W - 1), 3 * A_WIDTH), F32), conv_buf], axis=1)
    conv = (cbuf, wp["conv_w"])
    gdn_args = dict(bsz=bsz, t_len=t_len, chunk=gdn_chunk, rows=gdn_rows)
    if past is None:
        (za, qb, kb, vb, zb, qi, qm, zm, misc, qa, ka, va, kt, vt, conv_new) = _proj(
            x2, wp["g_in"], wp["w_all"], wp["groups"], wp["aux"], tm, t_len, conv=conv)
        k_new, v_new = heads_last(kt), heads_last(vt)
        oa, s_new = _gdn(qa, ka, va, za, misc, s0, wp["avec"], wp["dtvec"], wp["g_o"], **gdn_args)
    else:
        groups = [(wd, normed, "") for wd, normed, _ in wp["groups"]]
        (za, qb, kb, vb, zb, qi, qm, zm, misc, qa, ka, va) = _proj(x2, wp["g_in"], wp["w_all"], groups,
                                                                   wp["aux"], tm, m)
        k_new, v_new = (a.reshape(bsz, t_len, B_HEADS, B_HD) for a in (kb, vb))
        oa, s_new, conv_new = _gdn(qa, ka, va, za, misc, s0, wp["avec"], wp["dtvec"], wp["g_o"], conv=conv,
                                   **gdn_args)
    if past is None:
        n_sel = min(TOPK_MAX, t_len // 4)
        ob = _dsa_prompt(qb, qi, misc, zb, kb, vt, bsz=bsz, t_len=t_len, n_sel=n_sel, qblk=DSA_QBLK)
    else:
        ck, cv, cki = past
        p_len = ck.shape[1]
        assert (p_len + t_len - 1) // CHUNK <= p_len // CHUNK and p_len % KEY_BLOCK == 0
        n_sel = min(TOPK_MAX, (p_len + t_len) // 4)
        ob = _dsa_decode(qb, qi, misc, zb, kb, vb, ck, cv, cki, bsz=bsz, t_len=t_len, p_len=p_len,
                         n_sel=n_sel)
    y = _tail(x2, oa, ob, qm, zm, mem_k, mem_v, wp["w_out"], bsz=bsz, t_len=t_len, tq=min(TAIL_ROWS, t_len))
    return (y.reshape(bsz, t_len, d), conv_new, s_new, k_new, v_new,
            misc[:, :IDX_HD].reshape(bsz, t_len, IDX_HD))


def _memory_kv(mem, g_mem, w_mem_kv, g_km):
    bsz, n_mem, d = mem.shape
    groups = [(M_WIDTH, True, "t"), (M_WIDTH, False, "t")]
    aux = [_group_mean_matrix(M_WIDTH, M_HD, M_WIDTH), jnp.tile(g_km, M_HEADS)[None, :],
           jnp.ones((1, M_WIDTH), F32)]
    mk, mv, mk_t, mv_t = _proj(mem.reshape(bsz * n_mem, d), g_mem[None, :], w_mem_kv.astype(BF16),
                               groups, aux, n_mem, n_mem)
    heads_last = lambda a_t: a_t.reshape(bsz, M_HEADS, M_HD, n_mem).transpose(0, 3, 1, 2)
    return (mk.reshape(bsz, n_mem, M_WIDTH), mv.reshape(bsz, n_mem, M_WIDTH), heads_last(mk_t), heads_last(mv_t))


def kernel(x_prompt, x_sample, state_conv_A, state_ssm_A, cache_k_B, cache_v_B, cache_kidx_B, cache_mem_k,
           cache_mem_v, mem_prompt, g_in, w_in, conv_w_A, a_log_A, dt_bias_A, g_o_A, g_q_B, g_k_B, g_kidx_B,
           g_mem, w_mem_kv, g_q_M, g_k_M, w_out):
    depth = w_in.shape[0]
    assert depth == 1
    l = 0
    bp, t_p, _ = x_prompt.shape
    bs, t_s, _ = x_sample.shape
    wp = _prep_weights(g_in[l], w_in[l], conv_w_A[l], a_log_A[l], dt_bias_A[l], g_o_A[l], g_q_B[l], g_k_B[l],
                       g_kidx_B[l], g_q_M[l], w_out[l])
    mk, mv, mk_heads, mv_heads = _memory_kv(mem_prompt, g_mem[l], w_mem_kv[l], g_k_M[l])
    zero_conv = jnp.zeros((bp, CONV_W - 1, 3 * A_WIDTH), F32)
    zero_ssm = jnp.zeros((bp, A_HEADS, A_DK, A_DV), F32)
    yp, c1, s1, k1, v1, ki1 = _layer(x_prompt, mk, mv, zero_conv, zero_ssm, None, CHUNK, 4 * CHUNK, wp)
    ys, c2, s2, k2, v2, ki2 = _layer(
        x_sample, cache_mem_k[l], cache_mem_v[l],
        state_conv_A[l], state_ssm_A[l], (cache_k_B[l], cache_v_B[l], cache_kidx_B[l]), t_s, t_s, wp)
    st = lambda a: a[None]
    return (yp, ys, st(c1), st(s1), st(k1), st(v1), st(ki1),
            st(mk_heads), st(mv_heads),
            st(c2), st(s2), st(k2), st(v2), st(ki2))
```

```python
import functools
import math

import numpy as np
import jax
import jax.numpy as jnp
from jax import lax
from jax.experimental import pallas as pl
from jax.experimental.pallas import tpu as pltpu

F32 = jnp.float32
BF16 = jnp.bfloat16
HIGHEST = lax.Precision.HIGHEST

D_MODEL = 1024
CHUNK = 64
A_HEADS = 4
A_DK = 128
A_DV = 128
A_WIDTH = A_HEADS * A_DV
CONV_W = 4
B_HEADS = 4
B_HD = 64
B_WIDTH = B_HEADS * B_HD
IDX_HEADS = 8
IDX_HD = 32
IDX_SCALE = (IDX_HEADS ** -0.5) * (IDX_HD ** -0.5)
TOPK_MAX = 256
Q_BLOCK = 128
N_MEM = 256
M_HEADS = 4
M_HD = 64
M_WIDTH = M_HEADS * M_HD
EPS = 1e-6
IN_SPLITS = (3 * A_WIDTH, A_WIDTH, A_HEADS, A_HEADS,
             B_WIDTH, B_WIDTH, B_WIDTH, B_WIDTH, IDX_HEADS * IDX_HD, IDX_HD, IDX_HEADS,
             M_WIDTH, M_WIDTH)

LANES = 128
KEY_BLOCK = 256
DSA_QBLK = 512
GDN_CHUNKS_PER_STEP = 8
TAIL_ROWS = 512
PROJ_ROWS = 512
DEC_KEYS_PER_STEP = 4096
SCORE_STRIP = 128
MISC_W = LANES
MISC_WI = IDX_HD
MISC_B = IDX_HD + IDX_HEADS
MISC_A = MISC_B + A_HEADS
VMEM_LIMIT = 48 * 1024 * 1024
NEG_BIG = -1e30
BISECT_PROBES = 3
COUNT_RADIX = 1024
BISECT_BLIND_ITERS = 7
BISECT_MAX_ITERS = 20


def _cparams(sem):
    return pltpu.CompilerParams(dimension_semantics=sem, vmem_limit_bytes=VMEM_LIMIT)


def _dot(a, b):
    return jnp.dot(a, b, preferred_element_type=F32)


def _dot_nt(a, b, precision=None):
    return lax.dot_general(a, b, (((1,), (1,)), ((), ())), precision=precision,
                           preferred_element_type=F32)


def _split2(x):
    hi = x.astype(BF16)
    return hi, (x - hi.astype(F32)).astype(BF16)


def _mm_split(a, b):
    ah, al = a
    bh, _ = b
    n = ah.shape[0]
    t = _dot(jnp.concatenate([ah, al], axis=0), bh)
    return t[:n] + t[n:]


def _silu(x):
    return x * jax.nn.sigmoid(x)


def _lane_mask(width, lo, hi):
    lane = lax.broadcasted_iota(jnp.int32, (1, width), 1)
    return (lane >= lo) & (lane < hi)


CARRY_ROWS = 8


def _conv_silu_norm(xp_ref, raw, cw, rows, part):
    first = CARRY_ROWS - (CONV_W - 1)
    xp_ref[CARRY_ROWS:CARRY_ROWS + rows, :] = raw
    y = xp_ref[first:first + rows, :] * cw[0:1, :]
    for j in range(1, CONV_W):
        y = y + xp_ref[first + j:first + j + rows, :] * cw[j:j + 1, :]
    y = _silu(y)
    conv_new = xp_ref[rows + first:rows + CARRY_ROWS, :]
    xp_ref[0:CARRY_ROWS, :] = xp_ref[rows:rows + CARRY_ROWS, :]
    if part == 2:
        return y, conv_new
    scale = A_DK ** -0.5 if part == 0 else 1.0
    heads = []
    for h in range(A_HEADS):
        yh = y[:, h * A_DK:(h + 1) * A_DK]
        heads.append(yh * (lax.rsqrt(jnp.sum(yh * yh, axis=-1, keepdims=True) + EPS) * scale))
    return jnp.concatenate(heads, axis=1), conv_new


def _conv_part(extra):
    return int(extra[len("conv"):]) if extra.startswith("conv") else None


def _proj_body(*refs, groups, tiles):
    x_ref, g_ref, w_ref = refs[0], refs[1], refs[2]
    n_aux = 3 * sum(1 for _, normed, _ in groups if normed)
    aux = refs[3:3 + n_aux]
    pos = 3 + n_aux
    n_conv = sum(1 for _, _, extra in groups if _conv_part(extra) is not None)
    if n_conv:
        cbuf_ref, cw_ref = refs[pos], refs[pos + 1]
        pos += 2
    outs = refs[pos:pos + len(groups)]
    pos += len(groups)
    n_t = sum(1 for _, _, extra in groups if extra == "t")
    outs_t = list(refs[pos:pos + n_t])
    pos += n_t
    if n_conv:
        convnew_ref = refs[pos]
        xp_refs = refs[pos + 1:pos + 1 + n_conv]

        @pl.when(pl.program_id(0) % tiles == 0)
        def _():
            for part, xp_ref in enumerate(xp_refs):
                xp_ref[0:CARRY_ROWS, :] = cbuf_ref[:, part * A_WIDTH:(part + 1) * A_WIDTH]
    x = x_ref[...]
    r = lax.rsqrt(jnp.mean(x * x, axis=-1, keepdims=True) + EPS)
    hb = (x * g_ref[...]).astype(BF16)
    off = 0
    ai = 0
    for (width, normed, extra), o_ref in zip(groups, outs):
        y = _dot(hb, w_ref[:, off:off + width]) * r
        part = _conv_part(extra)
        if part is not None:
            cols = slice(part * A_WIDTH, (part + 1) * A_WIDTH)
            y, conv_new = _conv_silu_norm(xp_refs[part], y, cw_ref[:, cols], y.shape[0], part)
            convnew_ref[:, cols] = conv_new
        if normed:
            bd_ref, gain_ref, nm_ref = aux[ai], aux[ai + 1], aux[ai + 2]
            ai += 3
            sq = y * y
            hi = sq.astype(BF16)
            lo = (sq - hi.astype(F32)).astype(BF16)
            ms = _dot(hi, bd_ref[...]) + _dot(lo, bd_ref[...])
            scale = jnp.where(nm_ref[...] > 0.0, lax.rsqrt(ms + EPS), 1.0)
            y = y * scale * gain_ref[...]
        o_ref[...] = y
        if extra == "t":
            outs_t.pop(0)[...] = y.T
        off += width


def _group_mean_matrix(width, group, n_lanes):
    i = np.arange(width)
    m = ((i[:, None] // group) == (i[None, :] // group)) & (i[:, None] < n_lanes) & (i[None, :] < n_lanes)
    return jnp.asarray(m.astype(np.float32) / group, dtype=BF16)


def _proj(x, g, w, groups, aux, tm, t_len, conv=None):
    m, d = x.shape
    nw = w.shape[1]
    tiles = t_len // tm
    bsz = m // t_len
    in_specs = [pl.BlockSpec((tm, d), lambda i: (i, 0)),
                pl.BlockSpec((1, d), lambda i: (0, 0)),
                pl.BlockSpec((d, nw), lambda i: (0, 0))]
    for a in aux:
        in_specs.append(pl.BlockSpec(a.shape, lambda i: (0, 0)))
    out_shape = [jax.ShapeDtypeStruct((m, wd), F32) for wd, _, _ in groups]
    out_specs = [pl.BlockSpec((tm, wd), lambda i: (i, 0)) for wd, _, _ in groups]
    for wd, _, extra in groups:
        if extra == "t":
            out_shape.append(jax.ShapeDtypeStruct((bsz, wd, t_len), F32))
            out_specs.append(pl.BlockSpec((None, wd, tm), lambda i: (i // tiles, 0, i % tiles)))
    args = [x, g, w, *aux]
    scratch = []
    if conv is not None:
        cbuf, cw = conv
        wd = cw.shape[1]
        in_specs += [pl.BlockSpec((None, CARRY_ROWS, wd), lambda i: (i // tiles, 0, 0)),
                     pl.BlockSpec(cw.shape, lambda i: (0, 0))]
        args += [cbuf, cw]
        out_shape.append(jax.ShapeDtypeStruct((bsz, CONV_W - 1, wd), F32))
        out_specs.append(pl.BlockSpec((None, CONV_W - 1, wd), lambda i: (i // tiles, 0, 0)))
        scratch += [pltpu.VMEM((tm + CARRY_ROWS, A_WIDTH), F32) for _ in range(wd // A_WIDTH)]
    return pl.pallas_call(
        functools.partial(_proj_body, groups=tuple(groups), tiles=tiles),
        grid=(m // tm,),
        in_specs=in_specs, out_specs=out_specs, out_shape=out_shape, scratch_shapes=scratch,
        compiler_params=_cparams(("arbitrary",) if conv is not None else ("parallel",)),
        name="proj",
    )(*args)


def _gdn_body(*refs, chunk, rows, fused_conv):
    if fused_conv:
        (q_ref, k_ref, v_ref, za_ref, misc_ref, s0_ref, avec_ref, dtvec_ref, go_ref, cbuf_ref, cw_ref,
         o_ref, sfin_ref, convnew_ref, g_sc, b_sc, s_sc, *conv_sc) = refs
    else:
        (q_ref, k_ref, v_ref, za_ref, misc_ref, s0_ref, avec_ref, dtvec_ref, go_ref,
         o_ref, sfin_ref, g_sc, b_sc, s_sc) = refs
    t = pl.program_id(1)
    c = chunk

    @pl.when(t == 0)
    def _():
        s_sc[...] = s0_ref[0]

    if fused_conv:
        xp_refs, act_refs = conv_sc[:3], conv_sc[3:]

        @pl.when(t == 0)
        def _():
            for part, xp_ref in enumerate(xp_refs):
                xp_ref[0:CARRY_ROWS, :] = cbuf_ref[:, part * A_WIDTH:(part + 1) * A_WIDTH]
        for part, raw_ref in enumerate((q_ref, k_ref, v_ref)):
            cols = slice(part * A_WIDTH, (part + 1) * A_WIDTH)
            act, conv_new = _conv_silu_norm(xp_refs[part], raw_ref[...], cw_ref[:, cols], rows, part)
            convnew_ref[:, cols] = conv_new
            act_refs[part][...] = act
        q_sc, k_sc, v_sc = act_refs
    else:
        q_sc, k_sc, v_sc = q_ref, k_ref, v_ref
    misc = misc_ref[...]
    b_sc[...] = jax.nn.sigmoid(misc)
    g_sc[...] = avec_ref[...] * jax.nn.softplus(misc + dtvec_ref[...])

    ri = lax.broadcasted_iota(jnp.int32, (c, c), 0)
    ci = lax.broadcasted_iota(jnp.int32, (c, c), 1)
    tri = ri >= ci
    strict = ri > ci
    eye_f = (ri == ci).astype(F32)
    tri3 = (lax.broadcasted_iota(jnp.int32, (c, 3 * c), 0)
            >= lax.broadcasted_iota(jnp.int32, (c, 3 * c), 1) % c).astype(BF16)
    go = go_ref[...]
    n_chunks = rows // c

    problems = [(ic, h) for ic in range(n_chunks) for h in range(A_HEADS)]
    gc_all, gc_t = [], []
    for ic in range(n_chunks):
        g = g_sc[ic * c:(ic + 1) * c, :]
        g1 = g.astype(BF16)
        r1 = g - g1.astype(F32)
        g2 = r1.astype(BF16)
        g3 = (r1 - g2.astype(F32)).astype(BF16)
        gc_all.append(_dot(tri3, jnp.concatenate([g1, g2, g3], axis=0)))
    for ic in range(n_chunks):
        gc_t.append(gc_all[ic].T)

    def load(ref, ic, h):
        return ref[ic * c:(ic + 1) * c, h * A_DK:(h + 1) * A_DK]

    col = lambda a, lane: a[:, lane:lane + 1]
    k_bf = [load(k_sc, ic, h).astype(BF16) for ic, h in problems]
    kb = [load(k_sc, ic, h) * col(b_sc[ic * c:(ic + 1) * c, :], MISC_B + h) for ic, h in problems]
    kk = [_dot_nt(kb[i].astype(BF16), k_bf[i]) for i in range(len(problems))]
    qk_raw = [_dot_nt(load(q_sc, ic, h).astype(BF16), k_bf[i]) for i, (ic, h) in enumerate(problems)]
    decay = []
    for ic, h in problems:
        diff = col(gc_all[ic], MISC_A + h) - gc_t[ic][MISC_A + h:MISC_A + h + 1, :]
        decay.append(jnp.where(tri, jnp.exp(jnp.where(tri, diff, 0.0)), 0.0))
    lmat = [jnp.where(strict, kk[i] * decay[i], 0.0) for i in range(len(problems))]
    qk = [(qk_raw[i] * decay[i]).astype(BF16) for i in range(len(problems))]
    x_inv = [eye_f - m for m in lmat]
    p_split = [_split2(m) for m in lmat]
    for _ in range(int(math.log2(c)) - 1):
        p_split = [_split2(_mm_split(ps, ps)) for ps in p_split]
        x_inv = [x + _mm_split(_split2(x), ps) for x, ps in zip(x_inv, p_split)]
    sol = []
    for i, (ic, h) in enumerate(problems):
        egc = jnp.exp(col(gc_all[ic], MISC_A + h))
        beta = col(b_sc[ic * c:(ic + 1) * c, :], MISC_B + h)
        rhs = jnp.concatenate([load(v_sc, ic, h) * beta, kb[i] * egc], axis=-1).astype(BF16)
        sol.append(_mm_split(_split2(x_inv[i]), (rhs, None)))
    pre = []
    for i, (ic, h) in enumerate(problems):
        gc = col(gc_all[ic], MISC_A + h)
        g_last = gc[c - 1:c, :]
        k_dec_t = (load(k_sc, ic, h) * jnp.exp(g_last - gc)).T.astype(BF16)
        wq = jnp.concatenate([sol[i][:, A_DV:], load(q_sc, ic, h) * jnp.exp(gc)], axis=0).astype(BF16)
        pre.append((sol[i][:, :A_DV], wq, k_dec_t, jnp.exp(g_last)))

    heads = range(A_HEADS)
    s_cur = [s_sc[h] for h in heads]
    for ic in range(n_chunks):
        pr = [pre[ic * A_HEADS + h] for h in heads]
        ws = [_dot(pr[h][1], s_cur[h].astype(BF16)) for h in heads]
        ub = [(pr[h][0] - ws[h][:c]).astype(BF16) for h in heads]
        o = [ws[h][c:] + _dot(qk[ic * A_HEADS + h], ub[h]) for h in heads]
        s_cur = [s_cur[h] * pr[h][3] + _dot(pr[h][2], ub[h]) for h in heads]
        for h in heads:
            on = o[h] * lax.rsqrt(jnp.mean(o[h] * o[h], axis=-1, keepdims=True) + EPS) * go
            o_ref[ic * c:(ic + 1) * c, h * A_DV:(h + 1) * A_DV] = on * _silu(load(za_ref, ic, h))
    for h in heads:
        s_sc[h] = s_cur[h]
    sfin_ref[0] = s_sc[...]


def _gdn(q, k, v, za, misc, s0, avec, dtvec, g_o, *, bsz, t_len, chunk, rows, conv=None):
    nt = t_len // rows
    w3 = 3 * A_WIDTH
    row_map = lambda b, t: (b * nt + t, 0)
    const2 = lambda b, t: (0, 0)
    state_spec = pl.BlockSpec((1, A_HEADS, A_DK, A_DV), lambda b, t: (b, 0, 0, 0))
    in_specs = [pl.BlockSpec((rows, A_WIDTH), row_map)] * 4 + [
                pl.BlockSpec((rows, MISC_W), row_map),
                state_spec,
                pl.BlockSpec((1, MISC_W), const2),
                pl.BlockSpec((1, MISC_W), const2),
                pl.BlockSpec((1, A_DV), const2)]
    args = [q, k, v, za, misc, s0, avec, dtvec, g_o]
    out_specs = [pl.BlockSpec((rows, A_WIDTH), row_map), state_spec]
    out_shape = [jax.ShapeDtypeStruct((bsz * t_len, A_WIDTH), F32),
                 jax.ShapeDtypeStruct((bsz, A_HEADS, A_DK, A_DV), F32)]
    scratch = [pltpu.VMEM((rows, MISC_W), F32),
               pltpu.VMEM((rows, MISC_W), F32),
               pltpu.VMEM((A_HEADS, A_DK, A_DV), F32)]
    if conv is not None:
        cbuf, cw = conv
        in_specs += [pl.BlockSpec((None, CARRY_ROWS, w3), lambda b, t: (b, 0, 0)),
                     pl.BlockSpec((CONV_W, w3), const2)]
        args += [cbuf, cw]
        out_specs.append(pl.BlockSpec((None, CONV_W - 1, w3), lambda b, t: (b, 0, 0)))
        out_shape.append(jax.ShapeDtypeStruct((bsz, CONV_W - 1, w3), F32))
        scratch += [pltpu.VMEM((rows + CARRY_ROWS, A_WIDTH), F32)] * 3 + [pltpu.VMEM((rows, A_WIDTH), F32)] * 3
    return pl.pallas_call(
        functools.partial(_gdn_body, chunk=chunk, rows=rows, fused_conv=conv is not None),
        grid=(bsz, nt),
        in_specs=in_specs, out_specs=out_specs, out_shape=out_shape, scratch_shapes=scratch,
        compiler_params=_cparams(("parallel", "arbitrary")),
        name="gdn",
    )(*args)


def _topk_threshold(sc_ref, nkb, shape, key_axis, n_sel, n_adm):
    kf = float(n_sel)
    kblk = shape[key_axis]

    if key_axis == 0:
        strip = min(kblk, 32)
        part_shape = (strip, shape[1])
        strip_of = lambda kb, i: sc_ref[kb, i * strip:(i + 1) * strip, :]

        def put_strip(kb, i, v):
            sc_ref[kb, i * strip:(i + 1) * strip, :] = v
    else:
        strip = kblk
        part_shape = shape
        strip_of = lambda kb, i: sc_ref[kb]

        def put_strip(kb, i, v):
            sc_ref[kb] = v
    kpos = lax.broadcasted_iota(jnp.int32, part_shape, key_axis)

    def reduce_blocks(specs):
        def body(kb, accs):
            accs = list(accs)
            for i in range(kblk // strip):
                s = strip_of(kb, i)
                k0 = kb * kblk + i * strip
                accs = [cmb(acc, fn(s, k0)) for acc, (fn, _, _, _, cmb) in zip(accs, specs)]
            return tuple(accs)
        init = tuple(jnp.full(part_shape, i, dt) for _, i, dt, _, _ in specs)
        accs = lax.fori_loop(0, nkb, body, init)
        return [acc if op is None else op(acc, axis=key_axis, keepdims=True)
                for acc, (_, _, _, op, _) in zip(accs, specs)]

    one_if = lambda c: jnp.where(c, 1.0, 0.0)
    vmax, vmin, cpos, czero = reduce_blocks([
        (lambda s, k0: s, -jnp.inf, F32, jnp.max, jnp.maximum),
        (lambda s, k0: jnp.where(s == -jnp.inf, jnp.inf, s), jnp.inf, F32, jnp.min, jnp.minimum),
        (lambda s, k0: one_if(s > 0.0), 0.0, F32, jnp.sum, jnp.add),
        (lambda s, k0: one_if(s == 0.0), 0.0, F32, jnp.sum, jnp.add)])

    long_row = n_adm > kf
    pos_row = long_row & (cpos >= kf)
    zero_row = long_row & (cpos < kf) & (cpos + czero >= kf)
    neg_row = long_row & (cpos + czero < kf)
    lo0 = jnp.where(pos_row | zero_row, 0.0, vmin - jnp.maximum(1.0, jnp.abs(vmin)))
    cnt_lo0 = jnp.where(pos_row, cpos, jnp.where(zero_row, kf, n_adm))
    hi0 = jnp.where(neg_row, 0.0, vmax)
    cnt_hi0 = jnp.where(neg_row, cpos, 0.0)
    need_zero = jnp.where(zero_row, kf - cpos, 0.0)

    def narrow(lo, hi, cnt_lo, cnt_hi):
        fracs = [(t + 1.0) / (BISECT_PROBES + 1.0) for t in range(BISECT_PROBES)]
        mids = [lo * (1.0 - f) + hi * f for f in fracs]
        for t in range(1, BISECT_PROBES):
            mids[t] = jnp.maximum(mids[t], mids[t - 1])
        codes = [sum(COUNT_RADIX ** u for u in range(t + 1)) for t in range(BISECT_PROBES)]

        def encode(s, k0):
            e = jnp.zeros(s.shape, jnp.int32)
            for mid, code in zip(mids, codes):
                e = jnp.where(s > mid, code, e)
            return e
        (packed,) = reduce_blocks([(encode, 0, jnp.int32, None, jnp.add)])
        cnts = []
        for t in range(BISECT_PROBES):
            digit = packed & (COUNT_RADIX - 1) if t < BISECT_PROBES - 1 else packed
            packed = packed >> COUNT_RADIX.bit_length() - 1
            cnts.append(jnp.sum(digit.astype(F32), axis=key_axis, keepdims=True))
        lo_n, cnt_lo_n, hi_n, cnt_hi_n = lo, cnt_lo, hi, cnt_hi
        for mid, cm in zip(mids, cnts):
            up = cm >= kf
            lo_n = jnp.where(up, mid, lo_n)
            cnt_lo_n = jnp.where(up, cm, cnt_lo_n)
        for mid, cm in zip(mids[::-1], cnts[::-1]):
            dn = cm < kf
            hi_n = jnp.where(dn, mid, hi_n)
            cnt_hi_n = jnp.where(dn, cm, cnt_hi_n)
        return lo_n, hi_n, cnt_lo_n, cnt_hi_n

    def unresolved(cnt_lo):
        return (jnp.max(cnt_lo) > kf).astype(jnp.int32)

    blind = jnp.where(unresolved(cnt_lo0) > 0, BISECT_BLIND_ITERS, 0)
    state = lax.fori_loop(0, blind, lambda i, st: narrow(*st), (lo0, hi0, cnt_lo0, cnt_hi0))

    def cond(st):
        return (st[0] < BISECT_MAX_ITERS) & (st[1] > 0)

    def body(st):
        new = narrow(*st[2:])
        return (st[0] + 1, unresolved(new[2])) + new

    _, _, lo, hi, cnt_lo, cnt_hi = lax.while_loop(
        cond, body, (blind, unresolved(state[2])) + state)

    fix = cnt_lo > kf
    need0 = jnp.where(fix, kf - cnt_hi, 0.0)
    big_idx = 2 ** 30

    def fix_cond(need):
        return jnp.max(need) > 0.0

    def fix_body(need):
        def in_cluster(s):
            return (s > lo) & (s <= hi)
        (mval,) = reduce_blocks([(lambda s, k0: jnp.where(in_cluster(s), s, -jnp.inf), -jnp.inf, F32,
                                  jnp.max, jnp.maximum)])
        (idx,) = reduce_blocks(
            [(lambda s, k0: jnp.where(in_cluster(s) & (s == mval), kpos + k0, big_idx),
              big_idx, jnp.int32, jnp.min, jnp.minimum)])
        active = need > 0.0

        def promote(kb, carry):
            for i in range(kblk // strip):
                s = strip_of(kb, i)
                put_strip(kb, i, jnp.where(active & ((kpos + kb * kblk + i * strip) == idx), jnp.inf, s))
            return carry
        lax.fori_loop(0, nkb, promote, 0)
        return jnp.where(active, need - 1.0, need)

    lax.while_loop(fix_cond, fix_body, need0)

    @pl.when(jnp.max(need_zero) > 0.0)
    def _():
        r = lax.broadcasted_iota(jnp.int32, (kblk, kblk), 0)
        c = lax.broadcasted_iota(jnp.int32, (kblk, kblk), 1)
        tri = ((r >= c) if key_axis == 0 else (r <= c)).astype(BF16)

        def tie_blk(kb, seen):
            s = sc_ref[kb]
            z = (s == 0.0) & zero_row
            zb = one_if(z).astype(BF16)
            if key_axis == 0:
                rank = _dot(tri, zb) + seen
                last = rank[kblk - 1:kblk, :]
            else:
                rank = _dot(zb, tri) + seen
                last = rank[:, kblk - 1:kblk]
            sc_ref[kb] = jnp.where(z & (rank <= need_zero), jnp.inf, s)
            return last
        lax.fori_loop(0, nkb, tie_blk, jnp.zeros_like(need_zero))

    return jnp.where(fix, hi, lo)


def _dsa_body(qb_ref, qi_ref, miscq_ref, zb_ref, k_ref, vt_ref, misck_ref, o_ref, sc_ref, s_ref, *, n_sel, qblk):
    j = pl.program_id(1)
    nkb = ((j + 1) * qblk + KEY_BLOCK - 1) // KEY_BLOCK
    qi = qi_ref[...]
    q_h = [qi[:, h * IDX_HD:(h + 1) * IDX_HD].astype(BF16) for h in range(IDX_HEADS)]
    w_t = miscq_ref[...].T[MISC_WI:MISC_WI + IDX_HEADS, :]
    qchunk = (j * qblk + lax.broadcasted_iota(jnp.int32, (1, qblk), 1)) // CHUNK
    krow = lax.broadcasted_iota(jnp.int32, (KEY_BLOCK, 1), 0)

    def score_blk(kb, carry):
        k0 = pl.multiple_of(kb * KEY_BLOCK, KEY_BLOCK)
        for half in range(KEY_BLOCK // SCORE_STRIP):
            r0 = half * SCORE_STRIP
            ki = misck_ref[pl.ds(k0 + r0, SCORE_STRIP), 0:IDX_HD].astype(BF16)
            acc = jnp.zeros((SCORE_STRIP, qblk), F32)
            for h in range(IDX_HEADS):
                acc = acc + w_t[h:h + 1, :] * jnp.maximum(_dot_nt(ki, q_h[h]), 0.0)
            adm = ((k0 + r0 + krow[:SCORE_STRIP]) // CHUNK) <= qchunk
            sc_ref[kb, r0:r0 + SCORE_STRIP, :] = jnp.where(adm, acc, -jnp.inf)
        return carry

    npairs = (nkb + 1) // 2

    def pair_loop(body, init):
        return lax.fori_loop(0, npairs, lambda i, c: body(2 * i + 1, body(2 * i, c)), init)

    pair_loop(score_blk, 0)

    n_adm = ((qchunk + 1) * CHUNK).astype(F32)
    thr = _topk_threshold(sc_ref, nkb, (KEY_BLOCK, qblk), 0, n_sel, n_adm)

    qb = qb_ref[...] * (B_HD ** -0.5 * math.log2(math.e))
    qs = jnp.concatenate([jnp.where(_lane_mask(B_WIDTH, h * B_HD, (h + 1) * B_HD), qb, 0.0)
                          for h in range(B_HEADS)], axis=0).astype(BF16)
    wide = B_HEADS * qblk

    def fold(acc, x, combine):
        for i in range(x.shape[0] // 8):
            acc = combine(acc, x[8 * i:8 * (i + 1)])
        return acc

    def logits_blk(kb, m8):
        k0 = pl.multiple_of(kb * KEY_BLOCK, KEY_BLOCK)
        s = _dot_nt(k_ref[pl.ds(k0, KEY_BLOCK), :].astype(BF16), qs)
        sel = sc_ref[kb] > thr
        s = jnp.where(jnp.concatenate([sel] * B_HEADS, axis=1), s, NEG_BIG)
        s_ref[kb] = s
        return fold(m8, s, jnp.maximum)

    m8 = pair_loop(logits_blk, jnp.full((8, wide), NEG_BIG, F32))
    m = jnp.max(m8, axis=0, keepdims=True)

    def pv_blk(kb, carry):
        l8, accs = carry
        p = jnp.exp2(s_ref[kb] - m)
        pb = p.astype(BF16)
        k0 = pl.multiple_of(kb * KEY_BLOCK, KEY_BLOCK)
        vt = vt_ref[:, pl.ds(k0, KEY_BLOCK)].astype(BF16)
        accs = tuple(accs[h] + _dot(vt[h * B_HD:(h + 1) * B_HD, :], pb[:, h * qblk:(h + 1) * qblk])
                     for h in range(B_HEADS))
        return fold(l8, p, jnp.add), accs

    l8, accs = pair_loop(
        pv_blk, (jnp.zeros((8, wide), F32), tuple(jnp.zeros((B_HD, qblk), F32) for _ in range(B_HEADS))))
    l = jnp.sum(l8, axis=0, keepdims=True)
    o_t = jnp.concatenate([accs[h] / l[:, h * qblk:(h + 1) * qblk] for h in range(B_HEADS)], axis=0)
    o_ref[...] = o_t.T * _silu(zb_ref[...])


def _dsa_prompt(qb, qi, misc, zb, kb, vt, *, bsz, t_len, n_sel, qblk):
    nq = t_len // qblk
    nkb = t_len // KEY_BLOCK
    assert nkb * (KEY_BLOCK // 8) < COUNT_RADIX and nkb % 2 == 0
    qmap = lambda b, j: (b * nq + j, 0)
    kmap = lambda b, j: (b, 0)
    return pl.pallas_call(
        functools.partial(_dsa_body, n_sel=n_sel, qblk=qblk),
        grid=(bsz, nq),
        in_specs=[pl.BlockSpec((qblk, B_WIDTH), qmap),
                  pl.BlockSpec((qblk, IDX_HEADS * IDX_HD), qmap),
                  pl.BlockSpec((qblk, MISC_W), qmap),
                  pl.BlockSpec((qblk, B_WIDTH), qmap),
                  pl.BlockSpec((t_len, B_WIDTH), kmap),
                  pl.BlockSpec((None, B_WIDTH, t_len), lambda b, j: (b, 0, 0)),
                  pl.BlockSpec((t_len, MISC_W), kmap)],
        out_specs=pl.BlockSpec((qblk, B_WIDTH), qmap),
        out_shape=jax.ShapeDtypeStruct((bsz * t_len, B_WIDTH), F32),
        scratch_shapes=[pltpu.VMEM((nkb, KEY_BLOCK, qblk), F32),
                        pltpu.VMEM((nkb, KEY_BLOCK, B_HEADS * qblk), F32)],
        compiler_params=_cparams(("parallel", "arbitrary")),
        name="dsa_prompt",
    )(qb, qi, misc, zb, kb, vt, misc)


def _dsa_dec_body(qb_ref, qi_ref, misc_ref, zb_ref, kn_ref, vn_ref, ck_ref, cv_ref, cki_ref, o_ref,
                  sc_ref, thr_ref, m_ref, l_ref, acc_ref, *, p_len, tq, n_sel, kstep):
    step = pl.program_id(1)
    nsteps = p_len // kstep
    sub = kstep // KEY_BLOCK
    npast = p_len // KEY_BLOCK
    nkb = npast + 1
    pad_rows = KEY_BLOCK - tq
    rows = B_HEADS * tq

    @pl.when(step == 0)
    def _():
        qi = qi_ref[...]
        misc = misc_ref[...]
        q2 = jnp.concatenate([qi[:, h * IDX_HD:(h + 1) * IDX_HD] for h in range(IDX_HEADS)],
                             axis=0).astype(BF16)
        w_h = [misc[:, MISC_WI + h:MISC_WI + h + 1] for h in range(IDX_HEADS)]

        def head_sum(lg):
            acc = jnp.zeros((tq, lg.shape[1]), F32)
            for h in range(IDX_HEADS):
                acc = acc + w_h[h] * jnp.maximum(lg[h * tq:(h + 1) * tq, :], 0.0)
            return acc

        for i in range(nsteps):
            sc = head_sum(_dot(q2, cki_ref[:, i * kstep:(i + 1) * kstep].astype(BF16)))
            for u in range(sub):
                sc_ref[i * sub + u] = sc[:, u * KEY_BLOCK:(u + 1) * KEY_BLOCK]
        lane = lax.broadcasted_iota(jnp.int32, (1, KEY_BLOCK), 1)
        ki_new = jnp.concatenate([misc[:, 0:IDX_HD], jnp.zeros((pad_rows, IDX_HD), F32)], axis=0)
        sc_ref[npast] = jnp.where(lane < tq, head_sum(_dot_nt(q2, ki_new.astype(BF16))), -jnp.inf)
        n_adm = jnp.full((tq, 1), float(p_len + tq), F32)
        thr_ref[...] = _topk_threshold(sc_ref, nkb, (tq, KEY_BLOCK), 1, n_sel, n_adm)
        m_ref[...] = jnp.full((rows, 1), NEG_BIG, F32)
        l_ref[...] = jnp.zeros((rows, 1), F32)
        acc_ref[...] = jnp.zeros((rows, B_HD), F32)

    qb = qb_ref[...] * (B_HD ** -0.5 * math.log2(math.e))
    q_h = [qb[:, h * B_HD:(h + 1) * B_HD].astype(BF16) for h in range(B_HEADS)]
    thr = thr_ref[...]

    def fold_in(logits_of, pv_of, sel):
        s = jnp.concatenate([jnp.where(sel, logits_of(h), NEG_BIG) for h in range(B_HEADS)], axis=0)
        m_old = m_ref[...]
        m_new = jnp.maximum(m_old, jnp.max(s, axis=-1, keepdims=True))
        alpha = jnp.exp2(m_old - m_new)
        p = jnp.exp2(s - m_new)
        pb = p.astype(BF16)
        pv = jnp.concatenate([pv_of(h, pb[h * tq:(h + 1) * tq, :]) for h in range(B_HEADS)], axis=0)
        l_ref[...] = alpha * l_ref[...] + jnp.sum(p, axis=-1, keepdims=True)
        acc_ref[...] = alpha * acc_ref[...] + pv
        m_ref[...] = m_new

    sel = jnp.concatenate([sc_ref[step * sub + u] > thr for u in range(sub)], axis=1)
    fold_in(lambda h: _dot(q_h[h], ck_ref[h].astype(BF16)),
            lambda h, p: _dot_nt(p, cv_ref[h].astype(BF16)), sel)

    @pl.when(step == nsteps - 1)
    def _():
        zpad = jnp.zeros((pad_rows, B_WIDTH), F32)
        kn = jnp.concatenate([kn_ref[...], zpad], axis=0).astype(BF16)
        vn = jnp.concatenate([vn_ref[...], zpad], axis=0).astype(BF16)
        fold_in(lambda h: _dot_nt(q_h[h], kn[:, h * B_HD:(h + 1) * B_HD]),
                lambda h, p: _dot(p, vn[:, h * B_HD:(h + 1) * B_HD]), sc_ref[npast] > thr)
        res = acc_ref[...] / l_ref[...]
        out = jnp.concatenate([res[h * tq:(h + 1) * tq, :] for h in range(B_HEADS)], axis=1)
        o_ref[...] = out * _silu(zb_ref[...])


def _dsa_decode(qb, qi, misc, zb, kb, vb, ck, cv, cki, *, bsz, t_len, p_len, n_sel):
    kstep = DEC_KEYS_PER_STEP
    assert p_len // KEY_BLOCK + 1 < COUNT_RADIX and p_len % kstep == 0
    rows = B_HEADS * t_len
    qmap = lambda b, s: (b, 0)
    cmap = lambda b, s: (b, 0, 0, s)
    ck, cv = (a.transpose(0, 2, 3, 1) for a in (ck, cv))
    cki = cki.transpose(0, 2, 1)
    return pl.pallas_call(
        functools.partial(_dsa_dec_body, p_len=p_len, tq=t_len, n_sel=n_sel, kstep=kstep),
        grid=(bsz, p_len // kstep),
        in_specs=[pl.BlockSpec((t_len, B_WIDTH), qmap),
                  pl.BlockSpec((t_len, IDX_HEADS * IDX_HD), qmap),
                  pl.BlockSpec((t_len, MISC_W), qmap),
                  pl.BlockSpec((t_len, B_WIDTH), qmap),
                  pl.BlockSpec((t_len, B_WIDTH), qmap),
                  pl.BlockSpec((t_len, B_WIDTH), qmap),
                  pl.BlockSpec((None, B_HEADS, B_HD, kstep), cmap),
                  pl.BlockSpec((None, B_HEADS, B_HD, kstep), cmap),
                  pl.BlockSpec((None, IDX_HD, p_len), lambda b, s: (b, 0, 0))],
        out_specs=pl.BlockSpec((t_len, B_WIDTH), qmap),
        out_shape=jax.ShapeDtypeStruct((bsz * t_len, B_WIDTH), F32),
        scratch_shapes=[pltpu.VMEM((p_len // KEY_BLOCK + 1, t_len, KEY_BLOCK), F32),
                        pltpu.VMEM((t_len, 1), F32),
                        pltpu.VMEM((rows, 1), F32),
                        pltpu.VMEM((rows, 1), F32),
                        pltpu.VMEM((rows, B_HD), F32)],
        compiler_params=_cparams(("parallel", "arbitrary")),
        name="dsa_decode",
    )(qb, qi, misc, zb, kb, vb, ck, cv, cki)


def _tail_body(x_ref, oa_ref, ob_ref, qm_ref, zm_ref, mk_ref, mv_ref, w_ref, y_ref):
    qm = qm_ref[...] * (M_HD ** -0.5)
    mk = mk_ref[...].reshape(N_MEM, M_WIDTH).astype(BF16)
    mv = mv_ref[...].reshape(N_MEM, M_WIDTH).astype(BF16)
    acc = _dot(oa_ref[...].astype(BF16), w_ref[0:A_WIDTH, :])
    acc = acc + _dot(ob_ref[...].astype(BF16), w_ref[A_WIDTH:A_WIDTH + B_WIDTH, :])
    om = jnp.zeros(qm.shape, F32)
    for h in range(M_HEADS):
        mask = _lane_mask(M_WIDTH, h * M_HD, (h + 1) * M_HD)
        s = _dot_nt(jnp.where(mask, qm, 0.0).astype(BF16), mk)
        p = jnp.exp(s - jnp.max(s, axis=-1, keepdims=True))
        l = jnp.sum(p, axis=-1, keepdims=True)
        om = om + jnp.where(mask, _dot(p.astype(BF16), mv) / l, 0.0)
    om = om * _silu(zm_ref[...])
    acc = acc + _dot(om.astype(BF16), w_ref[A_WIDTH + B_WIDTH:, :])
    y_ref[...] = x_ref[...] + acc


def _tail(x, oa, ob, qm, zm, mk, mv, w, *, bsz, t_len, tq):
    m, d = x.shape
    nq = t_len // tq
    qmap = lambda b, j: (b * nq + j, 0)
    mem_spec = pl.BlockSpec((None,) + mk.shape[1:], lambda b, j: (b,) + (0,) * (mk.ndim - 1))
    return pl.pallas_call(
        _tail_body,
        grid=(bsz, nq),
        in_specs=[pl.BlockSpec((tq, d), qmap),
                  pl.BlockSpec((tq, A_WIDTH), qmap),
                  pl.BlockSpec((tq, B_WIDTH), qmap),
                  pl.BlockSpec((tq, M_WIDTH), qmap),
                  pl.BlockSpec((tq, M_WIDTH), qmap),
                  mem_spec, mem_spec,
                  pl.BlockSpec(w.shape, lambda b, j: (0, 0))],
        out_specs=pl.BlockSpec((tq, d), qmap),
        out_shape=jax.ShapeDtypeStruct((m, d), F32),
        compiler_params=_cparams(("parallel", "parallel")),
        name="tail",
    )(x, oa, ob, qm, zm, mk, mv, w)


def _prep_weights(g_in, w_in, conv_w, a_log, dt_bias, g_o, g_qb, g_kb, g_ki, g_qm, w_out):
    offs = np.concatenate([[0], np.cumsum(IN_SPLITS)])
    col = lambda i: w_in[:, int(offs[i]):int(offs[i + 1])]
    (w_qkv, w_za, w_ba, w_aa, w_qb, w_kb, w_vb, w_zb, w_qi, w_ki, w_wi, w_qm, w_zm) = [col(i) for i in range(13)]
    d = w_in.shape[0]
    w_misc = jnp.concatenate(
        [w_ki, w_wi, w_ba, w_aa, jnp.zeros((d, MISC_W - MISC_A - A_HEADS), w_in.dtype)], axis=1)
    w_all = jnp.concatenate([w_za, w_qb, w_kb, w_vb, w_zb, w_qi, w_qm, w_zm, w_misc, w_qkv],
                            axis=1).astype(BF16)
    groups = [(A_WIDTH, False, ""), (B_WIDTH, True, ""),
              (B_WIDTH, True, "t"), (B_WIDTH, False, "t"), (B_WIDTH, False, ""),
              (IDX_HEADS * IDX_HD, False, ""), (M_WIDTH, True, ""), (M_WIDTH, False, ""),
              (MISC_W, True, ""),
              (A_WIDTH, False, "conv0"), (A_WIDTH, False, "conv1"), (A_WIDTH, False, "conv2")]
    bd64 = _group_mean_matrix(B_WIDTH, B_HD, B_WIDTH)
    ones256 = jnp.ones((1, B_WIDTH), F32)
    misc_gain = jnp.concatenate([g_ki, jnp.full((IDX_HEADS,), IDX_SCALE, F32),
                                 jnp.ones((MISC_W - MISC_B,), F32)])[None, :]
    misc_nm = (jnp.arange(MISC_W) < IDX_HD).astype(F32)[None, :]
    aux = [bd64, jnp.tile(g_qb, B_HEADS)[None, :], ones256,
           bd64, jnp.tile(g_kb, B_HEADS)[None, :], ones256,
           bd64, jnp.tile(g_qm, M_HEADS)[None, :], ones256,
           _group_mean_matrix(MISC_W, IDX_HD, IDX_HD), misc_gain, misc_nm]
    pad_a = lambda v: jnp.zeros((1, MISC_W), F32).at[0, MISC_A:MISC_A + A_HEADS].set(v)
    return dict(g_in=g_in[None, :], w_all=w_all, groups=groups, aux=aux, conv_w=conv_w,
                avec=pad_a(-jnp.exp(a_log)), dtvec=pad_a(dt_bias), g_o=g_o[None, :],
                w_out=w_out.astype(BF16))


def _layer(x, mem_k, mem_v, conv_buf, s0, past, gdn_chunk, gdn_rows, wp):
    bsz, t_len, d = x.shape
    m = bsz * t_len
    x2 = x.reshape(m, d)
    tm = min(PROJ_ROWS, m)
    heads_last = lambda a_t: a_t.reshape(bsz, B_HEADS, B_HD, t_len).transpose(0, 3, 1, 2)
    cbuf = jnp.concatenate([jnp.zeros((bsz, CARRY_ROWS - (CONV_W - 1), 3 * A_WIDTH), F32), conv_buf], axis=1)
    conv = (cbuf, wp["conv_w"])
    gdn_args = dict(bsz=bsz, t_len=t_len, chunk=gdn_chunk, rows=gdn_rows)
    if past is None:
        (za, qb, kb, vb, zb, qi, qm, zm, misc, qa, ka, va, kt, vt, conv_new) = _proj(
            x2, wp["g_in"], wp["w_all"], wp["groups"], wp["aux"], tm, t_len, conv=conv)
        k_new, v_new = heads_last(kt), heads_last(vt)
        oa, s_new = _gdn(qa, ka, va, za, misc, s0, wp["avec"], wp["dtvec"], wp["g_o"], **gdn_args)
    else:
        groups = [(wd, normed, "") for wd, normed, _ in wp["groups"]]
        (za, qb, kb, vb, zb, qi, qm, zm, misc, qa, ka, va) = _proj(x2, wp["g_in"], wp["w_all"], groups,
                                                                   wp["aux"], tm, m)
        k_new, v_new = (a.reshape(bsz, t_len, B_HEADS, B_HD) for a in (kb, vb))
        oa, s_new, conv_new = _gdn(qa, ka, va, za, misc, s0, wp["avec"], wp["dtvec"], wp["g_o"], conv=conv,
                                   **gdn_args)
    if past is None:
        n_sel = min(TOPK_MAX, t_len // 4)
        ob = _dsa_prompt(qb, qi, misc, zb, kb, vt, bsz=bsz, t_len=t_len, n_sel=n_sel, qblk=DSA_QBLK)
    else:
        ck, cv, cki = past
        p_len = ck.shape[1]
        assert (p_len + t_len - 1) // CHUNK <= p_len // CHUNK and p_len % KEY_BLOCK == 0
        n_sel = min(TOPK_MAX, (p_len + t_len) // 4)
        ob = _dsa_decode(qb, qi, misc, zb, kb, vb, ck, cv, cki, bsz=bsz, t_len=t_len, p_len=p_len,
                         n_sel=n_sel)
    y = _tail(x2, oa, ob, qm, zm, mem_k, mem_v, wp["w_out"], bsz=bsz, t_len=t_len, tq=min(TAIL_ROWS, t_len))
    return (y.reshape(bsz, t_len, d), conv_new, s_new, k_new, v_new,
            misc[:, :IDX_HD].reshape(bsz, t_len, IDX_HD))


def _memory_kv(mem, g_mem, w_mem_kv, g_km):
    bsz, n_mem, d = mem.shape
    groups = [(M_WIDTH, True, "t"), (M_WIDTH, False, "t")]
    aux = [_group_mean_matrix(M_WIDTH, M_HD, M_WIDTH), jnp.tile(g_km, M_HEADS)[None, :],
           jnp.ones((1, M_WIDTH), F32)]
    mk, mv, mk_t, mv_t = _proj(mem.reshape(bsz * n_mem, d), g_mem[None, :], w_mem_kv.astype(BF16),
                               groups, aux, n_mem, n_mem)
    heads_last = lambda a_t: a_t.reshape(bsz, M_HEADS, M_HD, n_mem).transpose(0, 3, 1, 2)
    return (mk.reshape(bsz, n_mem, M_WIDTH), mv.reshape(bsz, n_mem, M_WIDTH), heads_last(mk_t), heads_last(mv_t))


def kernel(x_prompt, x_sample, state_conv_A, state_ssm_A, cache_k_B, cache_v_B, cache_kidx_B, cache_mem_k,
           cache_mem_v, mem_prompt, g_in, w_in, conv_w_A, a_log_A, dt_bias_A, g_o_A, g_q_B, g_k_B, g_kidx_B,
           g_mem, w_mem_kv, g_q_M, g_k_M, w_out):
    depth = w_in.shape[0]
    assert depth == 1
    l = 0
    bp, t_p, _ = x_prompt.shape
    bs, t_s, _ = x_sample.shape
    wp = _prep_weights(g_in[l], w_in[l], conv_w_A[l], a_log_A[l], dt_bias_A[l], g_o_A[l], g_q_B[l], g_k_B[l],
                       g_kidx_B[l], g_q_M[l], w_out[l])
    mk, mv, mk_heads, mv_heads = _memory_kv(mem_prompt, g_mem[l], w_mem_kv[l], g_k_M[l])
    zero_conv = jnp.zeros((bp, CONV_W - 1, 3 * A_WIDTH), F32)
    zero_ssm = jnp.zeros((bp, A_HEADS, A_DK, A_DV), F32)
    yp, c1, s1, k1, v1, ki1 = _layer(x_prompt, mk, mv, zero_conv, zero_ssm, None, CHUNK, GDN_CHUNKS_PER_STEP * CHUNK, wp)
    ys, c2, s2, k2, v2, ki2 = _layer(
        x_sample, cache_mem_k[l], cache_mem_v[l],
        state_conv_A[l], state_ssm_A[l], (cache_k_B[l], cache_v_B[l], cache_kidx_B[l]), t_s, t_s, wp)
    st = lambda a: a[None]
    return (yp, ys, st(c1), st(s1), st(k1), st(v1), st(ki1),
            st(mk_heads), st(mv_heads),
            st(c2), st(s2), st(k2), st(v2), st(ki2))
```

```python
import functools
import math

import numpy as np
import jax
import jax.numpy as jnp
from jax import lax
from jax.experimental import pallas as pl
from jax.experimental.pallas import tpu as pltpu

F32 = jnp.float32
BF16 = jnp.bfloat16
HIGHEST = lax.Precision.HIGHEST

D_MODEL = 1024
CHUNK = 64
A_HEADS = 4
A_DK = 128
A_DV = 128
A_WIDTH = A_HEADS * A_DV
CONV_W = 4
B_HEADS = 4
B_HD = 64
B_WIDTH = B_HEADS * B_HD
IDX_HEADS = 8
IDX_HD = 32
IDX_SCALE = (IDX_HEADS ** -0.5) * (IDX_HD ** -0.5)
TOPK_MAX = 256
Q_BLOCK = 128
N_MEM = 256
M_HEADS = 4
M_HD = 64
M_WIDTH = M_HEADS * M_HD
EPS = 1e-6
IN_SPLITS = (3 * A_WIDTH, A_WIDTH, A_HEADS, A_HEADS,
             B_WIDTH, B_WIDTH, B_WIDTH, B_WIDTH, IDX_HEADS * IDX_HD, IDX_HD, IDX_HEADS,
             M_WIDTH, M_WIDTH)

LANES = 128
KEY_BLOCK = 256
DSA_QBLK = 512
GDN_CHUNKS_PER_STEP = 8
TAIL_ROWS = 512
PROJ_ROWS = 512
DEC_KEYS_PER_STEP = 4096
SCORE_STRIP = 128
MISC_W = LANES
MISC_WI = IDX_HD
MISC_B = IDX_HD + IDX_HEADS
MISC_A = MISC_B + A_HEADS
VMEM_LIMIT = 48 * 1024 * 1024
NEG_BIG = -1e30
BISECT_PROBES = 3
COUNT_RADIX = 1024
BISECT_BLIND_ITERS = 7
BISECT_MAX_ITERS = 20


def _cparams(sem):
    return pltpu.CompilerParams(dimension_semantics=sem, vmem_limit_bytes=VMEM_LIMIT)


def _dot(a, b):
    return jnp.dot(a, b, preferred_element_type=F32)


def _dot_nt(a, b, precision=None):
    return lax.dot_general(a, b, (((1,), (1,)), ((), ())), precision=precision,
                           preferred_element_type=F32)


def _split2(x):
    hi = x.astype(BF16)
    return hi, (x - hi.astype(F32)).astype(BF16)


def _mm_split(a, b):
    ah, al = a
    bh, _ = b
    n = ah.shape[0]
    t = _dot(jnp.concatenate([ah, al], axis=0), bh)
    return t[:n] + t[n:]


def _silu(x):
    return x * jax.nn.sigmoid(x)


def _lane_mask(width, lo, hi):
    lane = lax.broadcasted_iota(jnp.int32, (1, width), 1)
    return (lane >= lo) & (lane < hi)


CARRY_ROWS = 8


def _conv_silu_norm(xp_ref, raw, cw, rows, part):
    first = CARRY_ROWS - (CONV_W - 1)
    xp_ref[CARRY_ROWS:CARRY_ROWS + rows, :] = raw
    y = xp_ref[first:first + rows, :] * cw[0:1, :]
    for j in range(1, CONV_W):
        y = y + xp_ref[first + j:first + j + rows, :] * cw[j:j + 1, :]
    y = _silu(y)
    conv_new = xp_ref[rows + first:rows + CARRY_ROWS, :]
    xp_ref[0:CARRY_ROWS, :] = xp_ref[rows:rows + CARRY_ROWS, :]
    if part == 2:
        return y, conv_new
    scale = A_DK ** -0.5 if part == 0 else 1.0
    heads = []
    for h in range(A_HEADS):
        yh = y[:, h * A_DK:(h + 1) * A_DK]
        heads.append(yh * (lax.rsqrt(jnp.sum(yh * yh, axis=-1, keepdims=True) + EPS) * scale))
    return jnp.concatenate(heads, axis=1), conv_new


def _conv_part(extra):
    return int(extra[len("conv"):]) if extra.startswith("conv") else None


def _proj_body(*refs, groups, tiles):
    x_ref, g_ref, w_ref = refs[0], refs[1], refs[2]
    n_aux = 3 * sum(1 for _, normed, _ in groups if normed)
    aux = refs[3:3 + n_aux]
    pos = 3 + n_aux
    n_conv = sum(1 for _, _, extra in groups if _conv_part(extra) is not None)
    if n_conv:
        cbuf_ref, cw_ref = refs[pos], refs[pos + 1]
        pos += 2
    outs = refs[pos:pos + len(groups)]
    pos += len(groups)
    n_t = sum(1 for _, _, extra in groups if extra == "t")
    outs_t = list(refs[pos:pos + n_t])
    pos += n_t
    if n_conv:
        convnew_ref = refs[pos]
        xp_refs = refs[pos + 1:pos + 1 + n_conv]

        @pl.when(pl.program_id(0) % tiles == 0)
        def _():
            for part, xp_ref in enumerate(xp_refs):
                xp_ref[0:CARRY_ROWS, :] = cbuf_ref[:, part * A_WIDTH:(part + 1) * A_WIDTH]
    x = x_ref[...]
    r = lax.rsqrt(jnp.mean(x * x, axis=-1, keepdims=True) + EPS)
    hb = (x * g_ref[...]).astype(BF16)
    off = 0
    ai = 0
    for (width, normed, extra), o_ref in zip(groups, outs):
        y = _dot(hb, w_ref[:, off:off + width]) * r
        part = _conv_part(extra)
        if part is not None:
            cols = slice(part * A_WIDTH, (part + 1) * A_WIDTH)
            y, conv_new = _conv_silu_norm(xp_refs[part], y, cw_ref[:, cols], y.shape[0], part)
            convnew_ref[:, cols] = conv_new
        if normed:
            bd_ref, gain_ref, nm_ref = aux[ai], aux[ai + 1], aux[ai + 2]
            ai += 3
            sq = y * y
            hi = sq.astype(BF16)
            lo = (sq - hi.astype(F32)).astype(BF16)
            ms = _dot(hi, bd_ref[...]) + _dot(lo, bd_ref[...])
            scale = jnp.where(nm_ref[...] > 0.0, lax.rsqrt(ms + EPS), 1.0)
            y = y * scale * gain_ref[...]
        o_ref[...] = y
        if extra == "t":
            outs_t.pop(0)[...] = y.T
        off += width


def _group_mean_matrix(width, group, n_lanes):
    i = np.arange(width)
    m = ((i[:, None] // group) == (i[None, :] // group)) & (i[:, None] < n_lanes) & (i[None, :] < n_lanes)
    return jnp.asarray(m.astype(np.float32) / group, dtype=BF16)


def _proj(x, g, w, groups, aux, tm, t_len, conv=None):
    m, d = x.shape
    nw = w.shape[1]
    tiles = t_len // tm
    bsz = m // t_len
    in_specs = [pl.BlockSpec((tm, d), lambda i: (i, 0)),
                pl.BlockSpec((1, d), lambda i: (0, 0)),
                pl.BlockSpec((d, nw), lambda i: (0, 0))]
    for a in aux:
        in_specs.append(pl.BlockSpec(a.shape, lambda i: (0, 0)))
    out_shape = [jax.ShapeDtypeStruct((m, wd), F32) for wd, _, _ in groups]
    out_specs = [pl.BlockSpec((tm, wd), lambda i: (i, 0)) for wd, _, _ in groups]
    for wd, _, extra in groups:
        if extra == "t":
            out_shape.append(jax.ShapeDtypeStruct((bsz, wd, t_len), F32))
            out_specs.append(pl.BlockSpec((None, wd, tm), lambda i: (i // tiles, 0, i % tiles)))
    args = [x, g, w, *aux]
    scratch = []
    if conv is not None:
        cbuf, cw = conv
        wd = cw.shape[1]
        in_specs += [pl.BlockSpec((None, CARRY_ROWS, wd), lambda i: (i // tiles, 0, 0)),
                     pl.BlockSpec(cw.shape, lambda i: (0, 0))]
        args += [cbuf, cw]
        out_shape.append(jax.ShapeDtypeStruct((bsz, CONV_W - 1, wd), F32))
        out_specs.append(pl.BlockSpec((None, CONV_W - 1, wd), lambda i: (i // tiles, 0, 0)))
        scratch += [pltpu.VMEM((tm + CARRY_ROWS, A_WIDTH), F32) for _ in range(wd // A_WIDTH)]
    return pl.pallas_call(
        functools.partial(_proj_body, groups=tuple(groups), tiles=tiles),
        grid=(m // tm,),
        in_specs=in_specs, out_specs=out_specs, out_shape=out_shape, scratch_shapes=scratch,
        compiler_params=_cparams(("arbitrary",) if conv is not None else ("parallel",)),
        name="proj",
    )(*args)


def _gdn_body(*refs, chunk, rows, fused_conv):
    if fused_conv:
        (q_ref, k_ref, v_ref, za_ref, misc_ref, s0_ref, avec_ref, dtvec_ref, go_ref, cbuf_ref, cw_ref,
         o_ref, sfin_ref, convnew_ref, g_sc, b_sc, s_sc, *conv_sc) = refs
    else:
        (q_ref, k_ref, v_ref, za_ref, misc_ref, s0_ref, avec_ref, dtvec_ref, go_ref,
         o_ref, sfin_ref, g_sc, b_sc, s_sc) = refs
    t = pl.program_id(1)
    c = chunk

    @pl.when(t == 0)
    def _():
        s_sc[...] = s0_ref[0]

    if fused_conv:
        xp_refs, act_refs = conv_sc[:3], conv_sc[3:]

        @pl.when(t == 0)
        def _():
            for part, xp_ref in enumerate(xp_refs):
                xp_ref[0:CARRY_ROWS, :] = cbuf_ref[:, part * A_WIDTH:(part + 1) * A_WIDTH]
        for part, raw_ref in enumerate((q_ref, k_ref, v_ref)):
            cols = slice(part * A_WIDTH, (part + 1) * A_WIDTH)
            act, conv_new = _conv_silu_norm(xp_refs[part], raw_ref[...], cw_ref[:, cols], rows, part)
            convnew_ref[:, cols] = conv_new
            act_refs[part][...] = act
        q_sc, k_sc, v_sc = act_refs
    else:
        q_sc, k_sc, v_sc = q_ref, k_ref, v_ref
    misc = misc_ref[...]
    b_sc[...] = jax.nn.sigmoid(misc)
    g_sc[...] = avec_ref[...] * jax.nn.softplus(misc + dtvec_ref[...])

    ri = lax.broadcasted_iota(jnp.int32, (c, c), 0)
    ci = lax.broadcasted_iota(jnp.int32, (c, c), 1)
    tri = ri >= ci
    strict = ri > ci
    eye_f = (ri == ci).astype(F32)
    tri3 = (lax.broadcasted_iota(jnp.int32, (c, 3 * c), 0)
            >= lax.broadcasted_iota(jnp.int32, (c, 3 * c), 1) % c).astype(BF16)
    go = go_ref[...]
    n_chunks = rows // c

    problems = [(ic, h) for ic in range(n_chunks) for h in range(A_HEADS)]
    gc_all, gc_t = [], []
    for ic in range(n_chunks):
        g = g_sc[ic * c:(ic + 1) * c, :]
        g1 = g.astype(BF16)
        r1 = g - g1.astype(F32)
        g2 = r1.astype(BF16)
        g3 = (r1 - g2.astype(F32)).astype(BF16)
        gc_all.append(_dot(tri3, jnp.concatenate([g1, g2, g3], axis=0)))
    for ic in range(n_chunks):
        gc_t.append(gc_all[ic].T)

    def load(ref, ic, h):
        return ref[ic * c:(ic + 1) * c, h * A_DK:(h + 1) * A_DK]

    col = lambda a, lane: a[:, lane:lane + 1]
    k_bf = [load(k_sc, ic, h).astype(BF16) for ic, h in problems]
    kb = [load(k_sc, ic, h) * col(b_sc[ic * c:(ic + 1) * c, :], MISC_B + h) for ic, h in problems]
    kk = [_dot_nt(kb[i].astype(BF16), k_bf[i]) for i in range(len(problems))]
    qk_raw = [_dot_nt(load(q_sc, ic, h).astype(BF16), k_bf[i]) for i, (ic, h) in enumerate(problems)]
    decay = []
    for ic, h in problems:
        diff = col(gc_all[ic], MISC_A + h) - gc_t[ic][MISC_A + h:MISC_A + h + 1, :]
        decay.append(jnp.where(tri, jnp.exp(jnp.where(tri, diff, 0.0)), 0.0))
    lmat = [jnp.where(strict, kk[i] * decay[i], 0.0) for i in range(len(problems))]
    qk = [(qk_raw[i] * decay[i]).astype(BF16) for i in range(len(problems))]
    x_inv = [eye_f - m for m in lmat]
    p_split = [_split2(m) for m in lmat]
    for _ in range(int(math.log2(c)) - 1):
        p_split = [_split2(_mm_split(ps, ps)) for ps in p_split]
        x_inv = [x + _mm_split(_split2(x), ps) for x, ps in zip(x_inv, p_split)]
    sol = []
    for i, (ic, h) in enumerate(problems):
        egc = jnp.exp(col(gc_all[ic], MISC_A + h))
        beta = col(b_sc[ic * c:(ic + 1) * c, :], MISC_B + h)
        rhs = jnp.concatenate([load(v_sc, ic, h) * beta, kb[i] * egc], axis=-1).astype(BF16)
        sol.append(_mm_split(_split2(x_inv[i]), (rhs, None)))
    pre = []
    for i, (ic, h) in enumerate(problems):
        gc = col(gc_all[ic], MISC_A + h)
        g_last = gc[c - 1:c, :]
        k_dec_t = (load(k_sc, ic, h) * jnp.exp(g_last - gc)).T.astype(BF16)
        wq = jnp.concatenate([sol[i][:, A_DV:], load(q_sc, ic, h) * jnp.exp(gc)], axis=0).astype(BF16)
        pre.append((sol[i][:, :A_DV], wq, k_dec_t, jnp.exp(g_last)))

    heads = range(A_HEADS)
    s_cur = [s_sc[h] for h in heads]
    for ic in range(n_chunks):
        pr = [pre[ic * A_HEADS + h] for h in heads]
        ws = [_dot(pr[h][1], s_cur[h].astype(BF16)) for h in heads]
        ub = [(pr[h][0] - ws[h][:c]).astype(BF16) for h in heads]
        o = [ws[h][c:] + _dot(qk[ic * A_HEADS + h], ub[h]) for h in heads]
        s_cur = [s_cur[h] * pr[h][3] + _dot(pr[h][2], ub[h]) for h in heads]
        for h in heads:
            on = o[h] * lax.rsqrt(jnp.mean(o[h] * o[h], axis=-1, keepdims=True) + EPS) * go
            o_ref[ic * c:(ic + 1) * c, h * A_DV:(h + 1) * A_DV] = on * _silu(load(za_ref, ic, h))
    for h in heads:
        s_sc[h] = s_cur[h]
    sfin_ref[0] = s_sc[...]


def _gdn(q, k, v, za, misc, s0, avec, dtvec, g_o, *, bsz, t_len, chunk, rows, conv=None):
    nt = t_len // rows
    w3 = 3 * A_WIDTH
    row_map = lambda b, t: (b * nt + t, 0)
    const2 = lambda b, t: (0, 0)
    state_spec = pl.BlockSpec((1, A_HEADS, A_DK, A_DV), lambda b, t: (b, 0, 0, 0))
    in_specs = [pl.BlockSpec((rows, A_WIDTH), row_map)] * 4 + [
                pl.BlockSpec((rows, MISC_W), row_map),
                state_spec,
                pl.BlockSpec((1, MISC_W), const2),
                pl.BlockSpec((1, MISC_W), const2),
                pl.BlockSpec((1, A_DV), const2)]
    args = [q, k, v, za, misc, s0, avec, dtvec, g_o]
    out_specs = [pl.BlockSpec((rows, A_WIDTH), row_map), state_spec]
    out_shape = [jax.ShapeDtypeStruct((bsz * t_len, A_WIDTH), F32),
                 jax.ShapeDtypeStruct((bsz, A_HEADS, A_DK, A_DV), F32)]
    scratch = [pltpu.VMEM((rows, MISC_W), F32),
               pltpu.VMEM((rows, MISC_W), F32),
               pltpu.VMEM((A_HEADS, A_DK, A_DV), F32)]
    if conv is not None:
        cbuf, cw = conv
        in_specs += [pl.BlockSpec((None, CARRY_ROWS, w3), lambda b, t: (b, 0, 0)),
                     pl.BlockSpec((CONV_W, w3), const2)]
        args += [cbuf, cw]
        out_specs.append(pl.BlockSpec((None, CONV_W - 1, w3), lambda b, t: (b, 0, 0)))
        out_shape.append(jax.ShapeDtypeStruct((bsz, CONV_W - 1, w3), F32))
        scratch += [pltpu.VMEM((rows + CARRY_ROWS, A_WIDTH), F32)] * 3 + [pltpu.VMEM((rows, A_WIDTH), F32)] * 3
    return pl.pallas_call(
        functools.partial(_gdn_body, chunk=chunk, rows=rows, fused_conv=conv is not None),
        grid=(bsz, nt),
        in_specs=in_specs, out_specs=out_specs, out_shape=out_shape, scratch_shapes=scratch,
        compiler_params=_cparams(("parallel", "arbitrary")),
        name="gdn",
    )(*args)


def _topk_threshold(sc_ref, nkb, shape, key_axis, n_sel, n_adm, even_blocks=False):
    kf = float(n_sel)
    kblk = shape[key_axis]

    if key_axis == 0:
        strip = min(kblk, 32)
        part_shape = (strip, shape[1])
        strip_of = lambda kb, i: sc_ref[kb, i * strip:(i + 1) * strip, :]

        def put_strip(kb, i, v):
            sc_ref[kb, i * strip:(i + 1) * strip, :] = v
    else:
        strip = kblk
        part_shape = shape
        strip_of = lambda kb, i: sc_ref[kb]

        def put_strip(kb, i, v):
            sc_ref[kb] = v
    kpos = lax.broadcasted_iota(jnp.int32, part_shape, key_axis)

    def reduce_blocks(specs):
        def body(kb, accs):
            accs = list(accs)
            for i in range(kblk // strip):
                s = strip_of(kb, i)
                k0 = kb * kblk + i * strip
                accs = [cmb(acc, fn(s, k0)) for acc, (fn, _, _, _, cmb) in zip(accs, specs)]
            return tuple(accs)
        init = tuple(jnp.full(part_shape, i, dt) for _, i, dt, _, _ in specs)
        accs = lax.fori_loop(0, nkb, body, init)
        return [acc if op is None else op(acc, axis=key_axis, keepdims=True)
                for acc, (_, _, _, op, _) in zip(accs, specs)]

    one_if = lambda c: jnp.where(c, 1.0, 0.0)
    vmax, vmin, cpos, czero = reduce_blocks([
        (lambda s, k0: s, -jnp.inf, F32, jnp.max, jnp.maximum),
        (lambda s, k0: jnp.where(s == -jnp.inf, jnp.inf, s), jnp.inf, F32, jnp.min, jnp.minimum),
        (lambda s, k0: one_if(s > 0.0), 0.0, F32, jnp.sum, jnp.add),
        (lambda s, k0: one_if(s == 0.0), 0.0, F32, jnp.sum, jnp.add)])

    long_row = n_adm > kf
    pos_row = long_row & (cpos >= kf)
    zero_row = long_row & (cpos < kf) & (cpos + czero >= kf)
    neg_row = long_row & (cpos + czero < kf)
    lo0 = jnp.where(pos_row | zero_row, 0.0, vmin - jnp.maximum(1.0, jnp.abs(vmin)))
    cnt_lo0 = jnp.where(pos_row, cpos, jnp.where(zero_row, kf, n_adm))
    hi0 = jnp.where(neg_row, 0.0, vmax)
    cnt_hi0 = jnp.where(neg_row, cpos, 0.0)
    need_zero = jnp.where(zero_row, kf - cpos, 0.0)

    def narrow(lo, hi, cnt_lo, cnt_hi):
        fracs = [(t + 1.0) / (BISECT_PROBES + 1.0) for t in range(BISECT_PROBES)]
        mids = [lo * (1.0 - f) + hi * f for f in fracs]
        for t in range(1, BISECT_PROBES):
            mids[t] = jnp.maximum(mids[t], mids[t - 1])
        codes = [sum(COUNT_RADIX ** u for u in range(t + 1)) for t in range(BISECT_PROBES)]

        def encode(s, k0):
            e = jnp.zeros(s.shape, jnp.int32)
            for mid, code in zip(mids, codes):
                e = jnp.where(s > mid, code, e)
            return e
        (packed,) = reduce_blocks([(encode, 0, jnp.int32, None, jnp.add)])
        cnts = []
        for t in range(BISECT_PROBES):
            digit = packed & (COUNT_RADIX - 1) if t < BISECT_PROBES - 1 else packed
            packed = packed >> COUNT_RADIX.bit_length() - 1
            cnts.append(jnp.sum(digit.astype(F32), axis=key_axis, keepdims=True))
        lo_n, cnt_lo_n, hi_n, cnt_hi_n = lo, cnt_lo, hi, cnt_hi
        for mid, cm in zip(mids, cnts):
            up = cm >= kf
            lo_n = jnp.where(up, mid, lo_n)
            cnt_lo_n = jnp.where(up, cm, cnt_lo_n)
        for mid, cm in zip(mids[::-1], cnts[::-1]):
            dn = cm < kf
            hi_n = jnp.where(dn, mid, hi_n)
            cnt_hi_n = jnp.where(dn, cm, cnt_hi_n)
        return lo_n, hi_n, cnt_lo_n, cnt_hi_n

    def unresolved(cnt_lo):
        return (jnp.max(cnt_lo) > kf).astype(jnp.int32)

    blind = jnp.where(unresolved(cnt_lo0) > 0, BISECT_BLIND_ITERS, 0)
    state = lax.fori_loop(0, blind, lambda i, st: narrow(*st), (lo0, hi0, cnt_lo0, cnt_hi0))

    def cond(st):
        return (st[0] < BISECT_MAX_ITERS) & (st[1] > 0)

    def body(st):
        new = narrow(*st[2:])
        return (st[0] + 1, unresolved(new[2])) + new

    _, _, lo, hi, cnt_lo, cnt_hi = lax.while_loop(
        cond, body, (blind, unresolved(state[2])) + state)

    fix = cnt_lo > kf
    need0 = jnp.where(fix, kf - cnt_hi, 0.0)
    big_idx = 2 ** 30

    def fix_cond(need):
        return jnp.max(need) > 0.0

    def fix_body(need):
        def in_cluster(s):
            return (s > lo) & (s <= hi)
        (mval,) = reduce_blocks([(lambda s, k0: jnp.where(in_cluster(s), s, -jnp.inf), -jnp.inf, F32,
                                  jnp.max, jnp.maximum)])
        (idx,) = reduce_blocks(
            [(lambda s, k0: jnp.where(in_cluster(s) & (s == mval), kpos + k0, big_idx),
              big_idx, jnp.int32, jnp.min, jnp.minimum)])
        active = need > 0.0

        def promote(kb, carry):
            for i in range(kblk // strip):
                s = strip_of(kb, i)
                put_strip(kb, i, jnp.where(active & ((kpos + kb * kblk + i * strip) == idx), jnp.inf, s))
            return carry
        lax.fori_loop(0, nkb, promote, 0)
        return jnp.where(active, need - 1.0, need)

    lax.while_loop(fix_cond, fix_body, need0)

    @pl.when(jnp.max(need_zero) > 0.0)
    def _():
        r = lax.broadcasted_iota(jnp.int32, (kblk, kblk), 0)
        c = lax.broadcasted_iota(jnp.int32, (kblk, kblk), 1)
        tri = ((r >= c) if key_axis == 0 else (r <= c)).astype(BF16)

        def tie_blk(kb, seen):
            s = sc_ref[kb]
            z = (s == 0.0) & zero_row
            zb = one_if(z).astype(BF16)
            if key_axis == 0:
                rank = _dot(tri, zb) + seen
                last = rank[kblk - 1:kblk, :]
            else:
                rank = _dot(zb, tri) + seen
                last = rank[:, kblk - 1:kblk]
            sc_ref[kb] = jnp.where(z & (rank <= need_zero), jnp.inf, s)
            return last
        if even_blocks:
            lax.fori_loop(0, nkb // 2, lambda i, seen: tie_blk(2 * i + 1, tie_blk(2 * i, seen)),
                          jnp.zeros_like(need_zero))
        else:
            lax.fori_loop(0, nkb, tie_blk, jnp.zeros_like(need_zero))

    return jnp.where(fix, hi, lo)


def _dsa_body(qb_ref, qi_ref, miscq_ref, zb_ref, k_ref, vt_ref, misck_ref, o_ref, sc_ref, s_ref, *, n_sel, qblk):
    j = pl.program_id(1)
    nkb = ((j + 1) * qblk + KEY_BLOCK - 1) // KEY_BLOCK
    qi = qi_ref[...]
    q_h = [qi[:, h * IDX_HD:(h + 1) * IDX_HD].astype(BF16) for h in range(IDX_HEADS)]
    w_t = miscq_ref[...].T[MISC_WI:MISC_WI + IDX_HEADS, :]
    qchunk = (j * qblk + lax.broadcasted_iota(jnp.int32, (1, qblk), 1)) // CHUNK
    krow = lax.broadcasted_iota(jnp.int32, (KEY_BLOCK, 1), 0)

    def fold(acc, x, combine):
        for i in range(x.shape[0] // 8):
            acc = combine(acc, x[8 * i:8 * (i + 1)])
        return acc

    def score_blk(kb, carry):
        k0 = pl.multiple_of(kb * KEY_BLOCK, KEY_BLOCK)
        for half in range(KEY_BLOCK // SCORE_STRIP):
            r0 = half * SCORE_STRIP
            ki = misck_ref[pl.ds(k0 + r0, SCORE_STRIP), 0:IDX_HD].astype(BF16)
            acc = jnp.zeros((SCORE_STRIP, qblk), F32)
            for h in range(IDX_HEADS):
                acc = acc + w_t[h:h + 1, :] * jnp.maximum(_dot_nt(ki, q_h[h]), 0.0)
            adm = ((k0 + r0 + krow[:SCORE_STRIP]) // CHUNK) <= qchunk
            sc_ref[kb, r0:r0 + SCORE_STRIP, :] = jnp.where(adm, acc, -jnp.inf)
        return carry

    npairs = (nkb + 1) // 2

    def pair_loop(body, init):
        return lax.fori_loop(0, npairs, lambda i, c: body(2 * i + 1, body(2 * i, c)), init)

    pair_loop(score_blk, 0)

    n_adm = ((qchunk + 1) * CHUNK).astype(F32)
    thr = _topk_threshold(sc_ref, nkb, (KEY_BLOCK, qblk), 0, n_sel, n_adm,
                          even_blocks=qblk % (2 * KEY_BLOCK) == 0)

    qb = qb_ref[...] * (B_HD ** -0.5 * math.log2(math.e))
    qs = jnp.concatenate([jnp.where(_lane_mask(B_WIDTH, h * B_HD, (h + 1) * B_HD), qb, 0.0)
                          for h in range(B_HEADS)], axis=0).astype(BF16)
    wide = B_HEADS * qblk

    def logits_blk(kb, m8):
        k0 = pl.multiple_of(kb * KEY_BLOCK, KEY_BLOCK)
        s = _dot_nt(k_ref[pl.ds(k0, KEY_BLOCK), :].astype(BF16), qs)
        sel = sc_ref[kb] > thr
        s = jnp.where(jnp.concatenate([sel] * B_HEADS, axis=1), s, NEG_BIG)
        s_ref[kb] = s
        return fold(m8, s, jnp.maximum)

    m8 = pair_loop(logits_blk, jnp.full((8, wide), NEG_BIG, F32))
    m = jnp.max(m8, axis=0, keepdims=True)

    def pv_blk(kb, carry):
        l8, accs = carry
        p = jnp.exp2(s_ref[kb] - m)
        pb = p.astype(BF16)
        k0 = pl.multiple_of(kb * KEY_BLOCK, KEY_BLOCK)
        vt = vt_ref[:, pl.ds(k0, KEY_BLOCK)].astype(BF16)
        accs = tuple(accs[h] + _dot(vt[h * B_HD:(h + 1) * B_HD, :], pb[:, h * qblk:(h + 1) * qblk])
                     for h in range(B_HEADS))
        return fold(l8, p, jnp.add), accs

    l8, accs = pair_loop(
        pv_blk, (jnp.zeros((8, wide), F32), tuple(jnp.zeros((B_HD, qblk), F32) for _ in range(B_HEADS))))
    l = jnp.sum(l8, axis=0, keepdims=True)
    o_t = jnp.concatenate([accs[h] / l[:, h * qblk:(h + 1) * qblk] for h in range(B_HEADS)], axis=0)
    o_ref[...] = o_t.T * _silu(zb_ref[...])


def _dsa_prompt(qb, qi, misc, zb, kb, vt, *, bsz, t_len, n_sel, qblk):
    nq = t_len // qblk
    nkb = t_len // KEY_BLOCK
    assert nkb * (KEY_BLOCK // 8) < COUNT_RADIX and nkb % 2 == 0
    qmap = lambda b, j: (b * nq + j, 0)
    kmap = lambda b, j: (b, 0)
    return pl.pallas_call(
        functools.partial(_dsa_body, n_sel=n_sel, qblk=qblk),
        grid=(bsz, nq),
        in_specs=[pl.BlockSpec((qblk, B_WIDTH), qmap),
                  pl.BlockSpec((qblk, IDX_HEADS * IDX_HD), qmap),
                  pl.BlockSpec((qblk, MISC_W), qmap),
                  pl.BlockSpec((qblk, B_WIDTH), qmap),
                  pl.BlockSpec((t_len, B_WIDTH), kmap),
                  pl.BlockSpec((None, B_WIDTH, t_len), lambda b, j: (b, 0, 0)),
                  pl.BlockSpec((t_len, MISC_W), kmap)],
        out_specs=pl.BlockSpec((qblk, B_WIDTH), qmap),
        out_shape=jax.ShapeDtypeStruct((bsz * t_len, B_WIDTH), F32),
        scratch_shapes=[pltpu.VMEM((nkb, KEY_BLOCK, qblk), F32),
                        pltpu.VMEM((nkb, KEY_BLOCK, B_HEADS * qblk), F32)],
        compiler_params=_cparams(("parallel", "arbitrary")),
        name="dsa_prompt",
    )(qb, qi, misc, zb, kb, vt, misc)


def _dsa_dec_body(qb_ref, qi_ref, misc_ref, zb_ref, kn_ref, vn_ref, ck_ref, cv_ref, cki_ref, o_ref,
                  sc_ref, thr_ref, m_ref, l_ref, acc_ref, *, p_len, tq, n_sel, kstep):
    step = pl.program_id(1)
    nsteps = p_len // kstep
    sub = kstep // KEY_BLOCK
    npast = p_len // KEY_BLOCK
    nkb = npast + 1
    pad_rows = KEY_BLOCK - tq
    rows = B_HEADS * tq

    @pl.when(step == 0)
    def _():
        qi = qi_ref[...]
        misc = misc_ref[...]
        q2 = jnp.concatenate([qi[:, h * IDX_HD:(h + 1) * IDX_HD] for h in range(IDX_HEADS)],
                             axis=0).astype(BF16)
        w_h = [misc[:, MISC_WI + h:MISC_WI + h + 1] for h in range(IDX_HEADS)]

        def head_sum(lg):
            acc = jnp.zeros((tq, lg.shape[1]), F32)
            for h in range(IDX_HEADS):
                acc = acc + w_h[h] * jnp.maximum(lg[h * tq:(h + 1) * tq, :], 0.0)
            return acc

        for i in range(nsteps):
            sc = head_sum(_dot(q2, cki_ref[:, i * kstep:(i + 1) * kstep].astype(BF16)))
            for u in range(sub):
                sc_ref[i * sub + u] = sc[:, u * KEY_BLOCK:(u + 1) * KEY_BLOCK]
        lane = lax.broadcasted_iota(jnp.int32, (1, KEY_BLOCK), 1)
        ki_new = jnp.concatenate([misc[:, 0:IDX_HD], jnp.zeros((pad_rows, IDX_HD), F32)], axis=0)
        sc_ref[npast] = jnp.where(lane < tq, head_sum(_dot_nt(q2, ki_new.astype(BF16))), -jnp.inf)
        n_adm = jnp.full((tq, 1), float(p_len + tq), F32)
        thr_ref[...] = _topk_threshold(sc_ref, nkb, (tq, KEY_BLOCK), 1, n_sel, n_adm)
        m_ref[...] = jnp.full((rows, 1), NEG_BIG, F32)
        l_ref[...] = jnp.zeros((rows, 1), F32)
        acc_ref[...] = jnp.zeros((rows, B_HD), F32)

    qb = qb_ref[...] * (B_HD ** -0.5 * math.log2(math.e))
    q_h = [qb[:, h * B_HD:(h + 1) * B_HD].astype(BF16) for h in range(B_HEADS)]
    thr = thr_ref[...]

    def fold_in(logits_of, pv_of, sel):
        s = jnp.concatenate([jnp.where(sel, logits_of(h), NEG_BIG) for h in range(B_HEADS)], axis=0)
        m_old = m_ref[...]
        m_new = jnp.maximum(m_old, jnp.max(s, axis=-1, keepdims=True))
        alpha = jnp.exp2(m_old - m_new)
        p = jnp.exp2(s - m_new)
        pb = p.astype(BF16)
        pv = jnp.concatenate([pv_of(h, pb[h * tq:(h + 1) * tq, :]) for h in range(B_HEADS)], axis=0)
        l_ref[...] = alpha * l_ref[...] + jnp.sum(p, axis=-1, keepdims=True)
        acc_ref[...] = alpha * acc_ref[...] + pv
        m_ref[...] = m_new

    sel = jnp.concatenate([sc_ref[step * sub + u] > thr for u in range(sub)], axis=1)
    fold_in(lambda h: _dot(q_h[h], ck_ref[h].astype(BF16)),
            lambda h, p: _dot_nt(p, cv_ref[h].astype(BF16)), sel)

    @pl.when(step == nsteps - 1)
    def _():
        zpad = jnp.zeros((pad_rows, B_WIDTH), F32)
        kn = jnp.concatenate([kn_ref[...], zpad], axis=0).astype(BF16)
        vn = jnp.concatenate([vn_ref[...], zpad], axis=0).astype(BF16)
        fold_in(lambda h: _dot_nt(q_h[h], kn[:, h * B_HD:(h + 1) * B_HD]),
                lambda h, p: _dot(p, vn[:, h * B_HD:(h + 1) * B_HD]), sc_ref[npast] > thr)
        res = acc_ref[...] / l_ref[...]
        out = jnp.concatenate([res[h * tq:(h + 1) * tq, :] for h in range(B_HEADS)], axis=1)
        o_ref[...] = out * _silu(zb_ref[...])


def _dsa_decode(qb, qi, misc, zb, kb, vb, ck, cv, cki, *, bsz, t_len, p_len, n_sel):
    kstep = DEC_KEYS_PER_STEP
    assert p_len // KEY_BLOCK + 1 < COUNT_RADIX and p_len % kstep == 0
    rows = B_HEADS * t_len
    qmap = lambda b, s: (b, 0)
    cmap = lambda b, s: (b, 0, 0, s)
    ck, cv = (a.transpose(0, 2, 3, 1) for a in (ck, cv))
    cki = cki.transpose(0, 2, 1)
    return pl.pallas_call(
        functools.partial(_dsa_dec_body, p_len=p_len, tq=t_len, n_sel=n_sel, kstep=kstep),
        grid=(bsz, p_len // kstep),
        in_specs=[pl.BlockSpec((t_len, B_WIDTH), qmap),
                  pl.BlockSpec((t_len, IDX_HEADS * IDX_HD), qmap),
                  pl.BlockSpec((t_len, MISC_W), qmap),
                  pl.BlockSpec((t_len, B_WIDTH), qmap),
                  pl.BlockSpec((t_len, B_WIDTH), qmap),
                  pl.BlockSpec((t_len, B_WIDTH), qmap),
                  pl.BlockSpec((None, B_HEADS, B_HD, kstep), cmap),
                  pl.BlockSpec((None, B_HEADS, B_HD, kstep), cmap),
                  pl.BlockSpec((None, IDX_HD, p_len), lambda b, s: (b, 0, 0))],
        out_specs=pl.BlockSpec((t_len, B_WIDTH), qmap),
        out_shape=jax.ShapeDtypeStruct((bsz * t_len, B_WIDTH), F32),
        scratch_shapes=[pltpu.VMEM((p_len // KEY_BLOCK + 1, t_len, KEY_BLOCK), F32),
                        pltpu.VMEM((t_len, 1), F32),
                        pltpu.VMEM((rows, 1), F32),
                        pltpu.VMEM((rows, 1), F32),
                        pltpu.VMEM((rows, B_HD), F32)],
        compiler_params=_cparams(("parallel", "arbitrary")),
        name="dsa_decode",
    )(qb, qi, misc, zb, kb, vb, ck, cv, cki)


def _tail_body(x_ref, oa_ref, ob_ref, qm_ref, zm_ref, mk_ref, mv_ref, w_ref, y_ref):
    qm = qm_ref[...] * (M_HD ** -0.5)
    mk = mk_ref[...].reshape(N_MEM, M_WIDTH).astype(BF16)
    mv = mv_ref[...].reshape(N_MEM, M_WIDTH).astype(BF16)
    acc = _dot(oa_ref[...].astype(BF16), w_ref[0:A_WIDTH, :])
    acc = acc + _dot(ob_ref[...].astype(BF16), w_ref[A_WIDTH:A_WIDTH + B_WIDTH, :])
    om = jnp.zeros(qm.shape, F32)
    for h in range(M_HEADS):
        mask = _lane_mask(M_WIDTH, h * M_HD, (h + 1) * M_HD)
        s = _dot_nt(jnp.where(mask, qm, 0.0).astype(BF16), mk)
        p = jnp.exp(s - jnp.max(s, axis=-1, keepdims=True))
        l = jnp.sum(p, axis=-1, keepdims=True)
        om = om + jnp.where(mask, _dot(p.astype(BF16), mv) / l, 0.0)
    om = om * _silu(zm_ref[...])
    acc = acc + _dot(om.astype(BF16), w_ref[A_WIDTH + B_WIDTH:, :])
    y_ref[...] = x_ref[...] + acc


def _tail(x, oa, ob, qm, zm, mk, mv, w, *, bsz, t_len, tq):
    m, d = x.shape
    nq = t_len // tq
    qmap = lambda b, j: (b * nq + j, 0)
    mem_spec = pl.BlockSpec((None,) + mk.shape[1:], lambda b, j: (b,) + (0,) * (mk.ndim - 1))
    return pl.pallas_call(
        _tail_body,
        grid=(bsz, nq),
        in_specs=[pl.BlockSpec((tq, d), qmap),
                  pl.BlockSpec((tq, A_WIDTH), qmap),
                  pl.BlockSpec((tq, B_WIDTH), qmap),
                  pl.BlockSpec((tq, M_WIDTH), qmap),
                  pl.BlockSpec((tq, M_WIDTH), qmap),
                  mem_spec, mem_spec,
                  pl.BlockSpec(w.shape, lambda b, j: (0, 0))],
        out_specs=pl.BlockSpec((tq, d), qmap),
        out_shape=jax.ShapeDtypeStruct((m, d), F32),
        compiler_params=_cparams(("parallel", "parallel")),
        name="tail",
    )(x, oa, ob, qm, zm, mk, mv, w)


def _prep_weights(g_in, w_in, conv_w, a_log, dt_bias, g_o, g_qb, g_kb, g_ki, g_qm, w_out):
    offs = np.concatenate([[0], np.cumsum(IN_SPLITS)])
    col = lambda i: w_in[:, int(offs[i]):int(offs[i + 1])]
    (w_qkv, w_za, w_ba, w_aa, w_qb, w_kb, w_vb, w_zb, w_qi, w_ki, w_wi, w_qm, w_zm) = [col(i) for i in range(13)]
    d = w_in.shape[0]
    w_misc = jnp.concatenate(
        [w_ki, w_wi, w_ba, w_aa, jnp.zeros((d, MISC_W - MISC_A - A_HEADS), w_in.dtype)], axis=1)
    w_all = jnp.concatenate([w_za, w_qb, w_kb, w_vb, w_zb, w_qi, w_qm, w_zm, w_misc, w_qkv],
                            axis=1).astype(BF16)
    groups = [(A_WIDTH, False, ""), (B_WIDTH, True, ""),
              (B_WIDTH, True, "t"), (B_WIDTH, False, "t"), (B_WIDTH, False, ""),
              (IDX_HEADS * IDX_HD, False, ""), (M_WIDTH, True, ""), (M_WIDTH, False, ""),
              (MISC_W, True, ""),
              (A_WIDTH, False, "conv0"), (A_WIDTH, False, "conv1"), (A_WIDTH, False, "conv2")]
    bd64 = _group_mean_matrix(B_WIDTH, B_HD, B_WIDTH)
    ones256 = jnp.ones((1, B_WIDTH), F32)
    misc_gain = jnp.concatenate([g_ki, jnp.full((IDX_HEADS,), IDX_SCALE, F32),
                                 jnp.ones((MISC_W - MISC_B,), F32)])[None, :]
    misc_nm = (jnp.arange(MISC_W) < IDX_HD).astype(F32)[None, :]
    aux = [bd64, jnp.tile(g_qb, B_HEADS)[None, :], ones256,
           bd64, jnp.tile(g_kb, B_HEADS)[None, :], ones256,
           bd64, jnp.tile(g_qm, M_HEADS)[None, :], ones256,
           _group_mean_matrix(MISC_W, IDX_HD, IDX_HD), misc_gain, misc_nm]
    pad_a = lambda v: jnp.zeros((1, MISC_W), F32).at[0, MISC_A:MISC_A + A_HEADS].set(v)
    return dict(g_in=g_in[None, :], w_all=w_all, groups=groups, aux=aux, conv_w=conv_w,
                avec=pad_a(-jnp.exp(a_log)), dtvec=pad_a(dt_bias), g_o=g_o[None, :],
                w_out=w_out.astype(BF16))


def _layer(x, mem_k, mem_v, conv_buf, s0, past, gdn_chunk, gdn_rows, wp):
    bsz, t_len, d = x.shape
    m = bsz * t_len
    x2 = x.reshape(m, d)
    tm = min(PROJ_ROWS, m)
    heads_last = lambda a_t: a_t.reshape(bsz, B_HEADS, B_HD, t_len).transpose(0, 3, 1, 2)
    cbuf = jnp.concatenate([jnp.zeros((bsz, CARRY_ROWS - (CONV_W - 1), 3 * A_WIDTH), F32), conv_buf], axis=1)
    conv = (cbuf, wp["conv_w"])
    gdn_args = dict(bsz=bsz, t_len=t_len, chunk=gdn_chunk, rows=gdn_rows)
    if past is None:
        (za, qb, kb, vb, zb, qi, qm, zm, misc, qa, ka, va, kt, vt, conv_new) = _proj(
            x2, wp["g_in"], wp["w_all"], wp["groups"], wp["aux"], tm, t_len, conv=conv)
        k_new, v_new = heads_last(kt), heads_last(vt)
        oa, s_new = _gdn(qa, ka, va, za, misc, s0, wp["avec"], wp["dtvec"], wp["g_o"], **gdn_args)
    else:
        groups = [(wd, normed, "") for wd, normed, _ in wp["groups"]]
        (za, qb, kb, vb, zb, qi, qm, zm, misc, qa, ka, va) = _proj(x2, wp["g_in"], wp["w_all"], groups,
                                                                   wp["aux"], tm, m)
        k_new, v_new = (a.reshape(bsz, t_len, B_HEADS, B_HD) for a in (kb, vb))
        oa, s_new, conv_new = _gdn(qa, ka, va, za, misc, s0, wp["avec"], wp["dtvec"], wp["g_o"], conv=conv,
                                   **gdn_args)
    if past is None:
        n_sel = min(TOPK_MAX, t_len // 4)
        ob = _dsa_prompt(qb, qi, misc, zb, kb, vt, bsz=bsz, t_len=t_len, n_sel=n_sel, qblk=DSA_QBLK)
    else:
        ck, cv, cki = past
        p_len = ck.shape[1]
        assert (p_len + t_len - 1) // CHUNK <= p_len // CHUNK and p_len % KEY_BLOCK == 0
        n_sel = min(TOPK_MAX, (p_len + t_len) // 4)
        ob = _dsa_decode(qb, qi, misc, zb, kb, vb, ck, cv, cki, bsz=bsz, t_len=t_len, p_len=p_len,
                         n_sel=n_sel)
    y = _tail(x2, oa, ob, qm, zm, mem_k, mem_v, wp["w_out"], bsz=bsz, t_len=t_len, tq=min(TAIL_ROWS, t_len))
    return (y.reshape(bsz, t_len, d), conv_new, s_new, k_new, v_new,
            misc[:, :IDX_HD].reshape(bsz, t_len, IDX_HD))


def _memory_kv(mem, g_mem, w_mem_kv, g_km):
    bsz, n_mem, d = mem.shape
    groups = [(M_WIDTH, True, "t"), (M_WIDTH, False, "t")]
    aux = [_group_mean_matrix(M_WIDTH, M_HD, M_WIDTH), jnp.tile(g_km, M_HEADS)[None, :],
           jnp.ones((1, M_WIDTH), F32)]
    mk, mv, mk_t, mv_t = _proj(mem.reshape(bsz * n_mem, d), g_mem[None, :], w_mem_kv.astype(BF16),
                               groups, aux, n_mem, n_mem)
    heads_last = lambda a_t: a_t.reshape(bsz, M_HEADS, M_HD, n_mem).transpose(0, 3, 1, 2)
    return (mk.reshape(bsz, n_mem, M_WIDTH), mv.reshape(bsz, n_mem, M_WIDTH), heads_last(mk_t), heads_last(mv_t))


def kernel(x_prompt, x_sample, state_conv_A, state_ssm_A, cache_k_B, cache_v_B, cache_kidx_B, cache_mem_k,
           cache_mem_v, mem_prompt, g_in, w_in, conv_w_A, a_log_A, dt_bias_A, g_o_A, g_q_B, g_k_B, g_kidx_B,
           g_mem, w_mem_kv, g_q_M, g_k_M, w_out):
    depth = w_in.shape[0]
    assert depth == 1
    l = 0
    bp, t_p, _ = x_prompt.shape
    bs, t_s, _ = x_sample.shape
    wp = _prep_weights(g_in[l], w_in[l], conv_w_A[l], a_log_A[l], dt_bias_A[l], g_o_A[l], g_q_B[l], g_k_B[l],
                       g_kidx_B[l], g_q_M[l], w_out[l])
    mk, mv, mk_heads, mv_heads = _memory_kv(mem_prompt, g_mem[l], w_mem_kv[l], g_k_M[l])
    zero_conv = jnp.zeros((bp, CONV_W - 1, 3 * A_WIDTH), F32)
    zero_ssm = jnp.zeros((bp, A_HEADS, A_DK, A_DV), F32)
    yp, c1, s1, k1, v1, ki1 = _layer(x_prompt, mk, mv, zero_conv, zero_ssm, None, CHUNK, GDN_CHUNKS_PER_STEP * CHUNK, wp)
    ys, c2, s2, k2, v2, ki2 = _layer(
        x_sample, cache_mem_k[l], cache_mem_v[l],
        state_conv_A[l], state_ssm_A[l], (cache_k_B[l], cache_v_B[l], cache_kidx_B[l]), t_s, t_s, wp)
    st = lambda a: a[None]
    return (yp, ys, st(c1), st(s1), st(k1), st(v1), st(ki1),
            st(mk_heads), st(mv_heads),
            st(c2), st(s2), st(k2), st(v2), st(ki2))
```

```python
import functools
import math

import numpy as np
import jax
import jax.numpy as jnp
from jax import lax
from jax.experimental import pallas as pl
from jax.experimental.pallas import tpu as pltpu

F32 = jnp.float32
BF16 = jnp.bfloat16
HIGHEST = lax.Precision.HIGHEST

D_MODEL = 1024
CHUNK = 64
A_HEADS = 4
A_DK = 128
A_DV = 128
A_WIDTH = A_HEADS * A_DV
CONV_W = 4
B_HEADS = 4
B_HD = 64
B_WIDTH = B_HEADS * B_HD
IDX_HEADS = 8
IDX_HD = 32
IDX_SCALE = (IDX_HEADS ** -0.5) * (IDX_HD ** -0.5)
TOPK_MAX = 256
Q_BLOCK = 128
N_MEM = 256
M_HEADS = 4
M_HD = 64
M_WIDTH = M_HEADS * M_HD
EPS = 1e-6
IN_SPLITS = (3 * A_WIDTH, A_WIDTH, A_HEADS, A_HEADS,
             B_WIDTH, B_WIDTH, B_WIDTH, B_WIDTH, IDX_HEADS * IDX_HD, IDX_HD, IDX_HEADS,
             M_WIDTH, M_WIDTH)

LANES = 128
KEY_BLOCK = 256
DSA_QBLK = 512
GDN_CHUNKS_PER_STEP = 8
TAIL_ROWS = 512
PROJ_ROWS = 512
DEC_KEYS_PER_STEP = 4096
SCORE_STRIP = 128
MISC_W = LANES
MISC_WI = IDX_HD
MISC_B = IDX_HD + IDX_HEADS
MISC_A = MISC_B + A_HEADS
VMEM_LIMIT = 48 * 1024 * 1024
NEG_BIG = -1e30
BISECT_PROBES = 2
COUNT_RADIX = 1024
BISECT_BLIND_ITERS = 10
BISECT_MAX_ITERS = 26


def _cparams(sem):
    return pltpu.CompilerParams(dimension_semantics=sem, vmem_limit_bytes=VMEM_LIMIT)


def _dot(a, b):
    return jnp.dot(a, b, preferred_element_type=F32)


def _dot_nt(a, b, precision=None):
    return lax.dot_general(a, b, (((1,), (1,)), ((), ())), precision=precision,
                           preferred_element_type=F32)


def _split2(x):
    hi = x.astype(BF16)
    return hi, (x - hi.astype(F32)).astype(BF16)


def _mm_split(a, b):
    ah, al = a
    bh, _ = b
    n = ah.shape[0]
    t = _dot(jnp.concatenate([ah, al], axis=0), bh)
    return t[:n] + t[n:]


def _silu(x):
    return x * jax.nn.sigmoid(x)


def _lane_mask(width, lo, hi):
    lane = lax.broadcasted_iota(jnp.int32, (1, width), 1)
    return (lane >= lo) & (lane < hi)


CARRY_ROWS = 8


def _conv_silu_norm(xp_ref, raw, cw, rows, part):
    first = CARRY_ROWS - (CONV_W - 1)
    xp_ref[CARRY_ROWS:CARRY_ROWS + rows, :] = raw
    y = xp_ref[first:first + rows, :] * cw[0:1, :]
    for j in range(1, CONV_W):
        y = y + xp_ref[first + j:first + j + rows, :] * cw[j:j + 1, :]
    y = _silu(y)
    conv_new = xp_ref[rows + first:rows + CARRY_ROWS, :]
    xp_ref[0:CARRY_ROWS, :] = xp_ref[rows:rows + CARRY_ROWS, :]
    if part == 2:
        return y, conv_new
    scale = A_DK ** -0.5 if part == 0 else 1.0
    heads = []
    for h in range(A_HEADS):
        yh = y[:, h * A_DK:(h + 1) * A_DK]
        heads.append(yh * (lax.rsqrt(jnp.sum(yh * yh, axis=-1, keepdims=True) + EPS) * scale))
    return jnp.concatenate(heads, axis=1), conv_new


def _conv_part(extra):
    return int(extra[len("conv"):]) if extra.startswith("conv") else None


def _proj_body(*refs, groups, tiles):
    x_ref, g_ref, w_ref = refs[0], refs[1], refs[2]
    n_aux = 3 * sum(1 for _, normed, _ in groups if normed)
    aux = refs[3:3 + n_aux]
    pos = 3 + n_aux
    n_conv = sum(1 for _, _, extra in groups if _conv_part(extra) is not None)
    if n_conv:
        cbuf_ref, cw_ref = refs[pos], refs[pos + 1]
        pos += 2
    outs = refs[pos:pos + len(groups)]
    pos += len(groups)
    n_t = sum(1 for _, _, extra in groups if extra == "t")
    outs_t = list(refs[pos:pos + n_t])
    pos += n_t
    if n_conv:
        convnew_ref = refs[pos]
        xp_refs = refs[pos + 1:pos + 1 + n_conv]

        @pl.when(pl.program_id(0) % tiles == 0)
        def _():
            for part, xp_ref in enumerate(xp_refs):
                xp_ref[0:CARRY_ROWS, :] = cbuf_ref[:, part * A_WIDTH:(part + 1) * A_WIDTH]
    x = x_ref[...]
    r = lax.rsqrt(jnp.mean(x * x, axis=-1, keepdims=True) + EPS)
    hb = (x * g_ref[...]).astype(BF16)
    off = 0
    ai = 0
    for (width, normed, extra), o_ref in zip(groups, outs):
        y = _dot(hb, w_ref[:, off:off + width]) * r
        part = _conv_part(extra)
        if part is not None:
            cols = slice(part * A_WIDTH, (part + 1) * A_WIDTH)
            y, conv_new = _conv_silu_norm(xp_refs[part], y, cw_ref[:, cols], y.shape[0], part)
            convnew_ref[:, cols] = conv_new
        if normed:
            bd_ref, gain_ref, nm_ref = aux[ai], aux[ai + 1], aux[ai + 2]
            ai += 3
            sq = y * y
            hi = sq.astype(BF16)
            lo = (sq - hi.astype(F32)).astype(BF16)
            ms = _dot(hi, bd_ref[...]) + _dot(lo, bd_ref[...])
            scale = jnp.where(nm_ref[...] > 0.0, lax.rsqrt(ms + EPS), 1.0)
            y = y * scale * gain_ref[...]
        o_ref[...] = y
        if extra == "t":
            outs_t.pop(0)[...] = y.T
        off += width


def _group_mean_matrix(width, group, n_lanes):
    i = np.arange(width)
    m = ((i[:, None] // group) == (i[None, :] // group)) & (i[:, None] < n_lanes) & (i[None, :] < n_lanes)
    return jnp.asarray(m.astype(np.float32) / group, dtype=BF16)


def _proj(x, g, w, groups, aux, tm, t_len, conv=None):
    m, d = x.shape
    nw = w.shape[1]
    tiles = t_len // tm
    bsz = m // t_len
    in_specs = [pl.BlockSpec((tm, d), lambda i: (i, 0)),
                pl.BlockSpec((1, d), lambda i: (0, 0)),
                pl.BlockSpec((d, nw), lambda i: (0, 0))]
    for a in aux:
        in_specs.append(pl.BlockSpec(a.shape, lambda i: (0, 0)))
    out_shape = [jax.ShapeDtypeStruct((m, wd), F32) for wd, _, _ in groups]
    out_specs = [pl.BlockSpec((tm, wd), lambda i: (i, 0)) for wd, _, _ in groups]
    for wd, _, extra in groups:
        if extra == "t":
            out_shape.append(jax.ShapeDtypeStruct((bsz, wd, t_len), F32))
            out_specs.append(pl.BlockSpec((None, wd, tm), lambda i: (i // tiles, 0, i % tiles)))
    args = [x, g, w, *aux]
    scratch = []
    if conv is not None:
        cbuf, cw = conv
        wd = cw.shape[1]
        in_specs += [pl.BlockSpec((None, CARRY_ROWS, wd), lambda i: (i // tiles, 0, 0)),
                     pl.BlockSpec(cw.shape, lambda i: (0, 0))]
        args += [cbuf, cw]
        out_shape.append(jax.ShapeDtypeStruct((bsz, CONV_W - 1, wd), F32))
        out_specs.append(pl.BlockSpec((None, CONV_W - 1, wd), lambda i: (i // tiles, 0, 0)))
        scratch += [pltpu.VMEM((tm + CARRY_ROWS, A_WIDTH), F32) for _ in range(wd // A_WIDTH)]
    return pl.pallas_call(
        functools.partial(_proj_body, groups=tuple(groups), tiles=tiles),
        grid=(m // tm,),
        in_specs=in_specs, out_specs=out_specs, out_shape=out_shape, scratch_shapes=scratch,
        compiler_params=_cparams(("arbitrary",) if conv is not None else ("parallel",)),
        name="proj",
    )(*args)


def _gdn_body(*refs, chunk, rows, fused_conv):
    if fused_conv:
        (q_ref, k_ref, v_ref, za_ref, misc_ref, s0_ref, avec_ref, dtvec_ref, go_ref, cbuf_ref, cw_ref,
         o_ref, sfin_ref, convnew_ref, g_sc, b_sc, s_sc, *conv_sc) = refs
    else:
        (q_ref, k_ref, v_ref, za_ref, misc_ref, s0_ref, avec_ref, dtvec_ref, go_ref,
         o_ref, sfin_ref, g_sc, b_sc, s_sc) = refs
    t = pl.program_id(1)
    c = chunk

    @pl.when(t == 0)
    def _():
        s_sc[...] = s0_ref[0]

    if fused_conv:
        xp_refs, act_refs = conv_sc[:3], conv_sc[3:]

        @pl.when(t == 0)
        def _():
            for part, xp_ref in enumerate(xp_refs):
                xp_ref[0:CARRY_ROWS, :] = cbuf_ref[:, part * A_WIDTH:(part + 1) * A_WIDTH]
        for part, raw_ref in enumerate((q_ref, k_ref, v_ref)):
            cols = slice(part * A_WIDTH, (part + 1) * A_WIDTH)
            act, conv_new = _conv_silu_norm(xp_refs[part], raw_ref[...], cw_ref[:, cols], rows, part)
            convnew_ref[:, cols] = conv_new
            act_refs[part][...] = act
        q_sc, k_sc, v_sc = act_refs
    else:
        q_sc, k_sc, v_sc = q_ref, k_ref, v_ref
    misc = misc_ref[...]
    b_sc[...] = jax.nn.sigmoid(misc)
    g_sc[...] = avec_ref[...] * jax.nn.softplus(misc + dtvec_ref[...])

    ri = lax.broadcasted_iota(jnp.int32, (c, c), 0)
    ci = lax.broadcasted_iota(jnp.int32, (c, c), 1)
    tri = ri >= ci
    strict = ri > ci
    eye_f = (ri == ci).astype(F32)
    tri3 = (lax.broadcasted_iota(jnp.int32, (c, 3 * c), 0)
            >= lax.broadcasted_iota(jnp.int32, (c, 3 * c), 1) % c).astype(BF16)
    go = go_ref[...]
    n_chunks = rows // c

    problems = [(ic, h) for ic in range(n_chunks) for h in range(A_HEADS)]
    gc_all, gc_t = [], []
    for ic in range(n_chunks):
        g = g_sc[ic * c:(ic + 1) * c, :]
        g1 = g.astype(BF16)
        r1 = g - g1.astype(F32)
        g2 = r1.astype(BF16)
        g3 = (r1 - g2.astype(F32)).astype(BF16)
        gc_all.append(_dot(tri3, jnp.concatenate([g1, g2, g3], axis=0)))
    for ic in range(n_chunks):
        gc_t.append(gc_all[ic].T)

    def load(ref, ic, h):
        return ref[ic * c:(ic + 1) * c, h * A_DK:(h + 1) * A_DK]

    col = lambda a, lane: a[:, lane:lane + 1]
    k_bf = [load(k_sc, ic, h).astype(BF16) for ic, h in problems]
    kb = [load(k_sc, ic, h) * col(b_sc[ic * c:(ic + 1) * c, :], MISC_B + h) for ic, h in problems]
    kk = [_dot_nt(kb[i].astype(BF16), k_bf[i]) for i in range(len(problems))]
    qk_raw = [_dot_nt(load(q_sc, ic, h).astype(BF16), k_bf[i]) for i, (ic, h) in enumerate(problems)]
    decay = []
    for ic, h in problems:
        diff = col(gc_all[ic], MISC_A + h) - gc_t[ic][MISC_A + h:MISC_A + h + 1, :]
        decay.append(jnp.where(tri, jnp.exp(jnp.where(tri, diff, 0.0)), 0.0))
    lmat = [jnp.where(strict, kk[i] * decay[i], 0.0) for i in range(len(problems))]
    qk = [(qk_raw[i] * decay[i]).astype(BF16) for i in range(len(problems))]
    x_inv = [eye_f - m for m in lmat]
    p_split = [_split2(m) for m in lmat]
    for _ in range(int(math.log2(c)) - 1):
        p_split = [_split2(_mm_split(ps, ps)) for ps in p_split]
        x_inv = [x + _mm_split(_split2(x), ps) for x, ps in zip(x_inv, p_split)]
    sol = []
    for i, (ic, h) in enumerate(problems):
        egc = jnp.exp(col(gc_all[ic], MISC_A + h))
        beta = col(b_sc[ic * c:(ic + 1) * c, :], MISC_B + h)
        rhs = jnp.concatenate([load(v_sc, ic, h) * beta, kb[i] * egc], axis=-1).astype(BF16)
        sol.append(_mm_split(_split2(x_inv[i]), (rhs, None)))
    pre = []
    for i, (ic, h) in enumerate(problems):
        gc = col(gc_all[ic], MISC_A + h)
        g_last = gc[c - 1:c, :]
        k_dec_t = (load(k_sc, ic, h) * jnp.exp(g_last - gc)).T.astype(BF16)
        wq = jnp.concatenate([sol[i][:, A_DV:], load(q_sc, ic, h) * jnp.exp(gc)], axis=0).astype(BF16)
        pre.append((sol[i][:, :A_DV], wq, k_dec_t, jnp.exp(g_last)))

    heads = range(A_HEADS)
    s_cur = [s_sc[h] for h in heads]
    for ic in range(n_chunks):
        pr = [pre[ic * A_HEADS + h] for h in heads]
        ws = [_dot(pr[h][1], s_cur[h].astype(BF16)) for h in heads]
        ub = [(pr[h][0] - ws[h][:c]).astype(BF16) for h in heads]
        o = [ws[h][c:] + _dot(qk[ic * A_HEADS + h], ub[h]) for h in heads]
        s_cur = [s_cur[h] * pr[h][3] + _dot(pr[h][2], ub[h]) for h in heads]
        for h in heads:
            on = o[h] * lax.rsqrt(jnp.mean(o[h] * o[h], axis=-1, keepdims=True) + EPS) * go
            o_ref[ic * c:(ic + 1) * c, h * A_DV:(h + 1) * A_DV] = on * _silu(load(za_ref, ic, h))
    for h in heads:
        s_sc[h] = s_cur[h]
    sfin_ref[0] = s_sc[...]


def _gdn(q, k, v, za, misc, s0, avec, dtvec, g_o, *, bsz, t_len, chunk, rows, conv=None):
    nt = t_len // rows
    w3 = 3 * A_WIDTH
    row_map = lambda b, t: (b * nt + t, 0)
    const2 = lambda b, t: (0, 0)
    state_spec = pl.BlockSpec((1, A_HEADS, A_DK, A_DV), lambda b, t: (b, 0, 0, 0))
    in_specs = [pl.BlockSpec((rows, A_WIDTH), row_map)] * 4 + [
                pl.BlockSpec((rows, MISC_W), row_map),
                state_spec,
                pl.BlockSpec((1, MISC_W), const2),
                pl.BlockSpec((1, MISC_W), const2),
                pl.BlockSpec((1, A_DV), const2)]
    args = [q, k, v, za, misc, s0, avec, dtvec, g_o]
    out_specs = [pl.BlockSpec((rows, A_WIDTH), row_map), state_spec]
    out_shape = [jax.ShapeDtypeStruct((bsz * t_len, A_WIDTH), F32),
                 jax.ShapeDtypeStruct((bsz, A_HEADS, A_DK, A_DV), F32)]
    scratch = [pltpu.VMEM((rows, MISC_W), F32),
               pltpu.VMEM((rows, MISC_W), F32),
               pltpu.VMEM((A_HEADS, A_DK, A_DV), F32)]
    if conv is not None:
        cbuf, cw = conv
        in_specs += [pl.BlockSpec((None, CARRY_ROWS, w3), lambda b, t: (b, 0, 0)),
                     pl.BlockSpec((CONV_W, w3), const2)]
        args += [cbuf, cw]
        out_specs.append(pl.BlockSpec((None, CONV_W - 1, w3), lambda b, t: (b, 0, 0)))
        out_shape.append(jax.ShapeDtypeStruct((bsz, CONV_W - 1, w3), F32))
        scratch += [pltpu.VMEM((rows + CARRY_ROWS, A_WIDTH), F32)] * 3 + [pltpu.VMEM((rows, A_WIDTH), F32)] * 3
    return pl.pallas_call(
        functools.partial(_gdn_body, chunk=chunk, rows=rows, fused_conv=conv is not None),
        grid=(bsz, nt),
        in_specs=in_specs, out_specs=out_specs, out_shape=out_shape, scratch_shapes=scratch,
        compiler_params=_cparams(("parallel", "arbitrary")),
        name="gdn",
    )(*args)


def _topk_threshold(sc_ref, nkb, shape, key_axis, n_sel, n_adm, even_blocks=False):
    kf = float(n_sel)
    kblk = shape[key_axis]

    if key_axis == 0:
        strip = min(kblk, 32)
        part_shape = (strip, shape[1])
        strip_of = lambda kb, i: sc_ref[kb, i * strip:(i + 1) * strip, :]

        def put_strip(kb, i, v):
            sc_ref[kb, i * strip:(i + 1) * strip, :] = v
    else:
        strip = kblk
        part_shape = shape
        strip_of = lambda kb, i: sc_ref[kb]

        def put_strip(kb, i, v):
            sc_ref[kb] = v
    kpos = lax.broadcasted_iota(jnp.int32, part_shape, key_axis)

    def reduce_blocks(specs):
        def body(kb, accs):
            accs = list(accs)
            for i in range(kblk // strip):
                s = strip_of(kb, i)
                k0 = kb * kblk + i * strip
                accs = [cmb(acc, fn(s, k0)) for acc, (fn, _, _, _, cmb) in zip(accs, specs)]
            return tuple(accs)
        init = tuple(jnp.full(part_shape, i, dt) for _, i, dt, _, _ in specs)
        accs = lax.fori_loop(0, nkb, body, init)
        return [acc if op is None else op(acc, axis=key_axis, keepdims=True)
                for acc, (_, _, _, op, _) in zip(accs, specs)]

    one_if = lambda c: jnp.where(c, 1.0, 0.0)
    vmax, vmin, cpos, czero = reduce_blocks([
        (lambda s, k0: s, -jnp.inf, F32, jnp.max, jnp.maximum),
        (lambda s, k0: jnp.where(s == -jnp.inf, jnp.inf, s), jnp.inf, F32, jnp.min, jnp.minimum),
        (lambda s, k0: one_if(s > 0.0), 0.0, F32, jnp.sum, jnp.add),
        (lambda s, k0: one_if(s == 0.0), 0.0, F32, jnp.sum, jnp.add)])

    long_row = n_adm > kf
    pos_row = long_row & (cpos >= kf)
    zero_row = long_row & (cpos < kf) & (cpos + czero >= kf)
    neg_row = long_row & (cpos + czero < kf)
    lo0 = jnp.where(pos_row | zero_row, 0.0, vmin - jnp.maximum(1.0, jnp.abs(vmin)))
    cnt_lo0 = jnp.where(pos_row, cpos, jnp.where(zero_row, kf, n_adm))
    hi0 = jnp.where(neg_row, 0.0, vmax)
    cnt_hi0 = jnp.where(neg_row, cpos, 0.0)
    need_zero = jnp.where(zero_row, kf - cpos, 0.0)

    def narrow(lo, hi, cnt_lo, cnt_hi):
        fracs = [(t + 1.0) / (BISECT_PROBES + 1.0) for t in range(BISECT_PROBES)]
        mids = [lo * (1.0 - f) + hi * f for f in fracs]
        for t in range(1, BISECT_PROBES):
            mids[t] = jnp.maximum(mids[t], mids[t - 1])
        codes = [sum(COUNT_RADIX ** u for u in range(t + 1)) for t in range(BISECT_PROBES)]

        def encode(s, k0):
            e = jnp.zeros(s.shape, jnp.int32)
            for mid, code in zip(mids, codes):
                e = jnp.where(s > mid, code, e)
            return e
        (packed,) = reduce_blocks([(encode, 0, jnp.int32, None, jnp.add)])
        cnts = []
        for t in range(BISECT_PROBES):
            digit = packed & (COUNT_RADIX - 1) if t < BISECT_PROBES - 1 else packed
            packed = packed >> COUNT_RADIX.bit_length() - 1
            cnts.append(jnp.sum(digit.astype(F32), axis=key_axis, keepdims=True))
        lo_n, cnt_lo_n, hi_n, cnt_hi_n = lo, cnt_lo, hi, cnt_hi
        for mid, cm in zip(mids, cnts):
            up = cm >= kf
            lo_n = jnp.where(up, mid, lo_n)
            cnt_lo_n = jnp.where(up, cm, cnt_lo_n)
        for mid, cm in zip(mids[::-1], cnts[::-1]):
            dn = cm < kf
            hi_n = jnp.where(dn, mid, hi_n)
            cnt_hi_n = jnp.where(dn, cm, cnt_hi_n)
        return lo_n, hi_n, cnt_lo_n, cnt_hi_n

    def unresolved(cnt_lo):
        return (jnp.max(cnt_lo) > kf).astype(jnp.int32)

    blind = jnp.where(unresolved(cnt_lo0) > 0, BISECT_BLIND_ITERS, 0)
    state = lax.fori_loop(0, blind, lambda i, st: narrow(*st), (lo0, hi0, cnt_lo0, cnt_hi0))

    def cond(st):
        return (st[0] < BISECT_MAX_ITERS) & (st[1] > 0)

    def body(st):
        new = narrow(*st[2:])
        return (st[0] + 1, unresolved(new[2])) + new

    _, _, lo, hi, cnt_lo, cnt_hi = lax.while_loop(
        cond, body, (blind, unresolved(state[2])) + state)

    fix = cnt_lo > kf
    need0 = jnp.where(fix, kf - cnt_hi, 0.0)
    big_idx = 2 ** 30

    def fix_cond(need):
        return jnp.max(need) > 0.0

    def fix_body(need):
        def in_cluster(s):
            return (s > lo) & (s <= hi)
        (mval,) = reduce_blocks([(lambda s, k0: jnp.where(in_cluster(s), s, -jnp.inf), -jnp.inf, F32,
                                  jnp.max, jnp.maximum)])
        (idx,) = reduce_blocks(
            [(lambda s, k0: jnp.where(in_cluster(s) & (s == mval), kpos + k0, big_idx),
              big_idx, jnp.int32, jnp.min, jnp.minimum)])
        active = need > 0.0

        def promote(kb, carry):
            for i in range(kblk // strip):
                s = strip_of(kb, i)
                put_strip(kb, i, jnp.where(active & ((kpos + kb * kblk + i * strip) == idx), jnp.inf, s))
            return carry
        lax.fori_loop(0, nkb, promote, 0)
        return jnp.where(active, need - 1.0, need)

    lax.while_loop(fix_cond, fix_body, need0)

    @pl.when(jnp.max(need_zero) > 0.0)
    def _():
        r = lax.broadcasted_iota(jnp.int32, (kblk, kblk), 0)
        c = lax.broadcasted_iota(jnp.int32, (kblk, kblk), 1)
        tri = ((r >= c) if key_axis == 0 else (r <= c)).astype(BF16)

        def tie_blk(kb, seen):
            s = sc_ref[kb]
            z = (s == 0.0) & zero_row
            zb = one_if(z).astype(BF16)
            if key_axis == 0:
                rank = _dot(tri, zb) + seen
                last = rank[kblk - 1:kblk, :]
            else:
                rank = _dot(zb, tri) + seen
                last = rank[:, kblk - 1:kblk]
            sc_ref[kb] = jnp.where(z & (rank <= need_zero), jnp.inf, s)
            return last
        if even_blocks:
            lax.fori_loop(0, nkb // 2, lambda i, seen: tie_blk(2 * i + 1, tie_blk(2 * i, seen)),
                          jnp.zeros_like(need_zero))
        else:
            lax.fori_loop(0, nkb, tie_blk, jnp.zeros_like(need_zero))

    return jnp.where(fix, hi, lo)


def _dsa_body(qb_ref, qi_ref, miscq_ref, zb_ref, k_ref, vt_ref, misck_ref, o_ref, sc_ref, s_ref, *, n_sel, qblk):
    j = pl.program_id(1)
    nkb = ((j + 1) * qblk + KEY_BLOCK - 1) // KEY_BLOCK
    qi = qi_ref[...]
    q_h = [qi[:, h * IDX_HD:(h + 1) * IDX_HD].astype(BF16) for h in range(IDX_HEADS)]
    w_t = miscq_ref[...].T[MISC_WI:MISC_WI + IDX_HEADS, :]
    qchunk = (j * qblk + lax.broadcasted_iota(jnp.int32, (1, qblk), 1)) // CHUNK
    krow = lax.broadcasted_iota(jnp.int32, (KEY_BLOCK, 1), 0)

    def fold(acc, x, combine):
        for i in range(x.shape[0] // 8):
            acc = combine(acc, x[8 * i:8 * (i + 1)])
        return acc

    def score_blk(kb, carry):
        k0 = pl.multiple_of(kb * KEY_BLOCK, KEY_BLOCK)
        for half in range(KEY_BLOCK // SCORE_STRIP):
            r0 = half * SCORE_STRIP
            ki = misck_ref[pl.ds(k0 + r0, SCORE_STRIP), 0:IDX_HD].astype(BF16)
            acc = jnp.zeros((SCORE_STRIP, qblk), F32)
            for h in range(IDX_HEADS):
                acc = acc + w_t[h:h + 1, :] * jnp.maximum(_dot_nt(ki, q_h[h]), 0.0)
            adm = ((k0 + r0 + krow[:SCORE_STRIP]) // CHUNK) <= qchunk
            sc_ref[kb, r0:r0 + SCORE_STRIP, :] = jnp.where(adm, acc, -jnp.inf)
        return carry

    npairs = (nkb + 1) // 2

    def pair_loop(body, init):
        return lax.fori_loop(0, npairs, lambda i, c: body(2 * i + 1, body(2 * i, c)), init)

    pair_loop(score_blk, 0)

    n_adm = ((qchunk + 1) * CHUNK).astype(F32)
    thr = _topk_threshold(sc_ref, nkb, (KEY_BLOCK, qblk), 0, n_sel, n_adm,
                          even_blocks=qblk % (2 * KEY_BLOCK) == 0)

    qb = qb_ref[...] * (B_HD ** -0.5 * math.log2(math.e))
    qs = jnp.concatenate([jnp.where(_lane_mask(B_WIDTH, h * B_HD, (h + 1) * B_HD), qb, 0.0)
                          for h in range(B_HEADS)], axis=0).astype(BF16)
    wide = B_HEADS * qblk

    def logits_blk(kb, m8):
        k0 = pl.multiple_of(kb * KEY_BLOCK, KEY_BLOCK)
        s = _dot_nt(k_ref[pl.ds(k0, KEY_BLOCK), :].astype(BF16), qs)
        sel = sc_ref[kb] > thr
        s = jnp.where(jnp.concatenate([sel] * B_HEADS, axis=1), s, NEG_BIG)
        s_ref[kb] = s
        return fold(m8, s, jnp.maximum)

    m8 = pair_loop(logits_blk, jnp.full((8, wide), NEG_BIG, F32))
    m = jnp.max(m8, axis=0, keepdims=True)

    def pv_blk(kb, carry):
        l8, accs = carry
        p = jnp.exp2(s_ref[kb] - m)
        pb = p.astype(BF16)
        k0 = pl.multiple_of(kb * KEY_BLOCK, KEY_BLOCK)
        vt = vt_ref[:, pl.ds(k0, KEY_BLOCK)].astype(BF16)
        accs = tuple(accs[h] + _dot(vt[h * B_HD:(h + 1) * B_HD, :], pb[:, h * qblk:(h + 1) * qblk])
                     for h in range(B_HEADS))
        return fold(l8, p, jnp.add), accs

    l8, accs = pair_loop(
        pv_blk, (jnp.zeros((8, wide), F32), tuple(jnp.zeros((B_HD, qblk), F32) for _ in range(B_HEADS))))
    l = jnp.sum(l8, axis=0, keepdims=True)
    o_t = jnp.concatenate([accs[h] / l[:, h * qblk:(h + 1) * qblk] for h in range(B_HEADS)], axis=0)
    o_ref[...] = o_t.T * _silu(zb_ref[...])


def _dsa_prompt(qb, qi, misc, zb, kb, vt, *, bsz, t_len, n_sel, qblk):
    nq = t_len // qblk
    nkb = t_len // KEY_BLOCK
    assert nkb * (KEY_BLOCK // 8) < COUNT_RADIX and nkb % 2 == 0
    qmap = lambda b, j: (b * nq + j, 0)
    kmap = lambda b, j: (b, 0)
    return pl.pallas_call(
        functools.partial(_dsa_body, n_sel=n_sel, qblk=qblk),
        grid=(bsz, nq),
        in_specs=[pl.BlockSpec((qblk, B_WIDTH), qmap),
                  pl.BlockSpec((qblk, IDX_HEADS * IDX_HD), qmap),
                  pl.BlockSpec((qblk, MISC_W), qmap),
                  pl.BlockSpec((qblk, B_WIDTH), qmap),
                  pl.BlockSpec((t_len, B_WIDTH), kmap),
                  pl.BlockSpec((None, B_WIDTH, t_len), lambda b, j: (b, 0, 0)),
                  pl.BlockSpec((t_len, MISC_W), kmap)],
        out_specs=pl.BlockSpec((qblk, B_WIDTH), qmap),
        out_shape=jax.ShapeDtypeStruct((bsz * t_len, B_WIDTH), F32),
        scratch_shapes=[pltpu.VMEM((nkb, KEY_BLOCK, qblk), F32),
                        pltpu.VMEM((nkb, KEY_BLOCK, B_HEADS * qblk), F32)],
        compiler_params=_cparams(("parallel", "arbitrary")),
        name="dsa_prompt",
    )(qb, qi, misc, zb, kb, vt, misc)


def _dsa_dec_body(qb_ref, qi_ref, misc_ref, zb_ref, kn_ref, vn_ref, ck_ref, cv_ref, cki_ref, o_ref,
                  sc_ref, thr_ref, m_ref, l_ref, acc_ref, *, p_len, tq, n_sel, kstep):
    step = pl.program_id(1)
    nsteps = p_len // kstep
    sub = kstep // KEY_BLOCK
    npast = p_len // KEY_BLOCK
    nkb = npast + 1
    pad_rows = KEY_BLOCK - tq
    rows = B_HEADS * tq

    @pl.when(step == 0)
    def _():
        qi = qi_ref[...]
        misc = misc_ref[...]
        q2 = jnp.concatenate([qi[:, h * IDX_HD:(h + 1) * IDX_HD] for h in range(IDX_HEADS)],
                             axis=0).astype(BF16)
        w_h = [misc[:, MISC_WI + h:MISC_WI + h + 1] for h in range(IDX_HEADS)]

        def head_sum(lg):
            acc = jnp.zeros((tq, lg.shape[1]), F32)
            for h in range(IDX_HEADS):
                acc = acc + w_h[h] * jnp.maximum(lg[h * tq:(h + 1) * tq, :], 0.0)
            return acc

        for i in range(nsteps):
            sc = head_sum(_dot(q2, cki_ref[:, i * kstep:(i + 1) * kstep].astype(BF16)))
            for u in range(sub):
                sc_ref[i * sub + u] = sc[:, u * KEY_BLOCK:(u + 1) * KEY_BLOCK]
        lane = lax.broadcasted_iota(jnp.int32, (1, KEY_BLOCK), 1)
        ki_new = jnp.concatenate([misc[:, 0:IDX_HD], jnp.zeros((pad_rows, IDX_HD), F32)], axis=0)
        sc_ref[npast] = jnp.where(lane < tq, head_sum(_dot_nt(q2, ki_new.astype(BF16))), -jnp.inf)
        n_adm = jnp.full((tq, 1), float(p_len + tq), F32)
        thr_ref[...] = _topk_threshold(sc_ref, nkb, (tq, KEY_BLOCK), 1, n_sel, n_adm)
        m_ref[...] = jnp.full((rows, 1), NEG_BIG, F32)
        l_ref[...] = jnp.zeros((rows, 1), F32)
        acc_ref[...] = jnp.zeros((rows, B_HD), F32)

    qb = qb_ref[...] * (B_HD ** -0.5 * math.log2(math.e))
    q_h = [qb[:, h * B_HD:(h + 1) * B_HD].astype(BF16) for h in range(B_HEADS)]
    thr = thr_ref[...]

    def fold_in(logits_of, pv_of, sel):
        s = jnp.concatenate([jnp.where(sel, logits_of(h), NEG_BIG) for h in range(B_HEADS)], axis=0)
        m_old = m_ref[...]
        m_new = jnp.maximum(m_old, jnp.max(s, axis=-1, keepdims=True))
        alpha = jnp.exp2(m_old - m_new)
        p = jnp.exp2(s - m_new)
        pb = p.astype(BF16)
        pv = jnp.concatenate([pv_of(h, pb[h * tq:(h + 1) * tq, :]) for h in range(B_HEADS)], axis=0)
        l_ref[...] = alpha * l_ref[...] + jnp.sum(p, axis=-1, keepdims=True)
        acc_ref[...] = alpha * acc_ref[...] + pv
        m_ref[...] = m_new

    sel = jnp.concatenate([sc_ref[step * sub + u] > thr for u in range(sub)], axis=1)
    fold_in(lambda h: _dot(q_h[h], ck_ref[h].astype(BF16)),
            lambda h, p: _dot_nt(p, cv_ref[h].astype(BF16)), sel)

    @pl.when(step == nsteps - 1)
    def _():
        zpad = jnp.zeros((pad_rows, B_WIDTH), F32)
        kn = jnp.concatenate([kn_ref[...], zpad], axis=0).astype(BF16)
        vn = jnp.concatenate([vn_ref[...], zpad], axis=0).astype(BF16)
        fold_in(lambda h: _dot_nt(q_h[h], kn[:, h * B_HD:(h + 1) * B_HD]),
                lambda h, p: _dot(p, vn[:, h * B_HD:(h + 1) * B_HD]), sc_ref[npast] > thr)
        res = acc_ref[...] / l_ref[...]
        out = jnp.concatenate([res[h * tq:(h + 1) * tq, :] for h in range(B_HEADS)], axis=1)
        o_ref[...] = out * _silu(zb_ref[...])


def _dsa_decode(qb, qi, misc, zb, kb, vb, ck, cv, cki, *, bsz, t_len, p_len, n_sel):
    kstep = DEC_KEYS_PER_STEP
    assert p_len // KEY_BLOCK + 1 < COUNT_RADIX and p_len % kstep == 0
    rows = B_HEADS * t_len
    qmap = lambda b, s: (b, 0)
    cmap = lambda b, s: (b, 0, 0, s)
    ck, cv = (a.transpose(0, 2, 3, 1) for a in (ck, cv))
    cki = cki.transpose(0, 2, 1)
    return pl.pallas_call(
        functools.partial(_dsa_dec_body, p_len=p_len, tq=t_len, n_sel=n_sel, kstep=kstep),
        grid=(bsz, p_len // kstep),
        in_specs=[pl.BlockSpec((t_len, B_WIDTH), qmap),
                  pl.BlockSpec((t_len, IDX_HEADS * IDX_HD), qmap),
                  pl.BlockSpec((t_len, MISC_W), qmap),
                  pl.BlockSpec((t_len, B_WIDTH), qmap),
                  pl.BlockSpec((t_len, B_WIDTH), qmap),
                  pl.BlockSpec((t_len, B_WIDTH), qmap),
                  pl.BlockSpec((None, B_HEADS, B_HD, kstep), cmap),
                  pl.BlockSpec((None, B_HEADS, B_HD, kstep), cmap),
                  pl.BlockSpec((None, IDX_HD, p_len), lambda b, s: (b, 0, 0))],
        out_specs=pl.BlockSpec((t_len, B_WIDTH), qmap),
        out_shape=jax.ShapeDtypeStruct((bsz * t_len, B_WIDTH), F32),
        scratch_shapes=[pltpu.VMEM((p_len // KEY_BLOCK + 1, t_len, KEY_BLOCK), F32),
                        pltpu.VMEM((t_len, 1), F32),
                        pltpu.VMEM((rows, 1), F32),
                        pltpu.VMEM((rows, 1), F32),
                        pltpu.VMEM((rows, B_HD), F32)],
        compiler_params=_cparams(("parallel", "arbitrary")),
        name="dsa_decode",
    )(qb, qi, misc, zb, kb, vb, ck, cv, cki)


def _tail_body(x_ref, oa_ref, ob_ref, qm_ref, zm_ref, mk_ref, mv_ref, w_ref, y_ref):
    qm = qm_ref[...] * (M_HD ** -0.5)
    mk = mk_ref[...].reshape(N_MEM, M_WIDTH).astype(BF16)
    mv = mv_ref[...].reshape(N_MEM, M_WIDTH).astype(BF16)
    acc = _dot(oa_ref[...].astype(BF16), w_ref[0:A_WIDTH, :])
    acc = acc + _dot(ob_ref[...].astype(BF16), w_ref[A_WIDTH:A_WIDTH + B_WIDTH, :])
    om = jnp.zeros(qm.shape, F32)
    for h in range(M_HEADS):
        mask = _lane_mask(M_WIDTH, h * M_HD, (h + 1) * M_HD)
        s = _dot_nt(jnp.where(mask, qm, 0.0).astype(BF16), mk)
        p = jnp.exp(s - jnp.max(s, axis=-1, keepdims=True))
        l = jnp.sum(p, axis=-1, keepdims=True)
        om = om + jnp.where(mask, _dot(p.astype(BF16), mv) / l, 0.0)
    om = om * _silu(zm_ref[...])
    acc = acc + _dot(om.astype(BF16), w_ref[A_WIDTH + B_WIDTH:, :])
    y_ref[...] = x_ref[...] + acc


def _tail(x, oa, ob, qm, zm, mk, mv, w, *, bsz, t_len, tq):
    m, d = x.shape
    nq = t_len // tq
    qmap = lambda b, j: (b * nq + j, 0)
    mem_spec = pl.BlockSpec((None,) + mk.shape[1:], lambda b, j: (b,) + (0,) * (mk.ndim - 1))
    return pl.pallas_call(
        _tail_body,
        grid=(bsz, nq),
        in_specs=[pl.BlockSpec((tq, d), qmap),
                  pl.BlockSpec((tq, A_WIDTH), qmap),
                  pl.BlockSpec((tq, B_WIDTH), qmap),
                  pl.BlockSpec((tq, M_WIDTH), qmap),
                  pl.BlockSpec((tq, M_WIDTH), qmap),
                  mem_spec, mem_spec,
                  pl.BlockSpec(w.shape, lambda b, j: (0, 0))],
        out_specs=pl.BlockSpec((tq, d), qmap),
        out_shape=jax.ShapeDtypeStruct((m, d), F32),
        compiler_params=_cparams(("parallel", "parallel")),
        name="tail",
    )(x, oa, ob, qm, zm, mk, mv, w)


def _prep_weights(g_in, w_in, conv_w, a_log, dt_bias, g_o, g_qb, g_kb, g_ki, g_qm, w_out):
    offs = np.concatenate([[0], np.cumsum(IN_SPLITS)])
    col = lambda i: w_in[:, int(offs[i]):int(offs[i + 1])]
    (w_qkv, w_za, w_ba, w_aa, w_qb, w_kb, w_vb, w_zb, w_qi, w_ki, w_wi, w_qm, w_zm) = [col(i) for i in range(13)]
    d = w_in.shape[0]
    w_misc = jnp.concatenate(
        [w_ki, w_wi, w_ba, w_aa, jnp.zeros((d, MISC_W - MISC_A - A_HEADS), w_in.dtype)], axis=1)
    w_all = jnp.concatenate([w_za, w_qb, w_kb, w_vb, w_zb, w_qi, w_qm, w_zm, w_misc, w_qkv],
                            axis=1).astype(BF16)
    groups = [(A_WIDTH, False, ""), (B_WIDTH, True, ""),
              (B_WIDTH, True, "t"), (B_WIDTH, False, "t"), (B_WIDTH, False, ""),
              (IDX_HEADS * IDX_HD, False, ""), (M_WIDTH, True, ""), (M_WIDTH, False, ""),
              (MISC_W, True, ""),
              (A_WIDTH, False, "conv0"), (A_WIDTH, False, "conv1"), (A_WIDTH, False, "conv2")]
    bd64 = _group_mean_matrix(B_WIDTH, B_HD, B_WIDTH)
    ones256 = jnp.ones((1, B_WIDTH), F32)
    misc_gain = jnp.concatenate([g_ki, jnp.full((IDX_HEADS,), IDX_SCALE, F32),
                                 jnp.ones((MISC_W - MISC_B,), F32)])[None, :]
    misc_nm = (jnp.arange(MISC_W) < IDX_HD).astype(F32)[None, :]
    aux = [bd64, jnp.tile(g_qb, B_HEADS)[None, :], ones256,
           bd64, jnp.tile(g_kb, B_HEADS)[None, :], ones256,
           bd64, jnp.tile(g_qm, M_HEADS)[None, :], ones256,
           _group_mean_matrix(MISC_W, IDX_HD, IDX_HD), misc_gain, misc_nm]
    pad_a = lambda v: jnp.zeros((1, MISC_W), F32).at[0, MISC_A:MISC_A + A_HEADS].set(v)
    return dict(g_in=g_in[None, :], w_all=w_all, groups=groups, aux=aux, conv_w=conv_w,
                avec=pad_a(-jnp.exp(a_log)), dtvec=pad_a(dt_bias), g_o=g_o[None, :],
                w_out=w_out.astype(BF16))


def _layer(x, mem_k, mem_v, conv_buf, s0, past, gdn_chunk, gdn_rows, wp):
    bsz, t_len, d = x.shape
    m = bsz * t_len
    x2 = x.reshape(m, d)
    tm = min(PROJ_ROWS, m)
    heads_last = lambda a_t: a_t.reshape(bsz, B_HEADS, B_HD, t_len).transpose(0, 3, 1, 2)
    cbuf = jnp.concatenate([jnp.zeros((bsz, CARRY_ROWS - (CONV_W - 1), 3 * A_WIDTH), F32), conv_buf], axis=1)
    conv = (cbuf, wp["conv_w"])
    gdn_args = dict(bsz=bsz, t_len=t_len, chunk=gdn_chunk, rows=gdn_rows)
    if past is None:
        (za, qb, kb, vb, zb, qi, qm, zm, misc, qa, ka, va, kt, vt, conv_new) = _proj(
            x2, wp["g_in"], wp["w_all"], wp["groups"], wp["aux"], tm, t_len, conv=conv)
        k_new, v_new = heads_last(kt), heads_last(vt)
        oa, s_new = _gdn(qa, ka, va, za, misc, s0, wp["avec"], wp["dtvec"], wp["g_o"], **gdn_args)
    else:
        groups = [(wd, normed, "") for wd, normed, _ in wp["groups"]]
        (za, qb, kb, vb, zb, qi, qm, zm, misc, qa, ka, va) = _proj(x2, wp["g_in"], wp["w_all"], groups,
                                                                   wp["aux"], tm, m)
        k_new, v_new = (a.reshape(bsz, t_len, B_HEADS, B_HD) for a in (kb, vb))
        oa, s_new, conv_new = _gdn(qa, ka, va, za, misc, s0, wp["avec"], wp["dtvec"], wp["g_o"], conv=conv,
                                   **gdn_args)
    if past is None:
        n_sel = min(TOPK_MAX, t_len // 4)
        ob = _dsa_prompt(qb, qi, misc, zb, kb, vt, bsz=bsz, t_len=t_len, n_sel=n_sel, qblk=DSA_QBLK)
    else:
        ck, cv, cki = past
        p_len = ck.shape[1]
        assert (p_len + t_len - 1) // CHUNK <= p_len // CHUNK and p_len % KEY_BLOCK == 0
        n_sel = min(TOPK_MAX, (p_len + t_len) // 4)
        ob = _dsa_decode(qb, qi, misc, zb, kb, vb, ck, cv, cki, bsz=bsz, t_len=t_len, p_len=p_len,
                         n_sel=n_sel)
    y = _tail(x2, oa, ob, qm, zm, mem_k, mem_v, wp["w_out"], bsz=bsz, t_len=t_len, tq=min(TAIL_ROWS, t_len))
    return (y.reshape(bsz, t_len, d), conv_new, s_new, k_new, v_new,
            misc[:, :IDX_HD].reshape(bsz, t_len, IDX_HD))


def _memory_kv(mem, g_mem, w_mem_kv, g_km):
    bsz, n_mem, d = mem.shape
    groups = [(M_WIDTH, True, "t"), (M_WIDTH, False, "t")]
    aux = [_group_mean_matrix(M_WIDTH, M_HD, M_WIDTH), jnp.tile(g_km, M_HEADS)[None, :],
           jnp.ones((1, M_WIDTH), F32)]
    mk, mv, mk_t, mv_t = _proj(mem.reshape(bsz * n_mem, d), g_mem[None, :], w_mem_kv.astype(BF16),
                               groups, aux, n_mem, n_mem)
    heads_last = lambda a_t: a_t.reshape(bsz, M_HEADS, M_HD, n_mem).transpose(0, 3, 1, 2)
    return (mk.reshape(bsz, n_mem, M_WIDTH), mv.reshape(bsz, n_mem, M_WIDTH), heads_last(mk_t), heads_last(mv_t))


def kernel(x_prompt, x_sample, state_conv_A, state_ssm_A, cache_k_B, cache_v_B, cache_kidx_B, cache_mem_k,
           cache_mem_v, mem_prompt, g_in, w_in, conv_w_A, a_log_A, dt_bias_A, g_o_A, g_q_B, g_k_B, g_kidx_B,
           g_mem, w_mem_kv, g_q_M, g_k_M, w_out):
    depth = w_in.shape[0]
    assert depth == 1
    l = 0
    bp, t_p, _ = x_prompt.shape
    bs, t_s, _ = x_sample.shape
    wp = _prep_weights(g_in[l], w_in[l], conv_w_A[l], a_log_A[l], dt_bias_A[l], g_o_A[l], g_q_B[l], g_k_B[l],
                       g_kidx_B[l], g_q_M[l], w_out[l])
    mk, mv, mk_heads, mv_heads = _memory_kv(mem_prompt, g_mem[l], w_mem_kv[l], g_k_M[l])
    zero_conv = jnp.zeros((bp, CONV_W - 1, 3 * A_WIDTH), F32)
    zero_ssm = jnp.zeros((bp, A_HEADS, A_DK, A_DV), F32)
    yp, c1, s1, k1, v1, ki1 = _layer(x_prompt, mk, mv, zero_conv, zero_ssm, None, CHUNK, GDN_CHUNKS_PER_STEP * CHUNK, wp)
    ys, c2, s2, k2, v2, ki2 = _layer(
        x_sample, cache_mem_k[l], cache_mem_v[l],
        state_conv_A[l], state_ssm_A[l], (cache_k_B[l], cache_v_B[l], cache_kidx_B[l]), t_s, t_s, wp)
    st = lambda a: a[None]
    return (yp, ys, st(c1), st(s1), st(k1), st(v1), st(ki1),
            st(mk_heads), st(mv_heads),
            st(c2), st(s2), st(k2), st(v2), st(ki2))
```

```python
import functools
import math

import numpy as np
import jax
import jax.numpy as jnp
from jax import lax
from jax.experimental import pallas as pl
from jax.experimental.pallas import tpu as pltpu

F32 = jnp.float32
BF16 = jnp.bfloat16
HIGHEST = lax.Precision.HIGHEST

D_MODEL = 1024
CHUNK = 64
A_HEADS = 4
A_DK = 128
A_DV = 128
A_WIDTH = A_HEADS * A_DV
CONV_W = 4
B_HEADS = 4
B_HD = 64
B_WIDTH = B_HEADS * B_HD
IDX_HEADS = 8
IDX_HD = 32
IDX_SCALE = (IDX_HEADS ** -0.5) * (IDX_HD ** -0.5)
TOPK_MAX = 256
Q_BLOCK = 128
N_MEM = 256
M_HEADS = 4
M_HD = 64
M_WIDTH = M_HEADS * M_HD
EPS = 1e-6
IN_SPLITS = (3 * A_WIDTH, A_WIDTH, A_HEADS, A_HEADS,
             B_WIDTH, B_WIDTH, B_WIDTH, B_WIDTH, IDX_HEADS * IDX_HD, IDX_HD, IDX_HEADS,
             M_WIDTH, M_WIDTH)

LANES = 128
KEY_BLOCK = 256
DSA_QBLK = 512
GDN_CHUNKS_PER_STEP = 8
TAIL_ROWS = 512
PROJ_ROWS = 512
DEC_KEYS_PER_STEP = 4096
SCORE_STRIP = 128
MISC_W = LANES
MISC_WI = IDX_HD
MISC_B = IDX_HD + IDX_HEADS
MISC_A = MISC_B + A_HEADS
VMEM_LIMIT = 48 * 1024 * 1024
NEG_BIG = -1e30
BISECT_PROBES = 2
COUNT_RADIX = 1024
BISECT_BLIND_ITERS = 10
BISECT_MAX_ITERS = 26


def _cparams(sem):
    return pltpu.CompilerParams(dimension_semantics=sem, vmem_limit_bytes=VMEM_LIMIT)


def _dot(a, b):
    return jnp.dot(a, b, preferred_element_type=F32)


def _dot_nt(a, b, precision=None):
    return lax.dot_general(a, b, (((1,), (1,)), ((), ())), precision=precision,
                           preferred_element_type=F32)


def _split2(x):
    hi = x.astype(BF16)
    return hi, (x - hi.astype(F32)).astype(BF16)


def _mm_split(a, b):
    ah, al = a
    bh, _ = b
    n = ah.shape[0]
    t = _dot(jnp.concatenate([ah, al], axis=0), bh)
    return t[:n] + t[n:]


def _silu(x):
    return x * jax.nn.sigmoid(x)


def _lane_mask(width, lo, hi):
    lane = lax.broadcasted_iota(jnp.int32, (1, width), 1)
    return (lane >= lo) & (lane < hi)


CARRY_ROWS = 8


def _conv_silu_norm(xp_ref, raw, cw, rows, part):
    first = CARRY_ROWS - (CONV_W - 1)
    xp_ref[CARRY_ROWS:CARRY_ROWS + rows, :] = raw
    y = xp_ref[first:first + rows, :] * cw[0:1, :]
    for j in range(1, CONV_W):
        y = y + xp_ref[first + j:first + j + rows, :] * cw[j:j + 1, :]
    y = _silu(y)
    conv_new = xp_ref[rows + first:rows + CARRY_ROWS, :]
    xp_ref[0:CARRY_ROWS, :] = xp_ref[rows:rows + CARRY_ROWS, :]
    if part == 2:
        return y, conv_new
    scale = A_DK ** -0.5 if part == 0 else 1.0
    heads = []
    for h in range(A_HEADS):
        yh = y[:, h * A_DK:(h + 1) * A_DK]
        heads.append(yh * (lax.rsqrt(jnp.sum(yh * yh, axis=-1, keepdims=True) + EPS) * scale))
    return jnp.concatenate(heads, axis=1), conv_new


def _conv_part(extra):
    return int(extra[len("conv"):]) if extra.startswith("conv") else None


def _proj_body(*refs, groups, tiles):
    x_ref, g_ref, w_ref = refs[0], refs[1], refs[2]
    n_aux = 3 * sum(1 for _, normed, _ in groups if normed)
    aux = refs[3:3 + n_aux]
    pos = 3 + n_aux
    n_conv = sum(1 for _, _, extra in groups if _conv_part(extra) is not None)
    if n_conv:
        cbuf_ref, cw_ref = refs[pos], refs[pos + 1]
        pos += 2
    outs = refs[pos:pos + len(groups)]
    pos += len(groups)
    n_t = sum(1 for _, _, extra in groups if extra == "t")
    outs_t = list(refs[pos:pos + n_t])
    pos += n_t
    if n_conv:
        convnew_ref = refs[pos]
        xp_refs = refs[pos + 1:pos + 1 + n_conv]

        @pl.when(pl.program_id(0) % tiles == 0)
        def _():
            for part, xp_ref in enumerate(xp_refs):
                xp_ref[0:CARRY_ROWS, :] = cbuf_ref[:, part * A_WIDTH:(part + 1) * A_WIDTH]
    x = x_ref[...]
    r = lax.rsqrt(jnp.mean(x * x, axis=-1, keepdims=True) + EPS)
    hb = (x * g_ref[...]).astype(BF16)
    off = 0
    ai = 0
    for (width, normed, extra), o_ref in zip(groups, outs):
        y = _dot(hb, w_ref[:, off:off + width]) * r
        part = _conv_part(extra)
        if part is not None:
            cols = slice(part * A_WIDTH, (part + 1) * A_WIDTH)
            y, conv_new = _conv_silu_norm(xp_refs[part], y, cw_ref[:, cols], y.shape[0], part)
            convnew_ref[:, cols] = conv_new
        if normed:
            bd_ref, gain_ref, nm_ref = aux[ai], aux[ai + 1], aux[ai + 2]
            ai += 3
            sq = y * y
            hi = sq.astype(BF16)
            lo = (sq - hi.astype(F32)).astype(BF16)
            ms = _dot(hi, bd_ref[...]) + _dot(lo, bd_ref[...])
            scale = jnp.where(nm_ref[...] > 0.0, lax.rsqrt(ms + EPS), 1.0)
            y = y * scale * gain_ref[...]
        o_ref[...] = y
        if extra == "t":
            outs_t.pop(0)[...] = y.T
        off += width


def _group_mean_matrix(width, group, n_lanes):
    i = np.arange(width)
    m = ((i[:, None] // group) == (i[None, :] // group)) & (i[:, None] < n_lanes) & (i[None, :] < n_lanes)
    return jnp.asarray(m.astype(np.float32) / group, dtype=BF16)


def _proj(x, g, w, groups, aux, tm, t_len, conv=None):
    m, d = x.shape
    nw = w.shape[1]
    tiles = t_len // tm
    bsz = m // t_len
    in_specs = [pl.BlockSpec((tm, d), lambda i: (i, 0)),
                pl.BlockSpec((1, d), lambda i: (0, 0)),
                pl.BlockSpec((d, nw), lambda i: (0, 0))]
    for a in aux:
        in_specs.append(pl.BlockSpec(a.shape, lambda i: (0, 0)))
    out_shape = [jax.ShapeDtypeStruct((m, wd), F32) for wd, _, _ in groups]
    out_specs = [pl.BlockSpec((tm, wd), lambda i: (i, 0)) for wd, _, _ in groups]
    for wd, _, extra in groups:
        if extra == "t":
            out_shape.append(jax.ShapeDtypeStruct((bsz, wd, t_len), F32))
            out_specs.append(pl.BlockSpec((None, wd, tm), lambda i: (i // tiles, 0, i % tiles)))
    args = [x, g, w, *aux]
    scratch = []
    if conv is not None:
        cbuf, cw = conv
        wd = cw.shape[1]
        in_specs += [pl.BlockSpec((None, CARRY_ROWS, wd), lambda i: (i // tiles, 0, 0)),
                     pl.BlockSpec(cw.shape, lambda i: (0, 0))]
        args += [cbuf, cw]
        out_shape.append(jax.ShapeDtypeStruct((bsz, CONV_W - 1, wd), F32))
        out_specs.append(pl.BlockSpec((None, CONV_W - 1, wd), lambda i: (i // tiles, 0, 0)))
        scratch += [pltpu.VMEM((tm + CARRY_ROWS, A_WIDTH), F32) for _ in range(wd // A_WIDTH)]
    return pl.pallas_call(
        functools.partial(_proj_body, groups=tuple(groups), tiles=tiles),
        grid=(m // tm,),
        in_specs=in_specs, out_specs=out_specs, out_shape=out_shape, scratch_shapes=scratch,
        compiler_params=_cparams(("arbitrary",) if conv is not None else ("parallel",)),
        name="proj",
    )(*args)


def _gdn_body(*refs, chunk, rows, fused_conv):
    if fused_conv:
        (q_ref, k_ref, v_ref, za_ref, misc_ref, s0_ref, avec_ref, dtvec_ref, go_ref, cbuf_ref, cw_ref,
         o_ref, sfin_ref, convnew_ref, g_sc, b_sc, s_sc, *conv_sc) = refs
    else:
        (q_ref, k_ref, v_ref, za_ref, misc_ref, s0_ref, avec_ref, dtvec_ref, go_ref,
         o_ref, sfin_ref, g_sc, b_sc, s_sc) = refs
    t = pl.program_id(1)
    c = chunk

    @pl.when(t == 0)
    def _():
        s_sc[...] = s0_ref[0]

    if fused_conv:
        xp_refs, act_refs = conv_sc[:3], conv_sc[3:]

        @pl.when(t == 0)
        def _():
            for part, xp_ref in enumerate(xp_refs):
                xp_ref[0:CARRY_ROWS, :] = cbuf_ref[:, part * A_WIDTH:(part + 1) * A_WIDTH]
        for part, raw_ref in enumerate((q_ref, k_ref, v_ref)):
            cols = slice(part * A_WIDTH, (part + 1) * A_WIDTH)
            act, conv_new = _conv_silu_norm(xp_refs[part], raw_ref[...], cw_ref[:, cols], rows, part)
            convnew_ref[:, cols] = conv_new
            act_refs[part][...] = act
        q_sc, k_sc, v_sc = act_refs
    else:
        q_sc, k_sc, v_sc = q_ref, k_ref, v_ref
    misc = misc_ref[...]
    b_sc[...] = jax.nn.sigmoid(misc)
    g_sc[...] = avec_ref[...] * jax.nn.softplus(misc + dtvec_ref[...])

    ri = lax.broadcasted_iota(jnp.int32, (c, c), 0)
    ci = lax.broadcasted_iota(jnp.int32, (c, c), 1)
    tri = ri >= ci
    strict = ri > ci
    eye_f = (ri == ci).astype(F32)
    tri3 = (lax.broadcasted_iota(jnp.int32, (c, 3 * c), 0)
            >= lax.broadcasted_iota(jnp.int32, (c, 3 * c), 1) % c).astype(BF16)
    go = go_ref[...]
    n_chunks = rows // c

    problems = [(ic, h) for ic in range(n_chunks) for h in range(A_HEADS)]
    gc_all, gc_t = [], []
    for ic in range(n_chunks):
        g = g_sc[ic * c:(ic + 1) * c, :]
        g1 = g.astype(BF16)
        r1 = g - g1.astype(F32)
        g2 = r1.astype(BF16)
        g3 = (r1 - g2.astype(F32)).astype(BF16)
        gc_all.append(_dot(tri3, jnp.concatenate([g1, g2, g3], axis=0)))
    for ic in range(n_chunks):
        gc_t.append(gc_all[ic].T)

    def load(ref, ic, h):
        return ref[ic * c:(ic + 1) * c, h * A_DK:(h + 1) * A_DK]

    col = lambda a, lane: a[:, lane:lane + 1]
    k_bf = [load(k_sc, ic, h).astype(BF16) for ic, h in problems]
    kb = [load(k_sc, ic, h) * col(b_sc[ic * c:(ic + 1) * c, :], MISC_B + h) for ic, h in problems]
    kk = [_dot_nt(kb[i].astype(BF16), k_bf[i]) for i in range(len(problems))]
    qk_raw = [_dot_nt(load(q_sc, ic, h).astype(BF16), k_bf[i]) for i, (ic, h) in enumerate(problems)]
    decay = []
    for ic, h in problems:
        diff = col(gc_all[ic], MISC_A + h) - gc_t[ic][MISC_A + h:MISC_A + h + 1, :]
        decay.append(jnp.where(tri, jnp.exp(jnp.where(tri, diff, 0.0)), 0.0))
    lmat = [jnp.where(strict, kk[i] * decay[i], 0.0) for i in range(len(problems))]
    qk = [(qk_raw[i] * decay[i]).astype(BF16) for i in range(len(problems))]
    x_inv = [eye_f - m for m in lmat]
    p_split = [_split2(m) for m in lmat]
    for _ in range(int(math.log2(c)) - 1):
        p_split = [_split2(_mm_split(ps, ps)) for ps in p_split]
        x_inv = [x + _mm_split(_split2(x), ps) for x, ps in zip(x_inv, p_split)]
    sol = []
    for i, (ic, h) in enumerate(problems):
        egc = jnp.exp(col(gc_all[ic], MISC_A + h))
        beta = col(b_sc[ic * c:(ic + 1) * c, :], MISC_B + h)
        rhs = jnp.concatenate([load(v_sc, ic, h) * beta, kb[i] * egc], axis=-1).astype(BF16)
        sol.append(_mm_split(_split2(x_inv[i]), (rhs, None)))
    pre = []
    for i, (ic, h) in enumerate(problems):
        gc = col(gc_all[ic], MISC_A + h)
        g_last = gc[c - 1:c, :]
        k_dec_t = (load(k_sc, ic, h) * jnp.exp(g_last - gc)).T.astype(BF16)
        wq = jnp.concatenate([sol[i][:, A_DV:], load(q_sc, ic, h) * jnp.exp(gc)], axis=0).astype(BF16)
        pre.append((sol[i][:, :A_DV], wq, k_dec_t, jnp.exp(g_last)))

    heads = range(A_HEADS)
    s_cur = [s_sc[h] for h in heads]
    for ic in range(n_chunks):
        pr = [pre[ic * A_HEADS + h] for h in heads]
        ws = [_dot(pr[h][1], s_cur[h].astype(BF16)) for h in heads]
        ub = [(pr[h][0] - ws[h][:c]).astype(BF16) for h in heads]
        o = [ws[h][c:] + _dot(qk[ic * A_HEADS + h], ub[h]) for h in heads]
        s_cur = [s_cur[h] * pr[h][3] + _dot(pr[h][2], ub[h]) for h in heads]
        for h in heads:
            on = o[h] * lax.rsqrt(jnp.mean(o[h] * o[h], axis=-1, keepdims=True) + EPS) * go
            o_ref[ic * c:(ic + 1) * c, h * A_DV:(h + 1) * A_DV] = on * _silu(load(za_ref, ic, h))
    for h in heads:
        s_sc[h] = s_cur[h]
    sfin_ref[0] = s_sc[...]


def _gdn(q, k, v, za, misc, s0, avec, dtvec, g_o, *, bsz, t_len, chunk, rows, conv=None):
    nt = t_len // rows
    w3 = 3 * A_WIDTH
    row_map = lambda b, t: (b * nt + t, 0)
    const2 = lambda b, t: (0, 0)
    state_spec = pl.BlockSpec((1, A_HEADS, A_DK, A_DV), lambda b, t: (b, 0, 0, 0))
    in_specs = [pl.BlockSpec((rows, A_WIDTH), row_map)] * 4 + [
                pl.BlockSpec((rows, MISC_W), row_map),
                state_spec,
                pl.BlockSpec((1, MISC_W), const2),
                pl.BlockSpec((1, MISC_W), const2),
                pl.BlockSpec((1, A_DV), const2)]
    args = [q, k, v, za, misc, s0, avec, dtvec, g_o]
    out_specs = [pl.BlockSpec((rows, A_WIDTH), row_map), state_spec]
    out_shape = [jax.ShapeDtypeStruct((bsz * t_len, A_WIDTH), F32),
                 jax.ShapeDtypeStruct((bsz, A_HEADS, A_DK, A_DV), F32)]
    scratch = [pltpu.VMEM((rows, MISC_W), F32),
               pltpu.VMEM((rows, MISC_W), F32),
               pltpu.VMEM((A_HEADS, A_DK, A_DV), F32)]
    if conv is not None:
        cbuf, cw = conv
        in_specs += [pl.BlockSpec((None, CARRY_ROWS, w3), lambda b, t: (b, 0, 0)),
                     pl.BlockSpec((CONV_W, w3), const2)]
        args += [cbuf, cw]
        out_specs.append(pl.BlockSpec((None, CONV_W - 1, w3), lambda b, t: (b, 0, 0)))
        out_shape.append(jax.ShapeDtypeStruct((bsz, CONV_W - 1, w3), F32))
        scratch += [pltpu.VMEM((rows + CARRY_ROWS, A_WIDTH), F32)] * 3 + [pltpu.VMEM((rows, A_WIDTH), F32)] * 3
    return pl.pallas_call(
        functools.partial(_gdn_body, chunk=chunk, rows=rows, fused_conv=conv is not None),
        grid=(bsz, nt),
        in_specs=in_specs, out_specs=out_specs, out_shape=out_shape, scratch_shapes=scratch,
        compiler_params=_cparams(("parallel", "arbitrary")),
        name="gdn",
    )(*args)


def _topk_threshold(sc_ref, nkb, shape, key_axis, n_sel, n_adm, even_blocks=False):
    kf = float(n_sel)
    kblk = shape[key_axis]

    if key_axis == 0:
        strip = min(kblk, 32)
        part_shape = (strip, shape[1])
        strip_of = lambda kb, i: sc_ref[kb, i * strip:(i + 1) * strip, :]

        def put_strip(kb, i, v):
            sc_ref[kb, i * strip:(i + 1) * strip, :] = v
    else:
        strip = kblk
        part_shape = shape
        strip_of = lambda kb, i: sc_ref[kb]

        def put_strip(kb, i, v):
            sc_ref[kb] = v
    kpos = lax.broadcasted_iota(jnp.int32, part_shape, key_axis)

    def reduce_blocks(specs):
        def body(kb, accs):
            accs = list(accs)
            for i in range(kblk // strip):
                s = strip_of(kb, i)
                k0 = kb * kblk + i * strip
                accs = [cmb(acc, fn(s, k0)) for acc, (fn, _, _, _, cmb) in zip(accs, specs)]
            return tuple(accs)
        init = tuple(jnp.full(part_shape, i, dt) for _, i, dt, _, _ in specs)
        accs = lax.fori_loop(0, nkb, body, init)
        return [acc if op is None else op(acc, axis=key_axis, keepdims=True)
                for acc, (_, _, _, op, _) in zip(accs, specs)]

    one_if = lambda c: jnp.where(c, 1.0, 0.0)
    vmax, vmin, cpos, czero = reduce_blocks([
        (lambda s, k0: s, -jnp.inf, F32, jnp.max, jnp.maximum),
        (lambda s, k0: jnp.where(s == -jnp.inf, jnp.inf, s), jnp.inf, F32, jnp.min, jnp.minimum),
        (lambda s, k0: one_if(s > 0.0), 0.0, F32, jnp.sum, jnp.add),
        (lambda s, k0: one_if(s == 0.0), 0.0, F32, jnp.sum, jnp.add)])

    long_row = n_adm > kf
    pos_row = long_row & (cpos >= kf)
    zero_row = long_row & (cpos < kf) & (cpos + czero >= kf)
    neg_row = long_row & (cpos + czero < kf)
    lo0 = jnp.where(pos_row | zero_row, 0.0, vmin - jnp.maximum(1.0, jnp.abs(vmin)))
    cnt_lo0 = jnp.where(pos_row, cpos, jnp.where(zero_row, kf, n_adm))
    hi0 = jnp.where(neg_row, 0.0, vmax)
    cnt_hi0 = jnp.where(neg_row, cpos, 0.0)
    need_zero = jnp.where(zero_row, kf - cpos, 0.0)

    def narrow(lo, hi, cnt_lo, cnt_hi):
        fracs = [(t + 1.0) / (BISECT_PROBES + 1.0) for t in range(BISECT_PROBES)]
        mids = [lo * (1.0 - f) + hi * f for f in fracs]
        for t in range(1, BISECT_PROBES):
            mids[t] = jnp.maximum(mids[t], mids[t - 1])
        codes = [sum(COUNT_RADIX ** u for u in range(t + 1)) for t in range(BISECT_PROBES)]

        def encode(s, k0):
            e = jnp.zeros(s.shape, jnp.int32)
            for mid, code in zip(mids, codes):
                e = jnp.where(s > mid, code, e)
            return e
        (packed,) = reduce_blocks([(encode, 0, jnp.int32, None, jnp.add)])
        cnts = []
        for t in range(BISECT_PROBES):
            digit = packed & (COUNT_RADIX - 1) if t < BISECT_PROBES - 1 else packed
            packed = packed >> COUNT_RADIX.bit_length() - 1
            cnts.append(jnp.sum(digit.astype(F32), axis=key_axis, keepdims=True))
        lo_n, cnt_lo_n, hi_n, cnt_hi_n = lo, cnt_lo, hi, cnt_hi
        for mid, cm in zip(mids, cnts):
            up = cm >= kf
            lo_n = jnp.where(up, mid, lo_n)
            cnt_lo_n = jnp.where(up, cm, cnt_lo_n)
        for mid, cm in zip(mids[::-1], cnts[::-1]):
            dn = cm < kf
            hi_n = jnp.where(dn, mid, hi_n)
            cnt_hi_n = jnp.where(dn, cm, cnt_hi_n)
        return lo_n, hi_n, cnt_lo_n, cnt_hi_n

    def unresolved(cnt_lo):
        return (jnp.max(cnt_lo) > kf).astype(jnp.int32)

    blind = jnp.where(unresolved(cnt_lo0) > 0, BISECT_BLIND_ITERS, 0)
    state = lax.fori_loop(0, blind, lambda i, st: narrow(*st), (lo0, hi0, cnt_lo0, cnt_hi0))

    def cond(st):
        return (st[0] < BISECT_MAX_ITERS) & (st[1] > 0)

    def body(st):
        new = narrow(*st[2:])
        return (st[0] + 1, unresolved(new[2])) + new

    _, _, lo, hi, cnt_lo, cnt_hi = lax.while_loop(
        cond, body, (blind, unresolved(state[2])) + state)

    fix = cnt_lo > kf
    need0 = jnp.where(fix, kf - cnt_hi, 0.0)
    big_idx = 2 ** 30

    def fix_cond(need):
        return jnp.max(need) > 0.0

    def fix_body(need):
        def in_cluster(s):
            return (s > lo) & (s <= hi)
        (mval,) = reduce_blocks([(lambda s, k0: jnp.where(in_cluster(s), s, -jnp.inf), -jnp.inf, F32,
                                  jnp.max, jnp.maximum)])
        (idx,) = reduce_blocks(
            [(lambda s, k0: jnp.where(in_cluster(s) & (s == mval), kpos + k0, big_idx),
              big_idx, jnp.int32, jnp.min, jnp.minimum)])
        active = need > 0.0

        def promote(kb, carry):
            for i in range(kblk // strip):
                s = strip_of(kb, i)
                put_strip(kb, i, jnp.where(active & ((kpos + kb * kblk + i * strip) == idx), jnp.inf, s))
            return carry
        lax.fori_loop(0, nkb, promote, 0)
        return jnp.where(active, need - 1.0, need)

    lax.while_loop(fix_cond, fix_body, need0)

    @pl.when(jnp.max(need_zero) > 0.0)
    def _():
        r = lax.broadcasted_iota(jnp.int32, (kblk, kblk), 0)
        c = lax.broadcasted_iota(jnp.int32, (kblk, kblk), 1)
        tri = ((r >= c) if key_axis == 0 else (r <= c)).astype(BF16)

        def tie_blk(kb, seen):
            s = sc_ref[kb]
            z = (s == 0.0) & zero_row
            zb = one_if(z).astype(BF16)
            if key_axis == 0:
                rank = _dot(tri, zb) + seen
                last = rank[kblk - 1:kblk, :]
            else:
                rank = _dot(zb, tri) + seen
                last = rank[:, kblk - 1:kblk]
            sc_ref[kb] = jnp.where(z & (rank <= need_zero), jnp.inf, s)
            return last
        if even_blocks:
            lax.fori_loop(0, nkb // 2, lambda i, seen: tie_blk(2 * i + 1, tie_blk(2 * i, seen)),
                          jnp.zeros_like(need_zero))
        else:
            lax.fori_loop(0, nkb, tie_blk, jnp.zeros_like(need_zero))

    return jnp.where(fix, hi, lo)


def _dsa_body(qb_ref, qi_ref, miscq_ref, zb_ref, k_ref, vt_ref, misck_ref, o_ref, sc_ref, s_ref, *, n_sel, qblk):
    j = pl.program_id(1)
    nkb = ((j + 1) * qblk + KEY_BLOCK - 1) // KEY_BLOCK
    qi = qi_ref[...]
    q_h = [qi[:, h * IDX_HD:(h + 1) * IDX_HD].astype(BF16) for h in range(IDX_HEADS)]
    w_t = miscq_ref[...].T[MISC_WI:MISC_WI + IDX_HEADS, :]
    qchunk = (j * qblk + lax.broadcasted_iota(jnp.int32, (1, qblk), 1)) // CHUNK
    krow = lax.broadcasted_iota(jnp.int32, (KEY_BLOCK, 1), 0)

    def fold(acc, x, combine):
        for i in range(x.shape[0] // 8):
            acc = combine(acc, x[8 * i:8 * (i + 1)])
        return acc

    def score_blk(kb, carry):
        k0 = pl.multiple_of(kb * KEY_BLOCK, KEY_BLOCK)
        for half in range(KEY_BLOCK // SCORE_STRIP):
            r0 = half * SCORE_STRIP
            ki = misck_ref[pl.ds(k0 + r0, SCORE_STRIP), 0:IDX_HD].astype(BF16)
            acc = jnp.zeros((SCORE_STRIP, qblk), F32)
            for h in range(IDX_HEADS):
                acc = acc + w_t[h:h + 1, :] * jnp.maximum(_dot_nt(ki, q_h[h]), 0.0)
            adm = ((k0 + r0 + krow[:SCORE_STRIP]) // CHUNK) <= qchunk
            sc_ref[kb, r0:r0 + SCORE_STRIP, :] = jnp.where(adm, acc, -jnp.inf)
        return carry

    npairs = (nkb + 1) // 2

    def pair_loop(body, init):
        return lax.fori_loop(0, npairs, lambda i, c: body(2 * i + 1, body(2 * i, c)), init)

    pair_loop(score_blk, 0)

    n_adm = ((qchunk + 1) * CHUNK).astype(F32)
    thr = _topk_threshold(sc_ref, nkb, (KEY_BLOCK, qblk), 0, n_sel, n_adm,
                          even_blocks=qblk % (2 * KEY_BLOCK) == 0)

    qb = qb_ref[...] * (B_HD ** -0.5 * math.log2(math.e))
    qs = jnp.concatenate([jnp.where(_lane_mask(B_WIDTH, h * B_HD, (h + 1) * B_HD), qb, 0.0)
                          for h in range(B_HEADS)], axis=0).astype(BF16)
    wide = B_HEADS * qblk

    def logits_blk(kb, m8):
        k0 = pl.multiple_of(kb * KEY_BLOCK, KEY_BLOCK)
        s = _dot_nt(k_ref[pl.ds(k0, KEY_BLOCK), :].astype(BF16), qs)
        sel = sc_ref[kb] > thr
        s = jnp.where(jnp.concatenate([sel] * B_HEADS, axis=1), s, NEG_BIG)
        s_ref[kb] = s
        return fold(m8, s, jnp.maximum)

    m8 = pair_loop(logits_blk, jnp.full((8, wide), NEG_BIG, F32))
    m = jnp.max(m8, axis=0, keepdims=True)

    def pv_blk(kb, carry):
        l8, accs = carry
        p = jnp.exp2(s_ref[kb] - m)
        pb = p.astype(BF16)
        k0 = pl.multiple_of(kb * KEY_BLOCK, KEY_BLOCK)
        vt = vt_ref[:, pl.ds(k0, KEY_BLOCK)].astype(BF16)
        accs = tuple(accs[h] + _dot(vt[h * B_HD:(h + 1) * B_HD, :], pb[:, h * qblk:(h + 1) * qblk])
                     for h in range(B_HEADS))
        return fold(l8, p, jnp.add), accs

    l8, accs = pair_loop(
        pv_blk, (jnp.zeros((8, wide), F32), tuple(jnp.zeros((B_HD, qblk), F32) for _ in range(B_HEADS))))
    l = jnp.sum(l8, axis=0, keepdims=True)
    o_t = jnp.concatenate([accs[h] / l[:, h * qblk:(h + 1) * qblk] for h in range(B_HEADS)], axis=0)
    o_ref[...] = o_t.T * _silu(zb_ref[...])


def _dsa_prompt(qb, qi, misc, zb, kb, vt, *, bsz, t_len, n_sel, qblk):
    nq = t_len // qblk
    nkb = t_len // KEY_BLOCK
    assert nkb * (KEY_BLOCK // 8) < COUNT_RADIX and nkb % 2 == 0
    qmap = lambda b, j: (b * nq + j, 0)
    kmap = lambda b, j: (b, 0)
    return pl.pallas_call(
        functools.partial(_dsa_body, n_sel=n_sel, qblk=qblk),
        grid=(bsz, nq),
        in_specs=[pl.BlockSpec((qblk, B_WIDTH), qmap),
                  pl.BlockSpec((qblk, IDX_HEADS * IDX_HD), qmap),
                  pl.BlockSpec((qblk, MISC_W), qmap),
                  pl.BlockSpec((qblk, B_WIDTH), qmap),
                  pl.BlockSpec((t_len, B_WIDTH), kmap),
                  pl.BlockSpec((None, B_WIDTH, t_len), lambda b, j: (b, 0, 0)),
                  pl.BlockSpec((t_len, MISC_W), kmap)],
        out_specs=pl.BlockSpec((qblk, B_WIDTH), qmap),
        out_shape=jax.ShapeDtypeStruct((bsz * t_len, B_WIDTH), F32),
        scratch_shapes=[pltpu.VMEM((nkb, KEY_BLOCK, qblk), F32),
                        pltpu.VMEM((nkb, KEY_BLOCK, B_HEADS * qblk), F32)],
        compiler_params=_cparams(("parallel", "arbitrary")),
        name="dsa_prompt",
    )(qb, qi, misc, zb, kb, vt, misc)


def _dsa_dec_body(qb_ref, qi_ref, misc_ref, zb_ref, kn_ref, vn_ref, ck_ref, cv_ref, cki_ref, o_ref,
                  sc_ref, thr_ref, m_ref, l_ref, acc_ref, *, p_len, tq, n_sel, kstep):
    step = pl.program_id(1)
    nsteps = p_len // kstep
    sub = kstep // KEY_BLOCK
    npast = p_len // KEY_BLOCK
    nkb = npast + 1
    pad_rows = KEY_BLOCK - tq
    rows = B_HEADS * tq

    @pl.when(step == 0)
    def _():
        qi = qi_ref[...]
        misc = misc_ref[...]
        q2 = jnp.concatenate([qi[:, h * IDX_HD:(h + 1) * IDX_HD] for h in range(IDX_HEADS)],
                             axis=0).astype(BF16)
        w_h = [misc[:, MISC_WI + h:MISC_WI + h + 1] for h in range(IDX_HEADS)]

        def head_sum(lg):
            acc = jnp.zeros((tq, lg.shape[1]), F32)
            for h in range(IDX_HEADS):
                acc = acc + w_h[h] * jnp.maximum(lg[h * tq:(h + 1) * tq, :], 0.0)
            return acc

        for i in range(nsteps):
            sc = head_sum(_dot(q2, cki_ref[:, i * kstep:(i + 1) * kstep].astype(BF16)))
            for u in range(sub):
                sc_ref[i * sub + u] = sc[:, u * KEY_BLOCK:(u + 1) * KEY_BLOCK]
        lane = lax.broadcasted_iota(jnp.int32, (1, KEY_BLOCK), 1)
        ki_new = jnp.concatenate([misc[:, 0:IDX_HD], jnp.zeros((pad_rows, IDX_HD), F32)], axis=0)
        sc_ref[npast] = jnp.where(lane < tq, head_sum(_dot_nt(q2, ki_new.astype(BF16))), -jnp.inf)
        n_adm = jnp.full((tq, 1), float(p_len + tq), F32)
        thr_ref[...] = _topk_threshold(sc_ref, nkb, (tq, KEY_BLOCK), 1, n_sel, n_adm)
        m_ref[...] = jnp.full((rows, 1), NEG_BIG, F32)
        l_ref[...] = jnp.zeros((rows, 1), F32)
        acc_ref[...] = jnp.zeros((rows, B_HD), F32)

    qb = qb_ref[...] * (B_HD ** -0.5 * math.log2(math.e))
    q_h = [qb[:, h * B_HD:(h + 1) * B_HD].astype(BF16) for h in range(B_HEADS)]
    thr = thr_ref[...]

    def fold_in(logits_of, pv_of, sel):
        s = jnp.concatenate([jnp.where(sel, logits_of(h), NEG_BIG) for h in range(B_HEADS)], axis=0)
        m_old = m_ref[...]
        m_new = jnp.maximum(m_old, jnp.max(s, axis=-1, keepdims=True))
        alpha = jnp.exp2(m_old - m_new)
        p = jnp.exp2(s - m_new)
        pb = p.astype(BF16)
        pv = jnp.concatenate([pv_of(h, pb[h * tq:(h + 1) * tq, :]) for h in range(B_HEADS)], axis=0)
        l_ref[...] = alpha * l_ref[...] + jnp.sum(p, axis=-1, keepdims=True)
        acc_ref[...] = alpha * acc_ref[...] + pv
        m_ref[...] = m_new

    sel = jnp.concatenate([sc_ref[step * sub + u] > thr for u in range(sub)], axis=1)
    fold_in(lambda h: _dot(q_h[h], ck_ref[h].astype(BF16)),
            lambda h, p: _dot_nt(p, cv_ref[h].astype(BF16)), sel)

    @pl.when(step == nsteps - 1)
    def _():
        zpad = jnp.zeros((pad_rows, B_WIDTH), F32)
        kn = jnp.concatenate([kn_ref[...], zpad], axis=0).astype(BF16)
        vn = jnp.concatenate([vn_ref[...], zpad], axis=0).astype(BF16)
        fold_in(lambda h: _dot_nt(q_h[h], kn[:, h * B_HD:(h + 1) * B_HD]),
                lambda h, p: _dot(p, vn[:, h * B_HD:(h + 1) * B_HD]), sc_ref[npast] > thr)
        res = acc_ref[...] / l_ref[...]
        out = jnp.concatenate([res[h * tq:(h + 1) * tq, :] for h in range(B_HEADS)], axis=1)
        o_ref[...] = out * _silu(zb_ref[...])


def _dsa_decode(qb, qi, misc, zb, kb, vb, ck, cv, cki, *, bsz, t_len, p_len, n_sel):
    kstep = DEC_KEYS_PER_STEP
    assert p_len // KEY_BLOCK + 1 < COUNT_RADIX and p_len % kstep == 0
    rows = B_HEADS * t_len
    qmap = lambda b, s: (b, 0)
    cmap = lambda b, s: (b, 0, 0, s)
    ck, cv = (a.transpose(0, 2, 3, 1) for a in (ck, cv))
    cki = cki.transpose(0, 2, 1)
    return pl.pallas_call(
        functools.partial(_dsa_dec_body, p_len=p_len, tq=t_len, n_sel=n_sel, kstep=kstep),
        grid=(bsz, p_len // kstep),
        in_specs=[pl.BlockSpec((t_len, B_WIDTH), qmap),
                  pl.BlockSpec((t_len, IDX_HEADS * IDX_HD), qmap),
                  pl.BlockSpec((t_len, MISC_W), qmap),
                  pl.BlockSpec((t_len, B_WIDTH), qmap),
                  pl.BlockSpec((t_len, B_WIDTH), qmap),
                  pl.BlockSpec((t_len, B_WIDTH), qmap),
                  pl.BlockSpec((None, B_HEADS, B_HD, kstep), cmap),
                  pl.BlockSpec((None, B_HEADS, B_HD, kstep), cmap),
                  pl.BlockSpec((None, IDX_HD, p_len), lambda b, s: (b, 0, 0))],
        out_specs=pl.BlockSpec((t_len, B_WIDTH), qmap),
        out_shape=jax.ShapeDtypeStruct((bsz * t_len, B_WIDTH), F32),
        scratch_shapes=[pltpu.VMEM((p_len // KEY_BLOCK + 1, t_len, KEY_BLOCK), F32),
                        pltpu.VMEM((t_len, 1), F32),
                        pltpu.VMEM((rows, 1), F32),
                        pltpu.VMEM((rows, 1), F32),
                        pltpu.VMEM((rows, B_HD), F32)],
        compiler_params=_cparams(("parallel", "arbitrary")),
        name="dsa_decode",
    )(qb, qi, misc, zb, kb, vb, ck, cv, cki)


def _tail_body(x_ref, oa_ref, ob_ref, qm_ref, zm_ref, mk_ref, mv_ref, w_ref, y_ref):
    qm = qm_ref[...] * (M_HD ** -0.5)
    mk_t = mk_ref[...].astype(BF16)
    mv_t = mv_ref[...].astype(BF16)
    acc = _dot(oa_ref[...].astype(BF16), w_ref[0:A_WIDTH, :])
    acc = acc + _dot(ob_ref[...].astype(BF16), w_ref[A_WIDTH:A_WIDTH + B_WIDTH, :])
    om = jnp.zeros(qm.shape, F32)
    for h in range(M_HEADS):
        mask = _lane_mask(M_WIDTH, h * M_HD, (h + 1) * M_HD)
        s = _dot(jnp.where(mask, qm, 0.0).astype(BF16), mk_t)
        p = jnp.exp(s - jnp.max(s, axis=-1, keepdims=True))
        l = jnp.sum(p, axis=-1, keepdims=True)
        om = om + jnp.where(mask, _dot_nt(p.astype(BF16), mv_t) / l, 0.0)
    om = om * _silu(zm_ref[...])
    acc = acc + _dot(om.astype(BF16), w_ref[A_WIDTH + B_WIDTH:, :])
    y_ref[...] = x_ref[...] + acc


def _tail(x, oa, ob, qm, zm, mk, mv, w, *, bsz, t_len, tq):
    m, d = x.shape
    nq = t_len // tq
    qmap = lambda b, j: (b * nq + j, 0)
    mem_spec = pl.BlockSpec((None, M_WIDTH, N_MEM), lambda b, j: (b, 0, 0))
    return pl.pallas_call(
        _tail_body,
        grid=(bsz, nq),
        in_specs=[pl.BlockSpec((tq, d), qmap),
                  pl.BlockSpec((tq, A_WIDTH), qmap),
                  pl.BlockSpec((tq, B_WIDTH), qmap),
                  pl.BlockSpec((tq, M_WIDTH), qmap),
                  pl.BlockSpec((tq, M_WIDTH), qmap),
                  mem_spec, mem_spec,
                  pl.BlockSpec(w.shape, lambda b, j: (0, 0))],
        out_specs=pl.BlockSpec((tq, d), qmap),
        out_shape=jax.ShapeDtypeStruct((m, d), F32),
        compiler_params=_cparams(("parallel", "parallel")),
        name="tail",
    )(x, oa, ob, qm, zm, mk, mv, w)


def _regroup_body(w_ref, o_ref):
    offs = [int(o) for o in np.concatenate([[0], np.cumsum(IN_SPLITS)])]
    col = lambda i: w_ref[:, offs[i]:offs[i + 1]]
    (w_qkv, w_za, w_ba, w_aa, w_qb, w_kb, w_vb, w_zb, w_qi, w_ki, w_wi, w_qm, w_zm) = [col(i) for i in range(13)]
    pad = jnp.zeros((w_ref.shape[0], MISC_W - MISC_A - A_HEADS), F32)
    o_ref[...] = jnp.concatenate([w_za, w_qb, w_kb, w_vb, w_zb, w_qi, w_qm, w_zm,
                                  w_ki, w_wi, w_ba, w_aa, pad, w_qkv], axis=1).astype(BF16)


def _regroup_weights(w_in):
    d, n_in = w_in.shape
    n_out = n_in - (IDX_HD + IDX_HEADS + 2 * A_HEADS) + MISC_W
    rows = 256
    return pl.pallas_call(
        _regroup_body,
        grid=(d // rows,),
        in_specs=[pl.BlockSpec((rows, n_in), lambda i: (i, 0))],
        out_specs=pl.BlockSpec((rows, n_out), lambda i: (i, 0)),
        out_shape=jax.ShapeDtypeStruct((d, n_out), BF16),
        compiler_params=_cparams(("parallel",)),
        name="regroup",
    )(w_in)


def _prep_weights(g_in, w_in, conv_w, a_log, dt_bias, g_o, g_qb, g_kb, g_ki, g_qm, w_out):
    w_all = _regroup_weights(w_in)
    groups = [(A_WIDTH, False, ""), (B_WIDTH, True, ""),
              (B_WIDTH, True, "t"), (B_WIDTH, False, "t"), (B_WIDTH, False, ""),
              (IDX_HEADS * IDX_HD, False, ""), (M_WIDTH, True, ""), (M_WIDTH, False, ""),
              (MISC_W, True, ""),
              (A_WIDTH, False, "conv0"), (A_WIDTH, False, "conv1"), (A_WIDTH, False, "conv2")]
    bd64 = _group_mean_matrix(B_WIDTH, B_HD, B_WIDTH)
    ones256 = jnp.ones((1, B_WIDTH), F32)
    misc_gain = jnp.concatenate([g_ki, jnp.full((IDX_HEADS,), IDX_SCALE, F32),
                                 jnp.ones((MISC_W - MISC_B,), F32)])[None, :]
    misc_nm = (jnp.arange(MISC_W) < IDX_HD).astype(F32)[None, :]
    aux = [bd64, jnp.tile(g_qb, B_HEADS)[None, :], ones256,
           bd64, jnp.tile(g_kb, B_HEADS)[None, :], ones256,
           bd64, jnp.tile(g_qm, M_HEADS)[None, :], ones256,
           _group_mean_matrix(MISC_W, IDX_HD, IDX_HD), misc_gain, misc_nm]
    pad_a = lambda v: jnp.zeros((1, MISC_W), F32).at[0, MISC_A:MISC_A + A_HEADS].set(v)
    return dict(g_in=g_in[None, :], w_all=w_all, groups=groups, aux=aux, conv_w=conv_w,
                avec=pad_a(-jnp.exp(a_log)), dtvec=pad_a(dt_bias), g_o=g_o[None, :],
                w_out=w_out.astype(BF16))


def _layer(x, mem_k, mem_v, conv_buf, s0, past, gdn_chunk, gdn_rows, wp):
    bsz, t_len, d = x.shape
    m = bsz * t_len
    x2 = x.reshape(m, d)
    tm = min(PROJ_ROWS, m)
    heads_last = lambda a_t: a_t.reshape(bsz, B_HEADS, B_HD, t_len).transpose(0, 3, 1, 2)
    cbuf = jnp.concatenate([jnp.zeros((bsz, CARRY_ROWS - (CONV_W - 1), 3 * A_WIDTH), F32), conv_buf], axis=1)
    conv = (cbuf, wp["conv_w"])
    gdn_args = dict(bsz=bsz, t_len=t_len, chunk=gdn_chunk, rows=gdn_rows)
    if past is None:
        (za, qb, kb, vb, zb, qi, qm, zm, misc, qa, ka, va, kt, vt, conv_new) = _proj(
            x2, wp["g_in"], wp["w_all"], wp["groups"], wp["aux"], tm, t_len, conv=conv)
        k_new, v_new = heads_last(kt), heads_last(vt)
        oa, s_new = _gdn(qa, ka, va, za, misc, s0, wp["avec"], wp["dtvec"], wp["g_o"], **gdn_args)
    else:
        groups = [(wd, normed, "") for wd, normed, _ in wp["groups"]]
        (za, qb, kb, vb, zb, qi, qm, zm, misc, qa, ka, va) = _proj(x2, wp["g_in"], wp["w_all"], groups,
                                                                   wp["aux"], tm, m)
        k_new, v_new = (a.reshape(bsz, t_len, B_HEADS, B_HD) for a in (kb, vb))
        oa, s_new, conv_new = _gdn(qa, ka, va, za, misc, s0, wp["avec"], wp["dtvec"], wp["g_o"], conv=conv,
                                   **gdn_args)
    if past is None:
        n_sel = min(TOPK_MAX, t_len // 4)
        ob = _dsa_prompt(qb, qi, misc, zb, kb, vt, bsz=bsz, t_len=t_len, n_sel=n_sel, qblk=DSA_QBLK)
    else:
        ck, cv, cki = past
        p_len = ck.shape[1]
        assert (p_len + t_len - 1) // CHUNK <= p_len // CHUNK and p_len % KEY_BLOCK == 0
        n_sel = min(TOPK_MAX, (p_len + t_len) // 4)
        ob = _dsa_decode(qb, qi, misc, zb, kb, vb, ck, cv, cki, bsz=bsz, t_len=t_len, p_len=p_len,
                         n_sel=n_sel)
    y = _tail(x2, oa, ob, qm, zm, mem_k, mem_v, wp["w_out"], bsz=bsz, t_len=t_len, tq=min(TAIL_ROWS, t_len))
    return (y.reshape(bsz, t_len, d), conv_new, s_new, k_new, v_new,
            misc[:, :IDX_HD].reshape(bsz, t_len, IDX_HD))


def _memory_kv(mem, g_mem, w_mem_kv, g_km):
    bsz, n_mem, d = mem.shape
    groups = [(M_WIDTH, True, "t"), (M_WIDTH, False, "t")]
    aux = [_group_mean_matrix(M_WIDTH, M_HD, M_WIDTH), jnp.tile(g_km, M_HEADS)[None, :],
           jnp.ones((1, M_WIDTH), F32)]
    _, _, mk_t, mv_t = _proj(mem.reshape(bsz * n_mem, d), g_mem[None, :], w_mem_kv.astype(BF16),
                             groups, aux, n_mem, n_mem)
    heads_last = lambda a_t: a_t.reshape(bsz, M_HEADS, M_HD, n_mem).transpose(0, 3, 1, 2)
    return mk_t, mv_t, heads_last(mk_t), heads_last(mv_t)


def kernel(x_prompt, x_sample, state_conv_A, state_ssm_A, cache_k_B, cache_v_B, cache_kidx_B, cache_mem_k,
           cache_mem_v, mem_prompt, g_in, w_in, conv_w_A, a_log_A, dt_bias_A, g_o_A, g_q_B, g_k_B, g_kidx_B,
           g_mem, w_mem_kv, g_q_M, g_k_M, w_out):
    depth = w_in.shape[0]
    assert depth == 1
    l = 0
    bp, t_p, _ = x_prompt.shape
    bs, t_s, _ = x_sample.shape
    wp = _prep_weights(g_in[l], w_in[l], conv_w_A[l], a_log_A[l], dt_bias_A[l], g_o_A[l], g_q_B[l], g_k_B[l],
                       g_kidx_B[l], g_q_M[l], w_out[l])
    mk, mv, mk_heads, mv_heads = _memory_kv(mem_prompt, g_mem[l], w_mem_kv[l], g_k_M[l])
    feature_major = lambda c: c.transpose(0, 2, 3, 1).reshape(c.shape[0], M_WIDTH, N_MEM)
    zero_conv = jnp.zeros((bp, CONV_W - 1, 3 * A_WIDTH), F32)
    zero_ssm = jnp.zeros((bp, A_HEADS, A_DK, A_DV), F32)
    yp, c1, s1, k1, v1, ki1 = _layer(x_prompt, mk, mv, zero_conv, zero_ssm, None, CHUNK, GDN_CHUNKS_PER_STEP * CHUNK, wp)
    ys, c2, s2, k2, v2, ki2 = _layer(
        x_sample, feature_major(cache_mem_k[l]), feature_major(cache_mem_v[l]),
        state_conv_A[l], state_ssm_A[l], (cache_k_B[l], cache_v_B[l], cache_kidx_B[l]), t_s, t_s, wp)
    st = lambda a: a[None]
    return (yp, ys, st(c1), st(s1), st(k1), st(v1), st(ki1),
            st(mk_heads), st(mv_heads),
            st(c2), st(s2), st(k2), st(v2), st(ki2))
```

```python
import functools
import math

import numpy as np
import jax
import jax.numpy as jnp
from jax import lax
from jax.experimental import pallas as pl
from jax.experimental.pallas import tpu as pltpu

F32 = jnp.float32
BF16 = jnp.bfloat16
HIGHEST = lax.Precision.HIGHEST

D_MODEL = 1024
CHUNK = 64
A_HEADS = 4
A_DK = 128
A_DV = 128
A_WIDTH = A_HEADS * A_DV
CONV_W = 4
B_HEADS = 4
B_HD = 64
B_WIDTH = B_HEADS * B_HD
IDX_HEADS = 8
IDX_HD = 32
IDX_SCALE = (IDX_HEADS ** -0.5) * (IDX_HD ** -0.5)
TOPK_MAX = 256
Q_BLOCK = 128
N_MEM = 256
M_HEADS = 4
M_HD = 64
M_WIDTH = M_HEADS * M_HD
EPS = 1e-6
IN_SPLITS = (3 * A_WIDTH, A_WIDTH, A_HEADS, A_HEADS,
             B_WIDTH, B_WIDTH, B_WIDTH, B_WIDTH, IDX_HEADS * IDX_HD, IDX_HD, IDX_HEADS,
             M_WIDTH, M_WIDTH)

LANES = 128
KEY_BLOCK = 256
DSA_QBLK = 512
GDN_CHUNKS_PER_STEP = 8
TAIL_ROWS = 512
PROJ_ROWS = 512
DEC_KEYS_PER_STEP = 4096
SCORE_STRIP = 128
MISC_W = LANES
MISC_WI = IDX_HD
MISC_B = IDX_HD + IDX_HEADS
MISC_A = MISC_B + A_HEADS
VMEM_LIMIT = 48 * 1024 * 1024
NEG_BIG = -1e30
BISECT_PROBES = 2
COUNT_RADIX = 1024
BISECT_BLIND_ITERS = 10
BISECT_MAX_ITERS = 26


def _cparams(sem):
    return pltpu.CompilerParams(dimension_semantics=sem, vmem_limit_bytes=VMEM_LIMIT)


def _dot(a, b):
    return jnp.dot(a, b, preferred_element_type=F32)


def _dot_nt(a, b, precision=None):
    return lax.dot_general(a, b, (((1,), (1,)), ((), ())), precision=precision,
                           preferred_element_type=F32)


def _split2(x):
    hi = x.astype(BF16)
    return hi, (x - hi.astype(F32)).astype(BF16)


def _mm_split(a, b):
    ah, al = a
    bh, _ = b
    n = ah.shape[0]
    t = _dot(jnp.concatenate([ah, al], axis=0), bh)
    return t[:n] + t[n:]


def _silu(x):
    return x * jax.nn.sigmoid(x)


def _lane_mask(width, lo, hi):
    lane = lax.broadcasted_iota(jnp.int32, (1, width), 1)
    return (lane >= lo) & (lane < hi)


CARRY_ROWS = 8


def _conv_silu_norm(xp_ref, raw, cw, rows, part):
    first = CARRY_ROWS - (CONV_W - 1)
    xp_ref[CARRY_ROWS:CARRY_ROWS + rows, :] = raw
    y = xp_ref[first:first + rows, :] * cw[0:1, :]
    for j in range(1, CONV_W):
        y = y + xp_ref[first + j:first + j + rows, :] * cw[j:j + 1, :]
    y = _silu(y)
    conv_new = xp_ref[rows + first:rows + CARRY_ROWS, :]
    xp_ref[0:CARRY_ROWS, :] = xp_ref[rows:rows + CARRY_ROWS, :]
    if part == 2:
        return y, conv_new
    scale = A_DK ** -0.5 if part == 0 else 1.0
    heads = []
    for h in range(A_HEADS):
        yh = y[:, h * A_DK:(h + 1) * A_DK]
        heads.append(yh * (lax.rsqrt(jnp.sum(yh * yh, axis=-1, keepdims=True) + EPS) * scale))
    return jnp.concatenate(heads, axis=1), conv_new


def _conv_part(extra):
    return int(extra[len("conv"):]) if extra.startswith("conv") else None


def _proj_body(*refs, groups, tiles):
    x_ref, g_ref, w_ref = refs[0], refs[1], refs[2]
    n_aux = 3 * sum(1 for _, normed, _ in groups if normed)
    aux = refs[3:3 + n_aux]
    pos = 3 + n_aux
    n_conv = sum(1 for _, _, extra in groups if _conv_part(extra) is not None)
    if n_conv:
        cbuf_ref, cw_ref = refs[pos], refs[pos + 1]
        pos += 2
    outs = refs[pos:pos + len(groups)]
    pos += len(groups)
    n_t = sum(1 for _, _, extra in groups if extra == "t")
    outs_t = list(refs[pos:pos + n_t])
    pos += n_t
    if n_conv:
        convnew_ref = refs[pos]
        xp_refs = refs[pos + 1:pos + 1 + n_conv]

        @pl.when(pl.program_id(0) % tiles == 0)
        def _():
            for part, xp_ref in enumerate(xp_refs):
                xp_ref[0:CARRY_ROWS, :] = cbuf_ref[:, part * A_WIDTH:(part + 1) * A_WIDTH]
    x = x_ref[...]
    r = lax.rsqrt(jnp.mean(x * x, axis=-1, keepdims=True) + EPS)
    hb = (x * g_ref[...]).astype(BF16)
    off = 0
    ai = 0
    for (width, normed, extra), o_ref in zip(groups, outs):
        y = _dot(hb, w_ref[:, off:off + width]) * r
        part = _conv_part(extra)
        if part is not None:
            cols = slice(part * A_WIDTH, (part + 1) * A_WIDTH)
            y, conv_new = _conv_silu_norm(xp_refs[part], y, cw_ref[:, cols], y.shape[0], part)
            convnew_ref[:, cols] = conv_new
        if normed:
            bd_ref, gain_ref, nm_ref = aux[ai], aux[ai + 1], aux[ai + 2]
            ai += 3
            sq = y * y
            hi = sq.astype(BF16)
            lo = (sq - hi.astype(F32)).astype(BF16)
            ms = _dot(hi, bd_ref[...]) + _dot(lo, bd_ref[...])
            scale = jnp.where(nm_ref[...] > 0.0, lax.rsqrt(ms + EPS), 1.0)
            y = y * scale * gain_ref[...]
        o_ref[...] = y
        if extra == "t":
            outs_t.pop(0)[...] = y.T
        off += width


def _group_mean_matrix(width, group, n_lanes):
    i = np.arange(width)
    m = ((i[:, None] // group) == (i[None, :] // group)) & (i[:, None] < n_lanes) & (i[None, :] < n_lanes)
    return jnp.asarray(m.astype(np.float32) / group, dtype=BF16)


def _proj(x, g, w, groups, aux, tm, t_len, conv=None):
    m, d = x.shape
    nw = w.shape[1]
    tiles = t_len // tm
    bsz = m // t_len
    in_specs = [pl.BlockSpec((tm, d), lambda i: (i, 0)),
                pl.BlockSpec((1, d), lambda i: (0, 0)),
                pl.BlockSpec((d, nw), lambda i: (0, 0))]
    for a in aux:
        in_specs.append(pl.BlockSpec(a.shape, lambda i: (0, 0)))
    out_shape = [jax.ShapeDtypeStruct((m, wd), F32) for wd, _, _ in groups]
    out_specs = [pl.BlockSpec((tm, wd), lambda i: (i, 0)) for wd, _, _ in groups]
    for wd, _, extra in groups:
        if extra == "t":
            out_shape.append(jax.ShapeDtypeStruct((bsz, wd, t_len), F32))
            out_specs.append(pl.BlockSpec((None, wd, tm), lambda i: (i // tiles, 0, i % tiles)))
    args = [x, g, w, *aux]
    scratch = []
    if conv is not None:
        cbuf, cw = conv
        wd = cw.shape[1]
        in_specs += [pl.BlockSpec((None, CARRY_ROWS, wd), lambda i: (i // tiles, 0, 0)),
                     pl.BlockSpec(cw.shape, lambda i: (0, 0))]
        args += [cbuf, cw]
        out_shape.append(jax.ShapeDtypeStruct((bsz, CONV_W - 1, wd), F32))
        out_specs.append(pl.BlockSpec((None, CONV_W - 1, wd), lambda i: (i // tiles, 0, 0)))
        scratch += [pltpu.VMEM((tm + CARRY_ROWS, A_WIDTH), F32) for _ in range(wd // A_WIDTH)]
    return pl.pallas_call(
        functools.partial(_proj_body, groups=tuple(groups), tiles=tiles),
        grid=(m // tm,),
        in_specs=in_specs, out_specs=out_specs, out_shape=out_shape, scratch_shapes=scratch,
        compiler_params=_cparams(("arbitrary",) if conv is not None else ("parallel",)),
        name="proj",
    )(*args)


def _gdn_body(*refs, chunk, rows, fused_conv):
    if fused_conv:
        (q_ref, k_ref, v_ref, za_ref, misc_ref, s0_ref, avec_ref, dtvec_ref, go_ref, cbuf_ref, cw_ref,
         o_ref, sfin_ref, convnew_ref, g_sc, b_sc, s_sc, *conv_sc) = refs
    else:
        (q_ref, k_ref, v_ref, za_ref, misc_ref, s0_ref, avec_ref, dtvec_ref, go_ref,
         o_ref, sfin_ref, g_sc, b_sc, s_sc) = refs
    t = pl.program_id(1)
    c = chunk

    @pl.when(t == 0)
    def _():
        s_sc[...] = s0_ref[0]

    if fused_conv:
        xp_refs, act_refs = conv_sc[:3], conv_sc[3:]

        @pl.when(t == 0)
        def _():
            for part, xp_ref in enumerate(xp_refs):
                xp_ref[0:CARRY_ROWS, :] = cbuf_ref[:, part * A_WIDTH:(part + 1) * A_WIDTH]
        for part, raw_ref in enumerate((q_ref, k_ref, v_ref)):
            cols = slice(part * A_WIDTH, (part + 1) * A_WIDTH)
            act, conv_new = _conv_silu_norm(xp_refs[part], raw_ref[...], cw_ref[:, cols], rows, part)
            convnew_ref[:, cols] = conv_new
            act_refs[part][...] = act
        q_sc, k_sc, v_sc = act_refs
    else:
        q_sc, k_sc, v_sc = q_ref, k_ref, v_ref
    misc = misc_ref[...]
    b_sc[...] = jax.nn.sigmoid(misc)
    g_sc[...] = avec_ref[...] * jax.nn.softplus(misc + dtvec_ref[...])

    ri = lax.broadcasted_iota(jnp.int32, (c, c), 0)
    ci = lax.broadcasted_iota(jnp.int32, (c, c), 1)
    tri = ri >= ci
    strict = ri > ci
    eye_f = (ri == ci).astype(F32)
    tri3 = (lax.broadcasted_iota(jnp.int32, (c, 3 * c), 0)
            >= lax.broadcasted_iota(jnp.int32, (c, 3 * c), 1) % c).astype(BF16)
    go = go_ref[...]
    n_chunks = rows // c

    problems = [(ic, h) for ic in range(n_chunks) for h in range(A_HEADS)]
    gc_all, gc_t = [], []
    for ic in range(n_chunks):
        g = g_sc[ic * c:(ic + 1) * c, :]
        g1 = g.astype(BF16)
        r1 = g - g1.astype(F32)
        g2 = r1.astype(BF16)
        g3 = (r1 - g2.astype(F32)).astype(BF16)
        gc_all.append(_dot(tri3, jnp.concatenate([g1, g2, g3], axis=0)))
    for ic in range(n_chunks):
        gc_t.append(gc_all[ic].T)

    def load(ref, ic, h):
        return ref[ic * c:(ic + 1) * c, h * A_DK:(h + 1) * A_DK]

    col = lambda a, lane: a[:, lane:lane + 1]
    k_bf = [load(k_sc, ic, h).astype(BF16) for ic, h in problems]
    kb = [load(k_sc, ic, h) * col(b_sc[ic * c:(ic + 1) * c, :], MISC_B + h) for ic, h in problems]
    kk = [_dot_nt(kb[i].astype(BF16), k_bf[i]) for i in range(len(problems))]
    qk_raw = [_dot_nt(load(q_sc, ic, h).astype(BF16), k_bf[i]) for i, (ic, h) in enumerate(problems)]
    decay = []
    for ic, h in problems:
        diff = col(gc_all[ic], MISC_A + h) - gc_t[ic][MISC_A + h:MISC_A + h + 1, :]
        decay.append(jnp.where(tri, jnp.exp(jnp.where(tri, diff, 0.0)), 0.0))
    lmat = [jnp.where(strict, kk[i] * decay[i], 0.0) for i in range(len(problems))]
    qk = [(qk_raw[i] * decay[i]).astype(BF16) for i in range(len(problems))]
    x_inv = [eye_f - m for m in lmat]
    p_split = [_split2(m) for m in lmat]
    for _ in range(int(math.log2(c)) - 1):
        p_split = [_split2(_mm_split(ps, ps)) for ps in p_split]
        x_inv = [x + _mm_split(_split2(x), ps) for x, ps in zip(x_inv, p_split)]
    sol = []
    for i, (ic, h) in enumerate(problems):
        egc = jnp.exp(col(gc_all[ic], MISC_A + h))
        beta = col(b_sc[ic * c:(ic + 1) * c, :], MISC_B + h)
        rhs = jnp.concatenate([load(v_sc, ic, h) * beta, kb[i] * egc], axis=-1).astype(BF16)
        sol.append(_mm_split(_split2(x_inv[i]), (rhs, None)))
    pre = []
    for i, (ic, h) in enumerate(problems):
        gc = col(gc_all[ic], MISC_A + h)
        g_last = gc[c - 1:c, :]
        k_dec_t = (load(k_sc, ic, h) * jnp.exp(g_last - gc)).T.astype(BF16)
        wq = jnp.concatenate([sol[i][:, A_DV:], load(q_sc, ic, h) * jnp.exp(gc)], axis=0).astype(BF16)
        pre.append((sol[i][:, :A_DV], wq, k_dec_t, jnp.exp(g_last)))

    heads = range(A_HEADS)
    s_cur = [s_sc[h] for h in heads]
    for ic in range(n_chunks):
        pr = [pre[ic * A_HEADS + h] for h in heads]
        ws = [_dot(pr[h][1], s_cur[h].astype(BF16)) for h in heads]
        ub = [(pr[h][0] - ws[h][:c]).astype(BF16) for h in heads]
        o = [ws[h][c:] + _dot(qk[ic * A_HEADS + h], ub[h]) for h in heads]
        s_cur = [s_cur[h] * pr[h][3] + _dot(pr[h][2], ub[h]) for h in heads]
        for h in heads:
            on = o[h] * lax.rsqrt(jnp.mean(o[h] * o[h], axis=-1, keepdims=True) + EPS) * go
            o_ref[ic * c:(ic + 1) * c, h * A_DV:(h + 1) * A_DV] = on * _silu(load(za_ref, ic, h))
    for h in heads:
        s_sc[h] = s_cur[h]
    sfin_ref[0] = s_sc[...]


def _gdn(q, k, v, za, misc, s0, avec, dtvec, g_o, *, bsz, t_len, chunk, rows, conv=None):
    nt = t_len // rows
    w3 = 3 * A_WIDTH
    row_map = lambda b, t: (b * nt + t, 0)
    const2 = lambda b, t: (0, 0)
    state_spec = pl.BlockSpec((1, A_HEADS, A_DK, A_DV), lambda b, t: (b, 0, 0, 0))
    in_specs = [pl.BlockSpec((rows, A_WIDTH), row_map)] * 4 + [
                pl.BlockSpec((rows, MISC_W), row_map),
                state_spec,
                pl.BlockSpec((1, MISC_W), const2),
                pl.BlockSpec((1, MISC_W), const2),
                pl.BlockSpec((1, A_DV), const2)]
    args = [q, k, v, za, misc, s0, avec, dtvec, g_o]
    out_specs = [pl.BlockSpec((rows, A_WIDTH), row_map), state_spec]
    out_shape = [jax.ShapeDtypeStruct((bsz * t_len, A_WIDTH), F32),
                 jax.ShapeDtypeStruct((bsz, A_HEADS, A_DK, A_DV), F32)]
    scratch = [pltpu.VMEM((rows, MISC_W), F32),
               pltpu.VMEM((rows, MISC_W), F32),
               pltpu.VMEM((A_HEADS, A_DK, A_DV), F32)]
    if conv is not None:
        cbuf, cw = conv
        in_specs += [pl.BlockSpec((None, CARRY_ROWS, w3), lambda b, t: (b, 0, 0)),
                     pl.BlockSpec((CONV_W, w3), const2)]
        args += [cbuf, cw]
        out_specs.append(pl.BlockSpec((None, CONV_W - 1, w3), lambda b, t: (b, 0, 0)))
        out_shape.append(jax.ShapeDtypeStruct((bsz, CONV_W - 1, w3), F32))
        scratch += [pltpu.VMEM((rows + CARRY_ROWS, A_WIDTH), F32)] * 3 + [pltpu.VMEM((rows, A_WIDTH), F32)] * 3
    return pl.pallas_call(
        functools.partial(_gdn_body, chunk=chunk, rows=rows, fused_conv=conv is not None),
        grid=(bsz, nt),
        in_specs=in_specs, out_specs=out_specs, out_shape=out_shape, scratch_shapes=scratch,
        compiler_params=_cparams(("parallel", "arbitrary")),
        name="gdn",
    )(*args)


def _topk_threshold(sc_ref, nkb, shape, key_axis, n_sel, n_adm, even_blocks=False):
    kf = float(n_sel)
    kblk = shape[key_axis]

    if key_axis == 0:
        strip = min(kblk, 32)
        part_shape = (strip, shape[1])
        strip_of = lambda kb, i: sc_ref[kb, i * strip:(i + 1) * strip, :]

        def put_strip(kb, i, v):
            sc_ref[kb, i * strip:(i + 1) * strip, :] = v
    else:
        strip = kblk
        part_shape = shape
        strip_of = lambda kb, i: sc_ref[kb]

        def put_strip(kb, i, v):
            sc_ref[kb] = v
    kpos = lax.broadcasted_iota(jnp.int32, part_shape, key_axis)

    def reduce_blocks(specs):
        def body(kb, accs):
            accs = list(accs)
            for i in range(kblk // strip):
                s = strip_of(kb, i)
                k0 = kb * kblk + i * strip
                accs = [cmb(acc, fn(s, k0)) for acc, (fn, _, _, _, cmb) in zip(accs, specs)]
            return tuple(accs)
        init = tuple(jnp.full(part_shape, i, dt) for _, i, dt, _, _ in specs)
        accs = lax.fori_loop(0, nkb, body, init)
        return [acc if op is None else op(acc, axis=key_axis, keepdims=True)
                for acc, (_, _, _, op, _) in zip(accs, specs)]

    one_if = lambda c: jnp.where(c, 1.0, 0.0)
    vmax, vmin, cpos, czero = reduce_blocks([
        (lambda s, k0: s, -jnp.inf, F32, jnp.max, jnp.maximum),
        (lambda s, k0: jnp.where(s == -jnp.inf, jnp.inf, s), jnp.inf, F32, jnp.min, jnp.minimum),
        (lambda s, k0: one_if(s > 0.0), 0.0, F32, jnp.sum, jnp.add),
        (lambda s, k0: one_if(s == 0.0), 0.0, F32, jnp.sum, jnp.add)])

    long_row = n_adm > kf
    pos_row = long_row & (cpos >= kf)
    zero_row = long_row & (cpos < kf) & (cpos + czero >= kf)
    neg_row = long_row & (cpos + czero < kf)
    lo0 = jnp.where(pos_row | zero_row, 0.0, vmin - jnp.maximum(1.0, jnp.abs(vmin)))
    cnt_lo0 = jnp.where(pos_row, cpos, jnp.where(zero_row, kf, n_adm))
    hi0 = jnp.where(neg_row, 0.0, vmax)
    cnt_hi0 = jnp.where(neg_row, cpos, 0.0)
    need_zero = jnp.where(zero_row, kf - cpos, 0.0)

    def narrow(lo, hi, cnt_lo, cnt_hi):
        fracs = [(t + 1.0) / (BISECT_PROBES + 1.0) for t in range(BISECT_PROBES)]
        mids = [lo * (1.0 - f) + hi * f for f in fracs]
        for t in range(1, BISECT_PROBES):
            mids[t] = jnp.maximum(mids[t], mids[t - 1])
        codes = [sum(COUNT_RADIX ** u for u in range(t + 1)) for t in range(BISECT_PROBES)]

        def encode(s, k0):
            e = jnp.zeros(s.shape, jnp.int32)
            for mid, code in zip(mids, codes):
                e = jnp.where(s > mid, code, e)
            return e
        (packed,) = reduce_blocks([(encode, 0, jnp.int32, None, jnp.add)])
        cnts = []
        for t in range(BISECT_PROBES):
            digit = packed & (COUNT_RADIX - 1) if t < BISECT_PROBES - 1 else packed
            packed = packed >> COUNT_RADIX.bit_length() - 1
            cnts.append(jnp.sum(digit.astype(F32), axis=key_axis, keepdims=True))
        lo_n, cnt_lo_n, hi_n, cnt_hi_n = lo, cnt_lo, hi, cnt_hi
        for mid, cm in zip(mids, cnts):
            up = cm >= kf
            lo_n = jnp.where(up, mid, lo_n)
            cnt_lo_n = jnp.where(up, cm, cnt_lo_n)
        for mid, cm in zip(mids[::-1], cnts[::-1]):
            dn = cm < kf
            hi_n = jnp.where(dn, mid, hi_n)
            cnt_hi_n = jnp.where(dn, cm, cnt_hi_n)
        return lo_n, hi_n, cnt_lo_n, cnt_hi_n

    def unresolved(cnt_lo):
        return (jnp.max(cnt_lo) > kf).astype(jnp.int32)

    blind = jnp.where(unresolved(cnt_lo0) > 0, BISECT_BLIND_ITERS, 0)
    state = lax.fori_loop(0, blind, lambda i, st: narrow(*st), (lo0, hi0, cnt_lo0, cnt_hi0))

    def cond(st):
        return (st[0] < BISECT_MAX_ITERS) & (st[1] > 0)

    def body(st):
        new = narrow(*st[2:])
        return (st[0] + 1, unresolved(new[2])) + new

    _, _, lo, hi, cnt_lo, cnt_hi = lax.while_loop(
        cond, body, (blind, unresolved(state[2])) + state)

    fix = cnt_lo > kf
    need0 = jnp.where(fix, kf - cnt_hi, 0.0)
    big_idx = 2 ** 30

    def fix_cond(need):
        return jnp.max(need) > 0.0

    def fix_body(need):
        def in_cluster(s):
            return (s > lo) & (s <= hi)
        (mval,) = reduce_blocks([(lambda s, k0: jnp.where(in_cluster(s), s, -jnp.inf), -jnp.inf, F32,
                                  jnp.max, jnp.maximum)])
        (idx,) = reduce_blocks(
            [(lambda s, k0: jnp.where(in_cluster(s) & (s == mval), kpos + k0, big_idx),
              big_idx, jnp.int32, jnp.min, jnp.minimum)])
        active = need > 0.0

        def promote(kb, carry):
            for i in range(kblk // strip):
                s = strip_of(kb, i)
                put_strip(kb, i, jnp.where(active & ((kpos + kb * kblk + i * strip) == idx), jnp.inf, s))
            return carry
        lax.fori_loop(0, nkb, promote, 0)
        return jnp.where(active, need - 1.0, need)

    lax.while_loop(fix_cond, fix_body, need0)

    @pl.when(jnp.max(need_zero) > 0.0)
    def _():
        r = lax.broadcasted_iota(jnp.int32, (kblk, kblk), 0)
        c = lax.broadcasted_iota(jnp.int32, (kblk, kblk), 1)
        tri = ((r >= c) if key_axis == 0 else (r <= c)).astype(BF16)

        def tie_blk(kb, seen):
            s = sc_ref[kb]
            z = (s == 0.0) & zero_row
            zb = one_if(z).astype(BF16)
            if key_axis == 0:
                rank = _dot(tri, zb) + seen
                last = rank[kblk - 1:kblk, :]
            else:
                rank = _dot(zb, tri) + seen
                last = rank[:, kblk - 1:kblk]
            sc_ref[kb] = jnp.where(z & (rank <= need_zero), jnp.inf, s)
            return last
        if even_blocks:
            lax.fori_loop(0, nkb // 2, lambda i, seen: tie_blk(2 * i + 1, tie_blk(2 * i, seen)),
                          jnp.zeros_like(need_zero))
        else:
            lax.fori_loop(0, nkb, tie_blk, jnp.zeros_like(need_zero))

    return jnp.where(fix, hi, lo)


def _dsa_body(qb_ref, qi_ref, miscq_ref, zb_ref, k_ref, vt_ref, misck_ref, o_ref, sc_ref, s_ref, *, n_sel, qblk):
    j = pl.program_id(1)
    nkb = ((j + 1) * qblk + KEY_BLOCK - 1) // KEY_BLOCK
    qi = qi_ref[...]
    q_h = [qi[:, h * IDX_HD:(h + 1) * IDX_HD].astype(BF16) for h in range(IDX_HEADS)]
    w_t = miscq_ref[...].T[MISC_WI:MISC_WI + IDX_HEADS, :]
    qchunk = (j * qblk + lax.broadcasted_iota(jnp.int32, (1, qblk), 1)) // CHUNK
    krow = lax.broadcasted_iota(jnp.int32, (KEY_BLOCK, 1), 0)

    def fold(acc, x, combine):
        for i in range(x.shape[0] // 8):
            acc = combine(acc, x[8 * i:8 * (i + 1)])
        return acc

    def score_blk(kb, carry):
        k0 = pl.multiple_of(kb * KEY_BLOCK, KEY_BLOCK)
        for half in range(KEY_BLOCK // SCORE_STRIP):
            r0 = half * SCORE_STRIP
            ki = misck_ref[pl.ds(k0 + r0, SCORE_STRIP), 0:IDX_HD].astype(BF16)
            acc = jnp.zeros((SCORE_STRIP, qblk), F32)
            for h in range(IDX_HEADS):
                acc = acc + w_t[h:h + 1, :] * jnp.maximum(_dot_nt(ki, q_h[h]), 0.0)
            adm = ((k0 + r0 + krow[:SCORE_STRIP]) // CHUNK) <= qchunk
            sc_ref[kb, r0:r0 + SCORE_STRIP, :] = jnp.where(adm, acc, -jnp.inf)
        return carry

    npairs = (nkb + 1) // 2

    def pair_loop(body, init):
        return lax.fori_loop(0, npairs, lambda i, c: body(2 * i + 1, body(2 * i, c)), init)

    pair_loop(score_blk, 0)

    n_adm = ((qchunk + 1) * CHUNK).astype(F32)
    thr = _topk_threshold(sc_ref, nkb, (KEY_BLOCK, qblk), 0, n_sel, n_adm,
                          even_blocks=qblk % (2 * KEY_BLOCK) == 0)

    qb = qb_ref[...] * (B_HD ** -0.5 * math.log2(math.e))
    qs = jnp.concatenate([jnp.where(_lane_mask(B_WIDTH, h * B_HD, (h + 1) * B_HD), qb, 0.0)
                          for h in range(B_HEADS)], axis=0).astype(BF16)
    wide = B_HEADS * qblk

    def logits_blk(kb, m8):
        k0 = pl.multiple_of(kb * KEY_BLOCK, KEY_BLOCK)
        s = _dot_nt(k_ref[pl.ds(k0, KEY_BLOCK), :].astype(BF16), qs)
        sel = sc_ref[kb] > thr
        s = jnp.where(jnp.concatenate([sel] * B_HEADS, axis=1), s, NEG_BIG)
        s_ref[kb] = s
        return fold(m8, s, jnp.maximum)

    m8 = pair_loop(logits_blk, jnp.full((8, wide), NEG_BIG, F32))
    m = jnp.max(m8, axis=0, keepdims=True)

    def pv_blk(kb, carry):
        l8, accs = carry
        p = jnp.exp2(s_ref[kb] - m)
        pb = p.astype(BF16)
        k0 = pl.multiple_of(kb * KEY_BLOCK, KEY_BLOCK)
        vt = vt_ref[:, pl.ds(k0, KEY_BLOCK)].astype(BF16)
        accs = tuple(accs[h] + _dot(vt[h * B_HD:(h + 1) * B_HD, :], pb[:, h * qblk:(h + 1) * qblk])
                     for h in range(B_HEADS))
        return fold(l8, p, jnp.add), accs

    l8, accs = pair_loop(
        pv_blk, (jnp.zeros((8, wide), F32), tuple(jnp.zeros((B_HD, qblk), F32) for _ in range(B_HEADS))))
    l = jnp.sum(l8, axis=0, keepdims=True)
    o_t = jnp.concatenate([accs[h] / l[:, h * qblk:(h + 1) * qblk] for h in range(B_HEADS)], axis=0)
    o_ref[...] = o_t.T * _silu(zb_ref[...])


def _dsa_prompt(qb, qi, misc, zb, kb, vt, *, bsz, t_len, n_sel, qblk):
    nq = t_len // qblk
    nkb = t_len // KEY_BLOCK
    assert nkb * (KEY_BLOCK // 8) < COUNT_RADIX and nkb % 2 == 0
    qmap = lambda b, j: (b * nq + j, 0)
    kmap = lambda b, j: (b, 0)
    return pl.pallas_call(
        functools.partial(_dsa_body, n_sel=n_sel, qblk=qblk),
        grid=(bsz, nq),
        in_specs=[pl.BlockSpec((qblk, B_WIDTH), qmap),
                  pl.BlockSpec((qblk, IDX_HEADS * IDX_HD), qmap),
                  pl.BlockSpec((qblk, MISC_W), qmap),
                  pl.BlockSpec((qblk, B_WIDTH), qmap),
                  pl.BlockSpec((t_len, B_WIDTH), kmap),
                  pl.BlockSpec((None, B_WIDTH, t_len), lambda b, j: (b, 0, 0)),
                  pl.BlockSpec((t_len, MISC_W), kmap)],
        out_specs=pl.BlockSpec((qblk, B_WIDTH), qmap),
        out_shape=jax.ShapeDtypeStruct((bsz * t_len, B_WIDTH), F32),
        scratch_shapes=[pltpu.VMEM((nkb, KEY_BLOCK, qblk), F32),
                        pltpu.VMEM((nkb, KEY_BLOCK, B_HEADS * qblk), F32)],
        compiler_params=_cparams(("parallel", "arbitrary")),
        name="dsa_prompt",
    )(qb, qi, misc, zb, kb, vt, misc)


def _dsa_dec_body(qb_ref, qi_ref, misc_ref, zb_ref, kn_ref, vn_ref, ck_ref, cv_ref, cki_ref, o_ref,
                  sc_ref, thr_ref, m_ref, l_ref, acc_ref, *, p_len, tq, n_sel, kstep):
    step = pl.program_id(1)
    nsteps = p_len // kstep
    sub = kstep // KEY_BLOCK
    npast = p_len // KEY_BLOCK
    nkb = npast + 1
    pad_rows = KEY_BLOCK - tq
    rows = B_HEADS * tq

    @pl.when(step == 0)
    def _():
        qi = qi_ref[...]
        misc = misc_ref[...]
        q2 = jnp.concatenate([qi[:, h * IDX_HD:(h + 1) * IDX_HD] for h in range(IDX_HEADS)],
                             axis=0).astype(BF16)
        w_h = [misc[:, MISC_WI + h:MISC_WI + h + 1] for h in range(IDX_HEADS)]

        def head_sum(lg):
            acc = jnp.zeros((tq, lg.shape[1]), F32)
            for h in range(IDX_HEADS):
                acc = acc + w_h[h] * jnp.maximum(lg[h * tq:(h + 1) * tq, :], 0.0)
            return acc

        for i in range(nsteps):
            sc = head_sum(_dot(q2, cki_ref[:, i * kstep:(i + 1) * kstep].astype(BF16)))
            for u in range(sub):
                sc_ref[i * sub + u] = sc[:, u * KEY_BLOCK:(u + 1) * KEY_BLOCK]
        lane = lax.broadcasted_iota(jnp.int32, (1, KEY_BLOCK), 1)
        ki_new = jnp.concatenate([misc[:, 0:IDX_HD], jnp.zeros((pad_rows, IDX_HD), F32)], axis=0)
        sc_ref[npast] = jnp.where(lane < tq, head_sum(_dot_nt(q2, ki_new.astype(BF16))), -jnp.inf)
        n_adm = jnp.full((tq, 1), float(p_len + tq), F32)
        thr_ref[...] = _topk_threshold(sc_ref, nkb, (tq, KEY_BLOCK), 1, n_sel, n_adm)
        m_ref[...] = jnp.full((rows, 1), NEG_BIG, F32)
        l_ref[...] = jnp.zeros((rows, 1), F32)
        acc_ref[...] = jnp.zeros((rows, B_HD), F32)

    qb = qb_ref[...] * (B_HD ** -0.5 * math.log2(math.e))
    q_h = [qb[:, h * B_HD:(h + 1) * B_HD].astype(BF16) for h in range(B_HEADS)]
    thr = thr_ref[...]

    def fold_in(logits_of, pv_of, sel):
        s = jnp.concatenate([jnp.where(sel, logits_of(h), NEG_BIG) for h in range(B_HEADS)], axis=0)
        m_old = m_ref[...]
        m_new = jnp.maximum(m_old, jnp.max(s, axis=-1, keepdims=True))
        alpha = jnp.exp2(m_old - m_new)
        p = jnp.exp2(s - m_new)
        pb = p.astype(BF16)
        pv = jnp.concatenate([pv_of(h, pb[h * tq:(h + 1) * tq, :]) for h in range(B_HEADS)], axis=0)
        l_ref[...] = alpha * l_ref[...] + jnp.sum(p, axis=-1, keepdims=True)
        acc_ref[...] = alpha * acc_ref[...] + pv
        m_ref[...] = m_new

    sel = jnp.concatenate([sc_ref[step * sub + u] > thr for u in range(sub)], axis=1)
    fold_in(lambda h: _dot(q_h[h], ck_ref[h].astype(BF16)),
            lambda h, p: _dot_nt(p, cv_ref[h].astype(BF16)), sel)

    @pl.when(step == nsteps - 1)
    def _():
        zpad = jnp.zeros((pad_rows, B_WIDTH), F32)
        kn = jnp.concatenate([kn_ref[...], zpad], axis=0).astype(BF16)
        vn = jnp.concatenate([vn_ref[...], zpad], axis=0).astype(BF16)
        fold_in(lambda h: _dot_nt(q_h[h], kn[:, h * B_HD:(h + 1) * B_HD]),
                lambda h, p: _dot(p, vn[:, h * B_HD:(h + 1) * B_HD]), sc_ref[npast] > thr)
        res = acc_ref[...] / l_ref[...]
        out = jnp.concatenate([res[h * tq:(h + 1) * tq, :] for h in range(B_HEADS)], axis=1)
        o_ref[...] = out * _silu(zb_ref[...])


def _dsa_decode(qb, qi, misc, zb, kb, vb, ck, cv, cki, *, bsz, t_len, p_len, n_sel):
    kstep = DEC_KEYS_PER_STEP
    assert p_len // KEY_BLOCK + 1 < COUNT_RADIX and p_len % kstep == 0
    rows = B_HEADS * t_len
    qmap = lambda b, s: (b, 0)
    cmap = lambda b, s: (b, 0, 0, s)
    ck, cv = (a.transpose(0, 2, 3, 1) for a in (ck, cv))
    cki = cki.transpose(0, 2, 1)
    return pl.pallas_call(
        functools.partial(_dsa_dec_body, p_len=p_len, tq=t_len, n_sel=n_sel, kstep=kstep),
        grid=(bsz, p_len // kstep),
        in_specs=[pl.BlockSpec((t_len, B_WIDTH), qmap),
                  pl.BlockSpec((t_len, IDX_HEADS * IDX_HD), qmap),
                  pl.BlockSpec((t_len, MISC_W), qmap),
                  pl.BlockSpec((t_len, B_WIDTH), qmap),
                  pl.BlockSpec((t_len, B_WIDTH), qmap),
                  pl.BlockSpec((t_len, B_WIDTH), qmap),
                  pl.BlockSpec((None, B_HEADS, B_HD, kstep), cmap),
                  pl.BlockSpec((None, B_HEADS, B_HD, kstep), cmap),
                  pl.BlockSpec((None, IDX_HD, p_len), lambda b, s: (b, 0, 0))],
        out_specs=pl.BlockSpec((t_len, B_WIDTH), qmap),
        out_shape=jax.ShapeDtypeStruct((bsz * t_len, B_WIDTH), F32),
        scratch_shapes=[pltpu.VMEM((p_len // KEY_BLOCK + 1, t_len, KEY_BLOCK), F32),
                        pltpu.VMEM((t_len, 1), F32),
                        pltpu.VMEM((rows, 1), F32),
                        pltpu.VMEM((rows, 1), F32),
                        pltpu.VMEM((rows, B_HD), F32)],
        compiler_params=_cparams(("parallel", "arbitrary")),
        name="dsa_decode",
    )(qb, qi, misc, zb, kb, vb, ck, cv, cki)


def _tail_body(x_ref, oa_ref, ob_ref, qm_ref, zm_ref, mk_ref, mv_ref, w_ref, y_ref):
    qm = qm_ref[...] * (M_HD ** -0.5)
    mk_t = mk_ref[...].astype(BF16)
    mv_t = mv_ref[...].astype(BF16)
    acc = _dot(oa_ref[...].astype(BF16), w_ref[0:A_WIDTH, :])
    acc = acc + _dot(ob_ref[...].astype(BF16), w_ref[A_WIDTH:A_WIDTH + B_WIDTH, :])
    om = jnp.zeros(qm.shape, F32)
    for h in range(M_HEADS):
        mask = _lane_mask(M_WIDTH, h * M_HD, (h + 1) * M_HD)
        s = _dot(jnp.where(mask, qm, 0.0).astype(BF16), mk_t)
        p = jnp.exp(s - jnp.max(s, axis=-1, keepdims=True))
        l = jnp.sum(p, axis=-1, keepdims=True)
        om = om + jnp.where(mask, _dot_nt(p.astype(BF16), mv_t) / l, 0.0)
    om = om * _silu(zm_ref[...])
    acc = acc + _dot(om.astype(BF16), w_ref[A_WIDTH + B_WIDTH:, :])
    y_ref[...] = x_ref[...] + acc


def _tail(x, oa, ob, qm, zm, mk, mv, w, *, bsz, t_len, tq):
    m, d = x.shape
    nq = t_len // tq
    qmap = lambda b, j: (b * nq + j, 0)
    mem_spec = pl.BlockSpec((None, M_WIDTH, N_MEM), lambda b, j: (b, 0, 0))
    return pl.pallas_call(
        _tail_body,
        grid=(bsz, nq),
        in_specs=[pl.BlockSpec((tq, d), qmap),
                  pl.BlockSpec((tq, A_WIDTH), qmap),
                  pl.BlockSpec((tq, B_WIDTH), qmap),
                  pl.BlockSpec((tq, M_WIDTH), qmap),
                  pl.BlockSpec((tq, M_WIDTH), qmap),
                  mem_spec, mem_spec,
                  pl.BlockSpec(w.shape, lambda b, j: (0, 0))],
        out_specs=pl.BlockSpec((tq, d), qmap),
        out_shape=jax.ShapeDtypeStruct((m, d), F32),
        compiler_params=_cparams(("parallel", "parallel")),
        name="tail",
    )(x, oa, ob, qm, zm, mk, mv, w)


def _regroup_body(wt_ref, o_ref):
    offs = [int(o) for o in np.concatenate([[0], np.cumsum(IN_SPLITS)])]
    grp = lambda i: wt_ref[offs[i]:offs[i + 1], :]
    (w_qkv, w_za, _, _, w_qb, w_kb, w_vb, w_zb, w_qi, w_ki, w_wi, w_qm, w_zm) = [grp(i) for i in range(13)]
    w_ba_aa = wt_ref[offs[2]:offs[4], :]
    pad = jnp.zeros((MISC_W - MISC_A - A_HEADS, wt_ref.shape[1]), F32)
    wt = jnp.concatenate([w_za, w_qb, w_kb, w_vb, w_zb, w_qi, w_qm, w_zm,
                          w_ki, w_wi, w_ba_aa, pad, w_qkv], axis=0)
    o_ref[...] = wt.T.astype(BF16)


def _regroup_weights(w_in):
    d, n_in = w_in.shape
    n_out = n_in - (IDX_HD + IDX_HEADS + 2 * A_HEADS) + MISC_W
    rows = 256
    return pl.pallas_call(
        _regroup_body,
        grid=(d // rows,),
        in_specs=[pl.BlockSpec((n_in, rows), lambda i: (0, i))],
        out_specs=pl.BlockSpec((rows, n_out), lambda i: (i, 0)),
        out_shape=jax.ShapeDtypeStruct((d, n_out), BF16),
        compiler_params=_cparams(("parallel",)),
        name="regroup",
    )(w_in.T)


def _prep_weights(g_in, w_in, conv_w, a_log, dt_bias, g_o, g_qb, g_kb, g_ki, g_qm, w_out):
    w_all = _regroup_weights(w_in)
    groups = [(A_WIDTH, False, ""), (B_WIDTH, True, ""),
              (B_WIDTH, True, "t"), (B_WIDTH, False, "t"), (B_WIDTH, False, ""),
              (IDX_HEADS * IDX_HD, False, ""), (M_WIDTH, True, ""), (M_WIDTH, False, ""),
              (MISC_W, True, ""),
              (A_WIDTH, False, "conv0"), (A_WIDTH, False, "conv1"), (A_WIDTH, False, "conv2")]
    bd64 = _group_mean_matrix(B_WIDTH, B_HD, B_WIDTH)
    ones256 = jnp.ones((1, B_WIDTH), F32)
    misc_gain = jnp.concatenate([g_ki, jnp.full((IDX_HEADS,), IDX_SCALE, F32),
                                 jnp.ones((MISC_W - MISC_B,), F32)])[None, :]
    misc_nm = (jnp.arange(MISC_W) < IDX_HD).astype(F32)[None, :]
    aux = [bd64, jnp.tile(g_qb, B_HEADS)[None, :], ones256,
           bd64, jnp.tile(g_kb, B_HEADS)[None, :], ones256,
           bd64, jnp.tile(g_qm, M_HEADS)[None, :], ones256,
           _group_mean_matrix(MISC_W, IDX_HD, IDX_HD), misc_gain, misc_nm]
    pad_a = lambda v: jnp.zeros((1, MISC_W), F32).at[0, MISC_A:MISC_A + A_HEADS].set(v)
    return dict(g_in=g_in[None, :], w_all=w_all, groups=groups, aux=aux, conv_w=conv_w,
                avec=pad_a(-jnp.exp(a_log)), dtvec=pad_a(dt_bias), g_o=g_o[None, :],
                w_out=w_out.astype(BF16))


def _layer(x, mem_k, mem_v, conv_buf, s0, past, gdn_chunk, gdn_rows, wp):
    bsz, t_len, d = x.shape
    m = bsz * t_len
    x2 = x.reshape(m, d)
    tm = min(PROJ_ROWS, m)
    heads_last = lambda a_t: a_t.reshape(bsz, B_HEADS, B_HD, t_len).transpose(0, 3, 1, 2)
    cbuf = jnp.concatenate([jnp.zeros((bsz, CARRY_ROWS - (CONV_W - 1), 3 * A_WIDTH), F32), conv_buf], axis=1)
    conv = (cbuf, wp["conv_w"])
    gdn_args = dict(bsz=bsz, t_len=t_len, chunk=gdn_chunk, rows=gdn_rows)
    if past is None:
        (za, qb, kb, vb, zb, qi, qm, zm, misc, qa, ka, va, kt, vt, conv_new) = _proj(
            x2, wp["g_in"], wp["w_all"], wp["groups"], wp["aux"], tm, t_len, conv=conv)
        k_new, v_new = heads_last(kt), heads_last(vt)
        oa, s_new = _gdn(qa, ka, va, za, misc, s0, wp["avec"], wp["dtvec"], wp["g_o"], **gdn_args)
    else:
        groups = [(wd, normed, "") for wd, normed, _ in wp["groups"]]
        (za, qb, kb, vb, zb, qi, qm, zm, misc, qa, ka, va) = _proj(x2, wp["g_in"], wp["w_all"], groups,
                                                                   wp["aux"], tm, m)
        k_new, v_new = (a.reshape(bsz, t_len, B_HEADS, B_HD) for a in (kb, vb))
        oa, s_new, conv_new = _gdn(qa, ka, va, za, misc, s0, wp["avec"], wp["dtvec"], wp["g_o"], conv=conv,
                                   **gdn_args)
    if past is None:
        n_sel = min(TOPK_MAX, t_len // 4)
        ob = _dsa_prompt(qb, qi, misc, zb, kb, vt, bsz=bsz, t_len=t_len, n_sel=n_sel, qblk=DSA_QBLK)
    else:
        ck, cv, cki = past
        p_len = ck.shape[1]
        assert (p_len + t_len - 1) // CHUNK <= p_len // CHUNK and p_len % KEY_BLOCK == 0
        n_sel = min(TOPK_MAX, (p_len + t_len) // 4)
        ob = _dsa_decode(qb, qi, misc, zb, kb, vb, ck, cv, cki, bsz=bsz, t_len=t_len, p_len=p_len,
                         n_sel=n_sel)
    y = _tail(x2, oa, ob, qm, zm, mem_k, mem_v, wp["w_out"], bsz=bsz, t_len=t_len, tq=min(TAIL_ROWS, t_len))
    return (y.reshape(bsz, t_len, d), conv_new, s_new, k_new, v_new,
            misc[:, :IDX_HD].reshape(bsz, t_len, IDX_HD))


def _memory_kv(mem, g_mem, w_mem_kv, g_km):
    bsz, n_mem, d = mem.shape
    groups = [(M_WIDTH, True, "t"), (M_WIDTH, False, "t")]
    aux = [_group_mean_matrix(M_WIDTH, M_HD, M_WIDTH), jnp.tile(g_km, M_HEADS)[None, :],
           jnp.ones((1, M_WIDTH), F32)]
    _, _, mk_t, mv_t = _proj(mem.reshape(bsz * n_mem, d), g_mem[None, :], w_mem_kv.astype(BF16),
                             groups, aux, n_mem, n_mem)
    heads_last = lambda a_t: a_t.reshape(bsz, M_HEADS, M_HD, n_mem).transpose(0, 3, 1, 2)
    return mk_t, mv_t, heads_last(mk_t), heads_last(mv_t)


def kernel(x_prompt, x_sample, state_conv_A, state_ssm_A, cache_k_B, cache_v_B, cache_kidx_B, cache_mem_k,
           cache_mem_v, mem_prompt, g_in, w_in, conv_w_A, a_log_A, dt_bias_A, g_o_A, g_q_B, g_k_B, g_kidx_B,
           g_mem, w_mem_kv, g_q_M, g_k_M, w_out):
    depth = w_in.shape[0]
    assert depth == 1
    l = 0
    bp, t_p, _ = x_prompt.shape
    bs, t_s, _ = x_sample.shape
    wp = _prep_weights(g_in[l], w_in[l], conv_w_A[l], a_log_A[l], dt_bias_A[l], g_o_A[l], g_q_B[l], g_k_B[l],
                       g_kidx_B[l], g_q_M[l], w_out[l])
    mk, mv, mk_heads, mv_heads = _memory_kv(mem_prompt, g_mem[l], w_mem_kv[l], g_k_M[l])
    feature_major = lambda c: c.transpose(0, 2, 3, 1).reshape(c.shape[0], M_WIDTH, N_MEM)
    zero_conv = jnp.zeros((bp, CONV_W - 1, 3 * A_WIDTH), F32)
    zero_ssm = jnp.zeros((bp, A_HEADS, A_DK, A_DV), F32)
    yp, c1, s1, k1, v1, ki1 = _layer(x_prompt, mk, mv, zero_conv, zero_ssm, None, CHUNK, GDN_CHUNKS_PER_STEP * CHUNK, wp)
    ys, c2, s2, k2, v2, ki2 = _layer(
        x_sample, feature_major(cache_mem_k[l]), feature_major(cache_mem_v[l]),
        state_conv_A[l], state_ssm_A[l], (cache_k_B[l], cache_v_B[l], cache_kidx_B[l]), t_s, t_s, wp)
    st = lambda a: a[None]
    return (yp, ys, st(c1), st(s1), st(k1), st(v1), st(ki1),
            st(mk_heads), st(mv_heads),
            st(c2), st(s2), st(k2), st(v2), st(ki2))
```

```python
import functools
import math

import numpy as np
import jax
import jax.numpy as jnp
from jax import lax
from jax.experimental import pallas as pl
from jax.experimental.pallas import tpu as pltpu

F32 = jnp.float32
BF16 = jnp.bfloat16
HIGHEST = lax.Precision.HIGHEST

D_MODEL = 1024
CHUNK = 64
A_HEADS = 4
A_DK = 128
A_DV = 128
A_WIDTH = A_HEADS * A_DV
CONV_W = 4
B_HEADS = 4
B_HD = 64
B_WIDTH = B_HEADS * B_HD
IDX_HEADS = 8
IDX_HD = 32
IDX_SCALE = (IDX_HEADS ** -0.5) * (IDX_HD ** -0.5)
TOPK_MAX = 256
Q_BLOCK = 128
N_MEM = 256
M_HEADS = 4
M_HD = 64
M_WIDTH = M_HEADS * M_HD
EPS = 1e-6
IN_SPLITS = (3 * A_WIDTH, A_WIDTH, A_HEADS, A_HEADS,
             B_WIDTH, B_WIDTH, B_WIDTH, B_WIDTH, IDX_HEADS * IDX_HD, IDX_HD, IDX_HEADS,
             M_WIDTH, M_WIDTH)

LANES = 128
KEY_BLOCK = 256
DSA_QBLK = 512
GDN_CHUNKS_PER_STEP = 8
TAIL_ROWS = 512
PROJ_ROWS = 512
DEC_KEYS_PER_STEP = 4096
SCORE_STRIP = 128
MISC_W = LANES
MISC_WI = IDX_HD
MISC_B = IDX_HD + IDX_HEADS
MISC_A = MISC_B + A_HEADS
VMEM_LIMIT = 48 * 1024 * 1024
NEG_BIG = -1e30
BISECT_PROBES = 2
COUNT_RADIX = 1024
BISECT_BLIND_ITERS = 10
BISECT_MAX_ITERS = 26


def _cparams(sem):
    return pltpu.CompilerParams(dimension_semantics=sem, vmem_limit_bytes=VMEM_LIMIT)


def _dot(a, b):
    return jnp.dot(a, b, preferred_element_type=F32)


def _dot_nt(a, b, precision=None):
    return lax.dot_general(a, b, (((1,), (1,)), ((), ())), precision=precision,
                           preferred_element_type=F32)


def _split2(x):
    hi = x.astype(BF16)
    return hi, (x - hi.astype(F32)).astype(BF16)


def _mm_split(a, b):
    ah, al = a
    bh, _ = b
    n = ah.shape[0]
    t = _dot(jnp.concatenate([ah, al], axis=0), bh)
    return t[:n] + t[n:]


def _silu(x):
    return x * jax.nn.sigmoid(x)


def _lane_mask(width, lo, hi):
    lane = lax.broadcasted_iota(jnp.int32, (1, width), 1)
    return (lane >= lo) & (lane < hi)


CARRY_ROWS = 8


def _conv_silu_norm(xp_ref, raw, cw, rows, part):
    first = CARRY_ROWS - (CONV_W - 1)
    xp_ref[CARRY_ROWS:CARRY_ROWS + rows, :] = raw
    y = xp_ref[first:first + rows, :] * cw[0:1, :]
    for j in range(1, CONV_W):
        y = y + xp_ref[first + j:first + j + rows, :] * cw[j:j + 1, :]
    y = _silu(y)
    conv_new = xp_ref[rows + first:rows + CARRY_ROWS, :]
    xp_ref[0:CARRY_ROWS, :] = xp_ref[rows:rows + CARRY_ROWS, :]
    if part == 2:
        return y, conv_new
    scale = A_DK ** -0.5 if part == 0 else 1.0
    heads = []
    for h in range(A_HEADS):
        yh = y[:, h * A_DK:(h + 1) * A_DK]
        heads.append(yh * (lax.rsqrt(jnp.sum(yh * yh, axis=-1, keepdims=True) + EPS) * scale))
    return jnp.concatenate(heads, axis=1), conv_new


def _conv_part(extra):
    return int(extra[len("conv"):]) if extra.startswith("conv") else None


def _proj_body(*refs, groups, tiles):
    x_ref, g_ref, w_ref = refs[0], refs[1], refs[2]
    n_aux = 3 * sum(1 for _, normed, _ in groups if normed)
    aux = refs[3:3 + n_aux]
    pos = 3 + n_aux
    n_conv = sum(1 for _, _, extra in groups if _conv_part(extra) is not None)
    if n_conv:
        cbuf_ref, cw_ref = refs[pos], refs[pos + 1]
        pos += 2
    outs = refs[pos:pos + len(groups)]
    pos += len(groups)
    n_t = sum(1 for _, _, extra in groups if extra == "t")
    outs_t = list(refs[pos:pos + n_t])
    pos += n_t
    if n_conv:
        convnew_ref = refs[pos]
        xp_refs = refs[pos + 1:pos + 1 + n_conv]

        @pl.when(pl.program_id(0) % tiles == 0)
        def _():
            for part, xp_ref in enumerate(xp_refs):
                xp_ref[0:CARRY_ROWS, :] = cbuf_ref[:, part * A_WIDTH:(part + 1) * A_WIDTH]
    x = x_ref[...]
    r = lax.rsqrt(jnp.mean(x * x, axis=-1, keepdims=True) + EPS)
    hb = (x * g_ref[...]).astype(BF16)
    off = 0
    ai = 0
    for (width, normed, extra), o_ref in zip(groups, outs):
        y = _dot(hb, w_ref[:, off:off + width]) * r
        part = _conv_part(extra)
        if part is not None:
            cols = slice(part * A_WIDTH, (part + 1) * A_WIDTH)
            y, conv_new = _conv_silu_norm(xp_refs[part], y, cw_ref[:, cols], y.shape[0], part)
            convnew_ref[:, cols] = conv_new
        if normed:
            bd_ref, gain_ref, nm_ref = aux[ai], aux[ai + 1], aux[ai + 2]
            ai += 3
            sq = y * y
            hi = sq.astype(BF16)
            lo = (sq - hi.astype(F32)).astype(BF16)
            ms = _dot(hi, bd_ref[...]) + _dot(lo, bd_ref[...])
            scale = jnp.where(nm_ref[...] > 0.0, lax.rsqrt(ms + EPS), 1.0)
            y = y * scale * gain_ref[...]
        o_ref[...] = y.astype(o_ref.dtype)
        if extra == "t":
            outs_t.pop(0)[...] = y.T
        off += width


def _group_mean_matrix(width, group, n_lanes):
    i = np.arange(width)
    m = ((i[:, None] // group) == (i[None, :] // group)) & (i[:, None] < n_lanes) & (i[None, :] < n_lanes)
    return jnp.asarray(m.astype(np.float32) / group, dtype=BF16)


def _proj(x, g, w, groups, aux, tm, t_len, conv=None):
    m, d = x.shape
    nw = w.shape[1]
    tiles = t_len // tm
    bsz = m // t_len
    in_specs = [pl.BlockSpec((tm, d), lambda i: (i, 0)),
                pl.BlockSpec((1, d), lambda i: (0, 0)),
                pl.BlockSpec((d, nw), lambda i: (0, 0))]
    for a in aux:
        in_specs.append(pl.BlockSpec(a.shape, lambda i: (0, 0)))
    out_shape = [jax.ShapeDtypeStruct((m, wd), BF16 if extra == "bf16" else F32) for wd, _, extra in groups]
    out_specs = [pl.BlockSpec((tm, wd), lambda i: (i, 0)) for wd, _, _ in groups]
    for wd, _, extra in groups:
        if extra == "t":
            out_shape.append(jax.ShapeDtypeStruct((bsz, wd, t_len), F32))
            out_specs.append(pl.BlockSpec((None, wd, tm), lambda i: (i // tiles, 0, i % tiles)))
    args = [x, g, w, *aux]
    scratch = []
    if conv is not None:
        cbuf, cw = conv
        wd = cw.shape[1]
        in_specs += [pl.BlockSpec((None, CARRY_ROWS, wd), lambda i: (i // tiles, 0, 0)),
                     pl.BlockSpec(cw.shape, lambda i: (0, 0))]
        args += [cbuf, cw]
        out_shape.append(jax.ShapeDtypeStruct((bsz, CONV_W - 1, wd), F32))
        out_specs.append(pl.BlockSpec((None, CONV_W - 1, wd), lambda i: (i // tiles, 0, 0)))
        scratch += [pltpu.VMEM((tm + CARRY_ROWS, A_WIDTH), F32) for _ in range(wd // A_WIDTH)]
    return pl.pallas_call(
        functools.partial(_proj_body, groups=tuple(groups), tiles=tiles),
        grid=(m // tm,),
        in_specs=in_specs, out_specs=out_specs, out_shape=out_shape, scratch_shapes=scratch,
        compiler_params=_cparams(("arbitrary",) if conv is not None else ("parallel",)),
        name="proj",
    )(*args)


def _gdn_body(*refs, chunk, rows, fused_conv):
    if fused_conv:
        (q_ref, k_ref, v_ref, za_ref, misc_ref, s0_ref, avec_ref, dtvec_ref, go_ref, cbuf_ref, cw_ref,
         o_ref, sfin_ref, convnew_ref, g_sc, b_sc, s_sc, *conv_sc) = refs
    else:
        (q_ref, k_ref, v_ref, za_ref, misc_ref, s0_ref, avec_ref, dtvec_ref, go_ref,
         o_ref, sfin_ref, g_sc, b_sc, s_sc) = refs
    t = pl.program_id(1)
    c = chunk

    @pl.when(t == 0)
    def _():
        s_sc[...] = s0_ref[0]

    if fused_conv:
        xp_refs, act_refs = conv_sc[:3], conv_sc[3:]

        @pl.when(t == 0)
        def _():
            for part, xp_ref in enumerate(xp_refs):
                xp_ref[0:CARRY_ROWS, :] = cbuf_ref[:, part * A_WIDTH:(part + 1) * A_WIDTH]
        for part, raw_ref in enumerate((q_ref, k_ref, v_ref)):
            cols = slice(part * A_WIDTH, (part + 1) * A_WIDTH)
            act, conv_new = _conv_silu_norm(xp_refs[part], raw_ref[...], cw_ref[:, cols], rows, part)
            convnew_ref[:, cols] = conv_new
            act_refs[part][...] = act
        q_sc, k_sc, v_sc = act_refs
    else:
        q_sc, k_sc, v_sc = q_ref, k_ref, v_ref
    misc = misc_ref[...]
    b_sc[...] = jax.nn.sigmoid(misc)
    g_sc[...] = avec_ref[...] * jax.nn.softplus(misc + dtvec_ref[...])

    ri = lax.broadcasted_iota(jnp.int32, (c, c), 0)
    ci = lax.broadcasted_iota(jnp.int32, (c, c), 1)
    tri = ri >= ci
    strict = ri > ci
    eye_f = (ri == ci).astype(F32)
    tri3 = (lax.broadcasted_iota(jnp.int32, (c, 3 * c), 0)
            >= lax.broadcasted_iota(jnp.int32, (c, 3 * c), 1) % c).astype(BF16)
    go = go_ref[...]
    n_chunks = rows // c

    problems = [(ic, h) for ic in range(n_chunks) for h in range(A_HEADS)]
    gc_all, gc_t = [], []
    for ic in range(n_chunks):
        g = g_sc[ic * c:(ic + 1) * c, :]
        g1 = g.astype(BF16)
        r1 = g - g1.astype(F32)
        g2 = r1.astype(BF16)
        g3 = (r1 - g2.astype(F32)).astype(BF16)
        gc_all.append(_dot(tri3, jnp.concatenate([g1, g2, g3], axis=0)))
    for ic in range(n_chunks):
        gc_t.append(gc_all[ic].T)

    def load(ref, ic, h):
        return ref[ic * c:(ic + 1) * c, h * A_DK:(h + 1) * A_DK]

    col = lambda a, lane: a[:, lane:lane + 1]
    k_bf = [load(k_sc, ic, h).astype(BF16) for ic, h in problems]
    kb = [load(k_sc, ic, h) * col(b_sc[ic * c:(ic + 1) * c, :], MISC_B + h) for ic, h in problems]
    kk = [_dot_nt(kb[i].astype(BF16), k_bf[i]) for i in range(len(problems))]
    qk_raw = [_dot_nt(load(q_sc, ic, h).astype(BF16), k_bf[i]) for i, (ic, h) in enumerate(problems)]
    decay = []
    for ic, h in problems:
        diff = col(gc_all[ic], MISC_A + h) - gc_t[ic][MISC_A + h:MISC_A + h + 1, :]
        decay.append(jnp.where(tri, jnp.exp(jnp.where(tri, diff, 0.0)), 0.0))
    lmat = [jnp.where(strict, kk[i] * decay[i], 0.0) for i in range(len(problems))]
    qk = [(qk_raw[i] * decay[i]).astype(BF16) for i in range(len(problems))]
    x_inv = [eye_f - m for m in lmat]
    p_split = [_split2(m) for m in lmat]
    for _ in range(int(math.log2(c)) - 1):
        p_split = [_split2(_mm_split(ps, ps)) for ps in p_split]
        x_inv = [x + _mm_split(_split2(x), ps) for x, ps in zip(x_inv, p_split)]
    sol = []
    for i, (ic, h) in enumerate(problems):
        egc = jnp.exp(col(gc_all[ic], MISC_A + h))
        beta = col(b_sc[ic * c:(ic + 1) * c, :], MISC_B + h)
        rhs = jnp.concatenate([load(v_sc, ic, h) * beta, kb[i] * egc], axis=-1).astype(BF16)
        sol.append(_mm_split(_split2(x_inv[i]), (rhs, None)))
    pre = []
    for i, (ic, h) in enumerate(problems):
        gc = col(gc_all[ic], MISC_A + h)
        g_last = gc[c - 1:c, :]
        k_dec_t = (load(k_sc, ic, h) * jnp.exp(g_last - gc)).T.astype(BF16)
        wq = jnp.concatenate([sol[i][:, A_DV:], load(q_sc, ic, h) * jnp.exp(gc)], axis=0).astype(BF16)
        pre.append((sol[i][:, :A_DV], wq, k_dec_t, jnp.exp(g_last)))

    heads = range(A_HEADS)
    s_cur = [s_sc[h] for h in heads]
    for ic in range(n_chunks):
        pr = [pre[ic * A_HEADS + h] for h in heads]
        ws = [_dot(pr[h][1], s_cur[h].astype(BF16)) for h in heads]
        ub = [(pr[h][0] - ws[h][:c]).astype(BF16) for h in heads]
        o = [ws[h][c:] + _dot(qk[ic * A_HEADS + h], ub[h]) for h in heads]
        s_cur = [s_cur[h] * pr[h][3] + _dot(pr[h][2], ub[h]) for h in heads]
        for h in heads:
            on = o[h] * lax.rsqrt(jnp.mean(o[h] * o[h], axis=-1, keepdims=True) + EPS) * go
            o_ref[ic * c:(ic + 1) * c, h * A_DV:(h + 1) * A_DV] = (on * _silu(load(za_ref, ic, h))).astype(BF16)
    for h in heads:
        s_sc[h] = s_cur[h]
    sfin_ref[0] = s_sc[...]


def _gdn(q, k, v, za, misc, s0, avec, dtvec, g_o, *, bsz, t_len, chunk, rows, conv=None):
    nt = t_len // rows
    w3 = 3 * A_WIDTH
    row_map = lambda b, t: (b * nt + t, 0)
    const2 = lambda b, t: (0, 0)
    state_spec = pl.BlockSpec((1, A_HEADS, A_DK, A_DV), lambda b, t: (b, 0, 0, 0))
    in_specs = [pl.BlockSpec((rows, A_WIDTH), row_map)] * 4 + [
                pl.BlockSpec((rows, MISC_W), row_map),
                state_spec,
                pl.BlockSpec((1, MISC_W), const2),
                pl.BlockSpec((1, MISC_W), const2),
                pl.BlockSpec((1, A_DV), const2)]
    args = [q, k, v, za, misc, s0, avec, dtvec, g_o]
    out_specs = [pl.BlockSpec((rows, A_WIDTH), row_map), state_spec]
    out_shape = [jax.ShapeDtypeStruct((bsz * t_len, A_WIDTH), BF16),
                 jax.ShapeDtypeStruct((bsz, A_HEADS, A_DK, A_DV), F32)]
    scratch = [pltpu.VMEM((rows, MISC_W), F32),
               pltpu.VMEM((rows, MISC_W), F32),
               pltpu.VMEM((A_HEADS, A_DK, A_DV), F32)]
    if conv is not None:
        cbuf, cw = conv
        in_specs += [pl.BlockSpec((None, CARRY_ROWS, w3), lambda b, t: (b, 0, 0)),
                     pl.BlockSpec((CONV_W, w3), const2)]
        args += [cbuf, cw]
        out_specs.append(pl.BlockSpec((None, CONV_W - 1, w3), lambda b, t: (b, 0, 0)))
        out_shape.append(jax.ShapeDtypeStruct((bsz, CONV_W - 1, w3), F32))
        scratch += [pltpu.VMEM((rows + CARRY_ROWS, A_WIDTH), F32)] * 3 + [pltpu.VMEM((rows, A_WIDTH), F32)] * 3
    return pl.pallas_call(
        functools.partial(_gdn_body, chunk=chunk, rows=rows, fused_conv=conv is not None),
        grid=(bsz, nt),
        in_specs=in_specs, out_specs=out_specs, out_shape=out_shape, scratch_shapes=scratch,
        compiler_params=_cparams(("parallel", "arbitrary")),
        name="gdn",
    )(*args)


def _topk_threshold(sc_ref, nkb, shape, key_axis, n_sel, n_adm, even_blocks=False):
    kf = float(n_sel)
    kblk = shape[key_axis]

    if key_axis == 0:
        strip = min(kblk, 32)
        part_shape = (strip, shape[1])
        strip_of = lambda kb, i: sc_ref[kb, i * strip:(i + 1) * strip, :]

        def put_strip(kb, i, v):
            sc_ref[kb, i * strip:(i + 1) * strip, :] = v
    else:
        strip = kblk
        part_shape = shape
        strip_of = lambda kb, i: sc_ref[kb]

        def put_strip(kb, i, v):
            sc_ref[kb] = v
    kpos = lax.broadcasted_iota(jnp.int32, part_shape, key_axis)

    def reduce_blocks(specs):
        def body(kb, accs):
            accs = list(accs)
            for i in range(kblk // strip):
                s = strip_of(kb, i)
                k0 = kb * kblk + i * strip
                accs = [cmb(acc, fn(s, k0)) for acc, (fn, _, _, _, cmb) in zip(accs, specs)]
            return tuple(accs)
        init = tuple(jnp.full(part_shape, i, dt) for _, i, dt, _, _ in specs)
        accs = lax.fori_loop(0, nkb, body, init)
        return [acc if op is None else op(acc, axis=key_axis, keepdims=True)
                for acc, (_, _, _, op, _) in zip(accs, specs)]

    one_if = lambda c: jnp.where(c, 1.0, 0.0)
    vmax, vmin, cpos, czero = reduce_blocks([
        (lambda s, k0: s, -jnp.inf, F32, jnp.max, jnp.maximum),
        (lambda s, k0: jnp.where(s == -jnp.inf, jnp.inf, s), jnp.inf, F32, jnp.min, jnp.minimum),
        (lambda s, k0: one_if(s > 0.0), 0.0, F32, jnp.sum, jnp.add),
        (lambda s, k0: one_if(s == 0.0), 0.0, F32, jnp.sum, jnp.add)])

    long_row = n_adm > kf
    pos_row = long_row & (cpos >= kf)
    zero_row = long_row & (cpos < kf) & (cpos + czero >= kf)
    neg_row = long_row & (cpos + czero < kf)
    lo0 = jnp.where(pos_row | zero_row, 0.0, vmin - jnp.maximum(1.0, jnp.abs(vmin)))
    cnt_lo0 = jnp.where(pos_row, cpos, jnp.where(zero_row, kf, n_adm))
    hi0 = jnp.where(neg_row, 0.0, vmax)
    cnt_hi0 = jnp.where(neg_row, cpos, 0.0)
    need_zero = jnp.where(zero_row, kf - cpos, 0.0)

    def narrow(lo, hi, cnt_lo, cnt_hi):
        fracs = [(t + 1.0) / (BISECT_PROBES + 1.0) for t in range(BISECT_PROBES)]
        mids = [lo * (1.0 - f) + hi * f for f in fracs]
        for t in range(1, BISECT_PROBES):
            mids[t] = jnp.maximum(mids[t], mids[t - 1])
        codes = [sum(COUNT_RADIX ** u for u in range(t + 1)) for t in range(BISECT_PROBES)]

        def encode(s, k0):
            e = jnp.zeros(s.shape, jnp.int32)
            for mid, code in zip(mids, codes):
                e = jnp.where(s > mid, code, e)
            return e
        (packed,) = reduce_blocks([(encode, 0, jnp.int32, None, jnp.add)])
        cnts = []
        for t in range(BISECT_PROBES):
            digit = packed & (COUNT_RADIX - 1) if t < BISECT_PROBES - 1 else packed
            packed = packed >> COUNT_RADIX.bit_length() - 1
            cnts.append(jnp.sum(digit.astype(F32), axis=key_axis, keepdims=True))
        lo_n, cnt_lo_n, hi_n, cnt_hi_n = lo, cnt_lo, hi, cnt_hi
        for mid, cm in zip(mids, cnts):
            up = cm >= kf
            lo_n = jnp.where(up, mid, lo_n)
            cnt_lo_n = jnp.where(up, cm, cnt_lo_n)
        for mid, cm in zip(mids[::-1], cnts[::-1]):
            dn = cm < kf
            hi_n = jnp.where(dn, mid, hi_n)
            cnt_hi_n = jnp.where(dn, cm, cnt_hi_n)
        return lo_n, hi_n, cnt_lo_n, cnt_hi_n

    def unresolved(cnt_lo):
        return (jnp.max(cnt_lo) > kf).astype(jnp.int32)

    blind = jnp.where(unresolved(cnt_lo0) > 0, BISECT_BLIND_ITERS, 0)
    state = lax.fori_loop(0, blind, lambda i, st: narrow(*st), (lo0, hi0, cnt_lo0, cnt_hi0))

    def cond(st):
        return (st[0] < BISECT_MAX_ITERS) & (st[1] > 0)

    def body(st):
        new = narrow(*st[2:])
        return (st[0] + 1, unresolved(new[2])) + new

    _, _, lo, hi, cnt_lo, cnt_hi = lax.while_loop(
        cond, body, (blind, unresolved(state[2])) + state)

    fix = cnt_lo > kf
    need0 = jnp.where(fix, kf - cnt_hi, 0.0)
    big_idx = 2 ** 30

    def fix_cond(need):
        return jnp.max(need) > 0.0

    def fix_body(need):
        def in_cluster(s):
            return (s > lo) & (s <= hi)
        (mval,) = reduce_blocks([(lambda s, k0: jnp.where(in_cluster(s), s, -jnp.inf), -jnp.inf, F32,
                                  jnp.max, jnp.maximum)])
        (idx,) = reduce_blocks(
            [(lambda s, k0: jnp.where(in_cluster(s) & (s == mval), kpos + k0, big_idx),
              big_idx, jnp.int32, jnp.min, jnp.minimum)])
        active = need > 0.0

        def promote(kb, carry):
            for i in range(kblk // strip):
                s = strip_of(kb, i)
                put_strip(kb, i, jnp.where(active & ((kpos + kb * kblk + i * strip) == idx), jnp.inf, s))
            return carry
        lax.fori_loop(0, nkb, promote, 0)
        return jnp.where(active, need - 1.0, need)

    lax.while_loop(fix_cond, fix_body, need0)

    @pl.when(jnp.max(need_zero) > 0.0)
    def _():
        r = lax.broadcasted_iota(jnp.int32, (kblk, kblk), 0)
        c = lax.broadcasted_iota(jnp.int32, (kblk, kblk), 1)
        tri = ((r >= c) if key_axis == 0 else (r <= c)).astype(BF16)

        def tie_blk(kb, seen):
            s = sc_ref[kb]
            z = (s == 0.0) & zero_row
            zb = one_if(z).astype(BF16)
            if key_axis == 0:
                rank = _dot(tri, zb) + seen
                last = rank[kblk - 1:kblk, :]
            else:
                rank = _dot(zb, tri) + seen
                last = rank[:, kblk - 1:kblk]
            sc_ref[kb] = jnp.where(z & (rank <= need_zero), jnp.inf, s)
            return last
        if even_blocks:
            lax.fori_loop(0, nkb // 2, lambda i, seen: tie_blk(2 * i + 1, tie_blk(2 * i, seen)),
                          jnp.zeros_like(need_zero))
        else:
            lax.fori_loop(0, nkb, tie_blk, jnp.zeros_like(need_zero))

    return jnp.where(fix, hi, lo)


def _dsa_body(qb_ref, qi_ref, miscq_ref, zb_ref, k_ref, vt_ref, misck_ref, o_ref, sc_ref, s_ref, *, n_sel, qblk):
    j = pl.program_id(1)
    nkb = ((j + 1) * qblk + KEY_BLOCK - 1) // KEY_BLOCK
    qi = qi_ref[...]
    q_h = [qi[:, h * IDX_HD:(h + 1) * IDX_HD].astype(BF16) for h in range(IDX_HEADS)]
    w_t = miscq_ref[...].T[MISC_WI:MISC_WI + IDX_HEADS, :]
    qchunk = (j * qblk + lax.broadcasted_iota(jnp.int32, (1, qblk), 1)) // CHUNK
    krow = lax.broadcasted_iota(jnp.int32, (KEY_BLOCK, 1), 0)

    def fold(acc, x, combine):
        for i in range(x.shape[0] // 8):
            acc = combine(acc, x[8 * i:8 * (i + 1)])
        return acc

    def score_blk(kb, carry):
        k0 = pl.multiple_of(kb * KEY_BLOCK, KEY_BLOCK)
        for half in range(KEY_BLOCK // SCORE_STRIP):
            r0 = half * SCORE_STRIP
            ki = misck_ref[pl.ds(k0 + r0, SCORE_STRIP), 0:IDX_HD].astype(BF16)
            acc = jnp.zeros((SCORE_STRIP, qblk), F32)
            for h in range(IDX_HEADS):
                acc = acc + w_t[h:h + 1, :] * jnp.maximum(_dot_nt(ki, q_h[h]), 0.0)
            adm = ((k0 + r0 + krow[:SCORE_STRIP]) // CHUNK) <= qchunk
            sc_ref[kb, r0:r0 + SCORE_STRIP, :] = jnp.where(adm, acc, -jnp.inf)
        return carry

    npairs = (nkb + 1) // 2

    def pair_loop(body, init):
        return lax.fori_loop(0, npairs, lambda i, c: body(2 * i + 1, body(2 * i, c)), init)

    pair_loop(score_blk, 0)

    n_adm = ((qchunk + 1) * CHUNK).astype(F32)
    thr = _topk_threshold(sc_ref, nkb, (KEY_BLOCK, qblk), 0, n_sel, n_adm,
                          even_blocks=qblk % (2 * KEY_BLOCK) == 0)

    qb = qb_ref[...] * (B_HD ** -0.5 * math.log2(math.e))
    qs = jnp.concatenate([jnp.where(_lane_mask(B_WIDTH, h * B_HD, (h + 1) * B_HD), qb, 0.0)
                          for h in range(B_HEADS)], axis=0).astype(BF16)
    wide = B_HEADS * qblk

    def logits_blk(kb, m8):
        k0 = pl.multiple_of(kb * KEY_BLOCK, KEY_BLOCK)
        s = _dot_nt(k_ref[pl.ds(k0, KEY_BLOCK), :].astype(BF16), qs)
        sel = sc_ref[kb] > thr
        s = jnp.where(jnp.concatenate([sel] * B_HEADS, axis=1), s, NEG_BIG)
        s_ref[kb] = s
        return fold(m8, s, jnp.maximum)

    m8 = pair_loop(logits_blk, jnp.full((8, wide), NEG_BIG, F32))
    m = jnp.max(m8, axis=0, keepdims=True)

    def pv_blk(kb, carry):
        l8, accs = carry
        p = jnp.exp2(s_ref[kb] - m)
        pb = p.astype(BF16)
        k0 = pl.multiple_of(kb * KEY_BLOCK, KEY_BLOCK)
        vt = vt_ref[:, pl.ds(k0, KEY_BLOCK)].astype(BF16)
        accs = tuple(accs[h] + _dot(vt[h * B_HD:(h + 1) * B_HD, :], pb[:, h * qblk:(h + 1) * qblk])
                     for h in range(B_HEADS))
        return fold(l8, p, jnp.add), accs

    l8, accs = pair_loop(
        pv_blk, (jnp.zeros((8, wide), F32), tuple(jnp.zeros((B_HD, qblk), F32) for _ in range(B_HEADS))))
    l = jnp.sum(l8, axis=0, keepdims=True)
    o_t = jnp.concatenate([accs[h] / l[:, h * qblk:(h + 1) * qblk] for h in range(B_HEADS)], axis=0)
    o_ref[...] = (o_t.T * _silu(zb_ref[...])).astype(BF16)


def _dsa_prompt(qb, qi, misc, zb, kb, vt, *, bsz, t_len, n_sel, qblk):
    nq = t_len // qblk
    nkb = t_len // KEY_BLOCK
    assert nkb * (KEY_BLOCK // 8) < COUNT_RADIX and nkb % 2 == 0
    qmap = lambda b, j: (b * nq + j, 0)
    kmap = lambda b, j: (b, 0)
    return pl.pallas_call(
        functools.partial(_dsa_body, n_sel=n_sel, qblk=qblk),
        grid=(bsz, nq),
        in_specs=[pl.BlockSpec((qblk, B_WIDTH), qmap),
                  pl.BlockSpec((qblk, IDX_HEADS * IDX_HD), qmap),
                  pl.BlockSpec((qblk, MISC_W), qmap),
                  pl.BlockSpec((qblk, B_WIDTH), qmap),
                  pl.BlockSpec((t_len, B_WIDTH), kmap),
                  pl.BlockSpec((None, B_WIDTH, t_len), lambda b, j: (b, 0, 0)),
                  pl.BlockSpec((t_len, MISC_W), kmap)],
        out_specs=pl.BlockSpec((qblk, B_WIDTH), qmap),
        out_shape=jax.ShapeDtypeStruct((bsz * t_len, B_WIDTH), BF16),
        scratch_shapes=[pltpu.VMEM((nkb, KEY_BLOCK, qblk), F32),
                        pltpu.VMEM((nkb, KEY_BLOCK, B_HEADS * qblk), F32)],
        compiler_params=_cparams(("parallel", "arbitrary")),
        name="dsa_prompt",
    )(qb, qi, misc, zb, kb, vt, misc)


def _dsa_dec_body(qb_ref, qi_ref, misc_ref, zb_ref, kn_ref, vn_ref, ck_ref, cv_ref, cki_ref, o_ref,
                  sc_ref, thr_ref, m_ref, l_ref, acc_ref, *, p_len, tq, n_sel, kstep):
    step = pl.program_id(1)
    nsteps = p_len // kstep
    sub = kstep // KEY_BLOCK
    npast = p_len // KEY_BLOCK
    nkb = npast + 1
    pad_rows = KEY_BLOCK - tq
    rows = B_HEADS * tq

    @pl.when(step == 0)
    def _():
        qi = qi_ref[...]
        misc = misc_ref[...]
        q2 = jnp.concatenate([qi[:, h * IDX_HD:(h + 1) * IDX_HD] for h in range(IDX_HEADS)],
                             axis=0).astype(BF16)
        w_h = [misc[:, MISC_WI + h:MISC_WI + h + 1] for h in range(IDX_HEADS)]

        def head_sum(lg):
            acc = jnp.zeros((tq, lg.shape[1]), F32)
            for h in range(IDX_HEADS):
                acc = acc + w_h[h] * jnp.maximum(lg[h * tq:(h + 1) * tq, :], 0.0)
            return acc

        for i in range(nsteps):
            sc = head_sum(_dot(q2, cki_ref[:, i * kstep:(i + 1) * kstep].astype(BF16)))
            for u in range(sub):
                sc_ref[i * sub + u] = sc[:, u * KEY_BLOCK:(u + 1) * KEY_BLOCK]
        lane = lax.broadcasted_iota(jnp.int32, (1, KEY_BLOCK), 1)
        ki_new = jnp.concatenate([misc[:, 0:IDX_HD], jnp.zeros((pad_rows, IDX_HD), F32)], axis=0)
        sc_ref[npast] = jnp.where(lane < tq, head_sum(_dot_nt(q2, ki_new.astype(BF16))), -jnp.inf)
        n_adm = jnp.full((tq, 1), float(p_len + tq), F32)
        thr_ref[...] = _topk_threshold(sc_ref, nkb, (tq, KEY_BLOCK), 1, n_sel, n_adm)
        m_ref[...] = jnp.full((rows, 1), NEG_BIG, F32)
        l_ref[...] = jnp.zeros((rows, 1), F32)
        acc_ref[...] = jnp.zeros((rows, B_HD), F32)

    qb = qb_ref[...] * (B_HD ** -0.5 * math.log2(math.e))
    q_h = [qb[:, h * B_HD:(h + 1) * B_HD].astype(BF16) for h in range(B_HEADS)]
    thr = thr_ref[...]

    def fold_in(logits_of, pv_of, sel):
        s = jnp.concatenate([jnp.where(sel, logits_of(h), NEG_BIG) for h in range(B_HEADS)], axis=0)
        m_old = m_ref[...]
        m_new = jnp.maximum(m_old, jnp.max(s, axis=-1, keepdims=True))
        alpha = jnp.exp2(m_old - m_new)
        p = jnp.exp2(s - m_new)
        pb = p.astype(BF16)
        pv = jnp.concatenate([pv_of(h, pb[h * tq:(h + 1) * tq, :]) for h in range(B_HEADS)], axis=0)
        l_ref[...] = alpha * l_ref[...] + jnp.sum(p, axis=-1, keepdims=True)
        acc_ref[...] = alpha * acc_ref[...] + pv
        m_ref[...] = m_new

    sel = jnp.concatenate([sc_ref[step * sub + u] > thr for u in range(sub)], axis=1)
    fold_in(lambda h: _dot(q_h[h], ck_ref[h].astype(BF16)),
            lambda h, p: _dot_nt(p, cv_ref[h].astype(BF16)), sel)

    @pl.when(step == nsteps - 1)
    def _():
        zpad = jnp.zeros((pad_rows, B_WIDTH), F32)
        kn = jnp.concatenate([kn_ref[...], zpad], axis=0).astype(BF16)
        vn = jnp.concatenate([vn_ref[...], zpad], axis=0).astype(BF16)
        fold_in(lambda h: _dot_nt(q_h[h], kn[:, h * B_HD:(h + 1) * B_HD]),
                lambda h, p: _dot(p, vn[:, h * B_HD:(h + 1) * B_HD]), sc_ref[npast] > thr)
        res = acc_ref[...] / l_ref[...]
        out = jnp.concatenate([res[h * tq:(h + 1) * tq, :] for h in range(B_HEADS)], axis=1)
        o_ref[...] = (out * _silu(zb_ref[...])).astype(BF16)


def _dsa_decode(qb, qi, misc, zb, kb, vb, ck, cv, cki, *, bsz, t_len, p_len, n_sel):
    kstep = DEC_KEYS_PER_STEP
    assert p_len // KEY_BLOCK + 1 < COUNT_RADIX and p_len % kstep == 0
    rows = B_HEADS * t_len
    qmap = lambda b, s: (b, 0)
    cmap = lambda b, s: (b, 0, 0, s)
    ck, cv = (a.transpose(0, 2, 3, 1) for a in (ck, cv))
    cki = cki.transpose(0, 2, 1)
    return pl.pallas_call(
        functools.partial(_dsa_dec_body, p_len=p_len, tq=t_len, n_sel=n_sel, kstep=kstep),
        grid=(bsz, p_len // kstep),
        in_specs=[pl.BlockSpec((t_len, B_WIDTH), qmap),
                  pl.BlockSpec((t_len, IDX_HEADS * IDX_HD), qmap),
                  pl.BlockSpec((t_len, MISC_W), qmap),
                  pl.BlockSpec((t_len, B_WIDTH), qmap),
                  pl.BlockSpec((t_len, B_WIDTH), qmap),
                  pl.BlockSpec((t_len, B_WIDTH), qmap),
                  pl.BlockSpec((None, B_HEADS, B_HD, kstep), cmap),
                  pl.BlockSpec((None, B_HEADS, B_HD, kstep), cmap),
                  pl.BlockSpec((None, IDX_HD, p_len), lambda b, s: (b, 0, 0))],
        out_specs=pl.BlockSpec((t_len, B_WIDTH), qmap),
        out_shape=jax.ShapeDtypeStruct((bsz * t_len, B_WIDTH), BF16),
        scratch_shapes=[pltpu.VMEM((p_len // KEY_BLOCK + 1, t_len, KEY_BLOCK), F32),
                        pltpu.VMEM((t_len, 1), F32),
                        pltpu.VMEM((rows, 1), F32),
                        pltpu.VMEM((rows, 1), F32),
                        pltpu.VMEM((rows, B_HD), F32)],
        compiler_params=_cparams(("parallel", "arbitrary")),
        name="dsa_decode",
    )(qb, qi, misc, zb, kb, vb, ck, cv, cki)


def _tail_body(x_ref, oa_ref, ob_ref, qm_ref, zm_ref, mk_ref, mv_ref, w_ref, y_ref):
    qm = qm_ref[...] * (M_HD ** -0.5)
    mk_t = mk_ref[...].astype(BF16)
    mv_t = mv_ref[...].astype(BF16)
    acc = _dot(oa_ref[...].astype(BF16), w_ref[0:A_WIDTH, :])
    acc = acc + _dot(ob_ref[...].astype(BF16), w_ref[A_WIDTH:A_WIDTH + B_WIDTH, :])
    om = jnp.zeros(qm.shape, F32)
    for h in range(M_HEADS):
        mask = _lane_mask(M_WIDTH, h * M_HD, (h + 1) * M_HD)
        s = _dot(jnp.where(mask, qm, 0.0).astype(BF16), mk_t)
        p = jnp.exp(s - jnp.max(s, axis=-1, keepdims=True))
        l = jnp.sum(p, axis=-1, keepdims=True)
        om = om + jnp.where(mask, _dot_nt(p.astype(BF16), mv_t) / l, 0.0)
    om = om * _silu(zm_ref[...])
    acc = acc + _dot(om.astype(BF16), w_ref[A_WIDTH + B_WIDTH:, :])
    y_ref[...] = x_ref[...] + acc


def _tail(x, oa, ob, qm, zm, mk, mv, w, *, bsz, t_len, tq):
    m, d = x.shape
    nq = t_len // tq
    qmap = lambda b, j: (b * nq + j, 0)
    mem_spec = pl.BlockSpec((None, M_WIDTH, N_MEM), lambda b, j: (b, 0, 0))
    return pl.pallas_call(
        _tail_body,
        grid=(bsz, nq),
        in_specs=[pl.BlockSpec((tq, d), qmap),
                  pl.BlockSpec((tq, A_WIDTH), qmap),
                  pl.BlockSpec((tq, B_WIDTH), qmap),
                  pl.BlockSpec((tq, M_WIDTH), qmap),
                  pl.BlockSpec((tq, M_WIDTH), qmap),
                  mem_spec, mem_spec,
                  pl.BlockSpec(w.shape, lambda b, j: (0, 0))],
        out_specs=pl.BlockSpec((tq, d), qmap),
        out_shape=jax.ShapeDtypeStruct((m, d), F32),
        compiler_params=_cparams(("parallel", "parallel")),
        name="tail",
    )(x, oa, ob, qm, zm, mk, mv, w)


def _regroup_body(wt_ref, o_ref):
    offs = [int(o) for o in np.concatenate([[0], np.cumsum(IN_SPLITS)])]
    grp = lambda i: wt_ref[offs[i]:offs[i + 1], :]
    (w_qkv, w_za, _, _, w_qb, w_kb, w_vb, w_zb, w_qi, w_ki, w_wi, w_qm, w_zm) = [grp(i) for i in range(13)]
    w_ba_aa = wt_ref[offs[2]:offs[4], :]
    pad = jnp.zeros((MISC_W - MISC_A - A_HEADS, wt_ref.shape[1]), F32)
    wt = jnp.concatenate([w_za, w_qb, w_kb, w_vb, w_zb, w_qi, w_qm, w_zm,
                          w_ki, w_wi, w_ba_aa, pad, w_qkv], axis=0)
    o_ref[...] = wt.T.astype(BF16)


def _regroup_weights(w_in):
    d, n_in = w_in.shape
    n_out = n_in - (IDX_HD + IDX_HEADS + 2 * A_HEADS) + MISC_W
    rows = 256
    return pl.pallas_call(
        _regroup_body,
        grid=(d // rows,),
        in_specs=[pl.BlockSpec((n_in, rows), lambda i: (0, i))],
        out_specs=pl.BlockSpec((rows, n_out), lambda i: (i, 0)),
        out_shape=jax.ShapeDtypeStruct((d, n_out), BF16),
        compiler_params=_cparams(("parallel",)),
        name="regroup",
    )(w_in.T)


def _prep_weights(g_in, w_in, conv_w, a_log, dt_bias, g_o, g_qb, g_kb, g_ki, g_qm, w_out):
    w_all = _regroup_weights(w_in)
    groups = [(A_WIDTH, False, ""), (B_WIDTH, True, ""),
              (B_WIDTH, True, "t"), (B_WIDTH, False, "t"), (B_WIDTH, False, ""),
              (IDX_HEADS * IDX_HD, False, ""), (M_WIDTH, True, "bf16"), (M_WIDTH, False, ""),
              (MISC_W, True, ""),
              (A_WIDTH, False, "conv0"), (A_WIDTH, False, "conv1"), (A_WIDTH, False, "conv2")]
    bd64 = _group_mean_matrix(B_WIDTH, B_HD, B_WIDTH)
    ones256 = jnp.ones((1, B_WIDTH), F32)
    misc_gain = jnp.concatenate([g_ki, jnp.full((IDX_HEADS,), IDX_SCALE, F32),
                                 jnp.ones((MISC_W - MISC_B,), F32)])[None, :]
    misc_nm = (jnp.arange(MISC_W) < IDX_HD).astype(F32)[None, :]
    aux = [bd64, jnp.tile(g_qb, B_HEADS)[None, :], ones256,
           bd64, jnp.tile(g_kb, B_HEADS)[None, :], ones256,
           bd64, jnp.tile(g_qm, M_HEADS)[None, :], ones256,
           _group_mean_matrix(MISC_W, IDX_HD, IDX_HD), misc_gain, misc_nm]
    pad_a = lambda v: jnp.zeros((1, MISC_W), F32).at[0, MISC_A:MISC_A + A_HEADS].set(v)
    return dict(g_in=g_in[None, :], w_all=w_all, groups=groups, aux=aux, conv_w=conv_w,
                avec=pad_a(-jnp.exp(a_log)), dtvec=pad_a(dt_bias), g_o=g_o[None, :],
                w_out=w_out.astype(BF16))


def _layer(x, mem_k, mem_v, conv_buf, s0, past, gdn_chunk, gdn_rows, wp):
    bsz, t_len, d = x.shape
    m = bsz * t_len
    x2 = x.reshape(m, d)
    tm = min(PROJ_ROWS, m)
    heads_last = lambda a_t: a_t.reshape(bsz, B_HEADS, B_HD, t_len).transpose(0, 3, 1, 2)
    cbuf = jnp.concatenate([jnp.zeros((bsz, CARRY_ROWS - (CONV_W - 1), 3 * A_WIDTH), F32), conv_buf], axis=1)
    conv = (cbuf, wp["conv_w"])
    gdn_args = dict(bsz=bsz, t_len=t_len, chunk=gdn_chunk, rows=gdn_rows)
    if past is None:
        (za, qb, kb, vb, zb, qi, qm, zm, misc, qa, ka, va, kt, vt, conv_new) = _proj(
            x2, wp["g_in"], wp["w_all"], wp["groups"], wp["aux"], tm, t_len, conv=conv)
        k_new, v_new = heads_last(kt), heads_last(vt)
        oa, s_new = _gdn(qa, ka, va, za, misc, s0, wp["avec"], wp["dtvec"], wp["g_o"], **gdn_args)
    else:
        groups = [(wd, normed, extra if extra == "bf16" else "") for wd, normed, extra in wp["groups"]]
        (za, qb, kb, vb, zb, qi, qm, zm, misc, qa, ka, va) = _proj(x2, wp["g_in"], wp["w_all"], groups,
                                                                   wp["aux"], tm, m)
        k_new, v_new = (a.reshape(bsz, t_len, B_HEADS, B_HD) for a in (kb, vb))
        oa, s_new, conv_new = _gdn(qa, ka, va, za, misc, s0, wp["avec"], wp["dtvec"], wp["g_o"], conv=conv,
                                   **gdn_args)
    if past is None:
        n_sel = min(TOPK_MAX, t_len // 4)
        ob = _dsa_prompt(qb, qi, misc, zb, kb, vt, bsz=bsz, t_len=t_len, n_sel=n_sel, qblk=DSA_QBLK)
    else:
        ck, cv, cki = past
        p_len = ck.shape[1]
        assert (p_len + t_len - 1) // CHUNK <= p_len // CHUNK and p_len % KEY_BLOCK == 0
        n_sel = min(TOPK_MAX, (p_len + t_len) // 4)
        ob = _dsa_decode(qb, qi, misc, zb, kb, vb, ck, cv, cki, bsz=bsz, t_len=t_len, p_len=p_len,
                         n_sel=n_sel)
    y = _tail(x2, oa, ob, qm, zm, mem_k, mem_v, wp["w_out"], bsz=bsz, t_len=t_len, tq=min(TAIL_ROWS, t_len))
    return (y.reshape(bsz, t_len, d), conv_new, s_new, k_new, v_new,
            misc[:, :IDX_HD].reshape(bsz, t_len, IDX_HD))


def _memory_kv(mem, g_mem, w_mem_kv, g_km):
    bsz, n_mem, d = mem.shape
    groups = [(M_WIDTH, True, "t"), (M_WIDTH, False, "t")]
    aux = [_group_mean_matrix(M_WIDTH, M_HD, M_WIDTH), jnp.tile(g_km, M_HEADS)[None, :],
           jnp.ones((1, M_WIDTH), F32)]
    _, _, mk_t, mv_t = _proj(mem.reshape(bsz * n_mem, d), g_mem[None, :], w_mem_kv.astype(BF16),
                             groups, aux, n_mem, n_mem)
    heads_last = lambda a_t: a_t.reshape(bsz, M_HEADS, M_HD, n_mem).transpose(0, 3, 1, 2)
    return mk_t, mv_t, heads_last(mk_t), heads_last(mv_t)


def kernel(x_prompt, x_sample, state_conv_A, state_ssm_A, cache_k_B, cache_v_B, cache_kidx_B, cache_mem_k,
           cache_mem_v, mem_prompt, g_in, w_in, conv_w_A, a_log_A, dt_bias_A, g_o_A, g_q_B, g_k_B, g_kidx_B,
           g_mem, w_mem_kv, g_q_M, g_k_M, w_out):
    depth = w_in.shape[0]
    assert depth == 1
    l = 0
    bp, t_p, _ = x_prompt.shape
    bs, t_s, _ = x_sample.shape
    wp = _prep_weights(g_in[l], w_in[l], conv_w_A[l], a_log_A[l], dt_bias_A[l], g_o_A[l], g_q_B[l], g_k_B[l],
                       g_kidx_B[l], g_q_M[l], w_out[l])
    mk, mv, mk_heads, mv_heads = _memory_kv(mem_prompt, g_mem[l], w_mem_kv[l], g_k_M[l])
    feature_major = lambda c: c.transpose(0, 2, 3, 1).reshape(c.shape[0], M_WIDTH, N_MEM)
    zero_conv = jnp.zeros((bp, CONV_W - 1, 3 * A_WIDTH), F32)
    zero_ssm = jnp.zeros((bp, A_HEADS, A_DK, A_DV), F32)
    yp, c1, s1, k1, v1, ki1 = _layer(x_prompt, mk, mv, zero_conv, zero_ssm, None, CHUNK, GDN_CHUNKS_PER_STEP * CHUNK, wp)
    ys, c2, s2, k2, v2, ki2 = _layer(
        x_sample, feature_major(cache_mem_k[l]), feature_major(cache_mem_v[l]),
        state_conv_A[l], state_ssm_A[l], (cache_k_B[l], cache_v_B[l], cache_kidx_B[l]), t_s, t_s, wp)
    st = lambda a: a[None]
    return (yp, ys, st(c1), st(s1), st(k1), st(v1), st(ki1),
            st(mk_heads), st(mv_heads),
            st(c2), st(s2), st(k2), st(v2), st(ki2))
```

```python
import functools
import math

import numpy as np
import jax
import jax.numpy as jnp
from jax import lax
from jax.experimental import pallas as pl
from jax.experimental.pallas import tpu as pltpu

F32 = jnp.float32
BF16 = jnp.bfloat16
HIGHEST = lax.Precision.HIGHEST

D_MODEL = 1024
CHUNK = 64
A_HEADS = 4
A_DK = 128
A_DV = 128
A_WIDTH = A_HEADS * A_DV
CONV_W = 4
B_HEADS = 4
B_HD = 64
B_WIDTH = B_HEADS * B_HD
IDX_HEADS = 8
IDX_HD = 32
IDX_SCALE = (IDX_HEADS ** -0.5) * (IDX_HD ** -0.5)
TOPK_MAX = 256
Q_BLOCK = 128
N_MEM = 256
M_HEADS = 4
M_HD = 64
M_WIDTH = M_HEADS * M_HD
EPS = 1e-6
IN_SPLITS = (3 * A_WIDTH, A_WIDTH, A_HEADS, A_HEADS,
             B_WIDTH, B_WIDTH, B_WIDTH, B_WIDTH, IDX_HEADS * IDX_HD, IDX_HD, IDX_HEADS,
             M_WIDTH, M_WIDTH)

LANES = 128
KEY_BLOCK = 256
DSA_QBLK = 512
GDN_CHUNKS_PER_STEP = 8
TAIL_ROWS = 1024
PROJ_ROWS = 512
DEC_KEYS_PER_STEP = 4096
SCORE_STRIP = 128
MISC_W = LANES
MISC_WI = IDX_HD
MISC_B = IDX_HD + IDX_HEADS
MISC_A = MISC_B + A_HEADS
VMEM_LIMIT = 48 * 1024 * 1024
NEG_BIG = -1e30
BISECT_PROBES = 2
COUNT_RADIX = 1024
BISECT_BLIND_ITERS = 10
BISECT_MAX_ITERS = 26


def _cparams(sem):
    return pltpu.CompilerParams(dimension_semantics=sem, vmem_limit_bytes=VMEM_LIMIT)


def _dot(a, b):
    return jnp.dot(a, b, preferred_element_type=F32)


def _dot_nt(a, b, precision=None):
    return lax.dot_general(a, b, (((1,), (1,)), ((), ())), precision=precision,
                           preferred_element_type=F32)


def _split2(x):
    hi = x.astype(BF16)
    return hi, (x - hi.astype(F32)).astype(BF16)


def _mm_split(a, b):
    ah, al = a
    bh, _ = b
    n = ah.shape[0]
    t = _dot(jnp.concatenate([ah, al], axis=0), bh)
    return t[:n] + t[n:]


def _silu(x):
    return x * jax.nn.sigmoid(x)


def _lane_mask(width, lo, hi):
    lane = lax.broadcasted_iota(jnp.int32, (1, width), 1)
    return (lane >= lo) & (lane < hi)


CARRY_ROWS = 8


def _conv_silu_norm(xp_ref, raw, cw, rows, part):
    first = CARRY_ROWS - (CONV_W - 1)
    xp_ref[CARRY_ROWS:CARRY_ROWS + rows, :] = raw
    y = xp_ref[first:first + rows, :] * cw[0:1, :]
    for j in range(1, CONV_W):
        y = y + xp_ref[first + j:first + j + rows, :] * cw[j:j + 1, :]
    y = _silu(y)
    conv_new = xp_ref[rows + first:rows + CARRY_ROWS, :]
    xp_ref[0:CARRY_ROWS, :] = xp_ref[rows:rows + CARRY_ROWS, :]
    if part == 2:
        return y, conv_new
    scale = A_DK ** -0.5 if part == 0 else 1.0
    heads = []
    for h in range(A_HEADS):
        yh = y[:, h * A_DK:(h + 1) * A_DK]
        heads.append(yh * (lax.rsqrt(jnp.sum(yh * yh, axis=-1, keepdims=True) + EPS) * scale))
    return jnp.concatenate(heads, axis=1), conv_new


def _conv_part(extra):
    return int(extra[len("conv"):]) if extra.startswith("conv") else None


def _proj_body(*refs, groups, tiles):
    x_ref, g_ref, w_ref = refs[0], refs[1], refs[2]
    n_aux = 3 * sum(1 for _, normed, _ in groups if normed)
    aux = refs[3:3 + n_aux]
    pos = 3 + n_aux
    n_conv = sum(1 for _, _, extra in groups if _conv_part(extra) is not None)
    if n_conv:
        cbuf_ref, cw_ref = refs[pos], refs[pos + 1]
        pos += 2
    outs = refs[pos:pos + len(groups)]
    pos += len(groups)
    n_t = sum(1 for _, _, extra in groups if extra == "t")
    outs_t = list(refs[pos:pos + n_t])
    pos += n_t
    if n_conv:
        convnew_ref = refs[pos]
        xp_refs = refs[pos + 1:pos + 1 + n_conv]

        @pl.when(pl.program_id(0) % tiles == 0)
        def _():
            for part, xp_ref in enumerate(xp_refs):
                xp_ref[0:CARRY_ROWS, :] = cbuf_ref[:, part * A_WIDTH:(part + 1) * A_WIDTH]
    x = x_ref[...]
    r = lax.rsqrt(jnp.mean(x * x, axis=-1, keepdims=True) + EPS)
    hb = (x * g_ref[...]).astype(BF16)
    off = 0
    ai = 0
    for (width, normed, extra), o_ref in zip(groups, outs):
        y = _dot(hb, w_ref[:, off:off + width]) * r
        part = _conv_part(extra)
        if part is not None:
            cols = slice(part * A_WIDTH, (part + 1) * A_WIDTH)
            y, conv_new = _conv_silu_norm(xp_refs[part], y, cw_ref[:, cols], y.shape[0], part)
            convnew_ref[:, cols] = conv_new
        if normed:
            bd_ref, gain_ref, nm_ref = aux[ai], aux[ai + 1], aux[ai + 2]
            ai += 3
            sq = y * y
            hi = sq.astype(BF16)
            lo = (sq - hi.astype(F32)).astype(BF16)
            ms = _dot(hi, bd_ref[...]) + _dot(lo, bd_ref[...])
            scale = jnp.where(nm_ref[...] > 0.0, lax.rsqrt(ms + EPS), 1.0)
            y = y * scale * gain_ref[...]
        o_ref[...] = y.astype(o_ref.dtype)
        if extra == "t":
            outs_t.pop(0)[...] = y.T
        off += width


def _group_mean_matrix(width, group, n_lanes):
    i = np.arange(width)
    m = ((i[:, None] // group) == (i[None, :] // group)) & (i[:, None] < n_lanes) & (i[None, :] < n_lanes)
    return jnp.asarray(m.astype(np.float32) / group, dtype=BF16)


def _proj(x, g, w, groups, aux, tm, t_len, conv=None):
    m, d = x.shape
    nw = w.shape[1]
    tiles = t_len // tm
    bsz = m // t_len
    in_specs = [pl.BlockSpec((tm, d), lambda i: (i, 0)),
                pl.BlockSpec((1, d), lambda i: (0, 0)),
                pl.BlockSpec((d, nw), lambda i: (0, 0))]
    for a in aux:
        in_specs.append(pl.BlockSpec(a.shape, lambda i: (0, 0)))
    out_shape = [jax.ShapeDtypeStruct((m, wd), BF16 if extra == "bf16" else F32) for wd, _, extra in groups]
    out_specs = [pl.BlockSpec((tm, wd), lambda i: (i, 0)) for wd, _, _ in groups]
    for wd, _, extra in groups:
        if extra == "t":
            out_shape.append(jax.ShapeDtypeStruct((bsz, wd, t_len), F32))
            out_specs.append(pl.BlockSpec((None, wd, tm), lambda i: (i // tiles, 0, i % tiles)))
    args = [x, g, w, *aux]
    scratch = []
    if conv is not None:
        cbuf, cw = conv
        wd = cw.shape[1]
        in_specs += [pl.BlockSpec((None, CARRY_ROWS, wd), lambda i: (i // tiles, 0, 0)),
                     pl.BlockSpec(cw.shape, lambda i: (0, 0))]
        args += [cbuf, cw]
        out_shape.append(jax.ShapeDtypeStruct((bsz, CONV_W - 1, wd), F32))
        out_specs.append(pl.BlockSpec((None, CONV_W - 1, wd), lambda i: (i // tiles, 0, 0)))
        scratch += [pltpu.VMEM((tm + CARRY_ROWS, A_WIDTH), F32) for _ in range(wd // A_WIDTH)]
    return pl.pallas_call(
        functools.partial(_proj_body, groups=tuple(groups), tiles=tiles),
        grid=(m // tm,),
        in_specs=in_specs, out_specs=out_specs, out_shape=out_shape, scratch_shapes=scratch,
        compiler_params=_cparams(("arbitrary",) if conv is not None else ("parallel",)),
        name="proj",
    )(*args)


def _gdn_body(*refs, chunk, rows, fused_conv):
    if fused_conv:
        (q_ref, k_ref, v_ref, za_ref, misc_ref, s0_ref, avec_ref, dtvec_ref, go_ref, cbuf_ref, cw_ref,
         o_ref, sfin_ref, convnew_ref, g_sc, b_sc, s_sc, *conv_sc) = refs
    else:
        (q_ref, k_ref, v_ref, za_ref, misc_ref, s0_ref, avec_ref, dtvec_ref, go_ref,
         o_ref, sfin_ref, g_sc, b_sc, s_sc) = refs
    t = pl.program_id(1)
    c = chunk

    @pl.when(t == 0)
    def _():
        s_sc[...] = s0_ref[0]

    if fused_conv:
        xp_refs, act_refs = conv_sc[:3], conv_sc[3:]

        @pl.when(t == 0)
        def _():
            for part, xp_ref in enumerate(xp_refs):
                xp_ref[0:CARRY_ROWS, :] = cbuf_ref[:, part * A_WIDTH:(part + 1) * A_WIDTH]
        for part, raw_ref in enumerate((q_ref, k_ref, v_ref)):
            cols = slice(part * A_WIDTH, (part + 1) * A_WIDTH)
            act, conv_new = _conv_silu_norm(xp_refs[part], raw_ref[...], cw_ref[:, cols], rows, part)
            convnew_ref[:, cols] = conv_new
            act_refs[part][...] = act
        q_sc, k_sc, v_sc = act_refs
    else:
        q_sc, k_sc, v_sc = q_ref, k_ref, v_ref
    misc = misc_ref[...]
    b_sc[...] = jax.nn.sigmoid(misc)
    g_sc[...] = avec_ref[...] * jax.nn.softplus(misc + dtvec_ref[...])

    ri = lax.broadcasted_iota(jnp.int32, (c, c), 0)
    ci = lax.broadcasted_iota(jnp.int32, (c, c), 1)
    tri = ri >= ci
    strict = ri > ci
    eye_f = (ri == ci).astype(F32)
    tri3 = (lax.broadcasted_iota(jnp.int32, (c, 3 * c), 0)
            >= lax.broadcasted_iota(jnp.int32, (c, 3 * c), 1) % c).astype(BF16)
    go = go_ref[...]
    n_chunks = rows // c

    problems = [(ic, h) for ic in range(n_chunks) for h in range(A_HEADS)]
    gc_all, gc_t = [], []
    for ic in range(n_chunks):
        g = g_sc[ic * c:(ic + 1) * c, :]
        g1 = g.astype(BF16)
        r1 = g - g1.astype(F32)
        g2 = r1.astype(BF16)
        g3 = (r1 - g2.astype(F32)).astype(BF16)
        gc_all.append(_dot(tri3, jnp.concatenate([g1, g2, g3], axis=0)))
    for ic in range(n_chunks):
        gc_t.append(gc_all[ic].T)

    def load(ref, ic, h):
        return ref[ic * c:(ic + 1) * c, h * A_DK:(h + 1) * A_DK]

    col = lambda a, lane: a[:, lane:lane + 1]
    k_bf = [load(k_sc, ic, h).astype(BF16) for ic, h in problems]
    kb = [load(k_sc, ic, h) * col(b_sc[ic * c:(ic + 1) * c, :], MISC_B + h) for ic, h in problems]
    kk = [_dot_nt(kb[i].astype(BF16), k_bf[i]) for i in range(len(problems))]
    qk_raw = [_dot_nt(load(q_sc, ic, h).astype(BF16), k_bf[i]) for i, (ic, h) in enumerate(problems)]
    decay = []
    for ic, h in problems:
        diff = col(gc_all[ic], MISC_A + h) - gc_t[ic][MISC_A + h:MISC_A + h + 1, :]
        decay.append(jnp.where(tri, jnp.exp(jnp.where(tri, diff, 0.0)), 0.0))
    lmat = [jnp.where(strict, kk[i] * decay[i], 0.0) for i in range(len(problems))]
    qk = [(qk_raw[i] * decay[i]).astype(BF16) for i in range(len(problems))]
    x_inv = [eye_f - m for m in lmat]
    p_split = [_split2(m) for m in lmat]
    for _ in range(int(math.log2(c)) - 1):
        p_split = [_split2(_mm_split(ps, ps)) for ps in p_split]
        x_inv = [x + _mm_split(_split2(x), ps) for x, ps in zip(x_inv, p_split)]
    sol = []
    for i, (ic, h) in enumerate(problems):
        egc = jnp.exp(col(gc_all[ic], MISC_A + h))
        beta = col(b_sc[ic * c:(ic + 1) * c, :], MISC_B + h)
        rhs = jnp.concatenate([load(v_sc, ic, h) * beta, kb[i] * egc], axis=-1).astype(BF16)
        sol.append(_mm_split(_split2(x_inv[i]), (rhs, None)))
    pre = []
    for i, (ic, h) in enumerate(problems):
        gc = col(gc_all[ic], MISC_A + h)
        g_last = gc[c - 1:c, :]
        k_dec_t = (load(k_sc, ic, h) * jnp.exp(g_last - gc)).T.astype(BF16)
        wq = jnp.concatenate([sol[i][:, A_DV:], load(q_sc, ic, h) * jnp.exp(gc)], axis=0).astype(BF16)
        pre.append((sol[i][:, :A_DV], wq, k_dec_t, jnp.exp(g_last)))

    heads = range(A_HEADS)
    s_cur = [s_sc[h] for h in heads]
    for ic in range(n_chunks):
        pr = [pre[ic * A_HEADS + h] for h in heads]
        ws = [_dot(pr[h][1], s_cur[h].astype(BF16)) for h in heads]
        ub = [(pr[h][0] - ws[h][:c]).astype(BF16) for h in heads]
        o = [ws[h][c:] + _dot(qk[ic * A_HEADS + h], ub[h]) for h in heads]
        s_cur = [s_cur[h] * pr[h][3] + _dot(pr[h][2], ub[h]) for h in heads]
        for h in heads:
            on = o[h] * lax.rsqrt(jnp.mean(o[h] * o[h], axis=-1, keepdims=True) + EPS) * go
            o_ref[ic * c:(ic + 1) * c, h * A_DV:(h + 1) * A_DV] = (on * _silu(load(za_ref, ic, h))).astype(BF16)
    for h in heads:
        s_sc[h] = s_cur[h]
    sfin_ref[0] = s_sc[...]


def _gdn(q, k, v, za, misc, s0, avec, dtvec, g_o, *, bsz, t_len, chunk, rows, conv=None):
    nt = t_len // rows
    w3 = 3 * A_WIDTH
    row_map = lambda b, t: (b * nt + t, 0)
    const2 = lambda b, t: (0, 0)
    state_spec = pl.BlockSpec((1, A_HEADS, A_DK, A_DV), lambda b, t: (b, 0, 0, 0))
    in_specs = [pl.BlockSpec((rows, A_WIDTH), row_map)] * 4 + [
                pl.BlockSpec((rows, MISC_W), row_map),
                state_spec,
                pl.BlockSpec((1, MISC_W), const2),
                pl.BlockSpec((1, MISC_W), const2),
                pl.BlockSpec((1, A_DV), const2)]
    args = [q, k, v, za, misc, s0, avec, dtvec, g_o]
    out_specs = [pl.BlockSpec((rows, A_WIDTH), row_map), state_spec]
    out_shape = [jax.ShapeDtypeStruct((bsz * t_len, A_WIDTH), BF16),
                 jax.ShapeDtypeStruct((bsz, A_HEADS, A_DK, A_DV), F32)]
    scratch = [pltpu.VMEM((rows, MISC_W), F32),
               pltpu.VMEM((rows, MISC_W), F32),
               pltpu.VMEM((A_HEADS, A_DK, A_DV), F32)]
    if conv is not None:
        cbuf, cw = conv
        in_specs += [pl.BlockSpec((None, CARRY_ROWS, w3), lambda b, t: (b, 0, 0)),
                     pl.BlockSpec((CONV_W, w3), const2)]
        args += [cbuf, cw]
        out_specs.append(pl.BlockSpec((None, CONV_W - 1, w3), lambda b, t: (b, 0, 0)))
        out_shape.append(jax.ShapeDtypeStruct((bsz, CONV_W - 1, w3), F32))
        scratch += [pltpu.VMEM((rows + CARRY_ROWS, A_WIDTH), F32)] * 3 + [pltpu.VMEM((rows, A_WIDTH), F32)] * 3
    return pl.pallas_call(
        functools.partial(_gdn_body, chunk=chunk, rows=rows, fused_conv=conv is not None),
        grid=(bsz, nt),
        in_specs=in_specs, out_specs=out_specs, out_shape=out_shape, scratch_shapes=scratch,
        compiler_params=_cparams(("parallel", "arbitrary")),
        name="gdn",
    )(*args)


def _topk_threshold(sc_ref, nkb, shape, key_axis, n_sel, n_adm, even_blocks=False):
    kf = float(n_sel)
    kblk = shape[key_axis]

    if key_axis == 0:
        strip = min(kblk, 32)
        part_shape = (strip, shape[1])
        strip_of = lambda kb, i: sc_ref[kb, i * strip:(i + 1) * strip, :]

        def put_strip(kb, i, v):
            sc_ref[kb, i * strip:(i + 1) * strip, :] = v
    else:
        strip = kblk
        part_shape = shape
        strip_of = lambda kb, i: sc_ref[kb]

        def put_strip(kb, i, v):
            sc_ref[kb] = v
    kpos = lax.broadcasted_iota(jnp.int32, part_shape, key_axis)

    def reduce_blocks(specs):
        def body(kb, accs):
            accs = list(accs)
            for i in range(kblk // strip):
                s = strip_of(kb, i)
                k0 = kb * kblk + i * strip
                accs = [cmb(acc, fn(s, k0)) for acc, (fn, _, _, _, cmb) in zip(accs, specs)]
            return tuple(accs)
        init = tuple(jnp.full(part_shape, i, dt) for _, i, dt, _, _ in specs)
        accs = lax.fori_loop(0, nkb, body, init)
        return [acc if op is None else op(acc, axis=key_axis, keepdims=True)
                for acc, (_, _, _, op, _) in zip(accs, specs)]

    one_if = lambda c: jnp.where(c, 1.0, 0.0)
    vmax, vmin, cpos, czero = reduce_blocks([
        (lambda s, k0: s, -jnp.inf, F32, jnp.max, jnp.maximum),
        (lambda s, k0: jnp.where(s == -jnp.inf, jnp.inf, s), jnp.inf, F32, jnp.min, jnp.minimum),
        (lambda s, k0: one_if(s > 0.0), 0.0, F32, jnp.sum, jnp.add),
        (lambda s, k0: one_if(s == 0.0), 0.0, F32, jnp.sum, jnp.add)])

    long_row = n_adm > kf
    pos_row = long_row & (cpos >= kf)
    zero_row = long_row & (cpos < kf) & (cpos + czero >= kf)
    neg_row = long_row & (cpos + czero < kf)
    lo0 = jnp.where(pos_row | zero_row, 0.0, vmin - jnp.maximum(1.0, jnp.abs(vmin)))
    cnt_lo0 = jnp.where(pos_row, cpos, jnp.where(zero_row, kf, n_adm))
    hi0 = jnp.where(neg_row, 0.0, vmax)
    cnt_hi0 = jnp.where(neg_row, cpos, 0.0)
    need_zero = jnp.where(zero_row, kf - cpos, 0.0)

    def narrow(lo, hi, cnt_lo, cnt_hi):
        fracs = [(t + 1.0) / (BISECT_PROBES + 1.0) for t in range(BISECT_PROBES)]
        mids = [lo * (1.0 - f) + hi * f for f in fracs]
        for t in range(1, BISECT_PROBES):
            mids[t] = jnp.maximum(mids[t], mids[t - 1])
        codes = [sum(COUNT_RADIX ** u for u in range(t + 1)) for t in range(BISECT_PROBES)]

        def encode(s, k0):
            e = jnp.zeros(s.shape, jnp.int32)
            for mid, code in zip(mids, codes):
                e = jnp.where(s > mid, code, e)
            return e
        (packed,) = reduce_blocks([(encode, 0, jnp.int32, None, jnp.add)])
        cnts = []
        for t in range(BISECT_PROBES):
            digit = packed & (COUNT_RADIX - 1) if t < BISECT_PROBES - 1 else packed
            packed = packed >> COUNT_RADIX.bit_length() - 1
            cnts.append(jnp.sum(digit.astype(F32), axis=key_axis, keepdims=True))
        lo_n, cnt_lo_n, hi_n, cnt_hi_n = lo, cnt_lo, hi, cnt_hi
        for mid, cm in zip(mids, cnts):
            up = cm >= kf
            lo_n = jnp.where(up, mid, lo_n)
            cnt_lo_n = jnp.where(up, cm, cnt_lo_n)
        for mid, cm in zip(mids[::-1], cnts[::-1]):
            dn = cm < kf
            hi_n = jnp.where(dn, mid, hi_n)
            cnt_hi_n = jnp.where(dn, cm, cnt_hi_n)
        return lo_n, hi_n, cnt_lo_n, cnt_hi_n

    def unresolved(cnt_lo):
        return (jnp.max(cnt_lo) > kf).astype(jnp.int32)

    blind = jnp.where(unresolved(cnt_lo0) > 0, BISECT_BLIND_ITERS, 0)
    state = lax.fori_loop(0, blind, lambda i, st: narrow(*st), (lo0, hi0, cnt_lo0, cnt_hi0))

    def cond(st):
        return (st[0] < BISECT_MAX_ITERS) & (st[1] > 0)

    def body(st):
        new = narrow(*st[2:])
        return (st[0] + 1, unresolved(new[2])) + new

    _, _, lo, hi, cnt_lo, cnt_hi = lax.while_loop(
        cond, body, (blind, unresolved(state[2])) + state)

    fix = cnt_lo > kf
    need0 = jnp.where(fix, kf - cnt_hi, 0.0)
    big_idx = 2 ** 30

    def fix_cond(need):
        return jnp.max(need) > 0.0

    def fix_body(need):
        def in_cluster(s):
            return (s > lo) & (s <= hi)
        (mval,) = reduce_blocks([(lambda s, k0: jnp.where(in_cluster(s), s, -jnp.inf), -jnp.inf, F32,
                                  jnp.max, jnp.maximum)])
        (idx,) = reduce_blocks(
            [(lambda s, k0: jnp.where(in_cluster(s) & (s == mval), kpos + k0, big_idx),
              big_idx, jnp.int32, jnp.min, jnp.minimum)])
        active = need > 0.0

        def promote(kb, carry):
            for i in range(kblk // strip):
                s = strip_of(kb, i)
                put_strip(kb, i, jnp.where(active & ((kpos + kb * kblk + i * strip) == idx), jnp.inf, s))
            return carry
        lax.fori_loop(0, nkb, promote, 0)
        return jnp.where(active, need - 1.0, need)

    lax.while_loop(fix_cond, fix_body, need0)

    @pl.when(jnp.max(need_zero) > 0.0)
    def _():
        r = lax.broadcasted_iota(jnp.int32, (kblk, kblk), 0)
        c = lax.broadcasted_iota(jnp.int32, (kblk, kblk), 1)
        tri = ((r >= c) if key_axis == 0 else (r <= c)).astype(BF16)

        def tie_blk(kb, seen):
            s = sc_ref[kb]
            z = (s == 0.0) & zero_row
            zb = one_if(z).astype(BF16)
            if key_axis == 0:
                rank = _dot(tri, zb) + seen
                last = rank[kblk - 1:kblk, :]
            else:
                rank = _dot(zb, tri) + seen
                last = rank[:, kblk - 1:kblk]
            sc_ref[kb] = jnp.where(z & (rank <= need_zero), jnp.inf, s)
            return last
        if even_blocks:
            lax.fori_loop(0, nkb // 2, lambda i, seen: tie_blk(2 * i + 1, tie_blk(2 * i, seen)),
                          jnp.zeros_like(need_zero))
        else:
            lax.fori_loop(0, nkb, tie_blk, jnp.zeros_like(need_zero))

    return jnp.where(fix, hi, lo)


def _dsa_body(qb_ref, qi_ref, miscq_ref, zb_ref, k_ref, vt_ref, misck_ref, o_ref, sc_ref, s_ref, *, n_sel, qblk):
    j = pl.program_id(1)
    nkb = ((j + 1) * qblk + KEY_BLOCK - 1) // KEY_BLOCK
    qi = qi_ref[...]
    q_h = [qi[:, h * IDX_HD:(h + 1) * IDX_HD].astype(BF16) for h in range(IDX_HEADS)]
    w_t = miscq_ref[...].T[MISC_WI:MISC_WI + IDX_HEADS, :]
    qchunk = (j * qblk + lax.broadcasted_iota(jnp.int32, (1, qblk), 1)) // CHUNK
    krow = lax.broadcasted_iota(jnp.int32, (KEY_BLOCK, 1), 0)

    def fold(acc, x, combine):
        for i in range(x.shape[0] // 8):
            acc = combine(acc, x[8 * i:8 * (i + 1)])
        return acc

    def score_blk(kb, carry):
        k0 = pl.multiple_of(kb * KEY_BLOCK, KEY_BLOCK)
        for half in range(KEY_BLOCK // SCORE_STRIP):
            r0 = half * SCORE_STRIP
            ki = misck_ref[pl.ds(k0 + r0, SCORE_STRIP), 0:IDX_HD].astype(BF16)
            acc = jnp.zeros((SCORE_STRIP, qblk), F32)
            for h in range(IDX_HEADS):
                acc = acc + w_t[h:h + 1, :] * jnp.maximum(_dot_nt(ki, q_h[h]), 0.0)
            adm = ((k0 + r0 + krow[:SCORE_STRIP]) // CHUNK) <= qchunk
            sc_ref[kb, r0:r0 + SCORE_STRIP, :] = jnp.where(adm, acc, -jnp.inf)
        return carry

    npairs = (nkb + 1) // 2

    def pair_loop(body, init):
        return lax.fori_loop(0, npairs, lambda i, c: body(2 * i + 1, body(2 * i, c)), init)

    pair_loop(score_blk, 0)

    n_adm = ((qchunk + 1) * CHUNK).astype(F32)
    thr = _topk_threshold(sc_ref, nkb, (KEY_BLOCK, qblk), 0, n_sel, n_adm,
                          even_blocks=qblk % (2 * KEY_BLOCK) == 0)

    qb = qb_ref[...] * (B_HD ** -0.5 * math.log2(math.e))
    qs = jnp.concatenate([jnp.where(_lane_mask(B_WIDTH, h * B_HD, (h + 1) * B_HD), qb, 0.0)
                          for h in range(B_HEADS)], axis=0).astype(BF16)
    wide = B_HEADS * qblk

    def logits_blk(kb, m8):
        k0 = pl.multiple_of(kb * KEY_BLOCK, KEY_BLOCK)
        s = _dot_nt(k_ref[pl.ds(k0, KEY_BLOCK), :].astype(BF16), qs)
        sel = sc_ref[kb] > thr
        s = jnp.where(jnp.concatenate([sel] * B_HEADS, axis=1), s, NEG_BIG)
        s_ref[kb] = s
        return fold(m8, s, jnp.maximum)

    m8 = pair_loop(logits_blk, jnp.full((8, wide), NEG_BIG, F32))
    m = jnp.max(m8, axis=0, keepdims=True)

    def pv_blk(kb, carry):
        l8, accs = carry
        p = jnp.exp2(s_ref[kb] - m)
        pb = p.astype(BF16)
        k0 = pl.multiple_of(kb * KEY_BLOCK, KEY_BLOCK)
        vt = vt_ref[:, pl.ds(k0, KEY_BLOCK)].astype(BF16)
        accs = tuple(accs[h] + _dot(vt[h * B_HD:(h + 1) * B_HD, :], pb[:, h * qblk:(h + 1) * qblk])
                     for h in range(B_HEADS))
        return fold(l8, p, jnp.add), accs

    l8, accs = pair_loop(
        pv_blk, (jnp.zeros((8, wide), F32), tuple(jnp.zeros((B_HD, qblk), F32) for _ in range(B_HEADS))))
    l = jnp.sum(l8, axis=0, keepdims=True)
    o_t = jnp.concatenate([accs[h] / l[:, h * qblk:(h + 1) * qblk] for h in range(B_HEADS)], axis=0)
    o_ref[...] = (o_t.T * _silu(zb_ref[...])).astype(BF16)


def _dsa_prompt(qb, qi, misc, zb, kb, vt, *, bsz, t_len, n_sel, qblk):
    nq = t_len // qblk
    nkb = t_len // KEY_BLOCK
    assert nkb * (KEY_BLOCK // 8) < COUNT_RADIX and nkb % 2 == 0
    qmap = lambda b, j: (b * nq + j, 0)
    kmap = lambda b, j: (b, 0)
    return pl.pallas_call(
        functools.partial(_dsa_body, n_sel=n_sel, qblk=qblk),
        grid=(bsz, nq),
        in_specs=[pl.BlockSpec((qblk, B_WIDTH), qmap),
                  pl.BlockSpec((qblk, IDX_HEADS * IDX_HD), qmap),
                  pl.BlockSpec((qblk, MISC_W), qmap),
                  pl.BlockSpec((qblk, B_WIDTH), qmap),
                  pl.BlockSpec((t_len, B_WIDTH), kmap),
                  pl.BlockSpec((None, B_WIDTH, t_len), lambda b, j: (b, 0, 0)),
                  pl.BlockSpec((t_len, MISC_W), kmap)],
        out_specs=pl.BlockSpec((qblk, B_WIDTH), qmap),
        out_shape=jax.ShapeDtypeStruct((bsz * t_len, B_WIDTH), BF16),
        scratch_shapes=[pltpu.VMEM((nkb, KEY_BLOCK, qblk), F32),
                        pltpu.VMEM((nkb, KEY_BLOCK, B_HEADS * qblk), F32)],
        compiler_params=_cparams(("parallel", "arbitrary")),
        name="dsa_prompt",
    )(qb, qi, misc, zb, kb, vt, misc)


def _dsa_dec_body(qb_ref, qi_ref, misc_ref, zb_ref, kn_ref, vn_ref, ck_ref, cv_ref, cki_ref, o_ref,
                  sc_ref, thr_ref, m_ref, l_ref, acc_ref, *, p_len, tq, n_sel, kstep):
    step = pl.program_id(1)
    nsteps = p_len // kstep
    sub = kstep // KEY_BLOCK
    npast = p_len // KEY_BLOCK
    nkb = npast + 1
    pad_rows = KEY_BLOCK - tq
    rows = B_HEADS * tq

    @pl.when(step == 0)
    def _():
        qi = qi_ref[...]
        misc = misc_ref[...]
        q2 = jnp.concatenate([qi[:, h * IDX_HD:(h + 1) * IDX_HD] for h in range(IDX_HEADS)],
                             axis=0).astype(BF16)
        w_h = [misc[:, MISC_WI + h:MISC_WI + h + 1] for h in range(IDX_HEADS)]

        def head_sum(lg):
            acc = jnp.zeros((tq, lg.shape[1]), F32)
            for h in range(IDX_HEADS):
                acc = acc + w_h[h] * jnp.maximum(lg[h * tq:(h + 1) * tq, :], 0.0)
            return acc

        for i in range(nsteps):
            sc = head_sum(_dot(q2, cki_ref[:, i * kstep:(i + 1) * kstep].astype(BF16)))
            for u in range(sub):
                sc_ref[i * sub + u] = sc[:, u * KEY_BLOCK:(u + 1) * KEY_BLOCK]
        lane = lax.broadcasted_iota(jnp.int32, (1, KEY_BLOCK), 1)
        ki_new = jnp.concatenate([misc[:, 0:IDX_HD], jnp.zeros((pad_rows, IDX_HD), F32)], axis=0)
        sc_ref[npast] = jnp.where(lane < tq, head_sum(_dot_nt(q2, ki_new.astype(BF16))), -jnp.inf)
        n_adm = jnp.full((tq, 1), float(p_len + tq), F32)
        thr_ref[...] = _topk_threshold(sc_ref, nkb, (tq, KEY_BLOCK), 1, n_sel, n_adm)
        m_ref[...] = jnp.full((rows, 1), NEG_BIG, F32)
        l_ref[...] = jnp.zeros((rows, 1), F32)
        acc_ref[...] = jnp.zeros((rows, B_HD), F32)

    qb = qb_ref[...] * (B_HD ** -0.5 * math.log2(math.e))
    q_h = [qb[:, h * B_HD:(h + 1) * B_HD].astype(BF16) for h in range(B_HEADS)]
    thr = thr_ref[...]

    def fold_in(logits_of, pv_of, sel):
        s = jnp.concatenate([jnp.where(sel, logits_of(h), NEG_BIG) for h in range(B_HEADS)], axis=0)
        m_old = m_ref[...]
        m_new = jnp.maximum(m_old, jnp.max(s, axis=-1, keepdims=True))
        alpha = jnp.exp2(m_old - m_new)
        p = jnp.exp2(s - m_new)
        pb = p.astype(BF16)
        pv = jnp.concatenate([pv_of(h, pb[h * tq:(h + 1) * tq, :]) for h in range(B_HEADS)], axis=0)
        l_ref[...] = alpha * l_ref[...] + jnp.sum(p, axis=-1, keepdims=True)
        acc_ref[...] = alpha * acc_ref[...] + pv
        m_ref[...] = m_new

    sel = jnp.concatenate([sc_ref[step * sub + u] > thr for u in range(sub)], axis=1)
    fold_in(lambda h: _dot(q_h[h], ck_ref[h].astype(BF16)),
            lambda h, p: _dot_nt(p, cv_ref[h].astype(BF16)), sel)

    @pl.when(step == nsteps - 1)
    def _():
        zpad = jnp.zeros((pad_rows, B_WIDTH), F32)
        kn = jnp.concatenate([kn_ref[...], zpad], axis=0).astype(BF16)
        vn = jnp.concatenate([vn_ref[...], zpad], axis=0).astype(BF16)
        fold_in(lambda h: _dot_nt(q_h[h], kn[:, h * B_HD:(h + 1) * B_HD]),
                lambda h, p: _dot(p, vn[:, h * B_HD:(h + 1) * B_HD]), sc_ref[npast] > thr)
        res = acc_ref[...] / l_ref[...]
        out = jnp.concatenate([res[h * tq:(h + 1) * tq, :] for h in range(B_HEADS)], axis=1)
        o_ref[...] = (out * _silu(zb_ref[...])).astype(BF16)


def _dsa_decode(qb, qi, misc, zb, kb, vb, ck, cv, cki, *, bsz, t_len, p_len, n_sel):
    kstep = DEC_KEYS_PER_STEP
    assert p_len // KEY_BLOCK + 1 < COUNT_RADIX and p_len % kstep == 0
    rows = B_HEADS * t_len
    qmap = lambda b, s: (b, 0)
    cmap = lambda b, s: (b, 0, 0, s)
    ck, cv = (a.transpose(0, 2, 3, 1) for a in (ck, cv))
    cki = cki.transpose(0, 2, 1)
    return pl.pallas_call(
        functools.partial(_dsa_dec_body, p_len=p_len, tq=t_len, n_sel=n_sel, kstep=kstep),
        grid=(bsz, p_len // kstep),
        in_specs=[pl.BlockSpec((t_len, B_WIDTH), qmap),
                  pl.BlockSpec((t_len, IDX_HEADS * IDX_HD), qmap),
                  pl.BlockSpec((t_len, MISC_W), qmap),
                  pl.BlockSpec((t_len, B_WIDTH), qmap),
                  pl.BlockSpec((t_len, B_WIDTH), qmap),
                  pl.BlockSpec((t_len, B_WIDTH), qmap),
                  pl.BlockSpec((None, B_HEADS, B_HD, kstep), cmap),
                  pl.BlockSpec((None, B_HEADS, B_HD, kstep), cmap),
                  pl.BlockSpec((None, IDX_HD, p_len), lambda b, s: (b, 0, 0))],
        out_specs=pl.BlockSpec((t_len, B_WIDTH), qmap),
        out_shape=jax.ShapeDtypeStruct((bsz * t_len, B_WIDTH), BF16),
        scratch_shapes=[pltpu.VMEM((p_len // KEY_BLOCK + 1, t_len, KEY_BLOCK), F32),
                        pltpu.VMEM((t_len, 1), F32),
                        pltpu.VMEM((rows, 1), F32),
                        pltpu.VMEM((rows, 1), F32),
                        pltpu.VMEM((rows, B_HD), F32)],
        compiler_params=_cparams(("parallel", "arbitrary")),
        name="dsa_decode",
    )(qb, qi, misc, zb, kb, vb, ck, cv, cki)


def _tail_body(x_ref, oa_ref, ob_ref, qm_ref, zm_ref, mk_ref, mv_ref, w_ref, y_ref):
    qm = qm_ref[...] * (M_HD ** -0.5)
    mk_t = mk_ref[...].astype(BF16)
    mv_t = mv_ref[...].astype(BF16)
    acc = _dot(oa_ref[...].astype(BF16), w_ref[0:A_WIDTH, :])
    acc = acc + _dot(ob_ref[...].astype(BF16), w_ref[A_WIDTH:A_WIDTH + B_WIDTH, :])
    om = jnp.zeros(qm.shape, F32)
    for h in range(M_HEADS):
        mask = _lane_mask(M_WIDTH, h * M_HD, (h + 1) * M_HD)
        s = _dot(jnp.where(mask, qm, 0.0).astype(BF16), mk_t)
        p = jnp.exp(s - jnp.max(s, axis=-1, keepdims=True))
        l = jnp.sum(p, axis=-1, keepdims=True)
        om = om + jnp.where(mask, _dot_nt(p.astype(BF16), mv_t) / l, 0.0)
    om = om * _silu(zm_ref[...])
    acc = acc + _dot(om.astype(BF16), w_ref[A_WIDTH + B_WIDTH:, :])
    y_ref[...] = x_ref[...] + acc


def _tail(x, oa, ob, qm, zm, mk, mv, w, *, bsz, t_len, tq):
    m, d = x.shape
    nq = t_len // tq
    qmap = lambda b, j: (b * nq + j, 0)
    mem_spec = pl.BlockSpec((None, M_WIDTH, N_MEM), lambda b, j: (b, 0, 0))
    return pl.pallas_call(
        _tail_body,
        grid=(bsz, nq),
        in_specs=[pl.BlockSpec((tq, d), qmap),
                  pl.BlockSpec((tq, A_WIDTH), qmap),
                  pl.BlockSpec((tq, B_WIDTH), qmap),
                  pl.BlockSpec((tq, M_WIDTH), qmap),
                  pl.BlockSpec((tq, M_WIDTH), qmap),
                  mem_spec, mem_spec,
                  pl.BlockSpec(w.shape, lambda b, j: (0, 0))],
        out_specs=pl.BlockSpec((tq, d), qmap),
        out_shape=jax.ShapeDtypeStruct((m, d), F32),
        compiler_params=_cparams(("parallel", "parallel")),
        name="tail",
    )(x, oa, ob, qm, zm, mk, mv, w)


def _regroup_body(wt_ref, o_ref):
    offs = [int(o) for o in np.concatenate([[0], np.cumsum(IN_SPLITS)])]
    grp = lambda i: wt_ref[offs[i]:offs[i + 1], :]
    (w_qkv, w_za, _, _, w_qb, w_kb, w_vb, w_zb, w_qi, w_ki, w_wi, w_qm, w_zm) = [grp(i) for i in range(13)]
    w_ba_aa = wt_ref[offs[2]:offs[4], :]
    pad = jnp.zeros((MISC_W - MISC_A - A_HEADS, wt_ref.shape[1]), F32)
    wt = jnp.concatenate([w_za, w_qb, w_kb, w_vb, w_zb, w_qi, w_qm, w_zm,
                          w_ki, w_wi, w_ba_aa, pad, w_qkv], axis=0)
    o_ref[...] = wt.T.astype(BF16)


def _regroup_weights(w_in):
    d, n_in = w_in.shape
    n_out = n_in - (IDX_HD + IDX_HEADS + 2 * A_HEADS) + MISC_W
    rows = 256
    return pl.pallas_call(
        _regroup_body,
        grid=(d // rows,),
        in_specs=[pl.BlockSpec((n_in, rows), lambda i: (0, i))],
        out_specs=pl.BlockSpec((rows, n_out), lambda i: (i, 0)),
        out_shape=jax.ShapeDtypeStruct((d, n_out), BF16),
        compiler_params=_cparams(("parallel",)),
        name="regroup",
    )(w_in.T)


def _prep_weights(g_in, w_in, conv_w, a_log, dt_bias, g_o, g_qb, g_kb, g_ki, g_qm, w_out):
    w_all = _regroup_weights(w_in)
    groups = [(A_WIDTH, False, ""), (B_WIDTH, True, ""),
              (B_WIDTH, True, "t"), (B_WIDTH, False, "t"), (B_WIDTH, False, ""),
              (IDX_HEADS * IDX_HD, False, ""), (M_WIDTH, True, "bf16"), (M_WIDTH, False, ""),
              (MISC_W, True, "t"),
              (A_WIDTH, False, "conv0"), (A_WIDTH, False, "conv1"), (A_WIDTH, False, "conv2")]
    bd64 = _group_mean_matrix(B_WIDTH, B_HD, B_WIDTH)
    ones256 = jnp.ones((1, B_WIDTH), F32)
    misc_gain = jnp.concatenate([g_ki, jnp.full((IDX_HEADS,), IDX_SCALE, F32),
                                 jnp.ones((MISC_W - MISC_B,), F32)])[None, :]
    misc_nm = (jnp.arange(MISC_W) < IDX_HD).astype(F32)[None, :]
    aux = [bd64, jnp.tile(g_qb, B_HEADS)[None, :], ones256,
           bd64, jnp.tile(g_kb, B_HEADS)[None, :], ones256,
           bd64, jnp.tile(g_qm, M_HEADS)[None, :], ones256,
           _group_mean_matrix(MISC_W, IDX_HD, IDX_HD), misc_gain, misc_nm]
    pad_a = lambda v: jnp.zeros((1, MISC_W), F32).at[0, MISC_A:MISC_A + A_HEADS].set(v)
    return dict(g_in=g_in[None, :], w_all=w_all, groups=groups, aux=aux, conv_w=conv_w,
                avec=pad_a(-jnp.exp(a_log)), dtvec=pad_a(dt_bias), g_o=g_o[None, :],
                w_out=w_out.astype(BF16))


def _layer(x, mem_k, mem_v, conv_buf, s0, past, gdn_chunk, gdn_rows, wp):
    bsz, t_len, d = x.shape
    m = bsz * t_len
    x2 = x.reshape(m, d)
    tm = min(PROJ_ROWS, m)
    heads_last = lambda a_t: a_t.reshape(bsz, B_HEADS, B_HD, t_len).transpose(0, 3, 1, 2)
    cbuf = jnp.concatenate([jnp.zeros((bsz, CARRY_ROWS - (CONV_W - 1), 3 * A_WIDTH), F32), conv_buf], axis=1)
    conv = (cbuf, wp["conv_w"])
    gdn_args = dict(bsz=bsz, t_len=t_len, chunk=gdn_chunk, rows=gdn_rows)
    if past is None:
        (za, qb, kb, vb, zb, qi, qm, zm, misc, qa, ka, va, kt, vt, misc_t, conv_new) = _proj(
            x2, wp["g_in"], wp["w_all"], wp["groups"], wp["aux"], tm, t_len, conv=conv)
        k_new, v_new = heads_last(kt), heads_last(vt)
        ki_new = misc_t[:, :IDX_HD, :].transpose(0, 2, 1)
        oa, s_new = _gdn(qa, ka, va, za, misc, s0, wp["avec"], wp["dtvec"], wp["g_o"], **gdn_args)
    else:
        groups = [(wd, normed, extra if extra == "bf16" else "") for wd, normed, extra in wp["groups"]]
        (za, qb, kb, vb, zb, qi, qm, zm, misc, qa, ka, va) = _proj(x2, wp["g_in"], wp["w_all"], groups,
                                                                   wp["aux"], tm, m)
        k_new, v_new = (a.reshape(bsz, t_len, B_HEADS, B_HD) for a in (kb, vb))
        ki_new = misc[:, :IDX_HD].reshape(bsz, t_len, IDX_HD)
        oa, s_new, conv_new = _gdn(qa, ka, va, za, misc, s0, wp["avec"], wp["dtvec"], wp["g_o"], conv=conv,
                                   **gdn_args)
    if past is None:
        n_sel = min(TOPK_MAX, t_len // 4)
        ob = _dsa_prompt(qb, qi, misc, zb, kb, vt, bsz=bsz, t_len=t_len, n_sel=n_sel, qblk=DSA_QBLK)
    else:
        ck, cv, cki = past
        p_len = ck.shape[1]
        assert (p_len + t_len - 1) // CHUNK <= p_len // CHUNK and p_len % KEY_BLOCK == 0
        n_sel = min(TOPK_MAX, (p_len + t_len) // 4)
        ob = _dsa_decode(qb, qi, misc, zb, kb, vb, ck, cv, cki, bsz=bsz, t_len=t_len, p_len=p_len,
                         n_sel=n_sel)
    y = _tail(x2, oa, ob, qm, zm, mem_k, mem_v, wp["w_out"], bsz=bsz, t_len=t_len, tq=min(TAIL_ROWS, t_len))
    return y.reshape(bsz, t_len, d), conv_new, s_new, k_new, v_new, ki_new


def _memory_kv(mem, g_mem, w_mem_kv, g_km):
    bsz, n_mem, d = mem.shape
    groups = [(M_WIDTH, True, "t"), (M_WIDTH, False, "t")]
    aux = [_group_mean_matrix(M_WIDTH, M_HD, M_WIDTH), jnp.tile(g_km, M_HEADS)[None, :],
           jnp.ones((1, M_WIDTH), F32)]
    _, _, mk_t, mv_t = _proj(mem.reshape(bsz * n_mem, d), g_mem[None, :], w_mem_kv.astype(BF16),
                             groups, aux, n_mem, n_mem)
    heads_last = lambda a_t: a_t.reshape(bsz, M_HEADS, M_HD, n_mem).transpose(0, 3, 1, 2)
    return mk_t, mv_t, heads_last(mk_t), heads_last(mv_t)


def kernel(x_prompt, x_sample, state_conv_A, state_ssm_A, cache_k_B, cache_v_B, cache_kidx_B, cache_mem_k,
           cache_mem_v, mem_prompt, g_in, w_in, conv_w_A, a_log_A, dt_bias_A, g_o_A, g_q_B, g_k_B, g_kidx_B,
           g_mem, w_mem_kv, g_q_M, g_k_M, w_out):
    depth = w_in.shape[0]
    assert depth == 1
    l = 0
    bp, t_p, _ = x_prompt.shape
    bs, t_s, _ = x_sample.shape
    wp = _prep_weights(g_in[l], w_in[l], conv_w_A[l], a_log_A[l], dt_bias_A[l], g_o_A[l], g_q_B[l], g_k_B[l],
                       g_kidx_B[l], g_q_M[l], w_out[l])
    mk, mv, mk_heads, mv_heads = _memory_kv(mem_prompt, g_mem[l], w_mem_kv[l], g_k_M[l])
    feature_major = lambda c: c.transpose(0, 2, 3, 1).reshape(c.shape[0], M_WIDTH, N_MEM)
    zero_conv = jnp.zeros((bp, CONV_W - 1, 3 * A_WIDTH), F32)
    zero_ssm = jnp.zeros((bp, A_HEADS, A_DK, A_DV), F32)
    yp, c1, s1, k1, v1, ki1 = _layer(x_prompt, mk, mv, zero_conv, zero_ssm, None, CHUNK, GDN_CHUNKS_PER_STEP * CHUNK, wp)
    ys, c2, s2, k2, v2, ki2 = _layer(
        x_sample, feature_major(cache_mem_k[l]), feature_major(cache_mem_v[l]),
        state_conv_A[l], state_ssm_A[l], (cache_k_B[l], cache_v_B[l], cache_kidx_B[l]), t_s, t_s, wp)
    st = lambda a: a[None]
    return (yp, ys, st(c1), st(s1), st(k1), st(v1), st(ki1),
            st(mk_heads), st(mv_heads),
            st(c2), st(s2), st(k2), st(v2), st(ki2))
```

```python
import functools
import math

import numpy as np
import jax
import jax.numpy as jnp
from jax import lax
from jax.experimental import pallas as pl
from jax.experimental.pallas import tpu as pltpu

F32 = jnp.float32
BF16 = jnp.bfloat16

CHUNK = 64
A_HEADS = 4
A_DK = 128
A_DV = 128
A_WIDTH = A_HEADS * A_DV
CONV_W = 4
B_HEADS = 4
B_HD = 64
B_WIDTH = B_HEADS * B_HD
IDX_HEADS = 8
IDX_HD = 32
IDX_SCALE = (IDX_HEADS ** -0.5) * (IDX_HD ** -0.5)
TOPK_MAX = 256
N_MEM = 256
M_HEADS = 4
M_HD = 64
M_WIDTH = M_HEADS * M_HD
EPS = 1e-6
IN_SPLITS = (3 * A_WIDTH, A_WIDTH, A_HEADS, A_HEADS,
             B_WIDTH, B_WIDTH, B_WIDTH, B_WIDTH, IDX_HEADS * IDX_HD, IDX_HD, IDX_HEADS,
             M_WIDTH, M_WIDTH)

LANES = 128
SUBLANES = 8
KEY_BLOCK = 256
VMEM_LIMIT = 48 * 1024 * 1024
DSA_QBLK = 512
GDN_CHUNKS_PER_STEP = 8
TAIL_ROWS = 1024
PROJ_ROWS = 512
REGROUP_ROWS = 256
DEC_KEYS_PER_STEP = 4096
SCORE_STRIP = 128
SEARCH_STRIP = 4 * SUBLANES
MISC_W = LANES
MISC_WI = IDX_HD
MISC_B = IDX_HD + IDX_HEADS
MISC_A = MISC_B + A_HEADS
NEG_BIG = -1e30
BISECT_PROBES = 2
COUNT_RADIX = 1024
BISECT_BLIND_ITERS = 10
BISECT_MAX_ITERS = 26


def _cparams(sem):
    return pltpu.CompilerParams(dimension_semantics=sem, vmem_limit_bytes=VMEM_LIMIT)


def _dot(a, b):
    return jnp.dot(a, b, preferred_element_type=F32)


def _dot_nt(a, b):
    return lax.dot_general(a, b, (((1,), (1,)), ((), ())), preferred_element_type=F32)


def _split2(x):
    hi = x.astype(BF16)
    return hi, (x - hi.astype(F32)).astype(BF16)


def _mm_split(a, b):
    ah, al = a
    bh, _ = b
    n = ah.shape[0]
    t = _dot(jnp.concatenate([ah, al], axis=0), bh)
    return t[:n] + t[n:]


def _silu(x):
    return x * jax.nn.sigmoid(x)


def _lane_mask(width, lo, hi):
    lane = lax.broadcasted_iota(jnp.int32, (1, width), 1)
    return (lane >= lo) & (lane < hi)


CARRY_ROWS = SUBLANES


def _conv_silu_norm(xp_ref, raw, cw, rows, part):
    first = CARRY_ROWS - (CONV_W - 1)
    xp_ref[CARRY_ROWS:CARRY_ROWS + rows, :] = raw
    y = xp_ref[first:first + rows, :] * cw[0:1, :]
    for j in range(1, CONV_W):
        y = y + xp_ref[first + j:first + j + rows, :] * cw[j:j + 1, :]
    y = _silu(y)
    conv_new = xp_ref[rows + first:rows + CARRY_ROWS, :]
    xp_ref[0:CARRY_ROWS, :] = xp_ref[rows:rows + CARRY_ROWS, :]
    if part == 2:
        return y, conv_new
    scale = A_DK ** -0.5 if part == 0 else 1.0
    heads = []
    for h in range(A_HEADS):
        yh = y[:, h * A_DK:(h + 1) * A_DK]
        heads.append(yh * (lax.rsqrt(jnp.sum(yh * yh, axis=-1, keepdims=True) + EPS) * scale))
    return jnp.concatenate(heads, axis=1), conv_new


def _conv_part(extra):
    return int(extra[len("conv"):]) if extra.startswith("conv") else None


def _proj_body(*refs, groups, tiles):
    x_ref, g_ref, w_ref = refs[0], refs[1], refs[2]
    n_aux = 3 * sum(1 for _, normed, _ in groups if normed)
    aux = refs[3:3 + n_aux]
    pos = 3 + n_aux
    n_conv = sum(1 for _, _, extra in groups if _conv_part(extra) is not None)
    if n_conv:
        cbuf_ref, cw_ref = refs[pos], refs[pos + 1]
        pos += 2
    outs = refs[pos:pos + len(groups)]
    pos += len(groups)
    n_t = sum(1 for _, _, extra in groups if extra == "t")
    outs_t = list(refs[pos:pos + n_t])
    pos += n_t
    if n_conv:
        convnew_ref = refs[pos]
        xp_refs = refs[pos + 1:pos + 1 + n_conv]

        @pl.when(pl.program_id(0) % tiles == 0)
        def _():
            for part, xp_ref in enumerate(xp_refs):
                xp_ref[0:CARRY_ROWS, :] = cbuf_ref[:, part * A_WIDTH:(part + 1) * A_WIDTH]
    x = x_ref[...]
    r = lax.rsqrt(jnp.mean(x * x, axis=-1, keepdims=True) + EPS)
    hb = (x * g_ref[...]).astype(BF16)
    off = 0
    ai = 0
    for (width, normed, extra), o_ref in zip(groups, outs):
        y = _dot(hb, w_ref[:, off:off + width]) * r
        part = _conv_part(extra)
        if part is not None:
            cols = slice(part * A_WIDTH, (part + 1) * A_WIDTH)
            y, conv_new = _conv_silu_norm(xp_refs[part], y, cw_ref[:, cols], y.shape[0], part)
            convnew_ref[:, cols] = conv_new
        if normed:
            bd_ref, gain_ref, nm_ref = aux[ai], aux[ai + 1], aux[ai + 2]
            ai += 3
            sq = y * y
            hi = sq.astype(BF16)
            lo = (sq - hi.astype(F32)).astype(BF16)
            ms = _dot(hi, bd_ref[...]) + _dot(lo, bd_ref[...])
            scale = jnp.where(nm_ref[...] > 0.0, lax.rsqrt(ms + EPS), 1.0)
            y = y * scale * gain_ref[...]
        o_ref[...] = y.astype(o_ref.dtype)
        if extra == "t":
            outs_t.pop(0)[...] = y.T
        off += width


def _group_mean_matrix(width, group, n_lanes):
    i = np.arange(width)
    m = ((i[:, None] // group) == (i[None, :] // group)) & (i[:, None] < n_lanes) & (i[None, :] < n_lanes)
    return jnp.asarray(m.astype(np.float32) / group, dtype=BF16)


def _proj(x, g, w, groups, aux, tm, t_len, conv=None):
    m, d = x.shape
    nw = w.shape[1]
    tiles = t_len // tm
    bsz = m // t_len
    in_specs = [pl.BlockSpec((tm, d), lambda i: (i, 0)),
                pl.BlockSpec((1, d), lambda i: (0, 0)),
                pl.BlockSpec((d, nw), lambda i: (0, 0))]
    for a in aux:
        in_specs.append(pl.BlockSpec(a.shape, lambda i: (0, 0)))
    out_shape = [jax.ShapeDtypeStruct((m, wd), BF16 if extra == "bf16" else F32) for wd, _, extra in groups]
    out_specs = [pl.BlockSpec((tm, wd), lambda i: (i, 0)) for wd, _, _ in groups]
    for wd, _, extra in groups:
        if extra == "t":
            out_shape.append(jax.ShapeDtypeStruct((bsz, wd, t_len), F32))
            out_specs.append(pl.BlockSpec((None, wd, tm), lambda i: (i // tiles, 0, i % tiles)))
    args = [x, g, w, *aux]
    scratch = []
    if conv is not None:
        cbuf, cw = conv
        wd = cw.shape[1]
        in_specs += [pl.BlockSpec((None, CARRY_ROWS, wd), lambda i: (i // tiles, 0, 0)),
                     pl.BlockSpec(cw.shape, lambda i: (0, 0))]
        args += [cbuf, cw]
        out_shape.append(jax.ShapeDtypeStruct((bsz, CONV_W - 1, wd), F32))
        out_specs.append(pl.BlockSpec((None, CONV_W - 1, wd), lambda i: (i // tiles, 0, 0)))
        scratch += [pltpu.VMEM((tm + CARRY_ROWS, A_WIDTH), F32) for _ in range(wd // A_WIDTH)]
    return pl.pallas_call(
        functools.partial(_proj_body, groups=tuple(groups), tiles=tiles),
        grid=(m // tm,),
        in_specs=in_specs, out_specs=out_specs, out_shape=out_shape, scratch_shapes=scratch,
        compiler_params=_cparams(("arbitrary",) if conv is not None else ("parallel",)),
        name="proj",
    )(*args)


def _gdn_body(*refs, chunk, rows, fused_conv):
    if fused_conv:
        (q_ref, k_ref, v_ref, za_ref, misc_ref, s0_ref, avec_ref, dtvec_ref, go_ref, cbuf_ref, cw_ref,
         o_ref, sfin_ref, convnew_ref, g_sc, b_sc, s_sc, *conv_sc) = refs
    else:
        (q_ref, k_ref, v_ref, za_ref, misc_ref, s0_ref, avec_ref, dtvec_ref, go_ref,
         o_ref, sfin_ref, g_sc, b_sc, s_sc) = refs
    t = pl.program_id(1)
    c = chunk

    @pl.when(t == 0)
    def _():
        s_sc[...] = s0_ref[0]

    if fused_conv:
        xp_refs, act_refs = conv_sc[:3], conv_sc[3:]

        @pl.when(t == 0)
        def _():
            for part, xp_ref in enumerate(xp_refs):
                xp_ref[0:CARRY_ROWS, :] = cbuf_ref[:, part * A_WIDTH:(part + 1) * A_WIDTH]
        for part, raw_ref in enumerate((q_ref, k_ref, v_ref)):
            cols = slice(part * A_WIDTH, (part + 1) * A_WIDTH)
            act, conv_new = _conv_silu_norm(xp_refs[part], raw_ref[...], cw_ref[:, cols], rows, part)
            convnew_ref[:, cols] = conv_new
            act_refs[part][...] = act
        q_sc, k_sc, v_sc = act_refs
    else:
        q_sc, k_sc, v_sc = q_ref, k_ref, v_ref
    misc = misc_ref[...]
    b_sc[...] = jax.nn.sigmoid(misc)
    g_sc[...] = avec_ref[...] * jax.nn.softplus(misc + dtvec_ref[...])

    ri = lax.broadcasted_iota(jnp.int32, (c, c), 0)
    ci = lax.broadcasted_iota(jnp.int32, (c, c), 1)
    tri = ri >= ci
    strict = ri > ci
    eye_f = (ri == ci).astype(F32)
    tri3 = (lax.broadcasted_iota(jnp.int32, (c, 3 * c), 0)
            >= lax.broadcasted_iota(jnp.int32, (c, 3 * c), 1) % c).astype(BF16)
    go = go_ref[...]
    n_chunks = rows // c

    problems = [(ic, h) for ic in range(n_chunks) for h in range(A_HEADS)]
    gc_all, gc_t = [], []
    for ic in range(n_chunks):
        g = g_sc[ic * c:(ic + 1) * c, :]
        g1 = g.astype(BF16)
        r1 = g - g1.astype(F32)
        g2 = r1.astype(BF16)
        g3 = (r1 - g2.astype(F32)).astype(BF16)
        gc_all.append(_dot(tri3, jnp.concatenate([g1, g2, g3], axis=0)))
    for ic in range(n_chunks):
        gc_t.append(gc_all[ic].T)

    def load(ref, ic, h):
        return ref[ic * c:(ic + 1) * c, h * A_DK:(h + 1) * A_DK]

    col = lambda a, lane: a[:, lane:lane + 1]
    k_bf = [load(k_sc, ic, h).astype(BF16) for ic, h in problems]
    kb = [load(k_sc, ic, h) * col(b_sc[ic * c:(ic + 1) * c, :], MISC_B + h) for ic, h in problems]
    kk = [_dot_nt(kb[i].astype(BF16), k_bf[i]) for i in range(len(problems))]
    qk_raw = [_dot_nt(load(q_sc, ic, h).astype(BF16), k_bf[i]) for i, (ic, h) in enumerate(problems)]
    decay = []
    for ic, h in problems:
        diff = col(gc_all[ic], MISC_A + h) - gc_t[ic][MISC_A + h:MISC_A + h + 1, :]
        decay.append(jnp.where(tri, jnp.exp(jnp.where(tri, diff, 0.0)), 0.0))
    lmat = [jnp.where(strict, kk[i] * decay[i], 0.0) for i in range(len(problems))]
    qk = [(qk_raw[i] * decay[i]).astype(BF16) for i in range(len(problems))]
    x_inv = [eye_f - m for m in lmat]
    p_split = [_split2(m) for m in lmat]
    for _ in range(int(math.log2(c)) - 1):
        p_split = [_split2(_mm_split(ps, ps)) for ps in p_split]
        x_inv = [x + _mm_split(_split2(x), ps) for x, ps in zip(x_inv, p_split)]
    sol = []
    for i, (ic, h) in enumerate(problems):
        egc = jnp.exp(col(gc_all[ic], MISC_A + h))
        beta = col(b_sc[ic * c:(ic + 1) * c, :], MISC_B + h)
        rhs = jnp.concatenate([load(v_sc, ic, h) * beta, kb[i] * egc], axis=-1).astype(BF16)
        sol.append(_mm_split(_split2(x_inv[i]), (rhs, None)))
    pre = []
    for i, (ic, h) in enumerate(problems):
        gc = col(gc_all[ic], MISC_A + h)
        g_last = gc[c - 1:c, :]
        k_dec_t = (load(k_sc, ic, h) * jnp.exp(g_last - gc)).T.astype(BF16)
        wq = jnp.concatenate([sol[i][:, A_DV:], load(q_sc, ic, h) * jnp.exp(gc)], axis=0).astype(BF16)
        pre.append((sol[i][:, :A_DV], wq, k_dec_t, jnp.exp(g_last)))

    heads = range(A_HEADS)
    s_cur = [s_sc[h] for h in heads]
    for ic in range(n_chunks):
        pr = [pre[ic * A_HEADS + h] for h in heads]
        ws = [_dot(pr[h][1], s_cur[h].astype(BF16)) for h in heads]
        ub = [(pr[h][0] - ws[h][:c]).astype(BF16) for h in heads]
        o = [ws[h][c:] + _dot(qk[ic * A_HEADS + h], ub[h]) for h in heads]
        s_cur = [s_cur[h] * pr[h][3] + _dot(pr[h][2], ub[h]) for h in heads]
        for h in heads:
            on = o[h] * lax.rsqrt(jnp.mean(o[h] * o[h], axis=-1, keepdims=True) + EPS) * go
            o_ref[ic * c:(ic + 1) * c, h * A_DV:(h + 1) * A_DV] = (on * _silu(load(za_ref, ic, h))).astype(BF16)
    for h in heads:
        s_sc[h] = s_cur[h]
    sfin_ref[0] = s_sc[...]


def _gdn(q, k, v, za, misc, s0, avec, dtvec, g_o, *, bsz, t_len, chunk, rows, conv=None):
    nt = t_len // rows
    w3 = 3 * A_WIDTH
    row_map = lambda b, t: (b * nt + t, 0)
    const2 = lambda b, t: (0, 0)
    state_spec = pl.BlockSpec((1, A_HEADS, A_DK, A_DV), lambda b, t: (b, 0, 0, 0))
    in_specs = [pl.BlockSpec((rows, A_WIDTH), row_map)] * 4 + [
                pl.BlockSpec((rows, MISC_W), row_map),
                state_spec,
                pl.BlockSpec((1, MISC_W), const2),
                pl.BlockSpec((1, MISC_W), const2),
                pl.BlockSpec((1, A_DV), const2)]
    args = [q, k, v, za, misc, s0, avec, dtvec, g_o]
    out_specs = [pl.BlockSpec((rows, A_WIDTH), row_map), state_spec]
    out_shape = [jax.ShapeDtypeStruct((bsz * t_len, A_WIDTH), BF16),
                 jax.ShapeDtypeStruct((bsz, A_HEADS, A_DK, A_DV), F32)]
    scratch = [pltpu.VMEM((rows, MISC_W), F32),
               pltpu.VMEM((rows, MISC_W), F32),
               pltpu.VMEM((A_HEADS, A_DK, A_DV), F32)]
    if conv is not None:
        cbuf, cw = conv
        in_specs += [pl.BlockSpec((None, CARRY_ROWS, w3), lambda b, t: (b, 0, 0)),
                     pl.BlockSpec((CONV_W, w3), const2)]
        args += [cbuf, cw]
        out_specs.append(pl.BlockSpec((None, CONV_W - 1, w3), lambda b, t: (b, 0, 0)))
        out_shape.append(jax.ShapeDtypeStruct((bsz, CONV_W - 1, w3), F32))
        scratch += [pltpu.VMEM((rows + CARRY_ROWS, A_WIDTH), F32)] * 3 + [pltpu.VMEM((rows, A_WIDTH), F32)] * 3
    return pl.pallas_call(
        functools.partial(_gdn_body, chunk=chunk, rows=rows, fused_conv=conv is not None),
        grid=(bsz, nt),
        in_specs=in_specs, out_specs=out_specs, out_shape=out_shape, scratch_shapes=scratch,
        compiler_params=_cparams(("parallel", "arbitrary")),
        name="gdn",
    )(*args)


def _topk_threshold(sc_ref, nkb, shape, key_axis, n_sel, n_adm, even_blocks=False):
    kf = float(n_sel)
    kblk = shape[key_axis]

    if key_axis == 0:
        strip = min(kblk, SEARCH_STRIP)
        part_shape = (strip, shape[1])
        strip_of = lambda kb, i: sc_ref[kb, i * strip:(i + 1) * strip, :]

        def put_strip(kb, i, v):
            sc_ref[kb, i * strip:(i + 1) * strip, :] = v
    else:
        strip = kblk
        part_shape = shape
        strip_of = lambda kb, i: sc_ref[kb]

        def put_strip(kb, i, v):
            sc_ref[kb] = v
    kpos = lax.broadcasted_iota(jnp.int32, part_shape, key_axis)

    def reduce_blocks(specs):
        def body(kb, accs):
            accs = list(accs)
            for i in range(kblk // strip):
                s = strip_of(kb, i)
                k0 = kb * kblk + i * strip
                accs = [cmb(acc, fn(s, k0)) for acc, (fn, _, _, _, cmb) in zip(accs, specs)]
            return tuple(accs)
        init = tuple(jnp.full(part_shape, i, dt) for _, i, dt, _, _ in specs)
        accs = lax.fori_loop(0, nkb, body, init)
        return [acc if op is None else op(acc, axis=key_axis, keepdims=True)
                for acc, (_, _, _, op, _) in zip(accs, specs)]

    one_if = lambda c: jnp.where(c, 1.0, 0.0)
    vmax, vmin, cpos, czero = reduce_blocks([
        (lambda s, k0: s, -jnp.inf, F32, jnp.max, jnp.maximum),
        (lambda s, k0: jnp.where(s == -jnp.inf, jnp.inf, s), jnp.inf, F32, jnp.min, jnp.minimum),
        (lambda s, k0: one_if(s > 0.0), 0.0, F32, jnp.sum, jnp.add),
        (lambda s, k0: one_if(s == 0.0), 0.0, F32, jnp.sum, jnp.add)])

    long_row = n_adm > kf
    pos_row = long_row & (cpos >= kf)
    zero_row = long_row & (cpos < kf) & (cpos + czero >= kf)
    neg_row = long_row & (cpos + czero < kf)
    lo0 = jnp.where(pos_row | zero_row, 0.0, vmin - jnp.maximum(1.0, jnp.abs(vmin)))
    cnt_lo0 = jnp.where(pos_row, cpos, jnp.where(zero_row, kf, n_adm))
    hi0 = jnp.where(neg_row, 0.0, vmax)
    cnt_hi0 = jnp.where(neg_row, cpos, 0.0)
    need_zero = jnp.where(zero_row, kf - cpos, 0.0)

    def narrow(lo, hi, cnt_lo, cnt_hi):
        fracs = [(t + 1.0) / (BISECT_PROBES + 1.0) for t in range(BISECT_PROBES)]
        mids = [lo * (1.0 - f) + hi * f for f in fracs]
        for t in range(1, BISECT_PROBES):
            mids[t] = jnp.maximum(mids[t], mids[t - 1])
        codes = [sum(COUNT_RADIX ** u for u in range(t + 1)) for t in range(BISECT_PROBES)]

        def encode(s, k0):
            e = jnp.zeros(s.shape, jnp.int32)
            for mid, code in zip(mids, codes):
                e = jnp.where(s > mid, code, e)
            return e
        (packed,) = reduce_blocks([(encode, 0, jnp.int32, None, jnp.add)])
        cnts = []
        for t in range(BISECT_PROBES):
            digit = packed & (COUNT_RADIX - 1) if t < BISECT_PROBES - 1 else packed
            packed = packed >> COUNT_RADIX.bit_length() - 1
            cnts.append(jnp.sum(digit.astype(F32), axis=key_axis, keepdims=True))
        lo_n, cnt_lo_n, hi_n, cnt_hi_n = lo, cnt_lo, hi, cnt_hi
        for mid, cm in zip(mids, cnts):
            up = cm >= kf
            lo_n = jnp.where(up, mid, lo_n)
            cnt_lo_n = jnp.where(up, cm, cnt_lo_n)
        for mid, cm in zip(mids[::-1], cnts[::-1]):
            dn = cm < kf
            hi_n = jnp.where(dn, mid, hi_n)
            cnt_hi_n = jnp.where(dn, cm, cnt_hi_n)
        return lo_n, hi_n, cnt_lo_n, cnt_hi_n

    def unresolved(cnt_lo):
        return (jnp.max(cnt_lo) > kf).astype(jnp.int32)

    blind = jnp.where(unresolved(cnt_lo0) > 0, BISECT_BLIND_ITERS, 0)
    state = lax.fori_loop(0, blind, lambda i, st: narrow(*st), (lo0, hi0, cnt_lo0, cnt_hi0))

    def cond(st):
        return (st[0] < BISECT_MAX_ITERS) & (st[1] > 0)

    def body(st):
        new = narrow(*st[2:])
        return (st[0] + 1, unresolved(new[2])) + new

    _, _, lo, hi, cnt_lo, cnt_hi = lax.while_loop(
        cond, body, (blind, unresolved(state[2])) + state)

    fix = cnt_lo > kf
    need0 = jnp.where(fix, kf - cnt_hi, 0.0)
    big_idx = 2 ** 30

    def fix_cond(need):
        return jnp.max(need) > 0.0

    def fix_body(need):
        def in_cluster(s):
            return (s > lo) & (s <= hi)
        (mval,) = reduce_blocks([(lambda s, k0: jnp.where(in_cluster(s), s, -jnp.inf), -jnp.inf, F32,
                                  jnp.max, jnp.maximum)])
        (idx,) = reduce_blocks(
            [(lambda s, k0: jnp.where(in_cluster(s) & (s == mval), kpos + k0, big_idx),
              big_idx, jnp.int32, jnp.min, jnp.minimum)])
        active = need > 0.0

        def promote(kb, carry):
            for i in range(kblk // strip):
                s = strip_of(kb, i)
                put_strip(kb, i, jnp.where(active & ((kpos + kb * kblk + i * strip) == idx), jnp.inf, s))
            return carry
        lax.fori_loop(0, nkb, promote, 0)
        return jnp.where(active, need - 1.0, need)

    lax.while_loop(fix_cond, fix_body, need0)

    @pl.when(jnp.max(need_zero) > 0.0)
    def _():
        r = lax.broadcasted_iota(jnp.int32, (kblk, kblk), 0)
        c = lax.broadcasted_iota(jnp.int32, (kblk, kblk), 1)
        tri = ((r >= c) if key_axis == 0 else (r <= c)).astype(BF16)

        def tie_blk(kb, seen):
            s = sc_ref[kb]
            z = (s == 0.0) & zero_row
            zb = one_if(z).astype(BF16)
            if key_axis == 0:
                rank = _dot(tri, zb) + seen
                last = rank[kblk - 1:kblk, :]
            else:
                rank = _dot(zb, tri) + seen
                last = rank[:, kblk - 1:kblk]
            sc_ref[kb] = jnp.where(z & (rank <= need_zero), jnp.inf, s)
            return last
        if even_blocks:
            lax.fori_loop(0, nkb // 2, lambda i, seen: tie_blk(2 * i + 1, tie_blk(2 * i, seen)),
                          jnp.zeros_like(need_zero))
        else:
            lax.fori_loop(0, nkb, tie_blk, jnp.zeros_like(need_zero))

    return jnp.where(fix, hi, lo)


def _dsa_body(qb_ref, qi_ref, miscq_ref, zb_ref, k_ref, vt_ref, misck_ref, o_ref, sc_ref, s_ref, *, n_sel, qblk):
    j = pl.program_id(1)
    nkb = ((j + 1) * qblk + KEY_BLOCK - 1) // KEY_BLOCK
    qi = qi_ref[...]
    q_h = [qi[:, h * IDX_HD:(h + 1) * IDX_HD].astype(BF16) for h in range(IDX_HEADS)]
    w_t = miscq_ref[...].T[MISC_WI:MISC_WI + IDX_HEADS, :]
    qchunk = (j * qblk + lax.broadcasted_iota(jnp.int32, (1, qblk), 1)) // CHUNK
    krow = lax.broadcasted_iota(jnp.int32, (KEY_BLOCK, 1), 0)

    def fold(acc, x, combine):
        for i in range(x.shape[0] // SUBLANES):
            acc = combine(acc, x[SUBLANES * i:SUBLANES * (i + 1)])
        return acc

    def score_blk(kb, carry):
        k0 = pl.multiple_of(kb * KEY_BLOCK, KEY_BLOCK)
        for half in range(KEY_BLOCK // SCORE_STRIP):
            r0 = half * SCORE_STRIP
            ki = misck_ref[pl.ds(k0 + r0, SCORE_STRIP), 0:IDX_HD].astype(BF16)
            acc = jnp.zeros((SCORE_STRIP, qblk), F32)
            for h in range(IDX_HEADS):
                acc = acc + w_t[h:h + 1, :] * jnp.maximum(_dot_nt(ki, q_h[h]), 0.0)
            adm = ((k0 + r0 + krow[:SCORE_STRIP]) // CHUNK) <= qchunk
            sc_ref[kb, r0:r0 + SCORE_STRIP, :] = jnp.where(adm, acc, -jnp.inf)
        return carry

    npairs = (nkb + 1) // 2

    def pair_loop(body, init):
        return lax.fori_loop(0, npairs, lambda i, c: body(2 * i + 1, body(2 * i, c)), init)

    pair_loop(score_blk, 0)

    n_adm = ((qchunk + 1) * CHUNK).astype(F32)
    thr = _topk_threshold(sc_ref, nkb, (KEY_BLOCK, qblk), 0, n_sel, n_adm,
                          even_blocks=qblk % (2 * KEY_BLOCK) == 0)

    qb = qb_ref[...] * (B_HD ** -0.5 * math.log2(math.e))
    qs = jnp.concatenate([jnp.where(_lane_mask(B_WIDTH, h * B_HD, (h + 1) * B_HD), qb, 0.0)
                          for h in range(B_HEADS)], axis=0).astype(BF16)
    wide = B_HEADS * qblk

    def logits_blk(kb, m8):
        k0 = pl.multiple_of(kb * KEY_BLOCK, KEY_BLOCK)
        s = _dot_nt(k_ref[pl.ds(k0, KEY_BLOCK), :].astype(BF16), qs)
        sel = sc_ref[kb] > thr
        s = jnp.where(jnp.concatenate([sel] * B_HEADS, axis=1), s, NEG_BIG)
        s_ref[kb] = s
        return fold(m8, s, jnp.maximum)

    m8 = pair_loop(logits_blk, jnp.full((SUBLANES, wide), NEG_BIG, F32))
    m = jnp.max(m8, axis=0, keepdims=True)

    def pv_blk(kb, carry):
        l8, accs = carry
        p = jnp.exp2(s_ref[kb] - m)
        pb = p.astype(BF16)
        k0 = pl.multiple_of(kb * KEY_BLOCK, KEY_BLOCK)
        vt = vt_ref[:, pl.ds(k0, KEY_BLOCK)].astype(BF16)
        accs = tuple(accs[h] + _dot(vt[h * B_HD:(h + 1) * B_HD, :], pb[:, h * qblk:(h + 1) * qblk])
                     for h in range(B_HEADS))
        return fold(l8, p, jnp.add), accs

    l8, accs = pair_loop(
        pv_blk, (jnp.zeros((SUBLANES, wide), F32), tuple(jnp.zeros((B_HD, qblk), F32) for _ in range(B_HEADS))))
    l = jnp.sum(l8, axis=0, keepdims=True)
    o_t = jnp.concatenate([accs[h] / l[:, h * qblk:(h + 1) * qblk] for h in range(B_HEADS)], axis=0)
    o_ref[...] = (o_t.T * _silu(zb_ref[...])).astype(BF16)


def _dsa_prompt(qb, qi, misc, zb, kb, vt, *, bsz, t_len, n_sel, qblk):
    nq = t_len // qblk
    nkb = t_len // KEY_BLOCK
    assert nkb * (KEY_BLOCK // SEARCH_STRIP) < COUNT_RADIX and nkb % 2 == 0
    qmap = lambda b, j: (b * nq + j, 0)
    kmap = lambda b, j: (b, 0)
    return pl.pallas_call(
        functools.partial(_dsa_body, n_sel=n_sel, qblk=qblk),
        grid=(bsz, nq),
        in_specs=[pl.BlockSpec((qblk, B_WIDTH), qmap),
                  pl.BlockSpec((qblk, IDX_HEADS * IDX_HD), qmap),
                  pl.BlockSpec((qblk, MISC_W), qmap),
                  pl.BlockSpec((qblk, B_WIDTH), qmap),
                  pl.BlockSpec((t_len, B_WIDTH), kmap),
                  pl.BlockSpec((None, B_WIDTH, t_len), lambda b, j: (b, 0, 0)),
                  pl.BlockSpec((t_len, MISC_W), kmap)],
        out_specs=pl.BlockSpec((qblk, B_WIDTH), qmap),
        out_shape=jax.ShapeDtypeStruct((bsz * t_len, B_WIDTH), BF16),
        scratch_shapes=[pltpu.VMEM((nkb, KEY_BLOCK, qblk), F32),
                        pltpu.VMEM((nkb, KEY_BLOCK, B_HEADS * qblk), F32)],
        compiler_params=_cparams(("parallel", "arbitrary")),
        name="dsa_prompt",
    )(qb, qi, misc, zb, kb, vt, misc)


def _dsa_dec_body(qb_ref, qi_ref, misc_ref, zb_ref, kn_ref, vn_ref, ck_ref, cv_ref, cki_ref, o_ref,
                  sc_ref, thr_ref, m_ref, l_ref, acc_ref, *, p_len, tq, n_sel, kstep):
    step = pl.program_id(1)
    nsteps = p_len // kstep
    sub = kstep // KEY_BLOCK
    npast = p_len // KEY_BLOCK
    nkb = npast + 1
    pad_rows = KEY_BLOCK - tq
    rows = B_HEADS * tq

    @pl.when(step == 0)
    def _():
        qi = qi_ref[...]
        misc = misc_ref[...]
        q2 = jnp.concatenate([qi[:, h * IDX_HD:(h + 1) * IDX_HD] for h in range(IDX_HEADS)],
                             axis=0).astype(BF16)
        w_h = [misc[:, MISC_WI + h:MISC_WI + h + 1] for h in range(IDX_HEADS)]

        def head_sum(lg):
            acc = jnp.zeros((tq, lg.shape[1]), F32)
            for h in range(IDX_HEADS):
                acc = acc + w_h[h] * jnp.maximum(lg[h * tq:(h + 1) * tq, :], 0.0)
            return acc

        for i in range(nsteps):
            sc = head_sum(_dot(q2, cki_ref[:, i * kstep:(i + 1) * kstep].astype(BF16)))
            for u in range(sub):
                sc_ref[i * sub + u] = sc[:, u * KEY_BLOCK:(u + 1) * KEY_BLOCK]
        lane = lax.broadcasted_iota(jnp.int32, (1, KEY_BLOCK), 1)
        ki_new = jnp.concatenate([misc[:, 0:IDX_HD], jnp.zeros((pad_rows, IDX_HD), F32)], axis=0)
        sc_ref[npast] = jnp.where(lane < tq, head_sum(_dot_nt(q2, ki_new.astype(BF16))), -jnp.inf)
        n_adm = jnp.full((tq, 1), float(p_len + tq), F32)
        thr_ref[...] = _topk_threshold(sc_ref, nkb, (tq, KEY_BLOCK), 1, n_sel, n_adm)
        m_ref[...] = jnp.full((rows, 1), NEG_BIG, F32)
        l_ref[...] = jnp.zeros((rows, 1), F32)
        acc_ref[...] = jnp.zeros((rows, B_HD), F32)

    qb = qb_ref[...] * (B_HD ** -0.5 * math.log2(math.e))
    q_h = [qb[:, h * B_HD:(h + 1) * B_HD].astype(BF16) for h in range(B_HEADS)]
    thr = thr_ref[...]

    def fold_in(logits_of, pv_of, sel):
        s = jnp.concatenate([jnp.where(sel, logits_of(h), NEG_BIG) for h in range(B_HEADS)], axis=0)
        m_old = m_ref[...]
        m_new = jnp.maximum(m_old, jnp.max(s, axis=-1, keepdims=True))
        alpha = jnp.exp2(m_old - m_new)
        p = jnp.exp2(s - m_new)
        pb = p.astype(BF16)
        pv = jnp.concatenate([pv_of(h, pb[h * tq:(h + 1) * tq, :]) for h in range(B_HEADS)], axis=0)
        l_ref[...] = alpha * l_ref[...] + jnp.sum(p, axis=-1, keepdims=True)
        acc_ref[...] = alpha * acc_ref[...] + pv
        m_ref[...] = m_new

    sel = jnp.concatenate([sc_ref[step * sub + u] > thr for u in range(sub)], axis=1)
    fold_in(lambda h: _dot(q_h[h], ck_ref[h].astype(BF16)),
            lambda h, p: _dot_nt(p, cv_ref[h].astype(BF16)), sel)

    @pl.when(step == nsteps - 1)
    def _():
        zpad = jnp.zeros((pad_rows, B_WIDTH), F32)
        kn = jnp.concatenate([kn_ref[...], zpad], axis=0).astype(BF16)
        vn = jnp.concatenate([vn_ref[...], zpad], axis=0).astype(BF16)
        fold_in(lambda h: _dot_nt(q_h[h], kn[:, h * B_HD:(h + 1) * B_HD]),
                lambda h, p: _dot(p, vn[:, h * B_HD:(h + 1) * B_HD]), sc_ref[npast] > thr)
        res = acc_ref[...] / l_ref[...]
        out = jnp.concatenate([res[h * tq:(h + 1) * tq, :] for h in range(B_HEADS)], axis=1)
        o_ref[...] = (out * _silu(zb_ref[...])).astype(BF16)


def _dsa_decode(qb, qi, misc, zb, kb, vb, ck, cv, cki, *, bsz, t_len, p_len, n_sel):
    kstep = DEC_KEYS_PER_STEP
    assert p_len // KEY_BLOCK + 1 < COUNT_RADIX and p_len % kstep == 0
    rows = B_HEADS * t_len
    qmap = lambda b, s: (b, 0)
    cmap = lambda b, s: (b, 0, 0, s)
    ck, cv = (a.transpose(0, 2, 3, 1) for a in (ck, cv))
    cki = cki.transpose(0, 2, 1)
    return pl.pallas_call(
        functools.partial(_dsa_dec_body, p_len=p_len, tq=t_len, n_sel=n_sel, kstep=kstep),
        grid=(bsz, p_len // kstep),
        in_specs=[pl.BlockSpec((t_len, B_WIDTH), qmap),
                  pl.BlockSpec((t_len, IDX_HEADS * IDX_HD), qmap),
                  pl.BlockSpec((t_len, MISC_W), qmap),
                  pl.BlockSpec((t_len, B_WIDTH), qmap),
                  pl.BlockSpec((t_len, B_WIDTH), qmap),
                  pl.BlockSpec((t_len, B_WIDTH), qmap),
                  pl.BlockSpec((None, B_HEADS, B_HD, kstep), cmap),
                  pl.BlockSpec((None, B_HEADS, B_HD, kstep), cmap),
                  pl.BlockSpec((None, IDX_HD, p_len), lambda b, s: (b, 0, 0))],
        out_specs=pl.BlockSpec((t_len, B_WIDTH), qmap),
        out_shape=jax.ShapeDtypeStruct((bsz * t_len, B_WIDTH), BF16),
        scratch_shapes=[pltpu.VMEM((p_len // KEY_BLOCK + 1, t_len, KEY_BLOCK), F32),
                        pltpu.VMEM((t_len, 1), F32),
                        pltpu.VMEM((rows, 1), F32),
                        pltpu.VMEM((rows, 1), F32),
                        pltpu.VMEM((rows, B_HD), F32)],
        compiler_params=_cparams(("parallel", "arbitrary")),
        name="dsa_decode",
    )(qb, qi, misc, zb, kb, vb, ck, cv, cki)


def _tail_body(x_ref, oa_ref, ob_ref, qm_ref, zm_ref, mk_ref, mv_ref, w_ref, y_ref):
    qm = qm_ref[...] * (M_HD ** -0.5)
    mk_t = mk_ref[...].astype(BF16)
    mv_t = mv_ref[...].astype(BF16)
    acc = _dot(oa_ref[...].astype(BF16), w_ref[0:A_WIDTH, :])
    acc = acc + _dot(ob_ref[...].astype(BF16), w_ref[A_WIDTH:A_WIDTH + B_WIDTH, :])
    om = jnp.zeros(qm.shape, F32)
    for h in range(M_HEADS):
        mask = _lane_mask(M_WIDTH, h * M_HD, (h + 1) * M_HD)
        s = _dot(jnp.where(mask, qm, 0.0).astype(BF16), mk_t)
        p = jnp.exp(s - jnp.max(s, axis=-1, keepdims=True))
        l = jnp.sum(p, axis=-1, keepdims=True)
        om = om + jnp.where(mask, _dot_nt(p.astype(BF16), mv_t) / l, 0.0)
    om = om * _silu(zm_ref[...])
    acc = acc + _dot(om.astype(BF16), w_ref[A_WIDTH + B_WIDTH:, :])
    y_ref[...] = x_ref[...] + acc


def _tail(x, oa, ob, qm, zm, mk, mv, w, *, bsz, t_len, tq):
    m, d = x.shape
    nq = t_len // tq
    qmap = lambda b, j: (b * nq + j, 0)
    mem_spec = pl.BlockSpec((None, M_WIDTH, N_MEM), lambda b, j: (b, 0, 0))
    return pl.pallas_call(
        _tail_body,
        grid=(bsz, nq),
        in_specs=[pl.BlockSpec((tq, d), qmap),
                  pl.BlockSpec((tq, A_WIDTH), qmap),
                  pl.BlockSpec((tq, B_WIDTH), qmap),
                  pl.BlockSpec((tq, M_WIDTH), qmap),
                  pl.BlockSpec((tq, M_WIDTH), qmap),
                  mem_spec, mem_spec,
                  pl.BlockSpec(w.shape, lambda b, j: (0, 0))],
        out_specs=pl.BlockSpec((tq, d), qmap),
        out_shape=jax.ShapeDtypeStruct((m, d), F32),
        compiler_params=_cparams(("parallel", "parallel")),
        name="tail",
    )(x, oa, ob, qm, zm, mk, mv, w)


def _regroup_body(wt_ref, o_ref):
    offs = [int(o) for o in np.concatenate([[0], np.cumsum(IN_SPLITS)])]
    grp = lambda i: wt_ref[offs[i]:offs[i + 1], :]
    (w_qkv, w_za, _, _, w_qb, w_kb, w_vb, w_zb, w_qi, w_ki, w_wi, w_qm, w_zm) = [grp(i) for i in range(13)]
    w_ba_aa = wt_ref[offs[2]:offs[4], :]
    pad = jnp.zeros((MISC_W - MISC_A - A_HEADS, wt_ref.shape[1]), F32)
    wt = jnp.concatenate([w_za, w_qb, w_kb, w_vb, w_zb, w_qi, w_qm, w_zm,
                          w_ki, w_wi, w_ba_aa, pad, w_qkv], axis=0)
    o_ref[...] = wt.T.astype(BF16)


def _regroup_weights(w_in):
    d, n_in = w_in.shape
    n_out = n_in - (IDX_HD + IDX_HEADS + 2 * A_HEADS) + MISC_W
    rows = REGROUP_ROWS
    return pl.pallas_call(
        _regroup_body,
        grid=(d // rows,),
        in_specs=[pl.BlockSpec((n_in, rows), lambda i: (0, i))],
        out_specs=pl.BlockSpec((rows, n_out), lambda i: (i, 0)),
        out_shape=jax.ShapeDtypeStruct((d, n_out), BF16),
        compiler_params=_cparams(("parallel",)),
        name="regroup",
    )(w_in.T)


def _prep_weights(g_in, w_in, conv_w, a_log, dt_bias, g_o, g_qb, g_kb, g_ki, g_qm, w_out):
    w_all = _regroup_weights(w_in)
    groups = [(A_WIDTH, False, ""), (B_WIDTH, True, ""),
              (B_WIDTH, True, "t"), (B_WIDTH, False, "t"), (B_WIDTH, False, ""),
              (IDX_HEADS * IDX_HD, False, ""), (M_WIDTH, True, "bf16"), (M_WIDTH, False, ""),
              (MISC_W, True, "t"),
              (A_WIDTH, False, "conv0"), (A_WIDTH, False, "conv1"), (A_WIDTH, False, "conv2")]
    bd64 = _group_mean_matrix(B_WIDTH, B_HD, B_WIDTH)
    ones256 = jnp.ones((1, B_WIDTH), F32)
    misc_gain = jnp.concatenate([g_ki, jnp.full((IDX_HEADS,), IDX_SCALE, F32),
                                 jnp.ones((MISC_W - MISC_B,), F32)])[None, :]
    misc_nm = (jnp.arange(MISC_W) < IDX_HD).astype(F32)[None, :]
    aux = [bd64, jnp.tile(g_qb, B_HEADS)[None, :], ones256,
           bd64, jnp.tile(g_kb, B_HEADS)[None, :], ones256,
           bd64, jnp.tile(g_qm, M_HEADS)[None, :], ones256,
           _group_mean_matrix(MISC_W, IDX_HD, IDX_HD), misc_gain, misc_nm]
    pad_a = lambda v: jnp.zeros((1, MISC_W), F32).at[0, MISC_A:MISC_A + A_HEADS].set(v)
    return dict(g_in=g_in[None, :], w_all=w_all, groups=groups, aux=aux, conv_w=conv_w,
                avec=pad_a(-jnp.exp(a_log)), dtvec=pad_a(dt_bias), g_o=g_o[None, :],
                w_out=w_out.astype(BF16))


def _layer(x, mem_k, mem_v, conv_buf, s0, past, gdn_chunk, gdn_rows, wp):
    bsz, t_len, d = x.shape
    m = bsz * t_len
    x2 = x.reshape(m, d)
    tm = min(PROJ_ROWS, m)
    heads_last = lambda a_t: a_t.reshape(bsz, B_HEADS, B_HD, t_len).transpose(0, 3, 1, 2)
    cbuf = jnp.concatenate([jnp.zeros((bsz, CARRY_ROWS - (CONV_W - 1), 3 * A_WIDTH), F32), conv_buf], axis=1)
    conv = (cbuf, wp["conv_w"])
    gdn_args = dict(bsz=bsz, t_len=t_len, chunk=gdn_chunk, rows=gdn_rows)
    if past is None:
        (za, qb, kb, vb, zb, qi, qm, zm, misc, qa, ka, va, kt, vt, misc_t, conv_new) = _proj(
            x2, wp["g_in"], wp["w_all"], wp["groups"], wp["aux"], tm, t_len, conv=conv)
        k_new, v_new = heads_last(kt), heads_last(vt)
        ki_new = misc_t[:, :IDX_HD, :].transpose(0, 2, 1)
        oa, s_new = _gdn(qa, ka, va, za, misc, s0, wp["avec"], wp["dtvec"], wp["g_o"], **gdn_args)
    else:
        groups = [(wd, normed, extra if extra == "bf16" else "") for wd, normed, extra in wp["groups"]]
        (za, qb, kb, vb, zb, qi, qm, zm, misc, qa, ka, va) = _proj(x2, wp["g_in"], wp["w_all"], groups,
                                                                   wp["aux"], tm, m)
        k_new, v_new = (a.reshape(bsz, t_len, B_HEADS, B_HD) for a in (kb, vb))
        ki_new = misc[:, :IDX_HD].reshape(bsz, t_len, IDX_HD)
        oa, s_new, conv_new = _gdn(qa, ka, va, za, misc, s0, wp["avec"], wp["dtvec"], wp["g_o"], conv=conv,
                                   **gdn_args)
    if past is None:
        n_sel = min(TOPK_MAX, t_len // 4)
        ob = _dsa_prompt(qb, qi, misc, zb, kb, vt, bsz=bsz, t_len=t_len, n_sel=n_sel, qblk=DSA_QBLK)
    else:
        ck, cv, cki = past
        p_len = ck.shape[1]
        assert (p_len + t_len - 1) // CHUNK <= p_len // CHUNK and p_len % KEY_BLOCK == 0
        n_sel = min(TOPK_MAX, (p_len + t_len) // 4)
        ob = _dsa_decode(qb, qi, misc, zb, kb, vb, ck, cv, cki, bsz=bsz, t_len=t_len, p_len=p_len,
                         n_sel=n_sel)
    y = _tail(x2, oa, ob, qm, zm, mem_k, mem_v, wp["w_out"], bsz=bsz, t_len=t_len, tq=min(TAIL_ROWS, t_len))
    return y.reshape(bsz, t_len, d), conv_new, s_new, k_new, v_new, ki_new


def _memory_kv(mem, g_mem, w_mem_kv, g_km):
    bsz, n_mem, d = mem.shape
    groups = [(M_WIDTH, True, "t"), (M_WIDTH, False, "t")]
    aux = [_group_mean_matrix(M_WIDTH, M_HD, M_WIDTH), jnp.tile(g_km, M_HEADS)[None, :],
           jnp.ones((1, M_WIDTH), F32)]
    _, _, mk_t, mv_t = _proj(mem.reshape(bsz * n_mem, d), g_mem[None, :], w_mem_kv.astype(BF16),
                             groups, aux, n_mem, n_mem)
    heads_last = lambda a_t: a_t.reshape(bsz, M_HEADS, M_HD, n_mem).transpose(0, 3, 1, 2)
    return mk_t, mv_t, heads_last(mk_t), heads_last(mv_t)


def kernel(x_prompt, x_sample, state_conv_A, state_ssm_A, cache_k_B, cache_v_B, cache_kidx_B, cache_mem_k,
           cache_mem_v, mem_prompt, g_in, w_in, conv_w_A, a_log_A, dt_bias_A, g_o_A, g_q_B, g_k_B, g_kidx_B,
           g_mem, w_mem_kv, g_q_M, g_k_M, w_out):
    depth = w_in.shape[0]
    assert depth == 1
    l = 0
    bp, t_p, _ = x_prompt.shape
    bs, t_s, _ = x_sample.shape
    wp = _prep_weights(g_in[l], w_in[l], conv_w_A[l], a_log_A[l], dt_bias_A[l], g_o_A[l], g_q_B[l], g_k_B[l],
                       g_kidx_B[l], g_q_M[l], w_out[l])
    mk, mv, mk_heads, mv_heads = _memory_kv(mem_prompt, g_mem[l], w_mem_kv[l], g_k_M[l])
    feature_major = lambda c: c.transpose(0, 2, 3, 1).reshape(c.shape[0], M_WIDTH, N_MEM)
    zero_conv = jnp.zeros((bp, CONV_W - 1, 3 * A_WIDTH), F32)
    zero_ssm = jnp.zeros((bp, A_HEADS, A_DK, A_DV), F32)
    yp, c1, s1, k1, v1, ki1 = _layer(x_prompt, mk, mv, zero_conv, zero_ssm, None, CHUNK, GDN_CHUNKS_PER_STEP * CHUNK, wp)
    ys, c2, s2, k2, v2, ki2 = _layer(
        x_sample, feature_major(cache_mem_k[l]), feature_major(cache_mem_v[l]),
        state_conv_A[l], state_ssm_A[l], (cache_k_B[l], cache_v_B[l], cache_kidx_B[l]), t_s, t_s, wp)
    st = lambda a: a[None]
    return (yp, ys, st(c1), st(s1), st(k1), st(v1), st(ki1),
            st(mk_heads), st(mv_heads),
            st(c2), st(s2), st(k2), st(v2), st(ki2))
```

```python
import functools
import math

import numpy as np
import jax
import jax.numpy as jnp
from jax import lax
from jax.experimental import pallas as pl
from jax.experimental.pallas import tpu as pltpu

F32 = jnp.float32
BF16 = jnp.bfloat16

CHUNK = 64
A_HEADS = 4
A_DK = 128
A_DV = 128
A_WIDTH = A_HEADS * A_DV
CONV_W = 4
B_HEADS = 4
B_HD = 64
B_WIDTH = B_HEADS * B_HD
IDX_HEADS = 8
IDX_HD = 32
IDX_SCALE = (IDX_HEADS ** -0.5) * (IDX_HD ** -0.5)
TOPK_MAX = 256
N_MEM = 256
M_HEADS = 4
M_HD = 64
M_WIDTH = M_HEADS * M_HD
EPS = 1e-6
IN_SPLITS = (3 * A_WIDTH, A_WIDTH, A_HEADS, A_HEADS,
             B_WIDTH, B_WIDTH, B_WIDTH, B_WIDTH, IDX_HEADS * IDX_HD, IDX_HD, IDX_HEADS,
             M_WIDTH, M_WIDTH)

LANES = 128
SUBLANES = 8
KEY_BLOCK = 256
VMEM_LIMIT = 48 * 1024 * 1024
DSA_QBLK = 512
GDN_CHUNKS_PER_STEP = 8
TAIL_ROWS = 1024
PROJ_ROWS = 512
REGROUP_ROWS = 256
DEC_KEYS_PER_STEP = 4096
SCORE_STRIP = 128
SEARCH_STRIP = 4 * SUBLANES
MISC_W = LANES
MISC_WI = IDX_HD
MISC_B = IDX_HD + IDX_HEADS
MISC_A = MISC_B + A_HEADS
NEG_BIG = -1e30
BISECT_PROBES = 2
COUNT_RADIX = 1024
BISECT_BLIND_ITERS = 10
BISECT_MAX_ITERS = 26


def _cparams(sem):
    return pltpu.CompilerParams(dimension_semantics=sem, vmem_limit_bytes=VMEM_LIMIT)


def _dot(a, b):
    return jnp.dot(a, b, preferred_element_type=F32)


def _dot_nt(a, b):
    return lax.dot_general(a, b, (((1,), (1,)), ((), ())), preferred_element_type=F32)


def _split2(x):
    hi = x.astype(BF16)
    return hi, (x - hi.astype(F32)).astype(BF16)


def _mm_split(a, b):
    ah, al = a
    bh, _ = b
    n = ah.shape[0]
    t = _dot(jnp.concatenate([ah, al], axis=0), bh)
    return t[:n] + t[n:]


def _silu(x):
    h = 0.5 * x
    return h + h * jnp.tanh(h)


def _lane_mask(width, lo, hi):
    lane = lax.broadcasted_iota(jnp.int32, (1, width), 1)
    return (lane >= lo) & (lane < hi)


CARRY_ROWS = SUBLANES


def _conv_silu_norm(xp_ref, raw, cw, rows, part):
    first = CARRY_ROWS - (CONV_W - 1)
    xp_ref[CARRY_ROWS:CARRY_ROWS + rows, :] = raw
    y = xp_ref[first:first + rows, :] * cw[0:1, :]
    for j in range(1, CONV_W):
        y = y + xp_ref[first + j:first + j + rows, :] * cw[j:j + 1, :]
    y = _silu(y)
    conv_new = xp_ref[rows + first:rows + CARRY_ROWS, :]
    xp_ref[0:CARRY_ROWS, :] = xp_ref[rows:rows + CARRY_ROWS, :]
    if part == 2:
        return y, conv_new
    scale = A_DK ** -0.5 if part == 0 else 1.0
    heads = []
    for h in range(A_HEADS):
        yh = y[:, h * A_DK:(h + 1) * A_DK]
        heads.append(yh * (lax.rsqrt(jnp.sum(yh * yh, axis=-1, keepdims=True) + EPS) * scale))
    return jnp.concatenate(heads, axis=1), conv_new


def _conv_part(extra):
    return int(extra[len("conv"):]) if extra.startswith("conv") else None


def _proj_body(*refs, groups, tiles):
    x_ref, g_ref, w_ref = refs[0], refs[1], refs[2]
    n_aux = 3 * sum(1 for _, normed, _ in groups if normed)
    aux = refs[3:3 + n_aux]
    pos = 3 + n_aux
    n_conv = sum(1 for _, _, extra in groups if _conv_part(extra) is not None)
    if n_conv:
        cbuf_ref, cw_ref = refs[pos], refs[pos + 1]
        pos += 2
    outs = refs[pos:pos + len(groups)]
    pos += len(groups)
    n_t = sum(1 for _, _, extra in groups if extra == "t")
    outs_t = list(refs[pos:pos + n_t])
    pos += n_t
    if n_conv:
        convnew_ref = refs[pos]
        xp_refs = refs[pos + 1:pos + 1 + n_conv]

        @pl.when(pl.program_id(0) % tiles == 0)
        def _():
            for part, xp_ref in enumerate(xp_refs):
                xp_ref[0:CARRY_ROWS, :] = cbuf_ref[:, part * A_WIDTH:(part + 1) * A_WIDTH]
    x = x_ref[...]
    r = lax.rsqrt(jnp.mean(x * x, axis=-1, keepdims=True) + EPS)
    hb = (x * g_ref[...]).astype(BF16)
    off = 0
    ai = 0
    for (width, normed, extra), o_ref in zip(groups, outs):
        y = _dot(hb, w_ref[:, off:off + width]) * r
        part = _conv_part(extra)
        if part is not None:
            cols = slice(part * A_WIDTH, (part + 1) * A_WIDTH)
            y, conv_new = _conv_silu_norm(xp_refs[part], y, cw_ref[:, cols], y.shape[0], part)
            convnew_ref[:, cols] = conv_new
        if normed:
            bd_ref, gain_ref, nm_ref = aux[ai], aux[ai + 1], aux[ai + 2]
            ai += 3
            ms = _dot((y * y).astype(BF16), bd_ref[...])
            scale = jnp.where(nm_ref[...] > 0.0, lax.rsqrt(ms + EPS), 1.0)
            y = y * scale * gain_ref[...]
        o_ref[...] = y.astype(o_ref.dtype)
        if extra == "t":
            outs_t.pop(0)[...] = y.T
        off += width


def _group_mean_matrix(width, group, n_lanes):
    i = np.arange(width)
    m = ((i[:, None] // group) == (i[None, :] // group)) & (i[:, None] < n_lanes) & (i[None, :] < n_lanes)
    return jnp.asarray(m.astype(np.float32) / group, dtype=BF16)


def _proj(x, g, w, groups, aux, tm, t_len, conv=None):
    m, d = x.shape
    nw = w.shape[1]
    tiles = t_len // tm
    bsz = m // t_len
    in_specs = [pl.BlockSpec((tm, d), lambda i: (i, 0)),
                pl.BlockSpec((1, d), lambda i: (0, 0)),
                pl.BlockSpec((d, nw), lambda i: (0, 0))]
    for a in aux:
        in_specs.append(pl.BlockSpec(a.shape, lambda i: (0, 0)))
    out_shape = [jax.ShapeDtypeStruct((m, wd), BF16 if extra == "bf16" else F32) for wd, _, extra in groups]
    out_specs = [pl.BlockSpec((tm, wd), lambda i: (i, 0)) for wd, _, _ in groups]
    for wd, _, extra in groups:
        if extra == "t":
            out_shape.append(jax.ShapeDtypeStruct((bsz, wd, t_len), F32))
            out_specs.append(pl.BlockSpec((None, wd, tm), lambda i: (i // tiles, 0, i % tiles)))
    args = [x, g, w, *aux]
    scratch = []
    if conv is not None:
        cbuf, cw = conv
        wd = cw.shape[1]
        in_specs += [pl.BlockSpec((None, CARRY_ROWS, wd), lambda i: (i // tiles, 0, 0)),
                     pl.BlockSpec(cw.shape, lambda i: (0, 0))]
        args += [cbuf, cw]
        out_shape.append(jax.ShapeDtypeStruct((bsz, CONV_W - 1, wd), F32))
        out_specs.append(pl.BlockSpec((None, CONV_W - 1, wd), lambda i: (i // tiles, 0, 0)))
        scratch += [pltpu.VMEM((tm + CARRY_ROWS, A_WIDTH), F32) for _ in range(wd // A_WIDTH)]
    return pl.pallas_call(
        functools.partial(_proj_body, groups=tuple(groups), tiles=tiles),
        grid=(m // tm,),
        in_specs=in_specs, out_specs=out_specs, out_shape=out_shape, scratch_shapes=scratch,
        compiler_params=_cparams(("arbitrary",) if conv is not None else ("parallel",)),
        name="proj",
    )(*args)


def _gdn_body(*refs, chunk, rows, fused_conv):
    if fused_conv:
        (q_ref, k_ref, v_ref, za_ref, misc_ref, s0_ref, avec_ref, dtvec_ref, go_ref, cbuf_ref, cw_ref,
         o_ref, sfin_ref, convnew_ref, g_sc, b_sc, s_sc, *conv_sc) = refs
    else:
        (q_ref, k_ref, v_ref, za_ref, misc_ref, s0_ref, avec_ref, dtvec_ref, go_ref,
         o_ref, sfin_ref, g_sc, b_sc, s_sc) = refs
    t = pl.program_id(1)
    c = chunk

    @pl.when(t == 0)
    def _():
        s_sc[...] = s0_ref[0]

    if fused_conv:
        xp_refs, act_refs = conv_sc[:3], conv_sc[3:]

        @pl.when(t == 0)
        def _():
            for part, xp_ref in enumerate(xp_refs):
                xp_ref[0:CARRY_ROWS, :] = cbuf_ref[:, part * A_WIDTH:(part + 1) * A_WIDTH]
        for part, raw_ref in enumerate((q_ref, k_ref, v_ref)):
            cols = slice(part * A_WIDTH, (part + 1) * A_WIDTH)
            act, conv_new = _conv_silu_norm(xp_refs[part], raw_ref[...], cw_ref[:, cols], rows, part)
            convnew_ref[:, cols] = conv_new
            act_refs[part][...] = act
        q_sc, k_sc, v_sc = act_refs
    else:
        q_sc, k_sc, v_sc = q_ref, k_ref, v_ref
    misc = misc_ref[...]
    b_sc[...] = jax.nn.sigmoid(misc)
    g_sc[...] = avec_ref[...] * jax.nn.softplus(misc + dtvec_ref[...])

    ri = lax.broadcasted_iota(jnp.int32, (c, c), 0)
    ci = lax.broadcasted_iota(jnp.int32, (c, c), 1)
    tri = ri >= ci
    strict = ri > ci
    eye_f = (ri == ci).astype(F32)
    tri3 = (lax.broadcasted_iota(jnp.int32, (c, 3 * c), 0)
            >= lax.broadcasted_iota(jnp.int32, (c, 3 * c), 1) % c).astype(BF16)
    go = go_ref[...]
    n_chunks = rows // c

    problems = [(ic, h) for ic in range(n_chunks) for h in range(A_HEADS)]
    gc_all, gc_t = [], []
    for ic in range(n_chunks):
        g = g_sc[ic * c:(ic + 1) * c, :]
        g1 = g.astype(BF16)
        r1 = g - g1.astype(F32)
        g2 = r1.astype(BF16)
        g3 = (r1 - g2.astype(F32)).astype(BF16)
        gc_all.append(_dot(tri3, jnp.concatenate([g1, g2, g3], axis=0)))
    for ic in range(n_chunks):
        gc_t.append(gc_all[ic].T)

    def load(ref, ic, h):
        return ref[ic * c:(ic + 1) * c, h * A_DK:(h + 1) * A_DK]

    col = lambda a, lane: a[:, lane:lane + 1]
    k_bf = [load(k_sc, ic, h).astype(BF16) for ic, h in problems]
    kb = [load(k_sc, ic, h) * col(b_sc[ic * c:(ic + 1) * c, :], MISC_B + h) for ic, h in problems]
    kk = [_dot_nt(kb[i].astype(BF16), k_bf[i]) for i in range(len(problems))]
    qk_raw = [_dot_nt(load(q_sc, ic, h).astype(BF16), k_bf[i]) for i, (ic, h) in enumerate(problems)]
    decay = []
    for ic, h in problems:
        diff = col(gc_all[ic], MISC_A + h) - gc_t[ic][MISC_A + h:MISC_A + h + 1, :]
        decay.append(jnp.where(tri, jnp.exp(jnp.where(tri, diff, 0.0)), 0.0))
    lmat = [jnp.where(strict, kk[i] * decay[i], 0.0) for i in range(len(problems))]
    qk = [(qk_raw[i] * decay[i]).astype(BF16) for i in range(len(problems))]
    x_inv = [eye_f - m for m in lmat]
    p_split = [_split2(m) for m in lmat]
    for _ in range(int(math.log2(c)) - 1):
        p_split = [_split2(_mm_split(ps, ps)) for ps in p_split]
        x_inv = [x + _mm_split(_split2(x), ps) for x, ps in zip(x_inv, p_split)]
    sol = []
    for i, (ic, h) in enumerate(problems):
        egc = jnp.exp(col(gc_all[ic], MISC_A + h))
        beta = col(b_sc[ic * c:(ic + 1) * c, :], MISC_B + h)
        rhs = jnp.concatenate([load(v_sc, ic, h) * beta, kb[i] * egc], axis=-1).astype(BF16)
        sol.append(_mm_split(_split2(x_inv[i]), (rhs, None)))
    pre = []
    for i, (ic, h) in enumerate(problems):
        gc = col(gc_all[ic], MISC_A + h)
        g_last = gc[c - 1:c, :]
        k_dec_t = (load(k_sc, ic, h) * jnp.exp(g_last - gc)).T.astype(BF16)
        wq = jnp.concatenate([sol[i][:, A_DV:], load(q_sc, ic, h) * jnp.exp(gc)], axis=0).astype(BF16)
        pre.append((sol[i][:, :A_DV], wq, k_dec_t, jnp.exp(g_last)))

    heads = range(A_HEADS)
    s_cur = [s_sc[h] for h in heads]
    for ic in range(n_chunks):
        pr = [pre[ic * A_HEADS + h] for h in heads]
        ws = [_dot(pr[h][1], s_cur[h].astype(BF16)) for h in heads]
        ub = [(pr[h][0] - ws[h][:c]).astype(BF16) for h in heads]
        o = [ws[h][c:] + _dot(qk[ic * A_HEADS + h], ub[h]) for h in heads]
        s_cur = [s_cur[h] * pr[h][3] + _dot(pr[h][2], ub[h]) for h in heads]
        for h in heads:
            on = o[h] * lax.rsqrt(jnp.mean(o[h] * o[h], axis=-1, keepdims=True) + EPS) * go
            o_ref[ic * c:(ic + 1) * c, h * A_DV:(h + 1) * A_DV] = (on * _silu(load(za_ref, ic, h))).astype(BF16)
    for h in heads:
        s_sc[h] = s_cur[h]
    sfin_ref[0] = s_sc[...]


def _gdn(q, k, v, za, misc, s0, avec, dtvec, g_o, *, bsz, t_len, chunk, rows, conv=None):
    nt = t_len // rows
    w3 = 3 * A_WIDTH
    row_map = lambda b, t: (b * nt + t, 0)
    const2 = lambda b, t: (0, 0)
    state_spec = pl.BlockSpec((1, A_HEADS, A_DK, A_DV), lambda b, t: (b, 0, 0, 0))
    in_specs = [pl.BlockSpec((rows, A_WIDTH), row_map)] * 4 + [
                pl.BlockSpec((rows, MISC_W), row_map),
                state_spec,
                pl.BlockSpec((1, MISC_W), const2),
                pl.BlockSpec((1, MISC_W), const2),
                pl.BlockSpec((1, A_DV), const2)]
    args = [q, k, v, za, misc, s0, avec, dtvec, g_o]
    out_specs = [pl.BlockSpec((rows, A_WIDTH), row_map), state_spec]
    out_shape = [jax.ShapeDtypeStruct((bsz * t_len, A_WIDTH), BF16),
                 jax.ShapeDtypeStruct((bsz, A_HEADS, A_DK, A_DV), F32)]
    scratch = [pltpu.VMEM((rows, MISC_W), F32),
               pltpu.VMEM((rows, MISC_W), F32),
               pltpu.VMEM((A_HEADS, A_DK, A_DV), F32)]
    if conv is not None:
        cbuf, cw = conv
        in_specs += [pl.BlockSpec((None, CARRY_ROWS, w3), lambda b, t: (b, 0, 0)),
                     pl.BlockSpec((CONV_W, w3), const2)]
        args += [cbuf, cw]
        out_specs.append(pl.BlockSpec((None, CONV_W - 1, w3), lambda b, t: (b, 0, 0)))
        out_shape.append(jax.ShapeDtypeStruct((bsz, CONV_W - 1, w3), F32))
        scratch += [pltpu.VMEM((rows + CARRY_ROWS, A_WIDTH), F32)] * 3 + [pltpu.VMEM((rows, A_WIDTH), F32)] * 3
    return pl.pallas_call(
        functools.partial(_gdn_body, chunk=chunk, rows=rows, fused_conv=conv is not None),
        grid=(bsz, nt),
        in_specs=in_specs, out_specs=out_specs, out_shape=out_shape, scratch_shapes=scratch,
        compiler_params=_cparams(("parallel", "arbitrary")),
        name="gdn",
    )(*args)


def _topk_threshold(sc_ref, nkb, shape, key_axis, n_sel, n_adm, even_blocks=False):
    kf = float(n_sel)
    kblk = shape[key_axis]

    if key_axis == 0:
        strip = min(kblk, SEARCH_STRIP)
        part_shape = (strip, shape[1])
        strip_of = lambda kb, i: sc_ref[kb, i * strip:(i + 1) * strip, :]

        def put_strip(kb, i, v):
            sc_ref[kb, i * strip:(i + 1) * strip, :] = v
    else:
        strip = kblk
        part_shape = shape
        strip_of = lambda kb, i: sc_ref[kb]

        def put_strip(kb, i, v):
            sc_ref[kb] = v
    kpos = lax.broadcasted_iota(jnp.int32, part_shape, key_axis)

    def reduce_blocks(specs):
        def body(kb, accs):
            accs = list(accs)
            for i in range(kblk // strip):
                s = strip_of(kb, i)
                k0 = kb * kblk + i * strip
                accs = [cmb(acc, fn(s, k0)) for acc, (fn, _, _, _, cmb) in zip(accs, specs)]
            return tuple(accs)
        init = tuple(jnp.full(part_shape, i, dt) for _, i, dt, _, _ in specs)
        accs = lax.fori_loop(0, nkb, body, init)
        return [acc if op is None else op(acc, axis=key_axis, keepdims=True)
                for acc, (_, _, _, op, _) in zip(accs, specs)]

    one_if = lambda c: jnp.where(c, 1.0, 0.0)
    vmax, vmin, cpos, czero = reduce_blocks([
        (lambda s, k0: s, -jnp.inf, F32, jnp.max, jnp.maximum),
        (lambda s, k0: jnp.where(s == -jnp.inf, jnp.inf, s), jnp.inf, F32, jnp.min, jnp.minimum),
        (lambda s, k0: one_if(s > 0.0), 0.0, F32, jnp.sum, jnp.add),
        (lambda s, k0: one_if(s == 0.0), 0.0, F32, jnp.sum, jnp.add)])

    long_row = n_adm > kf
    pos_row = long_row & (cpos >= kf)
    zero_row = long_row & (cpos < kf) & (cpos + czero >= kf)
    neg_row = long_row & (cpos + czero < kf)
    lo0 = jnp.where(pos_row | zero_row, 0.0, vmin - jnp.maximum(1.0, jnp.abs(vmin)))
    cnt_lo0 = jnp.where(pos_row, cpos, jnp.where(zero_row, kf, n_adm))
    hi0 = jnp.where(neg_row, 0.0, vmax)
    cnt_hi0 = jnp.where(neg_row, cpos, 0.0)
    need_zero = jnp.where(zero_row, kf - cpos, 0.0)

    def narrow(lo, hi, cnt_lo, cnt_hi):
        fracs = [(t + 1.0) / (BISECT_PROBES + 1.0) for t in range(BISECT_PROBES)]
        mids = [lo * (1.0 - f) + hi * f for f in fracs]
        for t in range(1, BISECT_PROBES):
            mids[t] = jnp.maximum(mids[t], mids[t - 1])
        codes = [sum(COUNT_RADIX ** u for u in range(t + 1)) for t in range(BISECT_PROBES)]

        def encode(s, k0):
            e = jnp.zeros(s.shape, jnp.int32)
            for mid, code in zip(mids, codes):
                e = jnp.where(s > mid, code, e)
            return e
        (packed,) = reduce_blocks([(encode, 0, jnp.int32, None, jnp.add)])
        cnts = []
        for t in range(BISECT_PROBES):
            digit = packed & (COUNT_RADIX - 1) if t < BISECT_PROBES - 1 else packed
            packed = packed >> COUNT_RADIX.bit_length() - 1
            cnts.append(jnp.sum(digit.astype(F32), axis=key_axis, keepdims=True))
        lo_n, cnt_lo_n, hi_n, cnt_hi_n = lo, cnt_lo, hi, cnt_hi
        for mid, cm in zip(mids, cnts):
            up = cm >= kf
            lo_n = jnp.where(up, mid, lo_n)
            cnt_lo_n = jnp.where(up, cm, cnt_lo_n)
        for mid, cm in zip(mids[::-1], cnts[::-1]):
            dn = cm < kf
            hi_n = jnp.where(dn, mid, hi_n)
            cnt_hi_n = jnp.where(dn, cm, cnt_hi_n)
        return lo_n, hi_n, cnt_lo_n, cnt_hi_n

    def unresolved(cnt_lo):
        return (jnp.max(cnt_lo) > kf).astype(jnp.int32)

    blind = jnp.where(unresolved(cnt_lo0) > 0, BISECT_BLIND_ITERS, 0)
    state = lax.fori_loop(0, blind, lambda i, st: narrow(*st), (lo0, hi0, cnt_lo0, cnt_hi0))

    def cond(st):
        return (st[0] < BISECT_MAX_ITERS) & (st[1] > 0)

    def body(st):
        new = narrow(*st[2:])
        return (st[0] + 1, unresolved(new[2])) + new

    _, _, lo, hi, cnt_lo, cnt_hi = lax.while_loop(
        cond, body, (blind, unresolved(state[2])) + state)

    fix = cnt_lo > kf
    need0 = jnp.where(fix, kf - cnt_hi, 0.0)
    big_idx = 2 ** 30

    def fix_cond(need):
        return jnp.max(need) > 0.0

    def fix_body(need):
        def in_cluster(s):
            return (s > lo) & (s <= hi)
        (mval,) = reduce_blocks([(lambda s, k0: jnp.where(in_cluster(s), s, -jnp.inf), -jnp.inf, F32,
                                  jnp.max, jnp.maximum)])
        (idx,) = reduce_blocks(
            [(lambda s, k0: jnp.where(in_cluster(s) & (s == mval), kpos + k0, big_idx),
              big_idx, jnp.int32, jnp.min, jnp.minimum)])
        active = need > 0.0

        def promote(kb, carry):
            for i in range(kblk // strip):
                s = strip_of(kb, i)
                put_strip(kb, i, jnp.where(active & ((kpos + kb * kblk + i * strip) == idx), jnp.inf, s))
            return carry
        lax.fori_loop(0, nkb, promote, 0)
        return jnp.where(active, need - 1.0, need)

    lax.while_loop(fix_cond, fix_body, need0)

    @pl.when(jnp.max(need_zero) > 0.0)
    def _():
        r = lax.broadcasted_iota(jnp.int32, (kblk, kblk), 0)
        c = lax.broadcasted_iota(jnp.int32, (kblk, kblk), 1)
        tri = ((r >= c) if key_axis == 0 else (r <= c)).astype(BF16)

        def tie_blk(kb, seen):
            s = sc_ref[kb]
            z = (s == 0.0) & zero_row
            zb = one_if(z).astype(BF16)
            if key_axis == 0:
                rank = _dot(tri, zb) + seen
                last = rank[kblk - 1:kblk, :]
            else:
                rank = _dot(zb, tri) + seen
                last = rank[:, kblk - 1:kblk]
            sc_ref[kb] = jnp.where(z & (rank <= need_zero), jnp.inf, s)
            return last
        if even_blocks:
            lax.fori_loop(0, nkb // 2, lambda i, seen: tie_blk(2 * i + 1, tie_blk(2 * i, seen)),
                          jnp.zeros_like(need_zero))
        else:
            lax.fori_loop(0, nkb, tie_blk, jnp.zeros_like(need_zero))

    return jnp.where(fix, hi, lo)


def _dsa_body(qb_ref, qi_ref, miscq_ref, zb_ref, k_ref, vt_ref, misck_ref, o_ref, sc_ref, s_ref, *, n_sel, qblk):
    j = pl.program_id(1)
    nkb = ((j + 1) * qblk + KEY_BLOCK - 1) // KEY_BLOCK
    qi = qi_ref[...]
    q_h = [qi[:, h * IDX_HD:(h + 1) * IDX_HD].astype(BF16) for h in range(IDX_HEADS)]
    w_t = miscq_ref[...].T[MISC_WI:MISC_WI + IDX_HEADS, :]
    qchunk = (j * qblk + lax.broadcasted_iota(jnp.int32, (1, qblk), 1)) // CHUNK
    krow = lax.broadcasted_iota(jnp.int32, (KEY_BLOCK, 1), 0)

    def fold(acc, x, combine):
        for i in range(x.shape[0] // SUBLANES):
            acc = combine(acc, x[SUBLANES * i:SUBLANES * (i + 1)])
        return acc

    def score_blk(kb, carry):
        k0 = pl.multiple_of(kb * KEY_BLOCK, KEY_BLOCK)
        for half in range(KEY_BLOCK // SCORE_STRIP):
            r0 = half * SCORE_STRIP
            ki = misck_ref[pl.ds(k0 + r0, SCORE_STRIP), 0:IDX_HD].astype(BF16)
            acc = jnp.zeros((SCORE_STRIP, qblk), F32)
            for h in range(IDX_HEADS):
                acc = acc + w_t[h:h + 1, :] * jnp.maximum(_dot_nt(ki, q_h[h]), 0.0)
            adm = ((k0 + r0 + krow[:SCORE_STRIP]) // CHUNK) <= qchunk
            sc_ref[kb, r0:r0 + SCORE_STRIP, :] = jnp.where(adm, acc, -jnp.inf)
        return carry

    npairs = (nkb + 1) // 2

    def pair_loop(body, init):
        return lax.fori_loop(0, npairs, lambda i, c: body(2 * i + 1, body(2 * i, c)), init)

    pair_loop(score_blk, 0)

    n_adm = ((qchunk + 1) * CHUNK).astype(F32)
    thr = _topk_threshold(sc_ref, nkb, (KEY_BLOCK, qblk), 0, n_sel, n_adm,
                          even_blocks=qblk % (2 * KEY_BLOCK) == 0)

    qb = qb_ref[...] * (B_HD ** -0.5 * math.log2(math.e))
    qs = jnp.concatenate([jnp.where(_lane_mask(B_WIDTH, h * B_HD, (h + 1) * B_HD), qb, 0.0)
                          for h in range(B_HEADS)], axis=0).astype(BF16)
    wide = B_HEADS * qblk

    def logits_blk(kb, m8):
        k0 = pl.multiple_of(kb * KEY_BLOCK, KEY_BLOCK)
        s = _dot_nt(k_ref[pl.ds(k0, KEY_BLOCK), :].astype(BF16), qs)
        sel = sc_ref[kb] > thr
        s = jnp.where(jnp.concatenate([sel] * B_HEADS, axis=1), s, NEG_BIG)
        s_ref[kb] = s
        return fold(m8, s, jnp.maximum)

    m8 = pair_loop(logits_blk, jnp.full((SUBLANES, wide), NEG_BIG, F32))
    m = jnp.max(m8, axis=0, keepdims=True)

    def pv_blk(kb, carry):
        l8, accs = carry
        p = jnp.exp2(s_ref[kb] - m)
        pb = p.astype(BF16)
        k0 = pl.multiple_of(kb * KEY_BLOCK, KEY_BLOCK)
        vt = vt_ref[:, pl.ds(k0, KEY_BLOCK)].astype(BF16)
        accs = tuple(accs[h] + _dot(vt[h * B_HD:(h + 1) * B_HD, :], pb[:, h * qblk:(h + 1) * qblk])
                     for h in range(B_HEADS))
        return fold(l8, p, jnp.add), accs

    l8, accs = pair_loop(
        pv_blk, (jnp.zeros((SUBLANES, wide), F32), tuple(jnp.zeros((B_HD, qblk), F32) for _ in range(B_HEADS))))
    l = jnp.sum(l8, axis=0, keepdims=True)
    o_t = jnp.concatenate([accs[h] / l[:, h * qblk:(h + 1) * qblk] for h in range(B_HEADS)], axis=0)
    o_ref[...] = (o_t.T * _silu(zb_ref[...])).astype(BF16)


def _dsa_prompt(qb, qi, misc, zb, kb, vt, *, bsz, t_len, n_sel, qblk):
    nq = t_len // qblk
    nkb = t_len // KEY_BLOCK
    assert nkb * (KEY_BLOCK // SEARCH_STRIP) < COUNT_RADIX and nkb % 2 == 0
    qmap = lambda b, j: (b * nq + j, 0)
    kmap = lambda b, j: (b, 0)
    return pl.pallas_call(
        functools.partial(_dsa_body, n_sel=n_sel, qblk=qblk),
        grid=(bsz, nq),
        in_specs=[pl.BlockSpec((qblk, B_WIDTH), qmap),
                  pl.BlockSpec((qblk, IDX_HEADS * IDX_HD), qmap),
                  pl.BlockSpec((qblk, MISC_W), qmap),
                  pl.BlockSpec((qblk, B_WIDTH), qmap),
                  pl.BlockSpec((t_len, B_WIDTH), kmap),
                  pl.BlockSpec((None, B_WIDTH, t_len), lambda b, j: (b, 0, 0)),
                  pl.BlockSpec((t_len, MISC_W), kmap)],
        out_specs=pl.BlockSpec((qblk, B_WIDTH), qmap),
        out_shape=jax.ShapeDtypeStruct((bsz * t_len, B_WIDTH), BF16),
        scratch_shapes=[pltpu.VMEM((nkb, KEY_BLOCK, qblk), F32),
                        pltpu.VMEM((nkb, KEY_BLOCK, B_HEADS * qblk), F32)],
        compiler_params=_cparams(("parallel", "arbitrary")),
        name="dsa_prompt",
    )(qb, qi, misc, zb, kb, vt, misc)


def _dsa_dec_body(qb_ref, qi_ref, misc_ref, zb_ref, kn_ref, vn_ref, ck_ref, cv_ref, cki_ref, o_ref,
                  sc_ref, thr_ref, m_ref, l_ref, acc_ref, *, p_len, tq, n_sel, kstep):
    step = pl.program_id(1)
    nsteps = p_len // kstep
    sub = kstep // KEY_BLOCK
    npast = p_len // KEY_BLOCK
    nkb = npast + 1
    pad_rows = KEY_BLOCK - tq
    rows = B_HEADS * tq

    @pl.when(step == 0)
    def _():
        qi = qi_ref[...]
        misc = misc_ref[...]
        q2 = jnp.concatenate([qi[:, h * IDX_HD:(h + 1) * IDX_HD] for h in range(IDX_HEADS)],
                             axis=0).astype(BF16)
        w_h = [misc[:, MISC_WI + h:MISC_WI + h + 1] for h in range(IDX_HEADS)]

        def head_sum(lg):
            acc = jnp.zeros((tq, lg.shape[1]), F32)
            for h in range(IDX_HEADS):
                acc = acc + w_h[h] * jnp.maximum(lg[h * tq:(h + 1) * tq, :], 0.0)
            return acc

        for i in range(nsteps):
            sc = head_sum(_dot(q2, cki_ref[:, i * kstep:(i + 1) * kstep].astype(BF16)))
            for u in range(sub):
                sc_ref[i * sub + u] = sc[:, u * KEY_BLOCK:(u + 1) * KEY_BLOCK]
        lane = lax.broadcasted_iota(jnp.int32, (1, KEY_BLOCK), 1)
        ki_new = jnp.concatenate([misc[:, 0:IDX_HD], jnp.zeros((pad_rows, IDX_HD), F32)], axis=0)
        sc_ref[npast] = jnp.where(lane < tq, head_sum(_dot_nt(q2, ki_new.astype(BF16))), -jnp.inf)
        n_adm = jnp.full((tq, 1), float(p_len + tq), F32)
        thr_ref[...] = _topk_threshold(sc_ref, nkb, (tq, KEY_BLOCK), 1, n_sel, n_adm)
        m_ref[...] = jnp.full((rows, 1), NEG_BIG, F32)
        l_ref[...] = jnp.zeros((rows, 1), F32)
        acc_ref[...] = jnp.zeros((rows, B_HD), F32)

    qb = qb_ref[...] * (B_HD ** -0.5 * math.log2(math.e))
    q_h = [qb[:, h * B_HD:(h + 1) * B_HD].astype(BF16) for h in range(B_HEADS)]
    thr = thr_ref[...]

    def fold_in(logits_of, pv_of, sel):
        s = jnp.concatenate([jnp.where(sel, logits_of(h), NEG_BIG) for h in range(B_HEADS)], axis=0)
        m_old = m_ref[...]
        m_new = jnp.maximum(m_old, jnp.max(s, axis=-1, keepdims=True))
        alpha = jnp.exp2(m_old - m_new)
        p = jnp.exp2(s - m_new)
        pb = p.astype(BF16)
        pv = jnp.concatenate([pv_of(h, pb[h * tq:(h + 1) * tq, :]) for h in range(B_HEADS)], axis=0)
        l_ref[...] = alpha * l_ref[...] + jnp.sum(p, axis=-1, keepdims=True)
        acc_ref[...] = alpha * acc_ref[...] + pv
        m_ref[...] = m_new

    sel = jnp.concatenate([sc_ref[step * sub + u] > thr for u in range(sub)], axis=1)
    fold_in(lambda h: _dot(q_h[h], ck_ref[h].astype(BF16)),
            lambda h, p: _dot_nt(p, cv_ref[h].astype(BF16)), sel)

    @pl.when(step == nsteps - 1)
    def _():
        zpad = jnp.zeros((pad_rows, B_WIDTH), F32)
        kn = jnp.concatenate([kn_ref[...], zpad], axis=0).astype(BF16)
        vn = jnp.concatenate([vn_ref[...], zpad], axis=0).astype(BF16)
        fold_in(lambda h: _dot_nt(q_h[h], kn[:, h * B_HD:(h + 1) * B_HD]),
                lambda h, p: _dot(p, vn[:, h * B_HD:(h + 1) * B_HD]), sc_ref[npast] > thr)
        res = acc_ref[...] / l_ref[...]
        out = jnp.concatenate([res[h * tq:(h + 1) * tq, :] for h in range(B_HEADS)], axis=1)
        o_ref[...] = (out * _silu(zb_ref[...])).astype(BF16)


def _dsa_decode(qb, qi, misc, zb, kb, vb, ck, cv, cki, *, bsz, t_len, p_len, n_sel):
    kstep = DEC_KEYS_PER_STEP
    assert p_len // KEY_BLOCK + 1 < COUNT_RADIX and p_len % kstep == 0
    rows = B_HEADS * t_len
    qmap = lambda b, s: (b, 0)
    cmap = lambda b, s: (b, 0, 0, s)
    ck, cv = (a.transpose(0, 2, 3, 1) for a in (ck, cv))
    cki = cki.transpose(0, 2, 1)
    return pl.pallas_call(
        functools.partial(_dsa_dec_body, p_len=p_len, tq=t_len, n_sel=n_sel, kstep=kstep),
        grid=(bsz, p_len // kstep),
        in_specs=[pl.BlockSpec((t_len, B_WIDTH), qmap),
                  pl.BlockSpec((t_len, IDX_HEADS * IDX_HD), qmap),
                  pl.BlockSpec((t_len, MISC_W), qmap),
                  pl.BlockSpec((t_len, B_WIDTH), qmap),
                  pl.BlockSpec((t_len, B_WIDTH), qmap),
                  pl.BlockSpec((t_len, B_WIDTH), qmap),
                  pl.BlockSpec((None, B_HEADS, B_HD, kstep), cmap),
                  pl.BlockSpec((None, B_HEADS, B_HD, kstep), cmap),
                  pl.BlockSpec((None, IDX_HD, p_len), lambda b, s: (b, 0, 0))],
        out_specs=pl.BlockSpec((t_len, B_WIDTH), qmap),
        out_shape=jax.ShapeDtypeStruct((bsz * t_len, B_WIDTH), BF16),
        scratch_shapes=[pltpu.VMEM((p_len // KEY_BLOCK + 1, t_len, KEY_BLOCK), F32),
                        pltpu.VMEM((t_len, 1), F32),
                        pltpu.VMEM((rows, 1), F32),
                        pltpu.VMEM((rows, 1), F32),
                        pltpu.VMEM((rows, B_HD), F32)],
        compiler_params=_cparams(("parallel", "arbitrary")),
        name="dsa_decode",
    )(qb, qi, misc, zb, kb, vb, ck, cv, cki)


def _tail_body(x_ref, oa_ref, ob_ref, qm_ref, zm_ref, mk_ref, mv_ref, w_ref, y_ref):
    qm = qm_ref[...] * (M_HD ** -0.5)
    mk_t = mk_ref[...].astype(BF16)
    mv_t = mv_ref[...].astype(BF16)
    acc = _dot(oa_ref[...].astype(BF16), w_ref[0:A_WIDTH, :])
    acc = acc + _dot(ob_ref[...].astype(BF16), w_ref[A_WIDTH:A_WIDTH + B_WIDTH, :])
    om = jnp.zeros(qm.shape, F32)
    for h in range(M_HEADS):
        mask = _lane_mask(M_WIDTH, h * M_HD, (h + 1) * M_HD)
        s = _dot(jnp.where(mask, qm, 0.0).astype(BF16), mk_t)
        p = jnp.exp(s - jnp.max(s, axis=-1, keepdims=True))
        l = jnp.sum(p, axis=-1, keepdims=True)
        om = om + jnp.where(mask, _dot_nt(p.astype(BF16), mv_t) / l, 0.0)
    om = om * _silu(zm_ref[...])
    acc = acc + _dot(om.astype(BF16), w_ref[A_WIDTH + B_WIDTH:, :])
    y_ref[...] = x_ref[...] + acc


def _tail(x, oa, ob, qm, zm, mk, mv, w, *, bsz, t_len, tq):
    m, d = x.shape
    nq = t_len // tq
    qmap = lambda b, j: (b * nq + j, 0)
    mem_spec = pl.BlockSpec((None, M_WIDTH, N_MEM), lambda b, j: (b, 0, 0))
    return pl.pallas_call(
        _tail_body,
        grid=(bsz, nq),
        in_specs=[pl.BlockSpec((tq, d), qmap),
                  pl.BlockSpec((tq, A_WIDTH), qmap),
                  pl.BlockSpec((tq, B_WIDTH), qmap),
                  pl.BlockSpec((tq, M_WIDTH), qmap),
                  pl.BlockSpec((tq, M_WIDTH), qmap),
                  mem_spec, mem_spec,
                  pl.BlockSpec(w.shape, lambda b, j: (0, 0))],
        out_specs=pl.BlockSpec((tq, d), qmap),
        out_shape=jax.ShapeDtypeStruct((m, d), F32),
        compiler_params=_cparams(("parallel", "parallel")),
        name="tail",
    )(x, oa, ob, qm, zm, mk, mv, w)


def _regroup_body(wt_ref, o_ref):
    offs = [int(o) for o in np.concatenate([[0], np.cumsum(IN_SPLITS)])]
    grp = lambda i: wt_ref[offs[i]:offs[i + 1], :]
    (w_qkv, w_za, _, _, w_qb, w_kb, w_vb, w_zb, w_qi, w_ki, w_wi, w_qm, w_zm) = [grp(i) for i in range(13)]
    w_ba_aa = wt_ref[offs[2]:offs[4], :]
    pad = jnp.zeros((MISC_W - MISC_A - A_HEADS, wt_ref.shape[1]), F32)
    wt = jnp.concatenate([w_za, w_qb, w_kb, w_vb, w_zb, w_qi, w_qm, w_zm,
                          w_ki, w_wi, w_ba_aa, pad, w_qkv], axis=0)
    o_ref[...] = wt.T.astype(BF16)


def _regroup_weights(w_in):
    d, n_in = w_in.shape
    n_out = n_in - (IDX_HD + IDX_HEADS + 2 * A_HEADS) + MISC_W
    rows = REGROUP_ROWS
    return pl.pallas_call(
        _regroup_body,
        grid=(d // rows,),
        in_specs=[pl.BlockSpec((n_in, rows), lambda i: (0, i))],
        out_specs=pl.BlockSpec((rows, n_out), lambda i: (i, 0)),
        out_shape=jax.ShapeDtypeStruct((d, n_out), BF16),
        compiler_params=_cparams(("parallel",)),
        name="regroup",
    )(w_in.T)


def _prep_weights(g_in, w_in, conv_w, a_log, dt_bias, g_o, g_qb, g_kb, g_ki, g_qm, w_out):
    w_all = _regroup_weights(w_in)
    groups = [(A_WIDTH, False, ""), (B_WIDTH, True, ""),
              (B_WIDTH, True, "t"), (B_WIDTH, False, "t"), (B_WIDTH, False, ""),
              (IDX_HEADS * IDX_HD, False, ""), (M_WIDTH, True, "bf16"), (M_WIDTH, False, ""),
              (MISC_W, True, "t"),
              (A_WIDTH, False, "conv0"), (A_WIDTH, False, "conv1"), (A_WIDTH, False, "conv2")]
    bd64 = _group_mean_matrix(B_WIDTH, B_HD, B_WIDTH)
    ones256 = jnp.ones((1, B_WIDTH), F32)
    misc_gain = jnp.concatenate([g_ki, jnp.full((IDX_HEADS,), IDX_SCALE, F32),
                                 jnp.ones((MISC_W - MISC_B,), F32)])[None, :]
    misc_nm = (jnp.arange(MISC_W) < IDX_HD).astype(F32)[None, :]
    aux = [bd64, jnp.tile(g_qb, B_HEADS)[None, :], ones256,
           bd64, jnp.tile(g_kb, B_HEADS)[None, :], ones256,
           bd64, jnp.tile(g_qm, M_HEADS)[None, :], ones256,
           _group_mean_matrix(MISC_W, IDX_HD, IDX_HD), misc_gain, misc_nm]
    pad_a = lambda v: jnp.zeros((1, MISC_W), F32).at[0, MISC_A:MISC_A + A_HEADS].set(v)
    return dict(g_in=g_in[None, :], w_all=w_all, groups=groups, aux=aux, conv_w=conv_w,
                avec=pad_a(-jnp.exp(a_log)), dtvec=pad_a(dt_bias), g_o=g_o[None, :],
                w_out=w_out.astype(BF16))


def _layer(x, mem_k, mem_v, conv_buf, s0, past, gdn_chunk, gdn_rows, wp):
    bsz, t_len, d = x.shape
    m = bsz * t_len
    x2 = x.reshape(m, d)
    tm = min(PROJ_ROWS, m)
    heads_last = lambda a_t: a_t.reshape(bsz, B_HEADS, B_HD, t_len).transpose(0, 3, 1, 2)
    cbuf = jnp.concatenate([jnp.zeros((bsz, CARRY_ROWS - (CONV_W - 1), 3 * A_WIDTH), F32), conv_buf], axis=1)
    conv = (cbuf, wp["conv_w"])
    gdn_args = dict(bsz=bsz, t_len=t_len, chunk=gdn_chunk, rows=gdn_rows)
    if past is None:
        (za, qb, kb, vb, zb, qi, qm, zm, misc, qa, ka, va, kt, vt, misc_t, conv_new) = _proj(
            x2, wp["g_in"], wp["w_all"], wp["groups"], wp["aux"], tm, t_len, conv=conv)
        k_new, v_new = heads_last(kt), heads_last(vt)
        ki_new = misc_t[:, :IDX_HD, :].transpose(0, 2, 1)
        oa, s_new = _gdn(qa, ka, va, za, misc, s0, wp["avec"], wp["dtvec"], wp["g_o"], **gdn_args)
    else:
        groups = [(wd, normed, extra if extra == "bf16" else "") for wd, normed, extra in wp["groups"]]
        (za, qb, kb, vb, zb, qi, qm, zm, misc, qa, ka, va) = _proj(x2, wp["g_in"], wp["w_all"], groups,
                                                                   wp["aux"], tm, m)
        k_new, v_new = (a.reshape(bsz, t_len, B_HEADS, B_HD) for a in (kb, vb))
        ki_new = misc[:, :IDX_HD].reshape(bsz, t_len, IDX_HD)
        oa, s_new, conv_new = _gdn(qa, ka, va, za, misc, s0, wp["avec"], wp["dtvec"], wp["g_o"], conv=conv,
                                   **gdn_args)
    if past is None:
        n_sel = min(TOPK_MAX, t_len // 4)
        ob = _dsa_prompt(qb, qi, misc, zb, kb, vt, bsz=bsz, t_len=t_len, n_sel=n_sel, qblk=DSA_QBLK)
    else:
        ck, cv, cki = past
        p_len = ck.shape[1]
        assert (p_len + t_len - 1) // CHUNK <= p_len // CHUNK and p_len % KEY_BLOCK == 0
        n_sel = min(TOPK_MAX, (p_len + t_len) // 4)
        ob = _dsa_decode(qb, qi, misc, zb, kb, vb, ck, cv, cki, bsz=bsz, t_len=t_len, p_len=p_len,
                         n_sel=n_sel)
    y = _tail(x2, oa, ob, qm, zm, mem_k, mem_v, wp["w_out"], bsz=bsz, t_len=t_len, tq=min(TAIL_ROWS, t_len))
    return y.reshape(bsz, t_len, d), conv_new, s_new, k_new, v_new, ki_new


def _memory_kv(mem, g_mem, w_mem_kv, g_km):
    bsz, n_mem, d = mem.shape
    groups = [(M_WIDTH, True, "t"), (M_WIDTH, False, "t")]
    aux = [_group_mean_matrix(M_WIDTH, M_HD, M_WIDTH), jnp.tile(g_km, M_HEADS)[None, :],
           jnp.ones((1, M_WIDTH), F32)]
    _, _, mk_t, mv_t = _proj(mem.reshape(bsz * n_mem, d), g_mem[None, :], w_mem_kv.astype(BF16),
                             groups, aux, n_mem, n_mem)
    heads_last = lambda a_t: a_t.reshape(bsz, M_HEADS, M_HD, n_mem).transpose(0, 3, 1, 2)
    return mk_t, mv_t, heads_last(mk_t), heads_last(mv_t)


def kernel(x_prompt, x_sample, state_conv_A, state_ssm_A, cache_k_B, cache_v_B, cache_kidx_B, cache_mem_k,
           cache_mem_v, mem_prompt, g_in, w_in, conv_w_A, a_log_A, dt_bias_A, g_o_A, g_q_B, g_k_B, g_kidx_B,
           g_mem, w_mem_kv, g_q_M, g_k_M, w_out):
    depth = w_in.shape[0]
    assert depth == 1
    l = 0
    bp, t_p, _ = x_prompt.shape
    bs, t_s, _ = x_sample.shape
    wp = _prep_weights(g_in[l], w_in[l], conv_w_A[l], a_log_A[l], dt_bias_A[l], g_o_A[l], g_q_B[l], g_k_B[l],
                       g_kidx_B[l], g_q_M[l], w_out[l])
    mk, mv, mk_heads, mv_heads = _memory_kv(mem_prompt, g_mem[l], w_mem_kv[l], g_k_M[l])
    feature_major = lambda c: c.transpose(0, 2, 3, 1).reshape(c.shape[0], M_WIDTH, N_MEM)
    zero_conv = jnp.zeros((bp, CONV_W - 1, 3 * A_WIDTH), F32)
    zero_ssm = jnp.zeros((bp, A_HEADS, A_DK, A_DV), F32)
    yp, c1, s1, k1, v1, ki1 = _layer(x_prompt, mk, mv, zero_conv, zero_ssm, None, CHUNK, GDN_CHUNKS_PER_STEP * CHUNK, wp)
    ys, c2, s2, k2, v2, ki2 = _layer(
        x_sample, feature_major(cache_mem_k[l]), feature_major(cache_mem_v[l]),
        state_conv_A[l], state_ssm_A[l], (cache_k_B[l], cache_v_B[l], cache_kidx_B[l]), t_s, t_s, wp)
    st = lambda a: a[None]
    return (yp, ys, st(c1), st(s1), st(k1), st(v1), st(ki1),
            st(mk_heads), st(mv_heads),
            st(c2), st(s2), st(k2), st(v2), st(ki2))
```

```python
import functools
import math

import numpy as np
import jax
import jax.numpy as jnp
from jax import lax
from jax.experimental import pallas as pl
from jax.experimental.pallas import tpu as pltpu

F32 = jnp.float32
BF16 = jnp.bfloat16

CHUNK = 64
A_HEADS = 4
A_DK = 128
A_DV = 128
A_WIDTH = A_HEADS * A_DV
CONV_W = 4
B_HEADS = 4
B_HD = 64
B_WIDTH = B_HEADS * B_HD
IDX_HEADS = 8
IDX_HD = 32
IDX_SCALE = (IDX_HEADS ** -0.5) * (IDX_HD ** -0.5)
TOPK_MAX = 256
N_MEM = 256
M_HEADS = 4
M_HD = 64
M_WIDTH = M_HEADS * M_HD
EPS = 1e-6
IN_SPLITS = (3 * A_WIDTH, A_WIDTH, A_HEADS, A_HEADS,
             B_WIDTH, B_WIDTH, B_WIDTH, B_WIDTH, IDX_HEADS * IDX_HD, IDX_HD, IDX_HEADS,
             M_WIDTH, M_WIDTH)

LANES = 128
SUBLANES = 8
KEY_BLOCK = 256
VMEM_LIMIT = 48 * 1024 * 1024
DSA_QBLK = 512
GDN_CHUNKS_PER_STEP = 8
TAIL_ROWS = 1024
PROJ_ROWS = 512
REGROUP_ROWS = 256
DEC_KEYS_PER_STEP = 4096
SCORE_STRIP = 128
SEARCH_STRIP = 4 * SUBLANES
MISC_W = LANES
MISC_WI = IDX_HD
MISC_B = IDX_HD + IDX_HEADS
MISC_A = MISC_B + A_HEADS
NEG_BIG = -1e30
BISECT_PROBES = 2
COUNT_RADIX = 1024
BISECT_BLIND_ITERS = 10
BISECT_MAX_ITERS = 26


def _cparams(sem):
    return pltpu.CompilerParams(dimension_semantics=sem, vmem_limit_bytes=VMEM_LIMIT)


def _dot(a, b):
    return jnp.dot(a, b, preferred_element_type=F32)


def _dot_nt(a, b):
    return lax.dot_general(a, b, (((1,), (1,)), ((), ())), preferred_element_type=F32)


def _split2(x):
    hi = x.astype(BF16)
    return hi, (x - hi.astype(F32)).astype(BF16)


def _mm_split(a, b):
    ah, al = a
    bh, _ = b
    n = ah.shape[0]
    t = _dot(jnp.concatenate([ah, al], axis=0), bh)
    return t[:n] + t[n:]


def _silu(x):
    h = 0.5 * x
    return h + h * jnp.tanh(h)


def _lane_mask(width, lo, hi):
    lane = lax.broadcasted_iota(jnp.int32, (1, width), 1)
    return (lane >= lo) & (lane < hi)


CARRY_ROWS = SUBLANES


def _conv_silu_norm(xp_ref, raw, cw, rows, part):
    first = CARRY_ROWS - (CONV_W - 1)
    xp_ref[CARRY_ROWS:CARRY_ROWS + rows, :] = raw
    y = xp_ref[first:first + rows, :] * cw[0:1, :]
    for j in range(1, CONV_W):
        y = y + xp_ref[first + j:first + j + rows, :] * cw[j:j + 1, :]
    y = _silu(y)
    conv_new = xp_ref[rows + first:rows + CARRY_ROWS, :]
    xp_ref[0:CARRY_ROWS, :] = xp_ref[rows:rows + CARRY_ROWS, :]
    if part == 2:
        return y, conv_new
    scale = A_DK ** -0.5 if part == 0 else 1.0
    heads = []
    for h in range(A_HEADS):
        yh = y[:, h * A_DK:(h + 1) * A_DK]
        heads.append(yh * (lax.rsqrt(jnp.sum(yh * yh, axis=-1, keepdims=True) + EPS) * scale))
    return jnp.concatenate(heads, axis=1), conv_new


def _conv_part(extra):
    return int(extra[len("conv"):]) if extra.startswith("conv") else None


def _proj_body(*refs, groups, tiles):
    x_ref, g_ref, w_ref = refs[0], refs[1], refs[2]
    n_aux = 3 * sum(1 for _, normed, _ in groups if normed)
    aux = refs[3:3 + n_aux]
    pos = 3 + n_aux
    n_conv = sum(1 for _, _, extra in groups if _conv_part(extra) is not None)
    if n_conv:
        cbuf_ref, cw_ref = refs[pos], refs[pos + 1]
        pos += 2
    outs = refs[pos:pos + len(groups)]
    pos += len(groups)
    n_t = sum(1 for _, _, extra in groups if extra == "t")
    outs_t = list(refs[pos:pos + n_t])
    pos += n_t
    if n_conv:
        convnew_ref = refs[pos]
        xp_refs = refs[pos + 1:pos + 1 + n_conv]

        @pl.when(pl.program_id(0) % tiles == 0)
        def _():
            for part, xp_ref in enumerate(xp_refs):
                xp_ref[0:CARRY_ROWS, :] = cbuf_ref[:, part * A_WIDTH:(part + 1) * A_WIDTH]
    x = x_ref[...]
    r = lax.rsqrt(jnp.mean(x * x, axis=-1, keepdims=True) + EPS)
    hb = (x * g_ref[...]).astype(BF16)
    off = 0
    ai = 0
    for (width, normed, extra), o_ref in zip(groups, outs):
        y = _dot(hb, w_ref[:, off:off + width]) * r
        part = _conv_part(extra)
        if part is not None:
            cols = slice(part * A_WIDTH, (part + 1) * A_WIDTH)
            y, conv_new = _conv_silu_norm(xp_refs[part], y, cw_ref[:, cols], y.shape[0], part)
            convnew_ref[:, cols] = conv_new
        if normed:
            bd_ref, gain_ref, nm_ref = aux[ai], aux[ai + 1], aux[ai + 2]
            ai += 3
            ms = _dot((y * y).astype(BF16), bd_ref[...])
            scale = jnp.where(nm_ref[...] > 0.0, lax.rsqrt(ms + EPS), 1.0)
            y = y * scale * gain_ref[...]
        o_ref[...] = y.astype(o_ref.dtype)
        if extra == "t":
            outs_t.pop(0)[...] = y.T
        off += width


def _group_mean_matrix(width, group, n_lanes):
    i = np.arange(width)
    m = ((i[:, None] // group) == (i[None, :] // group)) & (i[:, None] < n_lanes) & (i[None, :] < n_lanes)
    return jnp.asarray(m.astype(np.float32) / group, dtype=BF16)


def _proj(x, g, w, groups, aux, tm, t_len, conv=None):
    m, d = x.shape
    nw = w.shape[1]
    tiles = t_len // tm
    bsz = m // t_len
    in_specs = [pl.BlockSpec((tm, d), lambda i: (i, 0)),
                pl.BlockSpec((1, d), lambda i: (0, 0)),
                pl.BlockSpec((d, nw), lambda i: (0, 0))]
    for a in aux:
        in_specs.append(pl.BlockSpec(a.shape, lambda i: (0, 0)))
    out_shape = [jax.ShapeDtypeStruct((m, wd), BF16 if extra == "bf16" else F32) for wd, _, extra in groups]
    out_specs = [pl.BlockSpec((tm, wd), lambda i: (i, 0)) for wd, _, _ in groups]
    for wd, _, extra in groups:
        if extra == "t":
            out_shape.append(jax.ShapeDtypeStruct((bsz, wd, t_len), F32))
            out_specs.append(pl.BlockSpec((None, wd, tm), lambda i: (i // tiles, 0, i % tiles)))
    args = [x, g, w, *aux]
    scratch = []
    if conv is not None:
        cbuf, cw = conv
        wd = cw.shape[1]
        in_specs += [pl.BlockSpec((None, CARRY_ROWS, wd), lambda i: (i // tiles, 0, 0)),
                     pl.BlockSpec(cw.shape, lambda i: (0, 0))]
        args += [cbuf, cw]
        out_shape.append(jax.ShapeDtypeStruct((bsz, CONV_W - 1, wd), F32))
        out_specs.append(pl.BlockSpec((None, CONV_W - 1, wd), lambda i: (i // tiles, 0, 0)))
        scratch += [pltpu.VMEM((tm + CARRY_ROWS, A_WIDTH), F32) for _ in range(wd // A_WIDTH)]
    return pl.pallas_call(
        functools.partial(_proj_body, groups=tuple(groups), tiles=tiles),
        grid=(m // tm,),
        in_specs=in_specs, out_specs=out_specs, out_shape=out_shape, scratch_shapes=scratch,
        compiler_params=_cparams(("arbitrary",) if conv is not None else ("parallel",)),
        name="proj",
    )(*args)


def _gdn_body(*refs, chunk, rows, fused_conv):
    if fused_conv:
        (q_ref, k_ref, v_ref, za_ref, misc_ref, s0_ref, avec_ref, dtvec_ref, go_ref, cbuf_ref, cw_ref,
         o_ref, sfin_ref, convnew_ref, g_sc, b_sc, s_sc, *conv_sc) = refs
    else:
        (q_ref, k_ref, v_ref, za_ref, misc_ref, s0_ref, avec_ref, dtvec_ref, go_ref,
         o_ref, sfin_ref, g_sc, b_sc, s_sc) = refs
    t = pl.program_id(1)
    c = chunk

    @pl.when(t == 0)
    def _():
        s_sc[...] = s0_ref[0]

    if fused_conv:
        xp_refs, act_refs = conv_sc[:3], conv_sc[3:]

        @pl.when(t == 0)
        def _():
            for part, xp_ref in enumerate(xp_refs):
                xp_ref[0:CARRY_ROWS, :] = cbuf_ref[:, part * A_WIDTH:(part + 1) * A_WIDTH]
        for part, raw_ref in enumerate((q_ref, k_ref, v_ref)):
            cols = slice(part * A_WIDTH, (part + 1) * A_WIDTH)
            act, conv_new = _conv_silu_norm(xp_refs[part], raw_ref[...], cw_ref[:, cols], rows, part)
            convnew_ref[:, cols] = conv_new
            act_refs[part][...] = act
        q_sc, k_sc, v_sc = act_refs
    else:
        q_sc, k_sc, v_sc = q_ref, k_ref, v_ref
    misc = misc_ref[...]
    b_sc[...] = jax.nn.sigmoid(misc)
    g_sc[...] = avec_ref[...] * jax.nn.softplus(misc + dtvec_ref[...])

    ri = lax.broadcasted_iota(jnp.int32, (c, c), 0)
    ci = lax.broadcasted_iota(jnp.int32, (c, c), 1)
    tri = ri >= ci
    strict = ri > ci
    eye_f = (ri == ci).astype(F32)
    tri3 = (lax.broadcasted_iota(jnp.int32, (c, 3 * c), 0)
            >= lax.broadcasted_iota(jnp.int32, (c, 3 * c), 1) % c).astype(BF16)
    go = go_ref[...]
    n_chunks = rows // c

    problems = [(ic, h) for ic in range(n_chunks) for h in range(A_HEADS)]
    gc_all, gc_t = [], []
    for ic in range(n_chunks):
        g = g_sc[ic * c:(ic + 1) * c, :]
        g1 = g.astype(BF16)
        r1 = g - g1.astype(F32)
        g2 = r1.astype(BF16)
        g3 = (r1 - g2.astype(F32)).astype(BF16)
        gc_all.append(_dot(tri3, jnp.concatenate([g1, g2, g3], axis=0)))
    for ic in range(n_chunks):
        gc_t.append(gc_all[ic].T)

    def load(ref, ic, h):
        return ref[ic * c:(ic + 1) * c, h * A_DK:(h + 1) * A_DK]

    col = lambda a, lane: a[:, lane:lane + 1]
    k_bf = [load(k_sc, ic, h).astype(BF16) for ic, h in problems]
    kb = [load(k_sc, ic, h) * col(b_sc[ic * c:(ic + 1) * c, :], MISC_B + h) for ic, h in problems]
    kk = [_dot_nt(kb[i].astype(BF16), k_bf[i]) for i in range(len(problems))]
    qk_raw = [_dot_nt(load(q_sc, ic, h).astype(BF16), k_bf[i]) for i, (ic, h) in enumerate(problems)]
    decay = []
    for ic, h in problems:
        diff = col(gc_all[ic], MISC_A + h) - gc_t[ic][MISC_A + h:MISC_A + h + 1, :]
        decay.append(jnp.where(tri, jnp.exp(jnp.where(tri, diff, 0.0)), 0.0))
    lmat = [jnp.where(strict, kk[i] * decay[i], 0.0) for i in range(len(problems))]
    qk = [(qk_raw[i] * decay[i]).astype(BF16) for i in range(len(problems))]
    x_inv = [eye_f - m for m in lmat]
    p_split = [_split2(m) for m in lmat]
    for _ in range(int(math.log2(c)) - 1):
        p_split = [_split2(_mm_split(ps, ps)) for ps in p_split]
        x_inv = [x + _mm_split(_split2(x), ps) for x, ps in zip(x_inv, p_split)]
    sol = []
    for i, (ic, h) in enumerate(problems):
        egc = jnp.exp(col(gc_all[ic], MISC_A + h))
        beta = col(b_sc[ic * c:(ic + 1) * c, :], MISC_B + h)
        rhs = jnp.concatenate([load(v_sc, ic, h) * beta, kb[i] * egc], axis=-1).astype(BF16)
        sol.append(_mm_split(_split2(x_inv[i]), (rhs, None)))
    pre = []
    for i, (ic, h) in enumerate(problems):
        gc = col(gc_all[ic], MISC_A + h)
        g_last = gc[c - 1:c, :]
        k_dec_t = (load(k_sc, ic, h) * jnp.exp(g_last - gc)).T.astype(BF16)
        wq = jnp.concatenate([sol[i][:, A_DV:], load(q_sc, ic, h) * jnp.exp(gc)], axis=0).astype(BF16)
        pre.append((sol[i][:, :A_DV], wq, k_dec_t, jnp.exp(g_last)))

    heads = range(A_HEADS)
    s_cur = [s_sc[h] for h in heads]
    for ic in range(n_chunks):
        pr = [pre[ic * A_HEADS + h] for h in heads]
        ws = [_dot(pr[h][1], s_cur[h].astype(BF16)) for h in heads]
        ub = [(pr[h][0] - ws[h][:c]).astype(BF16) for h in heads]
        o = [ws[h][c:] + _dot(qk[ic * A_HEADS + h], ub[h]) for h in heads]
        s_cur = [s_cur[h] * pr[h][3] + _dot(pr[h][2], ub[h]) for h in heads]
        for h in heads:
            on = o[h] * lax.rsqrt(jnp.mean(o[h] * o[h], axis=-1, keepdims=True) + EPS) * go
            o_ref[ic * c:(ic + 1) * c, h * A_DV:(h + 1) * A_DV] = (on * _silu(load(za_ref, ic, h))).astype(BF16)
    for h in heads:
        s_sc[h] = s_cur[h]
    sfin_ref[0] = s_sc[...]


def _gdn(q, k, v, za, misc, s0, avec, dtvec, g_o, *, bsz, t_len, chunk, rows, conv=None):
    nt = t_len // rows
    w3 = 3 * A_WIDTH
    row_map = lambda b, t: (b * nt + t, 0)
    const2 = lambda b, t: (0, 0)
    state_spec = pl.BlockSpec((1, A_HEADS, A_DK, A_DV), lambda b, t: (b, 0, 0, 0))
    in_specs = [pl.BlockSpec((rows, A_WIDTH), row_map)] * 4 + [
                pl.BlockSpec((rows, MISC_W), row_map),
                state_spec,
                pl.BlockSpec((1, MISC_W), const2),
                pl.BlockSpec((1, MISC_W), const2),
                pl.BlockSpec((1, A_DV), const2)]
    args = [q, k, v, za, misc, s0, avec, dtvec, g_o]
    out_specs = [pl.BlockSpec((rows, A_WIDTH), row_map), state_spec]
    out_shape = [jax.ShapeDtypeStruct((bsz * t_len, A_WIDTH), BF16),
                 jax.ShapeDtypeStruct((bsz, A_HEADS, A_DK, A_DV), F32)]
    scratch = [pltpu.VMEM((rows, MISC_W), F32),
               pltpu.VMEM((rows, MISC_W), F32),
               pltpu.VMEM((A_HEADS, A_DK, A_DV), F32)]
    if conv is not None:
        cbuf, cw = conv
        in_specs += [pl.BlockSpec((None, CARRY_ROWS, w3), lambda b, t: (b, 0, 0)),
                     pl.BlockSpec((CONV_W, w3), const2)]
        args += [cbuf, cw]
        out_specs.append(pl.BlockSpec((None, CONV_W - 1, w3), lambda b, t: (b, 0, 0)))
        out_shape.append(jax.ShapeDtypeStruct((bsz, CONV_W - 1, w3), F32))
        scratch += [pltpu.VMEM((rows + CARRY_ROWS, A_WIDTH), F32)] * 3 + [pltpu.VMEM((rows, A_WIDTH), F32)] * 3
    return pl.pallas_call(
        functools.partial(_gdn_body, chunk=chunk, rows=rows, fused_conv=conv is not None),
        grid=(bsz, nt),
        in_specs=in_specs, out_specs=out_specs, out_shape=out_shape, scratch_shapes=scratch,
        compiler_params=_cparams(("parallel", "arbitrary")),
        name="gdn",
    )(*args)


def _topk_threshold(sc_ref, nkb, shape, key_axis, n_sel, n_adm, even_blocks=False):
    kf = float(n_sel)
    kblk = shape[key_axis]

    if key_axis == 0:
        strip = min(kblk, SEARCH_STRIP)
        part_shape = (strip, shape[1])
        strip_of = lambda kb, i: sc_ref[kb, i * strip:(i + 1) * strip, :]

        def put_strip(kb, i, v):
            sc_ref[kb, i * strip:(i + 1) * strip, :] = v
    else:
        strip = kblk
        part_shape = shape
        strip_of = lambda kb, i: sc_ref[kb]

        def put_strip(kb, i, v):
            sc_ref[kb] = v
    kpos = lax.broadcasted_iota(jnp.int32, part_shape, key_axis)

    def reduce_blocks(specs):
        def body(kb, accs):
            accs = list(accs)
            for i in range(kblk // strip):
                s = strip_of(kb, i)
                k0 = kb * kblk + i * strip
                accs = [cmb(acc, fn(s, k0)) for acc, (fn, _, _, _, cmb) in zip(accs, specs)]
            return tuple(accs)
        init = tuple(jnp.full(part_shape, i, dt) for _, i, dt, _, _ in specs)
        accs = lax.fori_loop(0, nkb, body, init)
        return [acc if op is None else op(acc, axis=key_axis, keepdims=True)
                for acc, (_, _, _, op, _) in zip(accs, specs)]

    one_if = lambda c: jnp.where(c, 1.0, 0.0)
    vmax, vmin, cpos, czero = reduce_blocks([
        (lambda s, k0: s, -jnp.inf, F32, jnp.max, jnp.maximum),
        (lambda s, k0: jnp.where(s == -jnp.inf, jnp.inf, s), jnp.inf, F32, jnp.min, jnp.minimum),
        (lambda s, k0: one_if(s > 0.0), 0.0, F32, jnp.sum, jnp.add),
        (lambda s, k0: one_if(s == 0.0), 0.0, F32, jnp.sum, jnp.add)])

    long_row = n_adm > kf
    pos_row = long_row & (cpos >= kf)
    zero_row = long_row & (cpos < kf) & (cpos + czero >= kf)
    neg_row = long_row & (cpos + czero < kf)
    lo0 = jnp.where(pos_row | zero_row, 0.0, vmin - jnp.maximum(1.0, jnp.abs(vmin)))
    cnt_lo0 = jnp.where(pos_row, cpos, jnp.where(zero_row, kf, n_adm))
    hi0 = jnp.where(neg_row, 0.0, vmax)
    cnt_hi0 = jnp.where(neg_row, cpos, 0.0)
    need_zero = jnp.where(zero_row, kf - cpos, 0.0)

    def narrow(lo, hi, cnt_lo, cnt_hi):
        fracs = [(t + 1.0) / (BISECT_PROBES + 1.0) for t in range(BISECT_PROBES)]
        mids = [lo * (1.0 - f) + hi * f for f in fracs]
        for t in range(1, BISECT_PROBES):
            mids[t] = jnp.maximum(mids[t], mids[t - 1])
        codes = [sum(COUNT_RADIX ** u for u in range(t + 1)) for t in range(BISECT_PROBES)]

        def encode(s, k0):
            e = jnp.zeros(s.shape, jnp.int32)
            for mid, code in zip(mids, codes):
                e = jnp.where(s > mid, code, e)
            return e
        (packed,) = reduce_blocks([(encode, 0, jnp.int32, None, jnp.add)])
        cnts = []
        for t in range(BISECT_PROBES):
            digit = packed & (COUNT_RADIX - 1) if t < BISECT_PROBES - 1 else packed
            packed = packed >> COUNT_RADIX.bit_length() - 1
            cnts.append(jnp.sum(digit.astype(F32), axis=key_axis, keepdims=True))
        lo_n, cnt_lo_n, hi_n, cnt_hi_n = lo, cnt_lo, hi, cnt_hi
        for mid, cm in zip(mids, cnts):
            up = cm >= kf
            lo_n = jnp.where(up, mid, lo_n)
            cnt_lo_n = jnp.where(up, cm, cnt_lo_n)
        for mid, cm in zip(mids[::-1], cnts[::-1]):
            dn = cm < kf
            hi_n = jnp.where(dn, mid, hi_n)
            cnt_hi_n = jnp.where(dn, cm, cnt_hi_n)
        return lo_n, hi_n, cnt_lo_n, cnt_hi_n

    def unresolved(cnt_lo):
        return (jnp.max(cnt_lo) > kf).astype(jnp.int32)

    blind = jnp.where(unresolved(cnt_lo0) > 0, BISECT_BLIND_ITERS, 0)
    state = lax.fori_loop(0, blind, lambda i, st: narrow(*st), (lo0, hi0, cnt_lo0, cnt_hi0))

    def cond(st):
        return (st[0] < BISECT_MAX_ITERS) & (st[1] > 0)

    def body(st):
        new = narrow(*st[2:])
        return (st[0] + 1, unresolved(new[2])) + new

    _, _, lo, hi, cnt_lo, cnt_hi = lax.while_loop(
        cond, body, (blind, unresolved(state[2])) + state)

    fix = cnt_lo > kf
    need0 = jnp.where(fix, kf - cnt_hi, 0.0)
    big_idx = 2 ** 30

    def fix_cond(need):
        return jnp.max(need) > 0.0

    def fix_body(need):
        def in_cluster(s):
            return (s > lo) & (s <= hi)
        (mval,) = reduce_blocks([(lambda s, k0: jnp.where(in_cluster(s), s, -jnp.inf), -jnp.inf, F32,
                                  jnp.max, jnp.maximum)])
        (idx,) = reduce_blocks(
            [(lambda s, k0: jnp.where(in_cluster(s) & (s == mval), kpos + k0, big_idx),
              big_idx, jnp.int32, jnp.min, jnp.minimum)])
        active = need > 0.0

        def promote(kb, carry):
            for i in range(kblk // strip):
                s = strip_of(kb, i)
                put_strip(kb, i, jnp.where(active & ((kpos + kb * kblk + i * strip) == idx), jnp.inf, s))
            return carry
        lax.fori_loop(0, nkb, promote, 0)
        return jnp.where(active, need - 1.0, need)

    lax.while_loop(fix_cond, fix_body, need0)

    @pl.when(jnp.max(need_zero) > 0.0)
    def _():
        r = lax.broadcasted_iota(jnp.int32, (kblk, kblk), 0)
        c = lax.broadcasted_iota(jnp.int32, (kblk, kblk), 1)
        tri = ((r >= c) if key_axis == 0 else (r <= c)).astype(BF16)

        def tie_blk(kb, seen):
            s = sc_ref[kb]
            z = (s == 0.0) & zero_row
            zb = one_if(z).astype(BF16)
            if key_axis == 0:
                rank = _dot(tri, zb) + seen
                last = rank[kblk - 1:kblk, :]
            else:
                rank = _dot(zb, tri) + seen
                last = rank[:, kblk - 1:kblk]
            sc_ref[kb] = jnp.where(z & (rank <= need_zero), jnp.inf, s)
            return last
        if even_blocks:
            lax.fori_loop(0, nkb // 2, lambda i, seen: tie_blk(2 * i + 1, tie_blk(2 * i, seen)),
                          jnp.zeros_like(need_zero))
        else:
            lax.fori_loop(0, nkb, tie_blk, jnp.zeros_like(need_zero))

    return jnp.where(fix, hi, lo)


def _dsa_body(qb_ref, qi_ref, miscq_ref, zb_ref, k_ref, vt_ref, misck_ref, o_ref, sc_ref, s_ref, *, n_sel, qblk):
    j = pl.program_id(1)
    nkb = ((j + 1) * qblk + KEY_BLOCK - 1) // KEY_BLOCK
    qi = qi_ref[...]
    q_h = [qi[:, h * IDX_HD:(h + 1) * IDX_HD].astype(BF16) for h in range(IDX_HEADS)]
    w_t = miscq_ref[...].T[MISC_WI:MISC_WI + IDX_HEADS, :]
    qchunk = (j * qblk + lax.broadcasted_iota(jnp.int32, (1, qblk), 1)) // CHUNK
    krow = lax.broadcasted_iota(jnp.int32, (KEY_BLOCK, 1), 0)

    def fold(acc, x, combine):
        for i in range(x.shape[0] // SUBLANES):
            acc = combine(acc, x[SUBLANES * i:SUBLANES * (i + 1)])
        return acc

    def score_blk(kb, carry):
        k0 = pl.multiple_of(kb * KEY_BLOCK, KEY_BLOCK)
        for half in range(KEY_BLOCK // SCORE_STRIP):
            r0 = half * SCORE_STRIP
            ki = misck_ref[pl.ds(k0 + r0, SCORE_STRIP), 0:IDX_HD].astype(BF16)
            acc = jnp.zeros((SCORE_STRIP, qblk), F32)
            for h in range(IDX_HEADS):
                acc = acc + w_t[h:h + 1, :] * jnp.maximum(_dot_nt(ki, q_h[h]), 0.0)
            adm = ((k0 + r0 + krow[:SCORE_STRIP]) // CHUNK) <= qchunk
            sc_ref[kb, r0:r0 + SCORE_STRIP, :] = jnp.where(adm, acc, -jnp.inf)
        return carry

    npairs = (nkb + 1) // 2

    def pair_loop(body, init):
        return lax.fori_loop(0, npairs, lambda i, c: body(2 * i + 1, body(2 * i, c)), init)

    pair_loop(score_blk, 0)

    n_adm = ((qchunk + 1) * CHUNK).astype(F32)
    thr = _topk_threshold(sc_ref, nkb, (KEY_BLOCK, qblk), 0, n_sel, n_adm,
                          even_blocks=qblk % (2 * KEY_BLOCK) == 0)

    qb = qb_ref[...] * (B_HD ** -0.5 * math.log2(math.e))
    qs = jnp.concatenate([jnp.where(_lane_mask(B_WIDTH, h * B_HD, (h + 1) * B_HD), qb, 0.0)
                          for h in range(B_HEADS)], axis=0).astype(BF16)
    wide = B_HEADS * qblk

    def logits_blk(kb, m8):
        k0 = pl.multiple_of(kb * KEY_BLOCK, KEY_BLOCK)
        s = _dot_nt(k_ref[pl.ds(k0, KEY_BLOCK), :].astype(BF16), qs)
        sel = sc_ref[kb] > thr
        s = jnp.where(jnp.concatenate([sel] * B_HEADS, axis=1), s, NEG_BIG)
        s_ref[kb] = s
        return fold(m8, s, jnp.maximum)

    m8 = pair_loop(logits_blk, jnp.full((SUBLANES, wide), NEG_BIG, F32))
    m = jnp.max(m8, axis=0, keepdims=True)

    def pv_blk(kb, carry):
        l8, accs = carry
        p = jnp.exp2(s_ref[kb] - m)
        pb = p.astype(BF16)
        k0 = pl.multiple_of(kb * KEY_BLOCK, KEY_BLOCK)
        vt = vt_ref[:, pl.ds(k0, KEY_BLOCK)].astype(BF16)
        accs = tuple(accs[h] + _dot(vt[h * B_HD:(h + 1) * B_HD, :], pb[:, h * qblk:(h + 1) * qblk])
                     for h in range(B_HEADS))
        return fold(l8, p, jnp.add), accs

    l8, accs = pair_loop(
        pv_blk, (jnp.zeros((SUBLANES, wide), F32), tuple(jnp.zeros((B_HD, qblk), F32) for _ in range(B_HEADS))))
    l = jnp.sum(l8, axis=0, keepdims=True)
    o_t = jnp.concatenate([accs[h] / l[:, h * qblk:(h + 1) * qblk] for h in range(B_HEADS)], axis=0)
    o_ref[...] = (o_t.T * _silu(zb_ref[...])).astype(BF16)


def _dsa_prompt(qb, qi, misc, zb, kb, vt, *, bsz, t_len, n_sel, qblk):
    nq = t_len // qblk
    nkb = t_len // KEY_BLOCK
    assert nkb * (KEY_BLOCK // SEARCH_STRIP) < COUNT_RADIX and nkb % 2 == 0
    qmap = lambda b, j: (b * nq + j, 0)
    kmap = lambda b, j: (b, 0)
    return pl.pallas_call(
        functools.partial(_dsa_body, n_sel=n_sel, qblk=qblk),
        grid=(bsz, nq),
        in_specs=[pl.BlockSpec((qblk, B_WIDTH), qmap),
                  pl.BlockSpec((qblk, IDX_HEADS * IDX_HD), qmap),
                  pl.BlockSpec((qblk, MISC_W), qmap),
                  pl.BlockSpec((qblk, B_WIDTH), qmap),
                  pl.BlockSpec((t_len, B_WIDTH), kmap),
                  pl.BlockSpec((None, B_WIDTH, t_len), lambda b, j: (b, 0, 0)),
                  pl.BlockSpec((t_len, MISC_W), kmap)],
        out_specs=pl.BlockSpec((qblk, B_WIDTH), qmap),
        out_shape=jax.ShapeDtypeStruct((bsz * t_len, B_WIDTH), BF16),
        scratch_shapes=[pltpu.VMEM((nkb, KEY_BLOCK, qblk), F32),
                        pltpu.VMEM((nkb, KEY_BLOCK, B_HEADS * qblk), F32)],
        compiler_params=_cparams(("parallel", "arbitrary")),
        name="dsa_prompt",
    )(qb, qi, misc, zb, kb, vt, misc)


def _dsa_dec_body(qb_ref, qi_ref, misc_ref, zb_ref, kn_ref, vn_ref, ck_ref, cv_ref, cki_ref, o_ref,
                  sc_ref, thr_ref, m_ref, l_ref, acc_ref, *, p_len, tq, n_sel, kstep):
    step = pl.program_id(1)
    nsteps = p_len // kstep
    sub = kstep // KEY_BLOCK
    npast = p_len // KEY_BLOCK
    nkb = npast + 1
    pad_rows = KEY_BLOCK - tq
    rows = B_HEADS * tq

    @pl.when(step == 0)
    def _():
        qi = qi_ref[...]
        misc = misc_ref[...]
        q2 = jnp.concatenate([qi[:, h * IDX_HD:(h + 1) * IDX_HD] for h in range(IDX_HEADS)],
                             axis=0).astype(BF16)
        w_h = [misc[:, MISC_WI + h:MISC_WI + h + 1] for h in range(IDX_HEADS)]

        def head_sum(lg):
            acc = jnp.zeros((tq, lg.shape[1]), F32)
            for h in range(IDX_HEADS):
                acc = acc + w_h[h] * jnp.maximum(lg[h * tq:(h + 1) * tq, :], 0.0)
            return acc

        for i in range(nsteps):
            sc = head_sum(_dot(q2, cki_ref[:, i * kstep:(i + 1) * kstep].astype(BF16)))
            for u in range(sub):
                sc_ref[i * sub + u] = sc[:, u * KEY_BLOCK:(u + 1) * KEY_BLOCK]
        lane = lax.broadcasted_iota(jnp.int32, (1, KEY_BLOCK), 1)
        ki_new = jnp.concatenate([misc[:, 0:IDX_HD], jnp.zeros((pad_rows, IDX_HD), F32)], axis=0)
        sc_ref[npast] = jnp.where(lane < tq, head_sum(_dot_nt(q2, ki_new.astype(BF16))), -jnp.inf)
        n_adm = jnp.full((tq, 1), float(p_len + tq), F32)
        thr_ref[...] = _topk_threshold(sc_ref, nkb, (tq, KEY_BLOCK), 1, n_sel, n_adm)
        m_ref[...] = jnp.full((rows, 1), NEG_BIG, F32)
        l_ref[...] = jnp.zeros((rows, 1), F32)
        acc_ref[...] = jnp.zeros((rows, B_HD), F32)

    qb = qb_ref[...] * (B_HD ** -0.5 * math.log2(math.e))
    q_h = [qb[:, h * B_HD:(h + 1) * B_HD].astype(BF16) for h in range(B_HEADS)]
    thr = thr_ref[...]

    def fold_in(logits_of, pv_of, sel):
        s = jnp.concatenate([jnp.where(sel, logits_of(h), NEG_BIG) for h in range(B_HEADS)], axis=0)
        m_old = m_ref[...]
        m_new = jnp.maximum(m_old, jnp.max(s, axis=-1, keepdims=True))
        alpha = jnp.exp2(m_old - m_new)
        p = jnp.exp2(s - m_new)
        pb = p.astype(BF16)
        pv = jnp.concatenate([pv_of(h, pb[h * tq:(h + 1) * tq, :]) for h in range(B_HEADS)], axis=0)
        l_ref[...] = alpha * l_ref[...] + jnp.sum(p, axis=-1, keepdims=True)
        acc_ref[...] = alpha * acc_ref[...] + pv
        m_ref[...] = m_new

    sel = jnp.concatenate([sc_ref[step * sub + u] > thr for u in range(sub)], axis=1)
    fold_in(lambda h: _dot(q_h[h], ck_ref[h].astype(BF16)),
            lambda h, p: _dot_nt(p, cv_ref[h].astype(BF16)), sel)

    @pl.when(step == nsteps - 1)
    def _():
        zpad = jnp.zeros((pad_rows, B_WIDTH), F32)
        kn = jnp.concatenate([kn_ref[...], zpad], axis=0).astype(BF16)
        vn = jnp.concatenate([vn_ref[...], zpad], axis=0).astype(BF16)
        fold_in(lambda h: _dot_nt(q_h[h], kn[:, h * B_HD:(h + 1) * B_HD]),
                lambda h, p: _dot(p, vn[:, h * B_HD:(h + 1) * B_HD]), sc_ref[npast] > thr)
        res = acc_ref[...] / l_ref[...]
        out = jnp.concatenate([res[h * tq:(h + 1) * tq, :] for h in range(B_HEADS)], axis=1)
        o_ref[...] = (out * _silu(zb_ref[...])).astype(BF16)


def _dsa_decode(qb, qi, misc, zb, kb, vb, ck, cv, cki, *, bsz, t_len, p_len, n_sel):
    kstep = DEC_KEYS_PER_STEP
    assert p_len // KEY_BLOCK + 1 < COUNT_RADIX and p_len % kstep == 0
    rows = B_HEADS * t_len
    qmap = lambda b, s: (b, 0)
    cmap = lambda b, s: (b, 0, 0, s)
    ck, cv = (a.transpose(0, 2, 3, 1) for a in (ck, cv))
    cki = cki.transpose(0, 2, 1)
    return pl.pallas_call(
        functools.partial(_dsa_dec_body, p_len=p_len, tq=t_len, n_sel=n_sel, kstep=kstep),
        grid=(bsz, p_len // kstep),
        in_specs=[pl.BlockSpec((t_len, B_WIDTH), qmap),
                  pl.BlockSpec((t_len, IDX_HEADS * IDX_HD), qmap),
                  pl.BlockSpec((t_len, MISC_W), qmap),
                  pl.BlockSpec((t_len, B_WIDTH), qmap),
                  pl.BlockSpec((t_len, B_WIDTH), qmap),
                  pl.BlockSpec((t_len, B_WIDTH), qmap),
                  pl.BlockSpec((None, B_HEADS, B_HD, kstep), cmap),
                  pl.BlockSpec((None, B_HEADS, B_HD, kstep), cmap),
                  pl.BlockSpec((None, IDX_HD, p_len), lambda b, s: (b, 0, 0))],
        out_specs=pl.BlockSpec((t_len, B_WIDTH), qmap),
        out_shape=jax.ShapeDtypeStruct((bsz * t_len, B_WIDTH), BF16),
        scratch_shapes=[pltpu.VMEM((p_len // KEY_BLOCK + 1, t_len, KEY_BLOCK), F32),
                        pltpu.VMEM((t_len, 1), F32),
                        pltpu.VMEM((rows, 1), F32),
                        pltpu.VMEM((rows, 1), F32),
                        pltpu.VMEM((rows, B_HD), F32)],
        compiler_params=_cparams(("parallel", "arbitrary")),
        name="dsa_decode",
    )(qb, qi, misc, zb, kb, vb, ck, cv, cki)


def _tail_body(x_ref, oa_ref, ob_ref, qm_ref, zm_ref, mk_ref, mv_ref, w_ref, y_ref, *, seqs, t_rows):
    acc = _dot(oa_ref[...].astype(BF16), w_ref[0:A_WIDTH, :])
    acc = acc + _dot(ob_ref[...].astype(BF16), w_ref[A_WIDTH:A_WIDTH + B_WIDTH, :])
    oms = []
    for b in range(seqs):
        rs = slice(b * t_rows, (b + 1) * t_rows)
        qm = qm_ref[rs, :] * (M_HD ** -0.5)
        mk_t = mk_ref[b].astype(BF16)
        mv_t = mv_ref[b].astype(BF16)
        om = jnp.zeros(qm.shape, F32)
        for h in range(M_HEADS):
            mask = _lane_mask(M_WIDTH, h * M_HD, (h + 1) * M_HD)
            s = _dot(jnp.where(mask, qm, 0.0).astype(BF16), mk_t)
            p = jnp.exp(s - jnp.max(s, axis=-1, keepdims=True))
            l = jnp.sum(p, axis=-1, keepdims=True)
            om = om + jnp.where(mask, _dot_nt(p.astype(BF16), mv_t) / l, 0.0)
        oms.append((om * _silu(zm_ref[rs, :])).astype(BF16))
    om_all = oms[0] if seqs == 1 else jnp.concatenate(oms, axis=0)
    acc = acc + _dot(om_all, w_ref[A_WIDTH + B_WIDTH:, :])
    y_ref[...] = x_ref[...] + acc


def _tail(x, oa, ob, qm, zm, mk, mv, w, *, bsz, t_len, tq):
    m, d = x.shape
    if t_len < LANES:
        seqs, nq, grid_b, tq = bsz, 1, 1, bsz * t_len
    else:
        seqs, nq, grid_b = 1, t_len // tq, bsz
    qmap = lambda b, j: (b * nq + j, 0)
    mem_spec = pl.BlockSpec((seqs, M_WIDTH, N_MEM), lambda b, j: (b, 0, 0))
    return pl.pallas_call(
        functools.partial(_tail_body, seqs=seqs, t_rows=tq // seqs),
        grid=(grid_b, nq),
        in_specs=[pl.BlockSpec((tq, d), qmap),
                  pl.BlockSpec((tq, A_WIDTH), qmap),
                  pl.BlockSpec((tq, B_WIDTH), qmap),
                  pl.BlockSpec((tq, M_WIDTH), qmap),
                  pl.BlockSpec((tq, M_WIDTH), qmap),
                  mem_spec, mem_spec,
                  pl.BlockSpec(w.shape, lambda b, j: (0, 0))],
        out_specs=pl.BlockSpec((tq, d), qmap),
        out_shape=jax.ShapeDtypeStruct((m, d), F32),
        compiler_params=_cparams(("parallel", "parallel")),
        name="tail",
    )(x, oa, ob, qm, zm, mk, mv, w)


def _regroup_body(wt_ref, o_ref):
    offs = [int(o) for o in np.concatenate([[0], np.cumsum(IN_SPLITS)])]
    grp = lambda i: wt_ref[offs[i]:offs[i + 1], :]
    (w_qkv, w_za, _, _, w_qb, w_kb, w_vb, w_zb, w_qi, w_ki, w_wi, w_qm, w_zm) = [grp(i) for i in range(13)]
    w_ba_aa = wt_ref[offs[2]:offs[4], :]
    pad = jnp.zeros((MISC_W - MISC_A - A_HEADS, wt_ref.shape[1]), F32)
    wt = jnp.concatenate([w_za, w_qb, w_kb, w_vb, w_zb, w_qi, w_qm, w_zm,
                          w_ki, w_wi, w_ba_aa, pad, w_qkv], axis=0)
    o_ref[...] = wt.T.astype(BF16)


def _regroup_weights(w_in):
    d, n_in = w_in.shape
    n_out = n_in - (IDX_HD + IDX_HEADS + 2 * A_HEADS) + MISC_W
    rows = REGROUP_ROWS
    return pl.pallas_call(
        _regroup_body,
        grid=(d // rows,),
        in_specs=[pl.BlockSpec((n_in, rows), lambda i: (0, i))],
        out_specs=pl.BlockSpec((rows, n_out), lambda i: (i, 0)),
        out_shape=jax.ShapeDtypeStruct((d, n_out), BF16),
        compiler_params=_cparams(("parallel",)),
        name="regroup",
    )(w_in.T)


def _prep_weights(g_in, w_in, conv_w, a_log, dt_bias, g_o, g_qb, g_kb, g_ki, g_qm, w_out):
    w_all = _regroup_weights(w_in)
    groups = [(A_WIDTH, False, ""), (B_WIDTH, True, ""),
              (B_WIDTH, True, "t"), (B_WIDTH, False, "t"), (B_WIDTH, False, ""),
              (IDX_HEADS * IDX_HD, False, ""), (M_WIDTH, True, "bf16"), (M_WIDTH, False, ""),
              (MISC_W, True, "t"),
              (A_WIDTH, False, "conv0"), (A_WIDTH, False, "conv1"), (A_WIDTH, False, "conv2")]
    bd64 = _group_mean_matrix(B_WIDTH, B_HD, B_WIDTH)
    ones256 = jnp.ones((1, B_WIDTH), F32)
    misc_gain = jnp.concatenate([g_ki, jnp.full((IDX_HEADS,), IDX_SCALE, F32),
                                 jnp.ones((MISC_W - MISC_B,), F32)])[None, :]
    misc_nm = (jnp.arange(MISC_W) < IDX_HD).astype(F32)[None, :]
    aux = [bd64, jnp.tile(g_qb, B_HEADS)[None, :], ones256,
           bd64, jnp.tile(g_kb, B_HEADS)[None, :], ones256,
           bd64, jnp.tile(g_qm, M_HEADS)[None, :], ones256,
           _group_mean_matrix(MISC_W, IDX_HD, IDX_HD), misc_gain, misc_nm]
    pad_a = lambda v: jnp.zeros((1, MISC_W), F32).at[0, MISC_A:MISC_A + A_HEADS].set(v)
    return dict(g_in=g_in[None, :], w_all=w_all, groups=groups, aux=aux, conv_w=conv_w,
                avec=pad_a(-jnp.exp(a_log)), dtvec=pad_a(dt_bias), g_o=g_o[None, :],
                w_out=w_out.astype(BF16))


def _layer(x, mem_k, mem_v, conv_buf, s0, past, gdn_chunk, gdn_rows, wp):
    bsz, t_len, d = x.shape
    m = bsz * t_len
    x2 = x.reshape(m, d)
    tm = min(PROJ_ROWS, m)
    heads_last = lambda a_t: a_t.reshape(bsz, B_HEADS, B_HD, t_len).transpose(0, 3, 1, 2)
    cbuf = jnp.concatenate([jnp.zeros((bsz, CARRY_ROWS - (CONV_W - 1), 3 * A_WIDTH), F32), conv_buf], axis=1)
    conv = (cbuf, wp["conv_w"])
    gdn_args = dict(bsz=bsz, t_len=t_len, chunk=gdn_chunk, rows=gdn_rows)
    if past is None:
        (za, qb, kb, vb, zb, qi, qm, zm, misc, qa, ka, va, kt, vt, misc_t, conv_new) = _proj(
            x2, wp["g_in"], wp["w_all"], wp["groups"], wp["aux"], tm, t_len, conv=conv)
        k_new, v_new = heads_last(kt), heads_last(vt)
        ki_new = misc_t[:, :IDX_HD, :].transpose(0, 2, 1)
        oa, s_new = _gdn(qa, ka, va, za, misc, s0, wp["avec"], wp["dtvec"], wp["g_o"], **gdn_args)
    else:
        groups = [(wd, normed, extra if extra == "bf16" else "") for wd, normed, extra in wp["groups"]]
        (za, qb, kb, vb, zb, qi, qm, zm, misc, qa, ka, va) = _proj(x2, wp["g_in"], wp["w_all"], groups,
                                                                   wp["aux"], tm, m)
        k_new, v_new = (a.reshape(bsz, t_len, B_HEADS, B_HD) for a in (kb, vb))
        ki_new = misc[:, :IDX_HD].reshape(bsz, t_len, IDX_HD)
        oa, s_new, conv_new = _gdn(qa, ka, va, za, misc, s0, wp["avec"], wp["dtvec"], wp["g_o"], conv=conv,
                                   **gdn_args)
    if past is None:
        n_sel = min(TOPK_MAX, t_len // 4)
        ob = _dsa_prompt(qb, qi, misc, zb, kb, vt, bsz=bsz, t_len=t_len, n_sel=n_sel, qblk=DSA_QBLK)
    else:
        ck, cv, cki = past
        p_len = ck.shape[1]
        assert (p_len + t_len - 1) // CHUNK <= p_len // CHUNK and p_len % KEY_BLOCK == 0
        n_sel = min(TOPK_MAX, (p_len + t_len) // 4)
        ob = _dsa_decode(qb, qi, misc, zb, kb, vb, ck, cv, cki, bsz=bsz, t_len=t_len, p_len=p_len,
                         n_sel=n_sel)
    y = _tail(x2, oa, ob, qm, zm, mem_k, mem_v, wp["w_out"], bsz=bsz, t_len=t_len, tq=min(TAIL_ROWS, t_len))
    return y.reshape(bsz, t_len, d), conv_new, s_new, k_new, v_new, ki_new


def _memory_kv(mem, g_mem, w_mem_kv, g_km):
    bsz, n_mem, d = mem.shape
    groups = [(M_WIDTH, True, "t"), (M_WIDTH, False, "t")]
    aux = [_group_mean_matrix(M_WIDTH, M_HD, M_WIDTH), jnp.tile(g_km, M_HEADS)[None, :],
           jnp.ones((1, M_WIDTH), F32)]
    _, _, mk_t, mv_t = _proj(mem.reshape(bsz * n_mem, d), g_mem[None, :], w_mem_kv.astype(BF16),
                             groups, aux, n_mem, n_mem)
    heads_last = lambda a_t: a_t.reshape(bsz, M_HEADS, M_HD, n_mem).transpose(0, 3, 1, 2)
    return mk_t, mv_t, heads_last(mk_t), heads_last(mv_t)


def kernel(x_prompt, x_sample, state_conv_A, state_ssm_A, cache_k_B, cache_v_B, cache_kidx_B, cache_mem_k,
           cache_mem_v, mem_prompt, g_in, w_in, conv_w_A, a_log_A, dt_bias_A, g_o_A, g_q_B, g_k_B, g_kidx_B,
           g_mem, w_mem_kv, g_q_M, g_k_M, w_out):
    depth = w_in.shape[0]
    assert depth == 1
    l = 0
    bp, t_p, _ = x_prompt.shape
    bs, t_s, _ = x_sample.shape
    wp = _prep_weights(g_in[l], w_in[l], conv_w_A[l], a_log_A[l], dt_bias_A[l], g_o_A[l], g_q_B[l], g_k_B[l],
                       g_kidx_B[l], g_q_M[l], w_out[l])
    mk, mv, mk_heads, mv_heads = _memory_kv(mem_prompt, g_mem[l], w_mem_kv[l], g_k_M[l])
    feature_major = lambda c: c.transpose(0, 2, 3, 1).reshape(c.shape[0], M_WIDTH, N_MEM)
    zero_conv = jnp.zeros((bp, CONV_W - 1, 3 * A_WIDTH), F32)
    zero_ssm = jnp.zeros((bp, A_HEADS, A_DK, A_DV), F32)
    yp, c1, s1, k1, v1, ki1 = _layer(x_prompt, mk, mv, zero_conv, zero_ssm, None, CHUNK, GDN_CHUNKS_PER_STEP * CHUNK, wp)
    ys, c2, s2, k2, v2, ki2 = _layer(
        x_sample, feature_major(cache_mem_k[l]), feature_major(cache_mem_v[l]),
        state_conv_A[l], state_ssm_A[l], (cache_k_B[l], cache_v_B[l], cache_kidx_B[l]), t_s, t_s, wp)
    st = lambda a: a[None]
    return (yp, ys, st(c1), st(s1), st(k1), st(v1), st(ki1),
            st(mk_heads), st(mv_heads),
            st(c2), st(s2), st(k2), st(v2), st(ki2))
```
